```python
import math
import jax, jax.numpy as jnp
from jax import lax
import numpy as np

D_MODEL = 2048
BATCH = 8
SEQ = 4096
DEPTH = 2

D_MIX = D_MODEL
HEAD_DIM = 64
N_Q_HEADS = 16
N_KV_HEADS = 4
GQ = N_Q_HEADS // N_KV_HEADS
WINDOW = 128
ATTN_BLOCK = 128
ROT_DIM = HEAD_DIM // 4
ROPE_THETA = 500000.0
CONV_GROUPS = 8
CONV_CH = CONV_GROUPS * HEAD_DIM
CONV_WIDTH = 31
SGU_HEADS = 8
SGU_CH = SGU_HEADS * HEAD_DIM
SGU_CHUNK = 128
D_FF = 5632
FFN_RESIDUAL_WEIGHT = 0.5
NORM_EPS = 1e-5

Q_END = N_Q_HEADS * HEAD_DIM
K_END = Q_END + N_KV_HEADS * HEAD_DIM
V_END = K_END + N_KV_HEADS * HEAD_DIM
CONV_END = V_END + 2 * CONV_CH
IN_COLS = CONV_END + 2 * SGU_CH

kernel_name = 'hybrid_parallel_conv_sgu_swa_block'


def rms_norm(x, g):
    xf = x.astype(jnp.float32)
    y = xf * lax.rsqrt(jnp.mean(xf * xf, axis=-1, keepdims=True) + NORM_EPS)
    return (y * g.astype(jnp.float32)).astype(x.dtype)


def layer_norm(x, g, b):
    xf = x.astype(jnp.float32)
    mu = jnp.mean(xf, axis=-1, keepdims=True)
    xc = xf - mu
    y = xc * lax.rsqrt(jnp.mean(xc * xc, axis=-1, keepdims=True) + NORM_EPS)
    return (y * g.astype(jnp.float32) + b.astype(jnp.float32)).astype(x.dtype)


def swiglu(h, w_in, w_out):
    gu = h @ w_in
    return (jax.nn.silu(gu[..., :D_FF]) * gu[..., D_FF:]) @ w_out


def rope_tables(positions):
    inv_freq = 1.0 / (ROPE_THETA ** (jnp.arange(0, ROT_DIM, 2, dtype=jnp.float32) / ROT_DIM))
    ang = positions.astype(jnp.float32)[..., None] * inv_freq
    return jnp.cos(ang), jnp.sin(ang)


def apply_partial_rope(t, cos, sin):
    half = ROT_DIM // 2
    t1 = t[..., :half].astype(jnp.float32)
    t2 = t[..., half:ROT_DIM].astype(jnp.float32)
    c = cos[:, :, None, :]
    s = sin[:, :, None, :]
    rot = jnp.concatenate([t1 * c - t2 * s, t2 * c + t1 * s], axis=-1).astype(t.dtype)
    return jnp.concatenate([rot, t[..., ROT_DIM:]], axis=-1)


def sliding_window_attention(q, k, v, sinks):
    B, S = q.shape[0], q.shape[1]
    nb = S // ATTN_BLOCK
    qb = q.reshape(B, nb, ATTN_BLOCK, N_KV_HEADS, GQ, HEAD_DIM)

    def with_prev(t):
        tb = t.reshape(B, nb, ATTN_BLOCK, N_KV_HEADS, HEAD_DIM)
        prev = jnp.concatenate([jnp.zeros_like(tb[:, :1]), tb[:, :-1]], axis=1)
        return jnp.concatenate([prev, tb], axis=2)

    kk, vv = with_prev(k), with_prev(v)
    scores = jnp.einsum('bnqhgd,bnkhd->bnhgqk', qb, kk).astype(jnp.float32) * (HEAD_DIM ** -0.5)
    qi = jnp.arange(ATTN_BLOCK)[:, None]
    kj = jnp.arange(2 * ATTN_BLOCK)[None, :]
    dist = qi + ATTN_BLOCK - kj
    band = (dist >= 0) & (dist < WINDOW)
    kpos = jnp.arange(nb)[:, None, None] * ATTN_BLOCK + kj[None] - ATTN_BLOCK
    mask = band[None] & (kpos >= 0)
    scores = jnp.where(mask[None, :, None, None], scores, jnp.float32(-1e30))
    s = sinks.astype(jnp.float32).reshape(N_KV_HEADS, GQ)[None, None, :, :, None, None]
    m = jnp.maximum(jnp.max(scores, axis=-1, keepdims=True), s)
    p = jnp.exp(scores - m)
    p = p / (jnp.sum(p, axis=-1, keepdims=True) + jnp.exp(s - m))
    out = jnp.einsum('bnhgqk,bnkhd->bnqhgd', p.astype(v.dtype), vv)
    return out.reshape(B, S, N_Q_HEADS * HEAD_DIM)


def conv_module(a, dw_w, dw_b, ln_g, ln_b):
    h = a[..., :CONV_CH] * jax.nn.sigmoid(a[..., CONV_CH:])
    hp = jnp.pad(h, ((0, 0), (CONV_WIDTH - 1, 0), (0, 0)))
    y = lax.conv_general_dilated(hp, dw_w[:, None, :].astype(h.dtype), window_strides=(1,),
                                 padding='VALID', dimension_numbers=('NWC', 'WIO', 'NWC'),
                                 feature_group_count=CONV_CH) + dw_b
    return jax.nn.silu(layer_norm(y, ln_g, ln_b))


def spatial_gating(a, ln_g, ln_b, w_s, b_s):
    B, S = a.shape[0], a.shape[1]
    u = a[..., :SGU_CH]
    v = layer_norm(a[..., SGU_CH:], ln_g, ln_b)
    vb = v.reshape(B, S // SGU_CHUNK, SGU_CHUNK, SGU_HEADS, HEAD_DIM)
    causal = jnp.tril(jnp.ones((SGU_CHUNK, SGU_CHUNK), dtype=bool))
    ws = jnp.where(causal[None], w_s, jnp.zeros_like(w_s))
    mixed = jnp.einsum('hts,bnshd->bnthd', ws, vb) + b_s.T[None, None, :, :, None]
    return u * mixed.reshape(B, S, SGU_CH)


def _fwd_setup_inputs(seed: int = 0) -> dict:
    key = jax.random.key(seed)
    ks = jax.random.split(key, 24)
    L, D, F = DEPTH, D_MODEL, D_FF
    nrm = lambda k, shape, scale: jax.random.normal(k, shape, jnp.float32) * scale
    gain = lambda k, shape: 1.0 + 0.05 * jax.random.normal(k, shape, jnp.float32)
    x = jax.random.normal(ks[0], (BATCH, SEQ, D), jnp.float32)
    offs = jax.random.randint(ks[1], (BATCH, 1), 0, 1024, dtype=jnp.int32)
    positions = (jnp.arange(SEQ, dtype=jnp.int32)[None, :] + offs).astype(jnp.int32)
    return {
        'x': x,
        'positions': positions,
        'norm_ffn1': gain(ks[2], (L, D)),
        'ffn1_w_in': nrm(ks[3], (L, D, 2 * F), D ** -0.5),
        'ffn1_w_out': nrm(ks[4], (L, F, D), F ** -0.5),
        'norm_mix': gain(ks[5], (L, D)),
        'w_in': nrm(ks[6], (L, D, IN_COLS), D ** -0.5),
        'conv_dw_w': nrm(ks[7], (L, CONV_WIDTH, CONV_CH), CONV_WIDTH ** -0.5),
        'conv_dw_b': nrm(ks[8], (L, CONV_CH), 0.02),
        'conv_ln_g': gain(ks[9], (L, CONV_CH)),
        'conv_ln_b': nrm(ks[10], (L, CONV_CH), 0.02),
        'sgu_ln_g': gain(ks[11], (L, SGU_CH)),
        'sgu_ln_b': nrm(ks[12], (L, SGU_CH), 0.02),
        'sgu_w': nrm(ks[13], (L, SGU_HEADS, SGU_CHUNK, SGU_CHUNK), SGU_CHUNK ** -0.5),
        'sgu_b': 1.0 + 0.1 * jax.random.normal(ks[14], (L, SGU_HEADS, SGU_CHUNK), jnp.float32),
        'attn_sinks': nrm(ks[15], (L, N_Q_HEADS), 0.5),
        'w_out': nrm(ks[16], (L, D_MIX, D), D_MIX ** -0.5),
        'norm_ffn2': gain(ks[17], (L, D)),
        'ffn2_w_in': nrm(ks[18], (L, D, 2 * F), D ** -0.5),
        'ffn2_w_out': nrm(ks[19], (L, F, D), F ** -0.5),
        'final_norm': gain(ks[20], (D,)),
    }


def _fwd_reference(x, positions, norm_ffn1, ffn1_w_in, ffn1_w_out, norm_mix, w_in, conv_dw_w, conv_dw_b,
              conv_ln_g, conv_ln_b, sgu_ln_g, sgu_ln_b, sgu_w, sgu_b, attn_sinks, w_out,
              norm_ffn2, ffn2_w_in, ffn2_w_out, final_norm):
    B, S = x.shape[0], x.shape[1]
    cos, sin = rope_tables(positions)
    for l in range(DEPTH):
        h = rms_norm(x, norm_ffn1[l])
        x = x + FFN_RESIDUAL_WEIGHT * swiglu(h, ffn1_w_in[l], ffn1_w_out[l])
        h = rms_norm(x, norm_mix[l])
        p = h @ w_in[l]
        q = apply_partial_rope(p[..., :Q_END].reshape(B, S, N_Q_HEADS, HEAD_DIM), cos, sin)
        k = apply_partial_rope(p[..., Q_END:K_END].reshape(B, S, N_KV_HEADS, HEAD_DIM), cos, sin)
        v = p[..., K_END:V_END].reshape(B, S, N_KV_HEADS, HEAD_DIM)
        attn = sliding_window_attention(q, k, v, attn_sinks[l])
        conv = conv_module(p[..., V_END:CONV_END], conv_dw_w[l], conv_dw_b[l],
                           conv_ln_g[l], conv_ln_b[l])
        sgu = spatial_gating(p[..., CONV_END:], sgu_ln_g[l], sgu_ln_b[l],
                             sgu_w[l], sgu_b[l])
        x = x + jnp.concatenate([attn, conv, sgu], axis=-1) @ w_out[l]
        h = rms_norm(x, norm_ffn2[l])
        x = x + FFN_RESIDUAL_WEIGHT * swiglu(h, ffn2_w_in[l], ffn2_w_out[l])
    return rms_norm(x, final_norm)


import jax as _jax
import jax.numpy as _jnp

TWIN_FORMAT = 'train_step'
FWD_PARAMS = ['x', 'positions', 'norm_ffn1', 'ffn1_w_in', 'ffn1_w_out', 'norm_mix', 'w_in', 'conv_dw_w', 'conv_dw_b', 'conv_ln_g', 'conv_ln_b', 'sgu_ln_g', 'sgu_ln_b', 'sgu_w', 'sgu_b', 'attn_sinks', 'w_out', 'norm_ffn2', 'ffn2_w_in', 'ffn2_w_out', 'final_norm']
TWIN_WEIGHTS = ['norm_ffn1', 'ffn1_w_in', 'ffn1_w_out', 'norm_mix', 'w_in', 'conv_dw_w', 'conv_dw_b', 'conv_ln_g', 'conv_ln_b', 'sgu_ln_g', 'sgu_ln_b', 'sgu_w', 'sgu_b', 'attn_sinks', 'w_out', 'norm_ffn2', 'ffn2_w_in', 'ffn2_w_out', 'final_norm']
TWIN_DIFF_INPUT = 'x'
TWIN_INPUTS = ['x', 'positions', 'norm_ffn1', 'ffn1_w_in', 'ffn1_w_out', 'norm_mix', 'w_in', 'conv_dw_w', 'conv_dw_b', 'conv_ln_g', 'conv_ln_b', 'sgu_ln_g', 'sgu_ln_b', 'sgu_w', 'sgu_b', 'attn_sinks', 'w_out', 'norm_ffn2', 'ffn2_w_in', 'ffn2_w_out', 'final_norm', 'loss_target', 'm_norm_ffn1', 'm_ffn1_w_in', 'm_ffn1_w_out', 'm_norm_mix', 'm_w_in', 'm_conv_dw_w', 'm_conv_dw_b', 'm_conv_ln_g', 'm_conv_ln_b', 'm_sgu_ln_g', 'm_sgu_ln_b', 'm_sgu_w', 'm_sgu_b', 'm_attn_sinks', 'm_w_out', 'm_norm_ffn2', 'm_ffn2_w_in', 'm_ffn2_w_out', 'm_final_norm', 'v_norm_ffn1', 'v_ffn1_w_in', 'v_ffn1_w_out', 'v_norm_mix', 'v_w_in', 'v_conv_dw_w', 'v_conv_dw_b', 'v_conv_ln_g', 'v_conv_ln_b', 'v_sgu_ln_g', 'v_sgu_ln_b', 'v_sgu_w', 'v_sgu_b', 'v_attn_sinks', 'v_w_out', 'v_norm_ffn2', 'v_ffn2_w_in', 'v_ffn2_w_out', 'v_final_norm']
TWIN_OUTPUTS = ['loss', 'grad_x', 'grad_norm_ffn1', 'grad_ffn1_w_in', 'grad_ffn1_w_out', 'grad_norm_mix', 'grad_w_in', 'grad_conv_dw_w', 'grad_conv_dw_b', 'grad_conv_ln_g', 'grad_conv_ln_b', 'grad_sgu_ln_g', 'grad_sgu_ln_b', 'grad_sgu_w', 'grad_sgu_b', 'grad_attn_sinks', 'grad_w_out', 'grad_norm_ffn2', 'grad_ffn2_w_in', 'grad_ffn2_w_out', 'grad_final_norm', 'delta_norm_ffn1', 'delta_ffn1_w_in', 'delta_ffn1_w_out', 'delta_norm_mix', 'delta_w_in', 'delta_conv_dw_w', 'delta_conv_dw_b', 'delta_conv_ln_g', 'delta_conv_ln_b', 'delta_sgu_ln_g', 'delta_sgu_ln_b', 'delta_sgu_w', 'delta_sgu_b', 'delta_attn_sinks', 'delta_w_out', 'delta_norm_ffn2', 'delta_ffn2_w_in', 'delta_ffn2_w_out', 'delta_final_norm', 'new_m_norm_ffn1', 'new_m_ffn1_w_in', 'new_m_ffn1_w_out', 'new_m_norm_mix', 'new_m_w_in', 'new_m_conv_dw_w', 'new_m_conv_dw_b', 'new_m_conv_ln_g', 'new_m_conv_ln_b', 'new_m_sgu_ln_g', 'new_m_sgu_ln_b', 'new_m_sgu_w', 'new_m_sgu_b', 'new_m_attn_sinks', 'new_m_w_out', 'new_m_norm_ffn2', 'new_m_ffn2_w_in', 'new_m_ffn2_w_out', 'new_m_final_norm', 'new_v_norm_ffn1', 'new_v_ffn1_w_in', 'new_v_ffn1_w_out', 'new_v_norm_mix', 'new_v_w_in', 'new_v_conv_dw_w', 'new_v_conv_dw_b', 'new_v_conv_ln_g', 'new_v_conv_ln_b', 'new_v_sgu_ln_g', 'new_v_sgu_ln_b', 'new_v_sgu_w', 'new_v_sgu_b', 'new_v_attn_sinks', 'new_v_w_out', 'new_v_norm_ffn2', 'new_v_ffn2_w_in', 'new_v_ffn2_w_out', 'new_v_final_norm']
TWIN_LEAF_KINDS = {'loss': 'loss', 'grad_x': 'grad_x', 'grad_norm_ffn1': 'grad_w', 'grad_ffn1_w_in': 'grad_w', 'grad_ffn1_w_out': 'grad_w', 'grad_norm_mix': 'grad_w', 'grad_w_in': 'grad_w', 'grad_conv_dw_w': 'grad_w', 'grad_conv_dw_b': 'grad_w', 'grad_conv_ln_g': 'grad_w', 'grad_conv_ln_b': 'grad_w', 'grad_sgu_ln_g': 'grad_w', 'grad_sgu_ln_b': 'grad_w', 'grad_sgu_w': 'grad_w', 'grad_sgu_b': 'grad_w', 'grad_attn_sinks': 'grad_w', 'grad_w_out': 'grad_w', 'grad_norm_ffn2': 'grad_w', 'grad_ffn2_w_in': 'grad_w', 'grad_ffn2_w_out': 'grad_w', 'grad_final_norm': 'grad_w', 'delta_norm_ffn1': 'delta_w', 'delta_ffn1_w_in': 'delta_w', 'delta_ffn1_w_out': 'delta_w', 'delta_norm_mix': 'delta_w', 'delta_w_in': 'delta_w', 'delta_conv_dw_w': 'delta_w', 'delta_conv_dw_b': 'delta_w', 'delta_conv_ln_g': 'delta_w', 'delta_conv_ln_b': 'delta_w', 'delta_sgu_ln_g': 'delta_w', 'delta_sgu_ln_b': 'delta_w', 'delta_sgu_w': 'delta_w', 'delta_sgu_b': 'delta_w', 'delta_attn_sinks': 'delta_w', 'delta_w_out': 'delta_w', 'delta_norm_ffn2': 'delta_w', 'delta_ffn2_w_in': 'delta_w', 'delta_ffn2_w_out': 'delta_w', 'delta_final_norm': 'delta_w', 'new_m_norm_ffn1': 'new_m', 'new_m_ffn1_w_in': 'new_m', 'new_m_ffn1_w_out': 'new_m', 'new_m_norm_mix': 'new_m', 'new_m_w_in': 'new_m', 'new_m_conv_dw_w': 'new_m', 'new_m_conv_dw_b': 'new_m', 'new_m_conv_ln_g': 'new_m', 'new_m_conv_ln_b': 'new_m', 'new_m_sgu_ln_g': 'new_m', 'new_m_sgu_ln_b': 'new_m', 'new_m_sgu_w': 'new_m', 'new_m_sgu_b': 'new_m', 'new_m_attn_sinks': 'new_m', 'new_m_w_out': 'new_m', 'new_m_norm_ffn2': 'new_m', 'new_m_ffn2_w_in': 'new_m', 'new_m_ffn2_w_out': 'new_m', 'new_m_final_norm': 'new_m', 'new_v_norm_ffn1': 'new_v', 'new_v_ffn1_w_in': 'new_v', 'new_v_ffn1_w_out': 'new_v', 'new_v_norm_mix': 'new_v', 'new_v_w_in': 'new_v', 'new_v_conv_dw_w': 'new_v', 'new_v_conv_dw_b': 'new_v', 'new_v_conv_ln_g': 'new_v', 'new_v_conv_ln_b': 'new_v', 'new_v_sgu_ln_g': 'new_v', 'new_v_sgu_ln_b': 'new_v', 'new_v_sgu_w': 'new_v', 'new_v_sgu_b': 'new_v', 'new_v_attn_sinks': 'new_v', 'new_v_w_out': 'new_v', 'new_v_norm_ffn2': 'new_v', 'new_v_ffn2_w_in': 'new_v', 'new_v_ffn2_w_out': 'new_v', 'new_v_final_norm': 'new_v'}


def _forward(args):
    return _fwd_reference(*[args[k] for k in FWD_PARAMS])


def _output_shape():
    def fwd():
        inp = _fwd_setup_inputs(0)
        return _fwd_reference(*[inp[k] for k in FWD_PARAMS])
    out = _jax.eval_shape(fwd)
    return out.shape, out.dtype

N_MICROBATCH = 1
ADAM_LR = 0.001
ADAM_B1 = 0.9
ADAM_B2 = 0.999
ADAM_EPS = 1e-08
ADAM_WD = 0.01
ADAM_STEP = 10
PER_EXAMPLE_BATCH_AXIS = {'x': 0, 'positions': 0, 'loss_target': 0}
SHARED_INPUTS = []
_WEIGHT_DTYPES = {'norm_ffn1': _jnp.float32, 'ffn1_w_in': _jnp.float32, 'ffn1_w_out': _jnp.float32, 'norm_mix': _jnp.float32, 'w_in': _jnp.float32, 'conv_dw_w': _jnp.float32, 'conv_dw_b': _jnp.float32, 'conv_ln_g': _jnp.float32, 'conv_ln_b': _jnp.float32, 'sgu_ln_g': _jnp.float32, 'sgu_ln_b': _jnp.float32, 'sgu_w': _jnp.float32, 'sgu_b': _jnp.float32, 'attn_sinks': _jnp.float32, 'w_out': _jnp.float32, 'norm_ffn2': _jnp.float32, 'ffn2_w_in': _jnp.float32, 'ffn2_w_out': _jnp.float32, 'final_norm': _jnp.float32}
MOMENT_SCALE = {'norm_ffn1': 3.736664e-02, 'ffn1_w_in': 1.600732e-02, 'ffn1_w_out': 2.612157e-02, 'norm_mix': 5.744778e-02, 'w_in': 4.360813e-02, 'conv_dw_w': 4.507880e-02, 'conv_dw_b': 1.210092e-01, 'conv_ln_g': 6.246085e-02, 'conv_ln_b': 7.302210e-02, 'sgu_ln_g': 5.045475e-02, 'sgu_ln_b': 4.887614e-02, 'sgu_w': 3.490218e-02, 'sgu_b': 5.069823e-02, 'attn_sinks': 8.753529e-03, 'w_out': 5.111780e-02, 'norm_ffn2': 2.925672e-02, 'ffn2_w_in': 1.241118e-02, 'ffn2_w_out': 2.030011e-02, 'final_norm': 1.605644e+01}


def _to_microbatches(a, axis):
    t = _jnp.moveaxis(a, axis, 0)
    t = t.reshape((N_MICROBATCH, t.shape[0] // N_MICROBATCH) + t.shape[1:])
    return _jnp.moveaxis(t, 1, axis + 1)


def setup_inputs(seed: int = 0) -> dict:
    inp = _fwd_setup_inputs(seed)
    key = _jax.random.fold_in(_jax.random.key(seed), 7919)
    shape, _ = _output_shape()
    out = dict(inp)
    out["loss_target"] = _jax.random.normal(_jax.random.fold_in(key, 0), shape, _jnp.float32)
    for i, name in enumerate(TWIN_WEIGHTS):
        w = inp[name].astype(_jnp.float32)
        if MOMENT_SCALE is None:
            s = _jnp.sqrt(_jnp.mean(_jnp.square(w)) + 1e-30)
        else:
            s = MOMENT_SCALE[name]
        km, kv = _jax.random.split(_jax.random.fold_in(key, i + 1))
        out[name] = w
        out["m_" + name] = s * _jax.random.normal(km, w.shape, _jnp.float32)
        out["v_" + name] = (s * s) * _jax.random.uniform(kv, w.shape, _jnp.float32, 0.5, 1.5)
    if N_MICROBATCH > 1:
        for name, axis in PER_EXAMPLE_BATCH_AXIS.items():
            out[name] = _to_microbatches(out[name], axis)
    return {'x': out['x'], 'positions': out['positions'], 'norm_ffn1': out['norm_ffn1'], 'ffn1_w_in': out['ffn1_w_in'], 'ffn1_w_out': out['ffn1_w_out'], 'norm_mix': out['norm_mix'], 'w_in': out['w_in'], 'conv_dw_w': out['conv_dw_w'], 'conv_dw_b': out['conv_dw_b'], 'conv_ln_g': out['conv_ln_g'], 'conv_ln_b': out['conv_ln_b'], 'sgu_ln_g': out['sgu_ln_g'], 'sgu_ln_b': out['sgu_ln_b'], 'sgu_w': out['sgu_w'], 'sgu_b': out['sgu_b'], 'attn_sinks': out['attn_sinks'], 'w_out': out['w_out'], 'norm_ffn2': out['norm_ffn2'], 'ffn2_w_in': out['ffn2_w_in'], 'ffn2_w_out': out['ffn2_w_out'], 'final_norm': out['final_norm'], 'loss_target': out['loss_target'], 'm_norm_ffn1': out['m_norm_ffn1'], 'm_ffn1_w_in': out['m_ffn1_w_in'], 'm_ffn1_w_out': out['m_ffn1_w_out'], 'm_norm_mix': out['m_norm_mix'], 'm_w_in': out['m_w_in'], 'm_conv_dw_w': out['m_conv_dw_w'], 'm_conv_dw_b': out['m_conv_dw_b'], 'm_conv_ln_g': out['m_conv_ln_g'], 'm_conv_ln_b': out['m_conv_ln_b'], 'm_sgu_ln_g': out['m_sgu_ln_g'], 'm_sgu_ln_b': out['m_sgu_ln_b'], 'm_sgu_w': out['m_sgu_w'], 'm_sgu_b': out['m_sgu_b'], 'm_attn_sinks': out['m_attn_sinks'], 'm_w_out': out['m_w_out'], 'm_norm_ffn2': out['m_norm_ffn2'], 'm_ffn2_w_in': out['m_ffn2_w_in'], 'm_ffn2_w_out': out['m_ffn2_w_out'], 'm_final_norm': out['m_final_norm'], 'v_norm_ffn1': out['v_norm_ffn1'], 'v_ffn1_w_in': out['v_ffn1_w_in'], 'v_ffn1_w_out': out['v_ffn1_w_out'], 'v_norm_mix': out['v_norm_mix'], 'v_w_in': out['v_w_in'], 'v_conv_dw_w': out['v_conv_dw_w'], 'v_conv_dw_b': out['v_conv_dw_b'], 'v_conv_ln_g': out['v_conv_ln_g'], 'v_conv_ln_b': out['v_conv_ln_b'], 'v_sgu_ln_g': out['v_sgu_ln_g'], 'v_sgu_ln_b': out['v_sgu_ln_b'], 'v_sgu_w': out['v_sgu_w'], 'v_sgu_b': out['v_sgu_b'], 'v_attn_sinks': out['v_attn_sinks'], 'v_w_out': out['v_w_out'], 'v_norm_ffn2': out['v_norm_ffn2'], 'v_ffn2_w_in': out['v_ffn2_w_in'], 'v_ffn2_w_out': out['v_ffn2_w_out'], 'v_final_norm': out['v_final_norm']}


def _loss(weights, diff, rest, loss_target):
    with _jax.named_scope("forward"):
        args = {**rest, TWIN_DIFF_INPUT: diff, **{k: w.astype(_WEIGHT_DTYPES[k]) for k, w in weights.items()}}
        y = _forward(args)
    with _jax.named_scope("loss_head"):
        err = _jnp.square(y.astype(_jnp.float32) - loss_target)
        return 0.5 * _jnp.sum(_jnp.mean(err, axis=-1)) if err.ndim else 0.5 * err


def _adamw(w, g, m, v):
    m = ADAM_B1 * m + (1.0 - ADAM_B1) * g
    v = ADAM_B2 * v + (1.0 - ADAM_B2) * _jnp.square(g)
    m_hat = m / (1.0 - ADAM_B1 ** ADAM_STEP)
    v_hat = v / (1.0 - ADAM_B2 ** ADAM_STEP)
    delta = -ADAM_LR * (m_hat / (_jnp.sqrt(v_hat) + ADAM_EPS) + ADAM_WD * w)
    return delta, m, v


def reference(x, positions, norm_ffn1, ffn1_w_in, ffn1_w_out, norm_mix, w_in, conv_dw_w, conv_dw_b, conv_ln_g, conv_ln_b, sgu_ln_g, sgu_ln_b, sgu_w, sgu_b, attn_sinks, w_out, norm_ffn2, ffn2_w_in, ffn2_w_out, final_norm, loss_target, m_norm_ffn1, m_ffn1_w_in, m_ffn1_w_out, m_norm_mix, m_w_in, m_conv_dw_w, m_conv_dw_b, m_conv_ln_g, m_conv_ln_b, m_sgu_ln_g, m_sgu_ln_b, m_sgu_w, m_sgu_b, m_attn_sinks, m_w_out, m_norm_ffn2, m_ffn2_w_in, m_ffn2_w_out, m_final_norm, v_norm_ffn1, v_ffn1_w_in, v_ffn1_w_out, v_norm_mix, v_w_in, v_conv_dw_w, v_conv_dw_b, v_conv_ln_g, v_conv_ln_b, v_sgu_ln_g, v_sgu_ln_b, v_sgu_w, v_sgu_b, v_attn_sinks, v_w_out, v_norm_ffn2, v_ffn2_w_in, v_ffn2_w_out, v_final_norm):
    given = dict(x=x, positions=positions, norm_ffn1=norm_ffn1, ffn1_w_in=ffn1_w_in, ffn1_w_out=ffn1_w_out, norm_mix=norm_mix, w_in=w_in, conv_dw_w=conv_dw_w, conv_dw_b=conv_dw_b, conv_ln_g=conv_ln_g, conv_ln_b=conv_ln_b, sgu_ln_g=sgu_ln_g, sgu_ln_b=sgu_ln_b, sgu_w=sgu_w, sgu_b=sgu_b, attn_sinks=attn_sinks, w_out=w_out, norm_ffn2=norm_ffn2, ffn2_w_in=ffn2_w_in, ffn2_w_out=ffn2_w_out, final_norm=final_norm, loss_target=loss_target, m_norm_ffn1=m_norm_ffn1, m_ffn1_w_in=m_ffn1_w_in, m_ffn1_w_out=m_ffn1_w_out, m_norm_mix=m_norm_mix, m_w_in=m_w_in, m_conv_dw_w=m_conv_dw_w, m_conv_dw_b=m_conv_dw_b, m_conv_ln_g=m_conv_ln_g, m_conv_ln_b=m_conv_ln_b, m_sgu_ln_g=m_sgu_ln_g, m_sgu_ln_b=m_sgu_ln_b, m_sgu_w=m_sgu_w, m_sgu_b=m_sgu_b, m_attn_sinks=m_attn_sinks, m_w_out=m_w_out, m_norm_ffn2=m_norm_ffn2, m_ffn2_w_in=m_ffn2_w_in, m_ffn2_w_out=m_ffn2_w_out, m_final_norm=m_final_norm, v_norm_ffn1=v_norm_ffn1, v_ffn1_w_in=v_ffn1_w_in, v_ffn1_w_out=v_ffn1_w_out, v_norm_mix=v_norm_mix, v_w_in=v_w_in, v_conv_dw_w=v_conv_dw_w, v_conv_dw_b=v_conv_dw_b, v_conv_ln_g=v_conv_ln_g, v_conv_ln_b=v_conv_ln_b, v_sgu_ln_g=v_sgu_ln_g, v_sgu_ln_b=v_sgu_ln_b, v_sgu_w=v_sgu_w, v_sgu_b=v_sgu_b, v_attn_sinks=v_attn_sinks, v_w_out=v_w_out, v_norm_ffn2=v_norm_ffn2, v_ffn2_w_in=v_ffn2_w_in, v_ffn2_w_out=v_ffn2_w_out, v_final_norm=v_final_norm)
    weights = {n: given[n] for n in TWIN_WEIGHTS}
    shared = {n: given[n] for n in SHARED_INPUTS}
    per_example = {n: given[n] for n in ['x', 'positions']}
    grad_fn = _jax.value_and_grad(_loss, argnums=(0, 1))

    def one_microbatch(ex, loss_target):
        ex = dict(ex)
        diff = ex.pop(TWIN_DIFF_INPUT)
        return grad_fn(weights, diff, {**shared, **ex}, loss_target)

    if N_MICROBATCH == 1:
        loss, (grad_w, grad_x) = one_microbatch(per_example, given["loss_target"])
    else:
        def body(carry, xs):
            loss_sum, grad_sum = carry
            l_k, (gw_k, gx_k) = one_microbatch(xs[0], xs[1])
            with _jax.named_scope("update"):
                return (loss_sum + l_k, _jax.tree.map(_jnp.add, grad_sum, gw_k)), gx_k

        init = (_jnp.zeros((), _jnp.float32), _jax.tree.map(_jnp.zeros_like, weights))
        (loss, grad_w), grad_x = _jax.lax.scan(body, init, (per_example, given["loss_target"]))
    with _jax.named_scope("update"):
        delta_w, new_m, new_v = {}, {}, {}
        for n in TWIN_WEIGHTS:
            delta_w[n], new_m[n], new_v[n] = _adamw(weights[n], grad_w[n], given["m_" + n], given["v_" + n])
    return (loss, grad_x, *[grad_w[n] for n in TWIN_WEIGHTS], *[delta_w[n] for n in TWIN_WEIGHTS],
            *[new_m[n] for n in TWIN_WEIGHTS], *[new_v[n] for n in TWIN_WEIGHTS])
```

```python
import functools

import jax
import jax.numpy as jnp
from jax import lax
from jax.experimental import pallas as pl
from jax.experimental.pallas import tpu as pltpu

F32 = jnp.float32
BF16 = jnp.bfloat16
MESH_ID = pl.DeviceIdType.MESH

V7X_VMEM_LIMIT_BYTES = 56 * 2**20
LANES = 128
SUBLANES = 8

HEAD_DIM = 64
N_Q_HEADS = 16
N_KV_HEADS = 4
GQ = N_Q_HEADS // N_KV_HEADS
BLK = 128
ROT_HALF = 8
ROPE_THETA = 500000.0
CONV_WIDTH = 31
CONV_PAD = 32
CONV_CH = 512
SGU_CH = 512
SGU_HEADS = 8
Q_END = N_Q_HEADS * HEAD_DIM
K_END = Q_END + N_KV_HEADS * HEAD_DIM
V_END = K_END + N_KV_HEADS * HEAD_DIM
CONV_END = V_END + 2 * CONV_CH
IN_COLS = CONV_END + 2 * SGU_CH
NORM_EPS = 1e-5
FFN_RESIDUAL_WEIGHT = 0.5
N_CHIPS = 4
N_DEV = 8

ADAM_LR = 0.001
ADAM_B1 = 0.9
ADAM_B2 = 0.999
ADAM_EPS = 1e-08
ADAM_WD = 0.01
ADAM_STEP = 10

NN = (((1,), (0,)), ((), ()))
NT = (((1,), (1,)), ((), ()))
TN = (((0,), (0,)), ((), ()))


def _pick(n, cands):
    for c in cands:
        if n % c == 0:
            return c
    raise ValueError(f"no tile of {cands} divides {n}")


def _params(n_axes):
    return pltpu.CompilerParams(dimension_semantics=("arbitrary",) * n_axes, vmem_limit_bytes=V7X_VMEM_LIMIT_BYTES)


def _call(body, **kw):
    return pl.pallas_call(body, **kw)


def _sigmoid(x):
    return 1.0 / (1.0 + jnp.exp(-x))


def _matmul(name, grid, a_ops, b_ops, terms, dims, out_shape, out_specs, epilogue, extra_ops=(), nk=1,
            acc_shapes=(), alias=None):
    na, nb, ne, no = len(a_ops), len(b_ops), len(extra_ops), len(out_shape)

    def body(*refs):
        a = refs[:na]
        b = refs[na:na + nb]
        e = refs[na + nb:na + nb + ne]
        first_out = na + nb + ne + (1 if alias is not None else 0)
        o = refs[first_out:first_out + no]
        accs = refs[first_out + no:]

        def partial(t):
            tot = None
            for ai, bi in t:
                d = lax.dot_general(a[ai][...], b[bi][...], dims, preferred_element_type=F32)
                tot = d if tot is None else tot + d
            return tot

        if nk == 1:
            epilogue([partial(t) for t in terms], e, o)
        else:
            k = pl.program_id(len(grid) - 1)

            @pl.when(k == 0)
            def _():
                for acc in accs:
                    acc[...] = jnp.zeros(acc.shape, F32)

            for acc, t in zip(accs, terms):
                acc[...] += partial(t)

            @pl.when(k == nk - 1)
            def _():
                epilogue([acc[...] for acc in accs], e, o)

    ops = list(a_ops) + list(b_ops) + list(extra_ops)
    arrays = [x for x, _ in ops]
    in_specs = [s for _, s in ops]
    kw = {}
    if alias is not None:
        arrays.append(alias[0])
        in_specs.append(pl.BlockSpec(memory_space=pl.ANY))
        kw["input_output_aliases"] = {len(arrays) - 1: alias[1]}
    scratch = [pltpu.VMEM(s, F32) for s in acc_shapes] if nk > 1 else []
    return _call(body, name=name, grid=grid, in_specs=in_specs, out_specs=out_specs, out_shape=out_shape,
                 scratch_shapes=scratch, compiler_params=_params(len(grid)), **kw)(*arrays)


def _mm_ffn_in(h, w_g, layer):
    t_len, d = h.shape
    fs = w_g.shape[3]
    f = 2 * fs
    tm = _pick(t_len, (2048, 1024, 512))
    tn = _pick(fs, (256, 128))
    nj = fs // tn

    def epilogue(accs, e, o):
        g, u = accs
        o[0][0] = g.astype(BF16)
        o[0][1] = u.astype(BF16)
        o[1][...] = (g * _sigmoid(g) * u).astype(BF16)

    return _matmul(
        "ffn_in", (t_len // tm, 2, nj),
        [(h, pl.BlockSpec((tm, d), lambda i, s, j: (i, 0)))],
        [(w_g, pl.BlockSpec((None, None, d, tn), lambda i, s, j: (layer, s, 0, j))),
         (w_g, pl.BlockSpec((None, None, d, tn), lambda i, s, j: (layer, s + 2, 0, j)))],
        [[(0, 0)], [(0, 1)]], NN,
        [jax.ShapeDtypeStruct((2, t_len, f), BF16), jax.ShapeDtypeStruct((t_len, f), BF16)],
        [pl.BlockSpec((2, tm, tn), lambda i, s, j: (0, i, s * nj + j)),
         pl.BlockSpec((tm, tn), lambda i, s, j: (i, s * nj + j))],
        epilogue)


def _mm_out_res(name, a, w_g, layer, x, scale):
    t_len = a.shape[0]
    ks, n = w_g.shape[2], w_g.shape[3]
    tm = _pick(t_len, (1024, 512))
    tn = _pick(n, (1024,))
    tk = _pick(ks, (1408, 512, 256))
    nks = ks // tk

    def epilogue(accs, e, o):
        o[0][...] = e[0][...] + scale * accs[0]

    return _matmul(
        name, (t_len // tm, n // tn, N_CHIPS * nks),
        [(a, pl.BlockSpec((tm, tk), lambda i, j, k: (i, k)))],
        [(w_g, pl.BlockSpec((None, None, tk, tn), lambda i, j, k: (layer, k // nks, k % nks, j)))],
        [[(0, 0)]], NN,
        [jax.ShapeDtypeStruct((t_len, n), F32)],
        [pl.BlockSpec((tm, tn), lambda i, j, k: (i, j))],
        epilogue, extra_ops=[(x, pl.BlockSpec((tm, tn), lambda i, j, k: (i, j)))],
        nk=N_CHIPS * nks, acc_shapes=[(tm, tn)])


def _mm_proj(h, w_g, layer):
    t_len, d = h.shape
    cs = w_g.shape[3]
    tm = _pick(t_len, (1024, 512))

    def epilogue(accs, e, o):
        o[0][...] = accs[0]

    return _matmul(
        "mix_in", (t_len // tm, N_CHIPS),
        [(h, pl.BlockSpec((tm, d), lambda i, s: (i, 0)))],
        [(w_g, pl.BlockSpec((None, None, d, cs), lambda i, s: (layer, s, 0, 0)))],
        [[(0, 0)]], NN,
        [jax.ShapeDtypeStruct((t_len, N_CHIPS * cs), F32)],
        [pl.BlockSpec((tm, cs), lambda i, s: (i, s))],
        epilogue)


def _mm_dact_swiglu(dxb, w_g, layer, gu, scale):
    t_len, d = dxb.shape
    rs = w_g.shape[2]
    tm = _pick(t_len, (512,))
    tn = _pick(rs, (1408, 256, 128))
    nj = rs // tn

    def epilogue(accs, e, o):
        dact = scale * accs[0]
        g = e[0][0].astype(F32)
        u = e[0][1].astype(F32)
        sig = _sigmoid(g)
        o[0][0] = (dact * u * (sig * (1.0 + g * (1.0 - sig)))).astype(BF16)
        o[0][1] = (dact * (g * sig)).astype(BF16)

    gu_spec = pl.BlockSpec((2, tm, tn), lambda i, s, j: (0, i, s * nj + j))
    return _matmul(
        "ffn_dact", (t_len // tm, N_CHIPS, nj),
        [(dxb, pl.BlockSpec((tm, d), lambda i, s, j: (i, 0)))],
        [(w_g, pl.BlockSpec((None, None, tn, d), lambda i, s, j: (layer, s, j, 0)))],
        [[(0, 0)]], NT,
        [jax.ShapeDtypeStruct(gu.shape, BF16)], [gu_spec],
        epilogue, extra_ops=[(gu, gu_spec)])[0]


def _mm_dh_ffn(dgu, w_g, layer):
    t_len = dgu.shape[1]
    d, fs = w_g.shape[2], w_g.shape[3]
    tm = _pick(t_len, (1024, 512))
    tk = _pick(fs, (256, 128))
    nks = fs // tk
    nk = 2 * nks

    def epilogue(accs, e, o):
        o[0][...] = accs[0]

    return _matmul(
        "ffn_dh", (t_len // tm, nk),
        [(dgu, pl.BlockSpec((None, tm, tk), lambda i, k: (0, i, k))),
         (dgu, pl.BlockSpec((None, tm, tk), lambda i, k: (1, i, k)))],
        [(w_g, pl.BlockSpec((None, None, d, tk), lambda i, k: (layer, k // nks, 0, k % nks))),
         (w_g, pl.BlockSpec((None, None, d, tk), lambda i, k: (layer, k // nks + 2, 0, k % nks)))],
        [[(0, 0), (1, 1)]], NT,
        [jax.ShapeDtypeStruct((t_len, d), F32)],
        [pl.BlockSpec((tm, d), lambda i, k: (i, 0))],
        epilogue, nk=nk, acc_shapes=[(tm, d)])[0]


def _mm_dw(name, a, a_spec_of, b, b_spec_of, layer, stack, rows, cols, tn, scale):
    t_len = a.shape[0]
    tt = _pick(t_len, (1024, 512))
    nj = cols // tn

    def epilogue(accs, e, o):
        o[0][...] = (scale * accs[0]).astype(BF16)

    shape = jax.ShapeDtypeStruct((2, N_CHIPS, rows, cols), BF16)
    return _matmul(
        name, (N_CHIPS, nj, t_len // tt),
        [(a, a_spec_of(tt))], [(b, b_spec_of(tt, tn, nj))],
        [[(0, 0)]], TN, [shape],
        [pl.BlockSpec((None, None, rows, tn), lambda s, j, t: (layer, s, 0, j))],
        epilogue, nk=t_len // tt, acc_shapes=[(rows, tn)],
        alias=None if stack is None else (stack, 0))[0]


def _dw_ffn_in(h, dgu, layer, stack):
    d = h.shape[1]
    fs = dgu.shape[2] // 2
    tn = _pick(fs, (1408, 256))
    return _mm_dw(
        "ffn_dw_in", h, lambda tt: pl.BlockSpec((tt, d), lambda s, j, t: (t, 0)),
        dgu, lambda tt, tn_, nj: pl.BlockSpec((None, tt, tn_), lambda s, j, t: (s // 2, t, (s % 2) * nj + j)),
        layer, stack, d, fs, tn, 1.0)


def _dw_rows(name, a, dxb, layer, stack, scale):
    rs = a.shape[1] // N_CHIPS
    d = dxb.shape[1]
    tn = _pick(d, (1024,))
    return _mm_dw(
        name, a, lambda tt: pl.BlockSpec((tt, rs), lambda s, j, t: (t, s)),
        dxb, lambda tt, tn_, nj: pl.BlockSpec((tt, tn_), lambda s, j, t: (t, j)),
        layer, stack, rs, d, tn, scale)


def _dw_mix_in(h, dp, layer, stack):
    d = h.shape[1]
    cs = dp.shape[1] // N_CHIPS
    return _mm_dw(
        "mix_dw_in", h, lambda tt: pl.BlockSpec((tt, d), lambda s, j, t: (t, 0)),
        dp, lambda tt, tn_, nj: pl.BlockSpec((tt, tn_), lambda s, j, t: (t, s)),
        layer, stack, d, cs, cs, 1.0)


def _mm_dmix(dxb, w_g, layer):
    t_len, d = dxb.shape
    rs = w_g.shape[2]
    tm = _pick(t_len, (1024, 512))

    def epilogue(accs, e, o):
        o[0][...] = accs[0]

    return _matmul(
        "mix_dout", (t_len // tm, N_CHIPS),
        [(dxb, pl.BlockSpec((tm, d), lambda i, s: (i, 0)))],
        [(w_g, pl.BlockSpec((None, None, rs, d), lambda i, s: (layer, s, 0, 0)))],
        [[(0, 0)]], NT,
        [jax.ShapeDtypeStruct((t_len, N_CHIPS * rs), F32)],
        [pl.BlockSpec((tm, rs), lambda i, s: (i, s))],
        epilogue)[0]


def _mm_dh_mix(dp, w_g, layer):
    t_len = dp.shape[0]
    d, cs = w_g.shape[2], w_g.shape[3]
    tm = _pick(t_len, (1024, 512))

    def epilogue(accs, e, o):
        o[0][...] = accs[0]

    return _matmul(
        "mix_dh", (t_len // tm, N_CHIPS),
        [(dp, pl.BlockSpec((tm, cs), lambda i, k: (i, k)))],
        [(w_g, pl.BlockSpec((None, None, d, cs), lambda i, k: (layer, k, 0, 0)))],
        [[(0, 0)]], NT,
        [jax.ShapeDtypeStruct((t_len, d), F32)],
        [pl.BlockSpec((tm, d), lambda i, k: (i, 0))],
        epilogue, nk=N_CHIPS, acc_shapes=[(tm, d)])[0]


def _rms_stats(x):
    r = lax.rsqrt(jnp.mean(x * x, axis=-1, keepdims=True) + NORM_EPS)
    return r, x * r


def _accumulate(ref, part, first):
    @pl.when(first)
    def _():
        ref[...] = part

    @pl.when(jnp.logical_not(first))
    def _():
        ref[...] += part


def _rmsnorm_fwd(x, g):
    t_len, d = x.shape
    tm = _pick(t_len, (512,))

    def body(x_ref, g_ref, h_ref):
        _, xhat = _rms_stats(x_ref[...])
        h_ref[...] = (xhat * g_ref[...]).astype(BF16)

    row = pl.BlockSpec((tm, d), lambda i: (i, 0))
    vec = pl.BlockSpec((1, d), lambda i: (0, 0))
    return _call(body, name="rmsnorm_fwd", grid=(t_len // tm,), in_specs=[row, vec], out_specs=row,
                 out_shape=jax.ShapeDtypeStruct((t_len, d), BF16), compiler_params=_params(1))(x, g)


def _rmsnorm_bwd(dh, x, g, dres):
    t_len, d = x.shape
    tm = _pick(t_len, (256,))

    def body(dh_ref, x_ref, g_ref, dres_ref, dx_ref, dxb_ref, dg_ref):
        r, xhat = _rms_stats(x_ref[...])
        dh_v = dh_ref[...]
        gd = dh_v * g_ref[...]
        dx = dres_ref[...] + r * (gd - xhat * jnp.mean(gd * xhat, axis=-1, keepdims=True))
        dx_ref[...] = dx
        dxb_ref[...] = dx.astype(BF16)
        _accumulate(dg_ref, jnp.sum(dh_v * xhat, axis=0, keepdims=True), pl.program_id(0) == 0)

    row = pl.BlockSpec((tm, d), lambda i: (i, 0))
    vec = pl.BlockSpec((1, d), lambda i: (0, 0))
    return _call(body, name="rmsnorm_bwd", grid=(t_len // tm,), in_specs=[row, row, vec, row],
                 out_specs=[row, row, vec],
                 out_shape=[jax.ShapeDtypeStruct((t_len, d), F32), jax.ShapeDtypeStruct((t_len, d), BF16),
                            jax.ShapeDtypeStruct((1, d), F32)],
                 compiler_params=_params(1))(dh, x, g, dres)


def _loss_head(x, g, target):
    t_len, d = x.shape
    tm = _pick(t_len, (256,))

    def body(x_ref, g_ref, t_ref, dx_ref, dxb_ref, dg_ref, loss_ref):
        first = pl.program_id(0) == 0
        r, xhat = _rms_stats(x_ref[...])
        g_v = g_ref[...]
        err = xhat * g_v - t_ref[...]
        per_token = jnp.mean(err * err, axis=-1, keepdims=True)
        part = 0.5 * jnp.sum(per_token, axis=0, keepdims=True)
        _accumulate(loss_ref, jnp.broadcast_to(part, (1, LANES)), first)
        dy = err * (1.0 / d)
        _accumulate(dg_ref, jnp.sum(dy * xhat, axis=0, keepdims=True), first)
        gd = dy * g_v
        dx = r * (gd - xhat * jnp.mean(gd * xhat, axis=-1, keepdims=True))
        dx_ref[...] = dx
        dxb_ref[...] = dx.astype(BF16)

    row = pl.BlockSpec((tm, d), lambda i: (i, 0))
    vec = pl.BlockSpec((1, d), lambda i: (0, 0))
    return _call(body, name="loss_head", grid=(t_len // tm,), in_specs=[row, vec, row],
                 out_specs=[row, row, vec, pl.BlockSpec((1, LANES), lambda i: (0, 0))],
                 out_shape=[jax.ShapeDtypeStruct((t_len, d), F32), jax.ShapeDtypeStruct((t_len, d), BF16),
                            jax.ShapeDtypeStruct((1, d), F32), jax.ShapeDtypeStruct((1, LANES), F32)],
                 compiler_params=_params(1))(x, g, target)


def _ln_stats(x):
    mu = jnp.mean(x, axis=-1, keepdims=True)
    xc = x - mu
    r = lax.rsqrt(jnp.mean(xc * xc, axis=-1, keepdims=True) + NORM_EPS)
    return r, xc * r


def _ln_bwd(dy, r, xhat, g):
    dxh = dy * g
    return r * (dxh - jnp.mean(dxh, axis=-1, keepdims=True) - xhat * jnp.mean(dxh * xhat, axis=-1, keepdims=True))


def _rope_tables(positions):
    t_len = positions.shape[-1]
    inv_freq = 1.0 / (ROPE_THETA ** (jnp.arange(0, 2 * ROT_HALF, 2, dtype=F32) / (2 * ROT_HALF)))
    ang = positions.astype(F32).reshape(t_len, 1) * inv_freq
    cos = jnp.tile(jnp.cos(ang), (1, LANES // ROT_HALF))
    sin = jnp.tile(jnp.sin(ang), (1, LANES // ROT_HALF))
    lane = jnp.arange(LANES) % HEAD_DIM
    c = jnp.where(lane < 2 * ROT_HALF, cos, 1.0)
    s1 = jnp.where(lane < ROT_HALF, -sin, 0.0)
    s2 = jnp.where((lane >= ROT_HALF) & (lane < 2 * ROT_HALF), sin, 0.0)
    return c.astype(F32), s1.astype(F32), s2.astype(F32)


def _rope_fwd(p, tables):
    t_len = p.shape[0]
    tm = _pick(t_len, (256,))
    n_rot = K_END // LANES

    def body(p_ref, c_ref, s1_ref, s2_ref, o_ref):
        c, s1, s2 = c_ref[...], s1_ref[...], s2_ref[...]
        for j in range(V_END // LANES):
            sl = slice(j * LANES, (j + 1) * LANES)
            t = p_ref[:, sl]
            if j < n_rot:
                t = t * c + pltpu.roll(t, LANES - ROT_HALF, 1) * s1 + pltpu.roll(t, ROT_HALF, 1) * s2
            o_ref[:, sl] = t.astype(BF16)

    tab = pl.BlockSpec((tm, LANES), lambda i: (i, 0))
    blk = pl.BlockSpec((tm, V_END), lambda i: (i, 0))
    return _call(body, name="rope_fwd", grid=(t_len // tm,), in_specs=[blk, tab, tab, tab], out_specs=blk,
                 out_shape=jax.ShapeDtypeStruct((t_len, V_END), BF16), compiler_params=_params(1))(p, *tables)


def _assemble_dp(dq, dkc, dkp, dvc, dvp, tables, dalin, dagate, du, dvin):
    t_len = dq.shape[0]
    nb = t_len // BLK

    def body(dq_ref, dkc_ref, dkp_ref, dvc_ref, dvp_ref, c_ref, s1_ref, s2_ref, dalin_ref, dagate_ref,
             du_ref, dvin_ref, o_ref):
        keep = (pl.program_id(0) < nb - 1).astype(F32)
        c, s1, s2 = c_ref[...], s1_ref[...], s2_ref[...]

        def unrotate(dr):
            return dr * c + pltpu.roll(dr * s1, ROT_HALF, 1) + pltpu.roll(dr * s2, LANES - ROT_HALF, 1)

        for j in range(Q_END // LANES):
            sl = slice(j * LANES, (j + 1) * LANES)
            o_ref[:, sl] = unrotate(dq_ref[:, sl]).astype(BF16)
        for j in range((K_END - Q_END) // LANES):
            sl = slice(j * LANES, (j + 1) * LANES)
            dk = dkc_ref[:, sl] + keep * dkp_ref[:, sl]
            o_ref[:, Q_END + j * LANES:Q_END + (j + 1) * LANES] = unrotate(dk).astype(BF16)
            dv = dvc_ref[:, sl] + keep * dvp_ref[:, sl]
            o_ref[:, K_END + j * LANES:K_END + (j + 1) * LANES] = dv.astype(BF16)
        o_ref[:, V_END:V_END + CONV_CH] = dalin_ref[...]
        o_ref[:, V_END + CONV_CH:CONV_END] = dagate_ref[...]
        o_ref[:, CONV_END:CONV_END + SGU_CH] = du_ref[...]
        o_ref[:, CONV_END + SGU_CH:IN_COLS] = dvin_ref[...]

    def cur(w):
        return pl.BlockSpec((BLK, w), lambda i: (i, 0))

    def nxt(w):
        return pl.BlockSpec((BLK, w), lambda i: (jnp.minimum(i + 1, nb - 1), 0))

    kvw = K_END - Q_END
    return _call(body, name="assemble_dp", grid=(nb,),
                 in_specs=[cur(Q_END), cur(kvw), nxt(kvw), cur(kvw), nxt(kvw), cur(LANES), cur(LANES), cur(LANES),
                           cur(CONV_CH), cur(CONV_CH), cur(SGU_CH), cur(SGU_CH)],
                 out_specs=cur(IN_COLS), out_shape=jax.ShapeDtypeStruct((t_len, IN_COLS), BF16),
                 compiler_params=_params(1))(dq, dkc, dkp, dvc, dvp, *tables, dalin, dagate, du, dvin)


def _attn_probs(q_ref, kp_ref, kc_ref, snk_ref):
    n = pl.program_id(1)
    q = q_ref[...].reshape(GQ * BLK, HEAD_DIM)
    kk = jnp.concatenate([kp_ref[...], kc_ref[...]], axis=0)
    s = lax.dot_general(q, kk, NT, preferred_element_type=F32) * (HEAD_DIM ** -0.5)
    qi = jnp.bitwise_and(lax.broadcasted_iota(jnp.int32, (GQ * BLK, 2 * BLK), 0), BLK - 1)
    kj = lax.broadcasted_iota(jnp.int32, (GQ * BLK, 2 * BLK), 1)
    dist = qi + BLK - kj
    mask = (dist >= 0) & (dist < BLK) & ((kj >= BLK) | (n > 0))
    s = jnp.where(mask, s, -1e30)
    snk = snk_ref[...]
    m = jnp.maximum(jnp.max(s, axis=-1, keepdims=True), snk)
    e = jnp.exp(s - m)
    es = jnp.exp(snk - m)
    den = jnp.sum(e, axis=-1, keepdims=True) + es
    return q, kk, e / den, es / den


def _attn_specs(nb):
    q_spec = pl.BlockSpec((GQ, BLK, HEAD_DIM), lambda g, n: (g, n, 0))
    cur = pl.BlockSpec((None, BLK, HEAD_DIM), lambda g, n: (g, n, 0))
    prev = pl.BlockSpec((None, BLK, HEAD_DIM), lambda g, n: (g, jnp.maximum(n - 1, 0), 0))
    snk = pl.BlockSpec((None, GQ * BLK, 1), lambda g, n: (g, 0, 0))
    return q_spec, cur, prev, snk


def _attn_fwd(q, k, v, snk):
    t_len = q.shape[1]
    nb = t_len // BLK

    def body(q_ref, kp_ref, kc_ref, vp_ref, vc_ref, snk_ref, o_ref):
        _, _, p, _ = _attn_probs(q_ref, kp_ref, kc_ref, snk_ref)
        vv = jnp.concatenate([vp_ref[...], vc_ref[...]], axis=0)
        o = lax.dot_general(p.astype(BF16), vv, NN, preferred_element_type=F32)
        o_ref[...] = o.reshape(GQ, BLK, HEAD_DIM).astype(BF16)

    q_spec, cur, prev, snk_spec = _attn_specs(nb)
    return _call(body, name="attn_fwd", grid=(N_KV_HEADS, nb), in_specs=[q_spec, prev, cur, prev, cur, snk_spec],
                 out_specs=q_spec, out_shape=jax.ShapeDtypeStruct(q.shape, BF16),
                 compiler_params=_params(2))(q, k, k, v, v, snk)


def _attn_bwd(q, k, v, snk, do):
    t_len = q.shape[1]
    nb = t_len // BLK

    def body(q_ref, kp_ref, kc_ref, vp_ref, vc_ref, snk_ref, do_ref, dq_ref, dkp_ref, dkc_ref, dvp_ref, dvc_ref,
             dsnk_ref):
        q_v, kk, p, p_snk = _attn_probs(q_ref, kp_ref, kc_ref, snk_ref)
        vv = jnp.concatenate([vp_ref[...], vc_ref[...]], axis=0)
        do_v = do_ref[...].reshape(GQ * BLK, HEAD_DIM).astype(BF16)
        dp = lax.dot_general(do_v, vv, NT, preferred_element_type=F32)
        delta = jnp.sum(p * dp, axis=-1, keepdims=True)
        ds = (p * (dp - delta)).astype(BF16)
        scale = HEAD_DIM ** -0.5
        dq = lax.dot_general(ds, kk, NN, preferred_element_type=F32) * scale
        dq_ref[...] = dq.reshape(GQ, BLK, HEAD_DIM)
        dkk = lax.dot_general(ds, q_v, TN, preferred_element_type=F32) * scale
        dkp_ref[...] = dkk[:BLK]
        dkc_ref[...] = dkk[BLK:]
        dvv = lax.dot_general(p.astype(BF16), do_v, TN, preferred_element_type=F32)
        dvp_ref[...] = dvv[:BLK]
        dvc_ref[...] = dvv[BLK:]
        per_row = -p_snk * delta
        row = lax.broadcasted_iota(jnp.int32, (SUBLANES, LANES), 0)
        tile = jnp.zeros((SUBLANES, LANES), F32)
        for hh in range(GQ):
            tot = jnp.sum(per_row[hh * BLK:(hh + 1) * BLK], axis=0, keepdims=True)
            tile = tile + jnp.where(row == hh, tot, 0.0)
        _accumulate(dsnk_ref, tile, pl.program_id(1) == 0)

    q_spec, cur, prev, snk_spec = _attn_specs(nb)
    kv_shape = jax.ShapeDtypeStruct(k.shape, F32)
    return _call(body, name="attn_bwd", grid=(N_KV_HEADS, nb),
                 in_specs=[q_spec, prev, cur, prev, cur, snk_spec, q_spec],
                 out_specs=[q_spec, cur, cur, cur, cur, pl.BlockSpec((None, SUBLANES, LANES), lambda g, n: (g, 0, 0))],
                 out_shape=[jax.ShapeDtypeStruct(q.shape, F32), kv_shape, kv_shape, kv_shape, kv_shape,
                            jax.ShapeDtypeStruct((N_KV_HEADS, SUBLANES, LANES), F32)],
                 compiler_params=_params(2))(q, k, k, v, v, snk, do)


CONV_CHUNK = 256


def _shift_up(win, s):
    n = win.shape[0]
    return win if s == 0 else pltpu.roll(win, n - s, 0)


def _conv_col_specs(t_len):
    lin = pl.BlockSpec((t_len, LANES), lambda j: (0, V_END // LANES + j))
    gate = pl.BlockSpec((t_len, LANES), lambda j: (0, (V_END + CONV_CH) // LANES + j))
    col = pl.BlockSpec((t_len, LANES), lambda j: (0, j))
    wsp = pl.BlockSpec((CONV_PAD, LANES), lambda j: (0, j))
    return lin, gate, col, wsp


def _conv_fwd(p, w, b):
    t_len = p.shape[0]
    ch = CONV_CHUNK

    def body(lin_ref, gate_ref, w_ref, b_ref, y_ref, hp_ref):
        hp_ref[0:CONV_PAD, :] = jnp.zeros((CONV_PAD, LANES), F32)

        def fill(c, carry):
            r0 = pl.multiple_of(c * ch, ch)
            hp_ref[pl.ds(r0 + CONV_PAD, ch), :] = lin_ref[pl.ds(r0, ch), :] * _sigmoid(gate_ref[pl.ds(r0, ch), :])
            return carry

        lax.fori_loop(0, t_len // ch, fill, 0)

        def conv(c, carry):
            r0 = pl.multiple_of(c * ch, ch)
            win = hp_ref[pl.ds(r0, ch + CONV_PAD), :]
            acc = jnp.zeros((ch, LANES), F32)
            for k in range(CONV_WIDTH):
                acc = acc + _shift_up(win, CONV_PAD - (CONV_WIDTH - 1) + k)[:ch] * w_ref[k:k + 1, :]
            y_ref[pl.ds(r0, ch), :] = acc + b_ref[...]
            return carry

        lax.fori_loop(0, t_len // ch, conv, 0)

    lin, gate, col, wsp = _conv_col_specs(t_len)
    return _call(body, name="conv_fwd", grid=(CONV_CH // LANES,),
                 in_specs=[lin, gate, wsp, pl.BlockSpec((1, LANES), lambda j: (0, j))], out_specs=col,
                 out_shape=jax.ShapeDtypeStruct((t_len, CONV_CH), F32),
                 scratch_shapes=[pltpu.VMEM((t_len + CONV_PAD, LANES), F32)],
                 compiler_params=_params(1))(p, p, w, b)


def _conv_post_fwd(y, g, b):
    t_len = y.shape[0]
    tm = _pick(t_len, (512,))

    def body(y_ref, g_ref, b_ref, o_ref):
        _, xhat = _ln_stats(y_ref[...])
        z = xhat * g_ref[...] + b_ref[...]
        o_ref[...] = (z * _sigmoid(z)).astype(BF16)

    row = pl.BlockSpec((tm, CONV_CH), lambda i: (i, 0))
    vec = pl.BlockSpec((1, CONV_CH), lambda i: (0, 0))
    return _call(body, name="conv_post_fwd", grid=(t_len // tm,), in_specs=[row, vec, vec], out_specs=row,
                 out_shape=jax.ShapeDtypeStruct((t_len, CONV_CH), BF16), compiler_params=_params(1))(y, g, b)


def _conv_post_bwd(dmix, y, g, b):
    t_len = y.shape[0]
    tm = _pick(t_len, (512,))

    def body(do_ref, y_ref, g_ref, b_ref, dy_ref, dg_ref, db_ref, dcb_ref):
        first = pl.program_id(0) == 0
        r, xhat = _ln_stats(y_ref[...])
        g_v = g_ref[...]
        z = xhat * g_v + b_ref[...]
        sig = _sigmoid(z)
        dz = do_ref[...] * (sig * (1.0 + z * (1.0 - sig)))
        _accumulate(db_ref, jnp.sum(dz, axis=0, keepdims=True), first)
        _accumulate(dg_ref, jnp.sum(dz * xhat, axis=0, keepdims=True), first)
        dy = _ln_bwd(dz, r, xhat, g_v)
        dy_ref[...] = dy
        _accumulate(dcb_ref, jnp.sum(dy, axis=0, keepdims=True), first)

    row = pl.BlockSpec((tm, CONV_CH), lambda i: (i, 0))
    do_spec = pl.BlockSpec((tm, CONV_CH), lambda i: (i, Q_END // CONV_CH))
    vec = pl.BlockSpec((1, CONV_CH), lambda i: (0, 0))
    vshape = jax.ShapeDtypeStruct((1, CONV_CH), F32)
    return _call(body, name="conv_post_bwd", grid=(t_len // tm,), in_specs=[do_spec, row, vec, vec],
                 out_specs=[row, vec, vec, vec],
                 out_shape=[jax.ShapeDtypeStruct((t_len, CONV_CH), F32), vshape, vshape, vshape],
                 compiler_params=_params(1))(dmix, y, g, b)


def _conv_bwd(p, dy, w):
    t_len = p.shape[0]
    ch = CONV_CHUNK

    def body(lin_ref, gate_ref, dy_ref, w_ref, dlin_ref, dgate_ref, dw_ref, hp_ref, dyp_ref):
        hp_ref[0:CONV_PAD, :] = jnp.zeros((CONV_PAD, LANES), F32)
        dyp_ref[t_len:t_len + CONV_PAD, :] = jnp.zeros((CONV_PAD, LANES), F32)
        dw_ref[...] = jnp.zeros((CONV_PAD, LANES), F32)

        def fill(c, carry):
            r0 = pl.multiple_of(c * ch, ch)
            hp_ref[pl.ds(r0 + CONV_PAD, ch), :] = lin_ref[pl.ds(r0, ch), :] * _sigmoid(gate_ref[pl.ds(r0, ch), :])
            dyp_ref[pl.ds(r0, ch), :] = dy_ref[pl.ds(r0, ch), :]
            return carry

        lax.fori_loop(0, t_len // ch, fill, 0)

        def step(c, carry):
            r0 = pl.multiple_of(c * ch, ch)
            win_h = hp_ref[pl.ds(r0, ch + CONV_PAD), :]
            win_dy = dyp_ref[pl.ds(r0, ch + CONV_PAD), :]
            dyc = win_dy[:ch]
            dh = jnp.zeros((ch, LANES), F32)
            for k in range(CONV_WIDTH):
                tap = _shift_up(win_h, CONV_PAD - (CONV_WIDTH - 1) + k)[:ch]
                dw_ref[k:k + 1, :] += jnp.sum(dyc * tap, axis=0, keepdims=True)
                dh = dh + _shift_up(win_dy, CONV_WIDTH - 1 - k)[:ch] * w_ref[k:k + 1, :]
            lin = lin_ref[pl.ds(r0, ch), :]
            sig = _sigmoid(gate_ref[pl.ds(r0, ch), :])
            dlin_ref[pl.ds(r0, ch), :] = (dh * sig).astype(BF16)
            dgate_ref[pl.ds(r0, ch), :] = (dh * lin * (sig * (1.0 - sig))).astype(BF16)
            return carry

        lax.fori_loop(0, t_len // ch, step, 0)

    lin, gate, col, wsp = _conv_col_specs(t_len)
    half = jax.ShapeDtypeStruct((t_len, CONV_CH), BF16)
    return _call(body, name="conv_bwd", grid=(CONV_CH // LANES,), in_specs=[lin, gate, col, wsp],
                 out_specs=[col, col, wsp],
                 out_shape=[half, half, jax.ShapeDtypeStruct((CONV_PAD, CONV_CH), F32)],
                 scratch_shapes=[pltpu.VMEM((t_len + CONV_PAD, LANES), F32), pltpu.VMEM((t_len + CONV_PAD, LANES), F32)],
                 compiler_params=_params(1))(p, p, dy, w)


def _sgu_mixed(v, w_ref, bt_ref, j):
    lane = lax.broadcasted_iota(jnp.int32, (BLK, LANES), 1)
    lo = lane < HEAD_DIM
    tri = lax.broadcasted_iota(jnp.int32, (BLK, BLK), 0) >= lax.broadcasted_iota(jnp.int32, (BLK, BLK), 1)
    vs = v[:, j * LANES:(j + 1) * LANES]
    v_lo = jnp.where(lo, vs, 0.0).astype(BF16)
    v_hi = jnp.where(lo, 0.0, vs).astype(BF16)
    w_lo = jnp.where(tri, w_ref[2 * j], 0.0).astype(BF16)
    w_hi = jnp.where(tri, w_ref[2 * j + 1], 0.0).astype(BF16)
    m = (lax.dot_general(w_lo, v_lo, NN, preferred_element_type=F32)
         + lax.dot_general(w_hi, v_hi, NN, preferred_element_type=F32))
    bias = jnp.where(lo, bt_ref[:, 2 * j:2 * j + 1], bt_ref[:, 2 * j + 1:2 * j + 2])
    return m + bias, (v_lo, v_hi, w_lo, w_hi, lo, tri)


def _sgu_specs():
    u_spec = pl.BlockSpec((BLK, SGU_CH), lambda i: (i, CONV_END // SGU_CH))
    v_spec = pl.BlockSpec((BLK, SGU_CH), lambda i: (i, CONV_END // SGU_CH + 1))
    vec = pl.BlockSpec((1, SGU_CH), lambda i: (0, 0))
    w_spec = pl.BlockSpec((SGU_HEADS, BLK, BLK), lambda i: (0, 0, 0))
    bt_spec = pl.BlockSpec((BLK, SGU_HEADS), lambda i: (0, 0))
    row = pl.BlockSpec((BLK, SGU_CH), lambda i: (i, 0))
    return u_spec, v_spec, vec, w_spec, bt_spec, row


def _sgu_fwd(p, g, b, w, bt):
    t_len = p.shape[0]

    def body(u_ref, vin_ref, g_ref, b_ref, w_ref, bt_ref, o_ref):
        _, xhat = _ln_stats(vin_ref[...])
        v = xhat * g_ref[...] + b_ref[...]
        for j in range(SGU_CH // LANES):
            m, _ = _sgu_mixed(v, w_ref, bt_ref, j)
            sl = slice(j * LANES, (j + 1) * LANES)
            o_ref[:, sl] = (u_ref[:, sl] * m).astype(BF16)

    u_spec, v_spec, vec, w_spec, bt_spec, row = _sgu_specs()
    return _call(body, name="sgu_fwd", grid=(t_len // BLK,), in_specs=[u_spec, v_spec, vec, vec, w_spec, bt_spec],
                 out_specs=row, out_shape=jax.ShapeDtypeStruct((t_len, SGU_CH), BF16),
                 compiler_params=_params(1))(p, p, g, b, w, bt)


def _sgu_bwd(p, dmix, g, b, w, bt):
    t_len = p.shape[0]

    def body(u_ref, vin_ref, do_ref, g_ref, b_ref, w_ref, bt_ref, du_ref, dvin_ref, dw_ref, dbt_ref, dg_ref,
             db_ref, dv_ref):
        first = pl.program_id(0) == 0
        r, xhat = _ln_stats(vin_ref[...])
        g_v = g_ref[...]
        v = xhat * g_v + b_ref[...]
        lane = lax.broadcasted_iota(jnp.int32, (BLK, LANES), 1)
        dbt = jnp.zeros((BLK, LANES), F32)

        @pl.when(first)
        def _():
            dw_ref[...] = jnp.zeros((SGU_HEADS, BLK, BLK), F32)

        for j in range(SGU_CH // LANES):
            m, (v_lo, v_hi, w_lo, w_hi, lo, tri) = _sgu_mixed(v, w_ref, bt_ref, j)
            sl = slice(j * LANES, (j + 1) * LANES)
            do_v = do_ref[:, sl]
            du_ref[:, sl] = (do_v * m).astype(BF16)
            dm = do_v * u_ref[:, sl]
            dm_lo = jnp.where(lo, dm, 0.0)
            dm_hi = jnp.where(lo, 0.0, dm)
            dbt = dbt + jnp.where(lane == 2 * j, jnp.sum(dm_lo, axis=-1, keepdims=True), 0.0)
            dbt = dbt + jnp.where(lane == 2 * j + 1, jnp.sum(dm_hi, axis=-1, keepdims=True), 0.0)
            dm_lo, dm_hi = dm_lo.astype(BF16), dm_hi.astype(BF16)
            dw_ref[2 * j] += jnp.where(tri, lax.dot_general(dm_lo, v_lo, NT, preferred_element_type=F32), 0.0)
            dw_ref[2 * j + 1] += jnp.where(tri, lax.dot_general(dm_hi, v_hi, NT, preferred_element_type=F32), 0.0)
            dv_ref[:, sl] = (lax.dot_general(w_lo, dm_lo, TN, preferred_element_type=F32)
                             + lax.dot_general(w_hi, dm_hi, TN, preferred_element_type=F32))
        _accumulate(dbt_ref, dbt, first)
        dv = dv_ref[...]
        _accumulate(db_ref, jnp.sum(dv, axis=0, keepdims=True), first)
        _accumulate(dg_ref, jnp.sum(dv * xhat, axis=0, keepdims=True), first)
        dvin_ref[...] = _ln_bwd(dv, r, xhat, g_v).astype(BF16)

    u_spec, v_spec, vec, w_spec, bt_spec, row = _sgu_specs()
    do_spec = pl.BlockSpec((BLK, SGU_CH), lambda i: (i, (Q_END + CONV_CH) // SGU_CH))
    half = jax.ShapeDtypeStruct((t_len, SGU_CH), BF16)
    vshape = jax.ShapeDtypeStruct((1, SGU_CH), F32)
    return _call(body, name="sgu_bwd", grid=(t_len // BLK,),
                 in_specs=[u_spec, v_spec, do_spec, vec, vec, w_spec, bt_spec],
                 out_specs=[row, row, w_spec, pl.BlockSpec((BLK, LANES), lambda i: (0, 0)), vec, vec],
                 out_shape=[half, half, jax.ShapeDtypeStruct((SGU_HEADS, BLK, BLK), F32),
                            jax.ShapeDtypeStruct((BLK, LANES), F32), vshape, vshape],
                 scratch_shapes=[pltpu.VMEM((BLK, SGU_CH), F32)],
                 compiler_params=_params(1))(p, p, dmix, g, b, w, bt)


def _place():
    x, y, c = lax.axis_index("x"), lax.axis_index("y"), lax.axis_index("c")
    chips = [(1 - x, y), (x, 1 - y), (1 - x, 1 - y)]
    return x, y, c, chips


def _hbm_specs(n):
    return [pl.BlockSpec(memory_space=pltpu.HBM)] * n


def _comm_params():
    return pltpu.CompilerParams(has_side_effects=True)


def _remote(src, dst, send_sem, recv_sem, to):
    return pltpu.make_async_remote_copy(src_ref=src, dst_ref=dst, send_sem=send_sem, recv_sem=recv_sem,
                                        device_id=to, device_id_type=MESH_ID)


def _all_gather_weights(shards):
    nt = len(shards)

    def body(*refs):
        ins, outs = refs[:nt], refs[nt:2 * nt]
        ici_send, ici_recv, d2d_send, d2d_recv, local_sem = refs[2 * nt:]
        x, y, c, chips = _place()
        me = 2 * x + y
        sibling = (x, y, 1 - c)
        local = [pltpu.make_async_copy(ins[t].at[l], outs[t].at[l, me], local_sem.at[2 * t + l])
                 for t in range(nt) for l in range(2)]
        for cp in local:
            cp.start()
        sends = []
        for t in range(nt):
            for j, (px, py) in enumerate(chips):
                sends.append(_remote(ins[t].at[c], outs[t].at[c, me], ici_send.at[3 * t + j], ici_recv.at[3 * t + j],
                                     (px, py, c)))
        for cp in sends:
            cp.start()
        for t in range(nt):
            for j, (px, py) in enumerate(chips):
                slab = outs[t].at[c, 2 * px + py]
                _remote(slab, slab, ici_send.at[3 * t + j], ici_recv.at[3 * t + j], (px, py, c)).wait_recv()
                fwd = _remote(slab, slab, d2d_send.at[3 * t + j], d2d_recv.at[3 * t + j], sibling)
                fwd.start()
                sends.append(fwd)
        for t in range(nt):
            for j, (px, py) in enumerate(chips):
                slab = outs[t].at[1 - c, 2 * px + py]
                _remote(slab, slab, d2d_send.at[3 * t + j], d2d_recv.at[3 * t + j], sibling).wait_recv()
        for cp in sends:
            cp.wait_send()
        for cp in local:
            cp.wait()

    out_shape = [jax.ShapeDtypeStruct((2, N_CHIPS) + s.shape[1:], s.dtype) for s in shards]
    sems = [pltpu.SemaphoreType.DMA((3 * nt,))] * 4 + [pltpu.SemaphoreType.DMA((2 * nt,))]
    return _call(body, name="all_gather_weights", in_specs=_hbm_specs(nt), out_specs=_hbm_specs(nt),
                 out_shape=out_shape, scratch_shapes=sems, compiler_params=_comm_params())(*shards)


def _sibling_exchange(stacks):
    nt = len(stacks)

    def body(*refs):
        ins, outs = refs[:nt], refs[nt:2 * nt]
        send_sem, recv_sem = refs[2 * nt:]
        x, y, c, _ = _place()
        cps = [_remote(ins[t].at[1 - c], outs[t], send_sem.at[t], recv_sem.at[t], (x, y, 1 - c)) for t in range(nt)]
        for cp in cps:
            cp.start()
        for cp in cps:
            cp.wait()

    out_shape = [jax.ShapeDtypeStruct(s.shape[1:], s.dtype) for s in stacks]
    sems = [pltpu.SemaphoreType.DMA((nt,))] * 2
    return _call(body, name="sibling_exchange", in_specs=_hbm_specs(nt), out_specs=_hbm_specs(nt),
                 out_shape=out_shape, scratch_shapes=sems, compiler_params=_comm_params())(*stacks)


def _chip_exchange(partials):
    nt = len(partials)

    def body(*refs):
        ins, outs = refs[:nt], refs[nt:2 * nt]
        send_sem, recv_sem = refs[2 * nt:]
        x, y, c, chips = _place()
        cps = []
        for t in range(nt):
            for j, (px, py) in enumerate(chips):
                cps.append(_remote(ins[t].at[2 * px + py], outs[t].at[j], send_sem.at[3 * t + j], recv_sem.at[3 * t + j],
                                   (px, py, c)))
        for cp in cps:
            cp.start()
        for cp in cps:
            cp.wait()

    out_shape = [jax.ShapeDtypeStruct((3,) + s.shape[1:], s.dtype) for s in partials]
    sems = [pltpu.SemaphoreType.DMA((3 * nt,))] * 2
    return _call(body, name="chip_exchange", in_specs=_hbm_specs(nt), out_specs=_hbm_specs(nt),
                 out_shape=out_shape, scratch_shapes=sems, compiler_params=_comm_params())(*partials)


def _sibling_gather(finals):
    nt = len(finals)

    def body(*refs):
        ins, outs = refs[:nt], refs[nt:2 * nt]
        send_sem, recv_sem, local_sem = refs[2 * nt:]
        x, y, c, _ = _place()
        local = [pltpu.make_async_copy(ins[t], outs[t].at[c], local_sem.at[t]) for t in range(nt)]
        cps = [_remote(ins[t], outs[t].at[c], send_sem.at[t], recv_sem.at[t], (x, y, 1 - c)) for t in range(nt)]
        for cp in local + cps:
            cp.start()
        for t in range(nt):
            theirs = outs[t].at[1 - c]
            _remote(theirs, theirs, send_sem.at[t], recv_sem.at[t], (x, y, 1 - c)).wait_recv()
        for cp in cps:
            cp.wait_send()
        for cp in local:
            cp.wait()

    out_shape = [jax.ShapeDtypeStruct((2,) + s.shape, s.dtype) for s in finals]
    sems = [pltpu.SemaphoreType.DMA((nt,))] * 3
    return _call(body, name="sibling_gather", in_specs=_hbm_specs(nt), out_specs=_hbm_specs(nt),
                 out_shape=out_shape, scratch_shapes=sems, compiler_params=_comm_params())(*finals)


def _all_reduce_small(buf):
    rows = buf.shape[0]

    def body(x_ref, out_ref, all_ref, send_sems, recv_sems, local_sem):
        x, y, c, chips = _place()
        me, sibling = (x, y, c), (x, y, 1 - c)

        def block(px, py, pc):
            return all_ref.at[pl.ds((4 * px + 2 * py + pc) * rows, rows), :]

        def copy(k, blk, to, src=None):
            return _remote(block(*blk) if src is None else src, block(*blk), send_sems.at[k], recv_sems.at[k], to)

        mine = pltpu.make_async_copy(x_ref, block(*me), local_sem)
        mine.start()
        first = [copy(0, me, sibling, src=x_ref)]
        first += [copy(1 + j, me, (*chip, c), src=x_ref) for j, chip in enumerate(chips)]
        for cp in first:
            cp.start()
        passed = [copy(4 + j, (*chip, c), sibling) for j, chip in enumerate(chips)]
        for j, chip in enumerate(chips):
            copy(1 + j, (*chip, c), me).wait_recv()
            passed[j].start()
        copy(0, sibling, me).wait_recv()
        for j, chip in enumerate(chips):
            copy(4 + j, (*chip, 1 - c), me).wait_recv()
        for cp in first + passed:
            cp.wait_send()
        mine.wait()
        tot = all_ref[0:rows, :]
        for k in range(1, N_DEV):
            tot = tot + all_ref[k * rows:(k + 1) * rows, :]
        out_ref[...] = tot

    vm = pl.BlockSpec(memory_space=pltpu.VMEM)
    return _call(body, name="all_reduce_small", in_specs=[vm], out_specs=vm,
                 out_shape=jax.ShapeDtypeStruct(buf.shape, F32),
                 scratch_shapes=[pltpu.VMEM((N_DEV * rows, LANES), F32), pltpu.SemaphoreType.DMA((7,)),
                                 pltpu.SemaphoreType.DMA((7,)), pltpu.SemaphoreType.DMA],
                 compiler_params=pltpu.CompilerParams(has_side_effects=True,
                                                      vmem_limit_bytes=V7X_VMEM_LIMIT_BYTES))(buf)


ROW_TILE = 128


def _chip_partial(stack, received, c):
    _, _, rows, cols = stack.shape

    def body(c_ref, a_ref, b_ref, o_ref):
        o_ref[...] = (a_ref[...].astype(F32) + b_ref[...].astype(F32)).astype(BF16)

    grid_spec = pltpu.PrefetchScalarGridSpec(
        num_scalar_prefetch=1, grid=(N_CHIPS, rows // ROW_TILE),
        in_specs=[pl.BlockSpec((None, None, ROW_TILE, cols), lambda s, i, c_ref: (c_ref[0], s, i, 0)),
                  pl.BlockSpec((None, ROW_TILE, cols), lambda s, i, c_ref: (s, i, 0))],
        out_specs=pl.BlockSpec((None, ROW_TILE, cols), lambda s, i, c_ref: (s, i, 0)))
    return _call(body, name="chip_partial", grid_spec=grid_spec, out_shape=jax.ShapeDtypeStruct(received.shape, BF16),
                 compiler_params=_params(2))(c, stack, received)


def _final_sum(stack, received, from_chips, place):
    _, _, rows, cols = stack.shape

    def body(place_ref, a_ref, b_ref, r_ref, o_ref):
        tot = a_ref[...].astype(F32) + b_ref[...].astype(F32)
        for j in range(3):
            tot = tot + r_ref[j].astype(F32)
        o_ref[...] = tot

    grid_spec = pltpu.PrefetchScalarGridSpec(
        num_scalar_prefetch=1, grid=(rows // ROW_TILE,),
        in_specs=[pl.BlockSpec((None, None, ROW_TILE, cols), lambda i, pr: (pr[0], pr[1], i, 0)),
                  pl.BlockSpec((None, ROW_TILE, cols), lambda i, pr: (pr[1], i, 0)),
                  pl.BlockSpec((3, ROW_TILE, cols), lambda i, pr: (0, i, 0))],
        out_specs=pl.BlockSpec((ROW_TILE, cols), lambda i, pr: (i, 0)))
    return _call(body, name="final_sum", grid_spec=grid_spec, out_shape=jax.ShapeDtypeStruct((rows, cols), F32),
                 compiler_params=_params(1))(place, stack, received, from_chips)


def _adamw(w, g, m, v):
    n, rows, cols = w.shape
    tr = _pick(rows, (ROW_TILE, SUBLANES))
    c1 = 1.0 - ADAM_B1 ** ADAM_STEP
    c2 = 1.0 - ADAM_B2 ** ADAM_STEP

    def body(w_ref, g_ref, m_ref, v_ref, d_ref, nm_ref, nv_ref):
        g_v = g_ref[...]
        nm = ADAM_B1 * m_ref[...] + (1.0 - ADAM_B1) * g_v
        nv = ADAM_B2 * v_ref[...] + (1.0 - ADAM_B2) * (g_v * g_v)
        nm_ref[...] = nm
        nv_ref[...] = nv
        d_ref[...] = -ADAM_LR * ((nm / c1) / (jnp.sqrt(nv / c2) + ADAM_EPS) + ADAM_WD * w_ref[...])

    blk = pl.BlockSpec((None, tr, cols), lambda l, i: (l, i, 0))
    shape = jax.ShapeDtypeStruct(w.shape, F32)
    return _call(body, name="adamw", grid=(n, rows // tr), in_specs=[blk] * 4, out_specs=[blk] * 3,
                 out_shape=[shape] * 3, compiler_params=_params(2))(w, g, m, v)


def _to_heads(a, n_heads):
    t_len = a.shape[0]
    return a.reshape(t_len, n_heads, HEAD_DIM).transpose(1, 0, 2)


def _from_heads(a):
    n_heads, t_len, _ = a.shape
    return a.transpose(1, 0, 2).reshape(t_len, n_heads * HEAD_DIM)


def _ffn_fwd(x, gain, w_in_g, w_out_g, layer):
    h = _rmsnorm_fwd(x, gain)
    gu, act = _mm_ffn_in(h, w_in_g, layer)
    x_new = _mm_out_res("ffn_out", act, w_out_g, layer, x, FFN_RESIDUAL_WEIGHT)[0]
    return x_new, (x, h, gu, act)


def _ffn_bwd(dx, dxb, saved, gain, w_in_g, w_out_g, layer, stack_in, stack_out):
    x, h, gu, act = saved
    dgu = _mm_dact_swiglu(dxb, w_out_g, layer, gu, FFN_RESIDUAL_WEIGHT)
    stack_out = _dw_rows("ffn_dw_out", act, dxb, layer, stack_out, FFN_RESIDUAL_WEIGHT)
    dh = _mm_dh_ffn(dgu, w_in_g, layer)
    stack_in = _dw_ffn_in(h, dgu, layer, stack_in)
    dx_in, dxb_in, dgain = _rmsnorm_bwd(dh, x, gain, dx)
    return dx_in, dxb_in, dgain, stack_in, stack_out


def _mix_fwd(x, gain, w_in_g, w_out_g, layer, small, tables):
    h = _rmsnorm_fwd(x, gain)
    p = _mm_proj(h, w_in_g, layer)[0]
    qkv = _rope_fwd(p, tables)
    q = _to_heads(qkv[:, :Q_END], N_Q_HEADS)
    k = _to_heads(qkv[:, Q_END:K_END], N_KV_HEADS)
    v = _to_heads(qkv[:, K_END:V_END], N_KV_HEADS)
    attn = _from_heads(_attn_fwd(q, k, v, small["snk"]))
    y = _conv_fwd(p, small["conv_w"], small["conv_b"])
    conv = _conv_post_fwd(y, small["conv_ln_g"], small["conv_ln_b"])
    sgu = _sgu_fwd(p, small["sgu_ln_g"], small["sgu_ln_b"], small["sgu_w"], small["sgu_bt"])
    mix = jnp.concatenate([attn, conv, sgu], axis=1)
    x_new = _mm_out_res("mix_out", mix, w_out_g, layer, x, 1.0)[0]
    return x_new, (x, h, p, q, k, v, y, mix)


def _mix_bwd(dx, dxb, saved, gain, w_in_g, w_out_g, small, tables, layer, stack_in, stack_out):
    x, h, p, q, k, v, y, mix = saved
    dmix = _mm_dmix(dxb, w_out_g, layer)
    stack_out = _dw_rows("mix_dw_out", mix, dxb, layer, stack_out, 1.0)
    do = _to_heads(dmix[:, :Q_END], N_Q_HEADS)
    dq, dkp, dkc, dvp, dvc, dsnk = _attn_bwd(q, k, v, small["snk"], do)
    dy, d_ln_g, d_ln_b, d_conv_b = _conv_post_bwd(dmix, y, small["conv_ln_g"], small["conv_ln_b"])
    dalin, dagate, d_conv_w = _conv_bwd(p, dy, small["conv_w"])
    du, dvin, d_sgu_w, d_sgu_bt, d_sgu_g, d_sgu_b = _sgu_bwd(p, dmix, small["sgu_ln_g"], small["sgu_ln_b"],
                                                           small["sgu_w"], small["sgu_bt"])
    dp = _assemble_dp(_from_heads(dq), _from_heads(dkc), _from_heads(dkp), _from_heads(dvc), _from_heads(dvp),
                      tables, dalin, dagate, du, dvin)
    dh = _mm_dh_mix(dp, w_in_g, layer)
    stack_in = _dw_mix_in(h, dp, layer, stack_in)
    dx_in, dxb_in, dgain = _rmsnorm_bwd(dh, x, gain, dx)
    grads = {
        "norm_mix": dgain[0], "conv_dw_w": d_conv_w[:CONV_WIDTH], "conv_dw_b": d_conv_b[0],
        "conv_ln_g": d_ln_g[0], "conv_ln_b": d_ln_b[0], "sgu_ln_g": d_sgu_g[0], "sgu_ln_b": d_sgu_b[0],
        "sgu_w": d_sgu_w, "sgu_b": d_sgu_bt[:, :SGU_HEADS].T, "attn_sinks": dsnk[:, :GQ, 0].reshape(N_Q_HEADS),
    }
    return dx_in, dxb_in, grads, stack_in, stack_out


BIG = ("ffn1_w_in", "ffn1_w_out", "w_in", "w_out", "ffn2_w_in", "ffn2_w_out")
SMALL = ("norm_ffn1", "norm_mix", "conv_dw_w", "conv_dw_b", "conv_ln_g", "conv_ln_b", "sgu_ln_g", "sgu_ln_b",
         "sgu_w", "sgu_b", "attn_sinks", "norm_ffn2", "final_norm")
WEIGHTS = ("norm_ffn1", "ffn1_w_in", "ffn1_w_out", "norm_mix", "w_in", "conv_dw_w", "conv_dw_b", "conv_ln_g",
           "conv_ln_b", "sgu_ln_g", "sgu_ln_b", "sgu_w", "sgu_b", "attn_sinks", "w_out", "norm_ffn2", "ffn2_w_in",
           "ffn2_w_out", "final_norm")
PACK_ROWS = SUBLANES * LANES


def _pack(arrays):
    flat = jnp.concatenate([a.reshape(-1).astype(F32) for a in arrays])
    pad = (-flat.shape[0]) % PACK_ROWS
    return jnp.pad(flat, (0, pad)).reshape(-1, LANES)


def _unpack(buf, shapes):
    flat = buf.reshape(-1)
    out, off = [], 0
    for s in shapes:
        n = 1
        for d in s:
            n *= d
        out.append(flat[off:off + n].reshape(s))
        off += n
    return out


def kernel(x, positions, norm_ffn1, ffn1_w_in, ffn1_w_out, norm_mix, w_in, conv_dw_w, conv_dw_b, conv_ln_g, conv_ln_b, sgu_ln_g, sgu_ln_b, sgu_w, sgu_b, attn_sinks, w_out, norm_ffn2, ffn2_w_in, ffn2_w_out, final_norm, loss_target, m_norm_ffn1, m_ffn1_w_in, m_ffn1_w_out, m_norm_mix, m_w_in, m_conv_dw_w, m_conv_dw_b, m_conv_ln_g, m_conv_ln_b, m_sgu_ln_g, m_sgu_ln_b, m_sgu_w, m_sgu_b, m_attn_sinks, m_w_out, m_norm_ffn2, m_ffn2_w_in, m_ffn2_w_out, m_final_norm, v_norm_ffn1, v_ffn1_w_in, v_ffn1_w_out, v_norm_mix, v_w_in, v_conv_dw_w, v_conv_dw_b, v_conv_ln_g, v_conv_ln_b, v_sgu_ln_g, v_sgu_ln_b, v_sgu_w, v_sgu_b, v_attn_sinks, v_w_out, v_norm_ffn2, v_ffn2_w_in, v_ffn2_w_out, v_final_norm):
    w = dict(norm_ffn1=norm_ffn1, ffn1_w_in=ffn1_w_in, ffn1_w_out=ffn1_w_out, norm_mix=norm_mix, w_in=w_in,
             conv_dw_w=conv_dw_w, conv_dw_b=conv_dw_b, conv_ln_g=conv_ln_g, conv_ln_b=conv_ln_b, sgu_ln_g=sgu_ln_g,
             sgu_ln_b=sgu_ln_b, sgu_w=sgu_w, sgu_b=sgu_b, attn_sinks=attn_sinks, w_out=w_out, norm_ffn2=norm_ffn2,
             ffn2_w_in=ffn2_w_in, ffn2_w_out=ffn2_w_out, final_norm=final_norm)
    m = dict(norm_ffn1=m_norm_ffn1, ffn1_w_in=m_ffn1_w_in, ffn1_w_out=m_ffn1_w_out, norm_mix=m_norm_mix, w_in=m_w_in,
             conv_dw_w=m_conv_dw_w, conv_dw_b=m_conv_dw_b, conv_ln_g=m_conv_ln_g, conv_ln_b=m_conv_ln_b,
             sgu_ln_g=m_sgu_ln_g, sgu_ln_b=m_sgu_ln_b, sgu_w=m_sgu_w, sgu_b=m_sgu_b, attn_sinks=m_attn_sinks,
             w_out=m_w_out, norm_ffn2=m_norm_ffn2, ffn2_w_in=m_ffn2_w_in, ffn2_w_out=m_ffn2_w_out,
             final_norm=m_final_norm)
    v = dict(norm_ffn1=v_norm_ffn1, ffn1_w_in=v_ffn1_w_in, ffn1_w_out=v_ffn1_w_out, norm_mix=v_norm_mix, w_in=v_w_in,
             conv_dw_w=v_conv_dw_w, conv_dw_b=v_conv_dw_b, conv_ln_g=v_conv_ln_g, conv_ln_b=v_conv_ln_b,
             sgu_ln_g=v_sgu_ln_g, sgu_ln_b=v_sgu_ln_b, sgu_w=v_sgu_w, sgu_b=v_sgu_b, attn_sinks=v_attn_sinks,
             w_out=v_w_out, norm_ffn2=v_norm_ffn2, ffn2_w_in=v_ffn2_w_in, ffn2_w_out=v_ffn2_w_out,
             final_norm=v_final_norm)
    depth = norm_ffn1.shape[0]
    assert depth == 2 and x.shape[0] == 1
    xc = lax.axis_index("x")
    yc = lax.axis_index("y")
    cc = lax.axis_index("c")
    chip = 2 * xc + yc

    gathered = _all_gather_weights([w[n].astype(BF16) for n in BIG] + [conv_dw_w])
    wg = dict(zip(BIG, gathered[:len(BIG)]))
    conv_w_full = gathered[-1].transpose(0, 2, 1, 3).reshape(depth, CONV_WIDTH, CONV_CH)
    conv_w_full = jnp.pad(conv_w_full, ((0, 0), (0, CONV_PAD - CONV_WIDTH), (0, 0)))

    tables = _rope_tables(positions)
    small = []
    for l in range(depth):
        small.append(dict(
            snk=jnp.broadcast_to(attn_sinks[l].reshape(N_KV_HEADS, GQ, 1, 1), (N_KV_HEADS, GQ, BLK, 1)).reshape(
                N_KV_HEADS, GQ * BLK, 1),
            conv_w=conv_w_full[l], conv_b=conv_dw_b[l][None], conv_ln_g=conv_ln_g[l][None],
            conv_ln_b=conv_ln_b[l][None], sgu_ln_g=sgu_ln_g[l][None], sgu_ln_b=sgu_ln_b[l][None], sgu_w=sgu_w[l],
            sgu_bt=sgu_b[l].T))

    xs = x[0]
    saved = []
    for l in range(depth):
        xs, s1 = _ffn_fwd(xs, norm_ffn1[l][None], wg["ffn1_w_in"], wg["ffn1_w_out"], l)
        xs, s2 = _mix_fwd(xs, norm_mix[l][None], wg["w_in"], wg["w_out"], l, small[l], tables)
        xs, s3 = _ffn_fwd(xs, norm_ffn2[l][None], wg["ffn2_w_in"], wg["ffn2_w_out"], l)
        saved.append((s1, s2, s3))
    dx, dxb, d_final, loss_part = _loss_head(xs, final_norm[None], loss_target[0])

    stacks = {n: None for n in BIG}
    small_grads = [None] * depth
    for l in reversed(range(depth)):
        s1, s2, s3 = saved[l]
        dx, dxb, dg2, stacks["ffn2_w_in"], stacks["ffn2_w_out"] = _ffn_bwd(
            dx, dxb, s3, norm_ffn2[l][None], wg["ffn2_w_in"], wg["ffn2_w_out"], l, stacks["ffn2_w_in"],
            stacks["ffn2_w_out"])
        dx, dxb, gm, stacks["w_in"], stacks["w_out"] = _mix_bwd(
            dx, dxb, s2, norm_mix[l][None], wg["w_in"], wg["w_out"], small[l], tables, l, stacks["w_in"],
            stacks["w_out"])
        dx, dxb, dg1, stacks["ffn1_w_in"], stacks["ffn1_w_out"] = _ffn_bwd(
            dx, dxb, s1, norm_ffn1[l][None], wg["ffn1_w_in"], wg["ffn1_w_out"], l, stacks["ffn1_w_in"],
            stacks["ffn1_w_out"])
        gm["norm_ffn1"] = dg1[0]
        gm["norm_ffn2"] = dg2[0]
        small_grads[l] = gm
    grad_x = dx[None]

    stack_list = [stacks[n] for n in BIG]
    from_sibling = _sibling_exchange(stack_list)
    c_arr = cc.reshape(1).astype(jnp.int32)
    partials = [_chip_partial(s, r, c_arr) for s, r in zip(stack_list, from_sibling)]
    from_chips = _chip_exchange(partials)
    place = jnp.stack([cc, chip]).astype(jnp.int32)
    finals = [_final_sum(s, r, f, place) for s, r, f in zip(stack_list, from_sibling, from_chips)]
    big_grads = dict(zip(BIG, _sibling_gather(finals)))

    per_layer = [n for n in SMALL if n != "final_norm"]
    small_local = [jnp.stack([small_grads[l][n] for l in range(depth)]) for n in per_layer]
    small_local += [d_final[0], loss_part[0, :1]]
    small_shapes = [a.shape for a in small_local]
    summed = _unpack(_all_reduce_small(_pack(small_local)), small_shapes)
    loss = summed[-1][0]
    sg = dict(zip(per_layer + ["final_norm"], summed[:-1]))
    sg["conv_dw_w"] = lax.dynamic_slice_in_dim(sg["conv_dw_w"], chip * LANES, LANES, axis=2)

    delta, new_m, new_v = {}, {}, {}
    for n in BIG:
        delta[n], new_m[n], new_v[n] = _adamw(w[n], big_grads[n], m[n], v[n])
    shapes = [w[n].shape for n in SMALL]
    packed = [_pack([d[n] for n in SMALL])[None] for d in (w, sg, m, v)]
    outs = _adamw(*packed)
    for d, buf in zip((delta, new_m, new_v), outs):
        d.update(zip(SMALL, _unpack(buf[0], shapes)))
    grads = {**big_grads, **sg}
    return (loss, grad_x, *[grads[n] for n in WEIGHTS], *[delta[n] for n in WEIGHTS],
            *[new_m[n] for n in WEIGHTS], *[new_v[n] for n in WEIGHTS])
```

```python
import functools

import jax
import jax.numpy as jnp
from jax import lax
from jax.experimental import pallas as pl
from jax.experimental.pallas import tpu as pltpu

F32 = jnp.float32
BF16 = jnp.bfloat16
MESH_ID = pl.DeviceIdType.MESH

V7X_VMEM_LIMIT_BYTES = 56 * 2**20
LANES = 128
SUBLANES = 8

HEAD_DIM = 64
N_Q_HEADS = 16
N_KV_HEADS = 4
GQ = N_Q_HEADS // N_KV_HEADS
BLK = 128
ROT_HALF = 8
ROPE_THETA = 500000.0
CONV_WIDTH = 31
CONV_PAD = 32
CONV_CH = 512
SGU_CH = 512
SGU_HEADS = 8
Q_END = N_Q_HEADS * HEAD_DIM
K_END = Q_END + N_KV_HEADS * HEAD_DIM
V_END = K_END + N_KV_HEADS * HEAD_DIM
CONV_END = V_END + 2 * CONV_CH
IN_COLS = CONV_END + 2 * SGU_CH
NORM_EPS = 1e-5
FFN_RESIDUAL_WEIGHT = 0.5
N_CHIPS = 4
N_DEV = 8

ADAM_LR = 0.001
ADAM_B1 = 0.9
ADAM_B2 = 0.999
ADAM_EPS = 1e-08
ADAM_WD = 0.01
ADAM_STEP = 10

NN = (((1,), (0,)), ((), ()))
NT = (((1,), (1,)), ((), ()))
TN = (((0,), (0,)), ((), ()))


def _pick(n, cands):
    for c in cands:
        if n % c == 0:
            return c
    raise ValueError(f"no tile of {cands} divides {n}")


def _params(n_axes):
    return pltpu.CompilerParams(dimension_semantics=("arbitrary",) * n_axes, vmem_limit_bytes=V7X_VMEM_LIMIT_BYTES)


def _call(body, **kw):
    return pl.pallas_call(body, **kw)


def _sigmoid(x):
    return 1.0 / (1.0 + jnp.exp(-x))


def _matmul(name, grid, a_ops, b_ops, terms, dims, out_shape, out_specs, epilogue, extra_ops=(), nk=1,
            acc_shapes=(), alias=None):
    na, nb, ne, no = len(a_ops), len(b_ops), len(extra_ops), len(out_shape)

    def body(*refs):
        a = refs[:na]
        b = refs[na:na + nb]
        e = refs[na + nb:na + nb + ne]
        first_out = na + nb + ne + (1 if alias is not None else 0)
        o = refs[first_out:first_out + no]
        accs = refs[first_out + no:]

        def partial(t):
            tot = None
            for ai, bi in t:
                d = lax.dot_general(a[ai][...], b[bi][...], dims, preferred_element_type=F32)
                tot = d if tot is None else tot + d
            return tot

        if nk == 1:
            epilogue([partial(t) for t in terms], e, o)
        else:
            k = pl.program_id(len(grid) - 1)

            @pl.when(k == 0)
            def _():
                for acc in accs:
                    acc[...] = jnp.zeros(acc.shape, F32)

            for acc, t in zip(accs, terms):
                acc[...] += partial(t)

            @pl.when(k == nk - 1)
            def _():
                epilogue([acc[...] for acc in accs], e, o)

    ops = list(a_ops) + list(b_ops) + list(extra_ops)
    arrays = [x for x, _ in ops]
    in_specs = [s for _, s in ops]
    kw = {}
    if alias is not None:
        arrays.append(alias[0])
        in_specs.append(pl.BlockSpec(memory_space=pl.ANY))
        kw["input_output_aliases"] = {len(arrays) - 1: alias[1]}
    scratch = [pltpu.VMEM(s, F32) for s in acc_shapes] if nk > 1 else []
    return _call(body, name=name, grid=grid, in_specs=in_specs, out_specs=out_specs, out_shape=out_shape,
                 scratch_shapes=scratch, compiler_params=_params(len(grid)), **kw)(*arrays)


def _mm_ffn_in(h, w_g, layer):
    t_len, d = h.shape
    fs = w_g.shape[3]
    f = 2 * fs
    tm = _pick(t_len, (2048, 1024, 512))
    tn = _pick(fs, (256, 128))
    nj = fs // tn

    def epilogue(accs, e, o):
        g, u = accs
        o[0][0] = g.astype(BF16)
        o[0][1] = u.astype(BF16)
        o[1][...] = (g * _sigmoid(g) * u).astype(BF16)

    return _matmul(
        "ffn_in", (t_len // tm, 2, nj),
        [(h, pl.BlockSpec((tm, d), lambda i, s, j: (i, 0)))],
        [(w_g, pl.BlockSpec((None, None, d, tn), lambda i, s, j: (layer, s, 0, j))),
         (w_g, pl.BlockSpec((None, None, d, tn), lambda i, s, j: (layer, s + 2, 0, j)))],
        [[(0, 0)], [(0, 1)]], NN,
        [jax.ShapeDtypeStruct((2, t_len, f), BF16), jax.ShapeDtypeStruct((t_len, f), BF16)],
        [pl.BlockSpec((2, tm, tn), lambda i, s, j: (0, i, s * nj + j)),
         pl.BlockSpec((tm, tn), lambda i, s, j: (i, s * nj + j))],
        epilogue)


def _mm_out_res(name, a, w_g, layer, x, scale):
    t_len = a.shape[0]
    ks, n = w_g.shape[2], w_g.shape[3]
    tm = _pick(t_len, (1024, 512))
    tn = _pick(n, (1024,))
    tk = _pick(ks, (1408, 512, 256))
    nks = ks // tk

    def epilogue(accs, e, o):
        o[0][...] = e[0][...] + scale * accs[0]

    return _matmul(
        name, (t_len // tm, n // tn, N_CHIPS * nks),
        [(a, pl.BlockSpec((tm, tk), lambda i, j, k: (i, k)))],
        [(w_g, pl.BlockSpec((None, None, tk, tn), lambda i, j, k: (layer, k // nks, k % nks, j)))],
        [[(0, 0)]], NN,
        [jax.ShapeDtypeStruct((t_len, n), F32)],
        [pl.BlockSpec((tm, tn), lambda i, j, k: (i, j))],
        epilogue, extra_ops=[(x, pl.BlockSpec((tm, tn), lambda i, j, k: (i, j)))],
        nk=N_CHIPS * nks, acc_shapes=[(tm, tn)])


def _mm_proj(h, w_g, layer):
    t_len, d = h.shape
    cs = w_g.shape[3]
    tm = _pick(t_len, (1024, 512))

    def epilogue(accs, e, o):
        o[0][...] = accs[0]

    return _matmul(
        "mix_in", (t_len // tm, N_CHIPS),
        [(h, pl.BlockSpec((tm, d), lambda i, s: (i, 0)))],
        [(w_g, pl.BlockSpec((None, None, d, cs), lambda i, s: (layer, s, 0, 0)))],
        [[(0, 0)]], NN,
        [jax.ShapeDtypeStruct((t_len, N_CHIPS * cs), F32)],
        [pl.BlockSpec((tm, cs), lambda i, s: (i, s))],
        epilogue)


def _mm_dact_swiglu(dxb, w_g, layer, gu, scale):
    t_len, d = dxb.shape
    rs = w_g.shape[2]
    tm = _pick(t_len, (512,))
    tn = _pick(rs, (1408, 256, 128))
    nj = rs // tn

    def epilogue(accs, e, o):
        dact = scale * accs[0]
        g = e[0][0].astype(F32)
        u = e[0][1].astype(F32)
        sig = _sigmoid(g)
        o[0][0] = (dact * u * (sig * (1.0 + g * (1.0 - sig)))).astype(BF16)
        o[0][1] = (dact * (g * sig)).astype(BF16)

    gu_spec = pl.BlockSpec((2, tm, tn), lambda i, s, j: (0, i, s * nj + j))
    return _matmul(
        "ffn_dact", (t_len // tm, N_CHIPS, nj),
        [(dxb, pl.BlockSpec((tm, d), lambda i, s, j: (i, 0)))],
        [(w_g, pl.BlockSpec((None, None, tn, d), lambda i, s, j: (layer, s, j, 0)))],
        [[(0, 0)]], NT,
        [jax.ShapeDtypeStruct(gu.shape, BF16)], [gu_spec],
        epilogue, extra_ops=[(gu, gu_spec)])[0]


def _mm_dh_ffn(dgu, w_g, layer):
    t_len = dgu.shape[1]
    d, fs = w_g.shape[2], w_g.shape[3]
    tm = _pick(t_len, (1024, 512))
    tk = _pick(fs, (256, 128))
    nks = fs // tk
    nk = 2 * nks

    def epilogue(accs, e, o):
        o[0][...] = accs[0]

    return _matmul(
        "ffn_dh", (t_len // tm, nk),
        [(dgu, pl.BlockSpec((None, tm, tk), lambda i, k: (0, i, k))),
         (dgu, pl.BlockSpec((None, tm, tk), lambda i, k: (1, i, k)))],
        [(w_g, pl.BlockSpec((None, None, d, tk), lambda i, k: (layer, k // nks, 0, k % nks))),
         (w_g, pl.BlockSpec((None, None, d, tk), lambda i, k: (layer, k // nks + 2, 0, k % nks)))],
        [[(0, 0), (1, 1)]], NT,
        [jax.ShapeDtypeStruct((t_len, d), F32)],
        [pl.BlockSpec((tm, d), lambda i, k: (i, 0))],
        epilogue, nk=nk, acc_shapes=[(tm, d)])[0]


def _mm_dw(name, a, a_spec_of, b, b_spec_of, layer, stack, rows, cols, tn, scale):
    t_len = a.shape[0]
    tt = _pick(t_len, (1024, 512))
    nj = cols // tn

    def epilogue(accs, e, o):
        o[0][...] = (scale * accs[0]).astype(BF16)

    shape = jax.ShapeDtypeStruct((2, N_CHIPS, rows, cols), BF16)
    return _matmul(
        name, (N_CHIPS, nj, t_len // tt),
        [(a, a_spec_of(tt))], [(b, b_spec_of(tt, tn, nj))],
        [[(0, 0)]], TN, [shape],
        [pl.BlockSpec((None, None, rows, tn), lambda s, j, t: (layer, s, 0, j))],
        epilogue, nk=t_len // tt, acc_shapes=[(rows, tn)],
        alias=None if stack is None else (stack, 0))[0]


def _dw_ffn_in(h, dgu, layer, stack):
    d = h.shape[1]
    fs = dgu.shape[2] // 2
    tn = _pick(fs, (1408, 256))
    return _mm_dw(
        "ffn_dw_in", h, lambda tt: pl.BlockSpec((tt, d), lambda s, j, t: (t, 0)),
        dgu, lambda tt, tn_, nj: pl.BlockSpec((None, tt, tn_), lambda s, j, t: (s // 2, t, (s % 2) * nj + j)),
        layer, stack, d, fs, tn, 1.0)


def _dw_rows(name, a, dxb, layer, stack, scale):
    rs = a.shape[1] // N_CHIPS
    d = dxb.shape[1]
    tn = _pick(d, (1024,))
    return _mm_dw(
        name, a, lambda tt: pl.BlockSpec((tt, rs), lambda s, j, t: (t, s)),
        dxb, lambda tt, tn_, nj: pl.BlockSpec((tt, tn_), lambda s, j, t: (t, j)),
        layer, stack, rs, d, tn, scale)


def _dw_mix_in(h, dp, layer, stack):
    d = h.shape[1]
    cs = dp.shape[1] // N_CHIPS
    return _mm_dw(
        "mix_dw_in", h, lambda tt: pl.BlockSpec((tt, d), lambda s, j, t: (t, 0)),
        dp, lambda tt, tn_, nj: pl.BlockSpec((tt, tn_), lambda s, j, t: (t, s)),
        layer, stack, d, cs, cs, 1.0)


def _mm_dmix(dxb, w_g, layer):
    t_len, d = dxb.shape
    rs = w_g.shape[2]
    tm = _pick(t_len, (1024, 512))

    def epilogue(accs, e, o):
        o[0][...] = accs[0]

    return _matmul(
        "mix_dout", (t_len // tm, N_CHIPS),
        [(dxb, pl.BlockSpec((tm, d), lambda i, s: (i, 0)))],
        [(w_g, pl.BlockSpec((None, None, rs, d), lambda i, s: (layer, s, 0, 0)))],
        [[(0, 0)]], NT,
        [jax.ShapeDtypeStruct((t_len, N_CHIPS * rs), F32)],
        [pl.BlockSpec((tm, rs), lambda i, s: (i, s))],
        epilogue)[0]


def _mm_dh_mix(dp, w_g, layer):
    t_len = dp.shape[0]
    d, cs = w_g.shape[2], w_g.shape[3]
    tm = _pick(t_len, (1024, 512))

    def epilogue(accs, e, o):
        o[0][...] = accs[0]

    return _matmul(
        "mix_dh", (t_len // tm, N_CHIPS),
        [(dp, pl.BlockSpec((tm, cs), lambda i, k: (i, k)))],
        [(w_g, pl.BlockSpec((None, None, d, cs), lambda i, k: (layer, k, 0, 0)))],
        [[(0, 0)]], NT,
        [jax.ShapeDtypeStruct((t_len, d), F32)],
        [pl.BlockSpec((tm, d), lambda i, k: (i, 0))],
        epilogue, nk=N_CHIPS, acc_shapes=[(tm, d)])[0]


def _rms_stats(x):
    r = lax.rsqrt(jnp.mean(x * x, axis=-1, keepdims=True) + NORM_EPS)
    return r, x * r


def _accumulate(ref, part, first):
    @pl.when(first)
    def _():
        ref[...] = part

    @pl.when(jnp.logical_not(first))
    def _():
        ref[...] += part


def _rmsnorm_fwd(x, g):
    t_len, d = x.shape
    tm = _pick(t_len, (512,))

    def body(x_ref, g_ref, h_ref):
        _, xhat = _rms_stats(x_ref[...])
        h_ref[...] = (xhat * g_ref[...]).astype(BF16)

    row = pl.BlockSpec((tm, d), lambda i: (i, 0))
    vec = pl.BlockSpec((1, d), lambda i: (0, 0))
    return _call(body, name="rmsnorm_fwd", grid=(t_len // tm,), in_specs=[row, vec], out_specs=row,
                 out_shape=jax.ShapeDtypeStruct((t_len, d), BF16), compiler_params=_params(1))(x, g)


def _rmsnorm_bwd(dh, x, g, dres):
    t_len, d = x.shape
    tm = _pick(t_len, (256,))

    def body(dh_ref, x_ref, g_ref, dres_ref, dx_ref, dxb_ref, dg_ref):
        r, xhat = _rms_stats(x_ref[...])
        dh_v = dh_ref[...]
        gd = dh_v * g_ref[...]
        dx = dres_ref[...] + r * (gd - xhat * jnp.mean(gd * xhat, axis=-1, keepdims=True))
        dx_ref[...] = dx
        dxb_ref[...] = dx.astype(BF16)
        _accumulate(dg_ref, jnp.sum(dh_v * xhat, axis=0, keepdims=True), pl.program_id(0) == 0)

    row = pl.BlockSpec((tm, d), lambda i: (i, 0))
    vec = pl.BlockSpec((1, d), lambda i: (0, 0))
    return _call(body, name="rmsnorm_bwd", grid=(t_len // tm,), in_specs=[row, row, vec, row],
                 out_specs=[row, row, vec],
                 out_shape=[jax.ShapeDtypeStruct((t_len, d), F32), jax.ShapeDtypeStruct((t_len, d), BF16),
                            jax.ShapeDtypeStruct((1, d), F32)],
                 compiler_params=_params(1))(dh, x, g, dres)


def _loss_head(x, g, target):
    t_len, d = x.shape
    tm = _pick(t_len, (256,))

    def body(x_ref, g_ref, t_ref, dx_ref, dxb_ref, dg_ref, loss_ref):
        first = pl.program_id(0) == 0
        r, xhat = _rms_stats(x_ref[...])
        g_v = g_ref[...]
        err = xhat * g_v - t_ref[...]
        per_token = jnp.mean(err * err, axis=-1, keepdims=True)
        part = 0.5 * jnp.sum(per_token, axis=0, keepdims=True)
        _accumulate(loss_ref, jnp.broadcast_to(part, (1, LANES)), first)
        dy = err * (1.0 / d)
        _accumulate(dg_ref, jnp.sum(dy * xhat, axis=0, keepdims=True), first)
        gd = dy * g_v
        dx = r * (gd - xhat * jnp.mean(gd * xhat, axis=-1, keepdims=True))
        dx_ref[...] = dx
        dxb_ref[...] = dx.astype(BF16)

    row = pl.BlockSpec((tm, d), lambda i: (i, 0))
    vec = pl.BlockSpec((1, d), lambda i: (0, 0))
    return _call(body, name="loss_head", grid=(t_len // tm,), in_specs=[row, vec, row],
                 out_specs=[row, row, vec, pl.BlockSpec((1, LANES), lambda i: (0, 0))],
                 out_shape=[jax.ShapeDtypeStruct((t_len, d), F32), jax.ShapeDtypeStruct((t_len, d), BF16),
                            jax.ShapeDtypeStruct((1, d), F32), jax.ShapeDtypeStruct((1, LANES), F32)],
                 compiler_params=_params(1))(x, g, target)


def _ln_stats(x):
    mu = jnp.mean(x, axis=-1, keepdims=True)
    xc = x - mu
    r = lax.rsqrt(jnp.mean(xc * xc, axis=-1, keepdims=True) + NORM_EPS)
    return r, xc * r


def _ln_bwd(dy, r, xhat, g):
    dxh = dy * g
    return r * (dxh - jnp.mean(dxh, axis=-1, keepdims=True) - xhat * jnp.mean(dxh * xhat, axis=-1, keepdims=True))


def _rope_tables(positions):
    t_len = positions.shape[-1]
    inv_freq = 1.0 / (ROPE_THETA ** (jnp.arange(0, 2 * ROT_HALF, 2, dtype=F32) / (2 * ROT_HALF)))
    ang = positions.astype(F32).reshape(t_len, 1) * inv_freq
    cos = jnp.tile(jnp.cos(ang), (1, LANES // ROT_HALF))
    sin = jnp.tile(jnp.sin(ang), (1, LANES // ROT_HALF))
    lane = jnp.arange(LANES) % HEAD_DIM
    c = jnp.where(lane < 2 * ROT_HALF, cos, 1.0)
    s1 = jnp.where(lane < ROT_HALF, -sin, 0.0)
    s2 = jnp.where((lane >= ROT_HALF) & (lane < 2 * ROT_HALF), sin, 0.0)
    return c.astype(F32), s1.astype(F32), s2.astype(F32)


def _rope_fwd(p, tables):
    t_len = p.shape[0]
    tm = _pick(t_len, (256,))
    n_rot = K_END // LANES

    def body(p_ref, c_ref, s1_ref, s2_ref, o_ref):
        c, s1, s2 = c_ref[...], s1_ref[...], s2_ref[...]
        for j in range(V_END // LANES):
            sl = slice(j * LANES, (j + 1) * LANES)
            t = p_ref[:, sl]
            if j < n_rot:
                t = t * c + pltpu.roll(t, LANES - ROT_HALF, 1) * s1 + pltpu.roll(t, ROT_HALF, 1) * s2
            o_ref[:, sl] = t.astype(BF16)

    tab = pl.BlockSpec((tm, LANES), lambda i: (i, 0))
    blk = pl.BlockSpec((tm, V_END), lambda i: (i, 0))
    return _call(body, name="rope_fwd", grid=(t_len // tm,), in_specs=[blk, tab, tab, tab], out_specs=blk,
                 out_shape=jax.ShapeDtypeStruct((t_len, V_END), BF16), compiler_params=_params(1))(p, *tables)


def _assemble_dp(dq, dkc, dkp, dvc, dvp, tables, dalin, dagate, du, dvin):
    t_len = dq.shape[0]
    nb = t_len // BLK

    def body(dq_ref, dkc_ref, dkp_ref, dvc_ref, dvp_ref, c_ref, s1_ref, s2_ref, dalin_ref, dagate_ref,
             du_ref, dvin_ref, o_ref):
        keep = (pl.program_id(0) < nb - 1).astype(F32)
        c, s1, s2 = c_ref[...], s1_ref[...], s2_ref[...]

        def unrotate(dr):
            return dr * c + pltpu.roll(dr * s1, ROT_HALF, 1) + pltpu.roll(dr * s2, LANES - ROT_HALF, 1)

        for j in range(Q_END // LANES):
            sl = slice(j * LANES, (j + 1) * LANES)
            o_ref[:, sl] = unrotate(dq_ref[:, sl]).astype(BF16)
        for j in range((K_END - Q_END) // LANES):
            sl = slice(j * LANES, (j + 1) * LANES)
            dk = dkc_ref[:, sl] + keep * dkp_ref[:, sl]
            o_ref[:, Q_END + j * LANES:Q_END + (j + 1) * LANES] = unrotate(dk).astype(BF16)
            dv = dvc_ref[:, sl] + keep * dvp_ref[:, sl]
            o_ref[:, K_END + j * LANES:K_END + (j + 1) * LANES] = dv.astype(BF16)
        o_ref[:, V_END:V_END + CONV_CH] = dalin_ref[...]
        o_ref[:, V_END + CONV_CH:CONV_END] = dagate_ref[...]
        o_ref[:, CONV_END:CONV_END + SGU_CH] = du_ref[...]
        o_ref[:, CONV_END + SGU_CH:IN_COLS] = dvin_ref[...]

    def cur(w):
        return pl.BlockSpec((BLK, w), lambda i: (i, 0))

    def nxt(w):
        return pl.BlockSpec((BLK, w), lambda i: (jnp.minimum(i + 1, nb - 1), 0))

    kvw = K_END - Q_END
    return _call(body, name="assemble_dp", grid=(nb,),
                 in_specs=[cur(Q_END), cur(kvw), nxt(kvw), cur(kvw), nxt(kvw), cur(LANES), cur(LANES), cur(LANES),
                           cur(CONV_CH), cur(CONV_CH), cur(SGU_CH), cur(SGU_CH)],
                 out_specs=cur(IN_COLS), out_shape=jax.ShapeDtypeStruct((t_len, IN_COLS), BF16),
                 compiler_params=_params(1))(dq, dkc, dkp, dvc, dvp, *tables, dalin, dagate, du, dvin)


def _attn_probs(q_ref, kp_ref, kc_ref, snk_ref):
    n = pl.program_id(1)
    q = q_ref[...].reshape(GQ * BLK, HEAD_DIM)
    kk = jnp.concatenate([kp_ref[...], kc_ref[...]], axis=0)
    s = lax.dot_general(q, kk, NT, preferred_element_type=F32) * (HEAD_DIM ** -0.5)
    qi = jnp.bitwise_and(lax.broadcasted_iota(jnp.int32, (GQ * BLK, 2 * BLK), 0), BLK - 1)
    kj = lax.broadcasted_iota(jnp.int32, (GQ * BLK, 2 * BLK), 1)
    dist = qi + BLK - kj
    mask = (dist >= 0) & (dist < BLK) & ((kj >= BLK) | (n > 0))
    s = jnp.where(mask, s, -1e30)
    snk = snk_ref[...]
    m = jnp.maximum(jnp.max(s, axis=-1, keepdims=True), snk)
    e = jnp.exp(s - m)
    es = jnp.exp(snk - m)
    den = jnp.sum(e, axis=-1, keepdims=True) + es
    return q, kk, e / den, es / den


def _attn_specs(nb):
    q_spec = pl.BlockSpec((GQ, BLK, HEAD_DIM), lambda g, n: (g, n, 0))
    cur = pl.BlockSpec((None, BLK, HEAD_DIM), lambda g, n: (g, n, 0))
    prev = pl.BlockSpec((None, BLK, HEAD_DIM), lambda g, n: (g, jnp.maximum(n - 1, 0), 0))
    snk = pl.BlockSpec((None, GQ * BLK, 1), lambda g, n: (g, 0, 0))
    return q_spec, cur, prev, snk


def _attn_fwd(q, k, v, snk):
    t_len = q.shape[1]
    nb = t_len // BLK

    def body(q_ref, kp_ref, kc_ref, vp_ref, vc_ref, snk_ref, o_ref):
        _, _, p, _ = _attn_probs(q_ref, kp_ref, kc_ref, snk_ref)
        vv = jnp.concatenate([vp_ref[...], vc_ref[...]], axis=0)
        o = lax.dot_general(p.astype(BF16), vv, NN, preferred_element_type=F32)
        o_ref[...] = o.reshape(GQ, BLK, HEAD_DIM).astype(BF16)

    q_spec, cur, prev, snk_spec = _attn_specs(nb)
    return _call(body, name="attn_fwd", grid=(N_KV_HEADS, nb), in_specs=[q_spec, prev, cur, prev, cur, snk_spec],
                 out_specs=q_spec, out_shape=jax.ShapeDtypeStruct(q.shape, BF16),
                 compiler_params=_params(2))(q, k, k, v, v, snk)


def _attn_bwd(q, k, v, snk, do):
    t_len = q.shape[1]
    nb = t_len // BLK

    def body(q_ref, kp_ref, kc_ref, vp_ref, vc_ref, snk_ref, do_ref, dq_ref, dkp_ref, dkc_ref, dvp_ref, dvc_ref,
             dsnk_ref):
        q_v, kk, p, p_snk = _attn_probs(q_ref, kp_ref, kc_ref, snk_ref)
        vv = jnp.concatenate([vp_ref[...], vc_ref[...]], axis=0)
        do_v = do_ref[...].reshape(GQ * BLK, HEAD_DIM).astype(BF16)
        dp = lax.dot_general(do_v, vv, NT, preferred_element_type=F32)
        delta = jnp.sum(p * dp, axis=-1, keepdims=True)
        ds = (p * (dp - delta)).astype(BF16)
        scale = HEAD_DIM ** -0.5
        dq = lax.dot_general(ds, kk, NN, preferred_element_type=F32) * scale
        dq_ref[...] = dq.reshape(GQ, BLK, HEAD_DIM)
        dkk = lax.dot_general(ds, q_v, TN, preferred_element_type=F32) * scale
        dkp_ref[...] = dkk[:BLK]
        dkc_ref[...] = dkk[BLK:]
        dvv = lax.dot_general(p.astype(BF16), do_v, TN, preferred_element_type=F32)
        dvp_ref[...] = dvv[:BLK]
        dvc_ref[...] = dvv[BLK:]
        per_row = -p_snk * delta
        row = lax.broadcasted_iota(jnp.int32, (SUBLANES, LANES), 0)
        tile = jnp.zeros((SUBLANES, LANES), F32)
        for hh in range(GQ):
            tot = jnp.sum(per_row[hh * BLK:(hh + 1) * BLK], axis=0, keepdims=True)
            tile = tile + jnp.where(row == hh, tot, 0.0)
        _accumulate(dsnk_ref, tile, pl.program_id(1) == 0)

    q_spec, cur, prev, snk_spec = _attn_specs(nb)
    kv_shape = jax.ShapeDtypeStruct(k.shape, F32)
    return _call(body, name="attn_bwd", grid=(N_KV_HEADS, nb),
                 in_specs=[q_spec, prev, cur, prev, cur, snk_spec, q_spec],
                 out_specs=[q_spec, cur, cur, cur, cur, pl.BlockSpec((None, SUBLANES, LANES), lambda g, n: (g, 0, 0))],
                 out_shape=[jax.ShapeDtypeStruct(q.shape, F32), kv_shape, kv_shape, kv_shape, kv_shape,
                            jax.ShapeDtypeStruct((N_KV_HEADS, SUBLANES, LANES), F32)],
                 compiler_params=_params(2))(q, k, k, v, v, snk, do)


CONV_CHUNK = 256


def _shift_up(win, s):
    n = win.shape[0]
    return win if s == 0 else pltpu.roll(win, n - s, 0)


def _conv_col_specs(t_len):
    lin = pl.BlockSpec((t_len, LANES), lambda j: (0, V_END // LANES + j))
    gate = pl.BlockSpec((t_len, LANES), lambda j: (0, (V_END + CONV_CH) // LANES + j))
    col = pl.BlockSpec((t_len, LANES), lambda j: (0, j))
    wsp = pl.BlockSpec((CONV_PAD, LANES), lambda j: (0, j))
    return lin, gate, col, wsp


def _conv_fwd(p, w, b):
    t_len = p.shape[0]
    ch = CONV_CHUNK

    def body(lin_ref, gate_ref, w_ref, b_ref, y_ref, hp_ref):
        hp_ref[0:CONV_PAD, :] = jnp.zeros((CONV_PAD, LANES), F32)

        def fill(c, carry):
            r0 = pl.multiple_of(c * ch, ch)
            hp_ref[pl.ds(r0 + CONV_PAD, ch), :] = lin_ref[pl.ds(r0, ch), :] * _sigmoid(gate_ref[pl.ds(r0, ch), :])
            return carry

        lax.fori_loop(0, t_len // ch, fill, 0)

        def conv(c, carry):
            r0 = pl.multiple_of(c * ch, ch)
            win = hp_ref[pl.ds(r0, ch + CONV_PAD), :]
            acc = jnp.zeros((ch, LANES), F32)
            for k in range(CONV_WIDTH):
                acc = acc + _shift_up(win, CONV_PAD - (CONV_WIDTH - 1) + k)[:ch] * w_ref[k:k + 1, :]
            y_ref[pl.ds(r0, ch), :] = acc + b_ref[...]
            return carry

        lax.fori_loop(0, t_len // ch, conv, 0)

    lin, gate, col, wsp = _conv_col_specs(t_len)
    return _call(body, name="conv_fwd", grid=(CONV_CH // LANES,),
                 in_specs=[lin, gate, wsp, pl.BlockSpec((1, LANES), lambda j: (0, j))], out_specs=col,
                 out_shape=jax.ShapeDtypeStruct((t_len, CONV_CH), F32),
                 scratch_shapes=[pltpu.VMEM((t_len + CONV_PAD, LANES), F32)],
                 compiler_params=_params(1))(p, p, w, b)


def _conv_post_fwd(y, g, b):
    t_len = y.shape[0]
    tm = _pick(t_len, (512,))

    def body(y_ref, g_ref, b_ref, o_ref):
        _, xhat = _ln_stats(y_ref[...])
        z = xhat * g_ref[...] + b_ref[...]
        o_ref[...] = (z * _sigmoid(z)).astype(BF16)

    row = pl.BlockSpec((tm, CONV_CH), lambda i: (i, 0))
    vec = pl.BlockSpec((1, CONV_CH), lambda i: (0, 0))
    return _call(body, name="conv_post_fwd", grid=(t_len // tm,), in_specs=[row, vec, vec], out_specs=row,
                 out_shape=jax.ShapeDtypeStruct((t_len, CONV_CH), BF16), compiler_params=_params(1))(y, g, b)


def _conv_post_bwd(dmix, y, g, b):
    t_len = y.shape[0]
    tm = _pick(t_len, (512,))

    def body(do_ref, y_ref, g_ref, b_ref, dy_ref, dg_ref, db_ref, dcb_ref):
        first = pl.program_id(0) == 0
        r, xhat = _ln_stats(y_ref[...])
        g_v = g_ref[...]
        z = xhat * g_v + b_ref[...]
        sig = _sigmoid(z)
        dz = do_ref[...] * (sig * (1.0 + z * (1.0 - sig)))
        _accumulate(db_ref, jnp.sum(dz, axis=0, keepdims=True), first)
        _accumulate(dg_ref, jnp.sum(dz * xhat, axis=0, keepdims=True), first)
        dy = _ln_bwd(dz, r, xhat, g_v)
        dy_ref[...] = dy
        _accumulate(dcb_ref, jnp.sum(dy, axis=0, keepdims=True), first)

    row = pl.BlockSpec((tm, CONV_CH), lambda i: (i, 0))
    do_spec = pl.BlockSpec((tm, CONV_CH), lambda i: (i, Q_END // CONV_CH))
    vec = pl.BlockSpec((1, CONV_CH), lambda i: (0, 0))
    vshape = jax.ShapeDtypeStruct((1, CONV_CH), F32)
    return _call(body, name="conv_post_bwd", grid=(t_len // tm,), in_specs=[do_spec, row, vec, vec],
                 out_specs=[row, vec, vec, vec],
                 out_shape=[jax.ShapeDtypeStruct((t_len, CONV_CH), F32), vshape, vshape, vshape],
                 compiler_params=_params(1))(dmix, y, g, b)


def _conv_bwd(p, dy, w):
    t_len = p.shape[0]
    ch = CONV_CHUNK

    def body(lin_ref, gate_ref, dy_ref, w_ref, dlin_ref, dgate_ref, dw_ref, hp_ref, dyp_ref):
        hp_ref[0:CONV_PAD, :] = jnp.zeros((CONV_PAD, LANES), F32)
        dyp_ref[t_len:t_len + CONV_PAD, :] = jnp.zeros((CONV_PAD, LANES), F32)
        dw_ref[...] = jnp.zeros((CONV_PAD, LANES), F32)

        def fill(c, carry):
            r0 = pl.multiple_of(c * ch, ch)
            hp_ref[pl.ds(r0 + CONV_PAD, ch), :] = lin_ref[pl.ds(r0, ch), :] * _sigmoid(gate_ref[pl.ds(r0, ch), :])
            dyp_ref[pl.ds(r0, ch), :] = dy_ref[pl.ds(r0, ch), :]
            return carry

        lax.fori_loop(0, t_len // ch, fill, 0)

        def step(c, carry):
            r0 = pl.multiple_of(c * ch, ch)
            win_h = hp_ref[pl.ds(r0, ch + CONV_PAD), :]
            win_dy = dyp_ref[pl.ds(r0, ch + CONV_PAD), :]
            dyc = win_dy[:ch]
            dh = jnp.zeros((ch, LANES), F32)
            for k in range(CONV_WIDTH):
                tap = _shift_up(win_h, CONV_PAD - (CONV_WIDTH - 1) + k)[:ch]
                dw_ref[k:k + 1, :] += jnp.sum(dyc * tap, axis=0, keepdims=True)
                dh = dh + _shift_up(win_dy, CONV_WIDTH - 1 - k)[:ch] * w_ref[k:k + 1, :]
            lin = lin_ref[pl.ds(r0, ch), :]
            sig = _sigmoid(gate_ref[pl.ds(r0, ch), :])
            dlin_ref[pl.ds(r0, ch), :] = (dh * sig).astype(BF16)
            dgate_ref[pl.ds(r0, ch), :] = (dh * lin * (sig * (1.0 - sig))).astype(BF16)
            return carry

        lax.fori_loop(0, t_len // ch, step, 0)

    lin, gate, col, wsp = _conv_col_specs(t_len)
    half = jax.ShapeDtypeStruct((t_len, CONV_CH), BF16)
    return _call(body, name="conv_bwd", grid=(CONV_CH // LANES,), in_specs=[lin, gate, col, wsp],
                 out_specs=[col, col, wsp],
                 out_shape=[half, half, jax.ShapeDtypeStruct((CONV_PAD, CONV_CH), F32)],
                 scratch_shapes=[pltpu.VMEM((t_len + CONV_PAD, LANES), F32), pltpu.VMEM((t_len + CONV_PAD, LANES), F32)],
                 compiler_params=_params(1))(p, p, dy, w)


def _sgu_mixed(v, w_ref, bt_ref, j):
    lane = lax.broadcasted_iota(jnp.int32, (BLK, LANES), 1)
    lo = lane < HEAD_DIM
    tri = lax.broadcasted_iota(jnp.int32, (BLK, BLK), 0) >= lax.broadcasted_iota(jnp.int32, (BLK, BLK), 1)
    vs = v[:, j * LANES:(j + 1) * LANES]
    v_lo = jnp.where(lo, vs, 0.0).astype(BF16)
    v_hi = jnp.where(lo, 0.0, vs).astype(BF16)
    w_lo = jnp.where(tri, w_ref[2 * j], 0.0).astype(BF16)
    w_hi = jnp.where(tri, w_ref[2 * j + 1], 0.0).astype(BF16)
    m = (lax.dot_general(w_lo, v_lo, NN, preferred_element_type=F32)
         + lax.dot_general(w_hi, v_hi, NN, preferred_element_type=F32))
    bias = jnp.where(lo, bt_ref[:, 2 * j:2 * j + 1], bt_ref[:, 2 * j + 1:2 * j + 2])
    return m + bias, (v_lo, v_hi, w_lo, w_hi, lo, tri)


def _sgu_specs():
    u_spec = pl.BlockSpec((BLK, SGU_CH), lambda i: (i, CONV_END // SGU_CH))
    v_spec = pl.BlockSpec((BLK, SGU_CH), lambda i: (i, CONV_END // SGU_CH + 1))
    vec = pl.BlockSpec((1, SGU_CH), lambda i: (0, 0))
    w_spec = pl.BlockSpec((SGU_HEADS, BLK, BLK), lambda i: (0, 0, 0))
    bt_spec = pl.BlockSpec((BLK, SGU_HEADS), lambda i: (0, 0))
    row = pl.BlockSpec((BLK, SGU_CH), lambda i: (i, 0))
    return u_spec, v_spec, vec, w_spec, bt_spec, row


def _sgu_fwd(p, g, b, w, bt):
    t_len = p.shape[0]

    def body(u_ref, vin_ref, g_ref, b_ref, w_ref, bt_ref, o_ref):
        _, xhat = _ln_stats(vin_ref[...])
        v = xhat * g_ref[...] + b_ref[...]
        for j in range(SGU_CH // LANES):
            m, _ = _sgu_mixed(v, w_ref, bt_ref, j)
            sl = slice(j * LANES, (j + 1) * LANES)
            o_ref[:, sl] = (u_ref[:, sl] * m).astype(BF16)

    u_spec, v_spec, vec, w_spec, bt_spec, row = _sgu_specs()
    return _call(body, name="sgu_fwd", grid=(t_len // BLK,), in_specs=[u_spec, v_spec, vec, vec, w_spec, bt_spec],
                 out_specs=row, out_shape=jax.ShapeDtypeStruct((t_len, SGU_CH), BF16),
                 compiler_params=_params(1))(p, p, g, b, w, bt)


def _sgu_bwd(p, dmix, g, b, w, bt):
    t_len = p.shape[0]

    def body(u_ref, vin_ref, do_ref, g_ref, b_ref, w_ref, bt_ref, du_ref, dvin_ref, dw_ref, dbt_ref, dg_ref,
             db_ref, dv_ref):
        first = pl.program_id(0) == 0
        r, xhat = _ln_stats(vin_ref[...])
        g_v = g_ref[...]
        v = xhat * g_v + b_ref[...]
        lane = lax.broadcasted_iota(jnp.int32, (BLK, LANES), 1)
        dbt = jnp.zeros((BLK, LANES), F32)

        @pl.when(first)
        def _():
            dw_ref[...] = jnp.zeros((SGU_HEADS, BLK, BLK), F32)

        for j in range(SGU_CH // LANES):
            m, (v_lo, v_hi, w_lo, w_hi, lo, tri) = _sgu_mixed(v, w_ref, bt_ref, j)
            sl = slice(j * LANES, (j + 1) * LANES)
            do_v = do_ref[:, sl]
            du_ref[:, sl] = (do_v * m).astype(BF16)
            dm = do_v * u_ref[:, sl]
            dm_lo = jnp.where(lo, dm, 0.0)
            dm_hi = jnp.where(lo, 0.0, dm)
            dbt = dbt + jnp.where(lane == 2 * j, jnp.sum(dm_lo, axis=-1, keepdims=True), 0.0)
            dbt = dbt + jnp.where(lane == 2 * j + 1, jnp.sum(dm_hi, axis=-1, keepdims=True), 0.0)
            dm_lo, dm_hi = dm_lo.astype(BF16), dm_hi.astype(BF16)
            dw_ref[2 * j] += jnp.where(tri, lax.dot_general(dm_lo, v_lo, NT, preferred_element_type=F32), 0.0)
            dw_ref[2 * j + 1] += jnp.where(tri, lax.dot_general(dm_hi, v_hi, NT, preferred_element_type=F32), 0.0)
            dv_ref[:, sl] = (lax.dot_general(w_lo, dm_lo, TN, preferred_element_type=F32)
                             + lax.dot_general(w_hi, dm_hi, TN, preferred_element_type=F32))
        _accumulate(dbt_ref, dbt, first)
        dv = dv_ref[...]
        _accumulate(db_ref, jnp.sum(dv, axis=0, keepdims=True), first)
        _accumulate(dg_ref, jnp.sum(dv * xhat, axis=0, keepdims=True), first)
        dvin_ref[...] = _ln_bwd(dv, r, xhat, g_v).astype(BF16)

    u_spec, v_spec, vec, w_spec, bt_spec, row = _sgu_specs()
    do_spec = pl.BlockSpec((BLK, SGU_CH), lambda i: (i, (Q_END + CONV_CH) // SGU_CH))
    half = jax.ShapeDtypeStruct((t_len, SGU_CH), BF16)
    vshape = jax.ShapeDtypeStruct((1, SGU_CH), F32)
    return _call(body, name="sgu_bwd", grid=(t_len // BLK,),
                 in_specs=[u_spec, v_spec, do_spec, vec, vec, w_spec, bt_spec],
                 out_specs=[row, row, w_spec, pl.BlockSpec((BLK, LANES), lambda i: (0, 0)), vec, vec],
                 out_shape=[half, half, jax.ShapeDtypeStruct((SGU_HEADS, BLK, BLK), F32),
                            jax.ShapeDtypeStruct((BLK, LANES), F32), vshape, vshape],
                 scratch_shapes=[pltpu.VMEM((BLK, SGU_CH), F32)],
                 compiler_params=_params(1))(p, p, dmix, g, b, w, bt)


def _place():
    x, y, c = lax.axis_index("x"), lax.axis_index("y"), lax.axis_index("c")
    chips = [(1 - x, y), (x, 1 - y), (1 - x, 1 - y)]
    return x, y, c, chips


def _hbm_specs(n):
    return [pl.BlockSpec(memory_space=pltpu.HBM)] * n


def _comm_params():
    return pltpu.CompilerParams(has_side_effects=True)


def _remote(src, dst, send_sem, recv_sem, to):
    return pltpu.make_async_remote_copy(src_ref=src, dst_ref=dst, send_sem=send_sem, recv_sem=recv_sem,
                                        device_id=to, device_id_type=MESH_ID)


def _cast_place(w_local, chip):
    n, rows, cols = w_local.shape

    def body(chip_ref, w_ref, o_ref):
        o_ref[...] = w_ref[...].astype(BF16)

    grid_spec = pltpu.PrefetchScalarGridSpec(
        num_scalar_prefetch=1, grid=(n, rows // ROW_TILE),
        in_specs=[pl.BlockSpec((None, ROW_TILE, cols), lambda l, i, ch: (l, i, 0))],
        out_specs=pl.BlockSpec((None, None, ROW_TILE, cols), lambda l, i, ch: (l, ch[0], i, 0)))
    return _call(body, name="cast_place", grid_spec=grid_spec,
                 out_shape=jax.ShapeDtypeStruct((n, N_CHIPS, rows, cols), BF16), compiler_params=_params(2))(chip, w_local)


def _all_gather_weights(placed, shards):
    n_placed, nt = len(placed), len(placed) + len(shards)

    def body(*refs):
        ins, outs = refs[:nt], refs[nt:2 * nt]
        ici_send, ici_recv, d2d_send, d2d_recv, local_sem = refs[2 * nt:]
        x, y, c, chips = _place()
        me = 2 * x + y
        sibling = (x, y, 1 - c)
        local = [pltpu.make_async_copy(ins[t].at[l], outs[t].at[l, me], local_sem.at[2 * (t - n_placed) + l])
                 for t in range(n_placed, nt) for l in range(2)]
        for cp in local:
            cp.start()
        sends = []
        for t in range(nt):
            src = outs[t].at[c, me] if t < n_placed else ins[t].at[c]
            for j, (px, py) in enumerate(chips):
                sends.append(_remote(src, outs[t].at[c, me], ici_send.at[3 * t + j], ici_recv.at[3 * t + j],
                                     (px, py, c)))
        for cp in sends:
            cp.start()
        for t in range(nt):
            for j, (px, py) in enumerate(chips):
                slab = outs[t].at[c, 2 * px + py]
                _remote(slab, slab, ici_send.at[3 * t + j], ici_recv.at[3 * t + j], (px, py, c)).wait_recv()
                fwd = _remote(slab, slab, d2d_send.at[3 * t + j], d2d_recv.at[3 * t + j], sibling)
                fwd.start()
                sends.append(fwd)
        for t in range(nt):
            for j, (px, py) in enumerate(chips):
                slab = outs[t].at[1 - c, 2 * px + py]
                _remote(slab, slab, d2d_send.at[3 * t + j], d2d_recv.at[3 * t + j], sibling).wait_recv()
        for cp in sends:
            cp.wait_send()
        for cp in local:
            cp.wait()

    out_shape = [jax.ShapeDtypeStruct(p.shape, p.dtype) for p in placed]
    out_shape += [jax.ShapeDtypeStruct((2, N_CHIPS) + s.shape[1:], s.dtype) for s in shards]
    sems = [pltpu.SemaphoreType.DMA((3 * nt,))] * 4 + [pltpu.SemaphoreType.DMA((2 * len(shards),))]
    return _call(body, name="all_gather_weights", in_specs=_hbm_specs(nt), out_specs=_hbm_specs(nt),
                 out_shape=out_shape, scratch_shapes=sems, input_output_aliases={t: t for t in range(n_placed)},
                 compiler_params=_comm_params())(*placed, *shards)


def _sibling_exchange(stacks):
    nt = len(stacks)

    def body(*refs):
        ins, outs = refs[:nt], refs[nt:2 * nt]
        send_sem, recv_sem = refs[2 * nt:]
        x, y, c, _ = _place()
        cps = [_remote(ins[t].at[1 - c], outs[t], send_sem.at[t], recv_sem.at[t], (x, y, 1 - c)) for t in range(nt)]
        for cp in cps:
            cp.start()
        for cp in cps:
            cp.wait()

    out_shape = [jax.ShapeDtypeStruct(s.shape[1:], s.dtype) for s in stacks]
    sems = [pltpu.SemaphoreType.DMA((nt,))] * 2
    return _call(body, name="sibling_exchange", in_specs=_hbm_specs(nt), out_specs=_hbm_specs(nt),
                 out_shape=out_shape, scratch_shapes=sems, compiler_params=_comm_params())(*stacks)


def _chip_exchange(partials):
    nt = len(partials)

    def body(*refs):
        ins, outs = refs[:nt], refs[nt:2 * nt]
        send_sem, recv_sem = refs[2 * nt:]
        x, y, c, chips = _place()
        cps = []
        for t in range(nt):
            for j, (px, py) in enumerate(chips):
                cps.append(_remote(ins[t].at[2 * px + py], outs[t].at[j], send_sem.at[3 * t + j], recv_sem.at[3 * t + j],
                                   (px, py, c)))
        for cp in cps:
            cp.start()
        for cp in cps:
            cp.wait()

    out_shape = [jax.ShapeDtypeStruct((3,) + s.shape[1:], s.dtype) for s in partials]
    sems = [pltpu.SemaphoreType.DMA((3 * nt,))] * 2
    return _call(body, name="chip_exchange", in_specs=_hbm_specs(nt), out_specs=_hbm_specs(nt),
                 out_shape=out_shape, scratch_shapes=sems, compiler_params=_comm_params())(*partials)


def _sibling_gather(finals):
    nt = len(finals)

    def body(*refs):
        outs = refs[nt:2 * nt]
        send_sem, recv_sem = refs[2 * nt:]
        x, y, c, _ = _place()
        cps = [_remote(outs[t].at[c], outs[t].at[c], send_sem.at[t], recv_sem.at[t], (x, y, 1 - c)) for t in range(nt)]
        for cp in cps:
            cp.start()
        for t in range(nt):
            theirs = outs[t].at[1 - c]
            _remote(theirs, theirs, send_sem.at[t], recv_sem.at[t], (x, y, 1 - c)).wait_recv()
        for cp in cps:
            cp.wait_send()

    out_shape = [jax.ShapeDtypeStruct(s.shape, s.dtype) for s in finals]
    sems = [pltpu.SemaphoreType.DMA((nt,))] * 2
    return _call(body, name="sibling_gather", in_specs=_hbm_specs(nt), out_specs=_hbm_specs(nt),
                 out_shape=out_shape, scratch_shapes=sems, input_output_aliases={t: t for t in range(nt)},
                 compiler_params=_comm_params())(*finals)


def _all_reduce_small(buf):
    rows = buf.shape[0]

    def body(x_ref, out_ref, all_ref, send_sems, recv_sems, local_sem):
        x, y, c, chips = _place()
        me, sibling = (x, y, c), (x, y, 1 - c)

        def block(px, py, pc):
            return all_ref.at[pl.ds((4 * px + 2 * py + pc) * rows, rows), :]

        def copy(k, blk, to, src=None):
            return _remote(block(*blk) if src is None else src, block(*blk), send_sems.at[k], recv_sems.at[k], to)

        mine = pltpu.make_async_copy(x_ref, block(*me), local_sem)
        mine.start()
        first = [copy(0, me, sibling, src=x_ref)]
        first += [copy(1 + j, me, (*chip, c), src=x_ref) for j, chip in enumerate(chips)]
        for cp in first:
            cp.start()
        passed = [copy(4 + j, (*chip, c), sibling) for j, chip in enumerate(chips)]
        for j, chip in enumerate(chips):
            copy(1 + j, (*chip, c), me).wait_recv()
            passed[j].start()
        copy(0, sibling, me).wait_recv()
        for j, chip in enumerate(chips):
            copy(4 + j, (*chip, 1 - c), me).wait_recv()
        for cp in first + passed:
            cp.wait_send()
        mine.wait()
        tot = all_ref[0:rows, :]
        for k in range(1, N_DEV):
            tot = tot + all_ref[k * rows:(k + 1) * rows, :]
        out_ref[...] = tot

    vm = pl.BlockSpec(memory_space=pltpu.VMEM)
    return _call(body, name="all_reduce_small", in_specs=[vm], out_specs=vm,
                 out_shape=jax.ShapeDtypeStruct(buf.shape, F32),
                 scratch_shapes=[pltpu.VMEM((N_DEV * rows, LANES), F32), pltpu.SemaphoreType.DMA((7,)),
                                 pltpu.SemaphoreType.DMA((7,)), pltpu.SemaphoreType.DMA],
                 compiler_params=pltpu.CompilerParams(has_side_effects=True,
                                                      vmem_limit_bytes=V7X_VMEM_LIMIT_BYTES))(buf)


ROW_TILE = 128


def _chip_partial(stack, received, c):
    _, _, rows, cols = stack.shape

    def body(c_ref, a_ref, b_ref, o_ref):
        o_ref[...] = (a_ref[...].astype(F32) + b_ref[...].astype(F32)).astype(BF16)

    grid_spec = pltpu.PrefetchScalarGridSpec(
        num_scalar_prefetch=1, grid=(N_CHIPS, rows // ROW_TILE),
        in_specs=[pl.BlockSpec((None, None, ROW_TILE, cols), lambda s, i, c_ref: (c_ref[0], s, i, 0)),
                  pl.BlockSpec((None, ROW_TILE, cols), lambda s, i, c_ref: (s, i, 0))],
        out_specs=pl.BlockSpec((None, ROW_TILE, cols), lambda s, i, c_ref: (s, i, 0)))
    return _call(body, name="chip_partial", grid_spec=grid_spec, out_shape=jax.ShapeDtypeStruct(received.shape, BF16),
                 compiler_params=_params(2))(c, stack, received)


def _final_sum(stack, received, from_chips, place):
    _, _, rows, cols = stack.shape

    def body(place_ref, a_ref, b_ref, r_ref, o_ref):
        tot = a_ref[...].astype(F32) + b_ref[...].astype(F32)
        for j in range(3):
            tot = tot + r_ref[j].astype(F32)
        o_ref[...] = tot

    grid_spec = pltpu.PrefetchScalarGridSpec(
        num_scalar_prefetch=1, grid=(rows // ROW_TILE,),
        in_specs=[pl.BlockSpec((None, None, ROW_TILE, cols), lambda i, pr: (pr[0], pr[1], i, 0)),
                  pl.BlockSpec((None, ROW_TILE, cols), lambda i, pr: (pr[1], i, 0)),
                  pl.BlockSpec((3, ROW_TILE, cols), lambda i, pr: (0, i, 0))],
        out_specs=pl.BlockSpec((None, ROW_TILE, cols), lambda i, pr: (pr[0], i, 0)))
    return _call(body, name="final_sum", grid_spec=grid_spec, out_shape=jax.ShapeDtypeStruct((2, rows, cols), F32),
                 compiler_params=_params(1))(place, stack, received, from_chips)


def _adamw(w, g, m, v):
    n, rows, cols = w.shape
    tr = _pick(rows, (ROW_TILE, SUBLANES))
    c1 = 1.0 - ADAM_B1 ** ADAM_STEP
    c2 = 1.0 - ADAM_B2 ** ADAM_STEP

    def body(w_ref, g_ref, m_ref, v_ref, d_ref, nm_ref, nv_ref, go_ref):
        g_v = g_ref[...]
        go_ref[...] = g_v
        nm = ADAM_B1 * m_ref[...] + (1.0 - ADAM_B1) * g_v
        nv = ADAM_B2 * v_ref[...] + (1.0 - ADAM_B2) * (g_v * g_v)
        nm_ref[...] = nm
        nv_ref[...] = nv
        d_ref[...] = -ADAM_LR * ((nm / c1) / (jnp.sqrt(nv / c2) + ADAM_EPS) + ADAM_WD * w_ref[...])

    blk = pl.BlockSpec((None, tr, cols), lambda l, i: (l, i, 0))
    shape = jax.ShapeDtypeStruct(w.shape, F32)
    return _call(body, name="adamw", grid=(n, rows // tr), in_specs=[blk] * 4, out_specs=[blk] * 4,
                 out_shape=[shape] * 4, compiler_params=_params(2))(w, g, m, v)


def _to_heads(a, n_heads):
    t_len = a.shape[0]
    return a.reshape(t_len, n_heads, HEAD_DIM).transpose(1, 0, 2)


def _from_heads(a):
    n_heads, t_len, _ = a.shape
    return a.transpose(1, 0, 2).reshape(t_len, n_heads * HEAD_DIM)


def _ffn_fwd(x, gain, w_in_g, w_out_g, layer):
    h = _rmsnorm_fwd(x, gain)
    gu, act = _mm_ffn_in(h, w_in_g, layer)
    x_new = _mm_out_res("ffn_out", act, w_out_g, layer, x, FFN_RESIDUAL_WEIGHT)[0]
    return x_new, (x, h, gu, act)


def _ffn_bwd(dx, dxb, saved, gain, w_in_g, w_out_g, layer, stack_in, stack_out):
    x, h, gu, act = saved
    dgu = _mm_dact_swiglu(dxb, w_out_g, layer, gu, FFN_RESIDUAL_WEIGHT)
    stack_out = _dw_rows("ffn_dw_out", act, dxb, layer, stack_out, FFN_RESIDUAL_WEIGHT)
    dh = _mm_dh_ffn(dgu, w_in_g, layer)
    stack_in = _dw_ffn_in(h, dgu, layer, stack_in)
    dx_in, dxb_in, dgain = _rmsnorm_bwd(dh, x, gain, dx)
    return dx_in, dxb_in, dgain, stack_in, stack_out


def _mix_fwd(x, gain, w_in_g, w_out_g, layer, small, tables):
    h = _rmsnorm_fwd(x, gain)
    p = _mm_proj(h, w_in_g, layer)[0]
    qkv = _rope_fwd(p, tables)
    q = _to_heads(qkv[:, :Q_END], N_Q_HEADS)
    k = _to_heads(qkv[:, Q_END:K_END], N_KV_HEADS)
    v = _to_heads(qkv[:, K_END:V_END], N_KV_HEADS)
    attn = _from_heads(_attn_fwd(q, k, v, small["snk"]))
    y = _conv_fwd(p, small["conv_w"], small["conv_b"])
    conv = _conv_post_fwd(y, small["conv_ln_g"], small["conv_ln_b"])
    sgu = _sgu_fwd(p, small["sgu_ln_g"], small["sgu_ln_b"], small["sgu_w"], small["sgu_bt"])
    mix = jnp.concatenate([attn, conv, sgu], axis=1)
    x_new = _mm_out_res("mix_out", mix, w_out_g, layer, x, 1.0)[0]
    return x_new, (x, h, p, q, k, v, y, mix)


def _mix_bwd(dx, dxb, saved, gain, w_in_g, w_out_g, small, tables, layer, stack_in, stack_out):
    x, h, p, q, k, v, y, mix = saved
    dmix = _mm_dmix(dxb, w_out_g, layer)
    stack_out = _dw_rows("mix_dw_out", mix, dxb, layer, stack_out, 1.0)
    do = _to_heads(dmix[:, :Q_END], N_Q_HEADS)
    dq, dkp, dkc, dvp, dvc, dsnk = _attn_bwd(q, k, v, small["snk"], do)
    dy, d_ln_g, d_ln_b, d_conv_b = _conv_post_bwd(dmix, y, small["conv_ln_g"], small["conv_ln_b"])
    dalin, dagate, d_conv_w = _conv_bwd(p, dy, small["conv_w"])
    du, dvin, d_sgu_w, d_sgu_bt, d_sgu_g, d_sgu_b = _sgu_bwd(p, dmix, small["sgu_ln_g"], small["sgu_ln_b"],
                                                           small["sgu_w"], small["sgu_bt"])
    dp = _assemble_dp(_from_heads(dq), _from_heads(dkc), _from_heads(dkp), _from_heads(dvc), _from_heads(dvp),
                      tables, dalin, dagate, du, dvin)
    dh = _mm_dh_mix(dp, w_in_g, layer)
    stack_in = _dw_mix_in(h, dp, layer, stack_in)
    dx_in, dxb_in, dgain = _rmsnorm_bwd(dh, x, gain, dx)
    grads = {
        "norm_mix": dgain[0], "conv_dw_w": d_conv_w[:CONV_WIDTH], "conv_dw_b": d_conv_b[0],
        "conv_ln_g": d_ln_g[0], "conv_ln_b": d_ln_b[0], "sgu_ln_g": d_sgu_g[0], "sgu_ln_b": d_sgu_b[0],
        "sgu_w": d_sgu_w, "sgu_b": d_sgu_bt[:, :SGU_HEADS].T, "attn_sinks": dsnk[:, :GQ, 0].reshape(N_Q_HEADS),
    }
    return dx_in, dxb_in, grads, stack_in, stack_out


BIG = ("ffn1_w_in", "ffn1_w_out", "w_in", "w_out", "ffn2_w_in", "ffn2_w_out")
SMALL = ("norm_ffn1", "norm_mix", "conv_dw_w", "conv_dw_b", "conv_ln_g", "conv_ln_b", "sgu_ln_g", "sgu_ln_b",
         "sgu_w", "sgu_b", "attn_sinks", "norm_ffn2", "final_norm")
WEIGHTS = ("norm_ffn1", "ffn1_w_in", "ffn1_w_out", "norm_mix", "w_in", "conv_dw_w", "conv_dw_b", "conv_ln_g",
           "conv_ln_b", "sgu_ln_g", "sgu_ln_b", "sgu_w", "sgu_b", "attn_sinks", "w_out", "norm_ffn2", "ffn2_w_in",
           "ffn2_w_out", "final_norm")
PACK_ROWS = SUBLANES * LANES


def _pack(arrays):
    flat = jnp.concatenate([a.reshape(-1).astype(F32) for a in arrays])
    pad = (-flat.shape[0]) % PACK_ROWS
    return jnp.pad(flat, (0, pad)).reshape(-1, LANES)


def _unpack(buf, shapes):
    flat = buf.reshape(-1)
    out, off = [], 0
    for s in shapes:
        n = 1
        for d in s:
            n *= d
        out.append(flat[off:off + n].reshape(s))
        off += n
    return out


def kernel(x, positions, norm_ffn1, ffn1_w_in, ffn1_w_out, norm_mix, w_in, conv_dw_w, conv_dw_b, conv_ln_g, conv_ln_b, sgu_ln_g, sgu_ln_b, sgu_w, sgu_b, attn_sinks, w_out, norm_ffn2, ffn2_w_in, ffn2_w_out, final_norm, loss_target, m_norm_ffn1, m_ffn1_w_in, m_ffn1_w_out, m_norm_mix, m_w_in, m_conv_dw_w, m_conv_dw_b, m_conv_ln_g, m_conv_ln_b, m_sgu_ln_g, m_sgu_ln_b, m_sgu_w, m_sgu_b, m_attn_sinks, m_w_out, m_norm_ffn2, m_ffn2_w_in, m_ffn2_w_out, m_final_norm, v_norm_ffn1, v_ffn1_w_in, v_ffn1_w_out, v_norm_mix, v_w_in, v_conv_dw_w, v_conv_dw_b, v_conv_ln_g, v_conv_ln_b, v_sgu_ln_g, v_sgu_ln_b, v_sgu_w, v_sgu_b, v_attn_sinks, v_w_out, v_norm_ffn2, v_ffn2_w_in, v_ffn2_w_out, v_final_norm):
    w = dict(norm_ffn1=norm_ffn1, ffn1_w_in=ffn1_w_in, ffn1_w_out=ffn1_w_out, norm_mix=norm_mix, w_in=w_in,
             conv_dw_w=conv_dw_w, conv_dw_b=conv_dw_b, conv_ln_g=conv_ln_g, conv_ln_b=conv_ln_b, sgu_ln_g=sgu_ln_g,
             sgu_ln_b=sgu_ln_b, sgu_w=sgu_w, sgu_b=sgu_b, attn_sinks=attn_sinks, w_out=w_out, norm_ffn2=norm_ffn2,
             ffn2_w_in=ffn2_w_in, ffn2_w_out=ffn2_w_out, final_norm=final_norm)
    m = dict(norm_ffn1=m_norm_ffn1, ffn1_w_in=m_ffn1_w_in, ffn1_w_out=m_ffn1_w_out, norm_mix=m_norm_mix, w_in=m_w_in,
             conv_dw_w=m_conv_dw_w, conv_dw_b=m_conv_dw_b, conv_ln_g=m_conv_ln_g, conv_ln_b=m_conv_ln_b,
             sgu_ln_g=m_sgu_ln_g, sgu_ln_b=m_sgu_ln_b, sgu_w=m_sgu_w, sgu_b=m_sgu_b, attn_sinks=m_attn_sinks,
             w_out=m_w_out, norm_ffn2=m_norm_ffn2, ffn2_w_in=m_ffn2_w_in, ffn2_w_out=m_ffn2_w_out,
             final_norm=m_final_norm)
    v = dict(norm_ffn1=v_norm_ffn1, ffn1_w_in=v_ffn1_w_in, ffn1_w_out=v_ffn1_w_out, norm_mix=v_norm_mix, w_in=v_w_in,
             conv_dw_w=v_conv_dw_w, conv_dw_b=v_conv_dw_b, conv_ln_g=v_conv_ln_g, conv_ln_b=v_conv_ln_b,
             sgu_ln_g=v_sgu_ln_g, sgu_ln_b=v_sgu_ln_b, sgu_w=v_sgu_w, sgu_b=v_sgu_b, attn_sinks=v_attn_sinks,
             w_out=v_w_out, norm_ffn2=v_norm_ffn2, ffn2_w_in=v_ffn2_w_in, ffn2_w_out=v_ffn2_w_out,
             final_norm=v_final_norm)
    depth = norm_ffn1.shape[0]
    assert depth == 2 and x.shape[0] == 1
    xc = lax.axis_index("x")
    yc = lax.axis_index("y")
    cc = lax.axis_index("c")
    chip = 2 * xc + yc

    chip_arr = chip.reshape(1).astype(jnp.int32)
    gathered = _all_gather_weights([_cast_place(w[n], chip_arr) for n in BIG], [conv_dw_w])
    wg = dict(zip(BIG, gathered[:len(BIG)]))
    conv_w_full = gathered[-1].transpose(0, 2, 1, 3).reshape(depth, CONV_WIDTH, CONV_CH)
    conv_w_full = jnp.pad(conv_w_full, ((0, 0), (0, CONV_PAD - CONV_WIDTH), (0, 0)))

    tables = _rope_tables(positions)
    small = []
    for l in range(depth):
        small.append(dict(
            snk=jnp.broadcast_to(attn_sinks[l].reshape(N_KV_HEADS, GQ, 1, 1), (N_KV_HEADS, GQ, BLK, 1)).reshape(
                N_KV_HEADS, GQ * BLK, 1),
            conv_w=conv_w_full[l], conv_b=conv_dw_b[l][None], conv_ln_g=conv_ln_g[l][None],
            conv_ln_b=conv_ln_b[l][None], sgu_ln_g=sgu_ln_g[l][None], sgu_ln_b=sgu_ln_b[l][None], sgu_w=sgu_w[l],
            sgu_bt=sgu_b[l].T))

    xs = x[0]
    saved = []
    for l in range(depth):
        xs, s1 = _ffn_fwd(xs, norm_ffn1[l][None], wg["ffn1_w_in"], wg["ffn1_w_out"], l)
        xs, s2 = _mix_fwd(xs, norm_mix[l][None], wg["w_in"], wg["w_out"], l, small[l], tables)
        xs, s3 = _ffn_fwd(xs, norm_ffn2[l][None], wg["ffn2_w_in"], wg["ffn2_w_out"], l)
        saved.append((s1, s2, s3))
    dx, dxb, d_final, loss_part = _loss_head(xs, final_norm[None], loss_target[0])

    stacks = {n: None for n in BIG}
    small_grads = [None] * depth
    for l in reversed(range(depth)):
        s1, s2, s3 = saved[l]
        dx, dxb, dg2, stacks["ffn2_w_in"], stacks["ffn2_w_out"] = _ffn_bwd(
            dx, dxb, s3, norm_ffn2[l][None], wg["ffn2_w_in"], wg["ffn2_w_out"], l, stacks["ffn2_w_in"],
            stacks["ffn2_w_out"])
        dx, dxb, gm, stacks["w_in"], stacks["w_out"] = _mix_bwd(
            dx, dxb, s2, norm_mix[l][None], wg["w_in"], wg["w_out"], small[l], tables, l, stacks["w_in"],
            stacks["w_out"])
        dx, dxb, dg1, stacks["ffn1_w_in"], stacks["ffn1_w_out"] = _ffn_bwd(
            dx, dxb, s1, norm_ffn1[l][None], wg["ffn1_w_in"], wg["ffn1_w_out"], l, stacks["ffn1_w_in"],
            stacks["ffn1_w_out"])
        gm["norm_ffn1"] = dg1[0]
        gm["norm_ffn2"] = dg2[0]
        small_grads[l] = gm
    grad_x = dx[None]

    stack_list = [stacks[n] for n in BIG]
    from_sibling = _sibling_exchange(stack_list)
    c_arr = cc.reshape(1).astype(jnp.int32)
    partials = [_chip_partial(s, r, c_arr) for s, r in zip(stack_list, from_sibling)]
    from_chips = _chip_exchange(partials)
    place = jnp.stack([cc, chip]).astype(jnp.int32)
    finals = [_final_sum(s, r, f, place) for s, r, f in zip(stack_list, from_sibling, from_chips)]
    big_grads = dict(zip(BIG, _sibling_gather(finals)))

    per_layer = [n for n in SMALL if n != "final_norm"]
    small_local = [jnp.stack([small_grads[l][n] for l in range(depth)]) for n in per_layer]
    small_local += [d_final[0], loss_part[0, :1]]
    small_shapes = [a.shape for a in small_local]
    summed = _unpack(_all_reduce_small(_pack(small_local)), small_shapes)
    loss = summed[-1][0]
    sg = dict(zip(per_layer + ["final_norm"], summed[:-1]))
    sg["conv_dw_w"] = lax.dynamic_slice_in_dim(sg["conv_dw_w"], chip * LANES, LANES, axis=2)

    delta, new_m, new_v = {}, {}, {}
    for n in BIG:
        delta[n], new_m[n], new_v[n], big_grads[n] = _adamw(w[n], big_grads[n], m[n], v[n])
    shapes = [w[n].shape for n in SMALL]
    packed = [_pack([d[n] for n in SMALL])[None] for d in (w, sg, m, v)]
    outs = _adamw(*packed)
    for d, buf in zip((delta, new_m, new_v), outs[:3]):
        d.update(zip(SMALL, _unpack(buf[0], shapes)))
    grads = {**big_grads, **sg}
    return (loss, grad_x, *[grads[n] for n in WEIGHTS], *[delta[n] for n in WEIGHTS],
            *[new_m[n] for n in WEIGHTS], *[new_v[n] for n in WEIGHTS])
```

```python
import functools

import jax
import jax.numpy as jnp
from jax import lax
from jax.experimental import pallas as pl
from jax.experimental.pallas import tpu as pltpu

F32 = jnp.float32
BF16 = jnp.bfloat16
MESH_ID = pl.DeviceIdType.MESH

V7X_VMEM_LIMIT_BYTES = 56 * 2**20
LANES = 128
SUBLANES = 8

HEAD_DIM = 64
N_Q_HEADS = 16
N_KV_HEADS = 4
GQ = N_Q_HEADS // N_KV_HEADS
BLK = 128
ROT_HALF = 8
ROPE_THETA = 500000.0
CONV_WIDTH = 31
CONV_PAD = 32
CONV_CH = 512
SGU_CH = 512
SGU_HEADS = 8
Q_END = N_Q_HEADS * HEAD_DIM
K_END = Q_END + N_KV_HEADS * HEAD_DIM
V_END = K_END + N_KV_HEADS * HEAD_DIM
CONV_END = V_END + 2 * CONV_CH
IN_COLS = CONV_END + 2 * SGU_CH
NORM_EPS = 1e-5
FFN_RESIDUAL_WEIGHT = 0.5
N_CHIPS = 4
N_DEV = 8

ADAM_LR = 0.001
ADAM_B1 = 0.9
ADAM_B2 = 0.999
ADAM_EPS = 1e-08
ADAM_WD = 0.01
ADAM_STEP = 10

NN = (((1,), (0,)), ((), ()))
NT = (((1,), (1,)), ((), ()))
TN = (((0,), (0,)), ((), ()))


def _pick(n, cands):
    for c in cands:
        if n % c == 0:
            return c
    raise ValueError(f"no tile of {cands} divides {n}")


def _params(n_axes):
    return pltpu.CompilerParams(dimension_semantics=("arbitrary",) * n_axes, vmem_limit_bytes=V7X_VMEM_LIMIT_BYTES)


def _call(body, **kw):
    return pl.pallas_call(body, **kw)


def _sigmoid(x):
    return 1.0 / (1.0 + jnp.exp(-x))


class _Comm:
    def __init__(self, reads, aliased, fresh, sems, start, finish):
        self.reads, self.aliased, self.fresh, self.sems = list(reads), list(aliased), list(fresh), list(sems)
        self.start, self.finish = start, finish
        self.aliased_out, self.fresh_out = None, None


def _merge_comms(comms):
    comms = [cm for cm in comms if cm is not None]
    if not comms:
        return None
    if len(comms) == 1:
        return comms[0]

    def split(refs, counts):
        out, off = [], 0
        for n in counts:
            out.append(refs[off:off + n])
            off += n
        return out

    def run(which):
        def f(rd, al, fr, sm):
            parts = zip(split(rd, [len(cm.reads) for cm in comms]), split(al, [len(cm.aliased) for cm in comms]),
                        split(fr, [len(cm.fresh) for cm in comms]), split(sm, [len(cm.sems) for cm in comms]))
            for cm, (r, a, f_, s) in zip(comms, parts):
                getattr(cm, which)(r, a, f_, s)
        return f

    merged = _Comm(sum((cm.reads for cm in comms), []), sum((cm.aliased for cm in comms), []),
                   sum((cm.fresh for cm in comms), []), sum((cm.sems for cm in comms), []), run("start"), run("finish"))
    merged.parts = comms
    return merged


def _hosted_call(body, comm, name, grid, inputs, in_specs, out_shape, out_specs, scratch_shapes=(), aliases=None):
    n_in, n_out, n_scr = len(inputs), len(out_shape), len(scratch_shapes)
    aliases = dict(aliases or {})
    if comm is None:
        return _call(body, name=name, grid=grid, in_specs=list(in_specs), out_specs=list(out_specs),
                     out_shape=list(out_shape), scratch_shapes=list(scratch_shapes), input_output_aliases=aliases,
                     compiler_params=_params(len(grid)))(*inputs)
    nr, na, nf = len(comm.reads), len(comm.aliased), len(comm.fresh)

    def full(*refs):
        ins = refs[:n_in]
        rd = refs[n_in:n_in + nr]
        pos = n_in + nr + na
        outs = refs[pos:pos + n_out]
        al = refs[pos + n_out:pos + n_out + na]
        fr = refs[pos + n_out + na:pos + n_out + na + nf]
        pos = pos + n_out + na + nf
        scr = refs[pos:pos + n_scr]
        sems = refs[pos + n_scr:]
        first, last = None, None
        for axis, size in enumerate(grid):
            f, l = pl.program_id(axis) == 0, pl.program_id(axis) == size - 1
            first = f if first is None else jnp.logical_and(first, f)
            last = l if last is None else jnp.logical_and(last, l)

        @pl.when(first)
        def _():
            comm.start(rd, al, fr, sems)

        body(*ins, *outs, *scr)

        @pl.when(last)
        def _():
            comm.finish(rd, al, fr, sems)

    hbm = pl.BlockSpec(memory_space=pltpu.HBM)
    for i in range(na):
        aliases[n_in + nr + i] = n_out + i
    struct = [jax.ShapeDtypeStruct(a.shape, a.dtype) for a in comm.aliased]
    res = _call(full, name=name, grid=grid, in_specs=list(in_specs) + [hbm] * (nr + na),
                out_specs=list(out_specs) + [hbm] * (na + nf), out_shape=list(out_shape) + struct + comm.fresh,
                scratch_shapes=list(scratch_shapes) + comm.sems, input_output_aliases=aliases,
                compiler_params=pltpu.CompilerParams(dimension_semantics=("arbitrary",) * len(grid),
                                                     vmem_limit_bytes=V7X_VMEM_LIMIT_BYTES, has_side_effects=True),
                )(*inputs, *comm.reads, *comm.aliased)
    _deliver(comm, res[n_out:n_out + na], res[n_out + na:])
    return res[:n_out]


def _deliver(comm, aliased_out, fresh_out):
    comm.aliased_out, comm.fresh_out = list(aliased_out), list(fresh_out)
    off_a = off_f = 0
    for part in getattr(comm, "parts", []):
        _deliver(part, aliased_out[off_a:off_a + len(part.aliased)], fresh_out[off_f:off_f + len(part.fresh)])
        off_a += len(part.aliased)
        off_f += len(part.fresh)


def _standalone(name, comm):
    def body(*refs):
        nr, na, nf = len(comm.reads), len(comm.aliased), len(comm.fresh)
        rd, al, fr, sems = refs[:nr], refs[nr + na:nr + 2 * na], refs[nr + 2 * na:nr + 2 * na + nf], refs[nr + 2 * na + nf:]
        comm.start(rd, al, fr, sems)
        comm.finish(rd, al, fr, sems)

    nr, na, nf = len(comm.reads), len(comm.aliased), len(comm.fresh)
    struct = [jax.ShapeDtypeStruct(a.shape, a.dtype) for a in comm.aliased]
    res = _call(body, name=name, in_specs=_hbm_specs(nr + na), out_specs=_hbm_specs(na + nf),
                out_shape=struct + comm.fresh, scratch_shapes=comm.sems,
                input_output_aliases={nr + i: i for i in range(na)},
                compiler_params=_comm_params())(*comm.reads, *comm.aliased)
    _deliver(comm, res[:na], res[na:])


def _matmul(name, grid, a_ops, b_ops, terms, dims, out_shape, out_specs, epilogue, extra_ops=(), nk=1,
            acc_shapes=(), alias=None, comm=None):
    na, nb, ne, no = len(a_ops), len(b_ops), len(extra_ops), len(out_shape)

    def body(*refs):
        a = refs[:na]
        b = refs[na:na + nb]
        e = refs[na + nb:na + nb + ne]
        first_out = na + nb + ne + (1 if alias is not None else 0)
        o = refs[first_out:first_out + no]
        accs = refs[first_out + no:]

        def partial(t):
            tot = None
            for ai, bi in t:
                d = lax.dot_general(a[ai][...], b[bi][...], dims, preferred_element_type=F32)
                tot = d if tot is None else tot + d
            return tot

        if nk == 1:
            epilogue([partial(t) for t in terms], e, o)
        else:
            k = pl.program_id(len(grid) - 1)

            @pl.when(k == 0)
            def _():
                for acc in accs:
                    acc[...] = jnp.zeros(acc.shape, F32)

            for acc, t in zip(accs, terms):
                acc[...] += partial(t)

            @pl.when(k == nk - 1)
            def _():
                epilogue([acc[...] for acc in accs], e, o)

    ops = list(a_ops) + list(b_ops) + list(extra_ops)
    arrays = [x for x, _ in ops]
    in_specs = [s for _, s in ops]
    aliases = {}
    if alias is not None:
        arrays.append(alias[0])
        in_specs.append(pl.BlockSpec(memory_space=pl.ANY))
        aliases[len(arrays) - 1] = alias[1]
    scratch = [pltpu.VMEM(s, F32) for s in acc_shapes] if nk > 1 else []
    return _hosted_call(body, comm, name, grid, arrays, in_specs, out_shape, out_specs, scratch, aliases)


def _mm_ffn_in(h, w_g, layer, comm=None):
    t_len, d = h.shape
    fs = w_g.shape[3]
    f = 2 * fs
    tm = _pick(t_len, (2048, 1024, 512))
    tn = _pick(fs, (256, 128))
    nj = fs // tn

    def epilogue(accs, e, o):
        g, u = accs
        o[0][0] = g.astype(BF16)
        o[0][1] = u.astype(BF16)
        o[1][...] = (g * _sigmoid(g) * u).astype(BF16)

    return _matmul(
        "ffn_in", (t_len // tm, 2, nj),
        [(h, pl.BlockSpec((tm, d), lambda i, s, j: (i, 0)))],
        [(w_g, pl.BlockSpec((None, None, d, tn), lambda i, s, j: (layer, s, 0, j))),
         (w_g, pl.BlockSpec((None, None, d, tn), lambda i, s, j: (layer, s + 2, 0, j)))],
        [[(0, 0)], [(0, 1)]], NN,
        [jax.ShapeDtypeStruct((2, t_len, f), BF16), jax.ShapeDtypeStruct((t_len, f), BF16)],
        [pl.BlockSpec((2, tm, tn), lambda i, s, j: (0, i, s * nj + j)),
         pl.BlockSpec((tm, tn), lambda i, s, j: (i, s * nj + j))],
        epilogue, comm=comm)


def _mm_out_res(name, a, w_g, layer, x, scale, comm=None):
    t_len = a.shape[0]
    ks, n = w_g.shape[2], w_g.shape[3]
    tm = _pick(t_len, (1024, 512))
    tn = _pick(n, (1024,))
    tk = _pick(ks, (1408, 512, 256))
    nks = ks // tk

    def epilogue(accs, e, o):
        o[0][...] = e[0][...] + scale * accs[0]

    return _matmul(
        name, (t_len // tm, n // tn, N_CHIPS * nks),
        [(a, pl.BlockSpec((tm, tk), lambda i, j, k: (i, k)))],
        [(w_g, pl.BlockSpec((None, None, tk, tn), lambda i, j, k: (layer, k // nks, k % nks, j)))],
        [[(0, 0)]], NN,
        [jax.ShapeDtypeStruct((t_len, n), F32)],
        [pl.BlockSpec((tm, tn), lambda i, j, k: (i, j))],
        epilogue, extra_ops=[(x, pl.BlockSpec((tm, tn), lambda i, j, k: (i, j)))],
        nk=N_CHIPS * nks, acc_shapes=[(tm, tn)], comm=comm)


def _mm_proj(h, w_g, layer, comm=None):
    t_len, d = h.shape
    cs = w_g.shape[3]
    tm = _pick(t_len, (1024, 512))

    def epilogue(accs, e, o):
        o[0][...] = accs[0]

    return _matmul(
        "mix_in", (t_len // tm, N_CHIPS),
        [(h, pl.BlockSpec((tm, d), lambda i, s: (i, 0)))],
        [(w_g, pl.BlockSpec((None, None, d, cs), lambda i, s: (layer, s, 0, 0)))],
        [[(0, 0)]], NN,
        [jax.ShapeDtypeStruct((t_len, N_CHIPS * cs), F32)],
        [pl.BlockSpec((tm, cs), lambda i, s: (i, s))],
        epilogue, comm=comm)


def _mm_dact_swiglu(dxb, w_g, layer, gu, scale, comm=None):
    t_len, d = dxb.shape
    rs = w_g.shape[2]
    tm = _pick(t_len, (512,))
    tn = _pick(rs, (1408, 256, 128))
    nj = rs // tn

    def epilogue(accs, e, o):
        dact = scale * accs[0]
        g = e[0][0].astype(F32)
        u = e[0][1].astype(F32)
        sig = _sigmoid(g)
        o[0][0] = (dact * u * (sig * (1.0 + g * (1.0 - sig)))).astype(BF16)
        o[0][1] = (dact * (g * sig)).astype(BF16)

    gu_spec = pl.BlockSpec((2, tm, tn), lambda i, s, j: (0, i, s * nj + j))
    return _matmul(
        "ffn_dact", (t_len // tm, N_CHIPS, nj),
        [(dxb, pl.BlockSpec((tm, d), lambda i, s, j: (i, 0)))],
        [(w_g, pl.BlockSpec((None, None, tn, d), lambda i, s, j: (layer, s, j, 0)))],
        [[(0, 0)]], NT,
        [jax.ShapeDtypeStruct(gu.shape, BF16)], [gu_spec],
        epilogue, extra_ops=[(gu, gu_spec)], comm=comm)[0]


def _mm_dh_ffn(dgu, w_g, layer, comm=None):
    t_len = dgu.shape[1]
    d, fs = w_g.shape[2], w_g.shape[3]
    tm = _pick(t_len, (1024, 512))
    tk = _pick(fs, (256, 128))
    nks = fs // tk
    nk = 2 * nks

    def epilogue(accs, e, o):
        o[0][...] = accs[0]

    return _matmul(
        "ffn_dh", (t_len // tm, nk),
        [(dgu, pl.BlockSpec((None, tm, tk), lambda i, k: (0, i, k))),
         (dgu, pl.BlockSpec((None, tm, tk), lambda i, k: (1, i, k)))],
        [(w_g, pl.BlockSpec((None, None, d, tk), lambda i, k: (layer, k // nks, 0, k % nks))),
         (w_g, pl.BlockSpec((None, None, d, tk), lambda i, k: (layer, k // nks + 2, 0, k % nks)))],
        [[(0, 0), (1, 1)]], NT,
        [jax.ShapeDtypeStruct((t_len, d), F32)],
        [pl.BlockSpec((tm, d), lambda i, k: (i, 0))],
        epilogue, nk=nk, acc_shapes=[(tm, d)], comm=comm)[0]


def _mm_dw(name, a, a_spec_of, b, b_spec_of, layer, stack, rows, cols, tn, scale, comm=None):
    t_len = a.shape[0]
    tt = _pick(t_len, (1024, 512))
    nj = cols // tn

    def epilogue(accs, e, o):
        o[0][...] = (scale * accs[0]).astype(BF16)

    shape = jax.ShapeDtypeStruct((2, N_CHIPS, rows, cols), BF16)
    return _matmul(
        name, (N_CHIPS, nj, t_len // tt),
        [(a, a_spec_of(tt))], [(b, b_spec_of(tt, tn, nj))],
        [[(0, 0)]], TN, [shape],
        [pl.BlockSpec((None, None, rows, tn), lambda s, j, t: (layer, s, 0, j))],
        epilogue, nk=t_len // tt, acc_shapes=[(rows, tn)],
        alias=None if stack is None else (stack, 0), comm=comm)[0]


def _dw_ffn_in(h, dgu, layer, stack, comm=None):
    d = h.shape[1]
    fs = dgu.shape[2] // 2
    tn = _pick(fs, (1408, 256))
    return _mm_dw(
        "ffn_dw_in", h, lambda tt: pl.BlockSpec((tt, d), lambda s, j, t: (t, 0)),
        dgu, lambda tt, tn_, nj: pl.BlockSpec((None, tt, tn_), lambda s, j, t: (s // 2, t, (s % 2) * nj + j)),
        layer, stack, d, fs, tn, 1.0, comm)


def _dw_rows(name, a, dxb, layer, stack, scale, comm=None):
    rs = a.shape[1] // N_CHIPS
    d = dxb.shape[1]
    tn = _pick(d, (1024,))
    return _mm_dw(
        name, a, lambda tt: pl.BlockSpec((tt, rs), lambda s, j, t: (t, s)),
        dxb, lambda tt, tn_, nj: pl.BlockSpec((tt, tn_), lambda s, j, t: (t, j)),
        layer, stack, rs, d, tn, scale, comm)


def _dw_mix_in(h, dp, layer, stack):
    d = h.shape[1]
    cs = dp.shape[1] // N_CHIPS
    return _mm_dw(
        "mix_dw_in", h, lambda tt: pl.BlockSpec((tt, d), lambda s, j, t: (t, 0)),
        dp, lambda tt, tn_, nj: pl.BlockSpec((tt, tn_), lambda s, j, t: (t, s)),
        layer, stack, d, cs, cs, 1.0)


def _mm_dmix(dxb, w_g, layer):
    t_len, d = dxb.shape
    rs = w_g.shape[2]
    tm = _pick(t_len, (1024, 512))

    def epilogue(accs, e, o):
        o[0][...] = accs[0]

    return _matmul(
        "mix_dout", (t_len // tm, N_CHIPS),
        [(dxb, pl.BlockSpec((tm, d), lambda i, s: (i, 0)))],
        [(w_g, pl.BlockSpec((None, None, rs, d), lambda i, s: (layer, s, 0, 0)))],
        [[(0, 0)]], NT,
        [jax.ShapeDtypeStruct((t_len, N_CHIPS * rs), F32)],
        [pl.BlockSpec((tm, rs), lambda i, s: (i, s))],
        epilogue)[0]


def _mm_dh_mix(dp, w_g, layer):
    t_len = dp.shape[0]
    d, cs = w_g.shape[2], w_g.shape[3]
    tm = _pick(t_len, (1024, 512))

    def epilogue(accs, e, o):
        o[0][...] = accs[0]

    return _matmul(
        "mix_dh", (t_len // tm, N_CHIPS),
        [(dp, pl.BlockSpec((tm, cs), lambda i, k: (i, k)))],
        [(w_g, pl.BlockSpec((None, None, d, cs), lambda i, k: (layer, k, 0, 0)))],
        [[(0, 0)]], NT,
        [jax.ShapeDtypeStruct((t_len, d), F32)],
        [pl.BlockSpec((tm, d), lambda i, k: (i, 0))],
        epilogue, nk=N_CHIPS, acc_shapes=[(tm, d)])[0]


def _rms_stats(x):
    r = lax.rsqrt(jnp.mean(x * x, axis=-1, keepdims=True) + NORM_EPS)
    return r, x * r


def _accumulate(ref, part, first):
    @pl.when(first)
    def _():
        ref[...] = part

    @pl.when(jnp.logical_not(first))
    def _():
        ref[...] += part


def _rmsnorm_fwd(x, g):
    t_len, d = x.shape
    tm = _pick(t_len, (512,))

    def body(x_ref, g_ref, h_ref):
        _, xhat = _rms_stats(x_ref[...])
        h_ref[...] = (xhat * g_ref[...]).astype(BF16)

    row = pl.BlockSpec((tm, d), lambda i: (i, 0))
    vec = pl.BlockSpec((1, d), lambda i: (0, 0))
    return _call(body, name="rmsnorm_fwd", grid=(t_len // tm,), in_specs=[row, vec], out_specs=row,
                 out_shape=jax.ShapeDtypeStruct((t_len, d), BF16), compiler_params=_params(1))(x, g)


def _rmsnorm_bwd(dh, x, g, dres):
    t_len, d = x.shape
    tm = _pick(t_len, (256,))

    def body(dh_ref, x_ref, g_ref, dres_ref, dx_ref, dxb_ref, dg_ref):
        r, xhat = _rms_stats(x_ref[...])
        dh_v = dh_ref[...]
        gd = dh_v * g_ref[...]
        dx = dres_ref[...] + r * (gd - xhat * jnp.mean(gd * xhat, axis=-1, keepdims=True))
        dx_ref[...] = dx
        dxb_ref[...] = dx.astype(BF16)
        _accumulate(dg_ref, jnp.sum(dh_v * xhat, axis=0, keepdims=True), pl.program_id(0) == 0)

    row = pl.BlockSpec((tm, d), lambda i: (i, 0))
    vec = pl.BlockSpec((1, d), lambda i: (0, 0))
    return _call(body, name="rmsnorm_bwd", grid=(t_len // tm,), in_specs=[row, row, vec, row],
                 out_specs=[row, row, vec],
                 out_shape=[jax.ShapeDtypeStruct((t_len, d), F32), jax.ShapeDtypeStruct((t_len, d), BF16),
                            jax.ShapeDtypeStruct((1, d), F32)],
                 compiler_params=_params(1))(dh, x, g, dres)


def _loss_head(x, g, target):
    t_len, d = x.shape
    tm = _pick(t_len, (256,))

    def body(x_ref, g_ref, t_ref, dx_ref, dxb_ref, dg_ref, loss_ref):
        first = pl.program_id(0) == 0
        r, xhat = _rms_stats(x_ref[...])
        g_v = g_ref[...]
        err = xhat * g_v - t_ref[...]
        per_token = jnp.mean(err * err, axis=-1, keepdims=True)
        part = 0.5 * jnp.sum(per_token, axis=0, keepdims=True)
        _accumulate(loss_ref, jnp.broadcast_to(part, (1, LANES)), first)
        dy = err * (1.0 / d)
        _accumulate(dg_ref, jnp.sum(dy * xhat, axis=0, keepdims=True), first)
        gd = dy * g_v
        dx = r * (gd - xhat * jnp.mean(gd * xhat, axis=-1, keepdims=True))
        dx_ref[...] = dx
        dxb_ref[...] = dx.astype(BF16)

    row = pl.BlockSpec((tm, d), lambda i: (i, 0))
    vec = pl.BlockSpec((1, d), lambda i: (0, 0))
    return _call(body, name="loss_head", grid=(t_len // tm,), in_specs=[row, vec, row],
                 out_specs=[row, row, vec, pl.BlockSpec((1, LANES), lambda i: (0, 0))],
                 out_shape=[jax.ShapeDtypeStruct((t_len, d), F32), jax.ShapeDtypeStruct((t_len, d), BF16),
                            jax.ShapeDtypeStruct((1, d), F32), jax.ShapeDtypeStruct((1, LANES), F32)],
                 compiler_params=_params(1))(x, g, target)


def _ln_stats(x):
    mu = jnp.mean(x, axis=-1, keepdims=True)
    xc = x - mu
    r = lax.rsqrt(jnp.mean(xc * xc, axis=-1, keepdims=True) + NORM_EPS)
    return r, xc * r


def _ln_bwd(dy, r, xhat, g):
    dxh = dy * g
    return r * (dxh - jnp.mean(dxh, axis=-1, keepdims=True) - xhat * jnp.mean(dxh * xhat, axis=-1, keepdims=True))


def _rope_tables(positions):
    t_len = positions.shape[-1]
    inv_freq = 1.0 / (ROPE_THETA ** (jnp.arange(0, 2 * ROT_HALF, 2, dtype=F32) / (2 * ROT_HALF)))
    ang = positions.astype(F32).reshape(t_len, 1) * inv_freq
    cos = jnp.tile(jnp.cos(ang), (1, LANES // ROT_HALF))
    sin = jnp.tile(jnp.sin(ang), (1, LANES // ROT_HALF))
    lane = jnp.arange(LANES) % HEAD_DIM
    c = jnp.where(lane < 2 * ROT_HALF, cos, 1.0)
    s1 = jnp.where(lane < ROT_HALF, -sin, 0.0)
    s2 = jnp.where((lane >= ROT_HALF) & (lane < 2 * ROT_HALF), sin, 0.0)
    return c.astype(F32), s1.astype(F32), s2.astype(F32)


def _rope_fwd(p, tables):
    t_len = p.shape[0]
    tm = _pick(t_len, (256,))
    n_rot = K_END // LANES

    def body(p_ref, c_ref, s1_ref, s2_ref, o_ref):
        c, s1, s2 = c_ref[...], s1_ref[...], s2_ref[...]
        for j in range(V_END // LANES):
            sl = slice(j * LANES, (j + 1) * LANES)
            t = p_ref[:, sl]
            if j < n_rot:
                t = t * c + pltpu.roll(t, LANES - ROT_HALF, 1) * s1 + pltpu.roll(t, ROT_HALF, 1) * s2
            o_ref[:, sl] = t.astype(BF16)

    tab = pl.BlockSpec((tm, LANES), lambda i: (i, 0))
    blk = pl.BlockSpec((tm, V_END), lambda i: (i, 0))
    return _call(body, name="rope_fwd", grid=(t_len // tm,), in_specs=[blk, tab, tab, tab], out_specs=blk,
                 out_shape=jax.ShapeDtypeStruct((t_len, V_END), BF16), compiler_params=_params(1))(p, *tables)


def _assemble_dp(dq, dkc, dkp, dvc, dvp, tables, dalin, dagate, du, dvin):
    t_len = dq.shape[0]
    nb = t_len // BLK

    def body(dq_ref, dkc_ref, dkp_ref, dvc_ref, dvp_ref, c_ref, s1_ref, s2_ref, dalin_ref, dagate_ref,
             du_ref, dvin_ref, o_ref):
        keep = (pl.program_id(0) < nb - 1).astype(F32)
        c, s1, s2 = c_ref[...], s1_ref[...], s2_ref[...]

        def unrotate(dr):
            return dr * c + pltpu.roll(dr * s1, ROT_HALF, 1) + pltpu.roll(dr * s2, LANES - ROT_HALF, 1)

        for j in range(Q_END // LANES):
            sl = slice(j * LANES, (j + 1) * LANES)
            o_ref[:, sl] = unrotate(dq_ref[:, sl]).astype(BF16)
        for j in range((K_END - Q_END) // LANES):
            sl = slice(j * LANES, (j + 1) * LANES)
            dk = dkc_ref[:, sl] + keep * dkp_ref[:, sl]
            o_ref[:, Q_END + j * LANES:Q_END + (j + 1) * LANES] = unrotate(dk).astype(BF16)
            dv = dvc_ref[:, sl] + keep * dvp_ref[:, sl]
            o_ref[:, K_END + j * LANES:K_END + (j + 1) * LANES] = dv.astype(BF16)
        o_ref[:, V_END:V_END + CONV_CH] = dalin_ref[...]
        o_ref[:, V_END + CONV_CH:CONV_END] = dagate_ref[...]
        o_ref[:, CONV_END:CONV_END + SGU_CH] = du_ref[...]
        o_ref[:, CONV_END + SGU_CH:IN_COLS] = dvin_ref[...]

    def cur(w):
        return pl.BlockSpec((BLK, w), lambda i: (i, 0))

    def nxt(w):
        return pl.BlockSpec((BLK, w), lambda i: (jnp.minimum(i + 1, nb - 1), 0))

    kvw = K_END - Q_END
    return _call(body, name="assemble_dp", grid=(nb,),
                 in_specs=[cur(Q_END), cur(kvw), nxt(kvw), cur(kvw), nxt(kvw), cur(LANES), cur(LANES), cur(LANES),
                           cur(CONV_CH), cur(CONV_CH), cur(SGU_CH), cur(SGU_CH)],
                 out_specs=cur(IN_COLS), out_shape=jax.ShapeDtypeStruct((t_len, IN_COLS), BF16),
                 compiler_params=_params(1))(dq, dkc, dkp, dvc, dvp, *tables, dalin, dagate, du, dvin)


def _attn_probs(q_ref, kp_ref, kc_ref, snk_ref):
    n = pl.program_id(1)
    q = q_ref[...].reshape(GQ * BLK, HEAD_DIM)
    kk = jnp.concatenate([kp_ref[...], kc_ref[...]], axis=0)
    s = lax.dot_general(q, kk, NT, preferred_element_type=F32) * (HEAD_DIM ** -0.5)
    qi = jnp.bitwise_and(lax.broadcasted_iota(jnp.int32, (GQ * BLK, 2 * BLK), 0), BLK - 1)
    kj = lax.broadcasted_iota(jnp.int32, (GQ * BLK, 2 * BLK), 1)
    dist = qi + BLK - kj
    mask = (dist >= 0) & (dist < BLK) & ((kj >= BLK) | (n > 0))
    s = jnp.where(mask, s, -1e30)
    snk = snk_ref[...]
    m = jnp.maximum(jnp.max(s, axis=-1, keepdims=True), snk)
    e = jnp.exp(s - m)
    es = jnp.exp(snk - m)
    den = jnp.sum(e, axis=-1, keepdims=True) + es
    return q, kk, e / den, es / den


def _attn_specs(nb):
    q_spec = pl.BlockSpec((GQ, BLK, HEAD_DIM), lambda g, n: (g, n, 0))
    cur = pl.BlockSpec((None, BLK, HEAD_DIM), lambda g, n: (g, n, 0))
    prev = pl.BlockSpec((None, BLK, HEAD_DIM), lambda g, n: (g, jnp.maximum(n - 1, 0), 0))
    snk = pl.BlockSpec((None, GQ * BLK, 1), lambda g, n: (g, 0, 0))
    return q_spec, cur, prev, snk


def _attn_fwd(q, k, v, snk, comm=None):
    t_len = q.shape[1]
    nb = t_len // BLK

    def body(q_ref, kp_ref, kc_ref, vp_ref, vc_ref, snk_ref, o_ref):
        _, _, p, _ = _attn_probs(q_ref, kp_ref, kc_ref, snk_ref)
        vv = jnp.concatenate([vp_ref[...], vc_ref[...]], axis=0)
        o = lax.dot_general(p.astype(BF16), vv, NN, preferred_element_type=F32)
        o_ref[...] = o.reshape(GQ, BLK, HEAD_DIM).astype(BF16)

    q_spec, cur, prev, snk_spec = _attn_specs(nb)
    return _hosted_call(body, comm, "attn_fwd", (N_KV_HEADS, nb), [q, k, k, v, v, snk],
                        [q_spec, prev, cur, prev, cur, snk_spec], [jax.ShapeDtypeStruct(q.shape, BF16)], [q_spec])[0]


def _attn_bwd(q, k, v, snk, do, comm=None):
    t_len = q.shape[1]
    nb = t_len // BLK

    def body(q_ref, kp_ref, kc_ref, vp_ref, vc_ref, snk_ref, do_ref, dq_ref, dkp_ref, dkc_ref, dvp_ref, dvc_ref,
             dsnk_ref):
        q_v, kk, p, p_snk = _attn_probs(q_ref, kp_ref, kc_ref, snk_ref)
        vv = jnp.concatenate([vp_ref[...], vc_ref[...]], axis=0)
        do_v = do_ref[...].reshape(GQ * BLK, HEAD_DIM).astype(BF16)
        dp = lax.dot_general(do_v, vv, NT, preferred_element_type=F32)
        delta = jnp.sum(p * dp, axis=-1, keepdims=True)
        ds = (p * (dp - delta)).astype(BF16)
        scale = HEAD_DIM ** -0.5
        dq = lax.dot_general(ds, kk, NN, preferred_element_type=F32) * scale
        dq_ref[...] = dq.reshape(GQ, BLK, HEAD_DIM)
        dkk = lax.dot_general(ds, q_v, TN, preferred_element_type=F32) * scale
        dkp_ref[...] = dkk[:BLK]
        dkc_ref[...] = dkk[BLK:]
        dvv = lax.dot_general(p.astype(BF16), do_v, TN, preferred_element_type=F32)
        dvp_ref[...] = dvv[:BLK]
        dvc_ref[...] = dvv[BLK:]
        per_row = -p_snk * delta
        row = lax.broadcasted_iota(jnp.int32, (SUBLANES, LANES), 0)
        tile = jnp.zeros((SUBLANES, LANES), F32)
        for hh in range(GQ):
            tot = jnp.sum(per_row[hh * BLK:(hh + 1) * BLK], axis=0, keepdims=True)
            tile = tile + jnp.where(row == hh, tot, 0.0)
        _accumulate(dsnk_ref, tile, pl.program_id(1) == 0)

    q_spec, cur, prev, snk_spec = _attn_specs(nb)
    kv_shape = jax.ShapeDtypeStruct(k.shape, F32)
    return _hosted_call(
        body, comm, "attn_bwd", (N_KV_HEADS, nb), [q, k, k, v, v, snk, do],
        [q_spec, prev, cur, prev, cur, snk_spec, q_spec],
        [jax.ShapeDtypeStruct(q.shape, F32), kv_shape, kv_shape, kv_shape, kv_shape,
         jax.ShapeDtypeStruct((N_KV_HEADS, SUBLANES, LANES), F32)],
        [q_spec, cur, cur, cur, cur, pl.BlockSpec((None, SUBLANES, LANES), lambda g, n: (g, 0, 0))])


CONV_CHUNK = 256


def _shift_up(win, s):
    n = win.shape[0]
    return win if s == 0 else pltpu.roll(win, n - s, 0)


def _conv_col_specs(t_len):
    lin = pl.BlockSpec((t_len, LANES), lambda j: (0, V_END // LANES + j))
    gate = pl.BlockSpec((t_len, LANES), lambda j: (0, (V_END + CONV_CH) // LANES + j))
    col = pl.BlockSpec((t_len, LANES), lambda j: (0, j))
    wsp = pl.BlockSpec((CONV_PAD, LANES), lambda j: (0, j))
    return lin, gate, col, wsp


def _conv_fwd(p, w, b):
    t_len = p.shape[0]
    ch = CONV_CHUNK

    def body(lin_ref, gate_ref, w_ref, b_ref, y_ref, hp_ref):
        hp_ref[0:CONV_PAD, :] = jnp.zeros((CONV_PAD, LANES), F32)

        def fill(c, carry):
            r0 = pl.multiple_of(c * ch, ch)
            hp_ref[pl.ds(r0 + CONV_PAD, ch), :] = lin_ref[pl.ds(r0, ch), :] * _sigmoid(gate_ref[pl.ds(r0, ch), :])
            return carry

        lax.fori_loop(0, t_len // ch, fill, 0)

        def conv(c, carry):
            r0 = pl.multiple_of(c * ch, ch)
            win = hp_ref[pl.ds(r0, ch + CONV_PAD), :]
            acc = jnp.zeros((ch, LANES), F32)
            for k in range(CONV_WIDTH):
                acc = acc + _shift_up(win, CONV_PAD - (CONV_WIDTH - 1) + k)[:ch] * w_ref[k:k + 1, :]
            y_ref[pl.ds(r0, ch), :] = acc + b_ref[...]
            return carry

        lax.fori_loop(0, t_len // ch, conv, 0)

    lin, gate, col, wsp = _conv_col_specs(t_len)
    return _call(body, name="conv_fwd", grid=(CONV_CH // LANES,),
                 in_specs=[lin, gate, wsp, pl.BlockSpec((1, LANES), lambda j: (0, j))], out_specs=col,
                 out_shape=jax.ShapeDtypeStruct((t_len, CONV_CH), F32),
                 scratch_shapes=[pltpu.VMEM((t_len + CONV_PAD, LANES), F32)],
                 compiler_params=_params(1))(p, p, w, b)


def _conv_post_fwd(y, g, b):
    t_len = y.shape[0]
    tm = _pick(t_len, (512,))

    def body(y_ref, g_ref, b_ref, o_ref):
        _, xhat = _ln_stats(y_ref[...])
        z = xhat * g_ref[...] + b_ref[...]
        o_ref[...] = (z * _sigmoid(z)).astype(BF16)

    row = pl.BlockSpec((tm, CONV_CH), lambda i: (i, 0))
    vec = pl.BlockSpec((1, CONV_CH), lambda i: (0, 0))
    return _call(body, name="conv_post_fwd", grid=(t_len // tm,), in_specs=[row, vec, vec], out_specs=row,
                 out_shape=jax.ShapeDtypeStruct((t_len, CONV_CH), BF16), compiler_params=_params(1))(y, g, b)


def _conv_post_bwd(dmix, y, g, b):
    t_len = y.shape[0]
    tm = _pick(t_len, (512,))

    def body(do_ref, y_ref, g_ref, b_ref, dy_ref, dg_ref, db_ref, dcb_ref):
        first = pl.program_id(0) == 0
        r, xhat = _ln_stats(y_ref[...])
        g_v = g_ref[...]
        z = xhat * g_v + b_ref[...]
        sig = _sigmoid(z)
        dz = do_ref[...] * (sig * (1.0 + z * (1.0 - sig)))
        _accumulate(db_ref, jnp.sum(dz, axis=0, keepdims=True), first)
        _accumulate(dg_ref, jnp.sum(dz * xhat, axis=0, keepdims=True), first)
        dy = _ln_bwd(dz, r, xhat, g_v)
        dy_ref[...] = dy
        _accumulate(dcb_ref, jnp.sum(dy, axis=0, keepdims=True), first)

    row = pl.BlockSpec((tm, CONV_CH), lambda i: (i, 0))
    do_spec = pl.BlockSpec((tm, CONV_CH), lambda i: (i, Q_END // CONV_CH))
    vec = pl.BlockSpec((1, CONV_CH), lambda i: (0, 0))
    vshape = jax.ShapeDtypeStruct((1, CONV_CH), F32)
    return _call(body, name="conv_post_bwd", grid=(t_len // tm,), in_specs=[do_spec, row, vec, vec],
                 out_specs=[row, vec, vec, vec],
                 out_shape=[jax.ShapeDtypeStruct((t_len, CONV_CH), F32), vshape, vshape, vshape],
                 compiler_params=_params(1))(dmix, y, g, b)


def _conv_bwd(p, dy, w):
    t_len = p.shape[0]
    ch = CONV_CHUNK

    def body(lin_ref, gate_ref, dy_ref, w_ref, dlin_ref, dgate_ref, dw_ref, hp_ref, dyp_ref):
        hp_ref[0:CONV_PAD, :] = jnp.zeros((CONV_PAD, LANES), F32)
        dyp_ref[t_len:t_len + CONV_PAD, :] = jnp.zeros((CONV_PAD, LANES), F32)
        dw_ref[...] = jnp.zeros((CONV_PAD, LANES), F32)

        def fill(c, carry):
            r0 = pl.multiple_of(c * ch, ch)
            hp_ref[pl.ds(r0 + CONV_PAD, ch), :] = lin_ref[pl.ds(r0, ch), :] * _sigmoid(gate_ref[pl.ds(r0, ch), :])
            dyp_ref[pl.ds(r0, ch), :] = dy_ref[pl.ds(r0, ch), :]
            return carry

        lax.fori_loop(0, t_len // ch, fill, 0)

        def step(c, carry):
            r0 = pl.multiple_of(c * ch, ch)
            win_h = hp_ref[pl.ds(r0, ch + CONV_PAD), :]
            win_dy = dyp_ref[pl.ds(r0, ch + CONV_PAD), :]
            dyc = win_dy[:ch]
            dh = jnp.zeros((ch, LANES), F32)
            for k in range(CONV_WIDTH):
                tap = _shift_up(win_h, CONV_PAD - (CONV_WIDTH - 1) + k)[:ch]
                dw_ref[k:k + 1, :] += jnp.sum(dyc * tap, axis=0, keepdims=True)
                dh = dh + _shift_up(win_dy, CONV_WIDTH - 1 - k)[:ch] * w_ref[k:k + 1, :]
            lin = lin_ref[pl.ds(r0, ch), :]
            sig = _sigmoid(gate_ref[pl.ds(r0, ch), :])
            dlin_ref[pl.ds(r0, ch), :] = (dh * sig).astype(BF16)
            dgate_ref[pl.ds(r0, ch), :] = (dh * lin * (sig * (1.0 - sig))).astype(BF16)
            return carry

        lax.fori_loop(0, t_len // ch, step, 0)

    lin, gate, col, wsp = _conv_col_specs(t_len)
    half = jax.ShapeDtypeStruct((t_len, CONV_CH), BF16)
    return _call(body, name="conv_bwd", grid=(CONV_CH // LANES,), in_specs=[lin, gate, col, wsp],
                 out_specs=[col, col, wsp],
                 out_shape=[half, half, jax.ShapeDtypeStruct((CONV_PAD, CONV_CH), F32)],
                 scratch_shapes=[pltpu.VMEM((t_len + CONV_PAD, LANES), F32), pltpu.VMEM((t_len + CONV_PAD, LANES), F32)],
                 compiler_params=_params(1))(p, p, dy, w)


def _sgu_mixed(v, w_ref, bt_ref, j):
    lane = lax.broadcasted_iota(jnp.int32, (BLK, LANES), 1)
    lo = lane < HEAD_DIM
    tri = lax.broadcasted_iota(jnp.int32, (BLK, BLK), 0) >= lax.broadcasted_iota(jnp.int32, (BLK, BLK), 1)
    vs = v[:, j * LANES:(j + 1) * LANES]
    v_lo = jnp.where(lo, vs, 0.0).astype(BF16)
    v_hi = jnp.where(lo, 0.0, vs).astype(BF16)
    w_lo = jnp.where(tri, w_ref[2 * j], 0.0).astype(BF16)
    w_hi = jnp.where(tri, w_ref[2 * j + 1], 0.0).astype(BF16)
    m = (lax.dot_general(w_lo, v_lo, NN, preferred_element_type=F32)
         + lax.dot_general(w_hi, v_hi, NN, preferred_element_type=F32))
    bias = jnp.where(lo, bt_ref[:, 2 * j:2 * j + 1], bt_ref[:, 2 * j + 1:2 * j + 2])
    return m + bias, (v_lo, v_hi, w_lo, w_hi, lo, tri)


def _sgu_specs():
    u_spec = pl.BlockSpec((BLK, SGU_CH), lambda i: (i, CONV_END // SGU_CH))
    v_spec = pl.BlockSpec((BLK, SGU_CH), lambda i: (i, CONV_END // SGU_CH + 1))
    vec = pl.BlockSpec((1, SGU_CH), lambda i: (0, 0))
    w_spec = pl.BlockSpec((SGU_HEADS, BLK, BLK), lambda i: (0, 0, 0))
    bt_spec = pl.BlockSpec((BLK, SGU_HEADS), lambda i: (0, 0))
    row = pl.BlockSpec((BLK, SGU_CH), lambda i: (i, 0))
    return u_spec, v_spec, vec, w_spec, bt_spec, row


def _sgu_fwd(p, g, b, w, bt):
    t_len = p.shape[0]

    def body(u_ref, vin_ref, g_ref, b_ref, w_ref, bt_ref, o_ref):
        _, xhat = _ln_stats(vin_ref[...])
        v = xhat * g_ref[...] + b_ref[...]
        for j in range(SGU_CH // LANES):
            m, _ = _sgu_mixed(v, w_ref, bt_ref, j)
            sl = slice(j * LANES, (j + 1) * LANES)
            o_ref[:, sl] = (u_ref[:, sl] * m).astype(BF16)

    u_spec, v_spec, vec, w_spec, bt_spec, row = _sgu_specs()
    return _call(body, name="sgu_fwd", grid=(t_len // BLK,), in_specs=[u_spec, v_spec, vec, vec, w_spec, bt_spec],
                 out_specs=row, out_shape=jax.ShapeDtypeStruct((t_len, SGU_CH), BF16),
                 compiler_params=_params(1))(p, p, g, b, w, bt)


def _sgu_bwd(p, dmix, g, b, w, bt):
    t_len = p.shape[0]

    def body(u_ref, vin_ref, do_ref, g_ref, b_ref, w_ref, bt_ref, du_ref, dvin_ref, dw_ref, dbt_ref, dg_ref,
             db_ref, dv_ref):
        first = pl.program_id(0) == 0
        r, xhat = _ln_stats(vin_ref[...])
        g_v = g_ref[...]
        v = xhat * g_v + b_ref[...]
        lane = lax.broadcasted_iota(jnp.int32, (BLK, LANES), 1)
        dbt = jnp.zeros((BLK, LANES), F32)

        @pl.when(first)
        def _():
            dw_ref[...] = jnp.zeros((SGU_HEADS, BLK, BLK), F32)

        for j in range(SGU_CH // LANES):
            m, (v_lo, v_hi, w_lo, w_hi, lo, tri) = _sgu_mixed(v, w_ref, bt_ref, j)
            sl = slice(j * LANES, (j + 1) * LANES)
            do_v = do_ref[:, sl]
            du_ref[:, sl] = (do_v * m).astype(BF16)
            dm = do_v * u_ref[:, sl]
            dm_lo = jnp.where(lo, dm, 0.0)
            dm_hi = jnp.where(lo, 0.0, dm)
            dbt = dbt + jnp.where(lane == 2 * j, jnp.sum(dm_lo, axis=-1, keepdims=True), 0.0)
            dbt = dbt + jnp.where(lane == 2 * j + 1, jnp.sum(dm_hi, axis=-1, keepdims=True), 0.0)
            dm_lo, dm_hi = dm_lo.astype(BF16), dm_hi.astype(BF16)
            dw_ref[2 * j] += jnp.where(tri, lax.dot_general(dm_lo, v_lo, NT, preferred_element_type=F32), 0.0)
            dw_ref[2 * j + 1] += jnp.where(tri, lax.dot_general(dm_hi, v_hi, NT, preferred_element_type=F32), 0.0)
            dv_ref[:, sl] = (lax.dot_general(w_lo, dm_lo, TN, preferred_element_type=F32)
                             + lax.dot_general(w_hi, dm_hi, TN, preferred_element_type=F32))
        _accumulate(dbt_ref, dbt, first)
        dv = dv_ref[...]
        _accumulate(db_ref, jnp.sum(dv, axis=0, keepdims=True), first)
        _accumulate(dg_ref, jnp.sum(dv * xhat, axis=0, keepdims=True), first)
        dvin_ref[...] = _ln_bwd(dv, r, xhat, g_v).astype(BF16)

    u_spec, v_spec, vec, w_spec, bt_spec, row = _sgu_specs()
    do_spec = pl.BlockSpec((BLK, SGU_CH), lambda i: (i, (Q_END + CONV_CH) // SGU_CH))
    half = jax.ShapeDtypeStruct((t_len, SGU_CH), BF16)
    vshape = jax.ShapeDtypeStruct((1, SGU_CH), F32)
    return _call(body, name="sgu_bwd", grid=(t_len // BLK,),
                 in_specs=[u_spec, v_spec, do_spec, vec, vec, w_spec, bt_spec],
                 out_specs=[row, row, w_spec, pl.BlockSpec((BLK, LANES), lambda i: (0, 0)), vec, vec],
                 out_shape=[half, half, jax.ShapeDtypeStruct((SGU_HEADS, BLK, BLK), F32),
                            jax.ShapeDtypeStruct((BLK, LANES), F32), vshape, vshape],
                 scratch_shapes=[pltpu.VMEM((BLK, SGU_CH), F32)],
                 compiler_params=_params(1))(p, p, dmix, g, b, w, bt)


def _place():
    x, y, c = lax.axis_index("x"), lax.axis_index("y"), lax.axis_index("c")
    chips = [(1 - x, y), (x, 1 - y), (1 - x, 1 - y)]
    return x, y, c, chips


def _hbm_specs(n):
    return [pl.BlockSpec(memory_space=pltpu.HBM)] * n


def _comm_params():
    return pltpu.CompilerParams(has_side_effects=True)


def _remote(src, dst, send_sem, recv_sem, to):
    return pltpu.make_async_remote_copy(src_ref=src, dst_ref=dst, send_sem=send_sem, recv_sem=recv_sem,
                                        device_id=to, device_id_type=MESH_ID)


def _cast_place(w_local, chip):
    n, rows, cols = w_local.shape

    def body(chip_ref, w_ref, o_ref):
        o_ref[...] = w_ref[...].astype(BF16)

    grid_spec = pltpu.PrefetchScalarGridSpec(
        num_scalar_prefetch=1, grid=(n, rows // ROW_TILE),
        in_specs=[pl.BlockSpec((None, ROW_TILE, cols), lambda l, i, ch: (l, i, 0))],
        out_specs=pl.BlockSpec((None, None, ROW_TILE, cols), lambda l, i, ch: (l, ch[0], i, 0)))
    return _call(body, name="cast_place", grid_spec=grid_spec,
                 out_shape=jax.ShapeDtypeStruct((n, N_CHIPS, rows, cols), BF16), compiler_params=_params(2))(chip, w_local)


def _all_gather_weights(placed, shards):
    n_placed, nt = len(placed), len(placed) + len(shards)

    def body(*refs):
        ins, outs = refs[:nt], refs[nt:2 * nt]
        ici_send, ici_recv, d2d_send, d2d_recv, local_sem = refs[2 * nt:]
        x, y, c, chips = _place()
        me = 2 * x + y
        sibling = (x, y, 1 - c)
        local = [pltpu.make_async_copy(ins[t].at[l], outs[t].at[l, me], local_sem.at[2 * (t - n_placed) + l])
                 for t in range(n_placed, nt) for l in range(2)]
        for cp in local:
            cp.start()
        sends = []
        for t in range(nt):
            src = outs[t].at[c, me] if t < n_placed else ins[t].at[c]
            for j, (px, py) in enumerate(chips):
                sends.append(_remote(src, outs[t].at[c, me], ici_send.at[3 * t + j], ici_recv.at[3 * t + j],
                                     (px, py, c)))
        for cp in sends:
            cp.start()
        for t in range(nt):
            for j, (px, py) in enumerate(chips):
                slab = outs[t].at[c, 2 * px + py]
                _remote(slab, slab, ici_send.at[3 * t + j], ici_recv.at[3 * t + j], (px, py, c)).wait_recv()
                fwd = _remote(slab, slab, d2d_send.at[3 * t + j], d2d_recv.at[3 * t + j], sibling)
                fwd.start()
                sends.append(fwd)
        for t in range(nt):
            for j, (px, py) in enumerate(chips):
                slab = outs[t].at[1 - c, 2 * px + py]
                _remote(slab, slab, d2d_send.at[3 * t + j], d2d_recv.at[3 * t + j], sibling).wait_recv()
        for cp in sends:
            cp.wait_send()
        for cp in local:
            cp.wait()

    out_shape = [jax.ShapeDtypeStruct(p.shape, p.dtype) for p in placed]
    out_shape += [jax.ShapeDtypeStruct((2, N_CHIPS) + s.shape[1:], s.dtype) for s in shards]
    sems = [pltpu.SemaphoreType.DMA((3 * nt,))] * 4 + [pltpu.SemaphoreType.DMA((2 * len(shards),))]
    return _call(body, name="all_gather_weights", in_specs=_hbm_specs(nt), out_specs=_hbm_specs(nt),
                 out_shape=out_shape, scratch_shapes=sems, input_output_aliases={t: t for t in range(n_placed)},
                 compiler_params=_comm_params())(*placed, *shards)


def _gather_comm(bufs, pieces):
    n = len(pieces)
    sems = [pltpu.SemaphoreType.DMA((3 * n,))] * 4

    def rows(ref, layer, chip, r0, nr):
        return ref.at[layer, chip, pl.ds(r0, nr)]

    def start(rd, al, fr, sm):
        ici_send, ici_recv, _, _ = sm
        x, y, c, chips = _place()
        for i, (t, layer, r0, nr) in enumerate(pieces):
            @pl.when(c == layer)
            def _():
                own = rows(al[t], layer, 2 * x + y, r0, nr)
                for j, (px, py) in enumerate(chips):
                    _remote(own, own, ici_send.at[3 * i + j], ici_recv.at[3 * i + j], (px, py, c)).start()

    def finish(rd, al, fr, sm):
        ici_send, ici_recv, d2d_send, d2d_recv = sm
        x, y, c, chips = _place()
        sibling = (x, y, 1 - c)
        for i, (t, layer, r0, nr) in enumerate(pieces):
            @pl.when(c == layer)
            def _():
                passed = []
                for j, (px, py) in enumerate(chips):
                    got = rows(al[t], layer, 2 * px + py, r0, nr)
                    _remote(got, got, ici_send.at[3 * i + j], ici_recv.at[3 * i + j], (px, py, c)).wait_recv()
                    fwd = _remote(got, got, d2d_send.at[3 * i + j], d2d_recv.at[3 * i + j], sibling)
                    fwd.start()
                    passed.append(fwd)
                own = rows(al[t], layer, 2 * x + y, r0, nr)
                for j, (px, py) in enumerate(chips):
                    _remote(own, own, ici_send.at[3 * i + j], ici_recv.at[3 * i + j], (px, py, c)).wait_send()
                for fwd in passed:
                    fwd.wait_send()

            @pl.when(c != layer)
            def _():
                for j, (px, py) in enumerate(chips):
                    got = rows(al[t], layer, 2 * px + py, r0, nr)
                    _remote(got, got, d2d_send.at[3 * i + j], d2d_recv.at[3 * i + j], sibling).wait_recv()

    return _Comm([], bufs, [], sems, start, finish)


def _to_owner_comm(stacks, layer):
    nt = len(stacks)
    sems = [pltpu.SemaphoreType.DMA((nt,))] * 2
    fresh = [jax.ShapeDtypeStruct(s.shape[1:], s.dtype) for s in stacks]

    def copies(rd, fr, sm):
        x, y, c, _ = _place()
        return c, [_remote(rd[t].at[layer], fr[t], sm[0].at[t], sm[1].at[t], (x, y, 1 - c)) for t in range(nt)]

    def start(rd, al, fr, sm):
        c, cps = copies(rd, fr, sm)

        @pl.when(c != layer)
        def _():
            for cp in cps:
                cp.start()

    def finish(rd, al, fr, sm):
        c, cps = copies(rd, fr, sm)

        @pl.when(c != layer)
        def _():
            for cp in cps:
                cp.wait_send()

        @pl.when(c == layer)
        def _():
            for cp in cps:
                cp.wait_recv()

    return _Comm(stacks, [], fresh, sems, start, finish)


def _chip_comm(partials):
    nt = len(partials)
    sems = [pltpu.SemaphoreType.DMA((3 * nt,))] * 2
    fresh = [jax.ShapeDtypeStruct((3,) + p.shape[1:], p.dtype) for p, _ in partials]

    def each(rd, fr, sm, act):
        x, y, c, chips = _place()
        for t, (_, layer) in enumerate(partials):
            @pl.when(c == layer)
            def _():
                for j, (px, py) in enumerate(chips):
                    act(_remote(rd[t].at[2 * px + py], fr[t].at[j], sm[0].at[3 * t + j], sm[1].at[3 * t + j],
                                (px, py, c)))

    def start(rd, al, fr, sm):
        each(rd, fr, sm, lambda cp: cp.start())

    def finish(rd, al, fr, sm):
        each(rd, fr, sm, lambda cp: cp.wait())

    return _Comm([p for p, _ in partials], [], fresh, sems, start, finish)


def _from_owner_comm(finals, layer):
    nt = len(finals)
    sems = [pltpu.SemaphoreType.DMA((nt,))] * 2

    def copies(al, sm):
        x, y, c, _ = _place()
        return c, [_remote(al[t].at[layer], al[t].at[layer], sm[0].at[t], sm[1].at[t], (x, y, 1 - c))
                   for t in range(nt)]

    def start(rd, al, fr, sm):
        c, cps = copies(al, sm)

        @pl.when(c == layer)
        def _():
            for cp in cps:
                cp.start()

    def finish(rd, al, fr, sm):
        c, cps = copies(al, sm)

        @pl.when(c == layer)
        def _():
            for cp in cps:
                cp.wait_send()

        @pl.when(c != layer)
        def _():
            for cp in cps:
                cp.wait_recv()

    return _Comm([], finals, [], sems, start, finish)


def _all_reduce_small(buf):
    rows = buf.shape[0]

    def body(x_ref, out_ref, all_ref, send_sems, recv_sems, local_sem):
        x, y, c, chips = _place()
        me, sibling = (x, y, c), (x, y, 1 - c)

        def block(px, py, pc):
            return all_ref.at[pl.ds((4 * px + 2 * py + pc) * rows, rows), :]

        def copy(k, blk, to, src=None):
            return _remote(block(*blk) if src is None else src, block(*blk), send_sems.at[k], recv_sems.at[k], to)

        mine = pltpu.make_async_copy(x_ref, block(*me), local_sem)
        mine.start()
        first = [copy(0, me, sibling, src=x_ref)]
        first += [copy(1 + j, me, (*chip, c), src=x_ref) for j, chip in enumerate(chips)]
        for cp in first:
            cp.start()
        passed = [copy(4 + j, (*chip, c), sibling) for j, chip in enumerate(chips)]
        for j, chip in enumerate(chips):
            copy(1 + j, (*chip, c), me).wait_recv()
            passed[j].start()
        copy(0, sibling, me).wait_recv()
        for j, chip in enumerate(chips):
            copy(4 + j, (*chip, 1 - c), me).wait_recv()
        for cp in first + passed:
            cp.wait_send()
        mine.wait()
        tot = all_ref[0:rows, :]
        for k in range(1, N_DEV):
            tot = tot + all_ref[k * rows:(k + 1) * rows, :]
        out_ref[...] = tot

    vm = pl.BlockSpec(memory_space=pltpu.VMEM)
    return _call(body, name="all_reduce_small", in_specs=[vm], out_specs=vm,
                 out_shape=jax.ShapeDtypeStruct(buf.shape, F32),
                 scratch_shapes=[pltpu.VMEM((N_DEV * rows, LANES), F32), pltpu.SemaphoreType.DMA((7,)),
                                 pltpu.SemaphoreType.DMA((7,)), pltpu.SemaphoreType.DMA],
                 compiler_params=pltpu.CompilerParams(has_side_effects=True,
                                                      vmem_limit_bytes=V7X_VMEM_LIMIT_BYTES))(buf)


ROW_TILE = 128


def _chip_partial(stack, received, layer):
    _, _, rows, cols = stack.shape

    def body(a_ref, b_ref, o_ref):
        o_ref[...] = (a_ref[...].astype(F32) + b_ref[...].astype(F32)).astype(BF16)

    blk = pl.BlockSpec((None, ROW_TILE, cols), lambda s, i: (s, i, 0))
    return _call(body, name="chip_partial", grid=(N_CHIPS, rows // ROW_TILE),
                 in_specs=[pl.BlockSpec((None, None, ROW_TILE, cols), lambda s, i: (layer, s, i, 0)), blk],
                 out_specs=blk, out_shape=jax.ShapeDtypeStruct(received.shape, BF16),
                 compiler_params=_params(2))(stack, received)


def _final_sum(partial, from_chips, layer, chip, finals):
    _, rows, cols = partial.shape

    def body(chip_ref, a_ref, r_ref, *rest):
        o_ref = rest[-1]
        tot = a_ref[...].astype(F32)
        for j in range(3):
            tot = tot + r_ref[j].astype(F32)
        o_ref[...] = tot

    in_specs = [pl.BlockSpec((None, ROW_TILE, cols), lambda i, ch: (ch[0], i, 0)),
                pl.BlockSpec((3, ROW_TILE, cols), lambda i, ch: (0, i, 0))]
    args = [chip, partial, from_chips]
    kw = {}
    if finals is not None:
        in_specs.append(pl.BlockSpec(memory_space=pl.ANY))
        args.append(finals)
        kw["input_output_aliases"] = {3: 0}
    grid_spec = pltpu.PrefetchScalarGridSpec(
        num_scalar_prefetch=1, grid=(rows // ROW_TILE,), in_specs=in_specs,
        out_specs=pl.BlockSpec((None, ROW_TILE, cols), lambda i, ch: (layer, i, 0)))
    return _call(body, name="final_sum", grid_spec=grid_spec, out_shape=jax.ShapeDtypeStruct((2, rows, cols), F32),
                 compiler_params=_params(1), **kw)(*args)


def _adamw(w, g, m, v):
    n, rows, cols = w.shape
    tr = _pick(rows, (ROW_TILE, SUBLANES))
    c1 = 1.0 - ADAM_B1 ** ADAM_STEP
    c2 = 1.0 - ADAM_B2 ** ADAM_STEP

    def body(w_ref, g_ref, m_ref, v_ref, d_ref, nm_ref, nv_ref, go_ref):
        g_v = g_ref[...]
        go_ref[...] = g_v
        nm = ADAM_B1 * m_ref[...] + (1.0 - ADAM_B1) * g_v
        nv = ADAM_B2 * v_ref[...] + (1.0 - ADAM_B2) * (g_v * g_v)
        nm_ref[...] = nm
        nv_ref[...] = nv
        d_ref[...] = -ADAM_LR * ((nm / c1) / (jnp.sqrt(nv / c2) + ADAM_EPS) + ADAM_WD * w_ref[...])

    blk = pl.BlockSpec((None, tr, cols), lambda l, i: (l, i, 0))
    shape = jax.ShapeDtypeStruct(w.shape, F32)
    return _call(body, name="adamw", grid=(n, rows // tr), in_specs=[blk] * 4, out_specs=[blk] * 4,
                 out_shape=[shape] * 4, compiler_params=_params(2))(w, g, m, v)


def _to_heads(a, n_heads):
    t_len = a.shape[0]
    return a.reshape(t_len, n_heads, HEAD_DIM).transpose(1, 0, 2)


def _from_heads(a):
    n_heads, t_len, _ = a.shape
    return a.transpose(1, 0, 2).reshape(t_len, n_heads * HEAD_DIM)


class _Schedule:
    def __init__(self):
        self.sites = {}
        self.open = []

    def add(self, site, make, done=None):
        self.sites.setdefault(site, []).append((make, done))

    def begin(self, site):
        self.open = [(make(), done) for make, done in self.sites.pop(site, [])]
        return _merge_comms([cm for cm, _ in self.open])

    def end(self):
        for cm, done in self.open:
            if done is not None:
                done(cm)
        self.open = []


def _ffn_fwd(x, gain, wg, which, layer, sched):
    w_in_name, w_out_name = f"ffn{which}_w_in", f"ffn{which}_w_out"
    h = _rmsnorm_fwd(x, gain)
    comm = sched.begin(("ffn_in", layer, which))
    gu, act = _mm_ffn_in(h, wg[w_in_name], layer, comm)
    sched.end()
    comm = sched.begin(("ffn_out", layer, which))
    x_new = _mm_out_res("ffn_out", act, wg[w_out_name], layer, x, FFN_RESIDUAL_WEIGHT, comm)[0]
    sched.end()
    return x_new, (x, h, gu, act)


def _ffn_bwd(dx, dxb, saved, gain, wg, which, layer, stacks, sched):
    w_in_name, w_out_name = f"ffn{which}_w_in", f"ffn{which}_w_out"
    x, h, gu, act = saved
    comm = sched.begin(("ffn_dact", layer, which))
    dgu = _mm_dact_swiglu(dxb, wg[w_out_name], layer, gu, FFN_RESIDUAL_WEIGHT, comm)
    sched.end()
    comm = sched.begin(("ffn_dw_out", layer, which))
    stacks[w_out_name] = _dw_rows("ffn_dw_out", act, dxb, layer, stacks[w_out_name], FFN_RESIDUAL_WEIGHT, comm)
    sched.end()
    comm = sched.begin(("ffn_dh", layer, which))
    dh = _mm_dh_ffn(dgu, wg[w_in_name], layer, comm)
    sched.end()
    comm = sched.begin(("ffn_dw_in", layer, which))
    stacks[w_in_name] = _dw_ffn_in(h, dgu, layer, stacks[w_in_name], comm)
    sched.end()
    return _rmsnorm_bwd(dh, x, gain, dx)


def _mix_fwd(x, gain, wg, layer, small, tables, sched):
    h = _rmsnorm_fwd(x, gain)
    comm = sched.begin(("mix_in", layer))
    p = _mm_proj(h, wg["w_in"], layer, comm)[0]
    sched.end()
    qkv = _rope_fwd(p, tables)
    q = _to_heads(qkv[:, :Q_END], N_Q_HEADS)
    k = _to_heads(qkv[:, Q_END:K_END], N_KV_HEADS)
    v = _to_heads(qkv[:, K_END:V_END], N_KV_HEADS)
    comm = sched.begin(("attn", layer))
    attn = _from_heads(_attn_fwd(q, k, v, small["snk"], comm))
    sched.end()
    y = _conv_fwd(p, small["conv_w"], small["conv_b"])
    conv = _conv_post_fwd(y, small["conv_ln_g"], small["conv_ln_b"])
    sgu = _sgu_fwd(p, small["sgu_ln_g"], small["sgu_ln_b"], small["sgu_w"], small["sgu_bt"])
    mix = jnp.concatenate([attn, conv, sgu], axis=1)
    x_new = _mm_out_res("mix_out", mix, wg["w_out"], layer, x, 1.0)[0]
    return x_new, (x, h, p, q, k, v, y, mix)


def _mix_bwd(dx, dxb, saved, gain, wg, small, tables, layer, stacks, sched):
    x, h, p, q, k, v, y, mix = saved
    dmix = _mm_dmix(dxb, wg["w_out"], layer)
    stacks["w_out"] = _dw_rows("mix_dw_out", mix, dxb, layer, stacks["w_out"], 1.0)
    do = _to_heads(dmix[:, :Q_END], N_Q_HEADS)
    comm = sched.begin(("attn_bwd", layer))
    dq, dkp, dkc, dvp, dvc, dsnk = _attn_bwd(q, k, v, small["snk"], do, comm)
    sched.end()
    dy, d_ln_g, d_ln_b, d_conv_b = _conv_post_bwd(dmix, y, small["conv_ln_g"], small["conv_ln_b"])
    dalin, dagate, d_conv_w = _conv_bwd(p, dy, small["conv_w"])
    du, dvin, d_sgu_w, d_sgu_bt, d_sgu_g, d_sgu_b = _sgu_bwd(p, dmix, small["sgu_ln_g"], small["sgu_ln_b"],
                                                           small["sgu_w"], small["sgu_bt"])
    dp = _assemble_dp(_from_heads(dq), _from_heads(dkc), _from_heads(dkp), _from_heads(dvc), _from_heads(dvp),
                      tables, dalin, dagate, du, dvin)
    dh = _mm_dh_mix(dp, wg["w_in"], layer)
    stacks["w_in"] = _dw_mix_in(h, dp, layer, stacks["w_in"])
    dx_in, dxb_in, dgain = _rmsnorm_bwd(dh, x, gain, dx)
    grads = {
        "norm_mix": dgain[0], "conv_dw_w": d_conv_w[:CONV_WIDTH], "conv_dw_b": d_conv_b[0],
        "conv_ln_g": d_ln_g[0], "conv_ln_b": d_ln_b[0], "sgu_ln_g": d_sgu_g[0], "sgu_ln_b": d_sgu_b[0],
        "sgu_w": d_sgu_w, "sgu_b": d_sgu_bt[:, :SGU_HEADS].T, "attn_sinks": dsnk[:, :GQ, 0].reshape(N_Q_HEADS),
    }
    return dx_in, dxb_in, grads


BIG = ("ffn1_w_in", "ffn1_w_out", "w_in", "w_out", "ffn2_w_in", "ffn2_w_out")
SMALL = ("norm_ffn1", "norm_mix", "conv_dw_w", "conv_dw_b", "conv_ln_g", "conv_ln_b", "sgu_ln_g", "sgu_ln_b",
         "sgu_w", "sgu_b", "attn_sinks", "norm_ffn2", "final_norm")
WEIGHTS = ("norm_ffn1", "ffn1_w_in", "ffn1_w_out", "norm_mix", "w_in", "conv_dw_w", "conv_dw_b", "conv_ln_g",
           "conv_ln_b", "sgu_ln_g", "sgu_ln_b", "sgu_w", "sgu_b", "attn_sinks", "w_out", "norm_ffn2", "ffn2_w_in",
           "ffn2_w_out", "final_norm")
PACK_ROWS = SUBLANES * LANES

FIRST_GATHER = [("ffn1_w_in", 0, None)]
FORWARD_PLAN = {
    ("ffn_in", 0, 1): [("ffn1_w_out", 0, None), ("w_in", 0, None)],
    ("ffn_out", 0, 1): [("ffn2_w_in", 0, 0)],
    ("mix_in", 0): [("w_out", 0, None)],
    ("attn", 0): [("ffn2_w_in", 0, 1)],
    ("ffn_in", 0, 2): [("ffn2_w_out", 0, None), ("ffn1_w_in", 1, 0)],
    ("ffn_out", 0, 2): [("ffn1_w_in", 1, 1)],
    ("ffn_in", 1, 1): [("ffn1_w_out", 1, None), ("w_in", 1, None)],
    ("ffn_out", 1, 1): [("ffn2_w_in", 1, 0)],
    ("mix_in", 1): [("w_out", 1, None)],
    ("attn", 1): [("ffn2_w_in", 1, 1)],
    ("ffn_in", 1, 2): [("ffn2_w_out", 1, None)],
}
BACKWARD_PLAN = {
    "to_owner": ("ffn_dact", 0, 2),
    "between_chips": {
        ("ffn_dw_out", 0, 2): ["ffn2_w_out"],
        ("ffn_dh", 0, 2): ["ffn2_w_in"],
        ("ffn_dw_in", 0, 2): ["ffn1_w_in"],
        ("attn_bwd", 0): ["ffn1_w_out", "w_in", "w_out"],
    },
    "sum_after": ("attn_bwd", 0),
    "from_owner": ("ffn_dh", 0, 1),
}


def _pack(arrays):
    flat = jnp.concatenate([a.reshape(-1).astype(F32) for a in arrays])
    pad = (-flat.shape[0]) % PACK_ROWS
    return jnp.pad(flat, (0, pad)).reshape(-1, LANES)


def _unpack(buf, shapes):
    flat = buf.reshape(-1)
    out, off = [], 0
    for s in shapes:
        n = 1
        for d in s:
            n *= d
        out.append(flat[off:off + n].reshape(s))
        off += n
    return out


def kernel(x, positions, norm_ffn1, ffn1_w_in, ffn1_w_out, norm_mix, w_in, conv_dw_w, conv_dw_b, conv_ln_g, conv_ln_b, sgu_ln_g, sgu_ln_b, sgu_w, sgu_b, attn_sinks, w_out, norm_ffn2, ffn2_w_in, ffn2_w_out, final_norm, loss_target, m_norm_ffn1, m_ffn1_w_in, m_ffn1_w_out, m_norm_mix, m_w_in, m_conv_dw_w, m_conv_dw_b, m_conv_ln_g, m_conv_ln_b, m_sgu_ln_g, m_sgu_ln_b, m_sgu_w, m_sgu_b, m_attn_sinks, m_w_out, m_norm_ffn2, m_ffn2_w_in, m_ffn2_w_out, m_final_norm, v_norm_ffn1, v_ffn1_w_in, v_ffn1_w_out, v_norm_mix, v_w_in, v_conv_dw_w, v_conv_dw_b, v_conv_ln_g, v_conv_ln_b, v_sgu_ln_g, v_sgu_ln_b, v_sgu_w, v_sgu_b, v_attn_sinks, v_w_out, v_norm_ffn2, v_ffn2_w_in, v_ffn2_w_out, v_final_norm):
    w = dict(norm_ffn1=norm_ffn1, ffn1_w_in=ffn1_w_in, ffn1_w_out=ffn1_w_out, norm_mix=norm_mix, w_in=w_in,
             conv_dw_w=conv_dw_w, conv_dw_b=conv_dw_b, conv_ln_g=conv_ln_g, conv_ln_b=conv_ln_b, sgu_ln_g=sgu_ln_g,
             sgu_ln_b=sgu_ln_b, sgu_w=sgu_w, sgu_b=sgu_b, attn_sinks=attn_sinks, w_out=w_out, norm_ffn2=norm_ffn2,
             ffn2_w_in=ffn2_w_in, ffn2_w_out=ffn2_w_out, final_norm=final_norm)
    m = dict(norm_ffn1=m_norm_ffn1, ffn1_w_in=m_ffn1_w_in, ffn1_w_out=m_ffn1_w_out, norm_mix=m_norm_mix, w_in=m_w_in,
             conv_dw_w=m_conv_dw_w, conv_dw_b=m_conv_dw_b, conv_ln_g=m_conv_ln_g, conv_ln_b=m_conv_ln_b,
             sgu_ln_g=m_sgu_ln_g, sgu_ln_b=m_sgu_ln_b, sgu_w=m_sgu_w, sgu_b=m_sgu_b, attn_sinks=m_attn_sinks,
             w_out=m_w_out, norm_ffn2=m_norm_ffn2, ffn2_w_in=m_ffn2_w_in, ffn2_w_out=m_ffn2_w_out,
             final_norm=m_final_norm)
    v = dict(norm_ffn1=v_norm_ffn1, ffn1_w_in=v_ffn1_w_in, ffn1_w_out=v_ffn1_w_out, norm_mix=v_norm_mix, w_in=v_w_in,
             conv_dw_w=v_conv_dw_w, conv_dw_b=v_conv_dw_b, conv_ln_g=v_conv_ln_g, conv_ln_b=v_conv_ln_b,
             sgu_ln_g=v_sgu_ln_g, sgu_ln_b=v_sgu_ln_b, sgu_w=v_sgu_w, sgu_b=v_sgu_b, attn_sinks=v_attn_sinks,
             w_out=v_w_out, norm_ffn2=v_norm_ffn2, ffn2_w_in=v_ffn2_w_in, ffn2_w_out=v_ffn2_w_out,
             final_norm=v_final_norm)
    depth = norm_ffn1.shape[0]
    assert depth == 2 and x.shape[0] == 1
    xc = lax.axis_index("x")
    yc = lax.axis_index("y")
    cc = lax.axis_index("c")
    chip = 2 * xc + yc

    chip_arr = chip.reshape(1).astype(jnp.int32)
    wg = {n: _cast_place(w[n], chip_arr) for n in BIG}
    sched = _Schedule()

    def gather(pieces):
        names = sorted({n for n, _, _ in pieces})
        half = w["ffn1_w_in"].shape[1] // 2

        def make():
            rows = lambda n, part: (0, wg[n].shape[2]) if part is None else (part * half, half)
            return _gather_comm([wg[n] for n in names], [(names.index(n), l, *rows(n, part)) for n, l, part in pieces])

        return make, lambda cm: wg.update(zip(names, cm.aliased_out))

    make, done = gather(FIRST_GATHER)
    first = make()
    _standalone("gather_first", first)
    done(first)
    for site, pieces in FORWARD_PLAN.items():
        sched.add(site, *gather(pieces))
    conv_w_full = _all_gather_weights([], [conv_dw_w])[0].transpose(0, 2, 1, 3).reshape(depth, CONV_WIDTH, CONV_CH)
    conv_w_full = jnp.pad(conv_w_full, ((0, 0), (0, CONV_PAD - CONV_WIDTH), (0, 0)))

    tables = _rope_tables(positions)
    small = []
    for l in range(depth):
        small.append(dict(
            snk=jnp.broadcast_to(attn_sinks[l].reshape(N_KV_HEADS, GQ, 1, 1), (N_KV_HEADS, GQ, BLK, 1)).reshape(
                N_KV_HEADS, GQ * BLK, 1),
            conv_w=conv_w_full[l], conv_b=conv_dw_b[l][None], conv_ln_g=conv_ln_g[l][None],
            conv_ln_b=conv_ln_b[l][None], sgu_ln_g=sgu_ln_g[l][None], sgu_ln_b=sgu_ln_b[l][None], sgu_w=sgu_w[l],
            sgu_bt=sgu_b[l].T))

    xs = x[0]
    saved = []
    for l in range(depth):
        xs, s1 = _ffn_fwd(xs, norm_ffn1[l][None], wg, 1, l, sched)
        xs, s2 = _mix_fwd(xs, norm_mix[l][None], wg, l, small[l], tables, sched)
        xs, s3 = _ffn_fwd(xs, norm_ffn2[l][None], wg, 2, l, sched)
        saved.append((s1, s2, s3))
    dx, dxb, d_final, loss_part = _loss_head(xs, final_norm[None], loss_target[0])

    stacks = {n: None for n in BIG}
    partials, from_chips = {}, {}
    finals = {n: None for n in BIG}

    def to_owner(layer):
        def done(cm):
            for n, received in zip(BIG, cm.fresh_out):
                partials[n, layer] = _chip_partial(stacks[n], received, layer)

        return lambda: _to_owner_comm([stacks[n] for n in BIG], layer), done

    def between_chips(layer, names):
        def done(cm):
            from_chips.update({(n, layer): r for n, r in zip(names, cm.fresh_out)})

        return lambda: _chip_comm([(partials[n, layer], layer) for n in names]), done

    def sum_up(layer):
        for n in BIG:
            finals[n] = _final_sum(partials[n, layer], from_chips[n, layer], layer, chip_arr, finals[n])

    def from_owner(layer):
        return (lambda: _from_owner_comm([finals[n] for n in BIG], layer),
                lambda cm: finals.update(zip(BIG, cm.aliased_out)))

    plan = BACKWARD_PLAN
    sched.add(plan["to_owner"], *to_owner(1))
    for site, names in plan["between_chips"].items():
        sched.add(site, *between_chips(1, names))
    sched.add(plan["sum_after"], lambda: None, lambda cm: sum_up(1))
    sched.add(plan["from_owner"], *from_owner(1))

    small_grads = [None] * depth
    for l in reversed(range(depth)):
        s1, s2, s3 = saved[l]
        dx, dxb, dg2 = _ffn_bwd(dx, dxb, s3, norm_ffn2[l][None], wg, 2, l, stacks, sched)
        dx, dxb, gm = _mix_bwd(dx, dxb, s2, norm_mix[l][None], wg, small[l], tables, l, stacks, sched)
        dx, dxb, dg1 = _ffn_bwd(dx, dxb, s1, norm_ffn1[l][None], wg, 1, l, stacks, sched)
        gm["norm_ffn1"] = dg1[0]
        gm["norm_ffn2"] = dg2[0]
        small_grads[l] = gm
    grad_x = dx[None]
    assert not sched.sites, sched.sites

    for name, (make, done) in (("to_owner", to_owner(0)), ("between_chips", between_chips(0, BIG))):
        cm = make()
        _standalone(name, cm)
        done(cm)
    sum_up(0)
    make, done = from_owner(0)
    cm = make()
    _standalone("from_owner", cm)
    done(cm)
    big_grads = dict(finals)

    per_layer = [n for n in SMALL if n != "final_norm"]
    small_local = [jnp.stack([small_grads[l][n] for l in range(depth)]) for n in per_layer]
    small_local += [d_final[0], loss_part[0, :1]]
    small_shapes = [a.shape for a in small_local]
    summed = _unpack(_all_reduce_small(_pack(small_local)), small_shapes)
    loss = summed[-1][0]
    sg = dict(zip(per_layer + ["final_norm"], summed[:-1]))
    sg["conv_dw_w"] = lax.dynamic_slice_in_dim(sg["conv_dw_w"], chip * LANES, LANES, axis=2)

    delta, new_m, new_v = {}, {}, {}
    for n in BIG:
        delta[n], new_m[n], new_v[n], big_grads[n] = _adamw(w[n], big_grads[n], m[n], v[n])
    shapes = [w[n].shape for n in SMALL]
    packed = [_pack([d[n] for n in SMALL])[None] for d in (w, sg, m, v)]
    outs = _adamw(*packed)
    for d, buf in zip((delta, new_m, new_v), outs[:3]):
        d.update(zip(SMALL, _unpack(buf[0], shapes)))
    grads = {**big_grads, **sg}
    return (loss, grad_x, *[grads[n] for n in WEIGHTS], *[delta[n] for n in WEIGHTS],
            *[new_m[n] for n in WEIGHTS], *[new_v[n] for n in WEIGHTS])
```

```python
import functools

import jax
import jax.numpy as jnp
from jax import lax
from jax.experimental import pallas as pl
from jax.experimental.pallas import tpu as pltpu

F32 = jnp.float32
BF16 = jnp.bfloat16
MESH_ID = pl.DeviceIdType.MESH

V7X_VMEM_LIMIT_BYTES = 56 * 2**20
LANES = 128
SUBLANES = 8

HEAD_DIM = 64
N_Q_HEADS = 16
N_KV_HEADS = 4
GQ = N_Q_HEADS // N_KV_HEADS
BLK = 128
ROT_HALF = 8
ROPE_THETA = 500000.0
CONV_WIDTH = 31
CONV_PAD = 32
CONV_CH = 512
SGU_CH = 512
SGU_HEADS = 8
Q_END = N_Q_HEADS * HEAD_DIM
K_END = Q_END + N_KV_HEADS * HEAD_DIM
V_END = K_END + N_KV_HEADS * HEAD_DIM
CONV_END = V_END + 2 * CONV_CH
IN_COLS = CONV_END + 2 * SGU_CH
NORM_EPS = 1e-5
FFN_RESIDUAL_WEIGHT = 0.5
N_CHIPS = 4
N_DEV = 8

ADAM_LR = 0.001
ADAM_B1 = 0.9
ADAM_B2 = 0.999
ADAM_EPS = 1e-08
ADAM_WD = 0.01
ADAM_STEP = 10

NN = (((1,), (0,)), ((), ()))
NT = (((1,), (1,)), ((), ()))
TN = (((0,), (0,)), ((), ()))


def _pick(n, cands):
    for c in cands:
        if n % c == 0:
            return c
    raise ValueError(f"no tile of {cands} divides {n}")


def _params(n_axes):
    return pltpu.CompilerParams(dimension_semantics=("arbitrary",) * n_axes, vmem_limit_bytes=V7X_VMEM_LIMIT_BYTES)


def _call(body, **kw):
    return pl.pallas_call(body, **kw)


def _sigmoid(x):
    return 1.0 / (1.0 + jnp.exp(-x))


class _Comm:
    def __init__(self, reads, aliased, fresh, sems, start, finish):
        self.reads, self.aliased, self.fresh, self.sems = list(reads), list(aliased), list(fresh), list(sems)
        self.start, self.finish = start, finish
        self.aliased_out, self.fresh_out = None, None


def _merge_comms(comms):
    comms = [cm for cm in comms if cm is not None]
    if not comms:
        return None
    if len(comms) == 1:
        return comms[0]

    def split(refs, counts):
        out, off = [], 0
        for n in counts:
            out.append(refs[off:off + n])
            off += n
        return out

    def run(which):
        def f(rd, al, fr, sm):
            parts = zip(split(rd, [len(cm.reads) for cm in comms]), split(al, [len(cm.aliased) for cm in comms]),
                        split(fr, [len(cm.fresh) for cm in comms]), split(sm, [len(cm.sems) for cm in comms]))
            for cm, (r, a, f_, s) in zip(comms, parts):
                getattr(cm, which)(r, a, f_, s)
        return f

    merged = _Comm(sum((cm.reads for cm in comms), []), sum((cm.aliased for cm in comms), []),
                   sum((cm.fresh for cm in comms), []), sum((cm.sems for cm in comms), []), run("start"), run("finish"))
    merged.parts = comms
    return merged


def _hosted_call(body, comm, name, grid, inputs, in_specs, out_shape, out_specs, scratch_shapes=(), aliases=None):
    n_in, n_out, n_scr = len(inputs), len(out_shape), len(scratch_shapes)
    aliases = dict(aliases or {})
    if comm is None:
        return _call(body, name=name, grid=grid, in_specs=list(in_specs), out_specs=list(out_specs),
                     out_shape=list(out_shape), scratch_shapes=list(scratch_shapes), input_output_aliases=aliases,
                     compiler_params=_params(len(grid)))(*inputs)
    nr, na, nf = len(comm.reads), len(comm.aliased), len(comm.fresh)

    def full(*refs):
        ins = refs[:n_in]
        rd = refs[n_in:n_in + nr]
        pos = n_in + nr + na
        outs = refs[pos:pos + n_out]
        al = refs[pos + n_out:pos + n_out + na]
        fr = refs[pos + n_out + na:pos + n_out + na + nf]
        pos = pos + n_out + na + nf
        scr = refs[pos:pos + n_scr]
        sems = refs[pos + n_scr:]
        first, last = None, None
        for axis, size in enumerate(grid):
            f, l = pl.program_id(axis) == 0, pl.program_id(axis) == size - 1
            first = f if first is None else jnp.logical_and(first, f)
            last = l if last is None else jnp.logical_and(last, l)

        @pl.when(first)
        def _():
            comm.start(rd, al, fr, sems)

        body(*ins, *outs, *scr)

        @pl.when(last)
        def _():
            comm.finish(rd, al, fr, sems)

    hbm = pl.BlockSpec(memory_space=pltpu.HBM)
    for i in range(na):
        aliases[n_in + nr + i] = n_out + i
    struct = [jax.ShapeDtypeStruct(a.shape, a.dtype) for a in comm.aliased]
    res = _call(full, name=name, grid=grid, in_specs=list(in_specs) + [hbm] * (nr + na),
                out_specs=list(out_specs) + [hbm] * (na + nf), out_shape=list(out_shape) + struct + comm.fresh,
                scratch_shapes=list(scratch_shapes) + comm.sems, input_output_aliases=aliases,
                compiler_params=pltpu.CompilerParams(dimension_semantics=("arbitrary",) * len(grid),
                                                     vmem_limit_bytes=V7X_VMEM_LIMIT_BYTES, has_side_effects=True),
                )(*inputs, *comm.reads, *comm.aliased)
    _deliver(comm, res[n_out:n_out + na], res[n_out + na:])
    return res[:n_out]


def _deliver(comm, aliased_out, fresh_out):
    comm.aliased_out, comm.fresh_out = list(aliased_out), list(fresh_out)
    off_a = off_f = 0
    for part in getattr(comm, "parts", []):
        _deliver(part, aliased_out[off_a:off_a + len(part.aliased)], fresh_out[off_f:off_f + len(part.fresh)])
        off_a += len(part.aliased)
        off_f += len(part.fresh)


def _standalone(name, comm):
    def body(*refs):
        nr, na, nf = len(comm.reads), len(comm.aliased), len(comm.fresh)
        rd, al, fr, sems = refs[:nr], refs[nr + na:nr + 2 * na], refs[nr + 2 * na:nr + 2 * na + nf], refs[nr + 2 * na + nf:]
        comm.start(rd, al, fr, sems)
        comm.finish(rd, al, fr, sems)

    nr, na, nf = len(comm.reads), len(comm.aliased), len(comm.fresh)
    struct = [jax.ShapeDtypeStruct(a.shape, a.dtype) for a in comm.aliased]
    res = _call(body, name=name, in_specs=_hbm_specs(nr + na), out_specs=_hbm_specs(na + nf),
                out_shape=struct + comm.fresh, scratch_shapes=comm.sems,
                input_output_aliases={nr + i: i for i in range(na)},
                compiler_params=_comm_params())(*comm.reads, *comm.aliased)
    _deliver(comm, res[:na], res[na:])


def _matmul(name, grid, a_ops, b_ops, terms, dims, out_shape, out_specs, epilogue, extra_ops=(), nk=1,
            acc_shapes=(), alias=None, comm=None):
    na, nb, ne, no = len(a_ops), len(b_ops), len(extra_ops), len(out_shape)

    def body(*refs):
        a = refs[:na]
        b = refs[na:na + nb]
        e = refs[na + nb:na + nb + ne]
        first_out = na + nb + ne + (1 if alias is not None else 0)
        o = refs[first_out:first_out + no]
        accs = refs[first_out + no:]

        def partial(t):
            tot = None
            for ai, bi in t:
                d = lax.dot_general(a[ai][...], b[bi][...], dims, preferred_element_type=F32)
                tot = d if tot is None else tot + d
            return tot

        if nk == 1:
            epilogue([partial(t) for t in terms], e, o)
        else:
            k = pl.program_id(len(grid) - 1)

            @pl.when(k == 0)
            def _():
                for acc in accs:
                    acc[...] = jnp.zeros(acc.shape, F32)

            for acc, t in zip(accs, terms):
                acc[...] += partial(t)

            @pl.when(k == nk - 1)
            def _():
                epilogue([acc[...] for acc in accs], e, o)

    ops = list(a_ops) + list(b_ops) + list(extra_ops)
    arrays = [x for x, _ in ops]
    in_specs = [s for _, s in ops]
    aliases = {}
    if alias is not None:
        arrays.append(alias[0])
        in_specs.append(pl.BlockSpec(memory_space=pl.ANY))
        aliases[len(arrays) - 1] = alias[1]
    scratch = [pltpu.VMEM(s, F32) for s in acc_shapes] if nk > 1 else []
    return _hosted_call(body, comm, name, grid, arrays, in_specs, out_shape, out_specs, scratch, aliases)


def _mm_ffn_in(h, w_g, layer, comm=None):
    t_len, d = h.shape
    fs = w_g.shape[3]
    f = 2 * fs
    tm = _pick(t_len, (2048, 1024, 512))
    tn = _pick(fs, (256, 128))
    nj = fs // tn

    def epilogue(accs, e, o):
        g, u = accs
        o[0][0] = g.astype(BF16)
        o[0][1] = u.astype(BF16)
        o[1][...] = (g * _sigmoid(g) * u).astype(BF16)

    return _matmul(
        "ffn_in", (t_len // tm, 2, nj),
        [(h, pl.BlockSpec((tm, d), lambda i, s, j: (i, 0)))],
        [(w_g, pl.BlockSpec((None, None, d, tn), lambda i, s, j: (layer, s, 0, j))),
         (w_g, pl.BlockSpec((None, None, d, tn), lambda i, s, j: (layer, s + 2, 0, j)))],
        [[(0, 0)], [(0, 1)]], NN,
        [jax.ShapeDtypeStruct((2, t_len, f), BF16), jax.ShapeDtypeStruct((t_len, f), BF16)],
        [pl.BlockSpec((2, tm, tn), lambda i, s, j: (0, i, s * nj + j)),
         pl.BlockSpec((tm, tn), lambda i, s, j: (i, s * nj + j))],
        epilogue, comm=comm)


def _mm_out_res(name, a, w_g, layer, x, scale, comm=None):
    t_len = a.shape[0]
    ks, n = w_g.shape[2], w_g.shape[3]
    tm = _pick(t_len, (1024, 512))
    tn = _pick(n, (1024,))
    tk = _pick(ks, (1408, 512, 256))
    nks = ks // tk

    def epilogue(accs, e, o):
        o[0][...] = e[0][...] + scale * accs[0]

    return _matmul(
        name, (t_len // tm, n // tn, N_CHIPS * nks),
        [(a, pl.BlockSpec((tm, tk), lambda i, j, k: (i, k)))],
        [(w_g, pl.BlockSpec((None, None, tk, tn), lambda i, j, k: (layer, k // nks, k % nks, j)))],
        [[(0, 0)]], NN,
        [jax.ShapeDtypeStruct((t_len, n), F32)],
        [pl.BlockSpec((tm, tn), lambda i, j, k: (i, j))],
        epilogue, extra_ops=[(x, pl.BlockSpec((tm, tn), lambda i, j, k: (i, j)))],
        nk=N_CHIPS * nks, acc_shapes=[(tm, tn)], comm=comm)


def _mm_proj(h, w_g, layer, comm=None):
    t_len, d = h.shape
    cs = w_g.shape[3]
    tm = _pick(t_len, (1024, 512))

    def epilogue(accs, e, o):
        o[0][...] = accs[0]

    return _matmul(
        "mix_in", (t_len // tm, N_CHIPS),
        [(h, pl.BlockSpec((tm, d), lambda i, s: (i, 0)))],
        [(w_g, pl.BlockSpec((None, None, d, cs), lambda i, s: (layer, s, 0, 0)))],
        [[(0, 0)]], NN,
        [jax.ShapeDtypeStruct((t_len, N_CHIPS * cs), F32)],
        [pl.BlockSpec((tm, cs), lambda i, s: (i, s))],
        epilogue, comm=comm)


def _mm_dact_swiglu(dxb, w_g, layer, gu, scale, comm=None):
    t_len, d = dxb.shape
    rs = w_g.shape[2]
    tm = _pick(t_len, (512,))
    tn = _pick(rs, (1408, 256, 128))
    nj = rs // tn

    def epilogue(accs, e, o):
        dact = scale * accs[0]
        g = e[0][0].astype(F32)
        u = e[0][1].astype(F32)
        sig = _sigmoid(g)
        o[0][0] = (dact * u * (sig * (1.0 + g * (1.0 - sig)))).astype(BF16)
        o[0][1] = (dact * (g * sig)).astype(BF16)

    gu_spec = pl.BlockSpec((2, tm, tn), lambda i, s, j: (0, i, s * nj + j))
    return _matmul(
        "ffn_dact", (t_len // tm, N_CHIPS, nj),
        [(dxb, pl.BlockSpec((tm, d), lambda i, s, j: (i, 0)))],
        [(w_g, pl.BlockSpec((None, None, tn, d), lambda i, s, j: (layer, s, j, 0)))],
        [[(0, 0)]], NT,
        [jax.ShapeDtypeStruct(gu.shape, BF16)], [gu_spec],
        epilogue, extra_ops=[(gu, gu_spec)], comm=comm)[0]


def _mm_dh_ffn(dgu, w_g, layer, comm=None):
    t_len = dgu.shape[1]
    d, fs = w_g.shape[2], w_g.shape[3]
    tm = _pick(t_len, (1024, 512))
    tk = _pick(fs, (256, 128))
    nks = fs // tk
    nk = 2 * nks

    def epilogue(accs, e, o):
        o[0][...] = accs[0]

    return _matmul(
        "ffn_dh", (t_len // tm, nk),
        [(dgu, pl.BlockSpec((None, tm, tk), lambda i, k: (0, i, k))),
         (dgu, pl.BlockSpec((None, tm, tk), lambda i, k: (1, i, k)))],
        [(w_g, pl.BlockSpec((None, None, d, tk), lambda i, k: (layer, k // nks, 0, k % nks))),
         (w_g, pl.BlockSpec((None, None, d, tk), lambda i, k: (layer, k // nks + 2, 0, k % nks)))],
        [[(0, 0), (1, 1)]], NT,
        [jax.ShapeDtypeStruct((t_len, d), F32)],
        [pl.BlockSpec((tm, d), lambda i, k: (i, 0))],
        epilogue, nk=nk, acc_shapes=[(tm, d)], comm=comm)[0]


def _mm_dw(name, a, a_spec_of, b, b_spec_of, layer, stack, rows, cols, tn, scale, comm=None):
    t_len = a.shape[0]
    tt = _pick(t_len, (1024, 512))
    nj = cols // tn

    def epilogue(accs, e, o):
        o[0][...] = (scale * accs[0]).astype(BF16)

    shape = jax.ShapeDtypeStruct((2, N_CHIPS, rows, cols), BF16)
    return _matmul(
        name, (N_CHIPS, nj, t_len // tt),
        [(a, a_spec_of(tt))], [(b, b_spec_of(tt, tn, nj))],
        [[(0, 0)]], TN, [shape],
        [pl.BlockSpec((None, None, rows, tn), lambda s, j, t: (layer, s, 0, j))],
        epilogue, nk=t_len // tt, acc_shapes=[(rows, tn)],
        alias=None if stack is None else (stack, 0), comm=comm)[0]


def _dw_ffn_in(h, dgu, layer, stack, comm=None):
    d = h.shape[1]
    fs = dgu.shape[2] // 2
    tn = _pick(fs, (1408, 256))
    return _mm_dw(
        "ffn_dw_in", h, lambda tt: pl.BlockSpec((tt, d), lambda s, j, t: (t, 0)),
        dgu, lambda tt, tn_, nj: pl.BlockSpec((None, tt, tn_), lambda s, j, t: (s // 2, t, (s % 2) * nj + j)),
        layer, stack, d, fs, tn, 1.0, comm)


def _dw_rows(name, a, dxb, layer, stack, scale, comm=None):
    rs = a.shape[1] // N_CHIPS
    d = dxb.shape[1]
    tn = _pick(d, (1024,))
    return _mm_dw(
        name, a, lambda tt: pl.BlockSpec((tt, rs), lambda s, j, t: (t, s)),
        dxb, lambda tt, tn_, nj: pl.BlockSpec((tt, tn_), lambda s, j, t: (t, j)),
        layer, stack, rs, d, tn, scale, comm)


def _dw_mix_in(h, dp, layer, stack):
    d = h.shape[1]
    cs = dp.shape[1] // N_CHIPS
    return _mm_dw(
        "mix_dw_in", h, lambda tt: pl.BlockSpec((tt, d), lambda s, j, t: (t, 0)),
        dp, lambda tt, tn_, nj: pl.BlockSpec((tt, tn_), lambda s, j, t: (t, s)),
        layer, stack, d, cs, cs, 1.0)


def _mm_dmix(dxb, w_g, layer, comm=None):
    t_len, d = dxb.shape
    rs = w_g.shape[2]
    tm = _pick(t_len, (1024, 512))

    def epilogue(accs, e, o):
        o[0][...] = accs[0]

    return _matmul(
        "mix_dout", (t_len // tm, N_CHIPS),
        [(dxb, pl.BlockSpec((tm, d), lambda i, s: (i, 0)))],
        [(w_g, pl.BlockSpec((None, None, rs, d), lambda i, s: (layer, s, 0, 0)))],
        [[(0, 0)]], NT,
        [jax.ShapeDtypeStruct((t_len, N_CHIPS * rs), F32)],
        [pl.BlockSpec((tm, rs), lambda i, s: (i, s))],
        epilogue, comm=comm)[0]


def _mm_dh_mix(dp, w_g, layer, comm=None):
    t_len = dp.shape[0]
    d, cs = w_g.shape[2], w_g.shape[3]
    tm = _pick(t_len, (1024, 512))

    def epilogue(accs, e, o):
        o[0][...] = accs[0]

    return _matmul(
        "mix_dh", (t_len // tm, N_CHIPS),
        [(dp, pl.BlockSpec((tm, cs), lambda i, k: (i, k)))],
        [(w_g, pl.BlockSpec((None, None, d, cs), lambda i, k: (layer, k, 0, 0)))],
        [[(0, 0)]], NT,
        [jax.ShapeDtypeStruct((t_len, d), F32)],
        [pl.BlockSpec((tm, d), lambda i, k: (i, 0))],
        epilogue, nk=N_CHIPS, acc_shapes=[(tm, d)], comm=comm)[0]


def _rms_stats(x):
    r = lax.rsqrt(jnp.mean(x * x, axis=-1, keepdims=True) + NORM_EPS)
    return r, x * r


def _accumulate(ref, part, first):
    @pl.when(first)
    def _():
        ref[...] = part

    @pl.when(jnp.logical_not(first))
    def _():
        ref[...] += part


def _rmsnorm_fwd(x, g):
    t_len, d = x.shape
    tm = _pick(t_len, (512,))

    def body(x_ref, g_ref, h_ref):
        _, xhat = _rms_stats(x_ref[...])
        h_ref[...] = (xhat * g_ref[...]).astype(BF16)

    row = pl.BlockSpec((tm, d), lambda i: (i, 0))
    vec = pl.BlockSpec((1, d), lambda i: (0, 0))
    return _call(body, name="rmsnorm_fwd", grid=(t_len // tm,), in_specs=[row, vec], out_specs=row,
                 out_shape=jax.ShapeDtypeStruct((t_len, d), BF16), compiler_params=_params(1))(x, g)


def _rmsnorm_bwd(dh, x, g, dres):
    t_len, d = x.shape
    tm = _pick(t_len, (256,))

    def body(dh_ref, x_ref, g_ref, dres_ref, dx_ref, dxb_ref, dg_ref):
        r, xhat = _rms_stats(x_ref[...])
        dh_v = dh_ref[...]
        gd = dh_v * g_ref[...]
        dx = dres_ref[...] + r * (gd - xhat * jnp.mean(gd * xhat, axis=-1, keepdims=True))
        dx_ref[...] = dx
        dxb_ref[...] = dx.astype(BF16)
        _accumulate(dg_ref, jnp.sum(dh_v * xhat, axis=0, keepdims=True), pl.program_id(0) == 0)

    row = pl.BlockSpec((tm, d), lambda i: (i, 0))
    vec = pl.BlockSpec((1, d), lambda i: (0, 0))
    return _call(body, name="rmsnorm_bwd", grid=(t_len // tm,), in_specs=[row, row, vec, row],
                 out_specs=[row, row, vec],
                 out_shape=[jax.ShapeDtypeStruct((t_len, d), F32), jax.ShapeDtypeStruct((t_len, d), BF16),
                            jax.ShapeDtypeStruct((1, d), F32)],
                 compiler_params=_params(1))(dh, x, g, dres)


def _loss_head(x, g, target):
    t_len, d = x.shape
    tm = _pick(t_len, (256,))

    def body(x_ref, g_ref, t_ref, dx_ref, dxb_ref, dg_ref, loss_ref):
        first = pl.program_id(0) == 0
        r, xhat = _rms_stats(x_ref[...])
        g_v = g_ref[...]
        err = xhat * g_v - t_ref[...]
        per_token = jnp.mean(err * err, axis=-1, keepdims=True)
        part = 0.5 * jnp.sum(per_token, axis=0, keepdims=True)
        _accumulate(loss_ref, jnp.broadcast_to(part, (1, LANES)), first)
        dy = err * (1.0 / d)
        _accumulate(dg_ref, jnp.sum(dy * xhat, axis=0, keepdims=True), first)
        gd = dy * g_v
        dx = r * (gd - xhat * jnp.mean(gd * xhat, axis=-1, keepdims=True))
        dx_ref[...] = dx
        dxb_ref[...] = dx.astype(BF16)

    row = pl.BlockSpec((tm, d), lambda i: (i, 0))
    vec = pl.BlockSpec((1, d), lambda i: (0, 0))
    return _call(body, name="loss_head", grid=(t_len // tm,), in_specs=[row, vec, row],
                 out_specs=[row, row, vec, pl.BlockSpec((1, LANES), lambda i: (0, 0))],
                 out_shape=[jax.ShapeDtypeStruct((t_len, d), F32), jax.ShapeDtypeStruct((t_len, d), BF16),
                            jax.ShapeDtypeStruct((1, d), F32), jax.ShapeDtypeStruct((1, LANES), F32)],
                 compiler_params=_params(1))(x, g, target)


def _ln_stats(x):
    mu = jnp.mean(x, axis=-1, keepdims=True)
    xc = x - mu
    r = lax.rsqrt(jnp.mean(xc * xc, axis=-1, keepdims=True) + NORM_EPS)
    return r, xc * r


def _ln_bwd(dy, r, xhat, g):
    dxh = dy * g
    return r * (dxh - jnp.mean(dxh, axis=-1, keepdims=True) - xhat * jnp.mean(dxh * xhat, axis=-1, keepdims=True))


def _rope_tables(positions):
    t_len = positions.shape[-1]
    inv_freq = 1.0 / (ROPE_THETA ** (jnp.arange(0, 2 * ROT_HALF, 2, dtype=F32) / (2 * ROT_HALF)))
    ang = positions.astype(F32).reshape(t_len, 1) * inv_freq
    cos = jnp.tile(jnp.cos(ang), (1, LANES // ROT_HALF))
    sin = jnp.tile(jnp.sin(ang), (1, LANES // ROT_HALF))
    lane = jnp.arange(LANES) % HEAD_DIM
    c = jnp.where(lane < 2 * ROT_HALF, cos, 1.0)
    s1 = jnp.where(lane < ROT_HALF, -sin, 0.0)
    s2 = jnp.where((lane >= ROT_HALF) & (lane < 2 * ROT_HALF), sin, 0.0)
    return c.astype(F32), s1.astype(F32), s2.astype(F32)


def _rope_fwd(p, tables):
    t_len = p.shape[0]
    tm = _pick(t_len, (256,))
    n_rot = K_END // LANES

    def body(p_ref, c_ref, s1_ref, s2_ref, o_ref):
        c, s1, s2 = c_ref[...], s1_ref[...], s2_ref[...]
        for j in range(V_END // LANES):
            sl = slice(j * LANES, (j + 1) * LANES)
            t = p_ref[:, sl]
            if j < n_rot:
                t = t * c + pltpu.roll(t, LANES - ROT_HALF, 1) * s1 + pltpu.roll(t, ROT_HALF, 1) * s2
            o_ref[:, sl] = t.astype(BF16)

    tab = pl.BlockSpec((tm, LANES), lambda i: (i, 0))
    blk = pl.BlockSpec((tm, V_END), lambda i: (i, 0))
    return _call(body, name="rope_fwd", grid=(t_len // tm,), in_specs=[blk, tab, tab, tab], out_specs=blk,
                 out_shape=jax.ShapeDtypeStruct((t_len, V_END), BF16), compiler_params=_params(1))(p, *tables)


def _assemble_dp(dq, dkc, dkp, dvc, dvp, tables, dalin, dagate, du, dvin):
    t_len = dq.shape[0]
    nb = t_len // BLK

    def body(dq_ref, dkc_ref, dkp_ref, dvc_ref, dvp_ref, c_ref, s1_ref, s2_ref, dalin_ref, dagate_ref,
             du_ref, dvin_ref, o_ref):
        keep = (pl.program_id(0) < nb - 1).astype(F32)
        c, s1, s2 = c_ref[...], s1_ref[...], s2_ref[...]

        def unrotate(dr):
            return dr * c + pltpu.roll(dr * s1, ROT_HALF, 1) + pltpu.roll(dr * s2, LANES - ROT_HALF, 1)

        for j in range(Q_END // LANES):
            sl = slice(j * LANES, (j + 1) * LANES)
            o_ref[:, sl] = unrotate(dq_ref[:, sl]).astype(BF16)
        for j in range((K_END - Q_END) // LANES):
            sl = slice(j * LANES, (j + 1) * LANES)
            dk = dkc_ref[:, sl] + keep * dkp_ref[:, sl]
            o_ref[:, Q_END + j * LANES:Q_END + (j + 1) * LANES] = unrotate(dk).astype(BF16)
            dv = dvc_ref[:, sl] + keep * dvp_ref[:, sl]
            o_ref[:, K_END + j * LANES:K_END + (j + 1) * LANES] = dv.astype(BF16)
        o_ref[:, V_END:V_END + CONV_CH] = dalin_ref[...]
        o_ref[:, V_END + CONV_CH:CONV_END] = dagate_ref[...]
        o_ref[:, CONV_END:CONV_END + SGU_CH] = du_ref[...]
        o_ref[:, CONV_END + SGU_CH:IN_COLS] = dvin_ref[...]

    def cur(w):
        return pl.BlockSpec((BLK, w), lambda i: (i, 0))

    def nxt(w):
        return pl.BlockSpec((BLK, w), lambda i: (jnp.minimum(i + 1, nb - 1), 0))

    kvw = K_END - Q_END
    return _call(body, name="assemble_dp", grid=(nb,),
                 in_specs=[cur(Q_END), cur(kvw), nxt(kvw), cur(kvw), nxt(kvw), cur(LANES), cur(LANES), cur(LANES),
                           cur(CONV_CH), cur(CONV_CH), cur(SGU_CH), cur(SGU_CH)],
                 out_specs=cur(IN_COLS), out_shape=jax.ShapeDtypeStruct((t_len, IN_COLS), BF16),
                 compiler_params=_params(1))(dq, dkc, dkp, dvc, dvp, *tables, dalin, dagate, du, dvin)


def _attn_probs(q_ref, kp_ref, kc_ref, snk_ref):
    n = pl.program_id(1)
    q = q_ref[...].reshape(GQ * BLK, HEAD_DIM)
    kk = jnp.concatenate([kp_ref[...], kc_ref[...]], axis=0)
    s = lax.dot_general(q, kk, NT, preferred_element_type=F32) * (HEAD_DIM ** -0.5)
    qi = jnp.bitwise_and(lax.broadcasted_iota(jnp.int32, (GQ * BLK, 2 * BLK), 0), BLK - 1)
    kj = lax.broadcasted_iota(jnp.int32, (GQ * BLK, 2 * BLK), 1)
    dist = qi + BLK - kj
    mask = (dist >= 0) & (dist < BLK) & ((kj >= BLK) | (n > 0))
    s = jnp.where(mask, s, -1e30)
    snk = snk_ref[...]
    m = jnp.maximum(jnp.max(s, axis=-1, keepdims=True), snk)
    e = jnp.exp(s - m)
    es = jnp.exp(snk - m)
    den = jnp.sum(e, axis=-1, keepdims=True) + es
    return q, kk, e / den, es / den


def _attn_specs(nb):
    q_spec = pl.BlockSpec((GQ, BLK, HEAD_DIM), lambda g, n: (g, n, 0))
    cur = pl.BlockSpec((None, BLK, HEAD_DIM), lambda g, n: (g, n, 0))
    prev = pl.BlockSpec((None, BLK, HEAD_DIM), lambda g, n: (g, jnp.maximum(n - 1, 0), 0))
    snk = pl.BlockSpec((None, GQ * BLK, 1), lambda g, n: (g, 0, 0))
    return q_spec, cur, prev, snk


def _attn_fwd(q, k, v, snk, comm=None):
    t_len = q.shape[1]
    nb = t_len // BLK

    def body(q_ref, kp_ref, kc_ref, vp_ref, vc_ref, snk_ref, o_ref):
        _, _, p, _ = _attn_probs(q_ref, kp_ref, kc_ref, snk_ref)
        vv = jnp.concatenate([vp_ref[...], vc_ref[...]], axis=0)
        o = lax.dot_general(p.astype(BF16), vv, NN, preferred_element_type=F32)
        o_ref[...] = o.reshape(GQ, BLK, HEAD_DIM).astype(BF16)

    q_spec, cur, prev, snk_spec = _attn_specs(nb)
    return _hosted_call(body, comm, "attn_fwd", (N_KV_HEADS, nb), [q, k, k, v, v, snk],
                        [q_spec, prev, cur, prev, cur, snk_spec], [jax.ShapeDtypeStruct(q.shape, BF16)], [q_spec])[0]


def _attn_bwd(q, k, v, snk, do, comm=None):
    t_len = q.shape[1]
    nb = t_len // BLK

    def body(q_ref, kp_ref, kc_ref, vp_ref, vc_ref, snk_ref, do_ref, dq_ref, dkp_ref, dkc_ref, dvp_ref, dvc_ref,
             dsnk_ref):
        q_v, kk, p, p_snk = _attn_probs(q_ref, kp_ref, kc_ref, snk_ref)
        vv = jnp.concatenate([vp_ref[...], vc_ref[...]], axis=0)
        do_v = do_ref[...].reshape(GQ * BLK, HEAD_DIM).astype(BF16)
        dp = lax.dot_general(do_v, vv, NT, preferred_element_type=F32)
        delta = jnp.sum(p * dp, axis=-1, keepdims=True)
        ds = (p * (dp - delta)).astype(BF16)
        scale = HEAD_DIM ** -0.5
        dq = lax.dot_general(ds, kk, NN, preferred_element_type=F32) * scale
        dq_ref[...] = dq.reshape(GQ, BLK, HEAD_DIM)
        dkk = lax.dot_general(ds, q_v, TN, preferred_element_type=F32) * scale
        dkp_ref[...] = dkk[:BLK]
        dkc_ref[...] = dkk[BLK:]
        dvv = lax.dot_general(p.astype(BF16), do_v, TN, preferred_element_type=F32)
        dvp_ref[...] = dvv[:BLK]
        dvc_ref[...] = dvv[BLK:]
        per_row = -p_snk * delta
        row = lax.broadcasted_iota(jnp.int32, (SUBLANES, LANES), 0)
        tile = jnp.zeros((SUBLANES, LANES), F32)
        for hh in range(GQ):
            tot = jnp.sum(per_row[hh * BLK:(hh + 1) * BLK], axis=0, keepdims=True)
            tile = tile + jnp.where(row == hh, tot, 0.0)
        _accumulate(dsnk_ref, tile, pl.program_id(1) == 0)

    q_spec, cur, prev, snk_spec = _attn_specs(nb)
    kv_shape = jax.ShapeDtypeStruct(k.shape, F32)
    return _hosted_call(
        body, comm, "attn_bwd", (N_KV_HEADS, nb), [q, k, k, v, v, snk, do],
        [q_spec, prev, cur, prev, cur, snk_spec, q_spec],
        [jax.ShapeDtypeStruct(q.shape, F32), kv_shape, kv_shape, kv_shape, kv_shape,
         jax.ShapeDtypeStruct((N_KV_HEADS, SUBLANES, LANES), F32)],
        [q_spec, cur, cur, cur, cur, pl.BlockSpec((None, SUBLANES, LANES), lambda g, n: (g, 0, 0))])


CONV_CHUNK = 256


def _shift_up(win, s):
    n = win.shape[0]
    return win if s == 0 else pltpu.roll(win, n - s, 0)


def _conv_col_specs(t_len):
    lin = pl.BlockSpec((t_len, LANES), lambda j: (0, V_END // LANES + j))
    gate = pl.BlockSpec((t_len, LANES), lambda j: (0, (V_END + CONV_CH) // LANES + j))
    col = pl.BlockSpec((t_len, LANES), lambda j: (0, j))
    wsp = pl.BlockSpec((CONV_PAD, LANES), lambda j: (0, j))
    return lin, gate, col, wsp


def _conv_fwd(p, w, b):
    t_len = p.shape[0]
    ch = CONV_CHUNK

    def body(lin_ref, gate_ref, w_ref, b_ref, y_ref, hp_ref):
        hp_ref[0:CONV_PAD, :] = jnp.zeros((CONV_PAD, LANES), F32)

        def fill(c, carry):
            r0 = pl.multiple_of(c * ch, ch)
            hp_ref[pl.ds(r0 + CONV_PAD, ch), :] = lin_ref[pl.ds(r0, ch), :] * _sigmoid(gate_ref[pl.ds(r0, ch), :])
            return carry

        lax.fori_loop(0, t_len // ch, fill, 0)

        def conv(c, carry):
            r0 = pl.multiple_of(c * ch, ch)
            win = hp_ref[pl.ds(r0, ch + CONV_PAD), :]
            acc = jnp.zeros((ch, LANES), F32)
            for k in range(CONV_WIDTH):
                acc = acc + _shift_up(win, CONV_PAD - (CONV_WIDTH - 1) + k)[:ch] * w_ref[k:k + 1, :]
            y_ref[pl.ds(r0, ch), :] = acc + b_ref[...]
            return carry

        lax.fori_loop(0, t_len // ch, conv, 0)

    lin, gate, col, wsp = _conv_col_specs(t_len)
    return _call(body, name="conv_fwd", grid=(CONV_CH // LANES,),
                 in_specs=[lin, gate, wsp, pl.BlockSpec((1, LANES), lambda j: (0, j))], out_specs=col,
                 out_shape=jax.ShapeDtypeStruct((t_len, CONV_CH), F32),
                 scratch_shapes=[pltpu.VMEM((t_len + CONV_PAD, LANES), F32)],
                 compiler_params=_params(1))(p, p, w, b)


def _conv_post_fwd(y, g, b):
    t_len = y.shape[0]
    tm = _pick(t_len, (512,))

    def body(y_ref, g_ref, b_ref, o_ref):
        _, xhat = _ln_stats(y_ref[...])
        z = xhat * g_ref[...] + b_ref[...]
        o_ref[...] = (z * _sigmoid(z)).astype(BF16)

    row = pl.BlockSpec((tm, CONV_CH), lambda i: (i, 0))
    vec = pl.BlockSpec((1, CONV_CH), lambda i: (0, 0))
    return _call(body, name="conv_post_fwd", grid=(t_len // tm,), in_specs=[row, vec, vec], out_specs=row,
                 out_shape=jax.ShapeDtypeStruct((t_len, CONV_CH), BF16), compiler_params=_params(1))(y, g, b)


def _conv_post_bwd(dmix, y, g, b):
    t_len = y.shape[0]
    tm = _pick(t_len, (512,))

    def body(do_ref, y_ref, g_ref, b_ref, dy_ref, dg_ref, db_ref, dcb_ref):
        first = pl.program_id(0) == 0
        r, xhat = _ln_stats(y_ref[...])
        g_v = g_ref[...]
        z = xhat * g_v + b_ref[...]
        sig = _sigmoid(z)
        dz = do_ref[...] * (sig * (1.0 + z * (1.0 - sig)))
        _accumulate(db_ref, jnp.sum(dz, axis=0, keepdims=True), first)
        _accumulate(dg_ref, jnp.sum(dz * xhat, axis=0, keepdims=True), first)
        dy = _ln_bwd(dz, r, xhat, g_v)
        dy_ref[...] = dy
        _accumulate(dcb_ref, jnp.sum(dy, axis=0, keepdims=True), first)

    row = pl.BlockSpec((tm, CONV_CH), lambda i: (i, 0))
    do_spec = pl.BlockSpec((tm, CONV_CH), lambda i: (i, Q_END // CONV_CH))
    vec = pl.BlockSpec((1, CONV_CH), lambda i: (0, 0))
    vshape = jax.ShapeDtypeStruct((1, CONV_CH), F32)
    return _call(body, name="conv_post_bwd", grid=(t_len // tm,), in_specs=[do_spec, row, vec, vec],
                 out_specs=[row, vec, vec, vec],
                 out_shape=[jax.ShapeDtypeStruct((t_len, CONV_CH), F32), vshape, vshape, vshape],
                 compiler_params=_params(1))(dmix, y, g, b)


def _conv_bwd(p, dy, w):
    t_len = p.shape[0]
    ch = CONV_CHUNK

    def body(lin_ref, gate_ref, dy_ref, w_ref, dlin_ref, dgate_ref, dw_ref, hp_ref, dyp_ref):
        hp_ref[0:CONV_PAD, :] = jnp.zeros((CONV_PAD, LANES), F32)
        dyp_ref[t_len:t_len + CONV_PAD, :] = jnp.zeros((CONV_PAD, LANES), F32)
        dw_ref[...] = jnp.zeros((CONV_PAD, LANES), F32)

        def fill(c, carry):
            r0 = pl.multiple_of(c * ch, ch)
            hp_ref[pl.ds(r0 + CONV_PAD, ch), :] = lin_ref[pl.ds(r0, ch), :] * _sigmoid(gate_ref[pl.ds(r0, ch), :])
            dyp_ref[pl.ds(r0, ch), :] = dy_ref[pl.ds(r0, ch), :]
            return carry

        lax.fori_loop(0, t_len // ch, fill, 0)

        def step(c, carry):
            r0 = pl.multiple_of(c * ch, ch)
            win_h = hp_ref[pl.ds(r0, ch + CONV_PAD), :]
            win_dy = dyp_ref[pl.ds(r0, ch + CONV_PAD), :]
            dyc = win_dy[:ch]
            dh = jnp.zeros((ch, LANES), F32)
            for k in range(CONV_WIDTH):
                tap = _shift_up(win_h, CONV_PAD - (CONV_WIDTH - 1) + k)[:ch]
                dw_ref[k:k + 1, :] += jnp.sum(dyc * tap, axis=0, keepdims=True)
                dh = dh + _shift_up(win_dy, CONV_WIDTH - 1 - k)[:ch] * w_ref[k:k + 1, :]
            lin = lin_ref[pl.ds(r0, ch), :]
            sig = _sigmoid(gate_ref[pl.ds(r0, ch), :])
            dlin_ref[pl.ds(r0, ch), :] = (dh * sig).astype(BF16)
            dgate_ref[pl.ds(r0, ch), :] = (dh * lin * (sig * (1.0 - sig))).astype(BF16)
            return carry

        lax.fori_loop(0, t_len // ch, step, 0)

    lin, gate, col, wsp = _conv_col_specs(t_len)
    half = jax.ShapeDtypeStruct((t_len, CONV_CH), BF16)
    return _call(body, name="conv_bwd", grid=(CONV_CH // LANES,), in_specs=[lin, gate, col, wsp],
                 out_specs=[col, col, wsp],
                 out_shape=[half, half, jax.ShapeDtypeStruct((CONV_PAD, CONV_CH), F32)],
                 scratch_shapes=[pltpu.VMEM((t_len + CONV_PAD, LANES), F32), pltpu.VMEM((t_len + CONV_PAD, LANES), F32)],
                 compiler_params=_params(1))(p, p, dy, w)


def _sgu_mixed(v, w_ref, bt_ref, j):
    lane = lax.broadcasted_iota(jnp.int32, (BLK, LANES), 1)
    lo = lane < HEAD_DIM
    tri = lax.broadcasted_iota(jnp.int32, (BLK, BLK), 0) >= lax.broadcasted_iota(jnp.int32, (BLK, BLK), 1)
    vs = v[:, j * LANES:(j + 1) * LANES]
    v_lo = jnp.where(lo, vs, 0.0).astype(BF16)
    v_hi = jnp.where(lo, 0.0, vs).astype(BF16)
    w_lo = jnp.where(tri, w_ref[2 * j], 0.0).astype(BF16)
    w_hi = jnp.where(tri, w_ref[2 * j + 1], 0.0).astype(BF16)
    m = (lax.dot_general(w_lo, v_lo, NN, preferred_element_type=F32)
         + lax.dot_general(w_hi, v_hi, NN, preferred_element_type=F32))
    bias = jnp.where(lo, bt_ref[:, 2 * j:2 * j + 1], bt_ref[:, 2 * j + 1:2 * j + 2])
    return m + bias, (v_lo, v_hi, w_lo, w_hi, lo, tri)


def _sgu_specs():
    u_spec = pl.BlockSpec((BLK, SGU_CH), lambda i: (i, CONV_END // SGU_CH))
    v_spec = pl.BlockSpec((BLK, SGU_CH), lambda i: (i, CONV_END // SGU_CH + 1))
    vec = pl.BlockSpec((1, SGU_CH), lambda i: (0, 0))
    w_spec = pl.BlockSpec((SGU_HEADS, BLK, BLK), lambda i: (0, 0, 0))
    bt_spec = pl.BlockSpec((BLK, SGU_HEADS), lambda i: (0, 0))
    row = pl.BlockSpec((BLK, SGU_CH), lambda i: (i, 0))
    return u_spec, v_spec, vec, w_spec, bt_spec, row


def _sgu_fwd(p, g, b, w, bt):
    t_len = p.shape[0]

    def body(u_ref, vin_ref, g_ref, b_ref, w_ref, bt_ref, o_ref):
        _, xhat = _ln_stats(vin_ref[...])
        v = xhat * g_ref[...] + b_ref[...]
        for j in range(SGU_CH // LANES):
            m, _ = _sgu_mixed(v, w_ref, bt_ref, j)
            sl = slice(j * LANES, (j + 1) * LANES)
            o_ref[:, sl] = (u_ref[:, sl] * m).astype(BF16)

    u_spec, v_spec, vec, w_spec, bt_spec, row = _sgu_specs()
    return _call(body, name="sgu_fwd", grid=(t_len // BLK,), in_specs=[u_spec, v_spec, vec, vec, w_spec, bt_spec],
                 out_specs=row, out_shape=jax.ShapeDtypeStruct((t_len, SGU_CH), BF16),
                 compiler_params=_params(1))(p, p, g, b, w, bt)


def _sgu_bwd(p, dmix, g, b, w, bt):
    t_len = p.shape[0]

    def body(u_ref, vin_ref, do_ref, g_ref, b_ref, w_ref, bt_ref, du_ref, dvin_ref, dw_ref, dbt_ref, dg_ref,
             db_ref, dv_ref):
        first = pl.program_id(0) == 0
        r, xhat = _ln_stats(vin_ref[...])
        g_v = g_ref[...]
        v = xhat * g_v + b_ref[...]
        lane = lax.broadcasted_iota(jnp.int32, (BLK, LANES), 1)
        dbt = jnp.zeros((BLK, LANES), F32)

        @pl.when(first)
        def _():
            dw_ref[...] = jnp.zeros((SGU_HEADS, BLK, BLK), F32)

        for j in range(SGU_CH // LANES):
            m, (v_lo, v_hi, w_lo, w_hi, lo, tri) = _sgu_mixed(v, w_ref, bt_ref, j)
            sl = slice(j * LANES, (j + 1) * LANES)
            do_v = do_ref[:, sl]
            du_ref[:, sl] = (do_v * m).astype(BF16)
            dm = do_v * u_ref[:, sl]
            dm_lo = jnp.where(lo, dm, 0.0)
            dm_hi = jnp.where(lo, 0.0, dm)
            dbt = dbt + jnp.where(lane == 2 * j, jnp.sum(dm_lo, axis=-1, keepdims=True), 0.0)
            dbt = dbt + jnp.where(lane == 2 * j + 1, jnp.sum(dm_hi, axis=-1, keepdims=True), 0.0)
            dm_lo, dm_hi = dm_lo.astype(BF16), dm_hi.astype(BF16)
            dw_ref[2 * j] += jnp.where(tri, lax.dot_general(dm_lo, v_lo, NT, preferred_element_type=F32), 0.0)
            dw_ref[2 * j + 1] += jnp.where(tri, lax.dot_general(dm_hi, v_hi, NT, preferred_element_type=F32), 0.0)
            dv_ref[:, sl] = (lax.dot_general(w_lo, dm_lo, TN, preferred_element_type=F32)
                             + lax.dot_general(w_hi, dm_hi, TN, preferred_element_type=F32))
        _accumulate(dbt_ref, dbt, first)
        dv = dv_ref[...]
        _accumulate(db_ref, jnp.sum(dv, axis=0, keepdims=True), first)
        _accumulate(dg_ref, jnp.sum(dv * xhat, axis=0, keepdims=True), first)
        dvin_ref[...] = _ln_bwd(dv, r, xhat, g_v).astype(BF16)

    u_spec, v_spec, vec, w_spec, bt_spec, row = _sgu_specs()
    do_spec = pl.BlockSpec((BLK, SGU_CH), lambda i: (i, (Q_END + CONV_CH) // SGU_CH))
    half = jax.ShapeDtypeStruct((t_len, SGU_CH), BF16)
    vshape = jax.ShapeDtypeStruct((1, SGU_CH), F32)
    return _call(body, name="sgu_bwd", grid=(t_len // BLK,),
                 in_specs=[u_spec, v_spec, do_spec, vec, vec, w_spec, bt_spec],
                 out_specs=[row, row, w_spec, pl.BlockSpec((BLK, LANES), lambda i: (0, 0)), vec, vec],
                 out_shape=[half, half, jax.ShapeDtypeStruct((SGU_HEADS, BLK, BLK), F32),
                            jax.ShapeDtypeStruct((BLK, LANES), F32), vshape, vshape],
                 scratch_shapes=[pltpu.VMEM((BLK, SGU_CH), F32)],
                 compiler_params=_params(1))(p, p, dmix, g, b, w, bt)


def _place():
    x, y, c = lax.axis_index("x"), lax.axis_index("y"), lax.axis_index("c")
    chips = [(1 - x, y), (x, 1 - y), (1 - x, 1 - y)]
    return x, y, c, chips


def _hbm_specs(n):
    return [pl.BlockSpec(memory_space=pltpu.HBM)] * n


def _comm_params():
    return pltpu.CompilerParams(has_side_effects=True)


def _remote(src, dst, send_sem, recv_sem, to):
    return pltpu.make_async_remote_copy(src_ref=src, dst_ref=dst, send_sem=send_sem, recv_sem=recv_sem,
                                        device_id=to, device_id_type=MESH_ID)


def _cast_place(w_local, chip):
    n, rows, cols = w_local.shape

    def body(chip_ref, w_ref, o_ref):
        o_ref[...] = w_ref[...].astype(BF16)

    grid_spec = pltpu.PrefetchScalarGridSpec(
        num_scalar_prefetch=1, grid=(n, rows // ROW_TILE),
        in_specs=[pl.BlockSpec((None, ROW_TILE, cols), lambda l, i, ch: (l, i, 0))],
        out_specs=pl.BlockSpec((None, None, ROW_TILE, cols), lambda l, i, ch: (l, ch[0], i, 0)))
    return _call(body, name="cast_place", grid_spec=grid_spec,
                 out_shape=jax.ShapeDtypeStruct((n, N_CHIPS, rows, cols), BF16), compiler_params=_params(2))(chip, w_local)


def _all_gather_weights(placed, shards):
    n_placed, nt = len(placed), len(placed) + len(shards)

    def body(*refs):
        ins, outs = refs[:nt], refs[nt:2 * nt]
        ici_send, ici_recv, d2d_send, d2d_recv, local_sem = refs[2 * nt:]
        x, y, c, chips = _place()
        me = 2 * x + y
        sibling = (x, y, 1 - c)
        local = [pltpu.make_async_copy(ins[t].at[l], outs[t].at[l, me], local_sem.at[2 * (t - n_placed) + l])
                 for t in range(n_placed, nt) for l in range(2)]
        for cp in local:
            cp.start()
        sends = []
        for t in range(nt):
            src = outs[t].at[c, me] if t < n_placed else ins[t].at[c]
            for j, (px, py) in enumerate(chips):
                sends.append(_remote(src, outs[t].at[c, me], ici_send.at[3 * t + j], ici_recv.at[3 * t + j],
                                     (px, py, c)))
        for cp in sends:
            cp.start()
        for t in range(nt):
            for j, (px, py) in enumerate(chips):
                slab = outs[t].at[c, 2 * px + py]
                _remote(slab, slab, ici_send.at[3 * t + j], ici_recv.at[3 * t + j], (px, py, c)).wait_recv()
                fwd = _remote(slab, slab, d2d_send.at[3 * t + j], d2d_recv.at[3 * t + j], sibling)
                fwd.start()
                sends.append(fwd)
        for t in range(nt):
            for j, (px, py) in enumerate(chips):
                slab = outs[t].at[1 - c, 2 * px + py]
                _remote(slab, slab, d2d_send.at[3 * t + j], d2d_recv.at[3 * t + j], sibling).wait_recv()
        for cp in sends:
            cp.wait_send()
        for cp in local:
            cp.wait()

    out_shape = [jax.ShapeDtypeStruct(p.shape, p.dtype) for p in placed]
    out_shape += [jax.ShapeDtypeStruct((2, N_CHIPS) + s.shape[1:], s.dtype) for s in shards]
    sems = [pltpu.SemaphoreType.DMA((3 * nt,))] * 4 + [pltpu.SemaphoreType.DMA((2 * len(shards),))]
    return _call(body, name="all_gather_weights", in_specs=_hbm_specs(nt), out_specs=_hbm_specs(nt),
                 out_shape=out_shape, scratch_shapes=sems, input_output_aliases={t: t for t in range(n_placed)},
                 compiler_params=_comm_params())(*placed, *shards)


def _gather_comm(bufs, pieces):
    n = len(pieces)
    sems = [pltpu.SemaphoreType.DMA((3 * n,))] * 4

    def rows(ref, layer, chip, r0, nr):
        return ref.at[layer, chip, pl.ds(r0, nr)]

    def start(rd, al, fr, sm):
        ici_send, ici_recv, _, _ = sm
        x, y, c, chips = _place()
        for i, (t, layer, r0, nr) in enumerate(pieces):
            @pl.when(c == layer)
            def _():
                own = rows(al[t], layer, 2 * x + y, r0, nr)
                for j, (px, py) in enumerate(chips):
                    _remote(own, own, ici_send.at[3 * i + j], ici_recv.at[3 * i + j], (px, py, c)).start()

    def finish(rd, al, fr, sm):
        ici_send, ici_recv, d2d_send, d2d_recv = sm
        x, y, c, chips = _place()
        sibling = (x, y, 1 - c)
        for i, (t, layer, r0, nr) in enumerate(pieces):
            @pl.when(c == layer)
            def _():
                passed = []
                for j, (px, py) in enumerate(chips):
                    got = rows(al[t], layer, 2 * px + py, r0, nr)
                    _remote(got, got, ici_send.at[3 * i + j], ici_recv.at[3 * i + j], (px, py, c)).wait_recv()
                    fwd = _remote(got, got, d2d_send.at[3 * i + j], d2d_recv.at[3 * i + j], sibling)
                    fwd.start()
                    passed.append(fwd)
                own = rows(al[t], layer, 2 * x + y, r0, nr)
                for j, (px, py) in enumerate(chips):
                    _remote(own, own, ici_send.at[3 * i + j], ici_recv.at[3 * i + j], (px, py, c)).wait_send()
                for fwd in passed:
                    fwd.wait_send()

            @pl.when(c != layer)
            def _():
                for j, (px, py) in enumerate(chips):
                    got = rows(al[t], layer, 2 * px + py, r0, nr)
                    _remote(got, got, d2d_send.at[3 * i + j], d2d_recv.at[3 * i + j], sibling).wait_recv()

    return _Comm([], bufs, [], sems, start, finish)


def _to_owner_comm(stacks, layer):
    nt = len(stacks)
    sems = [pltpu.SemaphoreType.DMA((nt,))] * 2
    fresh = [jax.ShapeDtypeStruct(s.shape[1:], s.dtype) for s in stacks]

    def copies(rd, fr, sm):
        x, y, c, _ = _place()
        return c, [_remote(rd[t].at[layer], fr[t], sm[0].at[t], sm[1].at[t], (x, y, 1 - c)) for t in range(nt)]

    def start(rd, al, fr, sm):
        c, cps = copies(rd, fr, sm)

        @pl.when(c != layer)
        def _():
            for cp in cps:
                cp.start()

    def finish(rd, al, fr, sm):
        c, cps = copies(rd, fr, sm)

        @pl.when(c != layer)
        def _():
            for cp in cps:
                cp.wait_send()

        @pl.when(c == layer)
        def _():
            for cp in cps:
                cp.wait_recv()

    return _Comm(stacks, [], fresh, sems, start, finish)


def _chip_comm(partials):
    nt = len(partials)
    sems = [pltpu.SemaphoreType.DMA((3 * nt,))] * 2
    fresh = [jax.ShapeDtypeStruct((3,) + p.shape[1:], p.dtype) for p, _ in partials]

    def each(rd, fr, sm, act):
        x, y, c, chips = _place()
        for t, (_, layer) in enumerate(partials):
            @pl.when(c == layer)
            def _():
                for j, (px, py) in enumerate(chips):
                    act(_remote(rd[t].at[2 * px + py], fr[t].at[j], sm[0].at[3 * t + j], sm[1].at[3 * t + j],
                                (px, py, c)))

    def start(rd, al, fr, sm):
        each(rd, fr, sm, lambda cp: cp.start())

    def finish(rd, al, fr, sm):
        each(rd, fr, sm, lambda cp: cp.wait())

    return _Comm([p for p, _ in partials], [], fresh, sems, start, finish)


def _from_owner_comm(finals, layer):
    nt = len(finals)
    sems = [pltpu.SemaphoreType.DMA((nt,))] * 2

    def copies(al, sm):
        x, y, c, _ = _place()
        return c, [_remote(al[t].at[layer], al[t].at[layer], sm[0].at[t], sm[1].at[t], (x, y, 1 - c))
                   for t in range(nt)]

    def start(rd, al, fr, sm):
        c, cps = copies(al, sm)

        @pl.when(c == layer)
        def _():
            for cp in cps:
                cp.start()

    def finish(rd, al, fr, sm):
        c, cps = copies(al, sm)

        @pl.when(c == layer)
        def _():
            for cp in cps:
                cp.wait_send()

        @pl.when(c != layer)
        def _():
            for cp in cps:
                cp.wait_recv()

    return _Comm([], finals, [], sems, start, finish)


def _all_reduce_small(buf):
    rows = buf.shape[0]

    def body(x_ref, out_ref, all_ref, send_sems, recv_sems, local_sem):
        x, y, c, chips = _place()
        me, sibling = (x, y, c), (x, y, 1 - c)

        def block(px, py, pc):
            return all_ref.at[pl.ds((4 * px + 2 * py + pc) * rows, rows), :]

        def copy(k, blk, to, src=None):
            return _remote(block(*blk) if src is None else src, block(*blk), send_sems.at[k], recv_sems.at[k], to)

        mine = pltpu.make_async_copy(x_ref, block(*me), local_sem)
        mine.start()
        first = [copy(0, me, sibling, src=x_ref)]
        first += [copy(1 + j, me, (*chip, c), src=x_ref) for j, chip in enumerate(chips)]
        for cp in first:
            cp.start()
        passed = [copy(4 + j, (*chip, c), sibling) for j, chip in enumerate(chips)]
        for j, chip in enumerate(chips):
            copy(1 + j, (*chip, c), me).wait_recv()
            passed[j].start()
        copy(0, sibling, me).wait_recv()
        for j, chip in enumerate(chips):
            copy(4 + j, (*chip, 1 - c), me).wait_recv()
        for cp in first + passed:
            cp.wait_send()
        mine.wait()
        tot = all_ref[0:rows, :]
        for k in range(1, N_DEV):
            tot = tot + all_ref[k * rows:(k + 1) * rows, :]
        out_ref[...] = tot

    vm = pl.BlockSpec(memory_space=pltpu.VMEM)
    return _call(body, name="all_reduce_small", in_specs=[vm], out_specs=vm,
                 out_shape=jax.ShapeDtypeStruct(buf.shape, F32),
                 scratch_shapes=[pltpu.VMEM((N_DEV * rows, LANES), F32), pltpu.SemaphoreType.DMA((7,)),
                                 pltpu.SemaphoreType.DMA((7,)), pltpu.SemaphoreType.DMA],
                 compiler_params=pltpu.CompilerParams(has_side_effects=True,
                                                      vmem_limit_bytes=V7X_VMEM_LIMIT_BYTES))(buf)


ROW_TILE = 128


def _chip_partial(stack, received, layer):
    _, _, rows, cols = stack.shape

    def body(a_ref, b_ref, o_ref):
        o_ref[...] = (a_ref[...].astype(F32) + b_ref[...].astype(F32)).astype(BF16)

    blk = pl.BlockSpec((None, ROW_TILE, cols), lambda s, i: (s, i, 0))
    return _call(body, name="chip_partial", grid=(N_CHIPS, rows // ROW_TILE),
                 in_specs=[pl.BlockSpec((None, None, ROW_TILE, cols), lambda s, i: (layer, s, i, 0)), blk],
                 out_specs=blk, out_shape=jax.ShapeDtypeStruct(received.shape, BF16),
                 compiler_params=_params(2))(stack, received)


def _final_sum(partial, from_chips, layer, chip, finals):
    _, rows, cols = partial.shape

    def body(chip_ref, a_ref, r_ref, *rest):
        o_ref = rest[-1]
        tot = a_ref[...].astype(F32)
        for j in range(3):
            tot = tot + r_ref[j].astype(F32)
        o_ref[...] = tot

    in_specs = [pl.BlockSpec((None, ROW_TILE, cols), lambda i, ch: (ch[0], i, 0)),
                pl.BlockSpec((3, ROW_TILE, cols), lambda i, ch: (0, i, 0))]
    args = [chip, partial, from_chips]
    kw = {}
    if finals is not None:
        in_specs.append(pl.BlockSpec(memory_space=pl.ANY))
        args.append(finals)
        kw["input_output_aliases"] = {3: 0}
    grid_spec = pltpu.PrefetchScalarGridSpec(
        num_scalar_prefetch=1, grid=(rows // ROW_TILE,), in_specs=in_specs,
        out_specs=pl.BlockSpec((None, ROW_TILE, cols), lambda i, ch: (layer, i, 0)))
    return _call(body, name="final_sum", grid_spec=grid_spec, out_shape=jax.ShapeDtypeStruct((2, rows, cols), F32),
                 compiler_params=_params(1), **kw)(*args)


def _adamw(w, g, m, v, comm=None):
    n, rows, cols = w.shape
    tr = _pick(rows, (ROW_TILE, SUBLANES))
    c1 = 1.0 - ADAM_B1 ** ADAM_STEP
    c2 = 1.0 - ADAM_B2 ** ADAM_STEP

    def body(w_ref, g_ref, m_ref, v_ref, d_ref, nm_ref, nv_ref, go_ref):
        g_v = g_ref[...]
        go_ref[...] = g_v
        nm = ADAM_B1 * m_ref[...] + (1.0 - ADAM_B1) * g_v
        nv = ADAM_B2 * v_ref[...] + (1.0 - ADAM_B2) * (g_v * g_v)
        nm_ref[...] = nm
        nv_ref[...] = nv
        d_ref[...] = -ADAM_LR * ((nm / c1) / (jnp.sqrt(nv / c2) + ADAM_EPS) + ADAM_WD * w_ref[...])

    blk = pl.BlockSpec((None, tr, cols), lambda l, i: (l, i, 0))
    shape = jax.ShapeDtypeStruct(w.shape, F32)
    return _hosted_call(body, comm, "adamw", (n, rows // tr), [w, g, m, v], [blk] * 4, [shape] * 4, [blk] * 4)


def _to_heads(a, n_heads):
    t_len = a.shape[0]
    return a.reshape(t_len, n_heads, HEAD_DIM).transpose(1, 0, 2)


def _from_heads(a):
    n_heads, t_len, _ = a.shape
    return a.transpose(1, 0, 2).reshape(t_len, n_heads * HEAD_DIM)


class _Schedule:
    def __init__(self):
        self.sites = {}
        self.open = []

    def add(self, site, make, done=None):
        self.sites.setdefault(site, []).append((make, done))

    def begin(self, site):
        self.open = [(make(), done) for make, done in self.sites.pop(site, [])]
        return _merge_comms([cm for cm, _ in self.open])

    def end(self):
        for cm, done in self.open:
            if done is not None:
                done(cm)
        self.open = []


def _ffn_fwd(x, gain, wg, which, layer, sched):
    w_in_name, w_out_name = f"ffn{which}_w_in", f"ffn{which}_w_out"
    h = _rmsnorm_fwd(x, gain)
    comm = sched.begin(("ffn_in", layer, which))
    gu, act = _mm_ffn_in(h, wg[w_in_name], layer, comm)
    sched.end()
    comm = sched.begin(("ffn_out", layer, which))
    x_new = _mm_out_res("ffn_out", act, wg[w_out_name], layer, x, FFN_RESIDUAL_WEIGHT, comm)[0]
    sched.end()
    return x_new, (x, h, gu, act)


def _ffn_bwd(dx, dxb, saved, gain, wg, which, layer, stacks, sched):
    w_in_name, w_out_name = f"ffn{which}_w_in", f"ffn{which}_w_out"
    x, h, gu, act = saved
    comm = sched.begin(("ffn_dact", layer, which))
    dgu = _mm_dact_swiglu(dxb, wg[w_out_name], layer, gu, FFN_RESIDUAL_WEIGHT, comm)
    sched.end()
    comm = sched.begin(("ffn_dw_out", layer, which))
    stacks[w_out_name] = _dw_rows("ffn_dw_out", act, dxb, layer, stacks[w_out_name], FFN_RESIDUAL_WEIGHT, comm)
    sched.end()
    comm = sched.begin(("ffn_dh", layer, which))
    dh = _mm_dh_ffn(dgu, wg[w_in_name], layer, comm)
    sched.end()
    comm = sched.begin(("ffn_dw_in", layer, which))
    stacks[w_in_name] = _dw_ffn_in(h, dgu, layer, stacks[w_in_name], comm)
    sched.end()
    return _rmsnorm_bwd(dh, x, gain, dx)


def _mix_fwd(x, gain, wg, layer, small, tables, sched):
    h = _rmsnorm_fwd(x, gain)
    comm = sched.begin(("mix_in", layer))
    p = _mm_proj(h, wg["w_in"], layer, comm)[0]
    sched.end()
    qkv = _rope_fwd(p, tables)
    q = _to_heads(qkv[:, :Q_END], N_Q_HEADS)
    k = _to_heads(qkv[:, Q_END:K_END], N_KV_HEADS)
    v = _to_heads(qkv[:, K_END:V_END], N_KV_HEADS)
    comm = sched.begin(("attn", layer))
    attn = _from_heads(_attn_fwd(q, k, v, small["snk"], comm))
    sched.end()
    y = _conv_fwd(p, small["conv_w"], small["conv_b"])
    conv = _conv_post_fwd(y, small["conv_ln_g"], small["conv_ln_b"])
    sgu = _sgu_fwd(p, small["sgu_ln_g"], small["sgu_ln_b"], small["sgu_w"], small["sgu_bt"])
    mix = jnp.concatenate([attn, conv, sgu], axis=1)
    x_new = _mm_out_res("mix_out", mix, wg["w_out"], layer, x, 1.0)[0]
    return x_new, (x, h, p, q, k, v, y, mix)


def _mix_bwd(dx, dxb, saved, gain, wg, small, tables, layer, stacks, sched):
    x, h, p, q, k, v, y, mix = saved
    comm = sched.begin(("mix_dout", layer))
    dmix = _mm_dmix(dxb, wg["w_out"], layer, comm)
    sched.end()
    stacks["w_out"] = _dw_rows("mix_dw_out", mix, dxb, layer, stacks["w_out"], 1.0)
    do = _to_heads(dmix[:, :Q_END], N_Q_HEADS)
    comm = sched.begin(("attn_bwd", layer))
    dq, dkp, dkc, dvp, dvc, dsnk = _attn_bwd(q, k, v, small["snk"], do, comm)
    sched.end()
    dy, d_ln_g, d_ln_b, d_conv_b = _conv_post_bwd(dmix, y, small["conv_ln_g"], small["conv_ln_b"])
    dalin, dagate, d_conv_w = _conv_bwd(p, dy, small["conv_w"])
    du, dvin, d_sgu_w, d_sgu_bt, d_sgu_g, d_sgu_b = _sgu_bwd(p, dmix, small["sgu_ln_g"], small["sgu_ln_b"],
                                                           small["sgu_w"], small["sgu_bt"])
    dp = _assemble_dp(_from_heads(dq), _from_heads(dkc), _from_heads(dkp), _from_heads(dvc), _from_heads(dvp),
                      tables, dalin, dagate, du, dvin)
    stacks["w_in"] = _dw_mix_in(h, dp, layer, stacks["w_in"])
    comm = sched.begin(("mix_dh", layer))
    dh = _mm_dh_mix(dp, wg["w_in"], layer, comm)
    sched.end()
    dx_in, dxb_in, dgain = _rmsnorm_bwd(dh, x, gain, dx)
    grads = {
        "norm_mix": dgain[0], "conv_dw_w": d_conv_w[:CONV_WIDTH], "conv_dw_b": d_conv_b[0],
        "conv_ln_g": d_ln_g[0], "conv_ln_b": d_ln_b[0], "sgu_ln_g": d_sgu_g[0], "sgu_ln_b": d_sgu_b[0],
        "sgu_w": d_sgu_w, "sgu_b": d_sgu_bt[:, :SGU_HEADS].T, "attn_sinks": dsnk[:, :GQ, 0].reshape(N_Q_HEADS),
    }
    return dx_in, dxb_in, grads


BIG = ("ffn1_w_in", "ffn1_w_out", "w_in", "w_out", "ffn2_w_in", "ffn2_w_out")
SMALL = ("norm_ffn1", "norm_mix", "conv_dw_w", "conv_dw_b", "conv_ln_g", "conv_ln_b", "sgu_ln_g", "sgu_ln_b",
         "sgu_w", "sgu_b", "attn_sinks", "norm_ffn2", "final_norm")
WEIGHTS = ("norm_ffn1", "ffn1_w_in", "ffn1_w_out", "norm_mix", "w_in", "conv_dw_w", "conv_dw_b", "conv_ln_g",
           "conv_ln_b", "sgu_ln_g", "sgu_ln_b", "sgu_w", "sgu_b", "attn_sinks", "w_out", "norm_ffn2", "ffn2_w_in",
           "ffn2_w_out", "final_norm")
PACK_ROWS = SUBLANES * LANES

FIRST_GATHER = [("ffn1_w_in", 0, None)]
FORWARD_PLAN = {
    ("ffn_in", 0, 1): [("ffn1_w_out", 0, None), ("w_in", 0, None)],
    ("ffn_out", 0, 1): [("ffn2_w_in", 0, 0)],
    ("mix_in", 0): [("w_out", 0, None)],
    ("attn", 0): [("ffn2_w_in", 0, 1)],
    ("ffn_in", 0, 2): [("ffn2_w_out", 0, None), ("ffn1_w_in", 1, 0)],
    ("ffn_out", 0, 2): [("ffn1_w_in", 1, 1)],
    ("ffn_in", 1, 1): [("ffn1_w_out", 1, None), ("w_in", 1, None)],
    ("ffn_out", 1, 1): [("ffn2_w_in", 1, 0)],
    ("mix_in", 1): [("w_out", 1, None)],
    ("attn", 1): [("ffn2_w_in", 1, 1)],
    ("ffn_in", 1, 2): [("ffn2_w_out", 1, None)],
}
SUBLAYER_WEIGHTS = {"ffn1": ["ffn1_w_out", "ffn1_w_in"], "ffn2": ["ffn2_w_out", "ffn2_w_in"], "mix": ["w_out", "w_in"]}


def _pack(arrays):
    flat = jnp.concatenate([a.reshape(-1).astype(F32) for a in arrays])
    pad = (-flat.shape[0]) % PACK_ROWS
    return jnp.pad(flat, (0, pad)).reshape(-1, LANES)


def _unpack(buf, shapes):
    flat = buf.reshape(-1)
    out, off = [], 0
    for s in shapes:
        n = 1
        for d in s:
            n *= d
        out.append(flat[off:off + n].reshape(s))
        off += n
    return out


def kernel(x, positions, norm_ffn1, ffn1_w_in, ffn1_w_out, norm_mix, w_in, conv_dw_w, conv_dw_b, conv_ln_g, conv_ln_b, sgu_ln_g, sgu_ln_b, sgu_w, sgu_b, attn_sinks, w_out, norm_ffn2, ffn2_w_in, ffn2_w_out, final_norm, loss_target, m_norm_ffn1, m_ffn1_w_in, m_ffn1_w_out, m_norm_mix, m_w_in, m_conv_dw_w, m_conv_dw_b, m_conv_ln_g, m_conv_ln_b, m_sgu_ln_g, m_sgu_ln_b, m_sgu_w, m_sgu_b, m_attn_sinks, m_w_out, m_norm_ffn2, m_ffn2_w_in, m_ffn2_w_out, m_final_norm, v_norm_ffn1, v_ffn1_w_in, v_ffn1_w_out, v_norm_mix, v_w_in, v_conv_dw_w, v_conv_dw_b, v_conv_ln_g, v_conv_ln_b, v_sgu_ln_g, v_sgu_ln_b, v_sgu_w, v_sgu_b, v_attn_sinks, v_w_out, v_norm_ffn2, v_ffn2_w_in, v_ffn2_w_out, v_final_norm):
    w = dict(norm_ffn1=norm_ffn1, ffn1_w_in=ffn1_w_in, ffn1_w_out=ffn1_w_out, norm_mix=norm_mix, w_in=w_in,
             conv_dw_w=conv_dw_w, conv_dw_b=conv_dw_b, conv_ln_g=conv_ln_g, conv_ln_b=conv_ln_b, sgu_ln_g=sgu_ln_g,
             sgu_ln_b=sgu_ln_b, sgu_w=sgu_w, sgu_b=sgu_b, attn_sinks=attn_sinks, w_out=w_out, norm_ffn2=norm_ffn2,
             ffn2_w_in=ffn2_w_in, ffn2_w_out=ffn2_w_out, final_norm=final_norm)
    m = dict(norm_ffn1=m_norm_ffn1, ffn1_w_in=m_ffn1_w_in, ffn1_w_out=m_ffn1_w_out, norm_mix=m_norm_mix, w_in=m_w_in,
             conv_dw_w=m_conv_dw_w, conv_dw_b=m_conv_dw_b, conv_ln_g=m_conv_ln_g, conv_ln_b=m_conv_ln_b,
             sgu_ln_g=m_sgu_ln_g, sgu_ln_b=m_sgu_ln_b, sgu_w=m_sgu_w, sgu_b=m_sgu_b, attn_sinks=m_attn_sinks,
             w_out=m_w_out, norm_ffn2=m_norm_ffn2, ffn2_w_in=m_ffn2_w_in, ffn2_w_out=m_ffn2_w_out,
             final_norm=m_final_norm)
    v = dict(norm_ffn1=v_norm_ffn1, ffn1_w_in=v_ffn1_w_in, ffn1_w_out=v_ffn1_w_out, norm_mix=v_norm_mix, w_in=v_w_in,
             conv_dw_w=v_conv_dw_w, conv_dw_b=v_conv_dw_b, conv_ln_g=v_conv_ln_g, conv_ln_b=v_conv_ln_b,
             sgu_ln_g=v_sgu_ln_g, sgu_ln_b=v_sgu_ln_b, sgu_w=v_sgu_w, sgu_b=v_sgu_b, attn_sinks=v_attn_sinks,
             w_out=v_w_out, norm_ffn2=v_norm_ffn2, ffn2_w_in=v_ffn2_w_in, ffn2_w_out=v_ffn2_w_out,
             final_norm=v_final_norm)
    depth = norm_ffn1.shape[0]
    assert depth == 2 and x.shape[0] == 1
    xc = lax.axis_index("x")
    yc = lax.axis_index("y")
    cc = lax.axis_index("c")
    chip = 2 * xc + yc

    chip_arr = chip.reshape(1).astype(jnp.int32)
    wg = {n: _cast_place(w[n], chip_arr) for n in BIG}
    sched = _Schedule()

    def gather(pieces):
        names = sorted({n for n, _, _ in pieces})
        half = w["ffn1_w_in"].shape[1] // 2

        def make():
            rows = lambda n, part: (0, wg[n].shape[2]) if part is None else (part * half, half)
            return _gather_comm([wg[n] for n in names], [(names.index(n), l, *rows(n, part)) for n, l, part in pieces])

        return make, lambda cm: wg.update(zip(names, cm.aliased_out))

    make, done = gather(FIRST_GATHER)
    first = make()
    _standalone("gather_first", first)
    done(first)
    for site, pieces in FORWARD_PLAN.items():
        sched.add(site, *gather(pieces))
    conv_w_full = _all_gather_weights([], [conv_dw_w])[0].transpose(0, 2, 1, 3).reshape(depth, CONV_WIDTH, CONV_CH)
    conv_w_full = jnp.pad(conv_w_full, ((0, 0), (0, CONV_PAD - CONV_WIDTH), (0, 0)))

    tables = _rope_tables(positions)
    small = []
    for l in range(depth):
        small.append(dict(
            snk=jnp.broadcast_to(attn_sinks[l].reshape(N_KV_HEADS, GQ, 1, 1), (N_KV_HEADS, GQ, BLK, 1)).reshape(
                N_KV_HEADS, GQ * BLK, 1),
            conv_w=conv_w_full[l], conv_b=conv_dw_b[l][None], conv_ln_g=conv_ln_g[l][None],
            conv_ln_b=conv_ln_b[l][None], sgu_ln_g=sgu_ln_g[l][None], sgu_ln_b=sgu_ln_b[l][None], sgu_w=sgu_w[l],
            sgu_bt=sgu_b[l].T))

    xs = x[0]
    saved = []
    for l in range(depth):
        xs, s1 = _ffn_fwd(xs, norm_ffn1[l][None], wg, 1, l, sched)
        xs, s2 = _mix_fwd(xs, norm_mix[l][None], wg, l, small[l], tables, sched)
        xs, s3 = _ffn_fwd(xs, norm_ffn2[l][None], wg, 2, l, sched)
        saved.append((s1, s2, s3))
    dx, dxb, d_final, loss_part = _loss_head(xs, final_norm[None], loss_target[0])

    stacks = {n: None for n in BIG}
    partials, from_chips = {}, {}
    finals = {n: None for n in BIG}

    def to_owner(layer, names):
        def done(cm):
            for n, received in zip(names, cm.fresh_out):
                partials[n, layer] = _chip_partial(stacks[n], received, layer)

        return lambda: _to_owner_comm([stacks[n] for n in names], layer), done

    def between_chips(layer, names, then_sum=()):
        def done(cm):
            from_chips.update({(n, layer): r for n, r in zip(names, cm.fresh_out)})
            for n in then_sum:
                finals[n] = _final_sum(partials[n, layer], from_chips[n, layer], layer, chip_arr, finals[n])

        return lambda: _chip_comm([(partials[n, layer], layer) for n in names]), done

    def from_owner(layer, names):
        return (lambda: _from_owner_comm([finals[n] for n in names], layer),
                lambda cm: finals.update(zip(names, cm.aliased_out)))

    order = [(l, kind) for l in reversed(range(depth)) for kind in ("ffn2", "mix", "ffn1")]
    for (layer, kind), (nxt_layer, nxt_kind) in zip(order[:-1], order[1:]):
        names = SUBLAYER_WEIGHTS[kind]
        if nxt_kind == "mix":
            sched.add(("mix_dout", nxt_layer), *to_owner(layer, names))
            sched.add(("attn_bwd", nxt_layer), *between_chips(layer, names, names))
            sched.add(("mix_dh", nxt_layer), *from_owner(layer, names))
        else:
            which = int(nxt_kind[-1])
            sched.add(("ffn_dact", nxt_layer, which), *to_owner(layer, names))
            if kind == "mix":
                sched.add(("ffn_dw_out", nxt_layer, which), *between_chips(layer, names, names))
            else:
                sched.add(("ffn_dw_out", nxt_layer, which), *between_chips(layer, names[:1]))
                sched.add(("ffn_dh", nxt_layer, which), *between_chips(layer, names[1:], names))
            sched.add(("ffn_dw_in", nxt_layer, which), *from_owner(layer, names))

    small_grads = [None] * depth
    for l in reversed(range(depth)):
        s1, s2, s3 = saved[l]
        dx, dxb, dg2 = _ffn_bwd(dx, dxb, s3, norm_ffn2[l][None], wg, 2, l, stacks, sched)
        dx, dxb, gm = _mix_bwd(dx, dxb, s2, norm_mix[l][None], wg, small[l], tables, l, stacks, sched)
        dx, dxb, dg1 = _ffn_bwd(dx, dxb, s1, norm_ffn1[l][None], wg, 1, l, stacks, sched)
        gm["norm_ffn1"] = dg1[0]
        gm["norm_ffn2"] = dg2[0]
        small_grads[l] = gm
    grad_x = dx[None]
    assert not sched.sites, sched.sites

    last_layer, last_kind = order[-1]
    names = SUBLAYER_WEIGHTS[last_kind]
    for name, (make, done) in (("to_owner", to_owner(last_layer, names)),
                               ("between_chips", between_chips(last_layer, names, names)),
                               ("from_owner", from_owner(last_layer, names))):
        cm = make()
        _standalone(name, cm)
        done(cm)
    big_grads = dict(finals)

    per_layer = [n for n in SMALL if n != "final_norm"]
    small_local = [jnp.stack([small_grads[l][n] for l in range(depth)]) for n in per_layer]
    small_local += [d_final[0], loss_part[0, :1]]
    small_shapes = [a.shape for a in small_local]
    summed = _unpack(_all_reduce_small(_pack(small_local)), small_shapes)
    loss = summed[-1][0]
    sg = dict(zip(per_layer + ["final_norm"], summed[:-1]))
    sg["conv_dw_w"] = lax.dynamic_slice_in_dim(sg["conv_dw_w"], chip * LANES, LANES, axis=2)

    delta, new_m, new_v = {}, {}, {}
    for n in BIG:
        delta[n], new_m[n], new_v[n], big_grads[n] = _adamw(w[n], big_grads[n], m[n], v[n])
    shapes = [w[n].shape for n in SMALL]
    packed = [_pack([d[n] for n in SMALL])[None] for d in (w, sg, m, v)]
    outs = _adamw(*packed)
    for d, buf in zip((delta, new_m, new_v), outs[:3]):
        d.update(zip(SMALL, _unpack(buf[0], shapes)))
    grads = {**big_grads, **sg}
    return (loss, grad_x, *[grads[n] for n in WEIGHTS], *[delta[n] for n in WEIGHTS],
            *[new_m[n] for n in WEIGHTS], *[new_v[n] for n in WEIGHTS])
```

```python
import functools

import jax
import jax.numpy as jnp
from jax import lax
from jax.experimental import pallas as pl
from jax.experimental.pallas import tpu as pltpu

F32 = jnp.float32
BF16 = jnp.bfloat16
MESH_ID = pl.DeviceIdType.MESH

V7X_VMEM_LIMIT_BYTES = 56 * 2**20
LANES = 128
SUBLANES = 8

HEAD_DIM = 64
N_Q_HEADS = 16
N_KV_HEADS = 4
GQ = N_Q_HEADS // N_KV_HEADS
BLK = 128
ROT_HALF = 8
ROPE_THETA = 500000.0
CONV_WIDTH = 31
CONV_PAD = 32
CONV_CH = 512
SGU_CH = 512
SGU_HEADS = 8
Q_END = N_Q_HEADS * HEAD_DIM
K_END = Q_END + N_KV_HEADS * HEAD_DIM
V_END = K_END + N_KV_HEADS * HEAD_DIM
CONV_END = V_END + 2 * CONV_CH
IN_COLS = CONV_END + 2 * SGU_CH
NORM_EPS = 1e-5
FFN_RESIDUAL_WEIGHT = 0.5
N_CHIPS = 4
N_DEV = 8

ADAM_LR = 0.001
ADAM_B1 = 0.9
ADAM_B2 = 0.999
ADAM_EPS = 1e-08
ADAM_WD = 0.01
ADAM_STEP = 10

NN = (((1,), (0,)), ((), ()))
NT = (((1,), (1,)), ((), ()))
TN = (((0,), (0,)), ((), ()))


def _pick(n, cands):
    for c in cands:
        if n % c == 0:
            return c
    raise ValueError(f"no tile of {cands} divides {n}")


def _params(n_axes):
    return pltpu.CompilerParams(dimension_semantics=("arbitrary",) * n_axes, vmem_limit_bytes=V7X_VMEM_LIMIT_BYTES)


def _call(body, **kw):
    return pl.pallas_call(body, **kw)


def _sigmoid(x):
    return 1.0 / (1.0 + jnp.exp(-x))


class _Comm:
    def __init__(self, reads, aliased, fresh, sems, start, finish):
        self.reads, self.aliased, self.fresh, self.sems = list(reads), list(aliased), list(fresh), list(sems)
        self.start, self.finish = start, finish
        self.aliased_out, self.fresh_out = None, None


def _merge_comms(comms):
    comms = [cm for cm in comms if cm is not None]
    if not comms:
        return None
    if len(comms) == 1:
        return comms[0]

    def split(refs, counts):
        out, off = [], 0
        for n in counts:
            out.append(refs[off:off + n])
            off += n
        return out

    def run(which):
        def f(rd, al, fr, sm):
            parts = zip(split(rd, [len(cm.reads) for cm in comms]), split(al, [len(cm.aliased) for cm in comms]),
                        split(fr, [len(cm.fresh) for cm in comms]), split(sm, [len(cm.sems) for cm in comms]))
            for cm, (r, a, f_, s) in zip(comms, parts):
                getattr(cm, which)(r, a, f_, s)
        return f

    merged = _Comm(sum((cm.reads for cm in comms), []), sum((cm.aliased for cm in comms), []),
                   sum((cm.fresh for cm in comms), []), sum((cm.sems for cm in comms), []), run("start"), run("finish"))
    merged.parts = comms
    return merged


def _hosted_call(body, comm, name, grid, inputs, in_specs, out_shape, out_specs, scratch_shapes=(), aliases=None):
    n_in, n_out, n_scr = len(inputs), len(out_shape), len(scratch_shapes)
    aliases = dict(aliases or {})
    if comm is None:
        return _call(body, name=name, grid=grid, in_specs=list(in_specs), out_specs=list(out_specs),
                     out_shape=list(out_shape), scratch_shapes=list(scratch_shapes), input_output_aliases=aliases,
                     compiler_params=_params(len(grid)))(*inputs)
    nr, na, nf = len(comm.reads), len(comm.aliased), len(comm.fresh)

    def full(*refs):
        ins = refs[:n_in]
        rd = refs[n_in:n_in + nr]
        pos = n_in + nr + na
        outs = refs[pos:pos + n_out]
        al = refs[pos + n_out:pos + n_out + na]
        fr = refs[pos + n_out + na:pos + n_out + na + nf]
        pos = pos + n_out + na + nf
        scr = refs[pos:pos + n_scr]
        sems = refs[pos + n_scr:]
        first, last = None, None
        for axis, size in enumerate(grid):
            f, l = pl.program_id(axis) == 0, pl.program_id(axis) == size - 1
            first = f if first is None else jnp.logical_and(first, f)
            last = l if last is None else jnp.logical_and(last, l)

        @pl.when(first)
        def _():
            comm.start(rd, al, fr, sems)

        body(*ins, *outs, *scr)

        @pl.when(last)
        def _():
            comm.finish(rd, al, fr, sems)

    hbm = pl.BlockSpec(memory_space=pltpu.HBM)
    for i in range(na):
        aliases[n_in + nr + i] = n_out + i
    struct = [jax.ShapeDtypeStruct(a.shape, a.dtype) for a in comm.aliased]
    res = _call(full, name=name, grid=grid, in_specs=list(in_specs) + [hbm] * (nr + na),
                out_specs=list(out_specs) + [hbm] * (na + nf), out_shape=list(out_shape) + struct + comm.fresh,
                scratch_shapes=list(scratch_shapes) + comm.sems, input_output_aliases=aliases,
                compiler_params=pltpu.CompilerParams(dimension_semantics=("arbitrary",) * len(grid),
                                                     vmem_limit_bytes=V7X_VMEM_LIMIT_BYTES, has_side_effects=True),
                )(*inputs, *comm.reads, *comm.aliased)
    _deliver(comm, res[n_out:n_out + na], res[n_out + na:])
    return res[:n_out]


def _deliver(comm, aliased_out, fresh_out):
    comm.aliased_out, comm.fresh_out = list(aliased_out), list(fresh_out)
    off_a = off_f = 0
    for part in getattr(comm, "parts", []):
        _deliver(part, aliased_out[off_a:off_a + len(part.aliased)], fresh_out[off_f:off_f + len(part.fresh)])
        off_a += len(part.aliased)
        off_f += len(part.fresh)


def _standalone(name, comm):
    def body(*refs):
        nr, na, nf = len(comm.reads), len(comm.aliased), len(comm.fresh)
        rd, al, fr, sems = refs[:nr], refs[nr + na:nr + 2 * na], refs[nr + 2 * na:nr + 2 * na + nf], refs[nr + 2 * na + nf:]
        comm.start(rd, al, fr, sems)
        comm.finish(rd, al, fr, sems)

    nr, na, nf = len(comm.reads), len(comm.aliased), len(comm.fresh)
    struct = [jax.ShapeDtypeStruct(a.shape, a.dtype) for a in comm.aliased]
    res = _call(body, name=name, in_specs=_hbm_specs(nr + na), out_specs=_hbm_specs(na + nf),
                out_shape=struct + comm.fresh, scratch_shapes=comm.sems,
                input_output_aliases={nr + i: i for i in range(na)},
                compiler_params=_comm_params())(*comm.reads, *comm.aliased)
    _deliver(comm, res[:na], res[na:])


def _matmul(name, grid, a_ops, b_ops, terms, dims, out_shape, out_specs, epilogue, extra_ops=(), nk=1,
            acc_shapes=(), alias=None, comm=None):
    na, nb, ne, no = len(a_ops), len(b_ops), len(extra_ops), len(out_shape)

    def body(*refs):
        a = refs[:na]
        b = refs[na:na + nb]
        e = refs[na + nb:na + nb + ne]
        first_out = na + nb + ne + (1 if alias is not None else 0)
        o = refs[first_out:first_out + no]
        accs = refs[first_out + no:]

        def partial(t):
            tot = None
            for ai, bi in t:
                d = lax.dot_general(a[ai][...], b[bi][...], dims, preferred_element_type=F32)
                tot = d if tot is None else tot + d
            return tot

        if nk == 1:
            epilogue([partial(t) for t in terms], e, o)
        else:
            k = pl.program_id(len(grid) - 1)

            @pl.when(k == 0)
            def _():
                for acc in accs:
                    acc[...] = jnp.zeros(acc.shape, F32)

            for acc, t in zip(accs, terms):
                acc[...] += partial(t)

            @pl.when(k == nk - 1)
            def _():
                epilogue([acc[...] for acc in accs], e, o)

    ops = list(a_ops) + list(b_ops) + list(extra_ops)
    arrays = [x for x, _ in ops]
    in_specs = [s for _, s in ops]
    aliases = {}
    if alias is not None:
        arrays.append(alias[0])
        in_specs.append(pl.BlockSpec(memory_space=pl.ANY))
        aliases[len(arrays) - 1] = alias[1]
    scratch = [pltpu.VMEM(s, F32) for s in acc_shapes] if nk > 1 else []
    return _hosted_call(body, comm, name, grid, arrays, in_specs, out_shape, out_specs, scratch, aliases)


def _mm_ffn_in(h, w_g, layer, comm=None):
    t_len, d = h.shape
    fs = w_g.shape[3]
    f = 2 * fs
    tm = _pick(t_len, (2048, 1024, 512))
    tn = _pick(fs, (256, 128))
    nj = fs // tn

    def epilogue(accs, e, o):
        g, u = accs
        o[0][0] = g.astype(BF16)
        o[0][1] = u.astype(BF16)
        o[1][...] = (g * _sigmoid(g) * u).astype(BF16)

    return _matmul(
        "ffn_in", (t_len // tm, 2, nj),
        [(h, pl.BlockSpec((tm, d), lambda i, s, j: (i, 0)))],
        [(w_g, pl.BlockSpec((None, None, d, tn), lambda i, s, j: (layer, s, 0, j))),
         (w_g, pl.BlockSpec((None, None, d, tn), lambda i, s, j: (layer, s + 2, 0, j)))],
        [[(0, 0)], [(0, 1)]], NN,
        [jax.ShapeDtypeStruct((2, t_len, f), BF16), jax.ShapeDtypeStruct((t_len, f), BF16)],
        [pl.BlockSpec((2, tm, tn), lambda i, s, j: (0, i, s * nj + j)),
         pl.BlockSpec((tm, tn), lambda i, s, j: (i, s * nj + j))],
        epilogue, comm=comm)


def _mm_out_res(name, a, w_g, layer, x, scale, comm=None):
    t_len = a.shape[0]
    ks, n = w_g.shape[2], w_g.shape[3]
    tm = _pick(t_len, (1024, 512))
    tn = _pick(n, (1024,))
    tk = _pick(ks, (1408, 512, 256))
    nks = ks // tk

    def epilogue(accs, e, o):
        o[0][...] = e[0][...] + scale * accs[0]

    return _matmul(
        name, (t_len // tm, n // tn, N_CHIPS * nks),
        [(a, pl.BlockSpec((tm, tk), lambda i, j, k: (i, k)))],
        [(w_g, pl.BlockSpec((None, None, tk, tn), lambda i, j, k: (layer, k // nks, k % nks, j)))],
        [[(0, 0)]], NN,
        [jax.ShapeDtypeStruct((t_len, n), F32)],
        [pl.BlockSpec((tm, tn), lambda i, j, k: (i, j))],
        epilogue, extra_ops=[(x, pl.BlockSpec((tm, tn), lambda i, j, k: (i, j)))],
        nk=N_CHIPS * nks, acc_shapes=[(tm, tn)], comm=comm)


def _mm_proj(h, w_g, layer, comm=None):
    t_len, d = h.shape
    cs = w_g.shape[3]
    tm = _pick(t_len, (1024, 512))

    def epilogue(accs, e, o):
        o[0][...] = accs[0]

    return _matmul(
        "mix_in", (t_len // tm, N_CHIPS),
        [(h, pl.BlockSpec((tm, d), lambda i, s: (i, 0)))],
        [(w_g, pl.BlockSpec((None, None, d, cs), lambda i, s: (layer, s, 0, 0)))],
        [[(0, 0)]], NN,
        [jax.ShapeDtypeStruct((t_len, N_CHIPS * cs), F32)],
        [pl.BlockSpec((tm, cs), lambda i, s: (i, s))],
        epilogue, comm=comm)


def _mm_dact_swiglu(dxb, w_g, layer, gu, scale, comm=None):
    t_len, d = dxb.shape
    rs = w_g.shape[2]
    tm = _pick(t_len, (512,))
    tn = _pick(rs, (1408, 256, 128))
    nj = rs // tn

    def epilogue(accs, e, o):
        dact = scale * accs[0]
        g = e[0][0].astype(F32)
        u = e[0][1].astype(F32)
        sig = _sigmoid(g)
        o[0][0] = (dact * u * (sig * (1.0 + g * (1.0 - sig)))).astype(BF16)
        o[0][1] = (dact * (g * sig)).astype(BF16)

    gu_spec = pl.BlockSpec((2, tm, tn), lambda i, s, j: (0, i, s * nj + j))
    return _matmul(
        "ffn_dact", (t_len // tm, N_CHIPS, nj),
        [(dxb, pl.BlockSpec((tm, d), lambda i, s, j: (i, 0)))],
        [(w_g, pl.BlockSpec((None, None, tn, d), lambda i, s, j: (layer, s, j, 0)))],
        [[(0, 0)]], NT,
        [jax.ShapeDtypeStruct(gu.shape, BF16)], [gu_spec],
        epilogue, extra_ops=[(gu, gu_spec)], comm=comm)[0]


def _mm_dh_ffn(dgu, w_g, layer, comm=None):
    t_len = dgu.shape[1]
    d, fs = w_g.shape[2], w_g.shape[3]
    tm = _pick(t_len, (1024, 512))
    tk = _pick(fs, (256, 128))
    nks = fs // tk
    nk = 2 * nks

    def epilogue(accs, e, o):
        o[0][...] = accs[0]

    return _matmul(
        "ffn_dh", (t_len // tm, nk),
        [(dgu, pl.BlockSpec((None, tm, tk), lambda i, k: (0, i, k))),
         (dgu, pl.BlockSpec((None, tm, tk), lambda i, k: (1, i, k)))],
        [(w_g, pl.BlockSpec((None, None, d, tk), lambda i, k: (layer, k // nks, 0, k % nks))),
         (w_g, pl.BlockSpec((None, None, d, tk), lambda i, k: (layer, k // nks + 2, 0, k % nks)))],
        [[(0, 0), (1, 1)]], NT,
        [jax.ShapeDtypeStruct((t_len, d), F32)],
        [pl.BlockSpec((tm, d), lambda i, k: (i, 0))],
        epilogue, nk=nk, acc_shapes=[(tm, d)], comm=comm)[0]


def _mm_dw(name, a, a_spec_of, b, b_spec_of, layer, stack, rows, cols, tn, scale, comm=None):
    t_len = a.shape[0]
    tt = _pick(t_len, (1024, 512))
    nj = cols // tn

    def epilogue(accs, e, o):
        o[0][...] = (scale * accs[0]).astype(BF16)

    shape = jax.ShapeDtypeStruct((2, N_CHIPS, rows, cols), BF16)
    return _matmul(
        name, (N_CHIPS, nj, t_len // tt),
        [(a, a_spec_of(tt))], [(b, b_spec_of(tt, tn, nj))],
        [[(0, 0)]], TN, [shape],
        [pl.BlockSpec((None, None, rows, tn), lambda s, j, t: (layer, s, 0, j))],
        epilogue, nk=t_len // tt, acc_shapes=[(rows, tn)],
        alias=None if stack is None else (stack, 0), comm=comm)[0]


def _dw_ffn_in(h, dgu, layer, stack, comm=None):
    d = h.shape[1]
    fs = dgu.shape[2] // 2
    tn = _pick(fs, (1408, 256))
    return _mm_dw(
        "ffn_dw_in", h, lambda tt: pl.BlockSpec((tt, d), lambda s, j, t: (t, 0)),
        dgu, lambda tt, tn_, nj: pl.BlockSpec((None, tt, tn_), lambda s, j, t: (s // 2, t, (s % 2) * nj + j)),
        layer, stack, d, fs, tn, 1.0, comm)


def _dw_rows(name, a, dxb, layer, stack, scale, comm=None):
    rs = a.shape[1] // N_CHIPS
    d = dxb.shape[1]
    tn = _pick(d, (1024,))
    return _mm_dw(
        name, a, lambda tt: pl.BlockSpec((tt, rs), lambda s, j, t: (t, s)),
        dxb, lambda tt, tn_, nj: pl.BlockSpec((tt, tn_), lambda s, j, t: (t, j)),
        layer, stack, rs, d, tn, scale, comm)


def _dw_mix_in(h, dp, layer, stack):
    d = h.shape[1]
    cs = dp.shape[1] // N_CHIPS
    return _mm_dw(
        "mix_dw_in", h, lambda tt: pl.BlockSpec((tt, d), lambda s, j, t: (t, 0)),
        dp, lambda tt, tn_, nj: pl.BlockSpec((tt, tn_), lambda s, j, t: (t, s)),
        layer, stack, d, cs, cs, 1.0)


def _mm_dmix(dxb, w_g, layer, comm=None):
    t_len, d = dxb.shape
    rs = w_g.shape[2]
    tm = _pick(t_len, (1024, 512))

    def epilogue(accs, e, o):
        o[0][...] = accs[0]

    return _matmul(
        "mix_dout", (t_len // tm, N_CHIPS),
        [(dxb, pl.BlockSpec((tm, d), lambda i, s: (i, 0)))],
        [(w_g, pl.BlockSpec((None, None, rs, d), lambda i, s: (layer, s, 0, 0)))],
        [[(0, 0)]], NT,
        [jax.ShapeDtypeStruct((t_len, N_CHIPS * rs), F32)],
        [pl.BlockSpec((tm, rs), lambda i, s: (i, s))],
        epilogue, comm=comm)[0]


def _mm_dh_mix(dp, w_g, layer, comm=None):
    t_len = dp.shape[0]
    d, cs = w_g.shape[2], w_g.shape[3]
    tm = _pick(t_len, (1024, 512))

    def epilogue(accs, e, o):
        o[0][...] = accs[0]

    return _matmul(
        "mix_dh", (t_len // tm, N_CHIPS),
        [(dp, pl.BlockSpec((tm, cs), lambda i, k: (i, k)))],
        [(w_g, pl.BlockSpec((None, None, d, cs), lambda i, k: (layer, k, 0, 0)))],
        [[(0, 0)]], NT,
        [jax.ShapeDtypeStruct((t_len, d), F32)],
        [pl.BlockSpec((tm, d), lambda i, k: (i, 0))],
        epilogue, nk=N_CHIPS, acc_shapes=[(tm, d)], comm=comm)[0]


def _rms_stats(x):
    r = lax.rsqrt(jnp.mean(x * x, axis=-1, keepdims=True) + NORM_EPS)
    return r, x * r


def _accumulate(ref, part, first):
    @pl.when(first)
    def _():
        ref[...] = part

    @pl.when(jnp.logical_not(first))
    def _():
        ref[...] += part


def _rmsnorm_fwd(x, g):
    t_len, d = x.shape
    tm = _pick(t_len, (512,))

    def body(x_ref, g_ref, h_ref):
        _, xhat = _rms_stats(x_ref[...])
        h_ref[...] = (xhat * g_ref[...]).astype(BF16)

    row = pl.BlockSpec((tm, d), lambda i: (i, 0))
    vec = pl.BlockSpec((1, d), lambda i: (0, 0))
    return _call(body, name="rmsnorm_fwd", grid=(t_len // tm,), in_specs=[row, vec], out_specs=row,
                 out_shape=jax.ShapeDtypeStruct((t_len, d), BF16), compiler_params=_params(1))(x, g)


def _rmsnorm_bwd(dh, x, g, dres):
    t_len, d = x.shape
    tm = _pick(t_len, (256,))

    def body(dh_ref, x_ref, g_ref, dres_ref, dx_ref, dxb_ref, dg_ref):
        r, xhat = _rms_stats(x_ref[...])
        dh_v = dh_ref[...]
        gd = dh_v * g_ref[...]
        dx = dres_ref[...] + r * (gd - xhat * jnp.mean(gd * xhat, axis=-1, keepdims=True))
        dx_ref[...] = dx
        dxb_ref[...] = dx.astype(BF16)
        _accumulate(dg_ref, jnp.sum(dh_v * xhat, axis=0, keepdims=True), pl.program_id(0) == 0)

    row = pl.BlockSpec((tm, d), lambda i: (i, 0))
    vec = pl.BlockSpec((1, d), lambda i: (0, 0))
    return _call(body, name="rmsnorm_bwd", grid=(t_len // tm,), in_specs=[row, row, vec, row],
                 out_specs=[row, row, vec],
                 out_shape=[jax.ShapeDtypeStruct((t_len, d), F32), jax.ShapeDtypeStruct((t_len, d), BF16),
                            jax.ShapeDtypeStruct((1, d), F32)],
                 compiler_params=_params(1))(dh, x, g, dres)


def _loss_head(x, g, target):
    t_len, d = x.shape
    tm = _pick(t_len, (256,))

    def body(x_ref, g_ref, t_ref, dx_ref, dxb_ref, dg_ref, loss_ref):
        first = pl.program_id(0) == 0
        r, xhat = _rms_stats(x_ref[...])
        g_v = g_ref[...]
        err = xhat * g_v - t_ref[...]
        per_token = jnp.mean(err * err, axis=-1, keepdims=True)
        part = 0.5 * jnp.sum(per_token, axis=0, keepdims=True)
        _accumulate(loss_ref, jnp.broadcast_to(part, (1, LANES)), first)
        dy = err * (1.0 / d)
        _accumulate(dg_ref, jnp.sum(dy * xhat, axis=0, keepdims=True), first)
        gd = dy * g_v
        dx = r * (gd - xhat * jnp.mean(gd * xhat, axis=-1, keepdims=True))
        dx_ref[...] = dx
        dxb_ref[...] = dx.astype(BF16)

    row = pl.BlockSpec((tm, d), lambda i: (i, 0))
    vec = pl.BlockSpec((1, d), lambda i: (0, 0))
    return _call(body, name="loss_head", grid=(t_len // tm,), in_specs=[row, vec, row],
                 out_specs=[row, row, vec, pl.BlockSpec((1, LANES), lambda i: (0, 0))],
                 out_shape=[jax.ShapeDtypeStruct((t_len, d), F32), jax.ShapeDtypeStruct((t_len, d), BF16),
                            jax.ShapeDtypeStruct((1, d), F32), jax.ShapeDtypeStruct((1, LANES), F32)],
                 compiler_params=_params(1))(x, g, target)


def _ln_stats(x):
    mu = jnp.mean(x, axis=-1, keepdims=True)
    xc = x - mu
    r = lax.rsqrt(jnp.mean(xc * xc, axis=-1, keepdims=True) + NORM_EPS)
    return r, xc * r


def _ln_bwd(dy, r, xhat, g):
    dxh = dy * g
    return r * (dxh - jnp.mean(dxh, axis=-1, keepdims=True) - xhat * jnp.mean(dxh * xhat, axis=-1, keepdims=True))


def _rope_tables(positions):
    t_len = positions.shape[-1]
    inv_freq = 1.0 / (ROPE_THETA ** (jnp.arange(0, 2 * ROT_HALF, 2, dtype=F32) / (2 * ROT_HALF)))
    ang = positions.astype(F32).reshape(t_len, 1) * inv_freq
    cos = jnp.tile(jnp.cos(ang), (1, LANES // ROT_HALF))
    sin = jnp.tile(jnp.sin(ang), (1, LANES // ROT_HALF))
    lane = jnp.arange(LANES) % HEAD_DIM
    c = jnp.where(lane < 2 * ROT_HALF, cos, 1.0)
    s1 = jnp.where(lane < ROT_HALF, -sin, 0.0)
    s2 = jnp.where((lane >= ROT_HALF) & (lane < 2 * ROT_HALF), sin, 0.0)
    return c.astype(F32), s1.astype(F32), s2.astype(F32)


def _rope_fwd(p, tables):
    t_len = p.shape[0]
    tm = _pick(t_len, (256,))
    n_rot = K_END // LANES

    def body(p_ref, c_ref, s1_ref, s2_ref, o_ref):
        c, s1, s2 = c_ref[...], s1_ref[...], s2_ref[...]
        for j in range(V_END // LANES):
            sl = slice(j * LANES, (j + 1) * LANES)
            t = p_ref[:, sl]
            if j < n_rot:
                t = t * c + pltpu.roll(t, LANES - ROT_HALF, 1) * s1 + pltpu.roll(t, ROT_HALF, 1) * s2
            if j < Q_END // LANES:
                t = t * ATTN_SCALE
            o_ref[:, sl] = t.astype(BF16)

    tab = pl.BlockSpec((tm, LANES), lambda i: (i, 0))
    blk = pl.BlockSpec((tm, V_END), lambda i: (i, 0))
    return _call(body, name="rope_fwd", grid=(t_len // tm,), in_specs=[blk, tab, tab, tab], out_specs=blk,
                 out_shape=jax.ShapeDtypeStruct((t_len, V_END), BF16), compiler_params=_params(1))(p, *tables)


def _assemble_dp(dq, dkc, dkp, dvc, dvp, tables, dalin, dagate, du, dvin):
    t_len = dq.shape[0]
    steps = t_len // ATTN_STEP

    def body(dq_ref, dkc_ref, dkp_ref, dvc_ref, dvp_ref, c_ref, s1_ref, s2_ref, dalin_ref, dagate_ref,
             du_ref, dvin_ref, o_ref):
        keep = (pl.program_id(0) < steps - 1).astype(F32)
        for part in range(ATTN_STEP // BLK):
            rows = slice(part * BLK, (part + 1) * BLK)
            c, s1, s2 = c_ref[rows, :], s1_ref[rows, :], s2_ref[rows, :]

            def unrotate(dr):
                return dr * c + pltpu.roll(dr * s1, ROT_HALF, 1) + pltpu.roll(dr * s2, LANES - ROT_HALF, 1)

            for j in range(Q_END // LANES):
                sl = slice(j * LANES, (j + 1) * LANES)
                o_ref[rows, sl] = unrotate(dq_ref[rows, sl]).astype(BF16)
            for j in range((K_END - Q_END) // LANES):
                sl = slice(j * LANES, (j + 1) * LANES)
                dk, dv = dkc_ref[rows, sl], dvc_ref[rows, sl]
                if part == ATTN_STEP // BLK - 1:
                    dk = dk + keep * dkp_ref[:, sl]
                    dv = dv + keep * dvp_ref[:, sl]
                o_ref[rows, Q_END + j * LANES:Q_END + (j + 1) * LANES] = unrotate(dk).astype(BF16)
                o_ref[rows, K_END + j * LANES:K_END + (j + 1) * LANES] = dv.astype(BF16)
        o_ref[:, V_END:V_END + CONV_CH] = dalin_ref[...]
        o_ref[:, V_END + CONV_CH:CONV_END] = dagate_ref[...]
        o_ref[:, CONV_END:CONV_END + SGU_CH] = du_ref[...]
        o_ref[:, CONV_END + SGU_CH:IN_COLS] = dvin_ref[...]

    def cur(w):
        return pl.BlockSpec((ATTN_STEP, w), lambda i: (i, 0))

    def nxt(w):
        return pl.BlockSpec((BLK, w), lambda i: (jnp.minimum(i + 1, steps - 1), 0))

    kvw = K_END - Q_END
    return _call(body, name="assemble_dp", grid=(steps,),
                 in_specs=[cur(Q_END), cur(kvw), nxt(kvw), cur(kvw), nxt(kvw), cur(LANES), cur(LANES), cur(LANES),
                           cur(CONV_CH), cur(CONV_CH), cur(SGU_CH), cur(SGU_CH)],
                 out_specs=cur(IN_COLS), out_shape=jax.ShapeDtypeStruct((t_len, IN_COLS), BF16),
                 compiler_params=_params(1))(dq, dkc, dkp, dvc, dvp, *tables, dalin, dagate, du, dvin)


ATTN_STEP = 2 * BLK
ATTN_SCALE = HEAD_DIM ** -0.5
MASKED = -1e30


def _attn_bias():
    qi = jnp.arange(GQ * BLK)[:, None] % BLK
    kj = jnp.arange(2 * BLK)[None, :]
    dist = qi + BLK - kj
    band = (dist >= 0) & (dist < BLK)
    return jnp.stack([jnp.where(band & (kj >= BLK), 0.0, MASKED), jnp.where(band, 0.0, MASKED)]).astype(F32)


def _attn_chains(q_ref, kp_ref, kc_ref, vp_ref, vc_ref):
    kc, vc = kc_ref[...], vc_ref[...]
    rows_a, rows_b = slice(0, BLK), slice(BLK, ATTN_STEP)
    return [
        (rows_a, q_ref[:, rows_a, :].reshape(GQ * BLK, HEAD_DIM), jnp.concatenate([kp_ref[...], kc[:BLK]], axis=0),
         jnp.concatenate([vp_ref[...], vc[:BLK]], axis=0)),
        (rows_b, q_ref[:, rows_b, :].reshape(GQ * BLK, HEAD_DIM), kc, vc),
    ]


def _attn_weights(q, kk, bias, snk):
    s = lax.dot_general(q, kk, NT, preferred_element_type=F32) + bias
    m = jnp.maximum(jnp.max(s, axis=-1, keepdims=True), snk)
    e = jnp.exp(s - m)
    es = jnp.exp(snk - m)
    return e, es, 1.0 / (jnp.sum(e, axis=-1, keepdims=True) + es)


def _attn_specs(steps):
    q_spec = pl.BlockSpec((GQ, ATTN_STEP, HEAD_DIM), lambda g, n: (g, n, 0))
    cur = pl.BlockSpec((None, ATTN_STEP, HEAD_DIM), lambda g, n: (g, n, 0))
    prev = pl.BlockSpec((None, BLK, HEAD_DIM), lambda g, n: (g, jnp.maximum(2 * n - 1, 0), 0))
    snk = pl.BlockSpec((None, GQ * BLK, 1), lambda g, n: (g, 0, 0))
    bias_a = pl.BlockSpec((None, GQ * BLK, 2 * BLK), lambda g, n: (jnp.minimum(n, 1), 0, 0))
    bias_b = pl.BlockSpec((None, GQ * BLK, 2 * BLK), lambda g, n: (1, 0, 0))
    return q_spec, cur, prev, snk, bias_a, bias_b


def _attn_fwd(q, k, v, snk, comm=None):
    t_len = q.shape[1]
    steps = t_len // ATTN_STEP

    def body(q_ref, kp_ref, kc_ref, vp_ref, vc_ref, snk_ref, ba_ref, bb_ref, o_ref):
        snk = snk_ref[...]
        for (rows, q_v, kk, vv), b_ref in zip(_attn_chains(q_ref, kp_ref, kc_ref, vp_ref, vc_ref), (ba_ref, bb_ref)):
            e, _, inv = _attn_weights(q_v, kk, b_ref[...], snk)
            o = lax.dot_general(e.astype(BF16), vv, NN, preferred_element_type=F32) * inv
            o_ref[:, rows, :] = o.reshape(GQ, BLK, HEAD_DIM).astype(BF16)

    q_spec, cur, prev, snk_spec, bias_a, bias_b = _attn_specs(steps)
    bias = _attn_bias()
    return _hosted_call(body, comm, "attn_fwd", (N_KV_HEADS, steps), [q, k, k, v, v, snk, bias, bias],
                        [q_spec, prev, cur, prev, cur, snk_spec, bias_a, bias_b],
                        [jax.ShapeDtypeStruct(q.shape, BF16)], [q_spec])[0]


def _attn_bwd(q, k, v, snk, do, comm=None):
    t_len = q.shape[1]
    steps = t_len // ATTN_STEP

    def body(q_ref, kp_ref, kc_ref, vp_ref, vc_ref, snk_ref, ba_ref, bb_ref, do_ref, dq_ref, dkp_ref, dkc_ref,
             dvp_ref, dvc_ref, dsnk_ref):
        snk = snk_ref[...]
        row = lax.broadcasted_iota(jnp.int32, (SUBLANES, LANES), 0)
        tile = jnp.zeros((SUBLANES, LANES), F32)
        grads = []
        for (rows, q_v, kk, vv), b_ref in zip(_attn_chains(q_ref, kp_ref, kc_ref, vp_ref, vc_ref), (ba_ref, bb_ref)):
            e, es, inv = _attn_weights(q_v, kk, b_ref[...], snk)
            p = e * inv
            do_v = do_ref[:, rows, :].reshape(GQ * BLK, HEAD_DIM).astype(BF16)
            dp = lax.dot_general(do_v, vv, NT, preferred_element_type=F32)
            delta = jnp.sum(p * dp, axis=-1, keepdims=True)
            ds = (p * (dp - delta)).astype(BF16)
            dq = lax.dot_general(ds, kk, NN, preferred_element_type=F32) * ATTN_SCALE
            dq_ref[:, rows, :] = dq.reshape(GQ, BLK, HEAD_DIM)
            grads.append((lax.dot_general(ds, q_v, TN, preferred_element_type=F32),
                          lax.dot_general(p.astype(BF16), do_v, TN, preferred_element_type=F32)))
            per_row = -(es * inv) * delta
            for hh in range(GQ):
                tot = jnp.sum(per_row[hh * BLK:(hh + 1) * BLK], axis=0, keepdims=True)
                tile = tile + jnp.where(row == hh, tot, 0.0)
        (dk_a, dv_a), (dk_b, dv_b) = grads
        dkp_ref[...] = dk_a[:BLK]
        dvp_ref[...] = dv_a[:BLK]
        dkc_ref[0:BLK, :] = dk_a[BLK:] + dk_b[:BLK]
        dvc_ref[0:BLK, :] = dv_a[BLK:] + dv_b[:BLK]
        dkc_ref[BLK:ATTN_STEP, :] = dk_b[BLK:]
        dvc_ref[BLK:ATTN_STEP, :] = dv_b[BLK:]
        _accumulate(dsnk_ref, tile, pl.program_id(1) == 0)

    q_spec, cur, prev, snk_spec, bias_a, bias_b = _attn_specs(steps)
    bias = _attn_bias()
    step_blk = pl.BlockSpec((None, BLK, HEAD_DIM), lambda g, n: (g, n, 0))
    kv_shape = jax.ShapeDtypeStruct(k.shape, F32)
    prev_shape = jax.ShapeDtypeStruct((N_KV_HEADS, steps * BLK, HEAD_DIM), F32)
    return _hosted_call(
        body, comm, "attn_bwd", (N_KV_HEADS, steps), [q, k, k, v, v, snk, bias, bias, do],
        [q_spec, prev, cur, prev, cur, snk_spec, bias_a, bias_b, q_spec],
        [jax.ShapeDtypeStruct(q.shape, F32), prev_shape, kv_shape, prev_shape, kv_shape,
         jax.ShapeDtypeStruct((N_KV_HEADS, SUBLANES, LANES), F32)],
        [q_spec, step_blk, cur, step_blk, cur, pl.BlockSpec((None, SUBLANES, LANES), lambda g, n: (g, 0, 0))])


CONV_CHUNK = 256


def _shift_up(win, s):
    n = win.shape[0]
    return win if s == 0 else pltpu.roll(win, n - s, 0)


def _conv_col_specs(t_len):
    lin = pl.BlockSpec((t_len, LANES), lambda j: (0, V_END // LANES + j))
    gate = pl.BlockSpec((t_len, LANES), lambda j: (0, (V_END + CONV_CH) // LANES + j))
    col = pl.BlockSpec((t_len, LANES), lambda j: (0, j))
    wsp = pl.BlockSpec((CONV_PAD, LANES), lambda j: (0, j))
    return lin, gate, col, wsp


def _conv_fwd(p, w, b):
    t_len = p.shape[0]
    ch = CONV_CHUNK

    def body(lin_ref, gate_ref, w_ref, b_ref, y_ref, hp_ref):
        hp_ref[0:CONV_PAD, :] = jnp.zeros((CONV_PAD, LANES), F32)

        def fill(c, carry):
            r0 = pl.multiple_of(c * ch, ch)
            hp_ref[pl.ds(r0 + CONV_PAD, ch), :] = lin_ref[pl.ds(r0, ch), :] * _sigmoid(gate_ref[pl.ds(r0, ch), :])
            return carry

        lax.fori_loop(0, t_len // ch, fill, 0)

        def conv(c, carry):
            r0 = pl.multiple_of(c * ch, ch)
            win = hp_ref[pl.ds(r0, ch + CONV_PAD), :]
            acc = jnp.zeros((ch, LANES), F32)
            for k in range(CONV_WIDTH):
                acc = acc + _shift_up(win, CONV_PAD - (CONV_WIDTH - 1) + k)[:ch] * w_ref[k:k + 1, :]
            y_ref[pl.ds(r0, ch), :] = acc + b_ref[...]
            return carry

        lax.fori_loop(0, t_len // ch, conv, 0)

    lin, gate, col, wsp = _conv_col_specs(t_len)
    return _call(body, name="conv_fwd", grid=(CONV_CH // LANES,),
                 in_specs=[lin, gate, wsp, pl.BlockSpec((1, LANES), lambda j: (0, j))], out_specs=col,
                 out_shape=jax.ShapeDtypeStruct((t_len, CONV_CH), F32),
                 scratch_shapes=[pltpu.VMEM((t_len + CONV_PAD, LANES), F32)],
                 compiler_params=_params(1))(p, p, w, b)


def _conv_post_fwd(y, g, b):
    t_len = y.shape[0]
    tm = _pick(t_len, (512,))

    def body(y_ref, g_ref, b_ref, o_ref):
        _, xhat = _ln_stats(y_ref[...])
        z = xhat * g_ref[...] + b_ref[...]
        o_ref[...] = (z * _sigmoid(z)).astype(BF16)

    row = pl.BlockSpec((tm, CONV_CH), lambda i: (i, 0))
    vec = pl.BlockSpec((1, CONV_CH), lambda i: (0, 0))
    return _call(body, name="conv_post_fwd", grid=(t_len // tm,), in_specs=[row, vec, vec], out_specs=row,
                 out_shape=jax.ShapeDtypeStruct((t_len, CONV_CH), BF16), compiler_params=_params(1))(y, g, b)


def _conv_post_bwd(dmix, y, g, b):
    t_len = y.shape[0]
    tm = _pick(t_len, (512,))

    def body(do_ref, y_ref, g_ref, b_ref, dy_ref, dg_ref, db_ref, dcb_ref):
        first = pl.program_id(0) == 0
        r, xhat = _ln_stats(y_ref[...])
        g_v = g_ref[...]
        z = xhat * g_v + b_ref[...]
        sig = _sigmoid(z)
        dz = do_ref[...] * (sig * (1.0 + z * (1.0 - sig)))
        _accumulate(db_ref, jnp.sum(dz, axis=0, keepdims=True), first)
        _accumulate(dg_ref, jnp.sum(dz * xhat, axis=0, keepdims=True), first)
        dy = _ln_bwd(dz, r, xhat, g_v)
        dy_ref[...] = dy
        _accumulate(dcb_ref, jnp.sum(dy, axis=0, keepdims=True), first)

    row = pl.BlockSpec((tm, CONV_CH), lambda i: (i, 0))
    do_spec = pl.BlockSpec((tm, CONV_CH), lambda i: (i, Q_END // CONV_CH))
    vec = pl.BlockSpec((1, CONV_CH), lambda i: (0, 0))
    vshape = jax.ShapeDtypeStruct((1, CONV_CH), F32)
    return _call(body, name="conv_post_bwd", grid=(t_len // tm,), in_specs=[do_spec, row, vec, vec],
                 out_specs=[row, vec, vec, vec],
                 out_shape=[jax.ShapeDtypeStruct((t_len, CONV_CH), F32), vshape, vshape, vshape],
                 compiler_params=_params(1))(dmix, y, g, b)


def _conv_bwd(p, dy, w):
    t_len = p.shape[0]
    ch = CONV_CHUNK

    def body(lin_ref, gate_ref, dy_ref, w_ref, dlin_ref, dgate_ref, dw_ref, hp_ref, dyp_ref):
        hp_ref[0:CONV_PAD, :] = jnp.zeros((CONV_PAD, LANES), F32)
        dyp_ref[t_len:t_len + CONV_PAD, :] = jnp.zeros((CONV_PAD, LANES), F32)
        dw_ref[...] = jnp.zeros((CONV_PAD, LANES), F32)

        def fill(c, carry):
            r0 = pl.multiple_of(c * ch, ch)
            hp_ref[pl.ds(r0 + CONV_PAD, ch), :] = lin_ref[pl.ds(r0, ch), :] * _sigmoid(gate_ref[pl.ds(r0, ch), :])
            dyp_ref[pl.ds(r0, ch), :] = dy_ref[pl.ds(r0, ch), :]
            return carry

        lax.fori_loop(0, t_len // ch, fill, 0)

        def step(c, carry):
            r0 = pl.multiple_of(c * ch, ch)
            win_h = hp_ref[pl.ds(r0, ch + CONV_PAD), :]
            win_dy = dyp_ref[pl.ds(r0, ch + CONV_PAD), :]
            dyc = win_dy[:ch]
            dh = jnp.zeros((ch, LANES), F32)
            for k in range(CONV_WIDTH):
                tap = _shift_up(win_h, CONV_PAD - (CONV_WIDTH - 1) + k)[:ch]
                dw_ref[k:k + 1, :] += jnp.sum(dyc * tap, axis=0, keepdims=True)
                dh = dh + _shift_up(win_dy, CONV_WIDTH - 1 - k)[:ch] * w_ref[k:k + 1, :]
            lin = lin_ref[pl.ds(r0, ch), :]
            sig = _sigmoid(gate_ref[pl.ds(r0, ch), :])
            dlin_ref[pl.ds(r0, ch), :] = (dh * sig).astype(BF16)
            dgate_ref[pl.ds(r0, ch), :] = (dh * lin * (sig * (1.0 - sig))).astype(BF16)
            return carry

        lax.fori_loop(0, t_len // ch, step, 0)

    lin, gate, col, wsp = _conv_col_specs(t_len)
    half = jax.ShapeDtypeStruct((t_len, CONV_CH), BF16)
    return _call(body, name="conv_bwd", grid=(CONV_CH // LANES,), in_specs=[lin, gate, col, wsp],
                 out_specs=[col, col, wsp],
                 out_shape=[half, half, jax.ShapeDtypeStruct((CONV_PAD, CONV_CH), F32)],
                 scratch_shapes=[pltpu.VMEM((t_len + CONV_PAD, LANES), F32), pltpu.VMEM((t_len + CONV_PAD, LANES), F32)],
                 compiler_params=_params(1))(p, p, dy, w)


def _sgu_mixed(v, w_ref, bt_ref, j):
    lane = lax.broadcasted_iota(jnp.int32, (BLK, LANES), 1)
    lo = lane < HEAD_DIM
    tri = lax.broadcasted_iota(jnp.int32, (BLK, BLK), 0) >= lax.broadcasted_iota(jnp.int32, (BLK, BLK), 1)
    vs = v[:, j * LANES:(j + 1) * LANES]
    v_lo = jnp.where(lo, vs, 0.0).astype(BF16)
    v_hi = jnp.where(lo, 0.0, vs).astype(BF16)
    w_lo = jnp.where(tri, w_ref[2 * j], 0.0).astype(BF16)
    w_hi = jnp.where(tri, w_ref[2 * j + 1], 0.0).astype(BF16)
    m = (lax.dot_general(w_lo, v_lo, NN, preferred_element_type=F32)
         + lax.dot_general(w_hi, v_hi, NN, preferred_element_type=F32))
    bias = jnp.where(lo, bt_ref[:, 2 * j:2 * j + 1], bt_ref[:, 2 * j + 1:2 * j + 2])
    return m + bias, (v_lo, v_hi, w_lo, w_hi, lo, tri)


def _sgu_specs():
    u_spec = pl.BlockSpec((BLK, SGU_CH), lambda i: (i, CONV_END // SGU_CH))
    v_spec = pl.BlockSpec((BLK, SGU_CH), lambda i: (i, CONV_END // SGU_CH + 1))
    vec = pl.BlockSpec((1, SGU_CH), lambda i: (0, 0))
    w_spec = pl.BlockSpec((SGU_HEADS, BLK, BLK), lambda i: (0, 0, 0))
    bt_spec = pl.BlockSpec((BLK, SGU_HEADS), lambda i: (0, 0))
    row = pl.BlockSpec((BLK, SGU_CH), lambda i: (i, 0))
    return u_spec, v_spec, vec, w_spec, bt_spec, row


def _sgu_fwd(p, g, b, w, bt):
    t_len = p.shape[0]

    def body(u_ref, vin_ref, g_ref, b_ref, w_ref, bt_ref, o_ref):
        _, xhat = _ln_stats(vin_ref[...])
        v = xhat * g_ref[...] + b_ref[...]
        for j in range(SGU_CH // LANES):
            m, _ = _sgu_mixed(v, w_ref, bt_ref, j)
            sl = slice(j * LANES, (j + 1) * LANES)
            o_ref[:, sl] = (u_ref[:, sl] * m).astype(BF16)

    u_spec, v_spec, vec, w_spec, bt_spec, row = _sgu_specs()
    return _call(body, name="sgu_fwd", grid=(t_len // BLK,), in_specs=[u_spec, v_spec, vec, vec, w_spec, bt_spec],
                 out_specs=row, out_shape=jax.ShapeDtypeStruct((t_len, SGU_CH), BF16),
                 compiler_params=_params(1))(p, p, g, b, w, bt)


def _sgu_bwd(p, dmix, g, b, w, bt):
    t_len = p.shape[0]

    def body(u_ref, vin_ref, do_ref, g_ref, b_ref, w_ref, bt_ref, du_ref, dvin_ref, dw_ref, dbt_ref, dg_ref,
             db_ref, dv_ref):
        first = pl.program_id(0) == 0
        r, xhat = _ln_stats(vin_ref[...])
        g_v = g_ref[...]
        v = xhat * g_v + b_ref[...]
        lane = lax.broadcasted_iota(jnp.int32, (BLK, LANES), 1)
        dbt = jnp.zeros((BLK, LANES), F32)

        @pl.when(first)
        def _():
            dw_ref[...] = jnp.zeros((SGU_HEADS, BLK, BLK), F32)

        for j in range(SGU_CH // LANES):
            m, (v_lo, v_hi, w_lo, w_hi, lo, tri) = _sgu_mixed(v, w_ref, bt_ref, j)
            sl = slice(j * LANES, (j + 1) * LANES)
            do_v = do_ref[:, sl]
            du_ref[:, sl] = (do_v * m).astype(BF16)
            dm = do_v * u_ref[:, sl]
            dm_lo = jnp.where(lo, dm, 0.0)
            dm_hi = jnp.where(lo, 0.0, dm)
            dbt = dbt + jnp.where(lane == 2 * j, jnp.sum(dm_lo, axis=-1, keepdims=True), 0.0)
            dbt = dbt + jnp.where(lane == 2 * j + 1, jnp.sum(dm_hi, axis=-1, keepdims=True), 0.0)
            dm_lo, dm_hi = dm_lo.astype(BF16), dm_hi.astype(BF16)
            dw_ref[2 * j] += jnp.where(tri, lax.dot_general(dm_lo, v_lo, NT, preferred_element_type=F32), 0.0)
            dw_ref[2 * j + 1] += jnp.where(tri, lax.dot_general(dm_hi, v_hi, NT, preferred_element_type=F32), 0.0)
            dv_ref[:, sl] = (lax.dot_general(w_lo, dm_lo, TN, preferred_element_type=F32)
                             + lax.dot_general(w_hi, dm_hi, TN, preferred_element_type=F32))
        _accumulate(dbt_ref, dbt, first)
        dv = dv_ref[...]
        _accumulate(db_ref, jnp.sum(dv, axis=0, keepdims=True), first)
        _accumulate(dg_ref, jnp.sum(dv * xhat, axis=0, keepdims=True), first)
        dvin_ref[...] = _ln_bwd(dv, r, xhat, g_v).astype(BF16)

    u_spec, v_spec, vec, w_spec, bt_spec, row = _sgu_specs()
    do_spec = pl.BlockSpec((BLK, SGU_CH), lambda i: (i, (Q_END + CONV_CH) // SGU_CH))
    half = jax.ShapeDtypeStruct((t_len, SGU_CH), BF16)
    vshape = jax.ShapeDtypeStruct((1, SGU_CH), F32)
    return _call(body, name="sgu_bwd", grid=(t_len // BLK,),
                 in_specs=[u_spec, v_spec, do_spec, vec, vec, w_spec, bt_spec],
                 out_specs=[row, row, w_spec, pl.BlockSpec((BLK, LANES), lambda i: (0, 0)), vec, vec],
                 out_shape=[half, half, jax.ShapeDtypeStruct((SGU_HEADS, BLK, BLK), F32),
                            jax.ShapeDtypeStruct((BLK, LANES), F32), vshape, vshape],
                 scratch_shapes=[pltpu.VMEM((BLK, SGU_CH), F32)],
                 compiler_params=_params(1))(p, p, dmix, g, b, w, bt)


def _place():
    x, y, c = lax.axis_index("x"), lax.axis_index("y"), lax.axis_index("c")
    chips = [(1 - x, y), (x, 1 - y), (1 - x, 1 - y)]
    return x, y, c, chips


def _hbm_specs(n):
    return [pl.BlockSpec(memory_space=pltpu.HBM)] * n


def _comm_params():
    return pltpu.CompilerParams(has_side_effects=True)


def _remote(src, dst, send_sem, recv_sem, to):
    return pltpu.make_async_remote_copy(src_ref=src, dst_ref=dst, send_sem=send_sem, recv_sem=recv_sem,
                                        device_id=to, device_id_type=MESH_ID)


def _cast_place(w_local, chip):
    n, rows, cols = w_local.shape

    def body(chip_ref, w_ref, o_ref):
        o_ref[...] = w_ref[...].astype(BF16)

    grid_spec = pltpu.PrefetchScalarGridSpec(
        num_scalar_prefetch=1, grid=(n, rows // ROW_TILE),
        in_specs=[pl.BlockSpec((None, ROW_TILE, cols), lambda l, i, ch: (l, i, 0))],
        out_specs=pl.BlockSpec((None, None, ROW_TILE, cols), lambda l, i, ch: (l, ch[0], i, 0)))
    return _call(body, name="cast_place", grid_spec=grid_spec,
                 out_shape=jax.ShapeDtypeStruct((n, N_CHIPS, rows, cols), BF16), compiler_params=_params(2))(chip, w_local)


def _all_gather_weights(placed, shards):
    n_placed, nt = len(placed), len(placed) + len(shards)

    def body(*refs):
        ins, outs = refs[:nt], refs[nt:2 * nt]
        ici_send, ici_recv, d2d_send, d2d_recv, local_sem = refs[2 * nt:]
        x, y, c, chips = _place()
        me = 2 * x + y
        sibling = (x, y, 1 - c)
        local = [pltpu.make_async_copy(ins[t].at[l], outs[t].at[l, me], local_sem.at[2 * (t - n_placed) + l])
                 for t in range(n_placed, nt) for l in range(2)]
        for cp in local:
            cp.start()
        sends = []
        for t in range(nt):
            src = outs[t].at[c, me] if t < n_placed else ins[t].at[c]
            for j, (px, py) in enumerate(chips):
                sends.append(_remote(src, outs[t].at[c, me], ici_send.at[3 * t + j], ici_recv.at[3 * t + j],
                                     (px, py, c)))
        for cp in sends:
            cp.start()
        for t in range(nt):
            for j, (px, py) in enumerate(chips):
                slab = outs[t].at[c, 2 * px + py]
                _remote(slab, slab, ici_send.at[3 * t + j], ici_recv.at[3 * t + j], (px, py, c)).wait_recv()
                fwd = _remote(slab, slab, d2d_send.at[3 * t + j], d2d_recv.at[3 * t + j], sibling)
                fwd.start()
                sends.append(fwd)
        for t in range(nt):
            for j, (px, py) in enumerate(chips):
                slab = outs[t].at[1 - c, 2 * px + py]
                _remote(slab, slab, d2d_send.at[3 * t + j], d2d_recv.at[3 * t + j], sibling).wait_recv()
        for cp in sends:
            cp.wait_send()
        for cp in local:
            cp.wait()

    out_shape = [jax.ShapeDtypeStruct(p.shape, p.dtype) for p in placed]
    out_shape += [jax.ShapeDtypeStruct((2, N_CHIPS) + s.shape[1:], s.dtype) for s in shards]
    sems = [pltpu.SemaphoreType.DMA((3 * nt,))] * 4 + [pltpu.SemaphoreType.DMA((2 * len(shards),))]
    return _call(body, name="all_gather_weights", in_specs=_hbm_specs(nt), out_specs=_hbm_specs(nt),
                 out_shape=out_shape, scratch_shapes=sems, input_output_aliases={t: t for t in range(n_placed)},
                 compiler_params=_comm_params())(*placed, *shards)


def _gather_comm(bufs, pieces):
    n = len(pieces)
    sems = [pltpu.SemaphoreType.DMA((3 * n,))] * 4

    def rows(ref, layer, chip, r0, nr):
        return ref.at[layer, chip, pl.ds(r0, nr)]

    def start(rd, al, fr, sm):
        ici_send, ici_recv, _, _ = sm
        x, y, c, chips = _place()
        for i, (t, layer, r0, nr) in enumerate(pieces):
            @pl.when(c == layer)
            def _():
                own = rows(al[t], layer, 2 * x + y, r0, nr)
                for j, (px, py) in enumerate(chips):
                    _remote(own, own, ici_send.at[3 * i + j], ici_recv.at[3 * i + j], (px, py, c)).start()

    def finish(rd, al, fr, sm):
        ici_send, ici_recv, d2d_send, d2d_recv = sm
        x, y, c, chips = _place()
        sibling = (x, y, 1 - c)
        for i, (t, layer, r0, nr) in enumerate(pieces):
            @pl.when(c == layer)
            def _():
                passed = []
                for j, (px, py) in enumerate(chips):
                    got = rows(al[t], layer, 2 * px + py, r0, nr)
                    _remote(got, got, ici_send.at[3 * i + j], ici_recv.at[3 * i + j], (px, py, c)).wait_recv()
                    fwd = _remote(got, got, d2d_send.at[3 * i + j], d2d_recv.at[3 * i + j], sibling)
                    fwd.start()
                    passed.append(fwd)
                own = rows(al[t], layer, 2 * x + y, r0, nr)
                for j, (px, py) in enumerate(chips):
                    _remote(own, own, ici_send.at[3 * i + j], ici_recv.at[3 * i + j], (px, py, c)).wait_send()
                for fwd in passed:
                    fwd.wait_send()

            @pl.when(c != layer)
            def _():
                for j, (px, py) in enumerate(chips):
                    got = rows(al[t], layer, 2 * px + py, r0, nr)
                    _remote(got, got, d2d_send.at[3 * i + j], d2d_recv.at[3 * i + j], sibling).wait_recv()

    return _Comm([], bufs, [], sems, start, finish)


def _to_owner_comm(stacks, layer):
    nt = len(stacks)
    sems = [pltpu.SemaphoreType.DMA((nt,))] * 2
    fresh = [jax.ShapeDtypeStruct(s.shape[1:], s.dtype) for s in stacks]

    def copies(rd, fr, sm):
        x, y, c, _ = _place()
        return c, [_remote(rd[t].at[layer], fr[t], sm[0].at[t], sm[1].at[t], (x, y, 1 - c)) for t in range(nt)]

    def start(rd, al, fr, sm):
        c, cps = copies(rd, fr, sm)

        @pl.when(c != layer)
        def _():
            for cp in cps:
                cp.start()

    def finish(rd, al, fr, sm):
        c, cps = copies(rd, fr, sm)

        @pl.when(c != layer)
        def _():
            for cp in cps:
                cp.wait_send()

        @pl.when(c == layer)
        def _():
            for cp in cps:
                cp.wait_recv()

    return _Comm(stacks, [], fresh, sems, start, finish)


def _chip_comm(partials):
    nt = len(partials)
    sems = [pltpu.SemaphoreType.DMA((3 * nt,))] * 2
    fresh = [jax.ShapeDtypeStruct((3,) + p.shape[1:], p.dtype) for p, _ in partials]

    def each(rd, fr, sm, act):
        x, y, c, chips = _place()
        for t, (_, layer) in enumerate(partials):
            @pl.when(c == layer)
            def _():
                for j, (px, py) in enumerate(chips):
                    act(_remote(rd[t].at[2 * px + py], fr[t].at[j], sm[0].at[3 * t + j], sm[1].at[3 * t + j],
                                (px, py, c)))

    def start(rd, al, fr, sm):
        each(rd, fr, sm, lambda cp: cp.start())

    def finish(rd, al, fr, sm):
        each(rd, fr, sm, lambda cp: cp.wait())

    return _Comm([p for p, _ in partials], [], fresh, sems, start, finish)


def _from_owner_comm(finals, layer):
    nt = len(finals)
    sems = [pltpu.SemaphoreType.DMA((nt,))] * 2

    def copies(al, sm):
        x, y, c, _ = _place()
        return c, [_remote(al[t].at[layer], al[t].at[layer], sm[0].at[t], sm[1].at[t], (x, y, 1 - c))
                   for t in range(nt)]

    def start(rd, al, fr, sm):
        c, cps = copies(al, sm)

        @pl.when(c == layer)
        def _():
            for cp in cps:
                cp.start()

    def finish(rd, al, fr, sm):
        c, cps = copies(al, sm)

        @pl.when(c == layer)
        def _():
            for cp in cps:
                cp.wait_send()

        @pl.when(c != layer)
        def _():
            for cp in cps:
                cp.wait_recv()

    return _Comm([], finals, [], sems, start, finish)


def _all_reduce_small(buf):
    rows = buf.shape[0]

    def body(x_ref, out_ref, all_ref, send_sems, recv_sems, local_sem):
        x, y, c, chips = _place()
        me, sibling = (x, y, c), (x, y, 1 - c)

        def block(px, py, pc):
            return all_ref.at[pl.ds((4 * px + 2 * py + pc) * rows, rows), :]

        def copy(k, blk, to, src=None):
            return _remote(block(*blk) if src is None else src, block(*blk), send_sems.at[k], recv_sems.at[k], to)

        mine = pltpu.make_async_copy(x_ref, block(*me), local_sem)
        mine.start()
        first = [copy(0, me, sibling, src=x_ref)]
        first += [copy(1 + j, me, (*chip, c), src=x_ref) for j, chip in enumerate(chips)]
        for cp in first:
            cp.start()
        passed = [copy(4 + j, (*chip, c), sibling) for j, chip in enumerate(chips)]
        for j, chip in enumerate(chips):
            copy(1 + j, (*chip, c), me).wait_recv()
            passed[j].start()
        copy(0, sibling, me).wait_recv()
        for j, chip in enumerate(chips):
            copy(4 + j, (*chip, 1 - c), me).wait_recv()
        for cp in first + passed:
            cp.wait_send()
        mine.wait()
        tot = all_ref[0:rows, :]
        for k in range(1, N_DEV):
            tot = tot + all_ref[k * rows:(k + 1) * rows, :]
        out_ref[...] = tot

    vm = pl.BlockSpec(memory_space=pltpu.VMEM)
    return _call(body, name="all_reduce_small", in_specs=[vm], out_specs=vm,
                 out_shape=jax.ShapeDtypeStruct(buf.shape, F32),
                 scratch_shapes=[pltpu.VMEM((N_DEV * rows, LANES), F32), pltpu.SemaphoreType.DMA((7,)),
                                 pltpu.SemaphoreType.DMA((7,)), pltpu.SemaphoreType.DMA],
                 compiler_params=pltpu.CompilerParams(has_side_effects=True,
                                                      vmem_limit_bytes=V7X_VMEM_LIMIT_BYTES))(buf)


ROW_TILE = 128


def _chip_partial(stack, received, layer):
    _, _, rows, cols = stack.shape

    def body(a_ref, b_ref, o_ref):
        o_ref[...] = (a_ref[...].astype(F32) + b_ref[...].astype(F32)).astype(BF16)

    blk = pl.BlockSpec((None, ROW_TILE, cols), lambda s, i: (s, i, 0))
    return _call(body, name="chip_partial", grid=(N_CHIPS, rows // ROW_TILE),
                 in_specs=[pl.BlockSpec((None, None, ROW_TILE, cols), lambda s, i: (layer, s, i, 0)), blk],
                 out_specs=blk, out_shape=jax.ShapeDtypeStruct(received.shape, BF16),
                 compiler_params=_params(2))(stack, received)


def _final_sum(partial, from_chips, layer, chip, finals):
    _, rows, cols = partial.shape

    def body(chip_ref, a_ref, r_ref, *rest):
        o_ref = rest[-1]
        tot = a_ref[...].astype(F32)
        for j in range(3):
            tot = tot + r_ref[j].astype(F32)
        o_ref[...] = tot

    in_specs = [pl.BlockSpec((None, ROW_TILE, cols), lambda i, ch: (ch[0], i, 0)),
                pl.BlockSpec((3, ROW_TILE, cols), lambda i, ch: (0, i, 0))]
    args = [chip, partial, from_chips]
    kw = {}
    if finals is not None:
        in_specs.append(pl.BlockSpec(memory_space=pl.ANY))
        args.append(finals)
        kw["input_output_aliases"] = {3: 0}
    grid_spec = pltpu.PrefetchScalarGridSpec(
        num_scalar_prefetch=1, grid=(rows // ROW_TILE,), in_specs=in_specs,
        out_specs=pl.BlockSpec((None, ROW_TILE, cols), lambda i, ch: (layer, i, 0)))
    return _call(body, name="final_sum", grid_spec=grid_spec, out_shape=jax.ShapeDtypeStruct((2, rows, cols), F32),
                 compiler_params=_params(1), **kw)(*args)


def _adamw(w, g, m, v, comm=None):
    n, rows, cols = w.shape
    tr = _pick(rows, (ROW_TILE, SUBLANES))
    c1 = 1.0 - ADAM_B1 ** ADAM_STEP
    c2 = 1.0 - ADAM_B2 ** ADAM_STEP

    def body(w_ref, g_ref, m_ref, v_ref, d_ref, nm_ref, nv_ref, go_ref):
        g_v = g_ref[...]
        go_ref[...] = g_v
        nm = ADAM_B1 * m_ref[...] + (1.0 - ADAM_B1) * g_v
        nv = ADAM_B2 * v_ref[...] + (1.0 - ADAM_B2) * (g_v * g_v)
        nm_ref[...] = nm
        nv_ref[...] = nv
        d_ref[...] = -ADAM_LR * ((nm / c1) / (jnp.sqrt(nv / c2) + ADAM_EPS) + ADAM_WD * w_ref[...])

    blk = pl.BlockSpec((None, tr, cols), lambda l, i: (l, i, 0))
    shape = jax.ShapeDtypeStruct(w.shape, F32)
    return _hosted_call(body, comm, "adamw", (n, rows // tr), [w, g, m, v], [blk] * 4, [shape] * 4, [blk] * 4)


def _to_heads(a, n_heads):
    t_len = a.shape[0]
    return a.reshape(t_len, n_heads, HEAD_DIM).transpose(1, 0, 2)


def _from_heads(a):
    n_heads, t_len, _ = a.shape
    return a.transpose(1, 0, 2).reshape(t_len, n_heads * HEAD_DIM)


class _Schedule:
    def __init__(self):
        self.sites = {}
        self.open = []

    def add(self, site, make, done=None):
        self.sites.setdefault(site, []).append((make, done))

    def begin(self, site):
        self.open = [(make(), done) for make, done in self.sites.pop(site, [])]
        return _merge_comms([cm for cm, _ in self.open])

    def end(self):
        for cm, done in self.open:
            if done is not None:
                done(cm)
        self.open = []


def _ffn_fwd(x, gain, wg, which, layer, sched):
    w_in_name, w_out_name = f"ffn{which}_w_in", f"ffn{which}_w_out"
    h = _rmsnorm_fwd(x, gain)
    comm = sched.begin(("ffn_in", layer, which))
    gu, act = _mm_ffn_in(h, wg[w_in_name], layer, comm)
    sched.end()
    comm = sched.begin(("ffn_out", layer, which))
    x_new = _mm_out_res("ffn_out", act, wg[w_out_name], layer, x, FFN_RESIDUAL_WEIGHT, comm)[0]
    sched.end()
    return x_new, (x, h, gu, act)


def _ffn_bwd(dx, dxb, saved, gain, wg, which, layer, stacks, sched, weights_first=False):
    w_in_name, w_out_name = f"ffn{which}_w_in", f"ffn{which}_w_out"
    x, h, gu, act = saved
    out = {}

    def dact():
        comm = sched.begin(("ffn_dact", layer, which))
        out["dgu"] = _mm_dact_swiglu(dxb, wg[w_out_name], layer, gu, FFN_RESIDUAL_WEIGHT, comm)
        sched.end()

    def dw_out():
        comm = sched.begin(("ffn_dw_out", layer, which))
        stacks[w_out_name] = _dw_rows("ffn_dw_out", act, dxb, layer, stacks[w_out_name], FFN_RESIDUAL_WEIGHT, comm)
        sched.end()

    def dh():
        comm = sched.begin(("ffn_dh", layer, which))
        out["dh"] = _mm_dh_ffn(out["dgu"], wg[w_in_name], layer, comm)
        sched.end()

    def dw_in():
        comm = sched.begin(("ffn_dw_in", layer, which))
        stacks[w_in_name] = _dw_ffn_in(h, out["dgu"], layer, stacks[w_in_name], comm)
        sched.end()

    for step in ((dw_out, dact, dw_in, dh) if weights_first else (dact, dw_out, dh, dw_in)):
        step()
    return _rmsnorm_bwd(out["dh"], x, gain, dx)


def _mix_fwd(x, gain, wg, layer, small, tables, sched):
    h = _rmsnorm_fwd(x, gain)
    comm = sched.begin(("mix_in", layer))
    p = _mm_proj(h, wg["w_in"], layer, comm)[0]
    sched.end()
    qkv = _rope_fwd(p, tables)
    q = _to_heads(qkv[:, :Q_END], N_Q_HEADS)
    k = _to_heads(qkv[:, Q_END:K_END], N_KV_HEADS)
    v = _to_heads(qkv[:, K_END:V_END], N_KV_HEADS)
    comm = sched.begin(("attn", layer))
    attn = _from_heads(_attn_fwd(q, k, v, small["snk"], comm))
    sched.end()
    y = _conv_fwd(p, small["conv_w"], small["conv_b"])
    conv = _conv_post_fwd(y, small["conv_ln_g"], small["conv_ln_b"])
    sgu = _sgu_fwd(p, small["sgu_ln_g"], small["sgu_ln_b"], small["sgu_w"], small["sgu_bt"])
    mix = jnp.concatenate([attn, conv, sgu], axis=1)
    x_new = _mm_out_res("mix_out", mix, wg["w_out"], layer, x, 1.0)[0]
    return x_new, (x, h, p, q, k, v, y, mix)


def _mix_bwd(dx, dxb, saved, gain, wg, small, tables, layer, stacks, sched):
    x, h, p, q, k, v, y, mix = saved
    comm = sched.begin(("mix_dout", layer))
    dmix = _mm_dmix(dxb, wg["w_out"], layer, comm)
    sched.end()
    stacks["w_out"] = _dw_rows("mix_dw_out", mix, dxb, layer, stacks["w_out"], 1.0)
    do = _to_heads(dmix[:, :Q_END], N_Q_HEADS)
    comm = sched.begin(("attn_bwd", layer))
    dq, dkp, dkc, dvp, dvc, dsnk = _attn_bwd(q, k, v, small["snk"], do, comm)
    sched.end()
    dy, d_ln_g, d_ln_b, d_conv_b = _conv_post_bwd(dmix, y, small["conv_ln_g"], small["conv_ln_b"])
    dalin, dagate, d_conv_w = _conv_bwd(p, dy, small["conv_w"])
    du, dvin, d_sgu_w, d_sgu_bt, d_sgu_g, d_sgu_b = _sgu_bwd(p, dmix, small["sgu_ln_g"], small["sgu_ln_b"],
                                                           small["sgu_w"], small["sgu_bt"])
    dp = _assemble_dp(_from_heads(dq), _from_heads(dkc), _from_heads(dkp), _from_heads(dvc), _from_heads(dvp),
                      tables, dalin, dagate, du, dvin)
    stacks["w_in"] = _dw_mix_in(h, dp, layer, stacks["w_in"])
    comm = sched.begin(("mix_dh", layer))
    dh = _mm_dh_mix(dp, wg["w_in"], layer, comm)
    sched.end()
    dx_in, dxb_in, dgain = _rmsnorm_bwd(dh, x, gain, dx)
    grads = {
        "norm_mix": dgain[0], "conv_dw_w": d_conv_w[:CONV_WIDTH], "conv_dw_b": d_conv_b[0],
        "conv_ln_g": d_ln_g[0], "conv_ln_b": d_ln_b[0], "sgu_ln_g": d_sgu_g[0], "sgu_ln_b": d_sgu_b[0],
        "sgu_w": d_sgu_w, "sgu_b": d_sgu_bt[:, :SGU_HEADS].T, "attn_sinks": dsnk[:, :GQ, 0].reshape(N_Q_HEADS),
    }
    return dx_in, dxb_in, grads


BIG = ("ffn1_w_in", "ffn1_w_out", "w_in", "w_out", "ffn2_w_in", "ffn2_w_out")
SMALL = ("norm_ffn1", "norm_mix", "conv_dw_w", "conv_dw_b", "conv_ln_g", "conv_ln_b", "sgu_ln_g", "sgu_ln_b",
         "sgu_w", "sgu_b", "attn_sinks", "norm_ffn2", "final_norm")
WEIGHTS = ("norm_ffn1", "ffn1_w_in", "ffn1_w_out", "norm_mix", "w_in", "conv_dw_w", "conv_dw_b", "conv_ln_g",
           "conv_ln_b", "sgu_ln_g", "sgu_ln_b", "sgu_w", "sgu_b", "attn_sinks", "w_out", "norm_ffn2", "ffn2_w_in",
           "ffn2_w_out", "final_norm")
PACK_ROWS = SUBLANES * LANES

FIRST_GATHER = [("ffn1_w_in", 0, None)]
FORWARD_PLAN = {
    ("ffn_in", 0, 1): [("ffn1_w_out", 0, None), ("w_in", 0, None)],
    ("ffn_out", 0, 1): [("ffn2_w_in", 0, 0)],
    ("mix_in", 0): [("w_out", 0, None)],
    ("attn", 0): [("ffn2_w_in", 0, 1)],
    ("ffn_in", 0, 2): [("ffn2_w_out", 0, None), ("ffn1_w_in", 1, 0)],
    ("ffn_out", 0, 2): [("ffn1_w_in", 1, 1)],
    ("ffn_in", 1, 1): [("ffn1_w_out", 1, None), ("w_in", 1, None)],
    ("ffn_out", 1, 1): [("ffn2_w_in", 1, 0)],
    ("mix_in", 1): [("w_out", 1, None)],
    ("attn", 1): [("ffn2_w_in", 1, 1)],
    ("ffn_in", 1, 2): [("ffn2_w_out", 1, None)],
}
SUBLAYER_WEIGHTS = {"ffn1": ["ffn1_w_out", "ffn1_w_in"], "ffn2": ["ffn2_w_out", "ffn2_w_in"], "mix": ["w_out", "w_in"]}


def _pack(arrays):
    flat = jnp.concatenate([a.reshape(-1).astype(F32) for a in arrays])
    pad = (-flat.shape[0]) % PACK_ROWS
    return jnp.pad(flat, (0, pad)).reshape(-1, LANES)


def _unpack(buf, shapes):
    flat = buf.reshape(-1)
    out, off = [], 0
    for s in shapes:
        n = 1
        for d in s:
            n *= d
        out.append(flat[off:off + n].reshape(s))
        off += n
    return out


def kernel(x, positions, norm_ffn1, ffn1_w_in, ffn1_w_out, norm_mix, w_in, conv_dw_w, conv_dw_b, conv_ln_g, conv_ln_b, sgu_ln_g, sgu_ln_b, sgu_w, sgu_b, attn_sinks, w_out, norm_ffn2, ffn2_w_in, ffn2_w_out, final_norm, loss_target, m_norm_ffn1, m_ffn1_w_in, m_ffn1_w_out, m_norm_mix, m_w_in, m_conv_dw_w, m_conv_dw_b, m_conv_ln_g, m_conv_ln_b, m_sgu_ln_g, m_sgu_ln_b, m_sgu_w, m_sgu_b, m_attn_sinks, m_w_out, m_norm_ffn2, m_ffn2_w_in, m_ffn2_w_out, m_final_norm, v_norm_ffn1, v_ffn1_w_in, v_ffn1_w_out, v_norm_mix, v_w_in, v_conv_dw_w, v_conv_dw_b, v_conv_ln_g, v_conv_ln_b, v_sgu_ln_g, v_sgu_ln_b, v_sgu_w, v_sgu_b, v_attn_sinks, v_w_out, v_norm_ffn2, v_ffn2_w_in, v_ffn2_w_out, v_final_norm):
    w = dict(norm_ffn1=norm_ffn1, ffn1_w_in=ffn1_w_in, ffn1_w_out=ffn1_w_out, norm_mix=norm_mix, w_in=w_in,
             conv_dw_w=conv_dw_w, conv_dw_b=conv_dw_b, conv_ln_g=conv_ln_g, conv_ln_b=conv_ln_b, sgu_ln_g=sgu_ln_g,
             sgu_ln_b=sgu_ln_b, sgu_w=sgu_w, sgu_b=sgu_b, attn_sinks=attn_sinks, w_out=w_out, norm_ffn2=norm_ffn2,
             ffn2_w_in=ffn2_w_in, ffn2_w_out=ffn2_w_out, final_norm=final_norm)
    m = dict(norm_ffn1=m_norm_ffn1, ffn1_w_in=m_ffn1_w_in, ffn1_w_out=m_ffn1_w_out, norm_mix=m_norm_mix, w_in=m_w_in,
             conv_dw_w=m_conv_dw_w, conv_dw_b=m_conv_dw_b, conv_ln_g=m_conv_ln_g, conv_ln_b=m_conv_ln_b,
             sgu_ln_g=m_sgu_ln_g, sgu_ln_b=m_sgu_ln_b, sgu_w=m_sgu_w, sgu_b=m_sgu_b, attn_sinks=m_attn_sinks,
             w_out=m_w_out, norm_ffn2=m_norm_ffn2, ffn2_w_in=m_ffn2_w_in, ffn2_w_out=m_ffn2_w_out,
             final_norm=m_final_norm)
    v = dict(norm_ffn1=v_norm_ffn1, ffn1_w_in=v_ffn1_w_in, ffn1_w_out=v_ffn1_w_out, norm_mix=v_norm_mix, w_in=v_w_in,
             conv_dw_w=v_conv_dw_w, conv_dw_b=v_conv_dw_b, conv_ln_g=v_conv_ln_g, conv_ln_b=v_conv_ln_b,
             sgu_ln_g=v_sgu_ln_g, sgu_ln_b=v_sgu_ln_b, sgu_w=v_sgu_w, sgu_b=v_sgu_b, attn_sinks=v_attn_sinks,
             w_out=v_w_out, norm_ffn2=v_norm_ffn2, ffn2_w_in=v_ffn2_w_in, ffn2_w_out=v_ffn2_w_out,
             final_norm=v_final_norm)
    depth = norm_ffn1.shape[0]
    assert depth == 2 and x.shape[0] == 1
    xc = lax.axis_index("x")
    yc = lax.axis_index("y")
    cc = lax.axis_index("c")
    chip = 2 * xc + yc

    chip_arr = chip.reshape(1).astype(jnp.int32)
    wg = {n: _cast_place(w[n], chip_arr) for n in BIG}
    sched = _Schedule()

    def gather(pieces):
        names = sorted({n for n, _, _ in pieces})
        half = w["ffn1_w_in"].shape[1] // 2

        def make():
            rows = lambda n, part: (0, wg[n].shape[2]) if part is None else (part * half, half)
            return _gather_comm([wg[n] for n in names], [(names.index(n), l, *rows(n, part)) for n, l, part in pieces])

        return make, lambda cm: wg.update(zip(names, cm.aliased_out))

    make, done = gather(FIRST_GATHER)
    first = make()
    _standalone("gather_first", first)
    done(first)
    for site, pieces in FORWARD_PLAN.items():
        sched.add(site, *gather(pieces))
    conv_w_full = _all_gather_weights([], [conv_dw_w])[0].transpose(0, 2, 1, 3).reshape(depth, CONV_WIDTH, CONV_CH)
    conv_w_full = jnp.pad(conv_w_full, ((0, 0), (0, CONV_PAD - CONV_WIDTH), (0, 0)))

    tables = _rope_tables(positions)
    small = []
    for l in range(depth):
        small.append(dict(
            snk=jnp.broadcast_to(attn_sinks[l].reshape(N_KV_HEADS, GQ, 1, 1), (N_KV_HEADS, GQ, BLK, 1)).reshape(
                N_KV_HEADS, GQ * BLK, 1),
            conv_w=conv_w_full[l], conv_b=conv_dw_b[l][None], conv_ln_g=conv_ln_g[l][None],
            conv_ln_b=conv_ln_b[l][None], sgu_ln_g=sgu_ln_g[l][None], sgu_ln_b=sgu_ln_b[l][None], sgu_w=sgu_w[l],
            sgu_bt=sgu_b[l].T))

    xs = x[0]
    saved = []
    for l in range(depth):
        xs, s1 = _ffn_fwd(xs, norm_ffn1[l][None], wg, 1, l, sched)
        xs, s2 = _mix_fwd(xs, norm_mix[l][None], wg, l, small[l], tables, sched)
        xs, s3 = _ffn_fwd(xs, norm_ffn2[l][None], wg, 2, l, sched)
        saved.append((s1, s2, s3))
    dx, dxb, d_final, loss_part = _loss_head(xs, final_norm[None], loss_target[0])

    stacks = {n: None for n in BIG}
    partials, from_chips = {}, {}
    finals = {n: None for n in BIG}

    def to_owner(layer, names):
        def done(cm):
            for n, received in zip(names, cm.fresh_out):
                partials[n, layer] = _chip_partial(stacks[n], received, layer)

        return lambda: _to_owner_comm([stacks[n] for n in names], layer), done

    def between_chips(layer, names, then_sum=()):
        def done(cm):
            from_chips.update({(n, layer): r for n, r in zip(names, cm.fresh_out)})
            for n in then_sum:
                finals[n] = _final_sum(partials[n, layer], from_chips[n, layer], layer, chip_arr, finals[n])

        return lambda: _chip_comm([(partials[n, layer], layer) for n in names]), done

    def from_owner(layer, names):
        return (lambda: _from_owner_comm([finals[n] for n in names], layer),
                lambda cm: finals.update(zip(names, cm.aliased_out)))

    order = [(l, kind) for l in reversed(range(depth)) for kind in ("ffn2", "mix", "ffn1")]
    for (layer, kind), (nxt_layer, nxt_kind) in zip(order[:-1], order[1:]):
        names = SUBLAYER_WEIGHTS[kind]
        if (nxt_layer, nxt_kind) == order[-1]:
            first_w, second_w = SUBLAYER_WEIGHTS[nxt_kind]
            which = int(nxt_kind[-1])
            sched.add(("ffn_dw_out", nxt_layer, which), *to_owner(layer, names))
            sched.add(("ffn_dact", nxt_layer, which), *between_chips(layer, names, names))
            sched.add(("ffn_dact", nxt_layer, which), *to_owner(nxt_layer, [first_w]))
            sched.add(("ffn_dw_in", nxt_layer, which), *from_owner(layer, names))
            sched.add(("ffn_dw_in", nxt_layer, which), *between_chips(nxt_layer, [first_w], [first_w]))
            sched.add(("ffn_dh", nxt_layer, which), *to_owner(nxt_layer, [second_w]))
            sched.add(("ffn_dh", nxt_layer, which), *from_owner(nxt_layer, [first_w]))
        elif nxt_kind == "mix":
            sched.add(("mix_dout", nxt_layer), *to_owner(layer, names))
            sched.add(("attn_bwd", nxt_layer), *between_chips(layer, names, names))
            sched.add(("mix_dh", nxt_layer), *from_owner(layer, names))
        else:
            which = int(nxt_kind[-1])
            sched.add(("ffn_dact", nxt_layer, which), *to_owner(layer, names))
            if kind == "mix":
                sched.add(("ffn_dw_out", nxt_layer, which), *between_chips(layer, names, names))
            else:
                sched.add(("ffn_dw_out", nxt_layer, which), *between_chips(layer, names[:1]))
                sched.add(("ffn_dh", nxt_layer, which), *between_chips(layer, names[1:], names))
            sched.add(("ffn_dw_in", nxt_layer, which), *from_owner(layer, names))

    small_grads = [None] * depth
    for l in reversed(range(depth)):
        s1, s2, s3 = saved[l]
        dx, dxb, dg2 = _ffn_bwd(dx, dxb, s3, norm_ffn2[l][None], wg, 2, l, stacks, sched)
        dx, dxb, gm = _mix_bwd(dx, dxb, s2, norm_mix[l][None], wg, small[l], tables, l, stacks, sched)
        dx, dxb, dg1 = _ffn_bwd(dx, dxb, s1, norm_ffn1[l][None], wg, 1, l, stacks, sched, weights_first=l == 0)
        gm["norm_ffn1"] = dg1[0]
        gm["norm_ffn2"] = dg2[0]
        small_grads[l] = gm
    grad_x = dx[None]
    assert not sched.sites, sched.sites

    last_layer, last_kind = order[-1]
    names = SUBLAYER_WEIGHTS[last_kind][1:]
    for name, (make, done) in (("between_chips", between_chips(last_layer, names, names)),
                               ("from_owner", from_owner(last_layer, names))):
        cm = make()
        _standalone(name, cm)
        done(cm)
    big_grads = dict(finals)

    per_layer = [n for n in SMALL if n != "final_norm"]
    small_local = [jnp.stack([small_grads[l][n] for l in range(depth)]) for n in per_layer]
    small_local += [d_final[0], loss_part[0, :1]]
    small_shapes = [a.shape for a in small_local]
    summed = _unpack(_all_reduce_small(_pack(small_local)), small_shapes)
    loss = summed[-1][0]
    sg = dict(zip(per_layer + ["final_norm"], summed[:-1]))
    sg["conv_dw_w"] = lax.dynamic_slice_in_dim(sg["conv_dw_w"], chip * LANES, LANES, axis=2)

    delta, new_m, new_v = {}, {}, {}
    for n in BIG:
        delta[n], new_m[n], new_v[n], big_grads[n] = _adamw(w[n], big_grads[n], m[n], v[n])
    shapes = [w[n].shape for n in SMALL]
    packed = [_pack([d[n] for n in SMALL])[None] for d in (w, sg, m, v)]
    outs = _adamw(*packed)
    for d, buf in zip((delta, new_m, new_v), outs[:3]):
        d.update(zip(SMALL, _unpack(buf[0], shapes)))
    grads = {**big_grads, **sg}
    return (loss, grad_x, *[grads[n] for n in WEIGHTS], *[delta[n] for n in WEIGHTS],
            *[new_m[n] for n in WEIGHTS], *[new_v[n] for n in WEIGHTS])
```

```python
import functools

import jax
import jax.numpy as jnp
from jax import lax
from jax.experimental import pallas as pl
from jax.experimental.pallas import tpu as pltpu

F32 = jnp.float32
BF16 = jnp.bfloat16
MESH_ID = pl.DeviceIdType.MESH

V7X_VMEM_LIMIT_BYTES = 56 * 2**20
LANES = 128
SUBLANES = 8

HEAD_DIM = 64
N_Q_HEADS = 16
N_KV_HEADS = 4
GQ = N_Q_HEADS // N_KV_HEADS
BLK = 128
ROT_HALF = 8
ROPE_THETA = 500000.0
CONV_WIDTH = 31
CONV_PAD = 32
CONV_CH = 512
SGU_CH = 512
SGU_HEADS = 8
Q_END = N_Q_HEADS * HEAD_DIM
K_END = Q_END + N_KV_HEADS * HEAD_DIM
V_END = K_END + N_KV_HEADS * HEAD_DIM
CONV_END = V_END + 2 * CONV_CH
IN_COLS = CONV_END + 2 * SGU_CH
NORM_EPS = 1e-5
FFN_RESIDUAL_WEIGHT = 0.5
N_CHIPS = 4
N_DEV = 8

ADAM_LR = 0.001
ADAM_B1 = 0.9
ADAM_B2 = 0.999
ADAM_EPS = 1e-08
ADAM_WD = 0.01
ADAM_STEP = 10

NN = (((1,), (0,)), ((), ()))
NT = (((1,), (1,)), ((), ()))
TN = (((0,), (0,)), ((), ()))


def _pick(n, cands):
    for c in cands:
        if n % c == 0:
            return c
    raise ValueError(f"no tile of {cands} divides {n}")


def _params(n_axes):
    return pltpu.CompilerParams(dimension_semantics=("arbitrary",) * n_axes, vmem_limit_bytes=V7X_VMEM_LIMIT_BYTES)


def _call(body, **kw):
    return pl.pallas_call(body, **kw)


def _sigmoid(x):
    return 1.0 / (1.0 + jnp.exp(-x))


class _Comm:
    def __init__(self, reads, aliased, fresh, sems, start, finish):
        self.reads, self.aliased, self.fresh, self.sems = list(reads), list(aliased), list(fresh), list(sems)
        self.start, self.finish = start, finish
        self.aliased_out, self.fresh_out = None, None


def _merge_comms(comms):
    comms = [cm for cm in comms if cm is not None]
    if not comms:
        return None
    if len(comms) == 1:
        return comms[0]

    def split(refs, counts):
        out, off = [], 0
        for n in counts:
            out.append(refs[off:off + n])
            off += n
        return out

    def run(which):
        def f(rd, al, fr, sm):
            parts = zip(split(rd, [len(cm.reads) for cm in comms]), split(al, [len(cm.aliased) for cm in comms]),
                        split(fr, [len(cm.fresh) for cm in comms]), split(sm, [len(cm.sems) for cm in comms]))
            for cm, (r, a, f_, s) in zip(comms, parts):
                getattr(cm, which)(r, a, f_, s)
        return f

    merged = _Comm(sum((cm.reads for cm in comms), []), sum((cm.aliased for cm in comms), []),
                   sum((cm.fresh for cm in comms), []), sum((cm.sems for cm in comms), []), run("start"), run("finish"))
    merged.parts = comms
    return merged


def _hosted_call(body, comm, name, grid, inputs, in_specs, out_shape, out_specs, scratch_shapes=(), aliases=None):
    n_in, n_out, n_scr = len(inputs), len(out_shape), len(scratch_shapes)
    aliases = dict(aliases or {})
    if comm is None:
        return _call(body, name=name, grid=grid, in_specs=list(in_specs), out_specs=list(out_specs),
                     out_shape=list(out_shape), scratch_shapes=list(scratch_shapes), input_output_aliases=aliases,
                     compiler_params=_params(len(grid)))(*inputs)
    nr, na, nf = len(comm.reads), len(comm.aliased), len(comm.fresh)

    def full(*refs):
        ins = refs[:n_in]
        rd = refs[n_in:n_in + nr]
        pos = n_in + nr + na
        outs = refs[pos:pos + n_out]
        al = refs[pos + n_out:pos + n_out + na]
        fr = refs[pos + n_out + na:pos + n_out + na + nf]
        pos = pos + n_out + na + nf
        scr = refs[pos:pos + n_scr]
        sems = refs[pos + n_scr:]
        first, last = None, None
        for axis, size in enumerate(grid):
            f, l = pl.program_id(axis) == 0, pl.program_id(axis) == size - 1
            first = f if first is None else jnp.logical_and(first, f)
            last = l if last is None else jnp.logical_and(last, l)

        @pl.when(first)
        def _():
            comm.start(rd, al, fr, sems)

        body(*ins, *outs, *scr)

        @pl.when(last)
        def _():
            comm.finish(rd, al, fr, sems)

    hbm = pl.BlockSpec(memory_space=pltpu.HBM)
    for i in range(na):
        aliases[n_in + nr + i] = n_out + i
    struct = [jax.ShapeDtypeStruct(a.shape, a.dtype) for a in comm.aliased]
    res = _call(full, name=name, grid=grid, in_specs=list(in_specs) + [hbm] * (nr + na),
                out_specs=list(out_specs) + [hbm] * (na + nf), out_shape=list(out_shape) + struct + comm.fresh,
                scratch_shapes=list(scratch_shapes) + comm.sems, input_output_aliases=aliases,
                compiler_params=pltpu.CompilerParams(dimension_semantics=("arbitrary",) * len(grid),
                                                     vmem_limit_bytes=V7X_VMEM_LIMIT_BYTES, has_side_effects=True),
                )(*inputs, *comm.reads, *comm.aliased)
    _deliver(comm, res[n_out:n_out + na], res[n_out + na:])
    return res[:n_out]


def _deliver(comm, aliased_out, fresh_out):
    comm.aliased_out, comm.fresh_out = list(aliased_out), list(fresh_out)
    off_a = off_f = 0
    for part in getattr(comm, "parts", []):
        _deliver(part, aliased_out[off_a:off_a + len(part.aliased)], fresh_out[off_f:off_f + len(part.fresh)])
        off_a += len(part.aliased)
        off_f += len(part.fresh)


def _standalone(name, comm):
    def body(*refs):
        nr, na, nf = len(comm.reads), len(comm.aliased), len(comm.fresh)
        rd, al, fr, sems = refs[:nr], refs[nr + na:nr + 2 * na], refs[nr + 2 * na:nr + 2 * na + nf], refs[nr + 2 * na + nf:]
        comm.start(rd, al, fr, sems)
        comm.finish(rd, al, fr, sems)

    nr, na, nf = len(comm.reads), len(comm.aliased), len(comm.fresh)
    struct = [jax.ShapeDtypeStruct(a.shape, a.dtype) for a in comm.aliased]
    res = _call(body, name=name, in_specs=_hbm_specs(nr + na), out_specs=_hbm_specs(na + nf),
                out_shape=struct + comm.fresh, scratch_shapes=comm.sems,
                input_output_aliases={nr + i: i for i in range(na)},
                compiler_params=_comm_params())(*comm.reads, *comm.aliased)
    _deliver(comm, res[:na], res[na:])


def _matmul(name, grid, a_ops, b_ops, terms, dims, out_shape, out_specs, epilogue, extra_ops=(), nk=1,
            acc_shapes=(), alias=None, comm=None):
    na, nb, ne, no = len(a_ops), len(b_ops), len(extra_ops), len(out_shape)

    def body(*refs):
        a = refs[:na]
        b = refs[na:na + nb]
        e = refs[na + nb:na + nb + ne]
        first_out = na + nb + ne + (1 if alias is not None else 0)
        o = refs[first_out:first_out + no]
        accs = refs[first_out + no:]

        def partial(t):
            tot = None
            for ai, bi in t:
                d = lax.dot_general(a[ai][...], b[bi][...], dims, preferred_element_type=F32)
                tot = d if tot is None else tot + d
            return tot

        if nk == 1:
            epilogue([partial(t) for t in terms], e, o)
        else:
            k = pl.program_id(len(grid) - 1)

            @pl.when(k == 0)
            def _():
                for acc in accs:
                    acc[...] = jnp.zeros(acc.shape, F32)

            for acc, t in zip(accs, terms):
                acc[...] += partial(t)

            @pl.when(k == nk - 1)
            def _():
                epilogue([acc[...] for acc in accs], e, o)

    ops = list(a_ops) + list(b_ops) + list(extra_ops)
    arrays = [x for x, _ in ops]
    in_specs = [s for _, s in ops]
    aliases = {}
    if alias is not None:
        arrays.append(alias[0])
        in_specs.append(pl.BlockSpec(memory_space=pl.ANY))
        aliases[len(arrays) - 1] = alias[1]
    scratch = [pltpu.VMEM(s, F32) for s in acc_shapes] if nk > 1 else []
    return _hosted_call(body, comm, name, grid, arrays, in_specs, out_shape, out_specs, scratch, aliases)


def _mm_ffn_in(h, w_g, layer, comm=None):
    t_len, d = h.shape
    fs = w_g.shape[3]
    f = 2 * fs
    tm = _pick(t_len, (2048, 1024, 512))
    tn = _pick(fs, (256, 128))
    nj = fs // tn

    def epilogue(accs, e, o):
        g, u = accs
        o[0][0] = g.astype(BF16)
        o[0][1] = u.astype(BF16)
        o[1][...] = (g * _sigmoid(g) * u).astype(BF16)

    return _matmul(
        "ffn_in", (t_len // tm, 2, nj),
        [(h, pl.BlockSpec((tm, d), lambda i, s, j: (i, 0)))],
        [(w_g, pl.BlockSpec((None, None, d, tn), lambda i, s, j: (layer, s, 0, j))),
         (w_g, pl.BlockSpec((None, None, d, tn), lambda i, s, j: (layer, s + 2, 0, j)))],
        [[(0, 0)], [(0, 1)]], NN,
        [jax.ShapeDtypeStruct((2, t_len, f), BF16), jax.ShapeDtypeStruct((t_len, f), BF16)],
        [pl.BlockSpec((2, tm, tn), lambda i, s, j: (0, i, s * nj + j)),
         pl.BlockSpec((tm, tn), lambda i, s, j: (i, s * nj + j))],
        epilogue, comm=comm)


def _mm_out_res(name, a, w_g, layer, x, scale, comm=None):
    t_len = a.shape[0]
    ks, n = w_g.shape[2], w_g.shape[3]
    tm = _pick(t_len, (1024, 512))
    tn = _pick(n, (1024,))
    tk = _pick(ks, (1408, 512, 256))
    nks = ks // tk

    def epilogue(accs, e, o):
        o[0][...] = e[0][...] + scale * accs[0]

    return _matmul(
        name, (t_len // tm, n // tn, N_CHIPS * nks),
        [(a, pl.BlockSpec((tm, tk), lambda i, j, k: (i, k)))],
        [(w_g, pl.BlockSpec((None, None, tk, tn), lambda i, j, k: (layer, k // nks, k % nks, j)))],
        [[(0, 0)]], NN,
        [jax.ShapeDtypeStruct((t_len, n), F32)],
        [pl.BlockSpec((tm, tn), lambda i, j, k: (i, j))],
        epilogue, extra_ops=[(x, pl.BlockSpec((tm, tn), lambda i, j, k: (i, j)))],
        nk=N_CHIPS * nks, acc_shapes=[(tm, tn)], comm=comm)


def _mm_proj(h, w_g, layer, comm=None):
    t_len, d = h.shape
    cs = w_g.shape[3]
    tm = _pick(t_len, (1024, 512))

    def epilogue(accs, e, o):
        o[0][...] = accs[0]

    return _matmul(
        "mix_in", (t_len // tm, N_CHIPS),
        [(h, pl.BlockSpec((tm, d), lambda i, s: (i, 0)))],
        [(w_g, pl.BlockSpec((None, None, d, cs), lambda i, s: (layer, s, 0, 0)))],
        [[(0, 0)]], NN,
        [jax.ShapeDtypeStruct((t_len, N_CHIPS * cs), F32)],
        [pl.BlockSpec((tm, cs), lambda i, s: (i, s))],
        epilogue, comm=comm)


def _mm_dact_swiglu(dxb, w_g, layer, gu, scale, comm=None):
    t_len, d = dxb.shape
    rs = w_g.shape[2]
    tm = _pick(t_len, (512,))
    tn = _pick(rs, (1408, 256, 128))
    nj = rs // tn

    def epilogue(accs, e, o):
        dact = scale * accs[0]
        g = e[0][0].astype(F32)
        u = e[0][1].astype(F32)
        sig = _sigmoid(g)
        o[0][0] = (dact * u * (sig * (1.0 + g * (1.0 - sig)))).astype(BF16)
        o[0][1] = (dact * (g * sig)).astype(BF16)

    gu_spec = pl.BlockSpec((2, tm, tn), lambda i, s, j: (0, i, s * nj + j))
    return _matmul(
        "ffn_dact", (t_len // tm, N_CHIPS, nj),
        [(dxb, pl.BlockSpec((tm, d), lambda i, s, j: (i, 0)))],
        [(w_g, pl.BlockSpec((None, None, tn, d), lambda i, s, j: (layer, s, j, 0)))],
        [[(0, 0)]], NT,
        [jax.ShapeDtypeStruct(gu.shape, BF16)], [gu_spec],
        epilogue, extra_ops=[(gu, gu_spec)], comm=comm)[0]


def _mm_dh_ffn(dgu, w_g, layer, comm=None):
    t_len = dgu.shape[1]
    d, fs = w_g.shape[2], w_g.shape[3]
    tm = _pick(t_len, (1024, 512))
    tk = _pick(fs, (256, 128))
    nks = fs // tk
    nk = 2 * nks

    def epilogue(accs, e, o):
        o[0][...] = accs[0]

    return _matmul(
        "ffn_dh", (t_len // tm, nk),
        [(dgu, pl.BlockSpec((None, tm, tk), lambda i, k: (0, i, k))),
         (dgu, pl.BlockSpec((None, tm, tk), lambda i, k: (1, i, k)))],
        [(w_g, pl.BlockSpec((None, None, d, tk), lambda i, k: (layer, k // nks, 0, k % nks))),
         (w_g, pl.BlockSpec((None, None, d, tk), lambda i, k: (layer, k // nks + 2, 0, k % nks)))],
        [[(0, 0), (1, 1)]], NT,
        [jax.ShapeDtypeStruct((t_len, d), F32)],
        [pl.BlockSpec((tm, d), lambda i, k: (i, 0))],
        epilogue, nk=nk, acc_shapes=[(tm, d)], comm=comm)[0]


def _mm_dw(name, a, a_spec_of, b, b_spec_of, layer, stack, rows, cols, tn, scale, comm=None):
    t_len = a.shape[0]
    tt = _pick(t_len, (1024, 512))
    nj = cols // tn

    def epilogue(accs, e, o):
        o[0][...] = (scale * accs[0]).astype(BF16)

    shape = jax.ShapeDtypeStruct((2, N_CHIPS, rows, cols), BF16)
    return _matmul(
        name, (N_CHIPS, nj, t_len // tt),
        [(a, a_spec_of(tt))], [(b, b_spec_of(tt, tn, nj))],
        [[(0, 0)]], TN, [shape],
        [pl.BlockSpec((None, None, rows, tn), lambda s, j, t: (layer, s, 0, j))],
        epilogue, nk=t_len // tt, acc_shapes=[(rows, tn)],
        alias=None if stack is None else (stack, 0), comm=comm)[0]


def _dw_ffn_in(h, dgu, layer, stack, comm=None):
    d = h.shape[1]
    fs = dgu.shape[2] // 2
    tn = _pick(fs, (1408, 256))
    return _mm_dw(
        "ffn_dw_in", h, lambda tt: pl.BlockSpec((tt, d), lambda s, j, t: (t, 0)),
        dgu, lambda tt, tn_, nj: pl.BlockSpec((None, tt, tn_), lambda s, j, t: (s // 2, t, (s % 2) * nj + j)),
        layer, stack, d, fs, tn, 1.0, comm)


def _dw_rows(name, a, dxb, layer, stack, scale, comm=None):
    rs = a.shape[1] // N_CHIPS
    d = dxb.shape[1]
    tn = _pick(d, (1024,))
    return _mm_dw(
        name, a, lambda tt: pl.BlockSpec((tt, rs), lambda s, j, t: (t, s)),
        dxb, lambda tt, tn_, nj: pl.BlockSpec((tt, tn_), lambda s, j, t: (t, j)),
        layer, stack, rs, d, tn, scale, comm)


def _dw_mix_in(h, dp, layer, stack):
    d = h.shape[1]
    cs = dp.shape[1] // N_CHIPS
    return _mm_dw(
        "mix_dw_in", h, lambda tt: pl.BlockSpec((tt, d), lambda s, j, t: (t, 0)),
        dp, lambda tt, tn_, nj: pl.BlockSpec((tt, tn_), lambda s, j, t: (t, s)),
        layer, stack, d, cs, cs, 1.0)


def _mm_dmix(dxb, w_g, layer, comm=None):
    t_len, d = dxb.shape
    rs = w_g.shape[2]
    tm = _pick(t_len, (1024, 512))

    def epilogue(accs, e, o):
        o[0][...] = accs[0]

    return _matmul(
        "mix_dout", (t_len // tm, N_CHIPS),
        [(dxb, pl.BlockSpec((tm, d), lambda i, s: (i, 0)))],
        [(w_g, pl.BlockSpec((None, None, rs, d), lambda i, s: (layer, s, 0, 0)))],
        [[(0, 0)]], NT,
        [jax.ShapeDtypeStruct((t_len, N_CHIPS * rs), F32)],
        [pl.BlockSpec((tm, rs), lambda i, s: (i, s))],
        epilogue, comm=comm)[0]


def _mm_dh_mix(dp, w_g, layer, comm=None):
    t_len = dp.shape[0]
    d, cs = w_g.shape[2], w_g.shape[3]
    tm = _pick(t_len, (1024, 512))

    def epilogue(accs, e, o):
        o[0][...] = accs[0]

    return _matmul(
        "mix_dh", (t_len // tm, N_CHIPS),
        [(dp, pl.BlockSpec((tm, cs), lambda i, k: (i, k)))],
        [(w_g, pl.BlockSpec((None, None, d, cs), lambda i, k: (layer, k, 0, 0)))],
        [[(0, 0)]], NT,
        [jax.ShapeDtypeStruct((t_len, d), F32)],
        [pl.BlockSpec((tm, d), lambda i, k: (i, 0))],
        epilogue, nk=N_CHIPS, acc_shapes=[(tm, d)], comm=comm)[0]


def _rms_stats(x):
    r = lax.rsqrt(jnp.mean(x * x, axis=-1, keepdims=True) + NORM_EPS)
    return r, x * r


def _accumulate(ref, part, first):
    @pl.when(first)
    def _():
        ref[...] = part

    @pl.when(jnp.logical_not(first))
    def _():
        ref[...] += part


def _rmsnorm_fwd(x, g):
    t_len, d = x.shape
    tm = _pick(t_len, (512,))

    def body(x_ref, g_ref, h_ref):
        _, xhat = _rms_stats(x_ref[...])
        h_ref[...] = (xhat * g_ref[...]).astype(BF16)

    row = pl.BlockSpec((tm, d), lambda i: (i, 0))
    vec = pl.BlockSpec((1, d), lambda i: (0, 0))
    return _call(body, name="rmsnorm_fwd", grid=(t_len // tm,), in_specs=[row, vec], out_specs=row,
                 out_shape=jax.ShapeDtypeStruct((t_len, d), BF16), compiler_params=_params(1))(x, g)


def _rmsnorm_bwd(dh, x, g, dres):
    t_len, d = x.shape
    tm = _pick(t_len, (256,))

    def body(dh_ref, x_ref, g_ref, dres_ref, dx_ref, dxb_ref, dg_ref):
        r, xhat = _rms_stats(x_ref[...])
        dh_v = dh_ref[...]
        gd = dh_v * g_ref[...]
        dx = dres_ref[...] + r * (gd - xhat * jnp.mean(gd * xhat, axis=-1, keepdims=True))
        dx_ref[...] = dx
        dxb_ref[...] = dx.astype(BF16)
        _accumulate(dg_ref, jnp.sum(dh_v * xhat, axis=0, keepdims=True), pl.program_id(0) == 0)

    row = pl.BlockSpec((tm, d), lambda i: (i, 0))
    vec = pl.BlockSpec((1, d), lambda i: (0, 0))
    return _call(body, name="rmsnorm_bwd", grid=(t_len // tm,), in_specs=[row, row, vec, row],
                 out_specs=[row, row, vec],
                 out_shape=[jax.ShapeDtypeStruct((t_len, d), F32), jax.ShapeDtypeStruct((t_len, d), BF16),
                            jax.ShapeDtypeStruct((1, d), F32)],
                 compiler_params=_params(1))(dh, x, g, dres)


def _loss_head(x, g, target):
    t_len, d = x.shape
    tm = _pick(t_len, (256,))

    def body(x_ref, g_ref, t_ref, dx_ref, dxb_ref, dg_ref, loss_ref):
        first = pl.program_id(0) == 0
        r, xhat = _rms_stats(x_ref[...])
        g_v = g_ref[...]
        err = xhat * g_v - t_ref[...]
        per_token = jnp.mean(err * err, axis=-1, keepdims=True)
        part = 0.5 * jnp.sum(per_token, axis=0, keepdims=True)
        _accumulate(loss_ref, jnp.broadcast_to(part, (1, LANES)), first)
        dy = err * (1.0 / d)
        _accumulate(dg_ref, jnp.sum(dy * xhat, axis=0, keepdims=True), first)
        gd = dy * g_v
        dx = r * (gd - xhat * jnp.mean(gd * xhat, axis=-1, keepdims=True))
        dx_ref[...] = dx
        dxb_ref[...] = dx.astype(BF16)

    row = pl.BlockSpec((tm, d), lambda i: (i, 0))
    vec = pl.BlockSpec((1, d), lambda i: (0, 0))
    return _call(body, name="loss_head", grid=(t_len // tm,), in_specs=[row, vec, row],
                 out_specs=[row, row, vec, pl.BlockSpec((1, LANES), lambda i: (0, 0))],
                 out_shape=[jax.ShapeDtypeStruct((t_len, d), F32), jax.ShapeDtypeStruct((t_len, d), BF16),
                            jax.ShapeDtypeStruct((1, d), F32), jax.ShapeDtypeStruct((1, LANES), F32)],
                 compiler_params=_params(1))(x, g, target)


def _ln_stats(x):
    mu = jnp.mean(x, axis=-1, keepdims=True)
    xc = x - mu
    r = lax.rsqrt(jnp.mean(xc * xc, axis=-1, keepdims=True) + NORM_EPS)
    return r, xc * r


def _ln_bwd(dy, r, xhat, g):
    dxh = dy * g
    return r * (dxh - jnp.mean(dxh, axis=-1, keepdims=True) - xhat * jnp.mean(dxh * xhat, axis=-1, keepdims=True))


def _rope_tables(positions):
    t_len = positions.shape[-1]
    inv_freq = 1.0 / (ROPE_THETA ** (jnp.arange(0, 2 * ROT_HALF, 2, dtype=F32) / (2 * ROT_HALF)))
    ang = positions.astype(F32).reshape(t_len, 1) * inv_freq
    cos = jnp.tile(jnp.cos(ang), (1, LANES // ROT_HALF))
    sin = jnp.tile(jnp.sin(ang), (1, LANES // ROT_HALF))
    lane = jnp.arange(LANES) % HEAD_DIM
    c = jnp.where(lane < 2 * ROT_HALF, cos, 1.0)
    s1 = jnp.where(lane < ROT_HALF, -sin, 0.0)
    s2 = jnp.where((lane >= ROT_HALF) & (lane < 2 * ROT_HALF), sin, 0.0)
    return c.astype(F32), s1.astype(F32), s2.astype(F32)


def _rope_fwd(p, tables):
    t_len = p.shape[0]
    tm = _pick(t_len, (256,))
    n_rot = K_END // LANES

    def body(p_ref, c_ref, s1_ref, s2_ref, o_ref):
        c, s1, s2 = c_ref[...], s1_ref[...], s2_ref[...]
        for j in range(V_END // LANES):
            sl = slice(j * LANES, (j + 1) * LANES)
            t = p_ref[:, sl]
            if j < n_rot:
                t = t * c + pltpu.roll(t, LANES - ROT_HALF, 1) * s1 + pltpu.roll(t, ROT_HALF, 1) * s2
            if j < Q_END // LANES:
                t = t * ATTN_SCALE
            o_ref[:, sl] = t.astype(BF16)

    tab = pl.BlockSpec((tm, LANES), lambda i: (i, 0))
    blk = pl.BlockSpec((tm, V_END), lambda i: (i, 0))
    return _call(body, name="rope_fwd", grid=(t_len // tm,), in_specs=[blk, tab, tab, tab], out_specs=blk,
                 out_shape=jax.ShapeDtypeStruct((t_len, V_END), BF16), compiler_params=_params(1))(p, *tables)


def _assemble_dp(dq, dkc, dkp, dvc, dvp, tables, dalin, dagate, du, dvin):
    t_len = dq.shape[0]
    steps = t_len // ATTN_STEP

    def body(dq_ref, dkc_ref, dkp_ref, dvc_ref, dvp_ref, c_ref, s1_ref, s2_ref, dalin_ref, dagate_ref,
             du_ref, dvin_ref, o_ref):
        keep = (pl.program_id(0) < steps - 1).astype(F32)
        for part in range(ATTN_STEP // BLK):
            rows = slice(part * BLK, (part + 1) * BLK)
            c, s1, s2 = c_ref[rows, :], s1_ref[rows, :], s2_ref[rows, :]

            def unrotate(dr):
                return dr * c + pltpu.roll(dr * s1, ROT_HALF, 1) + pltpu.roll(dr * s2, LANES - ROT_HALF, 1)

            for j in range(Q_END // LANES):
                sl = slice(j * LANES, (j + 1) * LANES)
                o_ref[rows, sl] = unrotate(dq_ref[rows, sl]).astype(BF16)
            for j in range((K_END - Q_END) // LANES):
                sl = slice(j * LANES, (j + 1) * LANES)
                dk, dv = dkc_ref[rows, sl], dvc_ref[rows, sl]
                if part == ATTN_STEP // BLK - 1:
                    dk = dk + keep * dkp_ref[:, sl]
                    dv = dv + keep * dvp_ref[:, sl]
                o_ref[rows, Q_END + j * LANES:Q_END + (j + 1) * LANES] = unrotate(dk).astype(BF16)
                o_ref[rows, K_END + j * LANES:K_END + (j + 1) * LANES] = dv.astype(BF16)
        o_ref[:, V_END:V_END + CONV_CH] = dalin_ref[...]
        o_ref[:, V_END + CONV_CH:CONV_END] = dagate_ref[...]
        o_ref[:, CONV_END:CONV_END + SGU_CH] = du_ref[...]
        o_ref[:, CONV_END + SGU_CH:IN_COLS] = dvin_ref[...]

    def cur(w):
        return pl.BlockSpec((ATTN_STEP, w), lambda i: (i, 0))

    def nxt(w):
        return pl.BlockSpec((BLK, w), lambda i: (jnp.minimum(i + 1, steps - 1), 0))

    kvw = K_END - Q_END
    return _call(body, name="assemble_dp", grid=(steps,),
                 in_specs=[cur(Q_END), cur(kvw), nxt(kvw), cur(kvw), nxt(kvw), cur(LANES), cur(LANES), cur(LANES),
                           cur(CONV_CH), cur(CONV_CH), cur(SGU_CH), cur(SGU_CH)],
                 out_specs=cur(IN_COLS), out_shape=jax.ShapeDtypeStruct((t_len, IN_COLS), BF16),
                 compiler_params=_params(1))(dq, dkc, dkp, dvc, dvp, *tables, dalin, dagate, du, dvin)


ATTN_STEP = 2 * BLK
ATTN_SCALE = HEAD_DIM ** -0.5
MASKED = -1e30


def _attn_bias():
    qi = jnp.arange(GQ * BLK)[:, None] % BLK
    kj = jnp.arange(2 * BLK)[None, :]
    dist = qi + BLK - kj
    band = (dist >= 0) & (dist < BLK)
    return jnp.stack([jnp.where(band & (kj >= BLK), 0.0, MASKED), jnp.where(band, 0.0, MASKED)]).astype(F32)


def _attn_chains(q_ref, kp_ref, kc_ref, vp_ref, vc_ref):
    kc, vc = kc_ref[...], vc_ref[...]
    rows_a, rows_b = slice(0, BLK), slice(BLK, ATTN_STEP)
    return [
        (rows_a, q_ref[:, rows_a, :].reshape(GQ * BLK, HEAD_DIM), jnp.concatenate([kp_ref[...], kc[:BLK]], axis=0),
         jnp.concatenate([vp_ref[...], vc[:BLK]], axis=0)),
        (rows_b, q_ref[:, rows_b, :].reshape(GQ * BLK, HEAD_DIM), kc, vc),
    ]


def _attn_weights(q, kk, bias, snk):
    s = lax.dot_general(q, kk, NT, preferred_element_type=F32) + bias
    m = jnp.maximum(jnp.max(s, axis=-1, keepdims=True), snk)
    e = jnp.exp(s - m)
    es = jnp.exp(snk - m)
    return e, es, 1.0 / (jnp.sum(e, axis=-1, keepdims=True) + es)


def _attn_specs(steps):
    q_spec = pl.BlockSpec((GQ, ATTN_STEP, HEAD_DIM), lambda g, n: (g, n, 0))
    cur = pl.BlockSpec((None, ATTN_STEP, HEAD_DIM), lambda g, n: (g, n, 0))
    prev = pl.BlockSpec((None, BLK, HEAD_DIM), lambda g, n: (g, jnp.maximum(2 * n - 1, 0), 0))
    snk = pl.BlockSpec((None, GQ * BLK, 1), lambda g, n: (g, 0, 0))
    bias_a = pl.BlockSpec((None, GQ * BLK, 2 * BLK), lambda g, n: (jnp.minimum(n, 1), 0, 0))
    bias_b = pl.BlockSpec((None, GQ * BLK, 2 * BLK), lambda g, n: (1, 0, 0))
    return q_spec, cur, prev, snk, bias_a, bias_b


def _attn_fwd(q, k, v, snk, comm=None):
    t_len = q.shape[1]
    steps = t_len // ATTN_STEP

    def body(q_ref, kp_ref, kc_ref, vp_ref, vc_ref, snk_ref, ba_ref, bb_ref, o_ref):
        snk = snk_ref[...]
        for (rows, q_v, kk, vv), b_ref in zip(_attn_chains(q_ref, kp_ref, kc_ref, vp_ref, vc_ref), (ba_ref, bb_ref)):
            e, _, inv = _attn_weights(q_v, kk, b_ref[...], snk)
            o = lax.dot_general(e.astype(BF16), vv, NN, preferred_element_type=F32) * inv
            o_ref[:, rows, :] = o.reshape(GQ, BLK, HEAD_DIM).astype(BF16)

    q_spec, cur, prev, snk_spec, bias_a, bias_b = _attn_specs(steps)
    bias = _attn_bias()
    return _hosted_call(body, comm, "attn_fwd", (N_KV_HEADS, steps), [q, k, k, v, v, snk, bias, bias],
                        [q_spec, prev, cur, prev, cur, snk_spec, bias_a, bias_b],
                        [jax.ShapeDtypeStruct(q.shape, BF16)], [q_spec])[0]


def _attn_bwd(q, k, v, snk, do, comm=None):
    t_len = q.shape[1]
    steps = t_len // ATTN_STEP

    def body(q_ref, kp_ref, kc_ref, vp_ref, vc_ref, snk_ref, ba_ref, bb_ref, do_ref, dq_ref, dkp_ref, dkc_ref,
             dvp_ref, dvc_ref, dsnk_ref):
        snk = snk_ref[...]
        row = lax.broadcasted_iota(jnp.int32, (SUBLANES, LANES), 0)
        tile = jnp.zeros((SUBLANES, LANES), F32)
        grads = []
        for (rows, q_v, kk, vv), b_ref in zip(_attn_chains(q_ref, kp_ref, kc_ref, vp_ref, vc_ref), (ba_ref, bb_ref)):
            e, es, inv = _attn_weights(q_v, kk, b_ref[...], snk)
            p = e * inv
            do_v = do_ref[:, rows, :].reshape(GQ * BLK, HEAD_DIM).astype(BF16)
            dp = lax.dot_general(do_v, vv, NT, preferred_element_type=F32)
            delta = jnp.sum(p * dp, axis=-1, keepdims=True)
            ds = (p * (dp - delta)).astype(BF16)
            dq = lax.dot_general(ds, kk, NN, preferred_element_type=F32) * ATTN_SCALE
            dq_ref[:, rows, :] = dq.reshape(GQ, BLK, HEAD_DIM)
            grads.append((lax.dot_general(ds, q_v, TN, preferred_element_type=F32),
                          lax.dot_general(p.astype(BF16), do_v, TN, preferred_element_type=F32)))
            per_row = -(es * inv) * delta
            for hh in range(GQ):
                tot = jnp.sum(per_row[hh * BLK:(hh + 1) * BLK], axis=0, keepdims=True)
                tile = tile + jnp.where(row == hh, tot, 0.0)
        (dk_a, dv_a), (dk_b, dv_b) = grads
        dkp_ref[...] = dk_a[:BLK]
        dvp_ref[...] = dv_a[:BLK]
        dkc_ref[0:BLK, :] = dk_a[BLK:] + dk_b[:BLK]
        dvc_ref[0:BLK, :] = dv_a[BLK:] + dv_b[:BLK]
        dkc_ref[BLK:ATTN_STEP, :] = dk_b[BLK:]
        dvc_ref[BLK:ATTN_STEP, :] = dv_b[BLK:]
        _accumulate(dsnk_ref, tile, pl.program_id(1) == 0)

    q_spec, cur, prev, snk_spec, bias_a, bias_b = _attn_specs(steps)
    bias = _attn_bias()
    step_blk = pl.BlockSpec((None, BLK, HEAD_DIM), lambda g, n: (g, n, 0))
    kv_shape = jax.ShapeDtypeStruct(k.shape, F32)
    prev_shape = jax.ShapeDtypeStruct((N_KV_HEADS, steps * BLK, HEAD_DIM), F32)
    return _hosted_call(
        body, comm, "attn_bwd", (N_KV_HEADS, steps), [q, k, k, v, v, snk, bias, bias, do],
        [q_spec, prev, cur, prev, cur, snk_spec, bias_a, bias_b, q_spec],
        [jax.ShapeDtypeStruct(q.shape, F32), prev_shape, kv_shape, prev_shape, kv_shape,
         jax.ShapeDtypeStruct((N_KV_HEADS, SUBLANES, LANES), F32)],
        [q_spec, step_blk, cur, step_blk, cur, pl.BlockSpec((None, SUBLANES, LANES), lambda g, n: (g, 0, 0))])


CONV_CHUNK = 256


def _shift_up(win, s):
    n = win.shape[0]
    return win if s == 0 else pltpu.roll(win, n - s, 0)


def _conv_col_specs(t_len):
    lin = pl.BlockSpec((t_len, LANES), lambda j: (0, V_END // LANES + j))
    gate = pl.BlockSpec((t_len, LANES), lambda j: (0, (V_END + CONV_CH) // LANES + j))
    col = pl.BlockSpec((t_len, LANES), lambda j: (0, j))
    wsp = pl.BlockSpec((CONV_PAD, LANES), lambda j: (0, j))
    return lin, gate, col, wsp


def _conv_fwd(p, w, b):
    t_len = p.shape[0]
    ch = CONV_CHUNK

    def body(lin_ref, gate_ref, w_ref, b_ref, y_ref, hp_ref):
        hp_ref[0:CONV_PAD, :] = jnp.zeros((CONV_PAD, LANES), F32)

        def fill(c, carry):
            r0 = pl.multiple_of(c * ch, ch)
            hp_ref[pl.ds(r0 + CONV_PAD, ch), :] = lin_ref[pl.ds(r0, ch), :] * _sigmoid(gate_ref[pl.ds(r0, ch), :])
            return carry

        lax.fori_loop(0, t_len // ch, fill, 0)

        def conv(c, carry):
            r0 = pl.multiple_of(c * ch, ch)
            win = hp_ref[pl.ds(r0, ch + CONV_PAD), :]
            acc = jnp.zeros((ch, LANES), F32)
            for k in range(CONV_WIDTH):
                acc = acc + _shift_up(win, CONV_PAD - (CONV_WIDTH - 1) + k)[:ch] * w_ref[k:k + 1, :]
            y_ref[pl.ds(r0, ch), :] = acc + b_ref[...]
            return carry

        lax.fori_loop(0, t_len // ch, conv, 0)

    lin, gate, col, wsp = _conv_col_specs(t_len)
    return _call(body, name="conv_fwd", grid=(CONV_CH // LANES,),
                 in_specs=[lin, gate, wsp, pl.BlockSpec((1, LANES), lambda j: (0, j))], out_specs=col,
                 out_shape=jax.ShapeDtypeStruct((t_len, CONV_CH), F32),
                 scratch_shapes=[pltpu.VMEM((t_len + CONV_PAD, LANES), F32)],
                 compiler_params=_params(1))(p, p, w, b)


def _conv_post_fwd(y, g, b):
    t_len = y.shape[0]
    tm = _pick(t_len, (512,))

    def body(y_ref, g_ref, b_ref, o_ref):
        _, xhat = _ln_stats(y_ref[...])
        z = xhat * g_ref[...] + b_ref[...]
        o_ref[...] = (z * _sigmoid(z)).astype(BF16)

    row = pl.BlockSpec((tm, CONV_CH), lambda i: (i, 0))
    vec = pl.BlockSpec((1, CONV_CH), lambda i: (0, 0))
    return _call(body, name="conv_post_fwd", grid=(t_len // tm,), in_specs=[row, vec, vec], out_specs=row,
                 out_shape=jax.ShapeDtypeStruct((t_len, CONV_CH), BF16), compiler_params=_params(1))(y, g, b)


def _conv_post_bwd(dmix, y, g, b):
    t_len = y.shape[0]
    tm = _pick(t_len, (512,))

    def body(do_ref, y_ref, g_ref, b_ref, dy_ref, dg_ref, db_ref, dcb_ref):
        first = pl.program_id(0) == 0
        r, xhat = _ln_stats(y_ref[...])
        g_v = g_ref[...]
        z = xhat * g_v + b_ref[...]
        sig = _sigmoid(z)
        dz = do_ref[...] * (sig * (1.0 + z * (1.0 - sig)))
        _accumulate(db_ref, jnp.sum(dz, axis=0, keepdims=True), first)
        _accumulate(dg_ref, jnp.sum(dz * xhat, axis=0, keepdims=True), first)
        dy = _ln_bwd(dz, r, xhat, g_v)
        dy_ref[...] = dy
        _accumulate(dcb_ref, jnp.sum(dy, axis=0, keepdims=True), first)

    row = pl.BlockSpec((tm, CONV_CH), lambda i: (i, 0))
    do_spec = pl.BlockSpec((tm, CONV_CH), lambda i: (i, Q_END // CONV_CH))
    vec = pl.BlockSpec((1, CONV_CH), lambda i: (0, 0))
    vshape = jax.ShapeDtypeStruct((1, CONV_CH), F32)
    return _call(body, name="conv_post_bwd", grid=(t_len // tm,), in_specs=[do_spec, row, vec, vec],
                 out_specs=[row, vec, vec, vec],
                 out_shape=[jax.ShapeDtypeStruct((t_len, CONV_CH), F32), vshape, vshape, vshape],
                 compiler_params=_params(1))(dmix, y, g, b)


def _conv_bwd(p, dy, w):
    t_len = p.shape[0]
    ch = CONV_CHUNK

    def body(lin_ref, gate_ref, dy_ref, w_ref, dlin_ref, dgate_ref, dw_ref, hp_ref, dyp_ref):
        hp_ref[0:CONV_PAD, :] = jnp.zeros((CONV_PAD, LANES), F32)
        dyp_ref[t_len:t_len + CONV_PAD, :] = jnp.zeros((CONV_PAD, LANES), F32)
        dw_ref[...] = jnp.zeros((CONV_PAD, LANES), F32)

        def fill(c, carry):
            r0 = pl.multiple_of(c * ch, ch)
            hp_ref[pl.ds(r0 + CONV_PAD, ch), :] = lin_ref[pl.ds(r0, ch), :] * _sigmoid(gate_ref[pl.ds(r0, ch), :])
            dyp_ref[pl.ds(r0, ch), :] = dy_ref[pl.ds(r0, ch), :]
            return carry

        lax.fori_loop(0, t_len // ch, fill, 0)

        def step(c, carry):
            r0 = pl.multiple_of(c * ch, ch)
            win_h = hp_ref[pl.ds(r0, ch + CONV_PAD), :]
            win_dy = dyp_ref[pl.ds(r0, ch + CONV_PAD), :]
            dyc = win_dy[:ch]
            dh = jnp.zeros((ch, LANES), F32)
            for k in range(CONV_WIDTH):
                tap = _shift_up(win_h, CONV_PAD - (CONV_WIDTH - 1) + k)[:ch]
                dw_ref[k:k + 1, :] += jnp.sum(dyc * tap, axis=0, keepdims=True)
                dh = dh + _shift_up(win_dy, CONV_WIDTH - 1 - k)[:ch] * w_ref[k:k + 1, :]
            lin = lin_ref[pl.ds(r0, ch), :]
            sig = _sigmoid(gate_ref[pl.ds(r0, ch), :])
            dlin_ref[pl.ds(r0, ch), :] = (dh * sig).astype(BF16)
            dgate_ref[pl.ds(r0, ch), :] = (dh * lin * (sig * (1.0 - sig))).astype(BF16)
            return carry

        lax.fori_loop(0, t_len // ch, step, 0)

    lin, gate, col, wsp = _conv_col_specs(t_len)
    half = jax.ShapeDtypeStruct((t_len, CONV_CH), BF16)
    return _call(body, name="conv_bwd", grid=(CONV_CH // LANES,), in_specs=[lin, gate, col, wsp],
                 out_specs=[col, col, wsp],
                 out_shape=[half, half, jax.ShapeDtypeStruct((CONV_PAD, CONV_CH), F32)],
                 scratch_shapes=[pltpu.VMEM((t_len + CONV_PAD, LANES), F32), pltpu.VMEM((t_len + CONV_PAD, LANES), F32)],
                 compiler_params=_params(1))(p, p, dy, w)


def _sgu_mixed(v, w_ref, bt_ref, j):
    lane = lax.broadcasted_iota(jnp.int32, (BLK, LANES), 1)
    lo = lane < HEAD_DIM
    tri = lax.broadcasted_iota(jnp.int32, (BLK, BLK), 0) >= lax.broadcasted_iota(jnp.int32, (BLK, BLK), 1)
    vs = v[:, j * LANES:(j + 1) * LANES]
    v_lo = jnp.where(lo, vs, 0.0).astype(BF16)
    v_hi = jnp.where(lo, 0.0, vs).astype(BF16)
    w_lo = jnp.where(tri, w_ref[2 * j], 0.0).astype(BF16)
    w_hi = jnp.where(tri, w_ref[2 * j + 1], 0.0).astype(BF16)
    m = (lax.dot_general(w_lo, v_lo, NN, preferred_element_type=F32)
         + lax.dot_general(w_hi, v_hi, NN, preferred_element_type=F32))
    bias = jnp.where(lo, bt_ref[:, 2 * j:2 * j + 1], bt_ref[:, 2 * j + 1:2 * j + 2])
    return m + bias, (v_lo, v_hi, w_lo, w_hi, lo, tri)


def _sgu_specs():
    u_spec = pl.BlockSpec((BLK, SGU_CH), lambda i: (i, CONV_END // SGU_CH))
    v_spec = pl.BlockSpec((BLK, SGU_CH), lambda i: (i, CONV_END // SGU_CH + 1))
    vec = pl.BlockSpec((1, SGU_CH), lambda i: (0, 0))
    w_spec = pl.BlockSpec((SGU_HEADS, BLK, BLK), lambda i: (0, 0, 0))
    bt_spec = pl.BlockSpec((BLK, SGU_HEADS), lambda i: (0, 0))
    row = pl.BlockSpec((BLK, SGU_CH), lambda i: (i, 0))
    return u_spec, v_spec, vec, w_spec, bt_spec, row


def _sgu_fwd(p, g, b, w, bt):
    t_len = p.shape[0]

    def body(u_ref, vin_ref, g_ref, b_ref, w_ref, bt_ref, o_ref):
        _, xhat = _ln_stats(vin_ref[...])
        v = xhat * g_ref[...] + b_ref[...]
        for j in range(SGU_CH // LANES):
            m, _ = _sgu_mixed(v, w_ref, bt_ref, j)
            sl = slice(j * LANES, (j + 1) * LANES)
            o_ref[:, sl] = (u_ref[:, sl] * m).astype(BF16)

    u_spec, v_spec, vec, w_spec, bt_spec, row = _sgu_specs()
    return _call(body, name="sgu_fwd", grid=(t_len // BLK,), in_specs=[u_spec, v_spec, vec, vec, w_spec, bt_spec],
                 out_specs=row, out_shape=jax.ShapeDtypeStruct((t_len, SGU_CH), BF16),
                 compiler_params=_params(1))(p, p, g, b, w, bt)


def _sgu_bwd(p, dmix, g, b, w, bt):
    t_len = p.shape[0]

    def body(u_ref, vin_ref, do_ref, g_ref, b_ref, w_ref, bt_ref, du_ref, dvin_ref, dw_ref, dbt_ref, dg_ref,
             db_ref, dv_ref):
        first = pl.program_id(0) == 0
        r, xhat = _ln_stats(vin_ref[...])
        g_v = g_ref[...]
        v = xhat * g_v + b_ref[...]
        lane = lax.broadcasted_iota(jnp.int32, (BLK, LANES), 1)
        dbt = jnp.zeros((BLK, LANES), F32)

        @pl.when(first)
        def _():
            dw_ref[...] = jnp.zeros((SGU_HEADS, BLK, BLK), F32)

        for j in range(SGU_CH // LANES):
            m, (v_lo, v_hi, w_lo, w_hi, lo, tri) = _sgu_mixed(v, w_ref, bt_ref, j)
            sl = slice(j * LANES, (j + 1) * LANES)
            do_v = do_ref[:, sl]
            du_ref[:, sl] = (do_v * m).astype(BF16)
            dm = do_v * u_ref[:, sl]
            dm_lo = jnp.where(lo, dm, 0.0)
            dm_hi = jnp.where(lo, 0.0, dm)
            dbt = dbt + jnp.where(lane == 2 * j, jnp.sum(dm_lo, axis=-1, keepdims=True), 0.0)
            dbt = dbt + jnp.where(lane == 2 * j + 1, jnp.sum(dm_hi, axis=-1, keepdims=True), 0.0)
            dm_lo, dm_hi = dm_lo.astype(BF16), dm_hi.astype(BF16)
            dw_ref[2 * j] += jnp.where(tri, lax.dot_general(dm_lo, v_lo, NT, preferred_element_type=F32), 0.0)
            dw_ref[2 * j + 1] += jnp.where(tri, lax.dot_general(dm_hi, v_hi, NT, preferred_element_type=F32), 0.0)
            dv_ref[:, sl] = (lax.dot_general(w_lo, dm_lo, TN, preferred_element_type=F32)
                             + lax.dot_general(w_hi, dm_hi, TN, preferred_element_type=F32))
        _accumulate(dbt_ref, dbt, first)
        dv = dv_ref[...]
        _accumulate(db_ref, jnp.sum(dv, axis=0, keepdims=True), first)
        _accumulate(dg_ref, jnp.sum(dv * xhat, axis=0, keepdims=True), first)
        dvin_ref[...] = _ln_bwd(dv, r, xhat, g_v).astype(BF16)

    u_spec, v_spec, vec, w_spec, bt_spec, row = _sgu_specs()
    do_spec = pl.BlockSpec((BLK, SGU_CH), lambda i: (i, (Q_END + CONV_CH) // SGU_CH))
    half = jax.ShapeDtypeStruct((t_len, SGU_CH), BF16)
    vshape = jax.ShapeDtypeStruct((1, SGU_CH), F32)
    return _call(body, name="sgu_bwd", grid=(t_len // BLK,),
                 in_specs=[u_spec, v_spec, do_spec, vec, vec, w_spec, bt_spec],
                 out_specs=[row, row, w_spec, pl.BlockSpec((BLK, LANES), lambda i: (0, 0)), vec, vec],
                 out_shape=[half, half, jax.ShapeDtypeStruct((SGU_HEADS, BLK, BLK), F32),
                            jax.ShapeDtypeStruct((BLK, LANES), F32), vshape, vshape],
                 scratch_shapes=[pltpu.VMEM((BLK, SGU_CH), F32)],
                 compiler_params=_params(1))(p, p, dmix, g, b, w, bt)


def _place():
    x, y, c = lax.axis_index("x"), lax.axis_index("y"), lax.axis_index("c")
    chips = [(1 - x, y), (x, 1 - y), (1 - x, 1 - y)]
    return x, y, c, chips


def _hbm_specs(n):
    return [pl.BlockSpec(memory_space=pltpu.HBM)] * n


def _comm_params():
    return pltpu.CompilerParams(has_side_effects=True)


def _remote(src, dst, send_sem, recv_sem, to):
    return pltpu.make_async_remote_copy(src_ref=src, dst_ref=dst, send_sem=send_sem, recv_sem=recv_sem,
                                        device_id=to, device_id_type=MESH_ID)


def _cast_place(w_local, chip):
    n, rows, cols = w_local.shape

    def body(chip_ref, w_ref, o_ref):
        o_ref[...] = w_ref[...].astype(BF16)

    grid_spec = pltpu.PrefetchScalarGridSpec(
        num_scalar_prefetch=1, grid=(n, rows // ROW_TILE),
        in_specs=[pl.BlockSpec((None, ROW_TILE, cols), lambda l, i, ch: (l, i, 0))],
        out_specs=pl.BlockSpec((None, None, ROW_TILE, cols), lambda l, i, ch: (l, ch[0], i, 0)))
    return _call(body, name="cast_place", grid_spec=grid_spec,
                 out_shape=jax.ShapeDtypeStruct((n, N_CHIPS, rows, cols), BF16), compiler_params=_params(2))(chip, w_local)


def _all_gather_weights(placed, shards):
    n_placed, nt = len(placed), len(placed) + len(shards)

    def body(*refs):
        ins, outs = refs[:nt], refs[nt:2 * nt]
        ici_send, ici_recv, d2d_send, d2d_recv, local_sem = refs[2 * nt:]
        x, y, c, chips = _place()
        me = 2 * x + y
        sibling = (x, y, 1 - c)
        local = [pltpu.make_async_copy(ins[t].at[l], outs[t].at[l, me], local_sem.at[2 * (t - n_placed) + l])
                 for t in range(n_placed, nt) for l in range(2)]
        for cp in local:
            cp.start()
        sends = []
        for t in range(nt):
            src = outs[t].at[c, me] if t < n_placed else ins[t].at[c]
            for j, (px, py) in enumerate(chips):
                sends.append(_remote(src, outs[t].at[c, me], ici_send.at[3 * t + j], ici_recv.at[3 * t + j],
                                     (px, py, c)))
        for cp in sends:
            cp.start()
        for t in range(nt):
            for j, (px, py) in enumerate(chips):
                slab = outs[t].at[c, 2 * px + py]
                _remote(slab, slab, ici_send.at[3 * t + j], ici_recv.at[3 * t + j], (px, py, c)).wait_recv()
                fwd = _remote(slab, slab, d2d_send.at[3 * t + j], d2d_recv.at[3 * t + j], sibling)
                fwd.start()
                sends.append(fwd)
        for t in range(nt):
            for j, (px, py) in enumerate(chips):
                slab = outs[t].at[1 - c, 2 * px + py]
                _remote(slab, slab, d2d_send.at[3 * t + j], d2d_recv.at[3 * t + j], sibling).wait_recv()
        for cp in sends:
            cp.wait_send()
        for cp in local:
            cp.wait()

    out_shape = [jax.ShapeDtypeStruct(p.shape, p.dtype) for p in placed]
    out_shape += [jax.ShapeDtypeStruct((2, N_CHIPS) + s.shape[1:], s.dtype) for s in shards]
    sems = [pltpu.SemaphoreType.DMA((3 * nt,))] * 4 + [pltpu.SemaphoreType.DMA((2 * len(shards),))]
    return _call(body, name="all_gather_weights", in_specs=_hbm_specs(nt), out_specs=_hbm_specs(nt),
                 out_shape=out_shape, scratch_shapes=sems, input_output_aliases={t: t for t in range(n_placed)},
                 compiler_params=_comm_params())(*placed, *shards)


def _gather_comm(bufs, pieces):
    n = len(pieces)
    sems = [pltpu.SemaphoreType.DMA((3 * n,))] * 4

    def half(ref, layer, chip, r0, nr, which):
        return ref.at[layer, chip, pl.ds(pl.multiple_of(r0 + which * (nr // 2), SUBLANES), nr // 2)]

    def start(rd, al, fr, sm):
        ici_send, ici_recv, _, _ = sm
        x, y, c, chips = _place()
        for i, (t, layer, r0, nr) in enumerate(pieces):
            own = half(al[t], layer, 2 * x + y, r0, nr, c)
            for j, (px, py) in enumerate(chips):
                _remote(own, own, ici_send.at[3 * i + j], ici_recv.at[3 * i + j], (px, py, c)).start()

    def finish(rd, al, fr, sm):
        ici_send, ici_recv, d2d_send, d2d_recv = sm
        x, y, c, chips = _place()
        sibling = (x, y, 1 - c)
        passed = []
        for i, (t, layer, r0, nr) in enumerate(pieces):
            for j, (px, py) in enumerate(chips):
                got = half(al[t], layer, 2 * px + py, r0, nr, c)
                _remote(got, got, ici_send.at[3 * i + j], ici_recv.at[3 * i + j], (px, py, c)).wait_recv()
                fwd = _remote(got, got, d2d_send.at[3 * i + j], d2d_recv.at[3 * i + j], sibling)
                fwd.start()
                passed.append(fwd)
        for i, (t, layer, r0, nr) in enumerate(pieces):
            own = half(al[t], layer, 2 * x + y, r0, nr, c)
            for j, (px, py) in enumerate(chips):
                _remote(own, own, ici_send.at[3 * i + j], ici_recv.at[3 * i + j], (px, py, c)).wait_send()
                theirs = half(al[t], layer, 2 * px + py, r0, nr, 1 - c)
                _remote(theirs, theirs, d2d_send.at[3 * i + j], d2d_recv.at[3 * i + j], sibling).wait_recv()
        for fwd in passed:
            fwd.wait_send()

    return _Comm([], bufs, [], sems, start, finish)


def _own_rows(ref, c, which=0):
    hr = ref.shape[-2] // 2
    start = pl.multiple_of((c if which == 0 else 1 - c) * hr, SUBLANES)
    return ref.at[(slice(None),) * (len(ref.shape) - 2) + (pl.ds(start, hr),)]


def _to_owner_comm(stacks, layer):
    nt = len(stacks)
    sems = [pltpu.SemaphoreType.DMA((nt,))] * 2
    fresh = [jax.ShapeDtypeStruct((N_CHIPS, s.shape[2] // 2, s.shape[3]), s.dtype) for s in stacks]

    def copies(rd, fr, sm):
        x, y, c, _ = _place()
        return [_remote(_own_rows(rd[t].at[layer], c, 1), fr[t], sm[0].at[t], sm[1].at[t], (x, y, 1 - c))
                for t in range(nt)]

    def start(rd, al, fr, sm):
        for cp in copies(rd, fr, sm):
            cp.start()

    def finish(rd, al, fr, sm):
        for cp in copies(rd, fr, sm):
            cp.wait()

    return _Comm(stacks, [], fresh, sems, start, finish)


def _chip_comm(partials):
    nt = len(partials)
    sems = [pltpu.SemaphoreType.DMA((3 * nt,))] * 2
    fresh = [jax.ShapeDtypeStruct((3,) + p.shape[1:], p.dtype) for p in partials]

    def each(rd, fr, sm, act):
        x, y, c, chips = _place()
        for t in range(nt):
            for j, (px, py) in enumerate(chips):
                act(_remote(rd[t].at[2 * px + py], fr[t].at[j], sm[0].at[3 * t + j], sm[1].at[3 * t + j], (px, py, c)))

    def start(rd, al, fr, sm):
        each(rd, fr, sm, lambda cp: cp.start())

    def finish(rd, al, fr, sm):
        each(rd, fr, sm, lambda cp: cp.wait())

    return _Comm(partials, [], fresh, sems, start, finish)


def _from_owner_comm(finals, layer):
    nt = len(finals)
    sems = [pltpu.SemaphoreType.DMA((nt,))] * 2

    def start(rd, al, fr, sm):
        x, y, c, _ = _place()
        for t in range(nt):
            mine = _own_rows(al[t].at[layer], c)
            _remote(mine, mine, sm[0].at[t], sm[1].at[t], (x, y, 1 - c)).start()

    def finish(rd, al, fr, sm):
        x, y, c, _ = _place()
        for t in range(nt):
            mine, theirs = _own_rows(al[t].at[layer], c), _own_rows(al[t].at[layer], c, 1)
            _remote(mine, mine, sm[0].at[t], sm[1].at[t], (x, y, 1 - c)).wait_send()
            _remote(theirs, theirs, sm[0].at[t], sm[1].at[t], (x, y, 1 - c)).wait_recv()

    return _Comm([], finals, [], sems, start, finish)


def _all_reduce_small(buf):
    rows = buf.shape[0]

    def body(x_ref, out_ref, all_ref, send_sems, recv_sems, local_sem):
        x, y, c, chips = _place()
        me, sibling = (x, y, c), (x, y, 1 - c)

        def block(px, py, pc):
            return all_ref.at[pl.ds((4 * px + 2 * py + pc) * rows, rows), :]

        def copy(k, blk, to, src=None):
            return _remote(block(*blk) if src is None else src, block(*blk), send_sems.at[k], recv_sems.at[k], to)

        mine = pltpu.make_async_copy(x_ref, block(*me), local_sem)
        mine.start()
        first = [copy(0, me, sibling, src=x_ref)]
        first += [copy(1 + j, me, (*chip, c), src=x_ref) for j, chip in enumerate(chips)]
        for cp in first:
            cp.start()
        passed = [copy(4 + j, (*chip, c), sibling) for j, chip in enumerate(chips)]
        for j, chip in enumerate(chips):
            copy(1 + j, (*chip, c), me).wait_recv()
            passed[j].start()
        copy(0, sibling, me).wait_recv()
        for j, chip in enumerate(chips):
            copy(4 + j, (*chip, 1 - c), me).wait_recv()
        for cp in first + passed:
            cp.wait_send()
        mine.wait()
        tot = all_ref[0:rows, :]
        for k in range(1, N_DEV):
            tot = tot + all_ref[k * rows:(k + 1) * rows, :]
        out_ref[...] = tot

    vm = pl.BlockSpec(memory_space=pltpu.VMEM)
    return _call(body, name="all_reduce_small", in_specs=[vm], out_specs=vm,
                 out_shape=jax.ShapeDtypeStruct(buf.shape, F32),
                 scratch_shapes=[pltpu.VMEM((N_DEV * rows, LANES), F32), pltpu.SemaphoreType.DMA((7,)),
                                 pltpu.SemaphoreType.DMA((7,)), pltpu.SemaphoreType.DMA],
                 compiler_params=pltpu.CompilerParams(has_side_effects=True,
                                                      vmem_limit_bytes=V7X_VMEM_LIMIT_BYTES))(buf)


ROW_TILE = 128


def _chip_partial(stack, received, layer, place):
    _, half_rows, cols = received.shape
    tr = _pick(half_rows, (ROW_TILE, ROW_TILE // 2))
    nh = half_rows // tr

    def body(place_ref, a_ref, b_ref, o_ref):
        o_ref[...] = (a_ref[...].astype(F32) + b_ref[...].astype(F32)).astype(BF16)

    blk = pl.BlockSpec((None, tr, cols), lambda s, i, pr: (s, i, 0))
    grid_spec = pltpu.PrefetchScalarGridSpec(
        num_scalar_prefetch=1, grid=(N_CHIPS, nh),
        in_specs=[pl.BlockSpec((None, None, tr, cols), lambda s, i, pr: (layer, s, pr[0] * nh + i, 0)), blk],
        out_specs=blk)
    return _call(body, name="chip_partial", grid_spec=grid_spec, out_shape=jax.ShapeDtypeStruct(received.shape, BF16),
                 compiler_params=_params(2))(place, stack, received)


def _final_sum(partial, from_chips, layer, place, finals):
    _, half_rows, cols = partial.shape
    tr = _pick(half_rows, (ROW_TILE, ROW_TILE // 2))
    nh = half_rows // tr

    def body(place_ref, a_ref, r_ref, *rest):
        o_ref = rest[-1]
        tot = a_ref[...].astype(F32)
        for j in range(3):
            tot = tot + r_ref[j].astype(F32)
        o_ref[...] = tot

    in_specs = [pl.BlockSpec((None, tr, cols), lambda i, pr: (pr[1], i, 0)),
                pl.BlockSpec((3, tr, cols), lambda i, pr: (0, i, 0))]
    args = [place, partial, from_chips]
    kw = {}
    if finals is not None:
        in_specs.append(pl.BlockSpec(memory_space=pl.ANY))
        args.append(finals)
        kw["input_output_aliases"] = {3: 0}
    grid_spec = pltpu.PrefetchScalarGridSpec(
        num_scalar_prefetch=1, grid=(nh,), in_specs=in_specs,
        out_specs=pl.BlockSpec((None, tr, cols), lambda i, pr: (layer, pr[0] * nh + i, 0)))
    return _call(body, name="final_sum", grid_spec=grid_spec,
                 out_shape=jax.ShapeDtypeStruct((2, 2 * half_rows, cols), F32), compiler_params=_params(1), **kw)(*args)


def _adamw(w, g, m, v, comm=None):
    n, rows, cols = w.shape
    tr = _pick(rows, (ROW_TILE, SUBLANES))
    c1 = 1.0 - ADAM_B1 ** ADAM_STEP
    c2 = 1.0 - ADAM_B2 ** ADAM_STEP

    def body(w_ref, g_ref, m_ref, v_ref, d_ref, nm_ref, nv_ref, go_ref):
        g_v = g_ref[...]
        go_ref[...] = g_v
        nm = ADAM_B1 * m_ref[...] + (1.0 - ADAM_B1) * g_v
        nv = ADAM_B2 * v_ref[...] + (1.0 - ADAM_B2) * (g_v * g_v)
        nm_ref[...] = nm
        nv_ref[...] = nv
        d_ref[...] = -ADAM_LR * ((nm / c1) / (jnp.sqrt(nv / c2) + ADAM_EPS) + ADAM_WD * w_ref[...])

    blk = pl.BlockSpec((None, tr, cols), lambda l, i: (l, i, 0))
    shape = jax.ShapeDtypeStruct(w.shape, F32)
    return _hosted_call(body, comm, "adamw", (n, rows // tr), [w, g, m, v], [blk] * 4, [shape] * 4, [blk] * 4)


def _to_heads(a, n_heads):
    t_len = a.shape[0]
    return a.reshape(t_len, n_heads, HEAD_DIM).transpose(1, 0, 2)


def _from_heads(a):
    n_heads, t_len, _ = a.shape
    return a.transpose(1, 0, 2).reshape(t_len, n_heads * HEAD_DIM)


class _Schedule:
    def __init__(self):
        self.sites = {}
        self.open = []

    def add(self, site, make, done=None):
        self.sites.setdefault(site, []).append((make, done))

    def begin(self, site):
        self.open = [(make(), done) for make, done in self.sites.pop(site, [])]
        return _merge_comms([cm for cm, _ in self.open])

    def end(self):
        for cm, done in self.open:
            if done is not None:
                done(cm)
        self.open = []


def _ffn_fwd(x, gain, wg, which, layer, sched):
    w_in_name, w_out_name = f"ffn{which}_w_in", f"ffn{which}_w_out"
    h = _rmsnorm_fwd(x, gain)
    comm = sched.begin(("ffn_in", layer, which))
    gu, act = _mm_ffn_in(h, wg[w_in_name], layer, comm)
    sched.end()
    comm = sched.begin(("ffn_out", layer, which))
    x_new = _mm_out_res("ffn_out", act, wg[w_out_name], layer, x, FFN_RESIDUAL_WEIGHT, comm)[0]
    sched.end()
    return x_new, (x, h, gu, act)


def _ffn_bwd(dx, dxb, saved, gain, wg, which, layer, stacks, sched, weights_first=False):
    w_in_name, w_out_name = f"ffn{which}_w_in", f"ffn{which}_w_out"
    x, h, gu, act = saved
    out = {}

    def dact():
        comm = sched.begin(("ffn_dact", layer, which))
        out["dgu"] = _mm_dact_swiglu(dxb, wg[w_out_name], layer, gu, FFN_RESIDUAL_WEIGHT, comm)
        sched.end()

    def dw_out():
        comm = sched.begin(("ffn_dw_out", layer, which))
        stacks[w_out_name] = _dw_rows("ffn_dw_out", act, dxb, layer, stacks[w_out_name], FFN_RESIDUAL_WEIGHT, comm)
        sched.end()

    def dh():
        comm = sched.begin(("ffn_dh", layer, which))
        out["dh"] = _mm_dh_ffn(out["dgu"], wg[w_in_name], layer, comm)
        sched.end()

    def dw_in():
        comm = sched.begin(("ffn_dw_in", layer, which))
        stacks[w_in_name] = _dw_ffn_in(h, out["dgu"], layer, stacks[w_in_name], comm)
        sched.end()

    for step in ((dw_out, dact, dw_in, dh) if weights_first else (dact, dw_out, dh, dw_in)):
        step()
    return _rmsnorm_bwd(out["dh"], x, gain, dx)


def _mix_fwd(x, gain, wg, layer, small, tables, sched):
    h = _rmsnorm_fwd(x, gain)
    comm = sched.begin(("mix_in", layer))
    p = _mm_proj(h, wg["w_in"], layer, comm)[0]
    sched.end()
    qkv = _rope_fwd(p, tables)
    q = _to_heads(qkv[:, :Q_END], N_Q_HEADS)
    k = _to_heads(qkv[:, Q_END:K_END], N_KV_HEADS)
    v = _to_heads(qkv[:, K_END:V_END], N_KV_HEADS)
    comm = sched.begin(("attn", layer))
    attn = _from_heads(_attn_fwd(q, k, v, small["snk"], comm))
    sched.end()
    y = _conv_fwd(p, small["conv_w"], small["conv_b"])
    conv = _conv_post_fwd(y, small["conv_ln_g"], small["conv_ln_b"])
    sgu = _sgu_fwd(p, small["sgu_ln_g"], small["sgu_ln_b"], small["sgu_w"], small["sgu_bt"])
    mix = jnp.concatenate([attn, conv, sgu], axis=1)
    x_new = _mm_out_res("mix_out", mix, wg["w_out"], layer, x, 1.0)[0]
    return x_new, (x, h, p, q, k, v, y, mix)


def _mix_bwd(dx, dxb, saved, gain, wg, small, tables, layer, stacks, sched):
    x, h, p, q, k, v, y, mix = saved
    comm = sched.begin(("mix_dout", layer))
    dmix = _mm_dmix(dxb, wg["w_out"], layer, comm)
    sched.end()
    stacks["w_out"] = _dw_rows("mix_dw_out", mix, dxb, layer, stacks["w_out"], 1.0)
    do = _to_heads(dmix[:, :Q_END], N_Q_HEADS)
    comm = sched.begin(("attn_bwd", layer))
    dq, dkp, dkc, dvp, dvc, dsnk = _attn_bwd(q, k, v, small["snk"], do, comm)
    sched.end()
    dy, d_ln_g, d_ln_b, d_conv_b = _conv_post_bwd(dmix, y, small["conv_ln_g"], small["conv_ln_b"])
    dalin, dagate, d_conv_w = _conv_bwd(p, dy, small["conv_w"])
    du, dvin, d_sgu_w, d_sgu_bt, d_sgu_g, d_sgu_b = _sgu_bwd(p, dmix, small["sgu_ln_g"], small["sgu_ln_b"],
                                                           small["sgu_w"], small["sgu_bt"])
    dp = _assemble_dp(_from_heads(dq), _from_heads(dkc), _from_heads(dkp), _from_heads(dvc), _from_heads(dvp),
                      tables, dalin, dagate, du, dvin)
    stacks["w_in"] = _dw_mix_in(h, dp, layer, stacks["w_in"])
    comm = sched.begin(("mix_dh", layer))
    dh = _mm_dh_mix(dp, wg["w_in"], layer, comm)
    sched.end()
    dx_in, dxb_in, dgain = _rmsnorm_bwd(dh, x, gain, dx)
    grads = {
        "norm_mix": dgain[0], "conv_dw_w": d_conv_w[:CONV_WIDTH], "conv_dw_b": d_conv_b[0],
        "conv_ln_g": d_ln_g[0], "conv_ln_b": d_ln_b[0], "sgu_ln_g": d_sgu_g[0], "sgu_ln_b": d_sgu_b[0],
        "sgu_w": d_sgu_w, "sgu_b": d_sgu_bt[:, :SGU_HEADS].T, "attn_sinks": dsnk[:, :GQ, 0].reshape(N_Q_HEADS),
    }
    return dx_in, dxb_in, grads


BIG = ("ffn1_w_in", "ffn1_w_out", "w_in", "w_out", "ffn2_w_in", "ffn2_w_out")
SMALL = ("norm_ffn1", "norm_mix", "conv_dw_w", "conv_dw_b", "conv_ln_g", "conv_ln_b", "sgu_ln_g", "sgu_ln_b",
         "sgu_w", "sgu_b", "attn_sinks", "norm_ffn2", "final_norm")
WEIGHTS = ("norm_ffn1", "ffn1_w_in", "ffn1_w_out", "norm_mix", "w_in", "conv_dw_w", "conv_dw_b", "conv_ln_g",
           "conv_ln_b", "sgu_ln_g", "sgu_ln_b", "sgu_w", "sgu_b", "attn_sinks", "w_out", "norm_ffn2", "ffn2_w_in",
           "ffn2_w_out", "final_norm")
PACK_ROWS = SUBLANES * LANES

FIRST_GATHER = [("ffn1_w_in", 0, None)]
FORWARD_PLAN = {
    ("ffn_in", 0, 1): [("ffn1_w_out", 0, None), ("w_in", 0, None)],
    ("ffn_out", 0, 1): [("ffn2_w_in", 0, 0)],
    ("mix_in", 0): [("w_out", 0, None)],
    ("attn", 0): [("ffn2_w_in", 0, 1)],
    ("ffn_in", 0, 2): [("ffn2_w_out", 0, None), ("ffn1_w_in", 1, 0)],
    ("ffn_out", 0, 2): [("ffn1_w_in", 1, 1)],
    ("ffn_in", 1, 1): [("ffn1_w_out", 1, None), ("w_in", 1, None)],
    ("ffn_out", 1, 1): [("ffn2_w_in", 1, 0)],
    ("mix_in", 1): [("w_out", 1, None)],
    ("attn", 1): [("ffn2_w_in", 1, 1)],
    ("ffn_in", 1, 2): [("ffn2_w_out", 1, None)],
}
SUBLAYER_WEIGHTS = {"ffn1": ["ffn1_w_out", "ffn1_w_in"], "ffn2": ["ffn2_w_out", "ffn2_w_in"], "mix": ["w_out", "w_in"]}


def _pack(arrays):
    flat = jnp.concatenate([a.reshape(-1).astype(F32) for a in arrays])
    pad = (-flat.shape[0]) % PACK_ROWS
    return jnp.pad(flat, (0, pad)).reshape(-1, LANES)


def _unpack(buf, shapes):
    flat = buf.reshape(-1)
    out, off = [], 0
    for s in shapes:
        n = 1
        for d in s:
            n *= d
        out.append(flat[off:off + n].reshape(s))
        off += n
    return out


def kernel(x, positions, norm_ffn1, ffn1_w_in, ffn1_w_out, norm_mix, w_in, conv_dw_w, conv_dw_b, conv_ln_g, conv_ln_b, sgu_ln_g, sgu_ln_b, sgu_w, sgu_b, attn_sinks, w_out, norm_ffn2, ffn2_w_in, ffn2_w_out, final_norm, loss_target, m_norm_ffn1, m_ffn1_w_in, m_ffn1_w_out, m_norm_mix, m_w_in, m_conv_dw_w, m_conv_dw_b, m_conv_ln_g, m_conv_ln_b, m_sgu_ln_g, m_sgu_ln_b, m_sgu_w, m_sgu_b, m_attn_sinks, m_w_out, m_norm_ffn2, m_ffn2_w_in, m_ffn2_w_out, m_final_norm, v_norm_ffn1, v_ffn1_w_in, v_ffn1_w_out, v_norm_mix, v_w_in, v_conv_dw_w, v_conv_dw_b, v_conv_ln_g, v_conv_ln_b, v_sgu_ln_g, v_sgu_ln_b, v_sgu_w, v_sgu_b, v_attn_sinks, v_w_out, v_norm_ffn2, v_ffn2_w_in, v_ffn2_w_out, v_final_norm):
    w = dict(norm_ffn1=norm_ffn1, ffn1_w_in=ffn1_w_in, ffn1_w_out=ffn1_w_out, norm_mix=norm_mix, w_in=w_in,
             conv_dw_w=conv_dw_w, conv_dw_b=conv_dw_b, conv_ln_g=conv_ln_g, conv_ln_b=conv_ln_b, sgu_ln_g=sgu_ln_g,
             sgu_ln_b=sgu_ln_b, sgu_w=sgu_w, sgu_b=sgu_b, attn_sinks=attn_sinks, w_out=w_out, norm_ffn2=norm_ffn2,
             ffn2_w_in=ffn2_w_in, ffn2_w_out=ffn2_w_out, final_norm=final_norm)
    m = dict(norm_ffn1=m_norm_ffn1, ffn1_w_in=m_ffn1_w_in, ffn1_w_out=m_ffn1_w_out, norm_mix=m_norm_mix, w_in=m_w_in,
             conv_dw_w=m_conv_dw_w, conv_dw_b=m_conv_dw_b, conv_ln_g=m_conv_ln_g, conv_ln_b=m_conv_ln_b,
             sgu_ln_g=m_sgu_ln_g, sgu_ln_b=m_sgu_ln_b, sgu_w=m_sgu_w, sgu_b=m_sgu_b, attn_sinks=m_attn_sinks,
             w_out=m_w_out, norm_ffn2=m_norm_ffn2, ffn2_w_in=m_ffn2_w_in, ffn2_w_out=m_ffn2_w_out,
             final_norm=m_final_norm)
    v = dict(norm_ffn1=v_norm_ffn1, ffn1_w_in=v_ffn1_w_in, ffn1_w_out=v_ffn1_w_out, norm_mix=v_norm_mix, w_in=v_w_in,
             conv_dw_w=v_conv_dw_w, conv_dw_b=v_conv_dw_b, conv_ln_g=v_conv_ln_g, conv_ln_b=v_conv_ln_b,
             sgu_ln_g=v_sgu_ln_g, sgu_ln_b=v_sgu_ln_b, sgu_w=v_sgu_w, sgu_b=v_sgu_b, attn_sinks=v_attn_sinks,
             w_out=v_w_out, norm_ffn2=v_norm_ffn2, ffn2_w_in=v_ffn2_w_in, ffn2_w_out=v_ffn2_w_out,
             final_norm=v_final_norm)
    depth = norm_ffn1.shape[0]
    assert depth == 2 and x.shape[0] == 1
    xc = lax.axis_index("x")
    yc = lax.axis_index("y")
    cc = lax.axis_index("c")
    chip = 2 * xc + yc

    chip_arr = chip.reshape(1).astype(jnp.int32)
    place = jnp.stack([cc, chip]).astype(jnp.int32)
    wg = {n: _cast_place(w[n], chip_arr) for n in BIG}
    sched = _Schedule()

    def gather(pieces):
        names = sorted({n for n, _, _ in pieces})
        half = w["ffn1_w_in"].shape[1] // 2

        def make():
            rows = lambda n, part: (0, wg[n].shape[2]) if part is None else (part * half, half)
            return _gather_comm([wg[n] for n in names], [(names.index(n), l, *rows(n, part)) for n, l, part in pieces])

        return make, lambda cm: wg.update(zip(names, cm.aliased_out))

    make, done = gather(FIRST_GATHER)
    first = make()
    _standalone("gather_first", first)
    done(first)
    for site, pieces in FORWARD_PLAN.items():
        sched.add(site, *gather(pieces))
    conv_w_full = _all_gather_weights([], [conv_dw_w])[0].transpose(0, 2, 1, 3).reshape(depth, CONV_WIDTH, CONV_CH)
    conv_w_full = jnp.pad(conv_w_full, ((0, 0), (0, CONV_PAD - CONV_WIDTH), (0, 0)))

    tables = _rope_tables(positions)
    small = []
    for l in range(depth):
        small.append(dict(
            snk=jnp.broadcast_to(attn_sinks[l].reshape(N_KV_HEADS, GQ, 1, 1), (N_KV_HEADS, GQ, BLK, 1)).reshape(
                N_KV_HEADS, GQ * BLK, 1),
            conv_w=conv_w_full[l], conv_b=conv_dw_b[l][None], conv_ln_g=conv_ln_g[l][None],
            conv_ln_b=conv_ln_b[l][None], sgu_ln_g=sgu_ln_g[l][None], sgu_ln_b=sgu_ln_b[l][None], sgu_w=sgu_w[l],
            sgu_bt=sgu_b[l].T))

    xs = x[0]
    saved = []
    for l in range(depth):
        xs, s1 = _ffn_fwd(xs, norm_ffn1[l][None], wg, 1, l, sched)
        xs, s2 = _mix_fwd(xs, norm_mix[l][None], wg, l, small[l], tables, sched)
        xs, s3 = _ffn_fwd(xs, norm_ffn2[l][None], wg, 2, l, sched)
        saved.append((s1, s2, s3))
    dx, dxb, d_final, loss_part = _loss_head(xs, final_norm[None], loss_target[0])

    stacks = {n: None for n in BIG}
    partials, from_chips = {}, {}
    finals = {n: None for n in BIG}

    def to_owner(layer, names):
        def done(cm):
            for n, received in zip(names, cm.fresh_out):
                partials[n, layer] = _chip_partial(stacks[n], received, layer, place)

        return lambda: _to_owner_comm([stacks[n] for n in names], layer), done

    def between_chips(layer, names, then_sum=()):
        def done(cm):
            from_chips.update({(n, layer): r for n, r in zip(names, cm.fresh_out)})
            for n in then_sum:
                finals[n] = _final_sum(partials[n, layer], from_chips[n, layer], layer, place, finals[n])

        return lambda: _chip_comm([partials[n, layer] for n in names]), done

    def from_owner(layer, names):
        return (lambda: _from_owner_comm([finals[n] for n in names], layer),
                lambda cm: finals.update(zip(names, cm.aliased_out)))

    order = [(l, kind) for l in reversed(range(depth)) for kind in ("ffn2", "mix", "ffn1")]
    for (layer, kind), (nxt_layer, nxt_kind) in zip(order[:-1], order[1:]):
        names = SUBLAYER_WEIGHTS[kind]
        if (nxt_layer, nxt_kind) == order[-1]:
            first_w, second_w = SUBLAYER_WEIGHTS[nxt_kind]
            which = int(nxt_kind[-1])
            sched.add(("ffn_dw_out", nxt_layer, which), *to_owner(layer, names))
            sched.add(("ffn_dact", nxt_layer, which), *between_chips(layer, names, names))
            sched.add(("ffn_dact", nxt_layer, which), *to_owner(nxt_layer, [first_w]))
            sched.add(("ffn_dw_in", nxt_layer, which), *from_owner(layer, names))
            sched.add(("ffn_dw_in", nxt_layer, which), *between_chips(nxt_layer, [first_w], [first_w]))
            sched.add(("ffn_dh", nxt_layer, which), *to_owner(nxt_layer, [second_w]))
            sched.add(("ffn_dh", nxt_layer, which), *from_owner(nxt_layer, [first_w]))
        elif nxt_kind == "mix":
            sched.add(("mix_dout", nxt_layer), *to_owner(layer, names))
            sched.add(("attn_bwd", nxt_layer), *between_chips(layer, names, names))
            sched.add(("mix_dh", nxt_layer), *from_owner(layer, names))
        else:
            which = int(nxt_kind[-1])
            sched.add(("ffn_dact", nxt_layer, which), *to_owner(layer, names))
            if kind == "mix":
                sched.add(("ffn_dw_out", nxt_layer, which), *between_chips(layer, names, names))
            else:
                sched.add(("ffn_dw_out", nxt_layer, which), *between_chips(layer, names[:1]))
                sched.add(("ffn_dh", nxt_layer, which), *between_chips(layer, names[1:], names))
            sched.add(("ffn_dw_in", nxt_layer, which), *from_owner(layer, names))

    small_grads = [None] * depth
    for l in reversed(range(depth)):
        s1, s2, s3 = saved[l]
        dx, dxb, dg2 = _ffn_bwd(dx, dxb, s3, norm_ffn2[l][None], wg, 2, l, stacks, sched)
        dx, dxb, gm = _mix_bwd(dx, dxb, s2, norm_mix[l][None], wg, small[l], tables, l, stacks, sched)
        dx, dxb, dg1 = _ffn_bwd(dx, dxb, s1, norm_ffn1[l][None], wg, 1, l, stacks, sched, weights_first=l == 0)
        gm["norm_ffn1"] = dg1[0]
        gm["norm_ffn2"] = dg2[0]
        small_grads[l] = gm
    grad_x = dx[None]
    assert not sched.sites, sched.sites

    per_layer = [n for n in SMALL if n != "final_norm"]
    small_local = [jnp.stack([small_grads[l][n] for l in range(depth)]) for n in per_layer]
    small_local += [d_final[0], loss_part[0, :1]]
    small_shapes = [a.shape for a in small_local]
    summed = _unpack(_all_reduce_small(_pack(small_local)), small_shapes)
    loss = summed[-1][0]
    sg = dict(zip(per_layer + ["final_norm"], summed[:-1]))
    sg["conv_dw_w"] = lax.dynamic_slice_in_dim(sg["conv_dw_w"], chip * LANES, LANES, axis=2)

    delta, new_m, new_v = {}, {}, {}
    shapes = [w[n].shape for n in SMALL]
    packed = [_pack([d[n] for n in SMALL])[None] for d in (w, sg, m, v)]
    last_layer, last_kind = order[-1]
    names = SUBLAYER_WEIGHTS[last_kind][1:]
    make, done = between_chips(last_layer, names, names)
    cm = make()
    outs = _adamw(*packed, comm=cm)
    done(cm)
    for d, buf in zip((delta, new_m, new_v), outs[:3]):
        d.update(zip(SMALL, _unpack(buf[0], shapes)))
    make, done = from_owner(last_layer, names)
    cm = make()
    _standalone("from_owner", cm)
    done(cm)
    big_grads = dict(finals)
    for n in BIG:
        delta[n], new_m[n], new_v[n], big_grads[n] = _adamw(w[n], big_grads[n], m[n], v[n])
    grads = {**big_grads, **sg}
    return (loss, grad_x, *[grads[n] for n in WEIGHTS], *[delta[n] for n in WEIGHTS],
            *[new_m[n] for n in WEIGHTS], *[new_v[n] for n in WEIGHTS])
```

```python
import functools

import jax
import jax.numpy as jnp
from jax import lax
from jax.experimental import pallas as pl
from jax.experimental.pallas import tpu as pltpu

F32 = jnp.float32
BF16 = jnp.bfloat16
MESH_ID = pl.DeviceIdType.MESH

V7X_VMEM_LIMIT_BYTES = 56 * 2**20
LANES = 128
SUBLANES = 8

HEAD_DIM = 64
N_Q_HEADS = 16
N_KV_HEADS = 4
GQ = N_Q_HEADS // N_KV_HEADS
BLK = 128
ROT_HALF = 8
ROPE_THETA = 500000.0
CONV_WIDTH = 31
CONV_PAD = 32
CONV_CH = 512
SGU_CH = 512
SGU_HEADS = 8
Q_END = N_Q_HEADS * HEAD_DIM
K_END = Q_END + N_KV_HEADS * HEAD_DIM
V_END = K_END + N_KV_HEADS * HEAD_DIM
CONV_END = V_END + 2 * CONV_CH
IN_COLS = CONV_END + 2 * SGU_CH
NORM_EPS = 1e-5
FFN_RESIDUAL_WEIGHT = 0.5
N_CHIPS = 4
N_DEV = 8

ADAM_LR = 0.001
ADAM_B1 = 0.9
ADAM_B2 = 0.999
ADAM_EPS = 1e-08
ADAM_WD = 0.01
ADAM_STEP = 10

NN = (((1,), (0,)), ((), ()))
NT = (((1,), (1,)), ((), ()))
TN = (((0,), (0,)), ((), ()))


def _pick(n, cands):
    for c in cands:
        if n % c == 0:
            return c
    raise ValueError(f"no tile of {cands} divides {n}")


def _params(n_axes):
    return pltpu.CompilerParams(dimension_semantics=("arbitrary",) * n_axes, vmem_limit_bytes=V7X_VMEM_LIMIT_BYTES)


def _call(body, **kw):
    return pl.pallas_call(body, **kw)


def _sigmoid(x):
    return 1.0 / (1.0 + jnp.exp(-x))


class _Comm:
    def __init__(self, reads, aliased, fresh, sems, start, finish):
        self.reads, self.aliased, self.fresh, self.sems = list(reads), list(aliased), list(fresh), list(sems)
        self.start, self.finish = start, finish
        self.aliased_out, self.fresh_out = None, None


def _merge_comms(comms):
    comms = [cm for cm in comms if cm is not None]
    if not comms:
        return None
    if len(comms) == 1:
        return comms[0]

    def split(refs, counts):
        out, off = [], 0
        for n in counts:
            out.append(refs[off:off + n])
            off += n
        return out

    def run(which):
        def f(rd, al, fr, sm):
            parts = zip(split(rd, [len(cm.reads) for cm in comms]), split(al, [len(cm.aliased) for cm in comms]),
                        split(fr, [len(cm.fresh) for cm in comms]), split(sm, [len(cm.sems) for cm in comms]))
            for cm, (r, a, f_, s) in zip(comms, parts):
                getattr(cm, which)(r, a, f_, s)
        return f

    merged = _Comm(sum((cm.reads for cm in comms), []), sum((cm.aliased for cm in comms), []),
                   sum((cm.fresh for cm in comms), []), sum((cm.sems for cm in comms), []), run("start"), run("finish"))
    merged.parts = comms
    return merged


def _hosted_call(body, comm, name, grid, inputs, in_specs, out_shape, out_specs, scratch_shapes=(), aliases=None):
    n_in, n_out, n_scr = len(inputs), len(out_shape), len(scratch_shapes)
    aliases = dict(aliases or {})
    if comm is None:
        return _call(body, name=name, grid=grid, in_specs=list(in_specs), out_specs=list(out_specs),
                     out_shape=list(out_shape), scratch_shapes=list(scratch_shapes), input_output_aliases=aliases,
                     compiler_params=_params(len(grid)))(*inputs)
    nr, na, nf = len(comm.reads), len(comm.aliased), len(comm.fresh)

    def full(*refs):
        ins = refs[:n_in]
        rd = refs[n_in:n_in + nr]
        pos = n_in + nr + na
        outs = refs[pos:pos + n_out]
        al = refs[pos + n_out:pos + n_out + na]
        fr = refs[pos + n_out + na:pos + n_out + na + nf]
        pos = pos + n_out + na + nf
        scr = refs[pos:pos + n_scr]
        sems = refs[pos + n_scr:]
        first, last = None, None
        for axis, size in enumerate(grid):
            f, l = pl.program_id(axis) == 0, pl.program_id(axis) == size - 1
            first = f if first is None else jnp.logical_and(first, f)
            last = l if last is None else jnp.logical_and(last, l)

        @pl.when(first)
        def _():
            comm.start(rd, al, fr, sems)

        body(*ins, *outs, *scr)

        @pl.when(last)
        def _():
            comm.finish(rd, al, fr, sems)

    hbm = pl.BlockSpec(memory_space=pltpu.HBM)
    for i in range(na):
        aliases[n_in + nr + i] = n_out + i
    struct = [jax.ShapeDtypeStruct(a.shape, a.dtype) for a in comm.aliased]
    res = _call(full, name=name, grid=grid, in_specs=list(in_specs) + [hbm] * (nr + na),
                out_specs=list(out_specs) + [hbm] * (na + nf), out_shape=list(out_shape) + struct + comm.fresh,
                scratch_shapes=list(scratch_shapes) + comm.sems, input_output_aliases=aliases,
                compiler_params=pltpu.CompilerParams(dimension_semantics=("arbitrary",) * len(grid),
                                                     vmem_limit_bytes=V7X_VMEM_LIMIT_BYTES, has_side_effects=True),
                )(*inputs, *comm.reads, *comm.aliased)
    _deliver(comm, res[n_out:n_out + na], res[n_out + na:])
    return res[:n_out]


def _deliver(comm, aliased_out, fresh_out):
    comm.aliased_out, comm.fresh_out = list(aliased_out), list(fresh_out)
    off_a = off_f = 0
    for part in getattr(comm, "parts", []):
        _deliver(part, aliased_out[off_a:off_a + len(part.aliased)], fresh_out[off_f:off_f + len(part.fresh)])
        off_a += len(part.aliased)
        off_f += len(part.fresh)


def _standalone(name, comm):
    def body(*refs):
        nr, na, nf = len(comm.reads), len(comm.aliased), len(comm.fresh)
        rd, al, fr, sems = refs[:nr], refs[nr + na:nr + 2 * na], refs[nr + 2 * na:nr + 2 * na + nf], refs[nr + 2 * na + nf:]
        comm.start(rd, al, fr, sems)
        comm.finish(rd, al, fr, sems)

    nr, na, nf = len(comm.reads), len(comm.aliased), len(comm.fresh)
    struct = [jax.ShapeDtypeStruct(a.shape, a.dtype) for a in comm.aliased]
    res = _call(body, name=name, in_specs=_hbm_specs(nr + na), out_specs=_hbm_specs(na + nf),
                out_shape=struct + comm.fresh, scratch_shapes=comm.sems,
                input_output_aliases={nr + i: i for i in range(na)},
                compiler_params=_comm_params())(*comm.reads, *comm.aliased)
    _deliver(comm, res[:na], res[na:])


def _matmul(name, grid, a_ops, b_ops, terms, dims, out_shape, out_specs, epilogue, extra_ops=(), nk=1,
            acc_shapes=(), alias=None, comm=None):
    na, nb, ne, no = len(a_ops), len(b_ops), len(extra_ops), len(out_shape)

    def body(*refs):
        a = refs[:na]
        b = refs[na:na + nb]
        e = refs[na + nb:na + nb + ne]
        first_out = na + nb + ne + (1 if alias is not None else 0)
        o = refs[first_out:first_out + no]
        accs = refs[first_out + no:]

        def partial(t):
            tot = None
            for ai, bi in t:
                d = lax.dot_general(a[ai][...], b[bi][...], dims, preferred_element_type=F32)
                tot = d if tot is None else tot + d
            return tot

        if nk == 1:
            epilogue([partial(t) for t in terms], e, o)
        else:
            k = pl.program_id(len(grid) - 1)

            @pl.when(k == 0)
            def _():
                for acc in accs:
                    acc[...] = jnp.zeros(acc.shape, F32)

            for acc, t in zip(accs, terms):
                acc[...] += partial(t)

            @pl.when(k == nk - 1)
            def _():
                epilogue([acc[...] for acc in accs], e, o)

    ops = list(a_ops) + list(b_ops) + list(extra_ops)
    arrays = [x for x, _ in ops]
    in_specs = [s for _, s in ops]
    aliases = {}
    if alias is not None:
        arrays.append(alias[0])
        in_specs.append(pl.BlockSpec(memory_space=pl.ANY))
        aliases[len(arrays) - 1] = alias[1]
    scratch = [pltpu.VMEM(s, F32) for s in acc_shapes] if nk > 1 else []
    return _hosted_call(body, comm, name, grid, arrays, in_specs, out_shape, out_specs, scratch, aliases)


def _mm_ffn_in(h, w_g, layer, comm=None):
    t_len, d = h.shape
    fs = w_g.shape[3]
    f = 2 * fs
    tm = _pick(t_len, (2048, 1024, 512))
    tn = _pick(fs, (256, 128))
    nj = fs // tn

    def epilogue(accs, e, o):
        g, u = accs
        o[0][0] = g.astype(BF16)
        o[0][1] = u.astype(BF16)
        o[1][...] = (g * _sigmoid(g) * u).astype(BF16)

    return _matmul(
        "ffn_in", (t_len // tm, 2, nj),
        [(h, pl.BlockSpec((tm, d), lambda i, s, j: (i, 0)))],
        [(w_g, pl.BlockSpec((None, None, d, tn), lambda i, s, j: (layer, s, 0, j))),
         (w_g, pl.BlockSpec((None, None, d, tn), lambda i, s, j: (layer, s + 2, 0, j)))],
        [[(0, 0)], [(0, 1)]], NN,
        [jax.ShapeDtypeStruct((2, t_len, f), BF16), jax.ShapeDtypeStruct((t_len, f), BF16)],
        [pl.BlockSpec((2, tm, tn), lambda i, s, j: (0, i, s * nj + j)),
         pl.BlockSpec((tm, tn), lambda i, s, j: (i, s * nj + j))],
        epilogue, comm=comm)


def _mm_out_res(name, a, w_g, layer, x, scale, comm=None):
    t_len = a.shape[0]
    ks, n = w_g.shape[2], w_g.shape[3]
    tm = _pick(t_len, (1024, 512))
    tn = _pick(n, (1024,))
    tk = _pick(ks, (1408, 512, 256))
    nks = ks // tk

    def epilogue(accs, e, o):
        o[0][...] = e[0][...] + scale * accs[0]

    return _matmul(
        name, (t_len // tm, n // tn, N_CHIPS * nks),
        [(a, pl.BlockSpec((tm, tk), lambda i, j, k: (i, k)))],
        [(w_g, pl.BlockSpec((None, None, tk, tn), lambda i, j, k: (layer, k // nks, k % nks, j)))],
        [[(0, 0)]], NN,
        [jax.ShapeDtypeStruct((t_len, n), F32)],
        [pl.BlockSpec((tm, tn), lambda i, j, k: (i, j))],
        epilogue, extra_ops=[(x, pl.BlockSpec((tm, tn), lambda i, j, k: (i, j)))],
        nk=N_CHIPS * nks, acc_shapes=[(tm, tn)], comm=comm)


def _mm_proj(h, w_g, layer, comm=None):
    t_len, d = h.shape
    cs = w_g.shape[3]
    tm = _pick(t_len, (1024, 512))

    def epilogue(accs, e, o):
        o[0][...] = accs[0]

    return _matmul(
        "mix_in", (t_len // tm, N_CHIPS),
        [(h, pl.BlockSpec((tm, d), lambda i, s: (i, 0)))],
        [(w_g, pl.BlockSpec((None, None, d, cs), lambda i, s: (layer, s, 0, 0)))],
        [[(0, 0)]], NN,
        [jax.ShapeDtypeStruct((t_len, N_CHIPS * cs), F32)],
        [pl.BlockSpec((tm, cs), lambda i, s: (i, s))],
        epilogue, comm=comm)


def _mm_dact_swiglu(dxb, w_g, layer, gu, scale, comm=None):
    t_len, d = dxb.shape
    rs = w_g.shape[2]
    tm = _pick(t_len, (512,))
    tn = _pick(rs, (1408, 256, 128))
    nj = rs // tn

    def epilogue(accs, e, o):
        dact = scale * accs[0]
        g = e[0][0].astype(F32)
        u = e[0][1].astype(F32)
        sig = _sigmoid(g)
        o[0][0] = (dact * u * (sig * (1.0 + g * (1.0 - sig)))).astype(BF16)
        o[0][1] = (dact * (g * sig)).astype(BF16)

    gu_spec = pl.BlockSpec((2, tm, tn), lambda i, s, j: (0, i, s * nj + j))
    return _matmul(
        "ffn_dact", (t_len // tm, N_CHIPS, nj),
        [(dxb, pl.BlockSpec((tm, d), lambda i, s, j: (i, 0)))],
        [(w_g, pl.BlockSpec((None, None, tn, d), lambda i, s, j: (layer, s, j, 0)))],
        [[(0, 0)]], NT,
        [jax.ShapeDtypeStruct(gu.shape, BF16)], [gu_spec],
        epilogue, extra_ops=[(gu, gu_spec)], comm=comm)[0]


def _mm_dh_ffn(dgu, w_g, layer, comm=None):
    t_len = dgu.shape[1]
    d, fs = w_g.shape[2], w_g.shape[3]
    tm = _pick(t_len, (1024, 512))
    tk = _pick(fs, (256, 128))
    nks = fs // tk
    nk = 2 * nks

    def epilogue(accs, e, o):
        o[0][...] = accs[0]

    return _matmul(
        "ffn_dh", (t_len // tm, nk),
        [(dgu, pl.BlockSpec((None, tm, tk), lambda i, k: (0, i, k))),
         (dgu, pl.BlockSpec((None, tm, tk), lambda i, k: (1, i, k)))],
        [(w_g, pl.BlockSpec((None, None, d, tk), lambda i, k: (layer, k // nks, 0, k % nks))),
         (w_g, pl.BlockSpec((None, None, d, tk), lambda i, k: (layer, k // nks + 2, 0, k % nks)))],
        [[(0, 0), (1, 1)]], NT,
        [jax.ShapeDtypeStruct((t_len, d), F32)],
        [pl.BlockSpec((tm, d), lambda i, k: (i, 0))],
        epilogue, nk=nk, acc_shapes=[(tm, d)], comm=comm)[0]


def _mm_dw(name, a, a_spec_of, b, b_spec_of, layer, stack, rows, cols, tn, scale, comm=None):
    t_len = a.shape[0]
    tt = _pick(t_len, (1024, 512))
    nj = cols // tn

    def epilogue(accs, e, o):
        o[0][...] = (scale * accs[0]).astype(BF16)

    shape = jax.ShapeDtypeStruct((2, N_CHIPS, rows, cols), BF16)
    return _matmul(
        name, (N_CHIPS, nj, t_len // tt),
        [(a, a_spec_of(tt))], [(b, b_spec_of(tt, tn, nj))],
        [[(0, 0)]], TN, [shape],
        [pl.BlockSpec((None, None, rows, tn), lambda s, j, t: (layer, s, 0, j))],
        epilogue, nk=t_len // tt, acc_shapes=[(rows, tn)],
        alias=None if stack is None else (stack, 0), comm=comm)[0]


def _dw_ffn_in(h, dgu, layer, stack, comm=None):
    d = h.shape[1]
    fs = dgu.shape[2] // 2
    tn = _pick(fs, (1408, 256))
    return _mm_dw(
        "ffn_dw_in", h, lambda tt: pl.BlockSpec((tt, d), lambda s, j, t: (t, 0)),
        dgu, lambda tt, tn_, nj: pl.BlockSpec((None, tt, tn_), lambda s, j, t: (s // 2, t, (s % 2) * nj + j)),
        layer, stack, d, fs, tn, 1.0, comm)


def _dw_rows(name, a, dxb, layer, stack, scale, comm=None):
    rs = a.shape[1] // N_CHIPS
    d = dxb.shape[1]
    tn = _pick(d, (1024,))
    return _mm_dw(
        name, a, lambda tt: pl.BlockSpec((tt, rs), lambda s, j, t: (t, s)),
        dxb, lambda tt, tn_, nj: pl.BlockSpec((tt, tn_), lambda s, j, t: (t, j)),
        layer, stack, rs, d, tn, scale, comm)


def _dw_mix_in(h, dp, layer, stack, comm=None):
    d = h.shape[1]
    cs = dp.shape[1] // N_CHIPS
    return _mm_dw(
        "mix_dw_in", h, lambda tt: pl.BlockSpec((tt, d), lambda s, j, t: (t, 0)),
        dp, lambda tt, tn_, nj: pl.BlockSpec((tt, tn_), lambda s, j, t: (t, s)),
        layer, stack, d, cs, cs, 1.0, comm)


def _mm_dmix(dxb, w_g, layer, comm=None):
    t_len, d = dxb.shape
    rs = w_g.shape[2]
    tm = _pick(t_len, (1024, 512))

    def epilogue(accs, e, o):
        o[0][...] = accs[0]

    return _matmul(
        "mix_dout", (t_len // tm, N_CHIPS),
        [(dxb, pl.BlockSpec((tm, d), lambda i, s: (i, 0)))],
        [(w_g, pl.BlockSpec((None, None, rs, d), lambda i, s: (layer, s, 0, 0)))],
        [[(0, 0)]], NT,
        [jax.ShapeDtypeStruct((t_len, N_CHIPS * rs), F32)],
        [pl.BlockSpec((tm, rs), lambda i, s: (i, s))],
        epilogue, comm=comm)[0]


def _mm_dh_mix(dp, w_g, layer, comm=None):
    t_len = dp.shape[0]
    d, cs = w_g.shape[2], w_g.shape[3]
    tm = _pick(t_len, (1024, 512))

    def epilogue(accs, e, o):
        o[0][...] = accs[0]

    return _matmul(
        "mix_dh", (t_len // tm, N_CHIPS),
        [(dp, pl.BlockSpec((tm, cs), lambda i, k: (i, k)))],
        [(w_g, pl.BlockSpec((None, None, d, cs), lambda i, k: (layer, k, 0, 0)))],
        [[(0, 0)]], NT,
        [jax.ShapeDtypeStruct((t_len, d), F32)],
        [pl.BlockSpec((tm, d), lambda i, k: (i, 0))],
        epilogue, nk=N_CHIPS, acc_shapes=[(tm, d)], comm=comm)[0]


def _rms_stats(x):
    r = lax.rsqrt(jnp.mean(x * x, axis=-1, keepdims=True) + NORM_EPS)
    return r, x * r


def _accumulate(ref, part, first):
    @pl.when(first)
    def _():
        ref[...] = part

    @pl.when(jnp.logical_not(first))
    def _():
        ref[...] += part


def _rmsnorm_fwd(x, g):
    t_len, d = x.shape
    tm = _pick(t_len, (512,))

    def body(x_ref, g_ref, h_ref):
        _, xhat = _rms_stats(x_ref[...])
        h_ref[...] = (xhat * g_ref[...]).astype(BF16)

    row = pl.BlockSpec((tm, d), lambda i: (i, 0))
    vec = pl.BlockSpec((1, d), lambda i: (0, 0))
    return _call(body, name="rmsnorm_fwd", grid=(t_len // tm,), in_specs=[row, vec], out_specs=row,
                 out_shape=jax.ShapeDtypeStruct((t_len, d), BF16), compiler_params=_params(1))(x, g)


def _rmsnorm_bwd(dh, x, g, dres):
    t_len, d = x.shape
    tm = _pick(t_len, (256,))

    def body(dh_ref, x_ref, g_ref, dres_ref, dx_ref, dxb_ref, dg_ref):
        r, xhat = _rms_stats(x_ref[...])
        dh_v = dh_ref[...]
        gd = dh_v * g_ref[...]
        dx = dres_ref[...] + r * (gd - xhat * jnp.mean(gd * xhat, axis=-1, keepdims=True))
        dx_ref[...] = dx
        dxb_ref[...] = dx.astype(BF16)
        _accumulate(dg_ref, jnp.sum(dh_v * xhat, axis=0, keepdims=True), pl.program_id(0) == 0)

    row = pl.BlockSpec((tm, d), lambda i: (i, 0))
    vec = pl.BlockSpec((1, d), lambda i: (0, 0))
    return _call(body, name="rmsnorm_bwd", grid=(t_len // tm,), in_specs=[row, row, vec, row],
                 out_specs=[row, row, vec],
                 out_shape=[jax.ShapeDtypeStruct((t_len, d), F32), jax.ShapeDtypeStruct((t_len, d), BF16),
                            jax.ShapeDtypeStruct((1, d), F32)],
                 compiler_params=_params(1))(dh, x, g, dres)


def _loss_head(x, g, target):
    t_len, d = x.shape
    tm = _pick(t_len, (256,))

    def body(x_ref, g_ref, t_ref, dx_ref, dxb_ref, dg_ref, loss_ref):
        first = pl.program_id(0) == 0
        r, xhat = _rms_stats(x_ref[...])
        g_v = g_ref[...]
        err = xhat * g_v - t_ref[...]
        per_token = jnp.mean(err * err, axis=-1, keepdims=True)
        part = 0.5 * jnp.sum(per_token, axis=0, keepdims=True)
        _accumulate(loss_ref, jnp.broadcast_to(part, (1, LANES)), first)
        dy = err * (1.0 / d)
        _accumulate(dg_ref, jnp.sum(dy * xhat, axis=0, keepdims=True), first)
        gd = dy * g_v
        dx = r * (gd - xhat * jnp.mean(gd * xhat, axis=-1, keepdims=True))
        dx_ref[...] = dx
        dxb_ref[...] = dx.astype(BF16)

    row = pl.BlockSpec((tm, d), lambda i: (i, 0))
    vec = pl.BlockSpec((1, d), lambda i: (0, 0))
    return _call(body, name="loss_head", grid=(t_len // tm,), in_specs=[row, vec, row],
                 out_specs=[row, row, vec, pl.BlockSpec((1, LANES), lambda i: (0, 0))],
                 out_shape=[jax.ShapeDtypeStruct((t_len, d), F32), jax.ShapeDtypeStruct((t_len, d), BF16),
                            jax.ShapeDtypeStruct((1, d), F32), jax.ShapeDtypeStruct((1, LANES), F32)],
                 compiler_params=_params(1))(x, g, target)


def _ln_stats(x):
    mu = jnp.mean(x, axis=-1, keepdims=True)
    xc = x - mu
    r = lax.rsqrt(jnp.mean(xc * xc, axis=-1, keepdims=True) + NORM_EPS)
    return r, xc * r


def _ln_bwd(dy, r, xhat, g):
    dxh = dy * g
    return r * (dxh - jnp.mean(dxh, axis=-1, keepdims=True) - xhat * jnp.mean(dxh * xhat, axis=-1, keepdims=True))


def _rope_tables(positions):
    t_len = positions.shape[-1]
    inv_freq = 1.0 / (ROPE_THETA ** (jnp.arange(0, 2 * ROT_HALF, 2, dtype=F32) / (2 * ROT_HALF)))
    ang = positions.astype(F32).reshape(t_len, 1) * inv_freq
    cos = jnp.tile(jnp.cos(ang), (1, LANES // ROT_HALF))
    sin = jnp.tile(jnp.sin(ang), (1, LANES // ROT_HALF))
    lane = jnp.arange(LANES) % HEAD_DIM
    c = jnp.where(lane < 2 * ROT_HALF, cos, 1.0)
    s1 = jnp.where(lane < ROT_HALF, -sin, 0.0)
    s2 = jnp.where((lane >= ROT_HALF) & (lane < 2 * ROT_HALF), sin, 0.0)
    return c.astype(F32), s1.astype(F32), s2.astype(F32)


def _rope_fwd(p, tables):
    t_len = p.shape[0]
    tm = _pick(t_len, (256,))
    n_rot = K_END // LANES

    def body(p_ref, c_ref, s1_ref, s2_ref, o_ref):
        c, s1, s2 = c_ref[...], s1_ref[...], s2_ref[...]
        for j in range(V_END // LANES):
            sl = slice(j * LANES, (j + 1) * LANES)
            t = p_ref[:, sl]
            if j < n_rot:
                t = t * c + pltpu.roll(t, LANES - ROT_HALF, 1) * s1 + pltpu.roll(t, ROT_HALF, 1) * s2
            if j < Q_END // LANES:
                t = t * ATTN_SCALE
            o_ref[:, sl] = t.astype(BF16)

    tab = pl.BlockSpec((tm, LANES), lambda i: (i, 0))
    blk = pl.BlockSpec((tm, V_END), lambda i: (i, 0))
    return _call(body, name="rope_fwd", grid=(t_len // tm,), in_specs=[blk, tab, tab, tab], out_specs=blk,
                 out_shape=jax.ShapeDtypeStruct((t_len, V_END), BF16), compiler_params=_params(1))(p, *tables)


def _assemble_dp(dq, dkc, dkp, dvc, dvp, tables, dalin, dagate, du, dvin):
    t_len = dq.shape[0]
    steps = t_len // ATTN_STEP

    def body(dq_ref, dkc_ref, dkp_ref, dvc_ref, dvp_ref, c_ref, s1_ref, s2_ref, dalin_ref, dagate_ref,
             du_ref, dvin_ref, o_ref):
        keep = (pl.program_id(0) < steps - 1).astype(F32)
        for part in range(ATTN_STEP // BLK):
            rows = slice(part * BLK, (part + 1) * BLK)
            c, s1, s2 = c_ref[rows, :], s1_ref[rows, :], s2_ref[rows, :]

            def unrotate(dr):
                return dr * c + pltpu.roll(dr * s1, ROT_HALF, 1) + pltpu.roll(dr * s2, LANES - ROT_HALF, 1)

            for j in range(Q_END // LANES):
                sl = slice(j * LANES, (j + 1) * LANES)
                o_ref[rows, sl] = unrotate(dq_ref[rows, sl]).astype(BF16)
            for j in range((K_END - Q_END) // LANES):
                sl = slice(j * LANES, (j + 1) * LANES)
                dk, dv = dkc_ref[rows, sl], dvc_ref[rows, sl]
                if part == ATTN_STEP // BLK - 1:
                    dk = dk + keep * dkp_ref[:, sl]
                    dv = dv + keep * dvp_ref[:, sl]
                o_ref[rows, Q_END + j * LANES:Q_END + (j + 1) * LANES] = unrotate(dk).astype(BF16)
                o_ref[rows, K_END + j * LANES:K_END + (j + 1) * LANES] = dv.astype(BF16)
        o_ref[:, V_END:V_END + CONV_CH] = dalin_ref[...]
        o_ref[:, V_END + CONV_CH:CONV_END] = dagate_ref[...]
        o_ref[:, CONV_END:CONV_END + SGU_CH] = du_ref[...]
        o_ref[:, CONV_END + SGU_CH:IN_COLS] = dvin_ref[...]

    def cur(w):
        return pl.BlockSpec((ATTN_STEP, w), lambda i: (i, 0))

    def nxt(w):
        return pl.BlockSpec((BLK, w), lambda i: (jnp.minimum(i + 1, steps - 1), 0))

    kvw = K_END - Q_END
    return _call(body, name="assemble_dp", grid=(steps,),
                 in_specs=[cur(Q_END), cur(kvw), nxt(kvw), cur(kvw), nxt(kvw), cur(LANES), cur(LANES), cur(LANES),
                           cur(CONV_CH), cur(CONV_CH), cur(SGU_CH), cur(SGU_CH)],
                 out_specs=cur(IN_COLS), out_shape=jax.ShapeDtypeStruct((t_len, IN_COLS), BF16),
                 compiler_params=_params(1))(dq, dkc, dkp, dvc, dvp, *tables, dalin, dagate, du, dvin)


ATTN_STEP = 2 * BLK
ATTN_SCALE = HEAD_DIM ** -0.5
MASKED = -1e30


def _attn_bias():
    qi = jnp.arange(GQ * BLK)[:, None] % BLK
    kj = jnp.arange(2 * BLK)[None, :]
    dist = qi + BLK - kj
    band = (dist >= 0) & (dist < BLK)
    return jnp.stack([jnp.where(band & (kj >= BLK), 0.0, MASKED), jnp.where(band, 0.0, MASKED)]).astype(F32)


def _attn_chains(q_ref, kp_ref, kc_ref, vp_ref, vc_ref):
    kc, vc = kc_ref[...], vc_ref[...]
    rows_a, rows_b = slice(0, BLK), slice(BLK, ATTN_STEP)
    return [
        (rows_a, q_ref[:, rows_a, :].reshape(GQ * BLK, HEAD_DIM), jnp.concatenate([kp_ref[...], kc[:BLK]], axis=0),
         jnp.concatenate([vp_ref[...], vc[:BLK]], axis=0)),
        (rows_b, q_ref[:, rows_b, :].reshape(GQ * BLK, HEAD_DIM), kc, vc),
    ]


def _attn_weights(q, kk, bias, snk):
    s = lax.dot_general(q, kk, NT, preferred_element_type=F32) + bias
    m = jnp.maximum(jnp.max(s, axis=-1, keepdims=True), snk)
    e = jnp.exp(s - m)
    es = jnp.exp(snk - m)
    return e, es, 1.0 / (jnp.sum(e, axis=-1, keepdims=True) + es)


def _attn_specs(steps):
    q_spec = pl.BlockSpec((GQ, ATTN_STEP, HEAD_DIM), lambda g, n: (g, n, 0))
    cur = pl.BlockSpec((None, ATTN_STEP, HEAD_DIM), lambda g, n: (g, n, 0))
    prev = pl.BlockSpec((None, BLK, HEAD_DIM), lambda g, n: (g, jnp.maximum(2 * n - 1, 0), 0))
    snk = pl.BlockSpec((None, GQ * BLK, 1), lambda g, n: (g, 0, 0))
    bias_a = pl.BlockSpec((None, GQ * BLK, 2 * BLK), lambda g, n: (jnp.minimum(n, 1), 0, 0))
    bias_b = pl.BlockSpec((None, GQ * BLK, 2 * BLK), lambda g, n: (1, 0, 0))
    return q_spec, cur, prev, snk, bias_a, bias_b


def _attn_fwd(q, k, v, snk, comm=None):
    t_len = q.shape[1]
    steps = t_len // ATTN_STEP

    def body(q_ref, kp_ref, kc_ref, vp_ref, vc_ref, snk_ref, ba_ref, bb_ref, o_ref):
        snk = snk_ref[...]
        for (rows, q_v, kk, vv), b_ref in zip(_attn_chains(q_ref, kp_ref, kc_ref, vp_ref, vc_ref), (ba_ref, bb_ref)):
            e, _, inv = _attn_weights(q_v, kk, b_ref[...], snk)
            o = lax.dot_general(e.astype(BF16), vv, NN, preferred_element_type=F32) * inv
            o_ref[:, rows, :] = o.reshape(GQ, BLK, HEAD_DIM).astype(BF16)

    q_spec, cur, prev, snk_spec, bias_a, bias_b = _attn_specs(steps)
    bias = _attn_bias()
    return _hosted_call(body, comm, "attn_fwd", (N_KV_HEADS, steps), [q, k, k, v, v, snk, bias, bias],
                        [q_spec, prev, cur, prev, cur, snk_spec, bias_a, bias_b],
                        [jax.ShapeDtypeStruct(q.shape, BF16)], [q_spec])[0]


def _attn_bwd(q, k, v, snk, do, comm=None):
    t_len = q.shape[1]
    steps = t_len // ATTN_STEP

    def body(q_ref, kp_ref, kc_ref, vp_ref, vc_ref, snk_ref, ba_ref, bb_ref, do_ref, dq_ref, dkp_ref, dkc_ref,
             dvp_ref, dvc_ref, dsnk_ref):
        snk = snk_ref[...]
        row = lax.broadcasted_iota(jnp.int32, (SUBLANES, LANES), 0)
        tile = jnp.zeros((SUBLANES, LANES), F32)
        grads = []
        for (rows, q_v, kk, vv), b_ref in zip(_attn_chains(q_ref, kp_ref, kc_ref, vp_ref, vc_ref), (ba_ref, bb_ref)):
            e, es, inv = _attn_weights(q_v, kk, b_ref[...], snk)
            p = e * inv
            do_v = do_ref[:, rows, :].reshape(GQ * BLK, HEAD_DIM).astype(BF16)
            dp = lax.dot_general(do_v, vv, NT, preferred_element_type=F32)
            delta = jnp.sum(p * dp, axis=-1, keepdims=True)
            ds = (p * (dp - delta)).astype(BF16)
            dq = lax.dot_general(ds, kk, NN, preferred_element_type=F32) * ATTN_SCALE
            dq_ref[:, rows, :] = dq.reshape(GQ, BLK, HEAD_DIM)
            grads.append((lax.dot_general(ds, q_v, TN, preferred_element_type=F32),
                          lax.dot_general(p.astype(BF16), do_v, TN, preferred_element_type=F32)))
            per_row = -(es * inv) * delta
            for hh in range(GQ):
                tot = jnp.sum(per_row[hh * BLK:(hh + 1) * BLK], axis=0, keepdims=True)
                tile = tile + jnp.where(row == hh, tot, 0.0)
        (dk_a, dv_a), (dk_b, dv_b) = grads
        dkp_ref[...] = dk_a[:BLK]
        dvp_ref[...] = dv_a[:BLK]
        dkc_ref[0:BLK, :] = dk_a[BLK:] + dk_b[:BLK]
        dvc_ref[0:BLK, :] = dv_a[BLK:] + dv_b[:BLK]
        dkc_ref[BLK:ATTN_STEP, :] = dk_b[BLK:]
        dvc_ref[BLK:ATTN_STEP, :] = dv_b[BLK:]
        _accumulate(dsnk_ref, tile, pl.program_id(1) == 0)

    q_spec, cur, prev, snk_spec, bias_a, bias_b = _attn_specs(steps)
    bias = _attn_bias()
    step_blk = pl.BlockSpec((None, BLK, HEAD_DIM), lambda g, n: (g, n, 0))
    kv_shape = jax.ShapeDtypeStruct(k.shape, F32)
    prev_shape = jax.ShapeDtypeStruct((N_KV_HEADS, steps * BLK, HEAD_DIM), F32)
    return _hosted_call(
        body, comm, "attn_bwd", (N_KV_HEADS, steps), [q, k, k, v, v, snk, bias, bias, do],
        [q_spec, prev, cur, prev, cur, snk_spec, bias_a, bias_b, q_spec],
        [jax.ShapeDtypeStruct(q.shape, F32), prev_shape, kv_shape, prev_shape, kv_shape,
         jax.ShapeDtypeStruct((N_KV_HEADS, SUBLANES, LANES), F32)],
        [q_spec, step_blk, cur, step_blk, cur, pl.BlockSpec((None, SUBLANES, LANES), lambda g, n: (g, 0, 0))])


CONV_CHUNK = 256


def _shift_up(win, s):
    n = win.shape[0]
    return win if s == 0 else pltpu.roll(win, n - s, 0)


def _conv_col_specs(t_len):
    lin = pl.BlockSpec((t_len, LANES), lambda j: (0, V_END // LANES + j))
    gate = pl.BlockSpec((t_len, LANES), lambda j: (0, (V_END + CONV_CH) // LANES + j))
    col = pl.BlockSpec((t_len, LANES), lambda j: (0, j))
    wsp = pl.BlockSpec((CONV_PAD, LANES), lambda j: (0, j))
    return lin, gate, col, wsp


def _conv_fwd(p, w, b):
    t_len = p.shape[0]
    ch = CONV_CHUNK

    def body(lin_ref, gate_ref, w_ref, b_ref, y_ref, hp_ref):
        hp_ref[0:CONV_PAD, :] = jnp.zeros((CONV_PAD, LANES), F32)

        def fill(c, carry):
            r0 = pl.multiple_of(c * ch, ch)
            hp_ref[pl.ds(r0 + CONV_PAD, ch), :] = lin_ref[pl.ds(r0, ch), :] * _sigmoid(gate_ref[pl.ds(r0, ch), :])
            return carry

        lax.fori_loop(0, t_len // ch, fill, 0)

        def conv(c, carry):
            r0 = pl.multiple_of(c * ch, ch)
            win = hp_ref[pl.ds(r0, ch + CONV_PAD), :]
            acc = jnp.zeros((ch, LANES), F32)
            for k in range(CONV_WIDTH):
                acc = acc + _shift_up(win, CONV_PAD - (CONV_WIDTH - 1) + k)[:ch] * w_ref[k:k + 1, :]
            y_ref[pl.ds(r0, ch), :] = acc + b_ref[...]
            return carry

        lax.fori_loop(0, t_len // ch, conv, 0)

    lin, gate, col, wsp = _conv_col_specs(t_len)
    return _call(body, name="conv_fwd", grid=(CONV_CH // LANES,),
                 in_specs=[lin, gate, wsp, pl.BlockSpec((1, LANES), lambda j: (0, j))], out_specs=col,
                 out_shape=jax.ShapeDtypeStruct((t_len, CONV_CH), F32),
                 scratch_shapes=[pltpu.VMEM((t_len + CONV_PAD, LANES), F32)],
                 compiler_params=_params(1))(p, p, w, b)


def _conv_post_fwd(y, g, b):
    t_len = y.shape[0]
    tm = _pick(t_len, (512,))

    def body(y_ref, g_ref, b_ref, o_ref):
        _, xhat = _ln_stats(y_ref[...])
        z = xhat * g_ref[...] + b_ref[...]
        o_ref[...] = (z * _sigmoid(z)).astype(BF16)

    row = pl.BlockSpec((tm, CONV_CH), lambda i: (i, 0))
    vec = pl.BlockSpec((1, CONV_CH), lambda i: (0, 0))
    return _call(body, name="conv_post_fwd", grid=(t_len // tm,), in_specs=[row, vec, vec], out_specs=row,
                 out_shape=jax.ShapeDtypeStruct((t_len, CONV_CH), BF16), compiler_params=_params(1))(y, g, b)


def _conv_post_bwd(dmix, y, g, b):
    t_len = y.shape[0]
    tm = _pick(t_len, (512,))

    def body(do_ref, y_ref, g_ref, b_ref, dy_ref, dg_ref, db_ref, dcb_ref):
        first = pl.program_id(0) == 0
        r, xhat = _ln_stats(y_ref[...])
        g_v = g_ref[...]
        z = xhat * g_v + b_ref[...]
        sig = _sigmoid(z)
        dz = do_ref[...] * (sig * (1.0 + z * (1.0 - sig)))
        _accumulate(db_ref, jnp.sum(dz, axis=0, keepdims=True), first)
        _accumulate(dg_ref, jnp.sum(dz * xhat, axis=0, keepdims=True), first)
        dy = _ln_bwd(dz, r, xhat, g_v)
        dy_ref[...] = dy
        _accumulate(dcb_ref, jnp.sum(dy, axis=0, keepdims=True), first)

    row = pl.BlockSpec((tm, CONV_CH), lambda i: (i, 0))
    do_spec = pl.BlockSpec((tm, CONV_CH), lambda i: (i, Q_END // CONV_CH))
    vec = pl.BlockSpec((1, CONV_CH), lambda i: (0, 0))
    vshape = jax.ShapeDtypeStruct((1, CONV_CH), F32)
    return _call(body, name="conv_post_bwd", grid=(t_len // tm,), in_specs=[do_spec, row, vec, vec],
                 out_specs=[row, vec, vec, vec],
                 out_shape=[jax.ShapeDtypeStruct((t_len, CONV_CH), F32), vshape, vshape, vshape],
                 compiler_params=_params(1))(dmix, y, g, b)


def _conv_bwd(p, dy, w):
    t_len = p.shape[0]
    ch = CONV_CHUNK

    def body(lin_ref, gate_ref, dy_ref, w_ref, dlin_ref, dgate_ref, dw_ref, hp_ref, dyp_ref):
        hp_ref[0:CONV_PAD, :] = jnp.zeros((CONV_PAD, LANES), F32)
        dyp_ref[t_len:t_len + CONV_PAD, :] = jnp.zeros((CONV_PAD, LANES), F32)
        dw_ref[...] = jnp.zeros((CONV_PAD, LANES), F32)

        def fill(c, carry):
            r0 = pl.multiple_of(c * ch, ch)
            hp_ref[pl.ds(r0 + CONV_PAD, ch), :] = lin_ref[pl.ds(r0, ch), :] * _sigmoid(gate_ref[pl.ds(r0, ch), :])
            dyp_ref[pl.ds(r0, ch), :] = dy_ref[pl.ds(r0, ch), :]
            return carry

        lax.fori_loop(0, t_len // ch, fill, 0)

        def step(c, carry):
            r0 = pl.multiple_of(c * ch, ch)
            win_h = hp_ref[pl.ds(r0, ch + CONV_PAD), :]
            win_dy = dyp_ref[pl.ds(r0, ch + CONV_PAD), :]
            dyc = win_dy[:ch]
            dh = jnp.zeros((ch, LANES), F32)
            for k in range(CONV_WIDTH):
                tap = _shift_up(win_h, CONV_PAD - (CONV_WIDTH - 1) + k)[:ch]
                dw_ref[k:k + 1, :] += jnp.sum(dyc * tap, axis=0, keepdims=True)
                dh = dh + _shift_up(win_dy, CONV_WIDTH - 1 - k)[:ch] * w_ref[k:k + 1, :]
            lin = lin_ref[pl.ds(r0, ch), :]
            sig = _sigmoid(gate_ref[pl.ds(r0, ch), :])
            dlin_ref[pl.ds(r0, ch), :] = (dh * sig).astype(BF16)
            dgate_ref[pl.ds(r0, ch), :] = (dh * lin * (sig * (1.0 - sig))).astype(BF16)
            return carry

        lax.fori_loop(0, t_len // ch, step, 0)

    lin, gate, col, wsp = _conv_col_specs(t_len)
    half = jax.ShapeDtypeStruct((t_len, CONV_CH), BF16)
    return _call(body, name="conv_bwd", grid=(CONV_CH // LANES,), in_specs=[lin, gate, col, wsp],
                 out_specs=[col, col, wsp],
                 out_shape=[half, half, jax.ShapeDtypeStruct((CONV_PAD, CONV_CH), F32)],
                 scratch_shapes=[pltpu.VMEM((t_len + CONV_PAD, LANES), F32), pltpu.VMEM((t_len + CONV_PAD, LANES), F32)],
                 compiler_params=_params(1))(p, p, dy, w)


def _sgu_mixed(v, w_ref, bt_ref, j):
    lane = lax.broadcasted_iota(jnp.int32, (BLK, LANES), 1)
    lo = lane < HEAD_DIM
    tri = lax.broadcasted_iota(jnp.int32, (BLK, BLK), 0) >= lax.broadcasted_iota(jnp.int32, (BLK, BLK), 1)
    vs = v[:, j * LANES:(j + 1) * LANES]
    v_lo = jnp.where(lo, vs, 0.0).astype(BF16)
    v_hi = jnp.where(lo, 0.0, vs).astype(BF16)
    w_lo = jnp.where(tri, w_ref[2 * j], 0.0).astype(BF16)
    w_hi = jnp.where(tri, w_ref[2 * j + 1], 0.0).astype(BF16)
    m = (lax.dot_general(w_lo, v_lo, NN, preferred_element_type=F32)
         + lax.dot_general(w_hi, v_hi, NN, preferred_element_type=F32))
    bias = jnp.where(lo, bt_ref[:, 2 * j:2 * j + 1], bt_ref[:, 2 * j + 1:2 * j + 2])
    return m + bias, (v_lo, v_hi, w_lo, w_hi, lo, tri)


def _sgu_specs():
    u_spec = pl.BlockSpec((BLK, SGU_CH), lambda i: (i, CONV_END // SGU_CH))
    v_spec = pl.BlockSpec((BLK, SGU_CH), lambda i: (i, CONV_END // SGU_CH + 1))
    vec = pl.BlockSpec((1, SGU_CH), lambda i: (0, 0))
    w_spec = pl.BlockSpec((SGU_HEADS, BLK, BLK), lambda i: (0, 0, 0))
    bt_spec = pl.BlockSpec((BLK, SGU_HEADS), lambda i: (0, 0))
    row = pl.BlockSpec((BLK, SGU_CH), lambda i: (i, 0))
    return u_spec, v_spec, vec, w_spec, bt_spec, row


def _sgu_fwd(p, g, b, w, bt):
    t_len = p.shape[0]

    def body(u_ref, vin_ref, g_ref, b_ref, w_ref, bt_ref, o_ref):
        _, xhat = _ln_stats(vin_ref[...])
        v = xhat * g_ref[...] + b_ref[...]
        for j in range(SGU_CH // LANES):
            m, _ = _sgu_mixed(v, w_ref, bt_ref, j)
            sl = slice(j * LANES, (j + 1) * LANES)
            o_ref[:, sl] = (u_ref[:, sl] * m).astype(BF16)

    u_spec, v_spec, vec, w_spec, bt_spec, row = _sgu_specs()
    return _call(body, name="sgu_fwd", grid=(t_len // BLK,), in_specs=[u_spec, v_spec, vec, vec, w_spec, bt_spec],
                 out_specs=row, out_shape=jax.ShapeDtypeStruct((t_len, SGU_CH), BF16),
                 compiler_params=_params(1))(p, p, g, b, w, bt)


def _sgu_bwd(p, dmix, g, b, w, bt):
    t_len = p.shape[0]

    def body(u_ref, vin_ref, do_ref, g_ref, b_ref, w_ref, bt_ref, du_ref, dvin_ref, dw_ref, dbt_ref, dg_ref,
             db_ref, dv_ref):
        first = pl.program_id(0) == 0
        r, xhat = _ln_stats(vin_ref[...])
        g_v = g_ref[...]
        v = xhat * g_v + b_ref[...]
        lane = lax.broadcasted_iota(jnp.int32, (BLK, LANES), 1)
        dbt = jnp.zeros((BLK, LANES), F32)

        @pl.when(first)
        def _():
            dw_ref[...] = jnp.zeros((SGU_HEADS, BLK, BLK), F32)

        for j in range(SGU_CH // LANES):
            m, (v_lo, v_hi, w_lo, w_hi, lo, tri) = _sgu_mixed(v, w_ref, bt_ref, j)
            sl = slice(j * LANES, (j + 1) * LANES)
            do_v = do_ref[:, sl]
            du_ref[:, sl] = (do_v * m).astype(BF16)
            dm = do_v * u_ref[:, sl]
            dm_lo = jnp.where(lo, dm, 0.0)
            dm_hi = jnp.where(lo, 0.0, dm)
            dbt = dbt + jnp.where(lane == 2 * j, jnp.sum(dm_lo, axis=-1, keepdims=True), 0.0)
            dbt = dbt + jnp.where(lane == 2 * j + 1, jnp.sum(dm_hi, axis=-1, keepdims=True), 0.0)
            dm_lo, dm_hi = dm_lo.astype(BF16), dm_hi.astype(BF16)
            dw_ref[2 * j] += jnp.where(tri, lax.dot_general(dm_lo, v_lo, NT, preferred_element_type=F32), 0.0)
            dw_ref[2 * j + 1] += jnp.where(tri, lax.dot_general(dm_hi, v_hi, NT, preferred_element_type=F32), 0.0)
            dv_ref[:, sl] = (lax.dot_general(w_lo, dm_lo, TN, preferred_element_type=F32)
                             + lax.dot_general(w_hi, dm_hi, TN, preferred_element_type=F32))
        _accumulate(dbt_ref, dbt, first)
        dv = dv_ref[...]
        _accumulate(db_ref, jnp.sum(dv, axis=0, keepdims=True), first)
        _accumulate(dg_ref, jnp.sum(dv * xhat, axis=0, keepdims=True), first)
        dvin_ref[...] = _ln_bwd(dv, r, xhat, g_v).astype(BF16)

    u_spec, v_spec, vec, w_spec, bt_spec, row = _sgu_specs()
    do_spec = pl.BlockSpec((BLK, SGU_CH), lambda i: (i, (Q_END + CONV_CH) // SGU_CH))
    half = jax.ShapeDtypeStruct((t_len, SGU_CH), BF16)
    vshape = jax.ShapeDtypeStruct((1, SGU_CH), F32)
    return _call(body, name="sgu_bwd", grid=(t_len // BLK,),
                 in_specs=[u_spec, v_spec, do_spec, vec, vec, w_spec, bt_spec],
                 out_specs=[row, row, w_spec, pl.BlockSpec((BLK, LANES), lambda i: (0, 0)), vec, vec],
                 out_shape=[half, half, jax.ShapeDtypeStruct((SGU_HEADS, BLK, BLK), F32),
                            jax.ShapeDtypeStruct((BLK, LANES), F32), vshape, vshape],
                 scratch_shapes=[pltpu.VMEM((BLK, SGU_CH), F32)],
                 compiler_params=_params(1))(p, p, dmix, g, b, w, bt)


def _place():
    x, y, c = lax.axis_index("x"), lax.axis_index("y"), lax.axis_index("c")
    chips = [(1 - x, y), (x, 1 - y), (1 - x, 1 - y)]
    return x, y, c, chips


def _hbm_specs(n):
    return [pl.BlockSpec(memory_space=pltpu.HBM)] * n


def _comm_params():
    return pltpu.CompilerParams(has_side_effects=True)


def _remote(src, dst, send_sem, recv_sem, to):
    return pltpu.make_async_remote_copy(src_ref=src, dst_ref=dst, send_sem=send_sem, recv_sem=recv_sem,
                                        device_id=to, device_id_type=MESH_ID)


def _cast_place(w_local, chip):
    n, rows, cols = w_local.shape

    def body(chip_ref, w_ref, o_ref):
        o_ref[...] = w_ref[...].astype(BF16)

    grid_spec = pltpu.PrefetchScalarGridSpec(
        num_scalar_prefetch=1, grid=(n, rows // ROW_TILE),
        in_specs=[pl.BlockSpec((None, ROW_TILE, cols), lambda l, i, ch: (l, i, 0))],
        out_specs=pl.BlockSpec((None, None, ROW_TILE, cols), lambda l, i, ch: (l, ch[0], i, 0)))
    return _call(body, name="cast_place", grid_spec=grid_spec,
                 out_shape=jax.ShapeDtypeStruct((n, N_CHIPS, rows, cols), BF16), compiler_params=_params(2))(chip, w_local)


def _all_gather_weights(placed, shards):
    n_placed, nt = len(placed), len(placed) + len(shards)

    def body(*refs):
        ins, outs = refs[:nt], refs[nt:2 * nt]
        ici_send, ici_recv, d2d_send, d2d_recv, local_sem = refs[2 * nt:]
        x, y, c, chips = _place()
        me = 2 * x + y
        sibling = (x, y, 1 - c)
        local = [pltpu.make_async_copy(ins[t].at[l], outs[t].at[l, me], local_sem.at[2 * (t - n_placed) + l])
                 for t in range(n_placed, nt) for l in range(2)]
        for cp in local:
            cp.start()
        sends = []
        for t in range(nt):
            src = outs[t].at[c, me] if t < n_placed else ins[t].at[c]
            for j, (px, py) in enumerate(chips):
                sends.append(_remote(src, outs[t].at[c, me], ici_send.at[3 * t + j], ici_recv.at[3 * t + j],
                                     (px, py, c)))
        for cp in sends:
            cp.start()
        for t in range(nt):
            for j, (px, py) in enumerate(chips):
                slab = outs[t].at[c, 2 * px + py]
                _remote(slab, slab, ici_send.at[3 * t + j], ici_recv.at[3 * t + j], (px, py, c)).wait_recv()
                fwd = _remote(slab, slab, d2d_send.at[3 * t + j], d2d_recv.at[3 * t + j], sibling)
                fwd.start()
                sends.append(fwd)
        for t in range(nt):
            for j, (px, py) in enumerate(chips):
                slab = outs[t].at[1 - c, 2 * px + py]
                _remote(slab, slab, d2d_send.at[3 * t + j], d2d_recv.at[3 * t + j], sibling).wait_recv()
        for cp in sends:
            cp.wait_send()
        for cp in local:
            cp.wait()

    out_shape = [jax.ShapeDtypeStruct(p.shape, p.dtype) for p in placed]
    out_shape += [jax.ShapeDtypeStruct((2, N_CHIPS) + s.shape[1:], s.dtype) for s in shards]
    sems = [pltpu.SemaphoreType.DMA((3 * nt,))] * 4 + [pltpu.SemaphoreType.DMA((2 * len(shards),))]
    return _call(body, name="all_gather_weights", in_specs=_hbm_specs(nt), out_specs=_hbm_specs(nt),
                 out_shape=out_shape, scratch_shapes=sems, input_output_aliases={t: t for t in range(n_placed)},
                 compiler_params=_comm_params())(*placed, *shards)


def _gather_comm(bufs, pieces):
    n = len(pieces)
    sems = [pltpu.SemaphoreType.DMA((3 * n,))] * 4

    def half(ref, layer, chip, r0, nr, which):
        return ref.at[layer, chip, pl.ds(pl.multiple_of(r0 + which * (nr // 2), SUBLANES), nr // 2)]

    def start(rd, al, fr, sm):
        ici_send, ici_recv, _, _ = sm
        x, y, c, chips = _place()
        for i, (t, layer, r0, nr) in enumerate(pieces):
            own = half(al[t], layer, 2 * x + y, r0, nr, c)
            for j, (px, py) in enumerate(chips):
                _remote(own, own, ici_send.at[3 * i + j], ici_recv.at[3 * i + j], (px, py, c)).start()

    def finish(rd, al, fr, sm):
        ici_send, ici_recv, d2d_send, d2d_recv = sm
        x, y, c, chips = _place()
        sibling = (x, y, 1 - c)
        passed = []
        for i, (t, layer, r0, nr) in enumerate(pieces):
            for j, (px, py) in enumerate(chips):
                got = half(al[t], layer, 2 * px + py, r0, nr, c)
                _remote(got, got, ici_send.at[3 * i + j], ici_recv.at[3 * i + j], (px, py, c)).wait_recv()
                fwd = _remote(got, got, d2d_send.at[3 * i + j], d2d_recv.at[3 * i + j], sibling)
                fwd.start()
                passed.append(fwd)
        for i, (t, layer, r0, nr) in enumerate(pieces):
            own = half(al[t], layer, 2 * x + y, r0, nr, c)
            for j, (px, py) in enumerate(chips):
                _remote(own, own, ici_send.at[3 * i + j], ici_recv.at[3 * i + j], (px, py, c)).wait_send()
                theirs = half(al[t], layer, 2 * px + py, r0, nr, 1 - c)
                _remote(theirs, theirs, d2d_send.at[3 * i + j], d2d_recv.at[3 * i + j], sibling).wait_recv()
        for fwd in passed:
            fwd.wait_send()

    return _Comm([], bufs, [], sems, start, finish)


def _own_rows(ref, c, which=0):
    hr = ref.shape[-2] // 2
    start = pl.multiple_of((c if which == 0 else 1 - c) * hr, SUBLANES)
    return ref.at[(slice(None),) * (len(ref.shape) - 2) + (pl.ds(start, hr),)]


def _to_owner_comm(stacks, layer):
    nt = len(stacks)
    sems = [pltpu.SemaphoreType.DMA((nt,))] * 2
    fresh = [jax.ShapeDtypeStruct((N_CHIPS, s.shape[2] // 2, s.shape[3]), s.dtype) for s in stacks]

    def copies(rd, fr, sm):
        x, y, c, _ = _place()
        return [_remote(_own_rows(rd[t].at[layer], c, 1), fr[t], sm[0].at[t], sm[1].at[t], (x, y, 1 - c))
                for t in range(nt)]

    def start(rd, al, fr, sm):
        for cp in copies(rd, fr, sm):
            cp.start()

    def finish(rd, al, fr, sm):
        for cp in copies(rd, fr, sm):
            cp.wait()

    return _Comm(stacks, [], fresh, sems, start, finish)


def _chip_comm(partials):
    nt = len(partials)
    sems = [pltpu.SemaphoreType.DMA((3 * nt,))] * 2
    fresh = [jax.ShapeDtypeStruct((3,) + p.shape[1:], p.dtype) for p in partials]

    def each(rd, fr, sm, act):
        x, y, c, chips = _place()
        for t in range(nt):
            for j, (px, py) in enumerate(chips):
                act(_remote(rd[t].at[2 * px + py], fr[t].at[j], sm[0].at[3 * t + j], sm[1].at[3 * t + j], (px, py, c)))

    def start(rd, al, fr, sm):
        each(rd, fr, sm, lambda cp: cp.start())

    def finish(rd, al, fr, sm):
        each(rd, fr, sm, lambda cp: cp.wait())

    return _Comm(partials, [], fresh, sems, start, finish)


def _from_owner_comm(finals, layer):
    nt = len(finals)
    sems = [pltpu.SemaphoreType.DMA((nt,))] * 2

    def start(rd, al, fr, sm):
        x, y, c, _ = _place()
        for t in range(nt):
            mine = _own_rows(al[t].at[layer], c)
            _remote(mine, mine, sm[0].at[t], sm[1].at[t], (x, y, 1 - c)).start()

    def finish(rd, al, fr, sm):
        x, y, c, _ = _place()
        for t in range(nt):
            mine, theirs = _own_rows(al[t].at[layer], c), _own_rows(al[t].at[layer], c, 1)
            _remote(mine, mine, sm[0].at[t], sm[1].at[t], (x, y, 1 - c)).wait_send()
            _remote(theirs, theirs, sm[0].at[t], sm[1].at[t], (x, y, 1 - c)).wait_recv()

    return _Comm([], finals, [], sems, start, finish)


def _all_reduce_small(buf):
    rows = buf.shape[0]

    def body(x_ref, out_ref, all_ref, send_sems, recv_sems, local_sem):
        x, y, c, chips = _place()
        me, sibling = (x, y, c), (x, y, 1 - c)

        def block(px, py, pc):
            return all_ref.at[pl.ds((4 * px + 2 * py + pc) * rows, rows), :]

        def copy(k, blk, to, src=None):
            return _remote(block(*blk) if src is None else src, block(*blk), send_sems.at[k], recv_sems.at[k], to)

        mine = pltpu.make_async_copy(x_ref, block(*me), local_sem)
        mine.start()
        first = [copy(0, me, sibling, src=x_ref)]
        first += [copy(1 + j, me, (*chip, c), src=x_ref) for j, chip in enumerate(chips)]
        for cp in first:
            cp.start()
        passed = [copy(4 + j, (*chip, c), sibling) for j, chip in enumerate(chips)]
        for j, chip in enumerate(chips):
            copy(1 + j, (*chip, c), me).wait_recv()
            passed[j].start()
        copy(0, sibling, me).wait_recv()
        for j, chip in enumerate(chips):
            copy(4 + j, (*chip, 1 - c), me).wait_recv()
        for cp in first + passed:
            cp.wait_send()
        mine.wait()
        tot = all_ref[0:rows, :]
        for k in range(1, N_DEV):
            tot = tot + all_ref[k * rows:(k + 1) * rows, :]
        out_ref[...] = tot

    vm = pl.BlockSpec(memory_space=pltpu.VMEM)
    return _call(body, name="all_reduce_small", in_specs=[vm], out_specs=vm,
                 out_shape=jax.ShapeDtypeStruct(buf.shape, F32),
                 scratch_shapes=[pltpu.VMEM((N_DEV * rows, LANES), F32), pltpu.SemaphoreType.DMA((7,)),
                                 pltpu.SemaphoreType.DMA((7,)), pltpu.SemaphoreType.DMA],
                 compiler_params=pltpu.CompilerParams(has_side_effects=True,
                                                      vmem_limit_bytes=V7X_VMEM_LIMIT_BYTES))(buf)


ROW_TILE = 128


def _chip_partial(stack, received, layer, place):
    _, half_rows, cols = received.shape
    tr = _pick(half_rows, (ROW_TILE, ROW_TILE // 2))
    nh = half_rows // tr

    def body(place_ref, a_ref, b_ref, o_ref):
        o_ref[...] = (a_ref[...].astype(F32) + b_ref[...].astype(F32)).astype(BF16)

    blk = pl.BlockSpec((None, tr, cols), lambda s, i, pr: (s, i, 0))
    grid_spec = pltpu.PrefetchScalarGridSpec(
        num_scalar_prefetch=1, grid=(N_CHIPS, nh),
        in_specs=[pl.BlockSpec((None, None, tr, cols), lambda s, i, pr: (layer, s, pr[0] * nh + i, 0)), blk],
        out_specs=blk)
    return _call(body, name="chip_partial", grid_spec=grid_spec, out_shape=jax.ShapeDtypeStruct(received.shape, BF16),
                 compiler_params=_params(2))(place, stack, received)


def _final_sum(partial, from_chips, layer, place, finals):
    _, half_rows, cols = partial.shape
    tr = _pick(half_rows, (ROW_TILE, ROW_TILE // 2))
    nh = half_rows // tr

    def body(place_ref, a_ref, r_ref, *rest):
        o_ref = rest[-1]
        tot = a_ref[...].astype(F32)
        for j in range(3):
            tot = tot + r_ref[j].astype(F32)
        o_ref[...] = tot

    in_specs = [pl.BlockSpec((None, tr, cols), lambda i, pr: (pr[1], i, 0)),
                pl.BlockSpec((3, tr, cols), lambda i, pr: (0, i, 0))]
    args = [place, partial, from_chips]
    kw = {}
    if finals is not None:
        in_specs.append(pl.BlockSpec(memory_space=pl.ANY))
        args.append(finals)
        kw["input_output_aliases"] = {3: 0}
    grid_spec = pltpu.PrefetchScalarGridSpec(
        num_scalar_prefetch=1, grid=(nh,), in_specs=in_specs,
        out_specs=pl.BlockSpec((None, tr, cols), lambda i, pr: (layer, pr[0] * nh + i, 0)))
    return _call(body, name="final_sum", grid_spec=grid_spec,
                 out_shape=jax.ShapeDtypeStruct((2, 2 * half_rows, cols), F32), compiler_params=_params(1), **kw)(*args)


def _adamw(w, g, m, v, comm=None):
    n, rows, cols = w.shape
    tr = _pick(rows, (ROW_TILE, SUBLANES))
    c1 = 1.0 - ADAM_B1 ** ADAM_STEP
    c2 = 1.0 - ADAM_B2 ** ADAM_STEP

    def body(w_ref, g_ref, m_ref, v_ref, d_ref, nm_ref, nv_ref, go_ref):
        g_v = g_ref[...]
        go_ref[...] = g_v
        nm = ADAM_B1 * m_ref[...] + (1.0 - ADAM_B1) * g_v
        nv = ADAM_B2 * v_ref[...] + (1.0 - ADAM_B2) * (g_v * g_v)
        nm_ref[...] = nm
        nv_ref[...] = nv
        d_ref[...] = -ADAM_LR * ((nm / c1) / (jnp.sqrt(nv / c2) + ADAM_EPS) + ADAM_WD * w_ref[...])

    blk = pl.BlockSpec((None, tr, cols), lambda l, i: (l, i, 0))
    shape = jax.ShapeDtypeStruct(w.shape, F32)
    return _hosted_call(body, comm, "adamw", (n, rows // tr), [w, g, m, v], [blk] * 4, [shape] * 4, [blk] * 4)


def _to_heads(a, n_heads):
    t_len = a.shape[0]
    return a.reshape(t_len, n_heads, HEAD_DIM).transpose(1, 0, 2)


def _from_heads(a):
    n_heads, t_len, _ = a.shape
    return a.transpose(1, 0, 2).reshape(t_len, n_heads * HEAD_DIM)


class _Schedule:
    def __init__(self):
        self.sites = {}
        self.open = []

    def add(self, site, make, done=None):
        self.sites.setdefault(site, []).append((make, done))

    def begin(self, site):
        self.open = [(make(), done) for make, done in self.sites.pop(site, [])]
        return _merge_comms([cm for cm, _ in self.open])

    def end(self):
        for cm, done in self.open:
            if done is not None:
                done(cm)
        self.open = []


def _ffn_fwd(x, gain, wg, which, layer, sched):
    w_in_name, w_out_name = f"ffn{which}_w_in", f"ffn{which}_w_out"
    h = _rmsnorm_fwd(x, gain)
    comm = sched.begin(("ffn_in", layer, which))
    gu, act = _mm_ffn_in(h, wg[w_in_name], layer, comm)
    sched.end()
    comm = sched.begin(("ffn_out", layer, which))
    x_new = _mm_out_res("ffn_out", act, wg[w_out_name], layer, x, FFN_RESIDUAL_WEIGHT, comm)[0]
    sched.end()
    return x_new, (x, h, gu, act)


def _ffn_bwd(dx, dxb, saved, gain, wg, which, layer, stacks, sched, weights_first=False):
    w_in_name, w_out_name = f"ffn{which}_w_in", f"ffn{which}_w_out"
    x, h, gu, act = saved
    out = {}

    def dact():
        comm = sched.begin(("ffn_dact", layer, which))
        out["dgu"] = _mm_dact_swiglu(dxb, wg[w_out_name], layer, gu, FFN_RESIDUAL_WEIGHT, comm)
        sched.end()

    def dw_out():
        comm = sched.begin(("ffn_dw_out", layer, which))
        stacks[w_out_name] = _dw_rows("ffn_dw_out", act, dxb, layer, stacks[w_out_name], FFN_RESIDUAL_WEIGHT, comm)
        sched.end()

    def dh():
        comm = sched.begin(("ffn_dh", layer, which))
        out["dh"] = _mm_dh_ffn(out["dgu"], wg[w_in_name], layer, comm)
        sched.end()

    def dw_in():
        comm = sched.begin(("ffn_dw_in", layer, which))
        stacks[w_in_name] = _dw_ffn_in(h, out["dgu"], layer, stacks[w_in_name], comm)
        sched.end()

    for step in ((dact, dw_in, dw_out, dh) if weights_first else (dact, dw_out, dh, dw_in)):
        step()
    return _rmsnorm_bwd(out["dh"], x, gain, dx)


def _mix_fwd(x, gain, wg, layer, small, tables, sched):
    h = _rmsnorm_fwd(x, gain)
    comm = sched.begin(("mix_in", layer))
    p = _mm_proj(h, wg["w_in"], layer, comm)[0]
    sched.end()
    qkv = _rope_fwd(p, tables)
    q = _to_heads(qkv[:, :Q_END], N_Q_HEADS)
    k = _to_heads(qkv[:, Q_END:K_END], N_KV_HEADS)
    v = _to_heads(qkv[:, K_END:V_END], N_KV_HEADS)
    comm = sched.begin(("attn", layer))
    attn = _from_heads(_attn_fwd(q, k, v, small["snk"], comm))
    sched.end()
    y = _conv_fwd(p, small["conv_w"], small["conv_b"])
    conv = _conv_post_fwd(y, small["conv_ln_g"], small["conv_ln_b"])
    sgu = _sgu_fwd(p, small["sgu_ln_g"], small["sgu_ln_b"], small["sgu_w"], small["sgu_bt"])
    mix = jnp.concatenate([attn, conv, sgu], axis=1)
    x_new = _mm_out_res("mix_out", mix, wg["w_out"], layer, x, 1.0)[0]
    return x_new, (x, h, p, q, k, v, y, mix)


def _mix_bwd(dx, dxb, saved, gain, wg, small, tables, layer, stacks, sched):
    x, h, p, q, k, v, y, mix = saved
    comm = sched.begin(("mix_dout", layer))
    dmix = _mm_dmix(dxb, wg["w_out"], layer, comm)
    sched.end()
    stacks["w_out"] = _dw_rows("mix_dw_out", mix, dxb, layer, stacks["w_out"], 1.0)
    do = _to_heads(dmix[:, :Q_END], N_Q_HEADS)
    comm = sched.begin(("attn_bwd", layer))
    dq, dkp, dkc, dvp, dvc, dsnk = _attn_bwd(q, k, v, small["snk"], do, comm)
    sched.end()
    dy, d_ln_g, d_ln_b, d_conv_b = _conv_post_bwd(dmix, y, small["conv_ln_g"], small["conv_ln_b"])
    dalin, dagate, d_conv_w = _conv_bwd(p, dy, small["conv_w"])
    du, dvin, d_sgu_w, d_sgu_bt, d_sgu_g, d_sgu_b = _sgu_bwd(p, dmix, small["sgu_ln_g"], small["sgu_ln_b"],
                                                           small["sgu_w"], small["sgu_bt"])
    dp = _assemble_dp(_from_heads(dq), _from_heads(dkc), _from_heads(dkp), _from_heads(dvc), _from_heads(dvp),
                      tables, dalin, dagate, du, dvin)
    comm = sched.begin(("mix_dw_in", layer))
    stacks["w_in"] = _dw_mix_in(h, dp, layer, stacks["w_in"], comm)
    sched.end()
    comm = sched.begin(("mix_dh", layer))
    dh = _mm_dh_mix(dp, wg["w_in"], layer, comm)
    sched.end()
    dx_in, dxb_in, dgain = _rmsnorm_bwd(dh, x, gain, dx)
    grads = {
        "norm_mix": dgain[0], "conv_dw_w": d_conv_w[:CONV_WIDTH], "conv_dw_b": d_conv_b[0],
        "conv_ln_g": d_ln_g[0], "conv_ln_b": d_ln_b[0], "sgu_ln_g": d_sgu_g[0], "sgu_ln_b": d_sgu_b[0],
        "sgu_w": d_sgu_w, "sgu_b": d_sgu_bt[:, :SGU_HEADS].T, "attn_sinks": dsnk[:, :GQ, 0].reshape(N_Q_HEADS),
    }
    return dx_in, dxb_in, grads


BIG = ("ffn1_w_in", "ffn1_w_out", "w_in", "w_out", "ffn2_w_in", "ffn2_w_out")
SMALL = ("norm_ffn1", "norm_mix", "conv_dw_w", "conv_dw_b", "conv_ln_g", "conv_ln_b", "sgu_ln_g", "sgu_ln_b",
         "sgu_w", "sgu_b", "attn_sinks", "norm_ffn2", "final_norm")
WEIGHTS = ("norm_ffn1", "ffn1_w_in", "ffn1_w_out", "norm_mix", "w_in", "conv_dw_w", "conv_dw_b", "conv_ln_g",
           "conv_ln_b", "sgu_ln_g", "sgu_ln_b", "sgu_w", "sgu_b", "attn_sinks", "w_out", "norm_ffn2", "ffn2_w_in",
           "ffn2_w_out", "final_norm")
PACK_ROWS = SUBLANES * LANES

FIRST_GATHER = [("ffn1_w_in", 0, None)]
FORWARD_PLAN = {
    ("ffn_in", 0, 1): [("ffn1_w_out", 0, None), ("w_in", 0, None)],
    ("ffn_out", 0, 1): [("ffn2_w_in", 0, 0)],
    ("mix_in", 0): [("w_out", 0, None)],
    ("attn", 0): [("ffn2_w_in", 0, 1)],
    ("ffn_in", 0, 2): [("ffn2_w_out", 0, None), ("ffn1_w_in", 1, 0)],
    ("ffn_out", 0, 2): [("ffn1_w_in", 1, 1)],
    ("ffn_in", 1, 1): [("ffn1_w_out", 1, None), ("w_in", 1, None)],
    ("ffn_out", 1, 1): [("ffn2_w_in", 1, 0)],
    ("mix_in", 1): [("w_out", 1, None)],
    ("attn", 1): [("ffn2_w_in", 1, 1)],
    ("ffn_in", 1, 2): [("ffn2_w_out", 1, None)],
}
SUBLAYER_WEIGHTS = {"ffn1": ["ffn1_w_out", "ffn1_w_in"], "ffn2": ["ffn2_w_out", "ffn2_w_in"], "mix": ["w_out", "w_in"]}


def _pack(arrays):
    flat = jnp.concatenate([a.reshape(-1).astype(F32) for a in arrays])
    pad = (-flat.shape[0]) % PACK_ROWS
    return jnp.pad(flat, (0, pad)).reshape(-1, LANES)


def _unpack(buf, shapes):
    flat = buf.reshape(-1)
    out, off = [], 0
    for s in shapes:
        n = 1
        for d in s:
            n *= d
        out.append(flat[off:off + n].reshape(s))
        off += n
    return out


def kernel(x, positions, norm_ffn1, ffn1_w_in, ffn1_w_out, norm_mix, w_in, conv_dw_w, conv_dw_b, conv_ln_g, conv_ln_b, sgu_ln_g, sgu_ln_b, sgu_w, sgu_b, attn_sinks, w_out, norm_ffn2, ffn2_w_in, ffn2_w_out, final_norm, loss_target, m_norm_ffn1, m_ffn1_w_in, m_ffn1_w_out, m_norm_mix, m_w_in, m_conv_dw_w, m_conv_dw_b, m_conv_ln_g, m_conv_ln_b, m_sgu_ln_g, m_sgu_ln_b, m_sgu_w, m_sgu_b, m_attn_sinks, m_w_out, m_norm_ffn2, m_ffn2_w_in, m_ffn2_w_out, m_final_norm, v_norm_ffn1, v_ffn1_w_in, v_ffn1_w_out, v_norm_mix, v_w_in, v_conv_dw_w, v_conv_dw_b, v_conv_ln_g, v_conv_ln_b, v_sgu_ln_g, v_sgu_ln_b, v_sgu_w, v_sgu_b, v_attn_sinks, v_w_out, v_norm_ffn2, v_ffn2_w_in, v_ffn2_w_out, v_final_norm):
    w = dict(norm_ffn1=norm_ffn1, ffn1_w_in=ffn1_w_in, ffn1_w_out=ffn1_w_out, norm_mix=norm_mix, w_in=w_in,
             conv_dw_w=conv_dw_w, conv_dw_b=conv_dw_b, conv_ln_g=conv_ln_g, conv_ln_b=conv_ln_b, sgu_ln_g=sgu_ln_g,
             sgu_ln_b=sgu_ln_b, sgu_w=sgu_w, sgu_b=sgu_b, attn_sinks=attn_sinks, w_out=w_out, norm_ffn2=norm_ffn2,
             ffn2_w_in=ffn2_w_in, ffn2_w_out=ffn2_w_out, final_norm=final_norm)
    m = dict(norm_ffn1=m_norm_ffn1, ffn1_w_in=m_ffn1_w_in, ffn1_w_out=m_ffn1_w_out, norm_mix=m_norm_mix, w_in=m_w_in,
             conv_dw_w=m_conv_dw_w, conv_dw_b=m_conv_dw_b, conv_ln_g=m_conv_ln_g, conv_ln_b=m_conv_ln_b,
             sgu_ln_g=m_sgu_ln_g, sgu_ln_b=m_sgu_ln_b, sgu_w=m_sgu_w, sgu_b=m_sgu_b, attn_sinks=m_attn_sinks,
             w_out=m_w_out, norm_ffn2=m_norm_ffn2, ffn2_w_in=m_ffn2_w_in, ffn2_w_out=m_ffn2_w_out,
             final_norm=m_final_norm)
    v = dict(norm_ffn1=v_norm_ffn1, ffn1_w_in=v_ffn1_w_in, ffn1_w_out=v_ffn1_w_out, norm_mix=v_norm_mix, w_in=v_w_in,
             conv_dw_w=v_conv_dw_w, conv_dw_b=v_conv_dw_b, conv_ln_g=v_conv_ln_g, conv_ln_b=v_conv_ln_b,
             sgu_ln_g=v_sgu_ln_g, sgu_ln_b=v_sgu_ln_b, sgu_w=v_sgu_w, sgu_b=v_sgu_b, attn_sinks=v_attn_sinks,
             w_out=v_w_out, norm_ffn2=v_norm_ffn2, ffn2_w_in=v_ffn2_w_in, ffn2_w_out=v_ffn2_w_out,
             final_norm=v_final_norm)
    depth = norm_ffn1.shape[0]
    assert depth == 2 and x.shape[0] == 1
    xc = lax.axis_index("x")
    yc = lax.axis_index("y")
    cc = lax.axis_index("c")
    chip = 2 * xc + yc

    chip_arr = chip.reshape(1).astype(jnp.int32)
    place = jnp.stack([cc, chip]).astype(jnp.int32)
    wg = {n: _cast_place(w[n], chip_arr) for n in BIG}
    sched = _Schedule()

    def gather(pieces):
        names = sorted({n for n, _, _ in pieces})
        half = w["ffn1_w_in"].shape[1] // 2

        def make():
            rows = lambda n, part: (0, wg[n].shape[2]) if part is None else (part * half, half)
            return _gather_comm([wg[n] for n in names], [(names.index(n), l, *rows(n, part)) for n, l, part in pieces])

        return make, lambda cm: wg.update(zip(names, cm.aliased_out))

    make, done = gather(FIRST_GATHER)
    first = make()
    _standalone("gather_first", first)
    done(first)
    for site, pieces in FORWARD_PLAN.items():
        sched.add(site, *gather(pieces))
    conv_w_full = _all_gather_weights([], [conv_dw_w])[0].transpose(0, 2, 1, 3).reshape(depth, CONV_WIDTH, CONV_CH)
    conv_w_full = jnp.pad(conv_w_full, ((0, 0), (0, CONV_PAD - CONV_WIDTH), (0, 0)))

    tables = _rope_tables(positions)
    small = []
    for l in range(depth):
        small.append(dict(
            snk=jnp.broadcast_to(attn_sinks[l].reshape(N_KV_HEADS, GQ, 1, 1), (N_KV_HEADS, GQ, BLK, 1)).reshape(
                N_KV_HEADS, GQ * BLK, 1),
            conv_w=conv_w_full[l], conv_b=conv_dw_b[l][None], conv_ln_g=conv_ln_g[l][None],
            conv_ln_b=conv_ln_b[l][None], sgu_ln_g=sgu_ln_g[l][None], sgu_ln_b=sgu_ln_b[l][None], sgu_w=sgu_w[l],
            sgu_bt=sgu_b[l].T))

    xs = x[0]
    saved = []
    for l in range(depth):
        xs, s1 = _ffn_fwd(xs, norm_ffn1[l][None], wg, 1, l, sched)
        xs, s2 = _mix_fwd(xs, norm_mix[l][None], wg, l, small[l], tables, sched)
        xs, s3 = _ffn_fwd(xs, norm_ffn2[l][None], wg, 2, l, sched)
        saved.append((s1, s2, s3))
    dx, dxb, d_final, loss_part = _loss_head(xs, final_norm[None], loss_target[0])

    stacks = {n: None for n in BIG}
    partials, from_chips = {}, {}
    finals = {n: None for n in BIG}

    def to_owner(layer, names):
        def done(cm):
            for n, received in zip(names, cm.fresh_out):
                partials[n, layer] = _chip_partial(stacks[n], received, layer, place)

        return lambda: _to_owner_comm([stacks[n] for n in names], layer), done

    def between_chips(layer, names, then_sum=()):
        def done(cm):
            from_chips.update({(n, layer): r for n, r in zip(names, cm.fresh_out)})
            for n in then_sum:
                finals[n] = _final_sum(partials[n, layer], from_chips[n, layer], layer, place, finals[n])

        return lambda: _chip_comm([partials[n, layer] for n in names]), done

    def from_owner(layer, names):
        return (lambda: _from_owner_comm([finals[n] for n in names], layer),
                lambda cm: finals.update(zip(names, cm.aliased_out)))

    order = [(l, kind) for l in reversed(range(depth)) for kind in ("ffn2", "mix", "ffn1")]
    for (layer, kind), (nxt_layer, nxt_kind) in zip(order[:-1], order[1:]):
        names = SUBLAYER_WEIGHTS[kind]
        if (nxt_layer, nxt_kind) == order[-1]:
            w_out_name, w_in_name = SUBLAYER_WEIGHTS[nxt_kind]
            which = int(nxt_kind[-1])
            sched.add(("ffn_dact", nxt_layer, which), *to_owner(layer, names))
            sched.add(("ffn_dw_in", nxt_layer, which), *between_chips(layer, names, names))
            sched.add(("ffn_dw_out", nxt_layer, which), *from_owner(layer, names))
            sched.add(("ffn_dw_out", nxt_layer, which), *to_owner(nxt_layer, [w_in_name]))
            sched.add(("ffn_dh", nxt_layer, which), *between_chips(nxt_layer, [w_in_name], [w_in_name]))
            sched.add(("ffn_dh", nxt_layer, which), *to_owner(nxt_layer, [w_out_name]))
        elif nxt_kind == "mix":
            sched.add(("mix_dout", nxt_layer), *to_owner(layer, names))
            sched.add(("attn_bwd", nxt_layer), *between_chips(layer, names[1:], names[1:]))
            sched.add(("mix_dw_in", nxt_layer), *between_chips(layer, names[:1], names[:1]))
            sched.add(("mix_dh", nxt_layer), *from_owner(layer, names))
        else:
            which = int(nxt_kind[-1])
            sched.add(("ffn_dact", nxt_layer, which), *to_owner(layer, names))
            if kind == "mix":
                sched.add(("ffn_dw_out", nxt_layer, which), *between_chips(layer, names, names))
            else:
                sched.add(("ffn_dw_out", nxt_layer, which), *between_chips(layer, names[:1]))
                sched.add(("ffn_dh", nxt_layer, which), *between_chips(layer, names[1:], names))
            sched.add(("ffn_dw_in", nxt_layer, which), *from_owner(layer, names))

    small_grads = [None] * depth
    for l in reversed(range(depth)):
        s1, s2, s3 = saved[l]
        dx, dxb, dg2 = _ffn_bwd(dx, dxb, s3, norm_ffn2[l][None], wg, 2, l, stacks, sched)
        dx, dxb, gm = _mix_bwd(dx, dxb, s2, norm_mix[l][None], wg, small[l], tables, l, stacks, sched)
        dx, dxb, dg1 = _ffn_bwd(dx, dxb, s1, norm_ffn1[l][None], wg, 1, l, stacks, sched, weights_first=l == 0)
        gm["norm_ffn1"] = dg1[0]
        gm["norm_ffn2"] = dg2[0]
        small_grads[l] = gm
    grad_x = dx[None]
    assert not sched.sites, sched.sites

    per_layer = [n for n in SMALL if n != "final_norm"]
    small_local = [jnp.stack([small_grads[l][n] for l in range(depth)]) for n in per_layer]
    small_local += [d_final[0], loss_part[0, :1]]
    small_shapes = [a.shape for a in small_local]
    summed = _unpack(_all_reduce_small(_pack(small_local)), small_shapes)
    loss = summed[-1][0]
    sg = dict(zip(per_layer + ["final_norm"], summed[:-1]))
    sg["conv_dw_w"] = lax.dynamic_slice_in_dim(sg["conv_dw_w"], chip * LANES, LANES, axis=2)

    delta, new_m, new_v = {}, {}, {}
    shapes = [w[n].shape for n in SMALL]
    packed = [_pack([d[n] for n in SMALL])[None] for d in (w, sg, m, v)]
    last_layer, last_kind = order[-1]
    names = SUBLAYER_WEIGHTS[last_kind]
    make, done = between_chips(last_layer, names[:1], names[:1])
    cm = make()
    outs = _adamw(*packed, comm=cm)
    done(cm)
    for d, buf in zip((delta, new_m, new_v), outs[:3]):
        d.update(zip(SMALL, _unpack(buf[0], shapes)))
    make, done = from_owner(last_layer, names)
    cm = make()
    _standalone("from_owner", cm)
    done(cm)
    big_grads = dict(finals)
    for n in BIG:
        delta[n], new_m[n], new_v[n], big_grads[n] = _adamw(w[n], big_grads[n], m[n], v[n])
    grads = {**big_grads, **sg}
    return (loss, grad_x, *[grads[n] for n in WEIGHTS], *[delta[n] for n in WEIGHTS],
            *[new_m[n] for n in WEIGHTS], *[new_v[n] for n in WEIGHTS])
```

```python
import functools

import jax
import jax.numpy as jnp
from jax import lax
from jax.experimental import pallas as pl
from jax.experimental.pallas import tpu as pltpu

F32 = jnp.float32
BF16 = jnp.bfloat16
MESH_ID = pl.DeviceIdType.MESH

V7X_VMEM_LIMIT_BYTES = 56 * 2**20
LANES = 128
SUBLANES = 8

HEAD_DIM = 64
N_Q_HEADS = 16
N_KV_HEADS = 4
GQ = N_Q_HEADS // N_KV_HEADS
BLK = 128
ROT_HALF = 8
ROPE_THETA = 500000.0
CONV_WIDTH = 31
CONV_PAD = 32
CONV_CH = 512
SGU_CH = 512
SGU_HEADS = 8
Q_END = N_Q_HEADS * HEAD_DIM
K_END = Q_END + N_KV_HEADS * HEAD_DIM
V_END = K_END + N_KV_HEADS * HEAD_DIM
CONV_END = V_END + 2 * CONV_CH
IN_COLS = CONV_END + 2 * SGU_CH
NORM_EPS = 1e-5
FFN_RESIDUAL_WEIGHT = 0.5
N_CHIPS = 4
N_DEV = 8

ADAM_LR = 0.001
ADAM_B1 = 0.9
ADAM_B2 = 0.999
ADAM_EPS = 1e-08
ADAM_WD = 0.01
ADAM_STEP = 10

NN = (((1,), (0,)), ((), ()))
NT = (((1,), (1,)), ((), ()))
TN = (((0,), (0,)), ((), ()))


def _pick(n, cands):
    for c in cands:
        if n % c == 0:
            return c
    raise ValueError(f"no tile of {cands} divides {n}")


def _params(n_axes):
    return pltpu.CompilerParams(dimension_semantics=("arbitrary",) * n_axes, vmem_limit_bytes=V7X_VMEM_LIMIT_BYTES)


def _call(body, **kw):
    return pl.pallas_call(body, **kw)


def _sigmoid(x):
    return 1.0 / (1.0 + jnp.exp(-x))


class _Comm:
    def __init__(self, reads, aliased, fresh, sems, start, finish):
        self.reads, self.aliased, self.fresh, self.sems = list(reads), list(aliased), list(fresh), list(sems)
        self.start, self.finish = start, finish
        self.aliased_out, self.fresh_out = None, None


def _merge_comms(comms):
    comms = [cm for cm in comms if cm is not None]
    if not comms:
        return None
    if len(comms) == 1:
        return comms[0]

    def split(refs, counts):
        out, off = [], 0
        for n in counts:
            out.append(refs[off:off + n])
            off += n
        return out

    def run(which):
        def f(rd, al, fr, sm):
            parts = zip(split(rd, [len(cm.reads) for cm in comms]), split(al, [len(cm.aliased) for cm in comms]),
                        split(fr, [len(cm.fresh) for cm in comms]), split(sm, [len(cm.sems) for cm in comms]))
            for cm, (r, a, f_, s) in zip(comms, parts):
                getattr(cm, which)(r, a, f_, s)
        return f

    merged = _Comm(sum((cm.reads for cm in comms), []), sum((cm.aliased for cm in comms), []),
                   sum((cm.fresh for cm in comms), []), sum((cm.sems for cm in comms), []), run("start"), run("finish"))
    merged.parts = comms
    return merged


def _hosted_call(body, comm, name, grid, inputs, in_specs, out_shape, out_specs, scratch_shapes=(), aliases=None):
    n_in, n_out, n_scr = len(inputs), len(out_shape), len(scratch_shapes)
    aliases = dict(aliases or {})
    if comm is None:
        return _call(body, name=name, grid=grid, in_specs=list(in_specs), out_specs=list(out_specs),
                     out_shape=list(out_shape), scratch_shapes=list(scratch_shapes), input_output_aliases=aliases,
                     compiler_params=_params(len(grid)))(*inputs)
    nr, na, nf = len(comm.reads), len(comm.aliased), len(comm.fresh)

    def full(*refs):
        ins = refs[:n_in]
        rd = refs[n_in:n_in + nr]
        pos = n_in + nr + na
        outs = refs[pos:pos + n_out]
        al = refs[pos + n_out:pos + n_out + na]
        fr = refs[pos + n_out + na:pos + n_out + na + nf]
        pos = pos + n_out + na + nf
        scr = refs[pos:pos + n_scr]
        sems = refs[pos + n_scr:]
        first, last = None, None
        for axis, size in enumerate(grid):
            f, l = pl.program_id(axis) == 0, pl.program_id(axis) == size - 1
            first = f if first is None else jnp.logical_and(first, f)
            last = l if last is None else jnp.logical_and(last, l)

        @pl.when(first)
        def _():
            comm.start(rd, al, fr, sems)

        body(*ins, *outs, *scr)

        @pl.when(last)
        def _():
            comm.finish(rd, al, fr, sems)

    hbm = pl.BlockSpec(memory_space=pltpu.HBM)
    for i in range(na):
        aliases[n_in + nr + i] = n_out + i
    struct = [jax.ShapeDtypeStruct(a.shape, a.dtype) for a in comm.aliased]
    res = _call(full, name=name, grid=grid, in_specs=list(in_specs) + [hbm] * (nr + na),
                out_specs=list(out_specs) + [hbm] * (na + nf), out_shape=list(out_shape) + struct + comm.fresh,
                scratch_shapes=list(scratch_shapes) + comm.sems, input_output_aliases=aliases,
                compiler_params=pltpu.CompilerParams(dimension_semantics=("arbitrary",) * len(grid),
                                                     vmem_limit_bytes=V7X_VMEM_LIMIT_BYTES, has_side_effects=True),
                )(*inputs, *comm.reads, *comm.aliased)
    _deliver(comm, res[n_out:n_out + na], res[n_out + na:])
    return res[:n_out]


def _deliver(comm, aliased_out, fresh_out):
    comm.aliased_out, comm.fresh_out = list(aliased_out), list(fresh_out)
    off_a = off_f = 0
    for part in getattr(comm, "parts", []):
        _deliver(part, aliased_out[off_a:off_a + len(part.aliased)], fresh_out[off_f:off_f + len(part.fresh)])
        off_a += len(part.aliased)
        off_f += len(part.fresh)


def _standalone(name, comm):
    def body(*refs):
        nr, na, nf = len(comm.reads), len(comm.aliased), len(comm.fresh)
        rd, al, fr, sems = refs[:nr], refs[nr + na:nr + 2 * na], refs[nr + 2 * na:nr + 2 * na + nf], refs[nr + 2 * na + nf:]
        comm.start(rd, al, fr, sems)
        comm.finish(rd, al, fr, sems)

    nr, na, nf = len(comm.reads), len(comm.aliased), len(comm.fresh)
    struct = [jax.ShapeDtypeStruct(a.shape, a.dtype) for a in comm.aliased]
    res = _call(body, name=name, in_specs=_hbm_specs(nr + na), out_specs=_hbm_specs(na + nf),
                out_shape=struct + comm.fresh, scratch_shapes=comm.sems,
                input_output_aliases={nr + i: i for i in range(na)},
                compiler_params=_comm_params())(*comm.reads, *comm.aliased)
    _deliver(comm, res[:na], res[na:])


def _matmul(name, grid, a_ops, b_ops, terms, dims, out_shape, out_specs, epilogue, extra_ops=(), nk=1,
            acc_shapes=(), alias=None, comm=None):
    na, nb, ne, no = len(a_ops), len(b_ops), len(extra_ops), len(out_shape)

    def body(*refs):
        a = refs[:na]
        b = refs[na:na + nb]
        e = refs[na + nb:na + nb + ne]
        first_out = na + nb + ne + (1 if alias is not None else 0)
        o = refs[first_out:first_out + no]
        accs = refs[first_out + no:]

        def partial(t):
            tot = None
            for ai, bi in t:
                d = lax.dot_general(a[ai][...], b[bi][...], dims, preferred_element_type=F32)
                tot = d if tot is None else tot + d
            return tot

        if nk == 1:
            epilogue([partial(t) for t in terms], e, o)
        else:
            k = pl.program_id(len(grid) - 1)

            @pl.when(k == 0)
            def _():
                for acc in accs:
                    acc[...] = jnp.zeros(acc.shape, F32)

            for acc, t in zip(accs, terms):
                acc[...] += partial(t)

            @pl.when(k == nk - 1)
            def _():
                epilogue([acc[...] for acc in accs], e, o)

    ops = list(a_ops) + list(b_ops) + list(extra_ops)
    arrays = [x for x, _ in ops]
    in_specs = [s for _, s in ops]
    aliases = {}
    if alias is not None:
        arrays.append(alias[0])
        in_specs.append(pl.BlockSpec(memory_space=pl.ANY))
        aliases[len(arrays) - 1] = alias[1]
    scratch = [pltpu.VMEM(s, F32) for s in acc_shapes] if nk > 1 else []
    return _hosted_call(body, comm, name, grid, arrays, in_specs, out_shape, out_specs, scratch, aliases)


def _mm_ffn_in(h, w_g, layer, comm=None):
    t_len, d = h.shape
    fs = w_g.shape[3]
    f = 2 * fs
    tm = _pick(t_len, (2048, 1024, 512))
    tn = _pick(fs, (256, 128))
    nj = fs // tn

    def epilogue(accs, e, o):
        g, u = accs
        o[0][0] = g.astype(BF16)
        o[0][1] = u.astype(BF16)
        o[1][...] = (g * _sigmoid(g) * u).astype(BF16)

    return _matmul(
        "ffn_in", (t_len // tm, 2, nj),
        [(h, pl.BlockSpec((tm, d), lambda i, s, j: (i, 0)))],
        [(w_g, pl.BlockSpec((None, None, d, tn), lambda i, s, j: (layer, s, 0, j))),
         (w_g, pl.BlockSpec((None, None, d, tn), lambda i, s, j: (layer, s + 2, 0, j)))],
        [[(0, 0)], [(0, 1)]], NN,
        [jax.ShapeDtypeStruct((2, t_len, f), BF16), jax.ShapeDtypeStruct((t_len, f), BF16)],
        [pl.BlockSpec((2, tm, tn), lambda i, s, j: (0, i, s * nj + j)),
         pl.BlockSpec((tm, tn), lambda i, s, j: (i, s * nj + j))],
        epilogue, comm=comm)


def _mm_out_res(name, a, w_g, layer, x, scale, comm=None):
    t_len = a.shape[0]
    ks, n = w_g.shape[2], w_g.shape[3]
    tm = _pick(t_len, (1024, 512))
    tn = _pick(n, (1024,))
    tk = _pick(ks, (1408, 512, 256))
    nks = ks // tk

    def epilogue(accs, e, o):
        o[0][...] = e[0][...] + scale * accs[0]

    return _matmul(
        name, (t_len // tm, n // tn, N_CHIPS * nks),
        [(a, pl.BlockSpec((tm, tk), lambda i, j, k: (i, k)))],
        [(w_g, pl.BlockSpec((None, None, tk, tn), lambda i, j, k: (layer, k // nks, k % nks, j)))],
        [[(0, 0)]], NN,
        [jax.ShapeDtypeStruct((t_len, n), F32)],
        [pl.BlockSpec((tm, tn), lambda i, j, k: (i, j))],
        epilogue, extra_ops=[(x, pl.BlockSpec((tm, tn), lambda i, j, k: (i, j)))],
        nk=N_CHIPS * nks, acc_shapes=[(tm, tn)], comm=comm)


def _mm_proj(h, w_g, layer, comm=None):
    t_len, d = h.shape
    cs = w_g.shape[3]
    tm = _pick(t_len, (1024, 512))

    def epilogue(accs, e, o):
        o[0][...] = accs[0]

    return _matmul(
        "mix_in", (t_len // tm, N_CHIPS),
        [(h, pl.BlockSpec((tm, d), lambda i, s: (i, 0)))],
        [(w_g, pl.BlockSpec((None, None, d, cs), lambda i, s: (layer, s, 0, 0)))],
        [[(0, 0)]], NN,
        [jax.ShapeDtypeStruct((t_len, N_CHIPS * cs), F32)],
        [pl.BlockSpec((tm, cs), lambda i, s: (i, s))],
        epilogue, comm=comm)


def _mm_dact_swiglu(dxb, w_g, layer, gu, scale, comm=None):
    t_len, d = dxb.shape
    rs = w_g.shape[2]
    tm = _pick(t_len, (512,))
    tn = _pick(rs, (1408, 256, 128))
    nj = rs // tn

    def epilogue(accs, e, o):
        dact = scale * accs[0]
        g = e[0][0].astype(F32)
        u = e[0][1].astype(F32)
        sig = _sigmoid(g)
        o[0][0] = (dact * u * (sig * (1.0 + g * (1.0 - sig)))).astype(BF16)
        o[0][1] = (dact * (g * sig)).astype(BF16)

    gu_spec = pl.BlockSpec((2, tm, tn), lambda i, s, j: (0, i, s * nj + j))
    return _matmul(
        "ffn_dact", (t_len // tm, N_CHIPS, nj),
        [(dxb, pl.BlockSpec((tm, d), lambda i, s, j: (i, 0)))],
        [(w_g, pl.BlockSpec((None, None, tn, d), lambda i, s, j: (layer, s, j, 0)))],
        [[(0, 0)]], NT,
        [jax.ShapeDtypeStruct(gu.shape, BF16)], [gu_spec],
        epilogue, extra_ops=[(gu, gu_spec)], comm=comm)[0]


def _mm_dh_ffn(dgu, w_g, layer, comm=None):
    t_len = dgu.shape[1]
    d, fs = w_g.shape[2], w_g.shape[3]
    tm = _pick(t_len, (1024, 512))
    tk = _pick(fs, (256, 128))
    nks = fs // tk
    nk = 2 * nks

    def epilogue(accs, e, o):
        o[0][...] = accs[0]

    return _matmul(
        "ffn_dh", (t_len // tm, nk),
        [(dgu, pl.BlockSpec((None, tm, tk), lambda i, k: (0, i, k))),
         (dgu, pl.BlockSpec((None, tm, tk), lambda i, k: (1, i, k)))],
        [(w_g, pl.BlockSpec((None, None, d, tk), lambda i, k: (layer, k // nks, 0, k % nks))),
         (w_g, pl.BlockSpec((None, None, d, tk), lambda i, k: (layer, k // nks + 2, 0, k % nks)))],
        [[(0, 0), (1, 1)]], NT,
        [jax.ShapeDtypeStruct((t_len, d), F32)],
        [pl.BlockSpec((tm, d), lambda i, k: (i, 0))],
        epilogue, nk=nk, acc_shapes=[(tm, d)], comm=comm)[0]


def _mm_dw(name, a, a_spec_of, b, b_spec_of, layer, stack, rows, cols, tn, scale, comm=None):
    t_len = a.shape[0]
    tt = _pick(t_len, (1024, 512))
    nj = cols // tn

    def epilogue(accs, e, o):
        o[0][...] = (scale * accs[0]).astype(BF16)

    shape = jax.ShapeDtypeStruct((2, N_CHIPS, rows, cols), BF16)
    return _matmul(
        name, (N_CHIPS, nj, t_len // tt),
        [(a, a_spec_of(tt))], [(b, b_spec_of(tt, tn, nj))],
        [[(0, 0)]], TN, [shape],
        [pl.BlockSpec((None, None, rows, tn), lambda s, j, t: (layer, s, 0, j))],
        epilogue, nk=t_len // tt, acc_shapes=[(rows, tn)],
        alias=None if stack is None else (stack, 0), comm=comm)[0]


def _dw_ffn_in(h, dgu, layer, stack, comm=None):
    d = h.shape[1]
    fs = dgu.shape[2] // 2
    tn = _pick(fs, (1408, 256))
    return _mm_dw(
        "ffn_dw_in", h, lambda tt: pl.BlockSpec((tt, d), lambda s, j, t: (t, 0)),
        dgu, lambda tt, tn_, nj: pl.BlockSpec((None, tt, tn_), lambda s, j, t: (s // 2, t, (s % 2) * nj + j)),
        layer, stack, d, fs, tn, 1.0, comm)


def _dw_rows(name, a, dxb, layer, stack, scale, comm=None):
    rs = a.shape[1] // N_CHIPS
    d = dxb.shape[1]
    tn = _pick(d, (1024,))
    return _mm_dw(
        name, a, lambda tt: pl.BlockSpec((tt, rs), lambda s, j, t: (t, s)),
        dxb, lambda tt, tn_, nj: pl.BlockSpec((tt, tn_), lambda s, j, t: (t, j)),
        layer, stack, rs, d, tn, scale, comm)


def _dw_mix_in(h, dp, layer, stack, comm=None):
    d = h.shape[1]
    cs = dp.shape[1] // N_CHIPS
    return _mm_dw(
        "mix_dw_in", h, lambda tt: pl.BlockSpec((tt, d), lambda s, j, t: (t, 0)),
        dp, lambda tt, tn_, nj: pl.BlockSpec((tt, tn_), lambda s, j, t: (t, s)),
        layer, stack, d, cs, cs, 1.0, comm)


def _mm_dmix(dxb, w_g, layer, comm=None):
    t_len, d = dxb.shape
    rs = w_g.shape[2]
    tm = _pick(t_len, (1024, 512))

    def epilogue(accs, e, o):
        o[0][...] = accs[0]

    return _matmul(
        "mix_dout", (t_len // tm, N_CHIPS),
        [(dxb, pl.BlockSpec((tm, d), lambda i, s: (i, 0)))],
        [(w_g, pl.BlockSpec((None, None, rs, d), lambda i, s: (layer, s, 0, 0)))],
        [[(0, 0)]], NT,
        [jax.ShapeDtypeStruct((t_len, N_CHIPS * rs), F32)],
        [pl.BlockSpec((tm, rs), lambda i, s: (i, s))],
        epilogue, comm=comm)[0]


def _mm_dh_mix(dp, w_g, layer, comm=None):
    t_len = dp.shape[0]
    d, cs = w_g.shape[2], w_g.shape[3]
    tm = _pick(t_len, (1024, 512))

    def epilogue(accs, e, o):
        o[0][...] = accs[0]

    return _matmul(
        "mix_dh", (t_len // tm, N_CHIPS),
        [(dp, pl.BlockSpec((tm, cs), lambda i, k: (i, k)))],
        [(w_g, pl.BlockSpec((None, None, d, cs), lambda i, k: (layer, k, 0, 0)))],
        [[(0, 0)]], NT,
        [jax.ShapeDtypeStruct((t_len, d), F32)],
        [pl.BlockSpec((tm, d), lambda i, k: (i, 0))],
        epilogue, nk=N_CHIPS, acc_shapes=[(tm, d)], comm=comm)[0]


def _rms_stats(x):
    r = lax.rsqrt(jnp.mean(x * x, axis=-1, keepdims=True) + NORM_EPS)
    return r, x * r


def _accumulate(ref, part, first):
    @pl.when(first)
    def _():
        ref[...] = part

    @pl.when(jnp.logical_not(first))
    def _():
        ref[...] += part


def _rmsnorm_fwd(x, g):
    t_len, d = x.shape
    tm = _pick(t_len, (512,))

    def body(x_ref, g_ref, h_ref):
        _, xhat = _rms_stats(x_ref[...])
        h_ref[...] = (xhat * g_ref[...]).astype(BF16)

    row = pl.BlockSpec((tm, d), lambda i: (i, 0))
    vec = pl.BlockSpec((1, d), lambda i: (0, 0))
    return _call(body, name="rmsnorm_fwd", grid=(t_len // tm,), in_specs=[row, vec], out_specs=row,
                 out_shape=jax.ShapeDtypeStruct((t_len, d), BF16), compiler_params=_params(1))(x, g)


def _rmsnorm_bwd(dh, x, g, dres):
    t_len, d = x.shape
    tm = _pick(t_len, (256,))

    def body(dh_ref, x_ref, g_ref, dres_ref, dx_ref, dxb_ref, dg_ref):
        r, xhat = _rms_stats(x_ref[...])
        dh_v = dh_ref[...]
        gd = dh_v * g_ref[...]
        dx = dres_ref[...] + r * (gd - xhat * jnp.mean(gd * xhat, axis=-1, keepdims=True))
        dx_ref[...] = dx
        dxb_ref[...] = dx.astype(BF16)
        _accumulate(dg_ref, jnp.sum(dh_v * xhat, axis=0, keepdims=True), pl.program_id(0) == 0)

    row = pl.BlockSpec((tm, d), lambda i: (i, 0))
    vec = pl.BlockSpec((1, d), lambda i: (0, 0))
    return _call(body, name="rmsnorm_bwd", grid=(t_len // tm,), in_specs=[row, row, vec, row],
                 out_specs=[row, row, vec],
                 out_shape=[jax.ShapeDtypeStruct((t_len, d), F32), jax.ShapeDtypeStruct((t_len, d), BF16),
                            jax.ShapeDtypeStruct((1, d), F32)],
                 compiler_params=_params(1))(dh, x, g, dres)


def _loss_head(x, g, target):
    t_len, d = x.shape
    tm = _pick(t_len, (256,))

    def body(x_ref, g_ref, t_ref, dx_ref, dxb_ref, dg_ref, loss_ref):
        first = pl.program_id(0) == 0
        r, xhat = _rms_stats(x_ref[...])
        g_v = g_ref[...]
        err = xhat * g_v - t_ref[...]
        per_token = jnp.mean(err * err, axis=-1, keepdims=True)
        part = 0.5 * jnp.sum(per_token, axis=0, keepdims=True)
        _accumulate(loss_ref, jnp.broadcast_to(part, (1, LANES)), first)
        dy = err * (1.0 / d)
        _accumulate(dg_ref, jnp.sum(dy * xhat, axis=0, keepdims=True), first)
        gd = dy * g_v
        dx = r * (gd - xhat * jnp.mean(gd * xhat, axis=-1, keepdims=True))
        dx_ref[...] = dx
        dxb_ref[...] = dx.astype(BF16)

    row = pl.BlockSpec((tm, d), lambda i: (i, 0))
    vec = pl.BlockSpec((1, d), lambda i: (0, 0))
    return _call(body, name="loss_head", grid=(t_len // tm,), in_specs=[row, vec, row],
                 out_specs=[row, row, vec, pl.BlockSpec((1, LANES), lambda i: (0, 0))],
                 out_shape=[jax.ShapeDtypeStruct((t_len, d), F32), jax.ShapeDtypeStruct((t_len, d), BF16),
                            jax.ShapeDtypeStruct((1, d), F32), jax.ShapeDtypeStruct((1, LANES), F32)],
                 compiler_params=_params(1))(x, g, target)


def _ln_stats(x):
    mu = jnp.mean(x, axis=-1, keepdims=True)
    xc = x - mu
    r = lax.rsqrt(jnp.mean(xc * xc, axis=-1, keepdims=True) + NORM_EPS)
    return r, xc * r


def _ln_bwd(dy, r, xhat, g):
    dxh = dy * g
    return r * (dxh - jnp.mean(dxh, axis=-1, keepdims=True) - xhat * jnp.mean(dxh * xhat, axis=-1, keepdims=True))


def _rope_tables(positions):
    t_len = positions.shape[-1]
    inv_freq = 1.0 / (ROPE_THETA ** (jnp.arange(0, 2 * ROT_HALF, 2, dtype=F32) / (2 * ROT_HALF)))
    ang = positions.astype(F32).reshape(t_len, 1) * inv_freq
    cos = jnp.tile(jnp.cos(ang), (1, LANES // ROT_HALF))
    sin = jnp.tile(jnp.sin(ang), (1, LANES // ROT_HALF))
    lane = jnp.arange(LANES) % HEAD_DIM
    c = jnp.where(lane < 2 * ROT_HALF, cos, 1.0)
    s1 = jnp.where(lane < ROT_HALF, -sin, 0.0)
    s2 = jnp.where((lane >= ROT_HALF) & (lane < 2 * ROT_HALF), sin, 0.0)
    return c.astype(F32), s1.astype(F32), s2.astype(F32)


def _rope_fwd(p, tables):
    t_len = p.shape[0]
    tm = _pick(t_len, (256,))
    n_rot = K_END // LANES

    def body(p_ref, c_ref, s1_ref, s2_ref, o_ref):
        c, s1, s2 = c_ref[...], s1_ref[...], s2_ref[...]
        for j in range(V_END // LANES):
            sl = slice(j * LANES, (j + 1) * LANES)
            t = p_ref[:, sl]
            if j < n_rot:
                t = t * c + pltpu.roll(t, LANES - ROT_HALF, 1) * s1 + pltpu.roll(t, ROT_HALF, 1) * s2
            if j < Q_END // LANES:
                t = t * ATTN_SCALE
            o_ref[:, sl] = t.astype(BF16)

    tab = pl.BlockSpec((tm, LANES), lambda i: (i, 0))
    blk = pl.BlockSpec((tm, V_END), lambda i: (i, 0))
    return _call(body, name="rope_fwd", grid=(t_len // tm,), in_specs=[blk, tab, tab, tab], out_specs=blk,
                 out_shape=jax.ShapeDtypeStruct((t_len, V_END), BF16), compiler_params=_params(1))(p, *tables)


def _assemble_dp(dq, dkc, dkp, dvc, dvp, tables, dalin, dagate, du, dvin):
    t_len = dq.shape[0]
    steps = t_len // ATTN_STEP

    def body(dq_ref, dkc_ref, dkp_ref, dvc_ref, dvp_ref, c_ref, s1_ref, s2_ref, dalin_ref, dagate_ref,
             du_ref, dvin_ref, o_ref):
        keep = (pl.program_id(0) < steps - 1).astype(F32)
        for part in range(ATTN_STEP // BLK):
            rows = slice(part * BLK, (part + 1) * BLK)
            c, s1, s2 = c_ref[rows, :], s1_ref[rows, :], s2_ref[rows, :]

            def unrotate(dr):
                return dr * c + pltpu.roll(dr * s1, ROT_HALF, 1) + pltpu.roll(dr * s2, LANES - ROT_HALF, 1)

            for j in range(Q_END // LANES):
                sl = slice(j * LANES, (j + 1) * LANES)
                o_ref[rows, sl] = unrotate(dq_ref[rows, sl]).astype(BF16)
            for j in range((K_END - Q_END) // LANES):
                sl = slice(j * LANES, (j + 1) * LANES)
                dk, dv = dkc_ref[rows, sl], dvc_ref[rows, sl]
                if part == ATTN_STEP // BLK - 1:
                    dk = dk + keep * dkp_ref[:, sl]
                    dv = dv + keep * dvp_ref[:, sl]
                o_ref[rows, Q_END + j * LANES:Q_END + (j + 1) * LANES] = unrotate(dk).astype(BF16)
                o_ref[rows, K_END + j * LANES:K_END + (j + 1) * LANES] = dv.astype(BF16)
        o_ref[:, V_END:V_END + CONV_CH] = dalin_ref[...]
        o_ref[:, V_END + CONV_CH:CONV_END] = dagate_ref[...]
        o_ref[:, CONV_END:CONV_END + SGU_CH] = du_ref[...]
        o_ref[:, CONV_END + SGU_CH:IN_COLS] = dvin_ref[...]

    def cur(w):
        return pl.BlockSpec((ATTN_STEP, w), lambda i: (i, 0))

    def nxt(w):
        return pl.BlockSpec((BLK, w), lambda i: (jnp.minimum(i + 1, steps - 1), 0))

    kvw = K_END - Q_END
    return _call(body, name="assemble_dp", grid=(steps,),
                 in_specs=[cur(Q_END), cur(kvw), nxt(kvw), cur(kvw), nxt(kvw), cur(LANES), cur(LANES), cur(LANES),
                           cur(CONV_CH), cur(CONV_CH), cur(SGU_CH), cur(SGU_CH)],
                 out_specs=cur(IN_COLS), out_shape=jax.ShapeDtypeStruct((t_len, IN_COLS), BF16),
                 compiler_params=_params(1))(dq, dkc, dkp, dvc, dvp, *tables, dalin, dagate, du, dvin)


ATTN_STEP = 2 * BLK
ATTN_SCALE = HEAD_DIM ** -0.5
MASKED = -1e30


def _attn_bias():
    qi = jnp.arange(GQ * BLK)[None, :] % BLK
    kj = jnp.arange(2 * BLK)[:, None]
    dist = qi + BLK - kj
    band = (dist >= 0) & (dist < BLK)
    return jnp.stack([jnp.where(band & (kj >= BLK), 0.0, MASKED), jnp.where(band, 0.0, MASKED)]).astype(F32)


def _side_by_side(ref, cols):
    return jnp.concatenate([ref[h, :, cols] for h in range(GQ)], axis=1)


def _attn_chains(qt_ref, kp_ref, kc_ref):
    kc = kc_ref[...]
    pos_a, pos_b = slice(0, BLK), slice(BLK, ATTN_STEP)
    return [(pos_a, _side_by_side(qt_ref, pos_a), jnp.concatenate([kp_ref[...], kc[:BLK]], axis=0)),
            (pos_b, _side_by_side(qt_ref, pos_b), kc)]


def _attn_weights(qt, kk, bias, snk):
    s = lax.dot_general(kk, qt, NN, preferred_element_type=F32) + bias
    m = jnp.maximum(jnp.max(s, axis=0, keepdims=True), snk)
    e = jnp.exp(s - m)
    es = jnp.exp(snk - m)
    return e, es, 1.0 / (jnp.sum(e, axis=0, keepdims=True) + es)


def _attn_specs():
    before = lambda n: jnp.maximum(2 * n - 1, 0)
    return dict(
        qt=pl.BlockSpec((GQ, HEAD_DIM, ATTN_STEP), lambda g, n: (g, 0, n)),
        q=pl.BlockSpec((GQ, ATTN_STEP, HEAD_DIM), lambda g, n: (g, n, 0)),
        cur=pl.BlockSpec((None, ATTN_STEP, HEAD_DIM), lambda g, n: (g, n, 0)),
        prev=pl.BlockSpec((None, BLK, HEAD_DIM), lambda g, n: (g, before(n), 0)),
        cur_t=pl.BlockSpec((None, HEAD_DIM, ATTN_STEP), lambda g, n: (g, 0, n)),
        prev_t=pl.BlockSpec((None, HEAD_DIM, BLK), lambda g, n: (g, 0, before(n))),
        snk=pl.BlockSpec((None, 1, GQ * BLK), lambda g, n: (g, 0, 0)),
        bias_a=pl.BlockSpec((None, 2 * BLK, GQ * BLK), lambda g, n: (jnp.minimum(n, 1), 0, 0)),
        bias_b=pl.BlockSpec((None, 2 * BLK, GQ * BLK), lambda g, n: (1, 0, 0)))


def _attn_fwd(qt, k, vt, snk, comm=None):
    steps = qt.shape[2] // ATTN_STEP

    def body(qt_ref, kp_ref, kc_ref, vtp_ref, vtc_ref, snk_ref, ba_ref, bb_ref, o_ref):
        snk = snk_ref[...]
        vtc = vtc_ref[...]
        values = (jnp.concatenate([vtp_ref[...], vtc[:, :BLK]], axis=1), vtc)
        for (pos, qt_v, kk), vt_v, b_ref in zip(_attn_chains(qt_ref, kp_ref, kc_ref), values, (ba_ref, bb_ref)):
            e, _, inv = _attn_weights(qt_v, kk, b_ref[...], snk)
            o = lax.dot_general(vt_v, e.astype(BF16), NN, preferred_element_type=F32) * inv
            for h in range(GQ):
                o_ref[h, :, pos] = o[:, h * BLK:(h + 1) * BLK].astype(BF16)

    sp = _attn_specs()
    bias = _attn_bias()
    return _hosted_call(body, comm, "attn_fwd", (N_KV_HEADS, steps), [qt, k, k, vt, vt, snk, bias, bias],
                        [sp["qt"], sp["prev"], sp["cur"], sp["prev_t"], sp["cur_t"], sp["snk"], sp["bias_a"],
                         sp["bias_b"]], [jax.ShapeDtypeStruct(qt.shape, BF16)], [sp["qt"]])[0]


def _attn_bwd(qt, q, k, kt, v, snk, dot_, do, comm=None):
    steps = qt.shape[2] // ATTN_STEP

    def body(qt_ref, q_ref, kp_ref, kc_ref, ktp_ref, ktc_ref, vp_ref, vc_ref, snk_ref, ba_ref, bb_ref, dot_ref,
             do_ref, dq_ref, dkp_ref, dkc_ref, dvp_ref, dvc_ref, dsnk_ref):
        snk = snk_ref[...]
        vc, ktc = vc_ref[...], ktc_ref[...]
        values = (jnp.concatenate([vp_ref[...], vc[:BLK]], axis=0), vc)
        keys_t = (jnp.concatenate([ktp_ref[...], ktc[:, :BLK]], axis=1), ktc)
        row = lax.broadcasted_iota(jnp.int32, (SUBLANES, LANES), 0)
        tile = jnp.zeros((SUBLANES, LANES), F32)
        grads = []
        for (pos, qt_v, kk), vv, kt_v, b_ref in zip(_attn_chains(qt_ref, kp_ref, kc_ref), values, keys_t,
                                                    (ba_ref, bb_ref)):
            e, es, inv = _attn_weights(qt_v, kk, b_ref[...], snk)
            p = e * inv
            dp = lax.dot_general(vv, _side_by_side(dot_ref, pos), NN, preferred_element_type=F32)
            delta = jnp.sum(p * dp, axis=0, keepdims=True)
            ds = (p * (dp - delta)).astype(BF16)
            dq = lax.dot_general(kt_v, ds, NN, preferred_element_type=F32) * ATTN_SCALE
            for h in range(GQ):
                dq_ref[h, :, pos] = dq[:, h * BLK:(h + 1) * BLK]
            q_v = q_ref[:, pos, :].reshape(GQ * BLK, HEAD_DIM)
            do_v = do_ref[:, pos, :].reshape(GQ * BLK, HEAD_DIM)
            grads.append((lax.dot_general(ds, q_v, NN, preferred_element_type=F32),
                          lax.dot_general(p.astype(BF16), do_v, NN, preferred_element_type=F32)))
            per_query = -(es * inv) * delta
            for hh in range(GQ):
                tot = jnp.sum(per_query[:, hh * BLK:(hh + 1) * BLK], axis=1, keepdims=True)
                tile = tile + jnp.where(row == hh, tot, 0.0)
        (dk_a, dv_a), (dk_b, dv_b) = grads
        dkp_ref[...] = dk_a[:BLK]
        dvp_ref[...] = dv_a[:BLK]
        dkc_ref[0:BLK, :] = dk_a[BLK:] + dk_b[:BLK]
        dvc_ref[0:BLK, :] = dv_a[BLK:] + dv_b[:BLK]
        dkc_ref[BLK:ATTN_STEP, :] = dk_b[BLK:]
        dvc_ref[BLK:ATTN_STEP, :] = dv_b[BLK:]
        _accumulate(dsnk_ref, tile, pl.program_id(1) == 0)

    sp = _attn_specs()
    bias = _attn_bias()
    step_blk = pl.BlockSpec((None, BLK, HEAD_DIM), lambda g, n: (g, n, 0))
    kv_shape = jax.ShapeDtypeStruct(k.shape, F32)
    prev_shape = jax.ShapeDtypeStruct((N_KV_HEADS, steps * BLK, HEAD_DIM), F32)
    return _hosted_call(
        body, comm, "attn_bwd", (N_KV_HEADS, steps), [qt, q, k, k, kt, kt, v, v, snk, bias, bias, dot_, do],
        [sp["qt"], sp["q"], sp["prev"], sp["cur"], sp["prev_t"], sp["cur_t"], sp["prev"], sp["cur"], sp["snk"],
         sp["bias_a"], sp["bias_b"], sp["qt"], sp["q"]],
        [jax.ShapeDtypeStruct(qt.shape, F32), prev_shape, kv_shape, prev_shape, kv_shape,
         jax.ShapeDtypeStruct((N_KV_HEADS, SUBLANES, LANES), F32)],
        [sp["qt"], step_blk, sp["cur"], step_blk, sp["cur"],
         pl.BlockSpec((None, SUBLANES, LANES), lambda g, n: (g, 0, 0))])


CONV_CHUNK = 256


def _shift_up(win, s):
    n = win.shape[0]
    return win if s == 0 else pltpu.roll(win, n - s, 0)


def _conv_col_specs(t_len):
    lin = pl.BlockSpec((t_len, LANES), lambda j: (0, V_END // LANES + j))
    gate = pl.BlockSpec((t_len, LANES), lambda j: (0, (V_END + CONV_CH) // LANES + j))
    col = pl.BlockSpec((t_len, LANES), lambda j: (0, j))
    wsp = pl.BlockSpec((CONV_PAD, LANES), lambda j: (0, j))
    return lin, gate, col, wsp


def _conv_fwd(p, w, b):
    t_len = p.shape[0]
    ch = CONV_CHUNK

    def body(lin_ref, gate_ref, w_ref, b_ref, y_ref, hp_ref):
        hp_ref[0:CONV_PAD, :] = jnp.zeros((CONV_PAD, LANES), F32)

        def fill(c, carry):
            r0 = pl.multiple_of(c * ch, ch)
            hp_ref[pl.ds(r0 + CONV_PAD, ch), :] = lin_ref[pl.ds(r0, ch), :] * _sigmoid(gate_ref[pl.ds(r0, ch), :])
            return carry

        lax.fori_loop(0, t_len // ch, fill, 0)

        def conv(c, carry):
            r0 = pl.multiple_of(c * ch, ch)
            win = hp_ref[pl.ds(r0, ch + CONV_PAD), :]
            acc = jnp.zeros((ch, LANES), F32)
            for k in range(CONV_WIDTH):
                acc = acc + _shift_up(win, CONV_PAD - (CONV_WIDTH - 1) + k)[:ch] * w_ref[k:k + 1, :]
            y_ref[pl.ds(r0, ch), :] = acc + b_ref[...]
            return carry

        lax.fori_loop(0, t_len // ch, conv, 0)

    lin, gate, col, wsp = _conv_col_specs(t_len)
    return _call(body, name="conv_fwd", grid=(CONV_CH // LANES,),
                 in_specs=[lin, gate, wsp, pl.BlockSpec((1, LANES), lambda j: (0, j))], out_specs=col,
                 out_shape=jax.ShapeDtypeStruct((t_len, CONV_CH), F32),
                 scratch_shapes=[pltpu.VMEM((t_len + CONV_PAD, LANES), F32)],
                 compiler_params=_params(1))(p, p, w, b)


def _conv_post_fwd(y, g, b):
    t_len = y.shape[0]
    tm = _pick(t_len, (512,))

    def body(y_ref, g_ref, b_ref, o_ref):
        _, xhat = _ln_stats(y_ref[...])
        z = xhat * g_ref[...] + b_ref[...]
        o_ref[...] = (z * _sigmoid(z)).astype(BF16)

    row = pl.BlockSpec((tm, CONV_CH), lambda i: (i, 0))
    vec = pl.BlockSpec((1, CONV_CH), lambda i: (0, 0))
    return _call(body, name="conv_post_fwd", grid=(t_len // tm,), in_specs=[row, vec, vec], out_specs=row,
                 out_shape=jax.ShapeDtypeStruct((t_len, CONV_CH), BF16), compiler_params=_params(1))(y, g, b)


def _conv_post_bwd(dmix, y, g, b):
    t_len = y.shape[0]
    tm = _pick(t_len, (512,))

    def body(do_ref, y_ref, g_ref, b_ref, dy_ref, dg_ref, db_ref, dcb_ref):
        first = pl.program_id(0) == 0
        r, xhat = _ln_stats(y_ref[...])
        g_v = g_ref[...]
        z = xhat * g_v + b_ref[...]
        sig = _sigmoid(z)
        dz = do_ref[...] * (sig * (1.0 + z * (1.0 - sig)))
        _accumulate(db_ref, jnp.sum(dz, axis=0, keepdims=True), first)
        _accumulate(dg_ref, jnp.sum(dz * xhat, axis=0, keepdims=True), first)
        dy = _ln_bwd(dz, r, xhat, g_v)
        dy_ref[...] = dy
        _accumulate(dcb_ref, jnp.sum(dy, axis=0, keepdims=True), first)

    row = pl.BlockSpec((tm, CONV_CH), lambda i: (i, 0))
    do_spec = pl.BlockSpec((tm, CONV_CH), lambda i: (i, Q_END // CONV_CH))
    vec = pl.BlockSpec((1, CONV_CH), lambda i: (0, 0))
    vshape = jax.ShapeDtypeStruct((1, CONV_CH), F32)
    return _call(body, name="conv_post_bwd", grid=(t_len // tm,), in_specs=[do_spec, row, vec, vec],
                 out_specs=[row, vec, vec, vec],
                 out_shape=[jax.ShapeDtypeStruct((t_len, CONV_CH), F32), vshape, vshape, vshape],
                 compiler_params=_params(1))(dmix, y, g, b)


def _conv_bwd(p, dy, w):
    t_len = p.shape[0]
    ch = CONV_CHUNK

    def body(lin_ref, gate_ref, dy_ref, w_ref, dlin_ref, dgate_ref, dw_ref, hp_ref, dyp_ref):
        hp_ref[0:CONV_PAD, :] = jnp.zeros((CONV_PAD, LANES), F32)
        dyp_ref[t_len:t_len + CONV_PAD, :] = jnp.zeros((CONV_PAD, LANES), F32)
        dw_ref[...] = jnp.zeros((CONV_PAD, LANES), F32)

        def fill(c, carry):
            r0 = pl.multiple_of(c * ch, ch)
            hp_ref[pl.ds(r0 + CONV_PAD, ch), :] = lin_ref[pl.ds(r0, ch), :] * _sigmoid(gate_ref[pl.ds(r0, ch), :])
            dyp_ref[pl.ds(r0, ch), :] = dy_ref[pl.ds(r0, ch), :]
            return carry

        lax.fori_loop(0, t_len // ch, fill, 0)

        def step(c, carry):
            r0 = pl.multiple_of(c * ch, ch)
            win_h = hp_ref[pl.ds(r0, ch + CONV_PAD), :]
            win_dy = dyp_ref[pl.ds(r0, ch + CONV_PAD), :]
            dyc = win_dy[:ch]
            dh = jnp.zeros((ch, LANES), F32)
            for k in range(CONV_WIDTH):
                tap = _shift_up(win_h, CONV_PAD - (CONV_WIDTH - 1) + k)[:ch]
                dw_ref[k:k + 1, :] += jnp.sum(dyc * tap, axis=0, keepdims=True)
                dh = dh + _shift_up(win_dy, CONV_WIDTH - 1 - k)[:ch] * w_ref[k:k + 1, :]
            lin = lin_ref[pl.ds(r0, ch), :]
            sig = _sigmoid(gate_ref[pl.ds(r0, ch), :])
            dlin_ref[pl.ds(r0, ch), :] = (dh * sig).astype(BF16)
            dgate_ref[pl.ds(r0, ch), :] = (dh * lin * (sig * (1.0 - sig))).astype(BF16)
            return carry

        lax.fori_loop(0, t_len // ch, step, 0)

    lin, gate, col, wsp = _conv_col_specs(t_len)
    half = jax.ShapeDtypeStruct((t_len, CONV_CH), BF16)
    return _call(body, name="conv_bwd", grid=(CONV_CH // LANES,), in_specs=[lin, gate, col, wsp],
                 out_specs=[col, col, wsp],
                 out_shape=[half, half, jax.ShapeDtypeStruct((CONV_PAD, CONV_CH), F32)],
                 scratch_shapes=[pltpu.VMEM((t_len + CONV_PAD, LANES), F32), pltpu.VMEM((t_len + CONV_PAD, LANES), F32)],
                 compiler_params=_params(1))(p, p, dy, w)


def _sgu_mixed(v, w_ref, bt_ref, j):
    lane = lax.broadcasted_iota(jnp.int32, (BLK, LANES), 1)
    lo = lane < HEAD_DIM
    tri = lax.broadcasted_iota(jnp.int32, (BLK, BLK), 0) >= lax.broadcasted_iota(jnp.int32, (BLK, BLK), 1)
    vs = v[:, j * LANES:(j + 1) * LANES]
    v_lo = jnp.where(lo, vs, 0.0).astype(BF16)
    v_hi = jnp.where(lo, 0.0, vs).astype(BF16)
    w_lo = jnp.where(tri, w_ref[2 * j], 0.0).astype(BF16)
    w_hi = jnp.where(tri, w_ref[2 * j + 1], 0.0).astype(BF16)
    m = (lax.dot_general(w_lo, v_lo, NN, preferred_element_type=F32)
         + lax.dot_general(w_hi, v_hi, NN, preferred_element_type=F32))
    bias = jnp.where(lo, bt_ref[:, 2 * j:2 * j + 1], bt_ref[:, 2 * j + 1:2 * j + 2])
    return m + bias, (v_lo, v_hi, w_lo, w_hi, lo, tri)


def _sgu_specs():
    u_spec = pl.BlockSpec((BLK, SGU_CH), lambda i: (i, CONV_END // SGU_CH))
    v_spec = pl.BlockSpec((BLK, SGU_CH), lambda i: (i, CONV_END // SGU_CH + 1))
    vec = pl.BlockSpec((1, SGU_CH), lambda i: (0, 0))
    w_spec = pl.BlockSpec((SGU_HEADS, BLK, BLK), lambda i: (0, 0, 0))
    bt_spec = pl.BlockSpec((BLK, SGU_HEADS), lambda i: (0, 0))
    row = pl.BlockSpec((BLK, SGU_CH), lambda i: (i, 0))
    return u_spec, v_spec, vec, w_spec, bt_spec, row


def _sgu_fwd(p, g, b, w, bt):
    t_len = p.shape[0]

    def body(u_ref, vin_ref, g_ref, b_ref, w_ref, bt_ref, o_ref):
        _, xhat = _ln_stats(vin_ref[...])
        v = xhat * g_ref[...] + b_ref[...]
        for j in range(SGU_CH // LANES):
            m, _ = _sgu_mixed(v, w_ref, bt_ref, j)
            sl = slice(j * LANES, (j + 1) * LANES)
            o_ref[:, sl] = (u_ref[:, sl] * m).astype(BF16)

    u_spec, v_spec, vec, w_spec, bt_spec, row = _sgu_specs()
    return _call(body, name="sgu_fwd", grid=(t_len // BLK,), in_specs=[u_spec, v_spec, vec, vec, w_spec, bt_spec],
                 out_specs=row, out_shape=jax.ShapeDtypeStruct((t_len, SGU_CH), BF16),
                 compiler_params=_params(1))(p, p, g, b, w, bt)


def _sgu_bwd(p, dmix, g, b, w, bt):
    t_len = p.shape[0]

    def body(u_ref, vin_ref, do_ref, g_ref, b_ref, w_ref, bt_ref, du_ref, dvin_ref, dw_ref, dbt_ref, dg_ref,
             db_ref, dv_ref):
        first = pl.program_id(0) == 0
        r, xhat = _ln_stats(vin_ref[...])
        g_v = g_ref[...]
        v = xhat * g_v + b_ref[...]
        lane = lax.broadcasted_iota(jnp.int32, (BLK, LANES), 1)
        dbt = jnp.zeros((BLK, LANES), F32)

        @pl.when(first)
        def _():
            dw_ref[...] = jnp.zeros((SGU_HEADS, BLK, BLK), F32)

        for j in range(SGU_CH // LANES):
            m, (v_lo, v_hi, w_lo, w_hi, lo, tri) = _sgu_mixed(v, w_ref, bt_ref, j)
            sl = slice(j * LANES, (j + 1) * LANES)
            do_v = do_ref[:, sl]
            du_ref[:, sl] = (do_v * m).astype(BF16)
            dm = do_v * u_ref[:, sl]
            dm_lo = jnp.where(lo, dm, 0.0)
            dm_hi = jnp.where(lo, 0.0, dm)
            dbt = dbt + jnp.where(lane == 2 * j, jnp.sum(dm_lo, axis=-1, keepdims=True), 0.0)
            dbt = dbt + jnp.where(lane == 2 * j + 1, jnp.sum(dm_hi, axis=-1, keepdims=True), 0.0)
            dm_lo, dm_hi = dm_lo.astype(BF16), dm_hi.astype(BF16)
            dw_ref[2 * j] += jnp.where(tri, lax.dot_general(dm_lo, v_lo, NT, preferred_element_type=F32), 0.0)
            dw_ref[2 * j + 1] += jnp.where(tri, lax.dot_general(dm_hi, v_hi, NT, preferred_element_type=F32), 0.0)
            dv_ref[:, sl] = (lax.dot_general(w_lo, dm_lo, TN, preferred_element_type=F32)
                             + lax.dot_general(w_hi, dm_hi, TN, preferred_element_type=F32))
        _accumulate(dbt_ref, dbt, first)
        dv = dv_ref[...]
        _accumulate(db_ref, jnp.sum(dv, axis=0, keepdims=True), first)
        _accumulate(dg_ref, jnp.sum(dv * xhat, axis=0, keepdims=True), first)
        dvin_ref[...] = _ln_bwd(dv, r, xhat, g_v).astype(BF16)

    u_spec, v_spec, vec, w_spec, bt_spec, row = _sgu_specs()
    do_spec = pl.BlockSpec((BLK, SGU_CH), lambda i: (i, (Q_END + CONV_CH) // SGU_CH))
    half = jax.ShapeDtypeStruct((t_len, SGU_CH), BF16)
    vshape = jax.ShapeDtypeStruct((1, SGU_CH), F32)
    return _call(body, name="sgu_bwd", grid=(t_len // BLK,),
                 in_specs=[u_spec, v_spec, do_spec, vec, vec, w_spec, bt_spec],
                 out_specs=[row, row, w_spec, pl.BlockSpec((BLK, LANES), lambda i: (0, 0)), vec, vec],
                 out_shape=[half, half, jax.ShapeDtypeStruct((SGU_HEADS, BLK, BLK), F32),
                            jax.ShapeDtypeStruct((BLK, LANES), F32), vshape, vshape],
                 scratch_shapes=[pltpu.VMEM((BLK, SGU_CH), F32)],
                 compiler_params=_params(1))(p, p, dmix, g, b, w, bt)


def _place():
    x, y, c = lax.axis_index("x"), lax.axis_index("y"), lax.axis_index("c")
    chips = [(1 - x, y), (x, 1 - y), (1 - x, 1 - y)]
    return x, y, c, chips


def _hbm_specs(n):
    return [pl.BlockSpec(memory_space=pltpu.HBM)] * n


def _comm_params():
    return pltpu.CompilerParams(has_side_effects=True)


def _remote(src, dst, send_sem, recv_sem, to):
    return pltpu.make_async_remote_copy(src_ref=src, dst_ref=dst, send_sem=send_sem, recv_sem=recv_sem,
                                        device_id=to, device_id_type=MESH_ID)


def _cast_place(w_local, chip):
    n, rows, cols = w_local.shape

    def body(chip_ref, w_ref, o_ref):
        o_ref[...] = w_ref[...].astype(BF16)

    grid_spec = pltpu.PrefetchScalarGridSpec(
        num_scalar_prefetch=1, grid=(n, rows // ROW_TILE),
        in_specs=[pl.BlockSpec((None, ROW_TILE, cols), lambda l, i, ch: (l, i, 0))],
        out_specs=pl.BlockSpec((None, None, ROW_TILE, cols), lambda l, i, ch: (l, ch[0], i, 0)))
    return _call(body, name="cast_place", grid_spec=grid_spec,
                 out_shape=jax.ShapeDtypeStruct((n, N_CHIPS, rows, cols), BF16), compiler_params=_params(2))(chip, w_local)


def _all_gather_weights(placed, shards):
    n_placed, nt = len(placed), len(placed) + len(shards)

    def body(*refs):
        ins, outs = refs[:nt], refs[nt:2 * nt]
        ici_send, ici_recv, d2d_send, d2d_recv, local_sem = refs[2 * nt:]
        x, y, c, chips = _place()
        me = 2 * x + y
        sibling = (x, y, 1 - c)
        local = [pltpu.make_async_copy(ins[t].at[l], outs[t].at[l, me], local_sem.at[2 * (t - n_placed) + l])
                 for t in range(n_placed, nt) for l in range(2)]
        for cp in local:
            cp.start()
        sends = []
        for t in range(nt):
            src = outs[t].at[c, me] if t < n_placed else ins[t].at[c]
            for j, (px, py) in enumerate(chips):
                sends.append(_remote(src, outs[t].at[c, me], ici_send.at[3 * t + j], ici_recv.at[3 * t + j],
                                     (px, py, c)))
        for cp in sends:
            cp.start()
        for t in range(nt):
            for j, (px, py) in enumerate(chips):
                slab = outs[t].at[c, 2 * px + py]
                _remote(slab, slab, ici_send.at[3 * t + j], ici_recv.at[3 * t + j], (px, py, c)).wait_recv()
                fwd = _remote(slab, slab, d2d_send.at[3 * t + j], d2d_recv.at[3 * t + j], sibling)
                fwd.start()
                sends.append(fwd)
        for t in range(nt):
            for j, (px, py) in enumerate(chips):
                slab = outs[t].at[1 - c, 2 * px + py]
                _remote(slab, slab, d2d_send.at[3 * t + j], d2d_recv.at[3 * t + j], sibling).wait_recv()
        for cp in sends:
            cp.wait_send()
        for cp in local:
            cp.wait()

    out_shape = [jax.ShapeDtypeStruct(p.shape, p.dtype) for p in placed]
    out_shape += [jax.ShapeDtypeStruct((2, N_CHIPS) + s.shape[1:], s.dtype) for s in shards]
    sems = [pltpu.SemaphoreType.DMA((3 * nt,))] * 4 + [pltpu.SemaphoreType.DMA((2 * len(shards),))]
    return _call(body, name="all_gather_weights", in_specs=_hbm_specs(nt), out_specs=_hbm_specs(nt),
                 out_shape=out_shape, scratch_shapes=sems, input_output_aliases={t: t for t in range(n_placed)},
                 compiler_params=_comm_params())(*placed, *shards)


def _gather_comm(bufs, pieces):
    n = len(pieces)
    sems = [pltpu.SemaphoreType.DMA((3 * n,))] * 4

    def half(ref, layer, chip, r0, nr, which):
        return ref.at[layer, chip, pl.ds(pl.multiple_of(r0 + which * (nr // 2), SUBLANES), nr // 2)]

    def start(rd, al, fr, sm):
        ici_send, ici_recv, _, _ = sm
        x, y, c, chips = _place()
        for i, (t, layer, r0, nr) in enumerate(pieces):
            own = half(al[t], layer, 2 * x + y, r0, nr, c)
            for j, (px, py) in enumerate(chips):
                _remote(own, own, ici_send.at[3 * i + j], ici_recv.at[3 * i + j], (px, py, c)).start()

    def finish(rd, al, fr, sm):
        ici_send, ici_recv, d2d_send, d2d_recv = sm
        x, y, c, chips = _place()
        sibling = (x, y, 1 - c)
        passed = []
        for i, (t, layer, r0, nr) in enumerate(pieces):
            for j, (px, py) in enumerate(chips):
                got = half(al[t], layer, 2 * px + py, r0, nr, c)
                _remote(got, got, ici_send.at[3 * i + j], ici_recv.at[3 * i + j], (px, py, c)).wait_recv()
                fwd = _remote(got, got, d2d_send.at[3 * i + j], d2d_recv.at[3 * i + j], sibling)
                fwd.start()
                passed.append(fwd)
        for i, (t, layer, r0, nr) in enumerate(pieces):
            own = half(al[t], layer, 2 * x + y, r0, nr, c)
            for j, (px, py) in enumerate(chips):
                _remote(own, own, ici_send.at[3 * i + j], ici_recv.at[3 * i + j], (px, py, c)).wait_send()
                theirs = half(al[t], layer, 2 * px + py, r0, nr, 1 - c)
                _remote(theirs, theirs, d2d_send.at[3 * i + j], d2d_recv.at[3 * i + j], sibling).wait_recv()
        for fwd in passed:
            fwd.wait_send()

    return _Comm([], bufs, [], sems, start, finish)


def _own_rows(ref, c, which=0):
    hr = ref.shape[-2] // 2
    start = pl.multiple_of((c if which == 0 else 1 - c) * hr, SUBLANES)
    return ref.at[(slice(None),) * (len(ref.shape) - 2) + (pl.ds(start, hr),)]


def _to_owner_comm(stacks, layer):
    nt = len(stacks)
    sems = [pltpu.SemaphoreType.DMA((nt,))] * 2
    fresh = [jax.ShapeDtypeStruct((N_CHIPS, s.shape[2] // 2, s.shape[3]), s.dtype) for s in stacks]

    def copies(rd, fr, sm):
        x, y, c, _ = _place()
        return [_remote(_own_rows(rd[t].at[layer], c, 1), fr[t], sm[0].at[t], sm[1].at[t], (x, y, 1 - c))
                for t in range(nt)]

    def start(rd, al, fr, sm):
        for cp in copies(rd, fr, sm):
            cp.start()

    def finish(rd, al, fr, sm):
        for cp in copies(rd, fr, sm):
            cp.wait()

    return _Comm(stacks, [], fresh, sems, start, finish)


def _chip_comm(partials):
    nt = len(partials)
    sems = [pltpu.SemaphoreType.DMA((3 * nt,))] * 2
    fresh = [jax.ShapeDtypeStruct((3,) + p.shape[1:], p.dtype) for p in partials]

    def each(rd, fr, sm, act):
        x, y, c, chips = _place()
        for t in range(nt):
            for j, (px, py) in enumerate(chips):
                act(_remote(rd[t].at[2 * px + py], fr[t].at[j], sm[0].at[3 * t + j], sm[1].at[3 * t + j], (px, py, c)))

    def start(rd, al, fr, sm):
        each(rd, fr, sm, lambda cp: cp.start())

    def finish(rd, al, fr, sm):
        each(rd, fr, sm, lambda cp: cp.wait())

    return _Comm(partials, [], fresh, sems, start, finish)


def _from_owner_comm(finals, layer):
    nt = len(finals)
    sems = [pltpu.SemaphoreType.DMA((nt,))] * 2

    def start(rd, al, fr, sm):
        x, y, c, _ = _place()
        for t in range(nt):
            mine = _own_rows(al[t].at[layer], c)
            _remote(mine, mine, sm[0].at[t], sm[1].at[t], (x, y, 1 - c)).start()

    def finish(rd, al, fr, sm):
        x, y, c, _ = _place()
        for t in range(nt):
            mine, theirs = _own_rows(al[t].at[layer], c), _own_rows(al[t].at[layer], c, 1)
            _remote(mine, mine, sm[0].at[t], sm[1].at[t], (x, y, 1 - c)).wait_send()
            _remote(theirs, theirs, sm[0].at[t], sm[1].at[t], (x, y, 1 - c)).wait_recv()

    return _Comm([], finals, [], sems, start, finish)


def _all_reduce_small(buf):
    rows = buf.shape[0]

    def body(x_ref, out_ref, all_ref, send_sems, recv_sems, local_sem):
        x, y, c, chips = _place()
        me, sibling = (x, y, c), (x, y, 1 - c)

        def block(px, py, pc):
            return all_ref.at[pl.ds((4 * px + 2 * py + pc) * rows, rows), :]

        def copy(k, blk, to, src=None):
            return _remote(block(*blk) if src is None else src, block(*blk), send_sems.at[k], recv_sems.at[k], to)

        mine = pltpu.make_async_copy(x_ref, block(*me), local_sem)
        mine.start()
        first = [copy(0, me, sibling, src=x_ref)]
        first += [copy(1 + j, me, (*chip, c), src=x_ref) for j, chip in enumerate(chips)]
        for cp in first:
            cp.start()
        passed = [copy(4 + j, (*chip, c), sibling) for j, chip in enumerate(chips)]
        for j, chip in enumerate(chips):
            copy(1 + j, (*chip, c), me).wait_recv()
            passed[j].start()
        copy(0, sibling, me).wait_recv()
        for j, chip in enumerate(chips):
            copy(4 + j, (*chip, 1 - c), me).wait_recv()
        for cp in first + passed:
            cp.wait_send()
        mine.wait()
        tot = all_ref[0:rows, :]
        for k in range(1, N_DEV):
            tot = tot + all_ref[k * rows:(k + 1) * rows, :]
        out_ref[...] = tot

    vm = pl.BlockSpec(memory_space=pltpu.VMEM)
    return _call(body, name="all_reduce_small", in_specs=[vm], out_specs=vm,
                 out_shape=jax.ShapeDtypeStruct(buf.shape, F32),
                 scratch_shapes=[pltpu.VMEM((N_DEV * rows, LANES), F32), pltpu.SemaphoreType.DMA((7,)),
                                 pltpu.SemaphoreType.DMA((7,)), pltpu.SemaphoreType.DMA],
                 compiler_params=pltpu.CompilerParams(has_side_effects=True,
                                                      vmem_limit_bytes=V7X_VMEM_LIMIT_BYTES))(buf)


ROW_TILE = 128


def _chip_partial(stack, received, layer, place):
    _, half_rows, cols = received.shape
    tr = _pick(half_rows, (ROW_TILE, ROW_TILE // 2))
    nh = half_rows // tr

    def body(place_ref, a_ref, b_ref, o_ref):
        o_ref[...] = (a_ref[...].astype(F32) + b_ref[...].astype(F32)).astype(BF16)

    blk = pl.BlockSpec((None, tr, cols), lambda s, i, pr: (s, i, 0))
    grid_spec = pltpu.PrefetchScalarGridSpec(
        num_scalar_prefetch=1, grid=(N_CHIPS, nh),
        in_specs=[pl.BlockSpec((None, None, tr, cols), lambda s, i, pr: (layer, s, pr[0] * nh + i, 0)), blk],
        out_specs=blk)
    return _call(body, name="chip_partial", grid_spec=grid_spec, out_shape=jax.ShapeDtypeStruct(received.shape, BF16),
                 compiler_params=_params(2))(place, stack, received)


def _final_sum(partial, from_chips, layer, place, finals):
    _, half_rows, cols = partial.shape
    tr = _pick(half_rows, (ROW_TILE, ROW_TILE // 2))
    nh = half_rows // tr

    def body(place_ref, a_ref, r_ref, *rest):
        o_ref = rest[-1]
        tot = a_ref[...].astype(F32)
        for j in range(3):
            tot = tot + r_ref[j].astype(F32)
        o_ref[...] = tot

    in_specs = [pl.BlockSpec((None, tr, cols), lambda i, pr: (pr[1], i, 0)),
                pl.BlockSpec((3, tr, cols), lambda i, pr: (0, i, 0))]
    args = [place, partial, from_chips]
    kw = {}
    if finals is not None:
        in_specs.append(pl.BlockSpec(memory_space=pl.ANY))
        args.append(finals)
        kw["input_output_aliases"] = {3: 0}
    grid_spec = pltpu.PrefetchScalarGridSpec(
        num_scalar_prefetch=1, grid=(nh,), in_specs=in_specs,
        out_specs=pl.BlockSpec((None, tr, cols), lambda i, pr: (layer, pr[0] * nh + i, 0)))
    return _call(body, name="final_sum", grid_spec=grid_spec,
                 out_shape=jax.ShapeDtypeStruct((2, 2 * half_rows, cols), F32), compiler_params=_params(1), **kw)(*args)


def _adamw(w, g, m, v, comm=None):
    n, rows, cols = w.shape
    tr = _pick(rows, (ROW_TILE, SUBLANES))
    c1 = 1.0 - ADAM_B1 ** ADAM_STEP
    c2 = 1.0 - ADAM_B2 ** ADAM_STEP

    def body(w_ref, g_ref, m_ref, v_ref, d_ref, nm_ref, nv_ref, go_ref):
        g_v = g_ref[...]
        go_ref[...] = g_v
        nm = ADAM_B1 * m_ref[...] + (1.0 - ADAM_B1) * g_v
        nv = ADAM_B2 * v_ref[...] + (1.0 - ADAM_B2) * (g_v * g_v)
        nm_ref[...] = nm
        nv_ref[...] = nv
        d_ref[...] = -ADAM_LR * ((nm / c1) / (jnp.sqrt(nv / c2) + ADAM_EPS) + ADAM_WD * w_ref[...])

    blk = pl.BlockSpec((None, tr, cols), lambda l, i: (l, i, 0))
    shape = jax.ShapeDtypeStruct(w.shape, F32)
    return _hosted_call(body, comm, "adamw", (n, rows // tr), [w, g, m, v], [blk] * 4, [shape] * 4, [blk] * 4)


def _to_heads(a, n_heads, transposed=False):
    t_len = a.shape[0]
    return a.reshape(t_len, n_heads, HEAD_DIM).transpose((1, 2, 0) if transposed else (1, 0, 2))


def _from_heads(a, transposed=False):
    a = a.transpose((2, 0, 1) if transposed else (1, 0, 2))
    return a.reshape(a.shape[0], a.shape[1] * HEAD_DIM)


class _Schedule:
    def __init__(self):
        self.sites = {}
        self.open = []

    def add(self, site, make, done=None):
        self.sites.setdefault(site, []).append((make, done))

    def begin(self, site):
        self.open = [(make(), done) for make, done in self.sites.pop(site, [])]
        return _merge_comms([cm for cm, _ in self.open])

    def end(self):
        for cm, done in self.open:
            if done is not None:
                done(cm)
        self.open = []


def _ffn_fwd(x, gain, wg, which, layer, sched):
    w_in_name, w_out_name = f"ffn{which}_w_in", f"ffn{which}_w_out"
    h = _rmsnorm_fwd(x, gain)
    comm = sched.begin(("ffn_in", layer, which))
    gu, act = _mm_ffn_in(h, wg[w_in_name], layer, comm)
    sched.end()
    comm = sched.begin(("ffn_out", layer, which))
    x_new = _mm_out_res("ffn_out", act, wg[w_out_name], layer, x, FFN_RESIDUAL_WEIGHT, comm)[0]
    sched.end()
    return x_new, (x, h, gu, act)


def _ffn_bwd(dx, dxb, saved, gain, wg, which, layer, stacks, sched, weights_first=False):
    w_in_name, w_out_name = f"ffn{which}_w_in", f"ffn{which}_w_out"
    x, h, gu, act = saved
    out = {}

    def dact():
        comm = sched.begin(("ffn_dact", layer, which))
        out["dgu"] = _mm_dact_swiglu(dxb, wg[w_out_name], layer, gu, FFN_RESIDUAL_WEIGHT, comm)
        sched.end()

    def dw_out():
        comm = sched.begin(("ffn_dw_out", layer, which))
        stacks[w_out_name] = _dw_rows("ffn_dw_out", act, dxb, layer, stacks[w_out_name], FFN_RESIDUAL_WEIGHT, comm)
        sched.end()

    def dh():
        comm = sched.begin(("ffn_dh", layer, which))
        out["dh"] = _mm_dh_ffn(out["dgu"], wg[w_in_name], layer, comm)
        sched.end()

    def dw_in():
        comm = sched.begin(("ffn_dw_in", layer, which))
        stacks[w_in_name] = _dw_ffn_in(h, out["dgu"], layer, stacks[w_in_name], comm)
        sched.end()

    for step in ((dact, dw_in, dw_out, dh) if weights_first else (dact, dw_out, dh, dw_in)):
        step()
    return _rmsnorm_bwd(out["dh"], x, gain, dx)


def _mix_fwd(x, gain, wg, layer, small, tables, sched):
    h = _rmsnorm_fwd(x, gain)
    comm = sched.begin(("mix_in", layer))
    p = _mm_proj(h, wg["w_in"], layer, comm)[0]
    sched.end()
    qkv = _rope_fwd(p, tables)
    q_rows, k_rows, v_rows = qkv[:, :Q_END], qkv[:, Q_END:K_END], qkv[:, K_END:V_END]
    qt, q = _to_heads(q_rows, N_Q_HEADS, True), _to_heads(q_rows, N_Q_HEADS)
    kt, k = _to_heads(k_rows, N_KV_HEADS, True), _to_heads(k_rows, N_KV_HEADS)
    vt, v = _to_heads(v_rows, N_KV_HEADS, True), _to_heads(v_rows, N_KV_HEADS)
    comm = sched.begin(("attn", layer))
    attn = _from_heads(_attn_fwd(qt, k, vt, small["snk"], comm), True)
    sched.end()
    y = _conv_fwd(p, small["conv_w"], small["conv_b"])
    conv = _conv_post_fwd(y, small["conv_ln_g"], small["conv_ln_b"])
    sgu = _sgu_fwd(p, small["sgu_ln_g"], small["sgu_ln_b"], small["sgu_w"], small["sgu_bt"])
    mix = jnp.concatenate([attn, conv, sgu], axis=1)
    x_new = _mm_out_res("mix_out", mix, wg["w_out"], layer, x, 1.0)[0]
    return x_new, (x, h, p, (qt, q, k, kt, v), y, mix)


def _mix_bwd(dx, dxb, saved, gain, wg, small, tables, layer, stacks, sched):
    x, h, p, (qt, q, k, kt, v), y, mix = saved
    comm = sched.begin(("mix_dout", layer))
    dmix = _mm_dmix(dxb, wg["w_out"], layer, comm)
    sched.end()
    stacks["w_out"] = _dw_rows("mix_dw_out", mix, dxb, layer, stacks["w_out"], 1.0)
    do_rows = dmix[:, :Q_END].astype(BF16)
    comm = sched.begin(("attn_bwd", layer))
    dqt, dkp, dkc, dvp, dvc, dsnk = _attn_bwd(qt, q, k, kt, v, small["snk"], _to_heads(do_rows, N_Q_HEADS, True),
                                              _to_heads(do_rows, N_Q_HEADS), comm)
    sched.end()
    dy, d_ln_g, d_ln_b, d_conv_b = _conv_post_bwd(dmix, y, small["conv_ln_g"], small["conv_ln_b"])
    dalin, dagate, d_conv_w = _conv_bwd(p, dy, small["conv_w"])
    du, dvin, d_sgu_w, d_sgu_bt, d_sgu_g, d_sgu_b = _sgu_bwd(p, dmix, small["sgu_ln_g"], small["sgu_ln_b"],
                                                           small["sgu_w"], small["sgu_bt"])
    dp = _assemble_dp(_from_heads(dqt, True), _from_heads(dkc), _from_heads(dkp), _from_heads(dvc), _from_heads(dvp),
                      tables, dalin, dagate, du, dvin)
    comm = sched.begin(("mix_dw_in", layer))
    stacks["w_in"] = _dw_mix_in(h, dp, layer, stacks["w_in"], comm)
    sched.end()
    comm = sched.begin(("mix_dh", layer))
    dh = _mm_dh_mix(dp, wg["w_in"], layer, comm)
    sched.end()
    dx_in, dxb_in, dgain = _rmsnorm_bwd(dh, x, gain, dx)
    grads = {
        "norm_mix": dgain[0], "conv_dw_w": d_conv_w[:CONV_WIDTH], "conv_dw_b": d_conv_b[0],
        "conv_ln_g": d_ln_g[0], "conv_ln_b": d_ln_b[0], "sgu_ln_g": d_sgu_g[0], "sgu_ln_b": d_sgu_b[0],
        "sgu_w": d_sgu_w, "sgu_b": d_sgu_bt[:, :SGU_HEADS].T, "attn_sinks": dsnk[:, :GQ, 0].reshape(N_Q_HEADS),
    }
    return dx_in, dxb_in, grads


BIG = ("ffn1_w_in", "ffn1_w_out", "w_in", "w_out", "ffn2_w_in", "ffn2_w_out")
SMALL = ("norm_ffn1", "norm_mix", "conv_dw_w", "conv_dw_b", "conv_ln_g", "conv_ln_b", "sgu_ln_g", "sgu_ln_b",
         "sgu_w", "sgu_b", "attn_sinks", "norm_ffn2", "final_norm")
WEIGHTS = ("norm_ffn1", "ffn1_w_in", "ffn1_w_out", "norm_mix", "w_in", "conv_dw_w", "conv_dw_b", "conv_ln_g",
           "conv_ln_b", "sgu_ln_g", "sgu_ln_b", "sgu_w", "sgu_b", "attn_sinks", "w_out", "norm_ffn2", "ffn2_w_in",
           "ffn2_w_out", "final_norm")
PACK_ROWS = SUBLANES * LANES

FIRST_GATHER = [("ffn1_w_in", 0, None)]
FORWARD_PLAN = {
    ("ffn_in", 0, 1): [("ffn1_w_out", 0, None), ("w_in", 0, None)],
    ("ffn_out", 0, 1): [("ffn2_w_in", 0, 0)],
    ("mix_in", 0): [("w_out", 0, None)],
    ("attn", 0): [("ffn2_w_in", 0, 1)],
    ("ffn_in", 0, 2): [("ffn2_w_out", 0, None), ("ffn1_w_in", 1, 0)],
    ("ffn_out", 0, 2): [("ffn1_w_in", 1, 1)],
    ("ffn_in", 1, 1): [("ffn1_w_out", 1, None), ("w_in", 1, None)],
    ("ffn_out", 1, 1): [("ffn2_w_in", 1, 0)],
    ("mix_in", 1): [("w_out", 1, None)],
    ("attn", 1): [("ffn2_w_in", 1, 1)],
    ("ffn_in", 1, 2): [("ffn2_w_out", 1, None)],
}
SUBLAYER_WEIGHTS = {"ffn1": ["ffn1_w_out", "ffn1_w_in"], "ffn2": ["ffn2_w_out", "ffn2_w_in"], "mix": ["w_out", "w_in"]}


def _pack(arrays):
    flat = jnp.concatenate([a.reshape(-1).astype(F32) for a in arrays])
    pad = (-flat.shape[0]) % PACK_ROWS
    return jnp.pad(flat, (0, pad)).reshape(-1, LANES)


def _unpack(buf, shapes):
    flat = buf.reshape(-1)
    out, off = [], 0
    for s in shapes:
        n = 1
        for d in s:
            n *= d
        out.append(flat[off:off + n].reshape(s))
        off += n
    return out


def kernel(x, positions, norm_ffn1, ffn1_w_in, ffn1_w_out, norm_mix, w_in, conv_dw_w, conv_dw_b, conv_ln_g, conv_ln_b, sgu_ln_g, sgu_ln_b, sgu_w, sgu_b, attn_sinks, w_out, norm_ffn2, ffn2_w_in, ffn2_w_out, final_norm, loss_target, m_norm_ffn1, m_ffn1_w_in, m_ffn1_w_out, m_norm_mix, m_w_in, m_conv_dw_w, m_conv_dw_b, m_conv_ln_g, m_conv_ln_b, m_sgu_ln_g, m_sgu_ln_b, m_sgu_w, m_sgu_b, m_attn_sinks, m_w_out, m_norm_ffn2, m_ffn2_w_in, m_ffn2_w_out, m_final_norm, v_norm_ffn1, v_ffn1_w_in, v_ffn1_w_out, v_norm_mix, v_w_in, v_conv_dw_w, v_conv_dw_b, v_conv_ln_g, v_conv_ln_b, v_sgu_ln_g, v_sgu_ln_b, v_sgu_w, v_sgu_b, v_attn_sinks, v_w_out, v_norm_ffn2, v_ffn2_w_in, v_ffn2_w_out, v_final_norm):
    w = dict(norm_ffn1=norm_ffn1, ffn1_w_in=ffn1_w_in, ffn1_w_out=ffn1_w_out, norm_mix=norm_mix, w_in=w_in,
             conv_dw_w=conv_dw_w, conv_dw_b=conv_dw_b, conv_ln_g=conv_ln_g, conv_ln_b=conv_ln_b, sgu_ln_g=sgu_ln_g,
             sgu_ln_b=sgu_ln_b, sgu_w=sgu_w, sgu_b=sgu_b, attn_sinks=attn_sinks, w_out=w_out, norm_ffn2=norm_ffn2,
             ffn2_w_in=ffn2_w_in, ffn2_w_out=ffn2_w_out, final_norm=final_norm)
    m = dict(norm_ffn1=m_norm_ffn1, ffn1_w_in=m_ffn1_w_in, ffn1_w_out=m_ffn1_w_out, norm_mix=m_norm_mix, w_in=m_w_in,
             conv_dw_w=m_conv_dw_w, conv_dw_b=m_conv_dw_b, conv_ln_g=m_conv_ln_g, conv_ln_b=m_conv_ln_b,
             sgu_ln_g=m_sgu_ln_g, sgu_ln_b=m_sgu_ln_b, sgu_w=m_sgu_w, sgu_b=m_sgu_b, attn_sinks=m_attn_sinks,
             w_out=m_w_out, norm_ffn2=m_norm_ffn2, ffn2_w_in=m_ffn2_w_in, ffn2_w_out=m_ffn2_w_out,
             final_norm=m_final_norm)
    v = dict(norm_ffn1=v_norm_ffn1, ffn1_w_in=v_ffn1_w_in, ffn1_w_out=v_ffn1_w_out, norm_mix=v_norm_mix, w_in=v_w_in,
             conv_dw_w=v_conv_dw_w, conv_dw_b=v_conv_dw_b, conv_ln_g=v_conv_ln_g, conv_ln_b=v_conv_ln_b,
             sgu_ln_g=v_sgu_ln_g, sgu_ln_b=v_sgu_ln_b, sgu_w=v_sgu_w, sgu_b=v_sgu_b, attn_sinks=v_attn_sinks,
             w_out=v_w_out, norm_ffn2=v_norm_ffn2, ffn2_w_in=v_ffn2_w_in, ffn2_w_out=v_ffn2_w_out,
             final_norm=v_final_norm)
    depth = norm_ffn1.shape[0]
    assert depth == 2 and x.shape[0] == 1
    xc = lax.axis_index("x")
    yc = lax.axis_index("y")
    cc = lax.axis_index("c")
    chip = 2 * xc + yc

    chip_arr = chip.reshape(1).astype(jnp.int32)
    place = jnp.stack([cc, chip]).astype(jnp.int32)
    wg = {n: _cast_place(w[n], chip_arr) for n in BIG}
    sched = _Schedule()

    def gather(pieces):
        names = sorted({n for n, _, _ in pieces})
        half = w["ffn1_w_in"].shape[1] // 2

        def make():
            rows = lambda n, part: (0, wg[n].shape[2]) if part is None else (part * half, half)
            return _gather_comm([wg[n] for n in names], [(names.index(n), l, *rows(n, part)) for n, l, part in pieces])

        return make, lambda cm: wg.update(zip(names, cm.aliased_out))

    make, done = gather(FIRST_GATHER)
    first = make()
    _standalone("gather_first", first)
    done(first)
    for site, pieces in FORWARD_PLAN.items():
        sched.add(site, *gather(pieces))
    conv_w_full = _all_gather_weights([], [conv_dw_w])[0].transpose(0, 2, 1, 3).reshape(depth, CONV_WIDTH, CONV_CH)
    conv_w_full = jnp.pad(conv_w_full, ((0, 0), (0, CONV_PAD - CONV_WIDTH), (0, 0)))

    tables = _rope_tables(positions)
    small = []
    for l in range(depth):
        small.append(dict(
            snk=jnp.broadcast_to(attn_sinks[l].reshape(N_KV_HEADS, 1, GQ, 1), (N_KV_HEADS, 1, GQ, BLK)).reshape(
                N_KV_HEADS, 1, GQ * BLK),
            conv_w=conv_w_full[l], conv_b=conv_dw_b[l][None], conv_ln_g=conv_ln_g[l][None],
            conv_ln_b=conv_ln_b[l][None], sgu_ln_g=sgu_ln_g[l][None], sgu_ln_b=sgu_ln_b[l][None], sgu_w=sgu_w[l],
            sgu_bt=sgu_b[l].T))

    xs = x[0]
    saved = []
    for l in range(depth):
        xs, s1 = _ffn_fwd(xs, norm_ffn1[l][None], wg, 1, l, sched)
        xs, s2 = _mix_fwd(xs, norm_mix[l][None], wg, l, small[l], tables, sched)
        xs, s3 = _ffn_fwd(xs, norm_ffn2[l][None], wg, 2, l, sched)
        saved.append((s1, s2, s3))
    dx, dxb, d_final, loss_part = _loss_head(xs, final_norm[None], loss_target[0])

    stacks = {n: None for n in BIG}
    partials, from_chips = {}, {}
    finals = {n: None for n in BIG}

    def to_owner(layer, names):
        def done(cm):
            for n, received in zip(names, cm.fresh_out):
                partials[n, layer] = _chip_partial(stacks[n], received, layer, place)

        return lambda: _to_owner_comm([stacks[n] for n in names], layer), done

    def between_chips(layer, names, then_sum=()):
        def done(cm):
            from_chips.update({(n, layer): r for n, r in zip(names, cm.fresh_out)})
            for n in then_sum:
                finals[n] = _final_sum(partials[n, layer], from_chips[n, layer], layer, place, finals[n])

        return lambda: _chip_comm([partials[n, layer] for n in names]), done

    def from_owner(layer, names):
        return (lambda: _from_owner_comm([finals[n] for n in names], layer),
                lambda cm: finals.update(zip(names, cm.aliased_out)))

    order = [(l, kind) for l in reversed(range(depth)) for kind in ("ffn2", "mix", "ffn1")]
    for (layer, kind), (nxt_layer, nxt_kind) in zip(order[:-1], order[1:]):
        names = SUBLAYER_WEIGHTS[kind]
        if (nxt_layer, nxt_kind) == order[-1]:
            w_out_name, w_in_name = SUBLAYER_WEIGHTS[nxt_kind]
            which = int(nxt_kind[-1])
            sched.add(("ffn_dact", nxt_layer, which), *to_owner(layer, names))
            sched.add(("ffn_dw_in", nxt_layer, which), *between_chips(layer, names, names))
            sched.add(("ffn_dw_out", nxt_layer, which), *from_owner(layer, names))
            sched.add(("ffn_dw_out", nxt_layer, which), *to_owner(nxt_layer, [w_in_name]))
            sched.add(("ffn_dh", nxt_layer, which), *between_chips(nxt_layer, [w_in_name], [w_in_name]))
            sched.add(("ffn_dh", nxt_layer, which), *to_owner(nxt_layer, [w_out_name]))
        elif nxt_kind == "mix":
            sched.add(("mix_dout", nxt_layer), *to_owner(layer, names))
            sched.add(("attn_bwd", nxt_layer), *between_chips(layer, names[1:], names[1:]))
            sched.add(("mix_dw_in", nxt_layer), *between_chips(layer, names[:1], names[:1]))
            sched.add(("mix_dh", nxt_layer), *from_owner(layer, names))
        else:
            which = int(nxt_kind[-1])
            sched.add(("ffn_dact", nxt_layer, which), *to_owner(layer, names))
            if kind == "mix":
                sched.add(("ffn_dw_out", nxt_layer, which), *between_chips(layer, names, names))
            else:
                sched.add(("ffn_dw_out", nxt_layer, which), *between_chips(layer, names[:1]))
                sched.add(("ffn_dh", nxt_layer, which), *between_chips(layer, names[1:], names))
            sched.add(("ffn_dw_in", nxt_layer, which), *from_owner(layer, names))

    small_grads = [None] * depth
    for l in reversed(range(depth)):
        s1, s2, s3 = saved[l]
        dx, dxb, dg2 = _ffn_bwd(dx, dxb, s3, norm_ffn2[l][None], wg, 2, l, stacks, sched)
        dx, dxb, gm = _mix_bwd(dx, dxb, s2, norm_mix[l][None], wg, small[l], tables, l, stacks, sched)
        dx, dxb, dg1 = _ffn_bwd(dx, dxb, s1, norm_ffn1[l][None], wg, 1, l, stacks, sched, weights_first=l == 0)
        gm["norm_ffn1"] = dg1[0]
        gm["norm_ffn2"] = dg2[0]
        small_grads[l] = gm
    grad_x = dx[None]
    assert not sched.sites, sched.sites

    per_layer = [n for n in SMALL if n != "final_norm"]
    small_local = [jnp.stack([small_grads[l][n] for l in range(depth)]) for n in per_layer]
    small_local += [d_final[0], loss_part[0, :1]]
    small_shapes = [a.shape for a in small_local]
    summed = _unpack(_all_reduce_small(_pack(small_local)), small_shapes)
    loss = summed[-1][0]
    sg = dict(zip(per_layer + ["final_norm"], summed[:-1]))
    sg["conv_dw_w"] = lax.dynamic_slice_in_dim(sg["conv_dw_w"], chip * LANES, LANES, axis=2)

    delta, new_m, new_v = {}, {}, {}
    shapes = [w[n].shape for n in SMALL]
    packed = [_pack([d[n] for n in SMALL])[None] for d in (w, sg, m, v)]
    last_layer, last_kind = order[-1]
    names = SUBLAYER_WEIGHTS[last_kind]
    make, done = between_chips(last_layer, names[:1], names[:1])
    cm = make()
    outs = _adamw(*packed, comm=cm)
    done(cm)
    for d, buf in zip((delta, new_m, new_v), outs[:3]):
        d.update(zip(SMALL, _unpack(buf[0], shapes)))
    make, done = from_owner(last_layer, names)
    cm = make()
    _standalone("from_owner", cm)
    done(cm)
    big_grads = dict(finals)
    for n in BIG:
        delta[n], new_m[n], new_v[n], big_grads[n] = _adamw(w[n], big_grads[n], m[n], v[n])
    grads = {**big_grads, **sg}
    return (loss, grad_x, *[grads[n] for n in WEIGHTS], *[delta[n] for n in WEIGHTS],
            *[new_m[n] for n in WEIGHTS], *[new_v[n] for n in WEIGHTS])
```

```python
import functools

import jax
import jax.numpy as jnp
from jax import lax
from jax.experimental import pallas as pl
from jax.experimental.pallas import tpu as pltpu

F32 = jnp.float32
BF16 = jnp.bfloat16
MESH_ID = pl.DeviceIdType.MESH

V7X_VMEM_LIMIT_BYTES = 56 * 2**20
LANES = 128
SUBLANES = 8

HEAD_DIM = 64
N_Q_HEADS = 16
N_KV_HEADS = 4
GQ = N_Q_HEADS // N_KV_HEADS
BLK = 128
ROT_HALF = 8
ROPE_THETA = 500000.0
CONV_WIDTH = 31
CONV_PAD = 32
CONV_CH = 512
SGU_CH = 512
SGU_HEADS = 8
Q_END = N_Q_HEADS * HEAD_DIM
K_END = Q_END + N_KV_HEADS * HEAD_DIM
V_END = K_END + N_KV_HEADS * HEAD_DIM
CONV_END = V_END + 2 * CONV_CH
IN_COLS = CONV_END + 2 * SGU_CH
NORM_EPS = 1e-5
FFN_RESIDUAL_WEIGHT = 0.5
N_CHIPS = 4
N_DEV = 8

ADAM_LR = 0.001
ADAM_B1 = 0.9
ADAM_B2 = 0.999
ADAM_EPS = 1e-08
ADAM_WD = 0.01
ADAM_STEP = 10

NN = (((1,), (0,)), ((), ()))
NT = (((1,), (1,)), ((), ()))
TN = (((0,), (0,)), ((), ()))


def _pick(n, cands):
    for c in cands:
        if n % c == 0:
            return c
    raise ValueError(f"no tile of {cands} divides {n}")


def _params(n_axes):
    return pltpu.CompilerParams(dimension_semantics=("arbitrary",) * n_axes, vmem_limit_bytes=V7X_VMEM_LIMIT_BYTES)


def _call(body, **kw):
    return pl.pallas_call(body, **kw)


def _sigmoid(x):
    return 1.0 / (1.0 + jnp.exp(-x))


class _Comm:
    def __init__(self, reads, aliased, fresh, sems, start, finish):
        self.reads, self.aliased, self.fresh, self.sems = list(reads), list(aliased), list(fresh), list(sems)
        self.start, self.finish = start, finish
        self.aliased_out, self.fresh_out = None, None


def _merge_comms(comms):
    comms = [cm for cm in comms if cm is not None]
    if not comms:
        return None
    if len(comms) == 1:
        return comms[0]

    def split(refs, counts):
        out, off = [], 0
        for n in counts:
            out.append(refs[off:off + n])
            off += n
        return out

    def run(which):
        def f(rd, al, fr, sm):
            parts = zip(split(rd, [len(cm.reads) for cm in comms]), split(al, [len(cm.aliased) for cm in comms]),
                        split(fr, [len(cm.fresh) for cm in comms]), split(sm, [len(cm.sems) for cm in comms]))
            for cm, (r, a, f_, s) in zip(comms, parts):
                getattr(cm, which)(r, a, f_, s)
        return f

    merged = _Comm(sum((cm.reads for cm in comms), []), sum((cm.aliased for cm in comms), []),
                   sum((cm.fresh for cm in comms), []), sum((cm.sems for cm in comms), []), run("start"), run("finish"))
    merged.parts = comms
    return merged


def _hosted_call(body, comm, name, grid, inputs, in_specs, out_shape, out_specs, scratch_shapes=(), aliases=None):
    n_in, n_out, n_scr = len(inputs), len(out_shape), len(scratch_shapes)
    aliases = dict(aliases or {})
    if comm is None:
        return _call(body, name=name, grid=grid, in_specs=list(in_specs), out_specs=list(out_specs),
                     out_shape=list(out_shape), scratch_shapes=list(scratch_shapes), input_output_aliases=aliases,
                     compiler_params=_params(len(grid)))(*inputs)
    nr, na, nf = len(comm.reads), len(comm.aliased), len(comm.fresh)

    def full(*refs):
        ins = refs[:n_in]
        rd = refs[n_in:n_in + nr]
        pos = n_in + nr + na
        outs = refs[pos:pos + n_out]
        al = refs[pos + n_out:pos + n_out + na]
        fr = refs[pos + n_out + na:pos + n_out + na + nf]
        pos = pos + n_out + na + nf
        scr = refs[pos:pos + n_scr]
        sems = refs[pos + n_scr:]
        first, last = None, None
        for axis, size in enumerate(grid):
            f, l = pl.program_id(axis) == 0, pl.program_id(axis) == size - 1
            first = f if first is None else jnp.logical_and(first, f)
            last = l if last is None else jnp.logical_and(last, l)

        @pl.when(first)
        def _():
            comm.start(rd, al, fr, sems)

        body(*ins, *outs, *scr)

        @pl.when(last)
        def _():
            comm.finish(rd, al, fr, sems)

    hbm = pl.BlockSpec(memory_space=pltpu.HBM)
    for i in range(na):
        aliases[n_in + nr + i] = n_out + i
    struct = [jax.ShapeDtypeStruct(a.shape, a.dtype) for a in comm.aliased]
    res = _call(full, name=name, grid=grid, in_specs=list(in_specs) + [hbm] * (nr + na),
                out_specs=list(out_specs) + [hbm] * (na + nf), out_shape=list(out_shape) + struct + comm.fresh,
                scratch_shapes=list(scratch_shapes) + comm.sems, input_output_aliases=aliases,
                compiler_params=pltpu.CompilerParams(dimension_semantics=("arbitrary",) * len(grid),
                                                     vmem_limit_bytes=V7X_VMEM_LIMIT_BYTES, has_side_effects=True),
                )(*inputs, *comm.reads, *comm.aliased)
    _deliver(comm, res[n_out:n_out + na], res[n_out + na:])
    return res[:n_out]


def _deliver(comm, aliased_out, fresh_out):
    comm.aliased_out, comm.fresh_out = list(aliased_out), list(fresh_out)
    off_a = off_f = 0
    for part in getattr(comm, "parts", []):
        _deliver(part, aliased_out[off_a:off_a + len(part.aliased)], fresh_out[off_f:off_f + len(part.fresh)])
        off_a += len(part.aliased)
        off_f += len(part.fresh)


def _standalone(name, comm):
    def body(*refs):
        nr, na, nf = len(comm.reads), len(comm.aliased), len(comm.fresh)
        rd, al, fr, sems = refs[:nr], refs[nr + na:nr + 2 * na], refs[nr + 2 * na:nr + 2 * na + nf], refs[nr + 2 * na + nf:]
        comm.start(rd, al, fr, sems)
        comm.finish(rd, al, fr, sems)

    nr, na, nf = len(comm.reads), len(comm.aliased), len(comm.fresh)
    struct = [jax.ShapeDtypeStruct(a.shape, a.dtype) for a in comm.aliased]
    res = _call(body, name=name, in_specs=_hbm_specs(nr + na), out_specs=_hbm_specs(na + nf),
                out_shape=struct + comm.fresh, scratch_shapes=comm.sems,
                input_output_aliases={nr + i: i for i in range(na)},
                compiler_params=_comm_params())(*comm.reads, *comm.aliased)
    _deliver(comm, res[:na], res[na:])


def _matmul(name, grid, a_ops, b_ops, terms, dims, out_shape, out_specs, epilogue, extra_ops=(), nk=1,
            acc_shapes=(), alias=None, comm=None, chunk=None):
    na, nb, ne, no = len(a_ops), len(b_ops), len(extra_ops), len(out_shape)

    def body(*refs):
        a = refs[:na]
        b = refs[na:na + nb]
        e = refs[na + nb:na + nb + ne]
        first_out = na + nb + ne + (1 if alias is not None else 0)
        o = refs[first_out:first_out + no]
        accs = refs[first_out + no:]

        def partial(t, cols=None):
            tot = None
            for ai, bi in t:
                rhs = b[bi][...] if cols is None else (b[bi][cols, :] if dims == NT else b[bi][:, cols])
                d = lax.dot_general(a[ai][...], rhs, dims, preferred_element_type=F32)
                tot = d if tot is None else tot + d
            return tot

        if nk == 1 and chunk is not None:
            width = b[0].shape[0] if dims == NT else b[0].shape[1]
            for c0 in range(0, width, chunk):
                cols = slice(c0, min(c0 + chunk, width))
                epilogue([partial(t, cols) for t in terms], e, o, cols)
        elif nk == 1:
            epilogue([partial(t) for t in terms], e, o)
        else:
            k = pl.program_id(len(grid) - 1)

            @pl.when(k == 0)
            def _():
                for acc in accs:
                    acc[...] = jnp.zeros(acc.shape, F32)

            for acc, t in zip(accs, terms):
                acc[...] += partial(t)

            @pl.when(k == nk - 1)
            def _():
                epilogue([acc[...] for acc in accs], e, o)

    ops = list(a_ops) + list(b_ops) + list(extra_ops)
    arrays = [x for x, _ in ops]
    in_specs = [s for _, s in ops]
    aliases = {}
    if alias is not None:
        arrays.append(alias[0])
        in_specs.append(pl.BlockSpec(memory_space=pl.ANY))
        aliases[len(arrays) - 1] = alias[1]
    scratch = [pltpu.VMEM(s, F32) for s in acc_shapes] if nk > 1 else []
    return _hosted_call(body, comm, name, grid, arrays, in_specs, out_shape, out_specs, scratch, aliases)


def _mm_ffn_in(h, w_g, layer, comm=None):
    t_len, d = h.shape
    fs = w_g.shape[3]
    f = 2 * fs
    tm = _pick(t_len, (2048, 1024, 512))
    tn = _pick(fs, (256, 128))
    nj = fs // tn

    def epilogue(accs, e, o):
        g, u = accs
        o[0][0] = g.astype(BF16)
        o[0][1] = u.astype(BF16)
        o[1][...] = (g * _sigmoid(g) * u).astype(BF16)

    return _matmul(
        "ffn_in", (t_len // tm, 2, nj),
        [(h, pl.BlockSpec((tm, d), lambda i, s, j: (i, 0)))],
        [(w_g, pl.BlockSpec((None, None, d, tn), lambda i, s, j: (layer, s, 0, j))),
         (w_g, pl.BlockSpec((None, None, d, tn), lambda i, s, j: (layer, s + 2, 0, j)))],
        [[(0, 0)], [(0, 1)]], NN,
        [jax.ShapeDtypeStruct((2, t_len, f), BF16), jax.ShapeDtypeStruct((t_len, f), BF16)],
        [pl.BlockSpec((2, tm, tn), lambda i, s, j: (0, i, s * nj + j)),
         pl.BlockSpec((tm, tn), lambda i, s, j: (i, s * nj + j))],
        epilogue, comm=comm)


def _mm_out_res(name, a, w_g, layer, x, scale, comm=None):
    t_len = a.shape[0]
    ks, n = w_g.shape[2], w_g.shape[3]
    tm = _pick(t_len, (1024, 512))
    tn = _pick(n, (1024,))
    tk = _pick(ks, (1408, 512, 256))
    nks = ks // tk

    def epilogue(accs, e, o):
        o[0][...] = e[0][...] + scale * accs[0]

    return _matmul(
        name, (t_len // tm, n // tn, N_CHIPS * nks),
        [(a, pl.BlockSpec((tm, tk), lambda i, j, k: (i, k)))],
        [(w_g, pl.BlockSpec((None, None, tk, tn), lambda i, j, k: (layer, k // nks, k % nks, j)))],
        [[(0, 0)]], NN,
        [jax.ShapeDtypeStruct((t_len, n), F32)],
        [pl.BlockSpec((tm, tn), lambda i, j, k: (i, j))],
        epilogue, extra_ops=[(x, pl.BlockSpec((tm, tn), lambda i, j, k: (i, j)))],
        nk=N_CHIPS * nks, acc_shapes=[(tm, tn)], comm=comm)


def _mm_proj(h, w_g, layer, comm=None):
    t_len, d = h.shape
    cs = w_g.shape[3]
    tm = _pick(t_len, (1024, 512))

    def epilogue(accs, e, o):
        o[0][...] = accs[0]

    return _matmul(
        "mix_in", (t_len // tm, N_CHIPS),
        [(h, pl.BlockSpec((tm, d), lambda i, s: (i, 0)))],
        [(w_g, pl.BlockSpec((None, None, d, cs), lambda i, s: (layer, s, 0, 0)))],
        [[(0, 0)]], NN,
        [jax.ShapeDtypeStruct((t_len, N_CHIPS * cs), F32)],
        [pl.BlockSpec((tm, cs), lambda i, s: (i, s))],
        epilogue, comm=comm)


def _mm_dact_swiglu(dxb, w_g, layer, gu, scale, comm=None):
    t_len, d = dxb.shape
    rs = w_g.shape[2]
    tm = _pick(t_len, (512,))
    tn = _pick(rs, (1408, 256, 128))
    nj = rs // tn

    def epilogue(accs, e, o, cols):
        dact = scale * accs[0]
        g = e[0][0, :, cols].astype(F32)
        u = e[0][1, :, cols].astype(F32)
        sig = _sigmoid(g)
        o[0][0, :, cols] = (dact * u * (sig * (1.0 + g * (1.0 - sig)))).astype(BF16)
        o[0][1, :, cols] = (dact * (g * sig)).astype(BF16)

    gu_spec = pl.BlockSpec((2, tm, tn), lambda i, s, j: (0, i, s * nj + j))
    return _matmul(
        "ffn_dact", (t_len // tm, N_CHIPS, nj),
        [(dxb, pl.BlockSpec((tm, d), lambda i, s, j: (i, 0)))],
        [(w_g, pl.BlockSpec((None, None, tn, d), lambda i, s, j: (layer, s, j, 0)))],
        [[(0, 0)]], NT,
        [jax.ShapeDtypeStruct(gu.shape, BF16)], [gu_spec],
        epilogue, extra_ops=[(gu, gu_spec)], comm=comm, chunk=2 * LANES)[0]


def _mm_dh_ffn(dgu, w_g, layer, comm=None):
    t_len = dgu.shape[1]
    d, fs = w_g.shape[2], w_g.shape[3]
    tm = _pick(t_len, (1024, 512))
    tk = _pick(fs, (256, 128))
    nks = fs // tk
    nk = 2 * nks

    def epilogue(accs, e, o):
        o[0][...] = accs[0]

    return _matmul(
        "ffn_dh", (t_len // tm, nk),
        [(dgu, pl.BlockSpec((None, tm, tk), lambda i, k: (0, i, k))),
         (dgu, pl.BlockSpec((None, tm, tk), lambda i, k: (1, i, k)))],
        [(w_g, pl.BlockSpec((None, None, d, tk), lambda i, k: (layer, k // nks, 0, k % nks))),
         (w_g, pl.BlockSpec((None, None, d, tk), lambda i, k: (layer, k // nks + 2, 0, k % nks)))],
        [[(0, 0), (1, 1)]], NT,
        [jax.ShapeDtypeStruct((t_len, d), F32)],
        [pl.BlockSpec((tm, d), lambda i, k: (i, 0))],
        epilogue, nk=nk, acc_shapes=[(tm, d)], comm=comm)[0]


def _mm_dw(name, a, a_spec_of, b, b_spec_of, layer, stack, rows, cols, tn, scale, comm=None):
    t_len = a.shape[0]
    tt = _pick(t_len, (1024, 512))
    nj = cols // tn

    def epilogue(accs, e, o):
        o[0][...] = (scale * accs[0]).astype(BF16)

    shape = jax.ShapeDtypeStruct((2, N_CHIPS, rows, cols), BF16)
    return _matmul(
        name, (N_CHIPS, nj, t_len // tt),
        [(a, a_spec_of(tt))], [(b, b_spec_of(tt, tn, nj))],
        [[(0, 0)]], TN, [shape],
        [pl.BlockSpec((None, None, rows, tn), lambda s, j, t: (layer, s, 0, j))],
        epilogue, nk=t_len // tt, acc_shapes=[(rows, tn)],
        alias=None if stack is None else (stack, 0), comm=comm)[0]


def _dw_ffn_in(h, dgu, layer, stack, comm=None):
    d = h.shape[1]
    fs = dgu.shape[2] // 2
    tn = _pick(fs, (1408, 256))
    return _mm_dw(
        "ffn_dw_in", h, lambda tt: pl.BlockSpec((tt, d), lambda s, j, t: (t, 0)),
        dgu, lambda tt, tn_, nj: pl.BlockSpec((None, tt, tn_), lambda s, j, t: (s // 2, t, (s % 2) * nj + j)),
        layer, stack, d, fs, tn, 1.0, comm)


def _dw_rows(name, a, dxb, layer, stack, scale, comm=None):
    rs = a.shape[1] // N_CHIPS
    d = dxb.shape[1]
    tn = _pick(d, (1024,))
    return _mm_dw(
        name, a, lambda tt: pl.BlockSpec((tt, rs), lambda s, j, t: (t, s)),
        dxb, lambda tt, tn_, nj: pl.BlockSpec((tt, tn_), lambda s, j, t: (t, j)),
        layer, stack, rs, d, tn, scale, comm)


def _dw_mix_in(h, dp, layer, stack, comm=None):
    d = h.shape[1]
    cs = dp.shape[1] // N_CHIPS
    return _mm_dw(
        "mix_dw_in", h, lambda tt: pl.BlockSpec((tt, d), lambda s, j, t: (t, 0)),
        dp, lambda tt, tn_, nj: pl.BlockSpec((tt, tn_), lambda s, j, t: (t, s)),
        layer, stack, d, cs, cs, 1.0, comm)


def _mm_dmix(dxb, w_g, layer, comm=None):
    t_len, d = dxb.shape
    rs = w_g.shape[2]
    tm = _pick(t_len, (1024, 512))

    def epilogue(accs, e, o):
        o[0][...] = accs[0]

    return _matmul(
        "mix_dout", (t_len // tm, N_CHIPS),
        [(dxb, pl.BlockSpec((tm, d), lambda i, s: (i, 0)))],
        [(w_g, pl.BlockSpec((None, None, rs, d), lambda i, s: (layer, s, 0, 0)))],
        [[(0, 0)]], NT,
        [jax.ShapeDtypeStruct((t_len, N_CHIPS * rs), F32)],
        [pl.BlockSpec((tm, rs), lambda i, s: (i, s))],
        epilogue, comm=comm)[0]


def _mm_dh_mix(dp, w_g, layer, comm=None):
    t_len = dp.shape[0]
    d, cs = w_g.shape[2], w_g.shape[3]
    tm = _pick(t_len, (1024, 512))

    def epilogue(accs, e, o):
        o[0][...] = accs[0]

    return _matmul(
        "mix_dh", (t_len // tm, N_CHIPS),
        [(dp, pl.BlockSpec((tm, cs), lambda i, k: (i, k)))],
        [(w_g, pl.BlockSpec((None, None, d, cs), lambda i, k: (layer, k, 0, 0)))],
        [[(0, 0)]], NT,
        [jax.ShapeDtypeStruct((t_len, d), F32)],
        [pl.BlockSpec((tm, d), lambda i, k: (i, 0))],
        epilogue, nk=N_CHIPS, acc_shapes=[(tm, d)], comm=comm)[0]


def _rms_stats(x):
    r = lax.rsqrt(jnp.mean(x * x, axis=-1, keepdims=True) + NORM_EPS)
    return r, x * r


def _accumulate(ref, part, first):
    @pl.when(first)
    def _():
        ref[...] = part

    @pl.when(jnp.logical_not(first))
    def _():
        ref[...] += part


def _rmsnorm_fwd(x, g):
    t_len, d = x.shape
    tm = _pick(t_len, (512,))

    def body(x_ref, g_ref, h_ref):
        _, xhat = _rms_stats(x_ref[...])
        h_ref[...] = (xhat * g_ref[...]).astype(BF16)

    row = pl.BlockSpec((tm, d), lambda i: (i, 0))
    vec = pl.BlockSpec((1, d), lambda i: (0, 0))
    return _call(body, name="rmsnorm_fwd", grid=(t_len // tm,), in_specs=[row, vec], out_specs=row,
                 out_shape=jax.ShapeDtypeStruct((t_len, d), BF16), compiler_params=_params(1))(x, g)


def _rmsnorm_bwd(dh, x, g, dres):
    t_len, d = x.shape
    tm = _pick(t_len, (256,))

    def body(dh_ref, x_ref, g_ref, dres_ref, dx_ref, dxb_ref, dg_ref):
        r, xhat = _rms_stats(x_ref[...])
        dh_v = dh_ref[...]
        gd = dh_v * g_ref[...]
        dx = dres_ref[...] + r * (gd - xhat * jnp.mean(gd * xhat, axis=-1, keepdims=True))
        dx_ref[...] = dx
        dxb_ref[...] = dx.astype(BF16)
        _accumulate(dg_ref, jnp.sum(dh_v * xhat, axis=0, keepdims=True), pl.program_id(0) == 0)

    row = pl.BlockSpec((tm, d), lambda i: (i, 0))
    vec = pl.BlockSpec((1, d), lambda i: (0, 0))
    return _call(body, name="rmsnorm_bwd", grid=(t_len // tm,), in_specs=[row, row, vec, row],
                 out_specs=[row, row, vec],
                 out_shape=[jax.ShapeDtypeStruct((t_len, d), F32), jax.ShapeDtypeStruct((t_len, d), BF16),
                            jax.ShapeDtypeStruct((1, d), F32)],
                 compiler_params=_params(1))(dh, x, g, dres)


def _loss_head(x, g, target):
    t_len, d = x.shape
    tm = _pick(t_len, (256,))

    def body(x_ref, g_ref, t_ref, dx_ref, dxb_ref, dg_ref, loss_ref):
        first = pl.program_id(0) == 0
        r, xhat = _rms_stats(x_ref[...])
        g_v = g_ref[...]
        err = xhat * g_v - t_ref[...]
        per_token = jnp.mean(err * err, axis=-1, keepdims=True)
        part = 0.5 * jnp.sum(per_token, axis=0, keepdims=True)
        _accumulate(loss_ref, jnp.broadcast_to(part, (1, LANES)), first)
        dy = err * (1.0 / d)
        _accumulate(dg_ref, jnp.sum(dy * xhat, axis=0, keepdims=True), first)
        gd = dy * g_v
        dx = r * (gd - xhat * jnp.mean(gd * xhat, axis=-1, keepdims=True))
        dx_ref[...] = dx
        dxb_ref[...] = dx.astype(BF16)

    row = pl.BlockSpec((tm, d), lambda i: (i, 0))
    vec = pl.BlockSpec((1, d), lambda i: (0, 0))
    return _call(body, name="loss_head", grid=(t_len // tm,), in_specs=[row, vec, row],
                 out_specs=[row, row, vec, pl.BlockSpec((1, LANES), lambda i: (0, 0))],
                 out_shape=[jax.ShapeDtypeStruct((t_len, d), F32), jax.ShapeDtypeStruct((t_len, d), BF16),
                            jax.ShapeDtypeStruct((1, d), F32), jax.ShapeDtypeStruct((1, LANES), F32)],
                 compiler_params=_params(1))(x, g, target)


def _ln_stats(x):
    mu = jnp.mean(x, axis=-1, keepdims=True)
    xc = x - mu
    r = lax.rsqrt(jnp.mean(xc * xc, axis=-1, keepdims=True) + NORM_EPS)
    return r, xc * r


def _ln_bwd(dy, r, xhat, g):
    dxh = dy * g
    return r * (dxh - jnp.mean(dxh, axis=-1, keepdims=True) - xhat * jnp.mean(dxh * xhat, axis=-1, keepdims=True))


def _rope_tables(positions):
    t_len = positions.shape[-1]
    inv_freq = 1.0 / (ROPE_THETA ** (jnp.arange(0, 2 * ROT_HALF, 2, dtype=F32) / (2 * ROT_HALF)))
    ang = positions.astype(F32).reshape(t_len, 1) * inv_freq
    cos = jnp.tile(jnp.cos(ang), (1, LANES // ROT_HALF))
    sin = jnp.tile(jnp.sin(ang), (1, LANES // ROT_HALF))
    lane = jnp.arange(LANES) % HEAD_DIM
    c = jnp.where(lane < 2 * ROT_HALF, cos, 1.0)
    s1 = jnp.where(lane < ROT_HALF, -sin, 0.0)
    s2 = jnp.where((lane >= ROT_HALF) & (lane < 2 * ROT_HALF), sin, 0.0)
    return c.astype(F32), s1.astype(F32), s2.astype(F32)


def _rope_fwd(p, tables):
    t_len = p.shape[0]
    tm = _pick(t_len, (256,))
    n_rot = K_END // LANES

    def body(p_ref, c_ref, s1_ref, s2_ref, o_ref):
        c, s1, s2 = c_ref[...], s1_ref[...], s2_ref[...]
        for j in range(V_END // LANES):
            sl = slice(j * LANES, (j + 1) * LANES)
            t = p_ref[:, sl]
            if j < n_rot:
                t = t * c + pltpu.roll(t, LANES - ROT_HALF, 1) * s1 + pltpu.roll(t, ROT_HALF, 1) * s2
            if j < Q_END // LANES:
                t = t * ATTN_SCALE
            o_ref[:, sl] = t.astype(BF16)

    tab = pl.BlockSpec((tm, LANES), lambda i: (i, 0))
    blk = pl.BlockSpec((tm, V_END), lambda i: (i, 0))
    return _call(body, name="rope_fwd", grid=(t_len // tm,), in_specs=[blk, tab, tab, tab], out_specs=blk,
                 out_shape=jax.ShapeDtypeStruct((t_len, V_END), BF16), compiler_params=_params(1))(p, *tables)


def _assemble_dp(dq, dkc, dkp, dvc, dvp, tables, dalin, dagate, du, dvin):
    t_len = dq.shape[0]
    steps = t_len // ATTN_STEP

    def body(dq_ref, dkc_ref, dkp_ref, dvc_ref, dvp_ref, c_ref, s1_ref, s2_ref, dalin_ref, dagate_ref,
             du_ref, dvin_ref, o_ref):
        keep = (pl.program_id(0) < steps - 1).astype(F32)
        for part in range(ATTN_STEP // BLK):
            rows = slice(part * BLK, (part + 1) * BLK)
            c, s1, s2 = c_ref[rows, :], s1_ref[rows, :], s2_ref[rows, :]

            def unrotate(dr):
                return dr * c + pltpu.roll(dr * s1, ROT_HALF, 1) + pltpu.roll(dr * s2, LANES - ROT_HALF, 1)

            for j in range(Q_END // LANES):
                sl = slice(j * LANES, (j + 1) * LANES)
                o_ref[rows, sl] = unrotate(dq_ref[rows, sl]).astype(BF16)
            for j in range((K_END - Q_END) // LANES):
                sl = slice(j * LANES, (j + 1) * LANES)
                dk, dv = dkc_ref[rows, sl], dvc_ref[rows, sl]
                if part == ATTN_STEP // BLK - 1:
                    dk = dk + keep * dkp_ref[:, sl]
                    dv = dv + keep * dvp_ref[:, sl]
                o_ref[rows, Q_END + j * LANES:Q_END + (j + 1) * LANES] = unrotate(dk).astype(BF16)
                o_ref[rows, K_END + j * LANES:K_END + (j + 1) * LANES] = dv.astype(BF16)
        o_ref[:, V_END:V_END + CONV_CH] = dalin_ref[...]
        o_ref[:, V_END + CONV_CH:CONV_END] = dagate_ref[...]
        o_ref[:, CONV_END:CONV_END + SGU_CH] = du_ref[...]
        o_ref[:, CONV_END + SGU_CH:IN_COLS] = dvin_ref[...]

    def cur(w):
        return pl.BlockSpec((ATTN_STEP, w), lambda i: (i, 0))

    def nxt(w):
        return pl.BlockSpec((BLK, w), lambda i: (jnp.minimum(i + 1, steps - 1), 0))

    kvw = K_END - Q_END
    return _call(body, name="assemble_dp", grid=(steps,),
                 in_specs=[cur(Q_END), cur(kvw), nxt(kvw), cur(kvw), nxt(kvw), cur(LANES), cur(LANES), cur(LANES),
                           cur(CONV_CH), cur(CONV_CH), cur(SGU_CH), cur(SGU_CH)],
                 out_specs=cur(IN_COLS), out_shape=jax.ShapeDtypeStruct((t_len, IN_COLS), BF16),
                 compiler_params=_params(1))(dq, dkc, dkp, dvc, dvp, *tables, dalin, dagate, du, dvin)


ATTN_STEP = 2 * BLK
ATTN_SCALE = HEAD_DIM ** -0.5
MASKED = -1e30


def _attn_bias():
    qi = jnp.arange(GQ * BLK)[None, :] % BLK
    kj = jnp.arange(2 * BLK)[:, None]
    dist = qi + BLK - kj
    band = (dist >= 0) & (dist < BLK)
    return jnp.stack([jnp.where(band & (kj >= BLK), 0.0, MASKED), jnp.where(band, 0.0, MASKED)]).astype(F32)


def _side_by_side(ref, cols):
    return jnp.concatenate([ref[h, :, cols] for h in range(GQ)], axis=1)


def _attn_chains(qt_ref, kp_ref, kc_ref):
    kc = kc_ref[...]
    pos_a, pos_b = slice(0, BLK), slice(BLK, ATTN_STEP)
    return [(pos_a, _side_by_side(qt_ref, pos_a), jnp.concatenate([kp_ref[...], kc[:BLK]], axis=0)),
            (pos_b, _side_by_side(qt_ref, pos_b), kc)]


def _attn_weights(qt, kk, bias, snk):
    s = lax.dot_general(kk, qt, NN, preferred_element_type=F32) + bias
    m = jnp.maximum(jnp.max(s, axis=0, keepdims=True), snk)
    e = jnp.exp(s - m)
    es = jnp.exp(snk - m)
    return e, es, 1.0 / (jnp.sum(e, axis=0, keepdims=True) + es)


def _attn_specs():
    before = lambda n: jnp.maximum(2 * n - 1, 0)
    return dict(
        qt=pl.BlockSpec((GQ, HEAD_DIM, ATTN_STEP), lambda g, n: (g, 0, n)),
        q=pl.BlockSpec((GQ, ATTN_STEP, HEAD_DIM), lambda g, n: (g, n, 0)),
        cur=pl.BlockSpec((None, ATTN_STEP, HEAD_DIM), lambda g, n: (g, n, 0)),
        prev=pl.BlockSpec((None, BLK, HEAD_DIM), lambda g, n: (g, before(n), 0)),
        cur_t=pl.BlockSpec((None, HEAD_DIM, ATTN_STEP), lambda g, n: (g, 0, n)),
        prev_t=pl.BlockSpec((None, HEAD_DIM, BLK), lambda g, n: (g, 0, before(n))),
        snk=pl.BlockSpec((None, 1, GQ * BLK), lambda g, n: (g, 0, 0)),
        bias_a=pl.BlockSpec((None, 2 * BLK, GQ * BLK), lambda g, n: (jnp.minimum(n, 1), 0, 0)),
        bias_b=pl.BlockSpec((None, 2 * BLK, GQ * BLK), lambda g, n: (1, 0, 0)))


def _attn_fwd(qt, k, vt, snk, comm=None):
    steps = qt.shape[2] // ATTN_STEP

    def body(qt_ref, kp_ref, kc_ref, vtp_ref, vtc_ref, snk_ref, ba_ref, bb_ref, o_ref):
        snk = snk_ref[...]
        vtc = vtc_ref[...]
        values = (jnp.concatenate([vtp_ref[...], vtc[:, :BLK]], axis=1), vtc)
        for (pos, qt_v, kk), vt_v, b_ref in zip(_attn_chains(qt_ref, kp_ref, kc_ref), values, (ba_ref, bb_ref)):
            e, _, inv = _attn_weights(qt_v, kk, b_ref[...], snk)
            o = lax.dot_general(vt_v, e.astype(BF16), NN, preferred_element_type=F32) * inv
            for h in range(GQ):
                o_ref[h, :, pos] = o[:, h * BLK:(h + 1) * BLK].astype(BF16)

    sp = _attn_specs()
    bias = _attn_bias()
    return _hosted_call(body, comm, "attn_fwd", (N_KV_HEADS, steps), [qt, k, k, vt, vt, snk, bias, bias],
                        [sp["qt"], sp["prev"], sp["cur"], sp["prev_t"], sp["cur_t"], sp["snk"], sp["bias_a"],
                         sp["bias_b"]], [jax.ShapeDtypeStruct(qt.shape, BF16)], [sp["qt"]])[0]


def _attn_bwd(qt, q, k, kt, v, snk, dot_, do, comm=None):
    steps = qt.shape[2] // ATTN_STEP

    def body(qt_ref, q_ref, kp_ref, kc_ref, ktp_ref, ktc_ref, vp_ref, vc_ref, snk_ref, ba_ref, bb_ref, dot_ref,
             do_ref, dq_ref, dkp_ref, dkc_ref, dvp_ref, dvc_ref, dsnk_ref):
        snk = snk_ref[...]
        vc, ktc = vc_ref[...], ktc_ref[...]
        values = (jnp.concatenate([vp_ref[...], vc[:BLK]], axis=0), vc)
        keys_t = (jnp.concatenate([ktp_ref[...], ktc[:, :BLK]], axis=1), ktc)
        row = lax.broadcasted_iota(jnp.int32, (SUBLANES, LANES), 0)
        tile = jnp.zeros((SUBLANES, LANES), F32)
        grads = []
        for (pos, qt_v, kk), vv, kt_v, b_ref in zip(_attn_chains(qt_ref, kp_ref, kc_ref), values, keys_t,
                                                    (ba_ref, bb_ref)):
            e, es, inv = _attn_weights(qt_v, kk, b_ref[...], snk)
            p = e * inv
            dp = lax.dot_general(vv, _side_by_side(dot_ref, pos), NN, preferred_element_type=F32)
            delta = jnp.sum(p * dp, axis=0, keepdims=True)
            ds = (p * (dp - delta)).astype(BF16)
            dq = lax.dot_general(kt_v, ds, NN, preferred_element_type=F32) * ATTN_SCALE
            for h in range(GQ):
                dq_ref[h, :, pos] = dq[:, h * BLK:(h + 1) * BLK]
            q_v = q_ref[:, pos, :].reshape(GQ * BLK, HEAD_DIM)
            do_v = do_ref[:, pos, :].reshape(GQ * BLK, HEAD_DIM)
            grads.append((lax.dot_general(ds, q_v, NN, preferred_element_type=F32),
                          lax.dot_general(p.astype(BF16), do_v, NN, preferred_element_type=F32)))
            per_query = -(es * inv) * delta
            for hh in range(GQ):
                tot = jnp.sum(per_query[:, hh * BLK:(hh + 1) * BLK], axis=1, keepdims=True)
                tile = tile + jnp.where(row == hh, tot, 0.0)
        (dk_a, dv_a), (dk_b, dv_b) = grads
        dkp_ref[...] = dk_a[:BLK]
        dvp_ref[...] = dv_a[:BLK]
        dkc_ref[0:BLK, :] = dk_a[BLK:] + dk_b[:BLK]
        dvc_ref[0:BLK, :] = dv_a[BLK:] + dv_b[:BLK]
        dkc_ref[BLK:ATTN_STEP, :] = dk_b[BLK:]
        dvc_ref[BLK:ATTN_STEP, :] = dv_b[BLK:]
        _accumulate(dsnk_ref, tile, pl.program_id(1) == 0)

    sp = _attn_specs()
    bias = _attn_bias()
    step_blk = pl.BlockSpec((None, BLK, HEAD_DIM), lambda g, n: (g, n, 0))
    kv_shape = jax.ShapeDtypeStruct(k.shape, F32)
    prev_shape = jax.ShapeDtypeStruct((N_KV_HEADS, steps * BLK, HEAD_DIM), F32)
    return _hosted_call(
        body, comm, "attn_bwd", (N_KV_HEADS, steps), [qt, q, k, k, kt, kt, v, v, snk, bias, bias, dot_, do],
        [sp["qt"], sp["q"], sp["prev"], sp["cur"], sp["prev_t"], sp["cur_t"], sp["prev"], sp["cur"], sp["snk"],
         sp["bias_a"], sp["bias_b"], sp["qt"], sp["q"]],
        [jax.ShapeDtypeStruct(qt.shape, F32), prev_shape, kv_shape, prev_shape, kv_shape,
         jax.ShapeDtypeStruct((N_KV_HEADS, SUBLANES, LANES), F32)],
        [sp["qt"], step_blk, sp["cur"], step_blk, sp["cur"],
         pl.BlockSpec((None, SUBLANES, LANES), lambda g, n: (g, 0, 0))])


CONV_CHUNK = 256


def _shift_up(win, s):
    n = win.shape[0]
    return win if s == 0 else pltpu.roll(win, n - s, 0)


def _conv_col_specs(t_len):
    lin = pl.BlockSpec((t_len, LANES), lambda j: (0, V_END // LANES + j))
    gate = pl.BlockSpec((t_len, LANES), lambda j: (0, (V_END + CONV_CH) // LANES + j))
    col = pl.BlockSpec((t_len, LANES), lambda j: (0, j))
    wsp = pl.BlockSpec((CONV_PAD, LANES), lambda j: (0, j))
    return lin, gate, col, wsp


def _conv_fwd(p, w, b):
    t_len = p.shape[0]
    ch = CONV_CHUNK

    def body(lin_ref, gate_ref, w_ref, b_ref, y_ref, hp_ref):
        hp_ref[0:CONV_PAD, :] = jnp.zeros((CONV_PAD, LANES), F32)

        def fill(c, carry):
            r0 = pl.multiple_of(c * ch, ch)
            hp_ref[pl.ds(r0 + CONV_PAD, ch), :] = lin_ref[pl.ds(r0, ch), :] * _sigmoid(gate_ref[pl.ds(r0, ch), :])
            return carry

        lax.fori_loop(0, t_len // ch, fill, 0)

        def conv(c, carry):
            r0 = pl.multiple_of(c * ch, ch)
            win = hp_ref[pl.ds(r0, ch + CONV_PAD), :]
            acc = jnp.zeros((ch, LANES), F32)
            for k in range(CONV_WIDTH):
                acc = acc + _shift_up(win, CONV_PAD - (CONV_WIDTH - 1) + k)[:ch] * w_ref[k:k + 1, :]
            y_ref[pl.ds(r0, ch), :] = acc + b_ref[...]
            return carry

        lax.fori_loop(0, t_len // ch, conv, 0)

    lin, gate, col, wsp = _conv_col_specs(t_len)
    return _call(body, name="conv_fwd", grid=(CONV_CH // LANES,),
                 in_specs=[lin, gate, wsp, pl.BlockSpec((1, LANES), lambda j: (0, j))], out_specs=col,
                 out_shape=jax.ShapeDtypeStruct((t_len, CONV_CH), F32),
                 scratch_shapes=[pltpu.VMEM((t_len + CONV_PAD, LANES), F32)],
                 compiler_params=_params(1))(p, p, w, b)


def _conv_post_fwd(y, g, b):
    t_len = y.shape[0]
    tm = _pick(t_len, (512,))

    def body(y_ref, g_ref, b_ref, o_ref):
        _, xhat = _ln_stats(y_ref[...])
        z = xhat * g_ref[...] + b_ref[...]
        o_ref[...] = (z * _sigmoid(z)).astype(BF16)

    row = pl.BlockSpec((tm, CONV_CH), lambda i: (i, 0))
    vec = pl.BlockSpec((1, CONV_CH), lambda i: (0, 0))
    return _call(body, name="conv_post_fwd", grid=(t_len // tm,), in_specs=[row, vec, vec], out_specs=row,
                 out_shape=jax.ShapeDtypeStruct((t_len, CONV_CH), BF16), compiler_params=_params(1))(y, g, b)


def _conv_post_bwd(dmix, y, g, b):
    t_len = y.shape[0]
    tm = _pick(t_len, (512,))

    def body(do_ref, y_ref, g_ref, b_ref, dy_ref, dg_ref, db_ref, dcb_ref):
        first = pl.program_id(0) == 0
        r, xhat = _ln_stats(y_ref[...])
        g_v = g_ref[...]
        z = xhat * g_v + b_ref[...]
        sig = _sigmoid(z)
        dz = do_ref[...] * (sig * (1.0 + z * (1.0 - sig)))
        _accumulate(db_ref, jnp.sum(dz, axis=0, keepdims=True), first)
        _accumulate(dg_ref, jnp.sum(dz * xhat, axis=0, keepdims=True), first)
        dy = _ln_bwd(dz, r, xhat, g_v)
        dy_ref[...] = dy
        _accumulate(dcb_ref, jnp.sum(dy, axis=0, keepdims=True), first)

    row = pl.BlockSpec((tm, CONV_CH), lambda i: (i, 0))
    do_spec = pl.BlockSpec((tm, CONV_CH), lambda i: (i, Q_END // CONV_CH))
    vec = pl.BlockSpec((1, CONV_CH), lambda i: (0, 0))
    vshape = jax.ShapeDtypeStruct((1, CONV_CH), F32)
    return _call(body, name="conv_post_bwd", grid=(t_len // tm,), in_specs=[do_spec, row, vec, vec],
                 out_specs=[row, vec, vec, vec],
                 out_shape=[jax.ShapeDtypeStruct((t_len, CONV_CH), F32), vshape, vshape, vshape],
                 compiler_params=_params(1))(dmix, y, g, b)


def _conv_bwd(p, dy, w):
    t_len = p.shape[0]
    ch = CONV_CHUNK

    def body(lin_ref, gate_ref, dy_ref, w_ref, dlin_ref, dgate_ref, dw_ref, hp_ref, dyp_ref):
        hp_ref[0:CONV_PAD, :] = jnp.zeros((CONV_PAD, LANES), F32)
        dyp_ref[t_len:t_len + CONV_PAD, :] = jnp.zeros((CONV_PAD, LANES), F32)
        dw_ref[...] = jnp.zeros((CONV_PAD, LANES), F32)

        def fill(c, carry):
            r0 = pl.multiple_of(c * ch, ch)
            hp_ref[pl.ds(r0 + CONV_PAD, ch), :] = lin_ref[pl.ds(r0, ch), :] * _sigmoid(gate_ref[pl.ds(r0, ch), :])
            dyp_ref[pl.ds(r0, ch), :] = dy_ref[pl.ds(r0, ch), :]
            return carry

        lax.fori_loop(0, t_len // ch, fill, 0)

        def step(c, carry):
            r0 = pl.multiple_of(c * ch, ch)
            win_h = hp_ref[pl.ds(r0, ch + CONV_PAD), :]
            win_dy = dyp_ref[pl.ds(r0, ch + CONV_PAD), :]
            dyc = win_dy[:ch]
            dh = jnp.zeros((ch, LANES), F32)
            for k in range(CONV_WIDTH):
                tap = _shift_up(win_h, CONV_PAD - (CONV_WIDTH - 1) + k)[:ch]
                dw_ref[k:k + 1, :] += jnp.sum(dyc * tap, axis=0, keepdims=True)
                dh = dh + _shift_up(win_dy, CONV_WIDTH - 1 - k)[:ch] * w_ref[k:k + 1, :]
            lin = lin_ref[pl.ds(r0, ch), :]
            sig = _sigmoid(gate_ref[pl.ds(r0, ch), :])
            dlin_ref[pl.ds(r0, ch), :] = (dh * sig).astype(BF16)
            dgate_ref[pl.ds(r0, ch), :] = (dh * lin * (sig * (1.0 - sig))).astype(BF16)
            return carry

        lax.fori_loop(0, t_len // ch, step, 0)

    lin, gate, col, wsp = _conv_col_specs(t_len)
    half = jax.ShapeDtypeStruct((t_len, CONV_CH), BF16)
    return _call(body, name="conv_bwd", grid=(CONV_CH // LANES,), in_specs=[lin, gate, col, wsp],
                 out_specs=[col, col, wsp],
                 out_shape=[half, half, jax.ShapeDtypeStruct((CONV_PAD, CONV_CH), F32)],
                 scratch_shapes=[pltpu.VMEM((t_len + CONV_PAD, LANES), F32), pltpu.VMEM((t_len + CONV_PAD, LANES), F32)],
                 compiler_params=_params(1))(p, p, dy, w)


def _sgu_mixed(v, w_ref, bt_ref, j):
    lane = lax.broadcasted_iota(jnp.int32, (BLK, LANES), 1)
    lo = lane < HEAD_DIM
    tri = lax.broadcasted_iota(jnp.int32, (BLK, BLK), 0) >= lax.broadcasted_iota(jnp.int32, (BLK, BLK), 1)
    vs = v[:, j * LANES:(j + 1) * LANES]
    v_lo = jnp.where(lo, vs, 0.0).astype(BF16)
    v_hi = jnp.where(lo, 0.0, vs).astype(BF16)
    w_lo = jnp.where(tri, w_ref[2 * j], 0.0).astype(BF16)
    w_hi = jnp.where(tri, w_ref[2 * j + 1], 0.0).astype(BF16)
    m = (lax.dot_general(w_lo, v_lo, NN, preferred_element_type=F32)
         + lax.dot_general(w_hi, v_hi, NN, preferred_element_type=F32))
    bias = jnp.where(lo, bt_ref[:, 2 * j:2 * j + 1], bt_ref[:, 2 * j + 1:2 * j + 2])
    return m + bias, (v_lo, v_hi, w_lo, w_hi, lo, tri)


def _sgu_specs():
    u_spec = pl.BlockSpec((BLK, SGU_CH), lambda i: (i, CONV_END // SGU_CH))
    v_spec = pl.BlockSpec((BLK, SGU_CH), lambda i: (i, CONV_END // SGU_CH + 1))
    vec = pl.BlockSpec((1, SGU_CH), lambda i: (0, 0))
    w_spec = pl.BlockSpec((SGU_HEADS, BLK, BLK), lambda i: (0, 0, 0))
    bt_spec = pl.BlockSpec((BLK, SGU_HEADS), lambda i: (0, 0))
    row = pl.BlockSpec((BLK, SGU_CH), lambda i: (i, 0))
    return u_spec, v_spec, vec, w_spec, bt_spec, row


def _sgu_fwd(p, g, b, w, bt):
    t_len = p.shape[0]

    def body(u_ref, vin_ref, g_ref, b_ref, w_ref, bt_ref, o_ref):
        _, xhat = _ln_stats(vin_ref[...])
        v = xhat * g_ref[...] + b_ref[...]
        for j in range(SGU_CH // LANES):
            m, _ = _sgu_mixed(v, w_ref, bt_ref, j)
            sl = slice(j * LANES, (j + 1) * LANES)
            o_ref[:, sl] = (u_ref[:, sl] * m).astype(BF16)

    u_spec, v_spec, vec, w_spec, bt_spec, row = _sgu_specs()
    return _call(body, name="sgu_fwd", grid=(t_len // BLK,), in_specs=[u_spec, v_spec, vec, vec, w_spec, bt_spec],
                 out_specs=row, out_shape=jax.ShapeDtypeStruct((t_len, SGU_CH), BF16),
                 compiler_params=_params(1))(p, p, g, b, w, bt)


def _sgu_bwd(p, dmix, g, b, w, bt):
    t_len = p.shape[0]

    def body(u_ref, vin_ref, do_ref, g_ref, b_ref, w_ref, bt_ref, du_ref, dvin_ref, dw_ref, dbt_ref, dg_ref,
             db_ref, dv_ref):
        first = pl.program_id(0) == 0
        r, xhat = _ln_stats(vin_ref[...])
        g_v = g_ref[...]
        v = xhat * g_v + b_ref[...]
        lane = lax.broadcasted_iota(jnp.int32, (BLK, LANES), 1)
        dbt = jnp.zeros((BLK, LANES), F32)

        @pl.when(first)
        def _():
            dw_ref[...] = jnp.zeros((SGU_HEADS, BLK, BLK), F32)

        for j in range(SGU_CH // LANES):
            m, (v_lo, v_hi, w_lo, w_hi, lo, tri) = _sgu_mixed(v, w_ref, bt_ref, j)
            sl = slice(j * LANES, (j + 1) * LANES)
            do_v = do_ref[:, sl]
            du_ref[:, sl] = (do_v * m).astype(BF16)
            dm = do_v * u_ref[:, sl]
            dm_lo = jnp.where(lo, dm, 0.0)
            dm_hi = jnp.where(lo, 0.0, dm)
            dbt = dbt + jnp.where(lane == 2 * j, jnp.sum(dm_lo, axis=-1, keepdims=True), 0.0)
            dbt = dbt + jnp.where(lane == 2 * j + 1, jnp.sum(dm_hi, axis=-1, keepdims=True), 0.0)
            dm_lo, dm_hi = dm_lo.astype(BF16), dm_hi.astype(BF16)
            dw_ref[2 * j] += jnp.where(tri, lax.dot_general(dm_lo, v_lo, NT, preferred_element_type=F32), 0.0)
            dw_ref[2 * j + 1] += jnp.where(tri, lax.dot_general(dm_hi, v_hi, NT, preferred_element_type=F32), 0.0)
            dv_ref[:, sl] = (lax.dot_general(w_lo, dm_lo, TN, preferred_element_type=F32)
                             + lax.dot_general(w_hi, dm_hi, TN, preferred_element_type=F32))
        _accumulate(dbt_ref, dbt, first)
        dv = dv_ref[...]
        _accumulate(db_ref, jnp.sum(dv, axis=0, keepdims=True), first)
        _accumulate(dg_ref, jnp.sum(dv * xhat, axis=0, keepdims=True), first)
        dvin_ref[...] = _ln_bwd(dv, r, xhat, g_v).astype(BF16)

    u_spec, v_spec, vec, w_spec, bt_spec, row = _sgu_specs()
    do_spec = pl.BlockSpec((BLK, SGU_CH), lambda i: (i, (Q_END + CONV_CH) // SGU_CH))
    half = jax.ShapeDtypeStruct((t_len, SGU_CH), BF16)
    vshape = jax.ShapeDtypeStruct((1, SGU_CH), F32)
    return _call(body, name="sgu_bwd", grid=(t_len // BLK,),
                 in_specs=[u_spec, v_spec, do_spec, vec, vec, w_spec, bt_spec],
                 out_specs=[row, row, w_spec, pl.BlockSpec((BLK, LANES), lambda i: (0, 0)), vec, vec],
                 out_shape=[half, half, jax.ShapeDtypeStruct((SGU_HEADS, BLK, BLK), F32),
                            jax.ShapeDtypeStruct((BLK, LANES), F32), vshape, vshape],
                 scratch_shapes=[pltpu.VMEM((BLK, SGU_CH), F32)],
                 compiler_params=_params(1))(p, p, dmix, g, b, w, bt)


def _place():
    x, y, c = lax.axis_index("x"), lax.axis_index("y"), lax.axis_index("c")
    chips = [(1 - x, y), (x, 1 - y), (1 - x, 1 - y)]
    return x, y, c, chips


def _hbm_specs(n):
    return [pl.BlockSpec(memory_space=pltpu.HBM)] * n


def _comm_params():
    return pltpu.CompilerParams(has_side_effects=True)


def _remote(src, dst, send_sem, recv_sem, to):
    return pltpu.make_async_remote_copy(src_ref=src, dst_ref=dst, send_sem=send_sem, recv_sem=recv_sem,
                                        device_id=to, device_id_type=MESH_ID)


def _cast_place(w_local, chip):
    n, rows, cols = w_local.shape

    def body(chip_ref, w_ref, o_ref):
        o_ref[...] = w_ref[...].astype(BF16)

    grid_spec = pltpu.PrefetchScalarGridSpec(
        num_scalar_prefetch=1, grid=(n, rows // ROW_TILE),
        in_specs=[pl.BlockSpec((None, ROW_TILE, cols), lambda l, i, ch: (l, i, 0))],
        out_specs=pl.BlockSpec((None, None, ROW_TILE, cols), lambda l, i, ch: (l, ch[0], i, 0)))
    return _call(body, name="cast_place", grid_spec=grid_spec,
                 out_shape=jax.ShapeDtypeStruct((n, N_CHIPS, rows, cols), BF16), compiler_params=_params(2))(chip, w_local)


def _all_gather_weights(placed, shards):
    n_placed, nt = len(placed), len(placed) + len(shards)

    def body(*refs):
        ins, outs = refs[:nt], refs[nt:2 * nt]
        ici_send, ici_recv, d2d_send, d2d_recv, local_sem = refs[2 * nt:]
        x, y, c, chips = _place()
        me = 2 * x + y
        sibling = (x, y, 1 - c)
        local = [pltpu.make_async_copy(ins[t].at[l], outs[t].at[l, me], local_sem.at[2 * (t - n_placed) + l])
                 for t in range(n_placed, nt) for l in range(2)]
        for cp in local:
            cp.start()
        sends = []
        for t in range(nt):
            src = outs[t].at[c, me] if t < n_placed else ins[t].at[c]
            for j, (px, py) in enumerate(chips):
                sends.append(_remote(src, outs[t].at[c, me], ici_send.at[3 * t + j], ici_recv.at[3 * t + j],
                                     (px, py, c)))
        for cp in sends:
            cp.start()
        for t in range(nt):
            for j, (px, py) in enumerate(chips):
                slab = outs[t].at[c, 2 * px + py]
                _remote(slab, slab, ici_send.at[3 * t + j], ici_recv.at[3 * t + j], (px, py, c)).wait_recv()
                fwd = _remote(slab, slab, d2d_send.at[3 * t + j], d2d_recv.at[3 * t + j], sibling)
                fwd.start()
                sends.append(fwd)
        for t in range(nt):
            for j, (px, py) in enumerate(chips):
                slab = outs[t].at[1 - c, 2 * px + py]
                _remote(slab, slab, d2d_send.at[3 * t + j], d2d_recv.at[3 * t + j], sibling).wait_recv()
        for cp in sends:
            cp.wait_send()
        for cp in local:
            cp.wait()

    out_shape = [jax.ShapeDtypeStruct(p.shape, p.dtype) for p in placed]
    out_shape += [jax.ShapeDtypeStruct((2, N_CHIPS) + s.shape[1:], s.dtype) for s in shards]
    sems = [pltpu.SemaphoreType.DMA((3 * nt,))] * 4 + [pltpu.SemaphoreType.DMA((2 * len(shards),))]
    return _call(body, name="all_gather_weights", in_specs=_hbm_specs(nt), out_specs=_hbm_specs(nt),
                 out_shape=out_shape, scratch_shapes=sems, input_output_aliases={t: t for t in range(n_placed)},
                 compiler_params=_comm_params())(*placed, *shards)


def _gather_comm(bufs, pieces):
    n = len(pieces)
    sems = [pltpu.SemaphoreType.DMA((3 * n,))] * 4

    def half(ref, layer, chip, r0, nr, which):
        return ref.at[layer, chip, pl.ds(pl.multiple_of(r0 + which * (nr // 2), SUBLANES), nr // 2)]

    def start(rd, al, fr, sm):
        ici_send, ici_recv, _, _ = sm
        x, y, c, chips = _place()
        for i, (t, layer, r0, nr) in enumerate(pieces):
            own = half(al[t], layer, 2 * x + y, r0, nr, c)
            for j, (px, py) in enumerate(chips):
                _remote(own, own, ici_send.at[3 * i + j], ici_recv.at[3 * i + j], (px, py, c)).start()

    def finish(rd, al, fr, sm):
        ici_send, ici_recv, d2d_send, d2d_recv = sm
        x, y, c, chips = _place()
        sibling = (x, y, 1 - c)
        passed = []
        for i, (t, layer, r0, nr) in enumerate(pieces):
            for j, (px, py) in enumerate(chips):
                got = half(al[t], layer, 2 * px + py, r0, nr, c)
                _remote(got, got, ici_send.at[3 * i + j], ici_recv.at[3 * i + j], (px, py, c)).wait_recv()
                fwd = _remote(got, got, d2d_send.at[3 * i + j], d2d_recv.at[3 * i + j], sibling)
                fwd.start()
                passed.append(fwd)
        for i, (t, layer, r0, nr) in enumerate(pieces):
            own = half(al[t], layer, 2 * x + y, r0, nr, c)
            for j, (px, py) in enumerate(chips):
                _remote(own, own, ici_send.at[3 * i + j], ici_recv.at[3 * i + j], (px, py, c)).wait_send()
                theirs = half(al[t], layer, 2 * px + py, r0, nr, 1 - c)
                _remote(theirs, theirs, d2d_send.at[3 * i + j], d2d_recv.at[3 * i + j], sibling).wait_recv()
        for fwd in passed:
            fwd.wait_send()

    return _Comm([], bufs, [], sems, start, finish)


def _own_rows(ref, c, which=0):
    hr = ref.shape[-2] // 2
    start = pl.multiple_of((c if which == 0 else 1 - c) * hr, SUBLANES)
    return ref.at[(slice(None),) * (len(ref.shape) - 2) + (pl.ds(start, hr),)]


def _to_owner_comm(stacks, layer):
    nt = len(stacks)
    sems = [pltpu.SemaphoreType.DMA((nt,))] * 2
    fresh = [jax.ShapeDtypeStruct((N_CHIPS, s.shape[2] // 2, s.shape[3]), s.dtype) for s in stacks]

    def copies(rd, fr, sm):
        x, y, c, _ = _place()
        return [_remote(_own_rows(rd[t].at[layer], c, 1), fr[t], sm[0].at[t], sm[1].at[t], (x, y, 1 - c))
                for t in range(nt)]

    def start(rd, al, fr, sm):
        for cp in copies(rd, fr, sm):
            cp.start()

    def finish(rd, al, fr, sm):
        for cp in copies(rd, fr, sm):
            cp.wait()

    return _Comm(stacks, [], fresh, sems, start, finish)


def _chip_comm(partials):
    nt = len(partials)
    sems = [pltpu.SemaphoreType.DMA((3 * nt,))] * 2
    fresh = [jax.ShapeDtypeStruct((3,) + p.shape[1:], p.dtype) for p in partials]

    def each(rd, fr, sm, act):
        x, y, c, chips = _place()
        for t in range(nt):
            for j, (px, py) in enumerate(chips):
                act(_remote(rd[t].at[2 * px + py], fr[t].at[j], sm[0].at[3 * t + j], sm[1].at[3 * t + j], (px, py, c)))

    def start(rd, al, fr, sm):
        each(rd, fr, sm, lambda cp: cp.start())

    def finish(rd, al, fr, sm):
        each(rd, fr, sm, lambda cp: cp.wait())

    return _Comm(partials, [], fresh, sems, start, finish)


def _from_owner_comm(finals, layer):
    nt = len(finals)
    sems = [pltpu.SemaphoreType.DMA((nt,))] * 2

    def start(rd, al, fr, sm):
        x, y, c, _ = _place()
        for t in range(nt):
            mine = _own_rows(al[t].at[layer], c)
            _remote(mine, mine, sm[0].at[t], sm[1].at[t], (x, y, 1 - c)).start()

    def finish(rd, al, fr, sm):
        x, y, c, _ = _place()
        for t in range(nt):
            mine, theirs = _own_rows(al[t].at[layer], c), _own_rows(al[t].at[layer], c, 1)
            _remote(mine, mine, sm[0].at[t], sm[1].at[t], (x, y, 1 - c)).wait_send()
            _remote(theirs, theirs, sm[0].at[t], sm[1].at[t], (x, y, 1 - c)).wait_recv()

    return _Comm([], finals, [], sems, start, finish)


def _all_reduce_small(buf):
    rows = buf.shape[0]

    def body(x_ref, out_ref, all_ref, send_sems, recv_sems, local_sem):
        x, y, c, chips = _place()
        me, sibling = (x, y, c), (x, y, 1 - c)

        def block(px, py, pc):
            return all_ref.at[pl.ds((4 * px + 2 * py + pc) * rows, rows), :]

        def copy(k, blk, to, src=None):
            return _remote(block(*blk) if src is None else src, block(*blk), send_sems.at[k], recv_sems.at[k], to)

        mine = pltpu.make_async_copy(x_ref, block(*me), local_sem)
        mine.start()
        first = [copy(0, me, sibling, src=x_ref)]
        first += [copy(1 + j, me, (*chip, c), src=x_ref) for j, chip in enumerate(chips)]
        for cp in first:
            cp.start()
        passed = [copy(4 + j, (*chip, c), sibling) for j, chip in enumerate(chips)]
        for j, chip in enumerate(chips):
            copy(1 + j, (*chip, c), me).wait_recv()
            passed[j].start()
        copy(0, sibling, me).wait_recv()
        for j, chip in enumerate(chips):
            copy(4 + j, (*chip, 1 - c), me).wait_recv()
        for cp in first + passed:
            cp.wait_send()
        mine.wait()
        tot = all_ref[0:rows, :]
        for k in range(1, N_DEV):
            tot = tot + all_ref[k * rows:(k + 1) * rows, :]
        out_ref[...] = tot

    vm = pl.BlockSpec(memory_space=pltpu.VMEM)
    return _call(body, name="all_reduce_small", in_specs=[vm], out_specs=vm,
                 out_shape=jax.ShapeDtypeStruct(buf.shape, F32),
                 scratch_shapes=[pltpu.VMEM((N_DEV * rows, LANES), F32), pltpu.SemaphoreType.DMA((7,)),
                                 pltpu.SemaphoreType.DMA((7,)), pltpu.SemaphoreType.DMA],
                 compiler_params=pltpu.CompilerParams(has_side_effects=True,
                                                      vmem_limit_bytes=V7X_VMEM_LIMIT_BYTES))(buf)


ROW_TILE = 128


def _chip_partial(stack, received, layer, place):
    _, half_rows, cols = received.shape
    tr = _pick(half_rows, (ROW_TILE, ROW_TILE // 2))
    nh = half_rows // tr

    def body(place_ref, a_ref, b_ref, o_ref):
        o_ref[...] = (a_ref[...].astype(F32) + b_ref[...].astype(F32)).astype(BF16)

    blk = pl.BlockSpec((None, tr, cols), lambda s, i, pr: (s, i, 0))
    grid_spec = pltpu.PrefetchScalarGridSpec(
        num_scalar_prefetch=1, grid=(N_CHIPS, nh),
        in_specs=[pl.BlockSpec((None, None, tr, cols), lambda s, i, pr: (layer, s, pr[0] * nh + i, 0)), blk],
        out_specs=blk)
    return _call(body, name="chip_partial", grid_spec=grid_spec, out_shape=jax.ShapeDtypeStruct(received.shape, BF16),
                 compiler_params=_params(2))(place, stack, received)


def _final_sum(partial, from_chips, layer, place, finals):
    _, half_rows, cols = partial.shape
    tr = _pick(half_rows, (ROW_TILE, ROW_TILE // 2))
    nh = half_rows // tr

    def body(place_ref, a_ref, r_ref, *rest):
        o_ref = rest[-1]
        tot = a_ref[...].astype(F32)
        for j in range(3):
            tot = tot + r_ref[j].astype(F32)
        o_ref[...] = tot

    in_specs = [pl.BlockSpec((None, tr, cols), lambda i, pr: (pr[1], i, 0)),
                pl.BlockSpec((3, tr, cols), lambda i, pr: (0, i, 0))]
    args = [place, partial, from_chips]
    kw = {}
    if finals is not None:
        in_specs.append(pl.BlockSpec(memory_space=pl.ANY))
        args.append(finals)
        kw["input_output_aliases"] = {3: 0}
    grid_spec = pltpu.PrefetchScalarGridSpec(
        num_scalar_prefetch=1, grid=(nh,), in_specs=in_specs,
        out_specs=pl.BlockSpec((None, tr, cols), lambda i, pr: (layer, pr[0] * nh + i, 0)))
    return _call(body, name="final_sum", grid_spec=grid_spec,
                 out_shape=jax.ShapeDtypeStruct((2, 2 * half_rows, cols), F32), compiler_params=_params(1), **kw)(*args)


def _adamw(w, g, m, v, comm=None):
    n, rows, cols = w.shape
    tr = _pick(rows, (ROW_TILE, SUBLANES))
    c1 = 1.0 - ADAM_B1 ** ADAM_STEP
    c2 = 1.0 - ADAM_B2 ** ADAM_STEP

    def body(w_ref, g_ref, m_ref, v_ref, d_ref, nm_ref, nv_ref, go_ref):
        g_v = g_ref[...]
        go_ref[...] = g_v
        nm = ADAM_B1 * m_ref[...] + (1.0 - ADAM_B1) * g_v
        nv = ADAM_B2 * v_ref[...] + (1.0 - ADAM_B2) * (g_v * g_v)
        nm_ref[...] = nm
        nv_ref[...] = nv
        d_ref[...] = -ADAM_LR * ((nm / c1) / (jnp.sqrt(nv / c2) + ADAM_EPS) + ADAM_WD * w_ref[...])

    blk = pl.BlockSpec((None, tr, cols), lambda l, i: (l, i, 0))
    shape = jax.ShapeDtypeStruct(w.shape, F32)
    return _hosted_call(body, comm, "adamw", (n, rows // tr), [w, g, m, v], [blk] * 4, [shape] * 4, [blk] * 4)


def _to_heads(a, n_heads, transposed=False):
    t_len = a.shape[0]
    return a.reshape(t_len, n_heads, HEAD_DIM).transpose((1, 2, 0) if transposed else (1, 0, 2))


def _from_heads(a, transposed=False):
    a = a.transpose((2, 0, 1) if transposed else (1, 0, 2))
    return a.reshape(a.shape[0], a.shape[1] * HEAD_DIM)


class _Schedule:
    def __init__(self):
        self.sites = {}
        self.open = []

    def add(self, site, make, done=None):
        self.sites.setdefault(site, []).append((make, done))

    def begin(self, site):
        self.open = [(make(), done) for make, done in self.sites.pop(site, [])]
        return _merge_comms([cm for cm, _ in self.open])

    def end(self):
        for cm, done in self.open:
            if done is not None:
                done(cm)
        self.open = []


def _ffn_fwd(x, gain, wg, which, layer, sched):
    w_in_name, w_out_name = f"ffn{which}_w_in", f"ffn{which}_w_out"
    h = _rmsnorm_fwd(x, gain)
    comm = sched.begin(("ffn_in", layer, which))
    gu, act = _mm_ffn_in(h, wg[w_in_name], layer, comm)
    sched.end()
    comm = sched.begin(("ffn_out", layer, which))
    x_new = _mm_out_res("ffn_out", act, wg[w_out_name], layer, x, FFN_RESIDUAL_WEIGHT, comm)[0]
    sched.end()
    return x_new, (x, h, gu, act)


def _ffn_bwd(dx, dxb, saved, gain, wg, which, layer, stacks, sched, weights_first=False):
    w_in_name, w_out_name = f"ffn{which}_w_in", f"ffn{which}_w_out"
    x, h, gu, act = saved
    out = {}

    def dact():
        comm = sched.begin(("ffn_dact", layer, which))
        out["dgu"] = _mm_dact_swiglu(dxb, wg[w_out_name], layer, gu, FFN_RESIDUAL_WEIGHT, comm)
        sched.end()

    def dw_out():
        comm = sched.begin(("ffn_dw_out", layer, which))
        stacks[w_out_name] = _dw_rows("ffn_dw_out", act, dxb, layer, stacks[w_out_name], FFN_RESIDUAL_WEIGHT, comm)
        sched.end()

    def dh():
        comm = sched.begin(("ffn_dh", layer, which))
        out["dh"] = _mm_dh_ffn(out["dgu"], wg[w_in_name], layer, comm)
        sched.end()

    def dw_in():
        comm = sched.begin(("ffn_dw_in", layer, which))
        stacks[w_in_name] = _dw_ffn_in(h, out["dgu"], layer, stacks[w_in_name], comm)
        sched.end()

    for step in ((dact, dw_in, dw_out, dh) if weights_first else (dact, dw_out, dh, dw_in)):
        step()
    return _rmsnorm_bwd(out["dh"], x, gain, dx)


def _mix_fwd(x, gain, wg, layer, small, tables, sched):
    h = _rmsnorm_fwd(x, gain)
    comm = sched.begin(("mix_in", layer))
    p = _mm_proj(h, wg["w_in"], layer, comm)[0]
    sched.end()
    qkv = _rope_fwd(p, tables)
    q_rows, k_rows, v_rows = qkv[:, :Q_END], qkv[:, Q_END:K_END], qkv[:, K_END:V_END]
    qt, q = _to_heads(q_rows, N_Q_HEADS, True), _to_heads(q_rows, N_Q_HEADS)
    kt, k = _to_heads(k_rows, N_KV_HEADS, True), _to_heads(k_rows, N_KV_HEADS)
    vt, v = _to_heads(v_rows, N_KV_HEADS, True), _to_heads(v_rows, N_KV_HEADS)
    comm = sched.begin(("attn", layer))
    attn = _from_heads(_attn_fwd(qt, k, vt, small["snk"], comm), True)
    sched.end()
    y = _conv_fwd(p, small["conv_w"], small["conv_b"])
    conv = _conv_post_fwd(y, small["conv_ln_g"], small["conv_ln_b"])
    sgu = _sgu_fwd(p, small["sgu_ln_g"], small["sgu_ln_b"], small["sgu_w"], small["sgu_bt"])
    mix = jnp.concatenate([attn, conv, sgu], axis=1)
    x_new = _mm_out_res("mix_out", mix, wg["w_out"], layer, x, 1.0)[0]
    return x_new, (x, h, p, (qt, q, k, kt, v), y, mix)


def _mix_bwd(dx, dxb, saved, gain, wg, small, tables, layer, stacks, sched):
    x, h, p, (qt, q, k, kt, v), y, mix = saved
    comm = sched.begin(("mix_dout", layer))
    dmix = _mm_dmix(dxb, wg["w_out"], layer, comm)
    sched.end()
    stacks["w_out"] = _dw_rows("mix_dw_out", mix, dxb, layer, stacks["w_out"], 1.0)
    do_rows = dmix[:, :Q_END].astype(BF16)
    comm = sched.begin(("attn_bwd", layer))
    dqt, dkp, dkc, dvp, dvc, dsnk = _attn_bwd(qt, q, k, kt, v, small["snk"], _to_heads(do_rows, N_Q_HEADS, True),
                                              _to_heads(do_rows, N_Q_HEADS), comm)
    sched.end()
    dy, d_ln_g, d_ln_b, d_conv_b = _conv_post_bwd(dmix, y, small["conv_ln_g"], small["conv_ln_b"])
    dalin, dagate, d_conv_w = _conv_bwd(p, dy, small["conv_w"])
    du, dvin, d_sgu_w, d_sgu_bt, d_sgu_g, d_sgu_b = _sgu_bwd(p, dmix, small["sgu_ln_g"], small["sgu_ln_b"],
                                                           small["sgu_w"], small["sgu_bt"])
    dp = _assemble_dp(_from_heads(dqt, True), _from_heads(dkc), _from_heads(dkp), _from_heads(dvc), _from_heads(dvp),
                      tables, dalin, dagate, du, dvin)
    comm = sched.begin(("mix_dw_in", layer))
    stacks["w_in"] = _dw_mix_in(h, dp, layer, stacks["w_in"], comm)
    sched.end()
    comm = sched.begin(("mix_dh", layer))
    dh = _mm_dh_mix(dp, wg["w_in"], layer, comm)
    sched.end()
    dx_in, dxb_in, dgain = _rmsnorm_bwd(dh, x, gain, dx)
    grads = {
        "norm_mix": dgain[0], "conv_dw_w": d_conv_w[:CONV_WIDTH], "conv_dw_b": d_conv_b[0],
        "conv_ln_g": d_ln_g[0], "conv_ln_b": d_ln_b[0], "sgu_ln_g": d_sgu_g[0], "sgu_ln_b": d_sgu_b[0],
        "sgu_w": d_sgu_w, "sgu_b": d_sgu_bt[:, :SGU_HEADS].T, "attn_sinks": dsnk[:, :GQ, 0].reshape(N_Q_HEADS),
    }
    return dx_in, dxb_in, grads


BIG = ("ffn1_w_in", "ffn1_w_out", "w_in", "w_out", "ffn2_w_in", "ffn2_w_out")
SMALL = ("norm_ffn1", "norm_mix", "conv_dw_w", "conv_dw_b", "conv_ln_g", "conv_ln_b", "sgu_ln_g", "sgu_ln_b",
         "sgu_w", "sgu_b", "attn_sinks", "norm_ffn2", "final_norm")
WEIGHTS = ("norm_ffn1", "ffn1_w_in", "ffn1_w_out", "norm_mix", "w_in", "conv_dw_w", "conv_dw_b", "conv_ln_g",
           "conv_ln_b", "sgu_ln_g", "sgu_ln_b", "sgu_w", "sgu_b", "attn_sinks", "w_out", "norm_ffn2", "ffn2_w_in",
           "ffn2_w_out", "final_norm")
PACK_ROWS = SUBLANES * LANES

FIRST_GATHER = [("ffn1_w_in", 0, None)]
FORWARD_PLAN = {
    ("ffn_in", 0, 1): [("ffn1_w_out", 0, None), ("w_in", 0, None)],
    ("ffn_out", 0, 1): [("ffn2_w_in", 0, 0)],
    ("mix_in", 0): [("w_out", 0, None)],
    ("attn", 0): [("ffn2_w_in", 0, 1)],
    ("ffn_in", 0, 2): [("ffn2_w_out", 0, None), ("ffn1_w_in", 1, 0)],
    ("ffn_out", 0, 2): [("ffn1_w_in", 1, 1)],
    ("ffn_in", 1, 1): [("ffn1_w_out", 1, None), ("w_in", 1, None)],
    ("ffn_out", 1, 1): [("ffn2_w_in", 1, 0)],
    ("mix_in", 1): [("w_out", 1, None)],
    ("attn", 1): [("ffn2_w_in", 1, 1)],
    ("ffn_in", 1, 2): [("ffn2_w_out", 1, None)],
}
SUBLAYER_WEIGHTS = {"ffn1": ["ffn1_w_out", "ffn1_w_in"], "ffn2": ["ffn2_w_out", "ffn2_w_in"], "mix": ["w_out", "w_in"]}


def _pack(arrays):
    flat = jnp.concatenate([a.reshape(-1).astype(F32) for a in arrays])
    pad = (-flat.shape[0]) % PACK_ROWS
    return jnp.pad(flat, (0, pad)).reshape(-1, LANES)


def _unpack(buf, shapes):
    flat = buf.reshape(-1)
    out, off = [], 0
    for s in shapes:
        n = 1
        for d in s:
            n *= d
        out.append(flat[off:off + n].reshape(s))
        off += n
    return out


def kernel(x, positions, norm_ffn1, ffn1_w_in, ffn1_w_out, norm_mix, w_in, conv_dw_w, conv_dw_b, conv_ln_g, conv_ln_b, sgu_ln_g, sgu_ln_b, sgu_w, sgu_b, attn_sinks, w_out, norm_ffn2, ffn2_w_in, ffn2_w_out, final_norm, loss_target, m_norm_ffn1, m_ffn1_w_in, m_ffn1_w_out, m_norm_mix, m_w_in, m_conv_dw_w, m_conv_dw_b, m_conv_ln_g, m_conv_ln_b, m_sgu_ln_g, m_sgu_ln_b, m_sgu_w, m_sgu_b, m_attn_sinks, m_w_out, m_norm_ffn2, m_ffn2_w_in, m_ffn2_w_out, m_final_norm, v_norm_ffn1, v_ffn1_w_in, v_ffn1_w_out, v_norm_mix, v_w_in, v_conv_dw_w, v_conv_dw_b, v_conv_ln_g, v_conv_ln_b, v_sgu_ln_g, v_sgu_ln_b, v_sgu_w, v_sgu_b, v_attn_sinks, v_w_out, v_norm_ffn2, v_ffn2_w_in, v_ffn2_w_out, v_final_norm):
    w = dict(norm_ffn1=norm_ffn1, ffn1_w_in=ffn1_w_in, ffn1_w_out=ffn1_w_out, norm_mix=norm_mix, w_in=w_in,
             conv_dw_w=conv_dw_w, conv_dw_b=conv_dw_b, conv_ln_g=conv_ln_g, conv_ln_b=conv_ln_b, sgu_ln_g=sgu_ln_g,
             sgu_ln_b=sgu_ln_b, sgu_w=sgu_w, sgu_b=sgu_b, attn_sinks=attn_sinks, w_out=w_out, norm_ffn2=norm_ffn2,
             ffn2_w_in=ffn2_w_in, ffn2_w_out=ffn2_w_out, final_norm=final_norm)
    m = dict(norm_ffn1=m_norm_ffn1, ffn1_w_in=m_ffn1_w_in, ffn1_w_out=m_ffn1_w_out, norm_mix=m_norm_mix, w_in=m_w_in,
             conv_dw_w=m_conv_dw_w, conv_dw_b=m_conv_dw_b, conv_ln_g=m_conv_ln_g, conv_ln_b=m_conv_ln_b,
             sgu_ln_g=m_sgu_ln_g, sgu_ln_b=m_sgu_ln_b, sgu_w=m_sgu_w, sgu_b=m_sgu_b, attn_sinks=m_attn_sinks,
             w_out=m_w_out, norm_ffn2=m_norm_ffn2, ffn2_w_in=m_ffn2_w_in, ffn2_w_out=m_ffn2_w_out,
             final_norm=m_final_norm)
    v = dict(norm_ffn1=v_norm_ffn1, ffn1_w_in=v_ffn1_w_in, ffn1_w_out=v_ffn1_w_out, norm_mix=v_norm_mix, w_in=v_w_in,
             conv_dw_w=v_conv_dw_w, conv_dw_b=v_conv_dw_b, conv_ln_g=v_conv_ln_g, conv_ln_b=v_conv_ln_b,
             sgu_ln_g=v_sgu_ln_g, sgu_ln_b=v_sgu_ln_b, sgu_w=v_sgu_w, sgu_b=v_sgu_b, attn_sinks=v_attn_sinks,
             w_out=v_w_out, norm_ffn2=v_norm_ffn2, ffn2_w_in=v_ffn2_w_in, ffn2_w_out=v_ffn2_w_out,
             final_norm=v_final_norm)
    depth = norm_ffn1.shape[0]
    assert depth == 2 and x.shape[0] == 1
    xc = lax.axis_index("x")
    yc = lax.axis_index("y")
    cc = lax.axis_index("c")
    chip = 2 * xc + yc

    chip_arr = chip.reshape(1).astype(jnp.int32)
    place = jnp.stack([cc, chip]).astype(jnp.int32)
    wg = {n: _cast_place(w[n], chip_arr) for n in BIG}
    sched = _Schedule()

    def gather(pieces):
        names = sorted({n for n, _, _ in pieces})
        half = w["ffn1_w_in"].shape[1] // 2

        def make():
            rows = lambda n, part: (0, wg[n].shape[2]) if part is None else (part * half, half)
            return _gather_comm([wg[n] for n in names], [(names.index(n), l, *rows(n, part)) for n, l, part in pieces])

        return make, lambda cm: wg.update(zip(names, cm.aliased_out))

    make, done = gather(FIRST_GATHER)
    first = make()
    _standalone("gather_first", first)
    done(first)
    for site, pieces in FORWARD_PLAN.items():
        sched.add(site, *gather(pieces))
    conv_w_full = _all_gather_weights([], [conv_dw_w])[0].transpose(0, 2, 1, 3).reshape(depth, CONV_WIDTH, CONV_CH)
    conv_w_full = jnp.pad(conv_w_full, ((0, 0), (0, CONV_PAD - CONV_WIDTH), (0, 0)))

    tables = _rope_tables(positions)
    small = []
    for l in range(depth):
        small.append(dict(
            snk=jnp.broadcast_to(attn_sinks[l].reshape(N_KV_HEADS, 1, GQ, 1), (N_KV_HEADS, 1, GQ, BLK)).reshape(
                N_KV_HEADS, 1, GQ * BLK),
            conv_w=conv_w_full[l], conv_b=conv_dw_b[l][None], conv_ln_g=conv_ln_g[l][None],
            conv_ln_b=conv_ln_b[l][None], sgu_ln_g=sgu_ln_g[l][None], sgu_ln_b=sgu_ln_b[l][None], sgu_w=sgu_w[l],
            sgu_bt=sgu_b[l].T))

    xs = x[0]
    saved = []
    for l in range(depth):
        xs, s1 = _ffn_fwd(xs, norm_ffn1[l][None], wg, 1, l, sched)
        xs, s2 = _mix_fwd(xs, norm_mix[l][None], wg, l, small[l], tables, sched)
        xs, s3 = _ffn_fwd(xs, norm_ffn2[l][None], wg, 2, l, sched)
        saved.append((s1, s2, s3))
    dx, dxb, d_final, loss_part = _loss_head(xs, final_norm[None], loss_target[0])

    stacks = {n: None for n in BIG}
    partials, from_chips = {}, {}
    finals = {n: None for n in BIG}

    def to_owner(layer, names):
        def done(cm):
            for n, received in zip(names, cm.fresh_out):
                partials[n, layer] = _chip_partial(stacks[n], received, layer, place)

        return lambda: _to_owner_comm([stacks[n] for n in names], layer), done

    def between_chips(layer, names, then_sum=()):
        def done(cm):
            from_chips.update({(n, layer): r for n, r in zip(names, cm.fresh_out)})
            for n in then_sum:
                finals[n] = _final_sum(partials[n, layer], from_chips[n, layer], layer, place, finals[n])

        return lambda: _chip_comm([partials[n, layer] for n in names]), done

    def from_owner(layer, names):
        return (lambda: _from_owner_comm([finals[n] for n in names], layer),
                lambda cm: finals.update(zip(names, cm.aliased_out)))

    order = [(l, kind) for l in reversed(range(depth)) for kind in ("ffn2", "mix", "ffn1")]
    for (layer, kind), (nxt_layer, nxt_kind) in zip(order[:-1], order[1:]):
        names = SUBLAYER_WEIGHTS[kind]
        if (nxt_layer, nxt_kind) == order[-1]:
            w_out_name, w_in_name = SUBLAYER_WEIGHTS[nxt_kind]
            which = int(nxt_kind[-1])
            sched.add(("ffn_dact", nxt_layer, which), *to_owner(layer, names))
            sched.add(("ffn_dw_in", nxt_layer, which), *between_chips(layer, names, names))
            sched.add(("ffn_dw_out", nxt_layer, which), *from_owner(layer, names))
            sched.add(("ffn_dw_out", nxt_layer, which), *to_owner(nxt_layer, [w_in_name]))
            sched.add(("ffn_dh", nxt_layer, which), *between_chips(nxt_layer, [w_in_name], [w_in_name]))
            sched.add(("ffn_dh", nxt_layer, which), *to_owner(nxt_layer, [w_out_name]))
        elif nxt_kind == "mix":
            after = order[order.index((nxt_layer, nxt_kind)) + 1]
            sched.add(("mix_dout", nxt_layer), *to_owner(layer, names))
            if after == order[-1]:
                sched.add(("attn_bwd", nxt_layer), *between_chips(layer, names[1:], names[1:]))
                sched.add(("mix_dw_in", nxt_layer), *between_chips(layer, names[:1], names[:1]))
                sched.add(("mix_dh", nxt_layer), *from_owner(layer, names))
            else:
                sched.add(("attn_bwd", nxt_layer), *between_chips(layer, names[:1], names[:1]))
                sched.add(("mix_dh", nxt_layer), *from_owner(layer, names[:1]))
                which = int(after[1][-1])
                sched.add(("ffn_dh", after[0], which), *between_chips(layer, names[1:], names[1:]))
                sched.add(("ffn_dw_in", after[0], which), *from_owner(layer, names[1:]))
        else:
            which = int(nxt_kind[-1])
            sched.add(("ffn_dact", nxt_layer, which), *to_owner(layer, names))
            if kind == "mix":
                sched.add(("ffn_dw_out", nxt_layer, which), *between_chips(layer, names, names))
            else:
                sched.add(("ffn_dw_out", nxt_layer, which), *between_chips(layer, names[:1]))
                sched.add(("ffn_dh", nxt_layer, which), *between_chips(layer, names[1:], names))
            sched.add(("ffn_dw_in", nxt_layer, which), *from_owner(layer, names))

    small_grads = [None] * depth
    for l in reversed(range(depth)):
        s1, s2, s3 = saved[l]
        dx, dxb, dg2 = _ffn_bwd(dx, dxb, s3, norm_ffn2[l][None], wg, 2, l, stacks, sched)
        dx, dxb, gm = _mix_bwd(dx, dxb, s2, norm_mix[l][None], wg, small[l], tables, l, stacks, sched)
        dx, dxb, dg1 = _ffn_bwd(dx, dxb, s1, norm_ffn1[l][None], wg, 1, l, stacks, sched, weights_first=l == 0)
        gm["norm_ffn1"] = dg1[0]
        gm["norm_ffn2"] = dg2[0]
        small_grads[l] = gm
    grad_x = dx[None]
    assert not sched.sites, sched.sites

    per_layer = [n for n in SMALL if n != "final_norm"]
    small_local = [jnp.stack([small_grads[l][n] for l in range(depth)]) for n in per_layer]
    small_local += [d_final[0], loss_part[0, :1]]
    small_shapes = [a.shape for a in small_local]
    summed = _unpack(_all_reduce_small(_pack(small_local)), small_shapes)
    loss = summed[-1][0]
    sg = dict(zip(per_layer + ["final_norm"], summed[:-1]))
    sg["conv_dw_w"] = lax.dynamic_slice_in_dim(sg["conv_dw_w"], chip * LANES, LANES, axis=2)

    delta, new_m, new_v = {}, {}, {}
    shapes = [w[n].shape for n in SMALL]
    packed = [_pack([d[n] for n in SMALL])[None] for d in (w, sg, m, v)]
    last_layer, last_kind = order[-1]
    names = SUBLAYER_WEIGHTS[last_kind]
    make, done = between_chips(last_layer, names[:1], names[:1])
    cm = make()
    outs = _adamw(*packed, comm=cm)
    done(cm)
    for d, buf in zip((delta, new_m, new_v), outs[:3]):
        d.update(zip(SMALL, _unpack(buf[0], shapes)))
    make, done = from_owner(last_layer, names)
    cm = make()
    _standalone("from_owner", cm)
    done(cm)
    big_grads = dict(finals)
    for n in BIG:
        delta[n], new_m[n], new_v[n], big_grads[n] = _adamw(w[n], big_grads[n], m[n], v[n])
    grads = {**big_grads, **sg}
    return (loss, grad_x, *[grads[n] for n in WEIGHTS], *[delta[n] for n in WEIGHTS],
            *[new_m[n] for n in WEIGHTS], *[new_v[n] for n in WEIGHTS])
```

```python
import functools

import jax
import jax.numpy as jnp
from jax import lax
from jax.experimental import pallas as pl
from jax.experimental.pallas import tpu as pltpu

F32 = jnp.float32
BF16 = jnp.bfloat16
MESH_ID = pl.DeviceIdType.MESH

V7X_VMEM_LIMIT_BYTES = 56 * 2**20
LANES = 128
SUBLANES = 8

HEAD_DIM = 64
N_Q_HEADS = 16
N_KV_HEADS = 4
GQ = N_Q_HEADS // N_KV_HEADS
BLK = 128
ROT_HALF = 8
ROPE_THETA = 500000.0
CONV_WIDTH = 31
CONV_PAD = 32
CONV_CH = 512
SGU_CH = 512
SGU_HEADS = 8
Q_END = N_Q_HEADS * HEAD_DIM
K_END = Q_END + N_KV_HEADS * HEAD_DIM
V_END = K_END + N_KV_HEADS * HEAD_DIM
CONV_END = V_END + 2 * CONV_CH
IN_COLS = CONV_END + 2 * SGU_CH
NORM_EPS = 1e-5
FFN_RESIDUAL_WEIGHT = 0.5
N_CHIPS = 4
N_DEV = 8

ADAM_LR = 0.001
ADAM_B1 = 0.9
ADAM_B2 = 0.999
ADAM_EPS = 1e-08
ADAM_WD = 0.01
ADAM_STEP = 10

NN = (((1,), (0,)), ((), ()))
NT = (((1,), (1,)), ((), ()))
TN = (((0,), (0,)), ((), ()))


def _pick(n, cands):
    for c in cands:
        if n % c == 0:
            return c
    raise ValueError(f"no tile of {cands} divides {n}")


def _params(n_axes):
    return pltpu.CompilerParams(dimension_semantics=("arbitrary",) * n_axes, vmem_limit_bytes=V7X_VMEM_LIMIT_BYTES)


def _call(body, **kw):
    return pl.pallas_call(body, **kw)


def _sigmoid(x):
    return 1.0 / (1.0 + jnp.exp(-x))


class _Comm:
    def __init__(self, reads, aliased, fresh, sems, start, finish):
        self.reads, self.aliased, self.fresh, self.sems = list(reads), list(aliased), list(fresh), list(sems)
        self.start, self.finish = start, finish
        self.aliased_out, self.fresh_out = None, None


def _merge_comms(comms):
    comms = [cm for cm in comms if cm is not None]
    if not comms:
        return None
    if len(comms) == 1:
        return comms[0]

    def split(refs, counts):
        out, off = [], 0
        for n in counts:
            out.append(refs[off:off + n])
            off += n
        return out

    def run(which):
        def f(rd, al, fr, sm):
            parts = zip(split(rd, [len(cm.reads) for cm in comms]), split(al, [len(cm.aliased) for cm in comms]),
                        split(fr, [len(cm.fresh) for cm in comms]), split(sm, [len(cm.sems) for cm in comms]))
            for cm, (r, a, f_, s) in zip(comms, parts):
                getattr(cm, which)(r, a, f_, s)
        return f

    merged = _Comm(sum((cm.reads for cm in comms), []), sum((cm.aliased for cm in comms), []),
                   sum((cm.fresh for cm in comms), []), sum((cm.sems for cm in comms), []), run("start"), run("finish"))
    merged.parts = comms
    return merged


def _hosted_call(body, comm, name, grid, inputs, in_specs, out_shape, out_specs, scratch_shapes=(), aliases=None):
    n_in, n_out, n_scr = len(inputs), len(out_shape), len(scratch_shapes)
    aliases = dict(aliases or {})
    if comm is None:
        return _call(body, name=name, grid=grid, in_specs=list(in_specs), out_specs=list(out_specs),
                     out_shape=list(out_shape), scratch_shapes=list(scratch_shapes), input_output_aliases=aliases,
                     compiler_params=_params(len(grid)))(*inputs)
    nr, na, nf = len(comm.reads), len(comm.aliased), len(comm.fresh)

    def full(*refs):
        ins = refs[:n_in]
        rd = refs[n_in:n_in + nr]
        pos = n_in + nr + na
        outs = refs[pos:pos + n_out]
        al = refs[pos + n_out:pos + n_out + na]
        fr = refs[pos + n_out + na:pos + n_out + na + nf]
        pos = pos + n_out + na + nf
        scr = refs[pos:pos + n_scr]
        sems = refs[pos + n_scr:]
        first, last = None, None
        for axis, size in enumerate(grid):
            f, l = pl.program_id(axis) == 0, pl.program_id(axis) == size - 1
            first = f if first is None else jnp.logical_and(first, f)
            last = l if last is None else jnp.logical_and(last, l)

        @pl.when(first)
        def _():
            comm.start(rd, al, fr, sems)

        body(*ins, *outs, *scr)

        @pl.when(last)
        def _():
            comm.finish(rd, al, fr, sems)

    hbm = pl.BlockSpec(memory_space=pltpu.HBM)
    for i in range(na):
        aliases[n_in + nr + i] = n_out + i
    struct = [jax.ShapeDtypeStruct(a.shape, a.dtype) for a in comm.aliased]
    res = _call(full, name=name, grid=grid, in_specs=list(in_specs) + [hbm] * (nr + na),
                out_specs=list(out_specs) + [hbm] * (na + nf), out_shape=list(out_shape) + struct + comm.fresh,
                scratch_shapes=list(scratch_shapes) + comm.sems, input_output_aliases=aliases,
                compiler_params=pltpu.CompilerParams(dimension_semantics=("arbitrary",) * len(grid),
                                                     vmem_limit_bytes=V7X_VMEM_LIMIT_BYTES, has_side_effects=True),
                )(*inputs, *comm.reads, *comm.aliased)
    _deliver(comm, res[n_out:n_out + na], res[n_out + na:])
    return res[:n_out]


def _deliver(comm, aliased_out, fresh_out):
    comm.aliased_out, comm.fresh_out = list(aliased_out), list(fresh_out)
    off_a = off_f = 0
    for part in getattr(comm, "parts", []):
        _deliver(part, aliased_out[off_a:off_a + len(part.aliased)], fresh_out[off_f:off_f + len(part.fresh)])
        off_a += len(part.aliased)
        off_f += len(part.fresh)


def _standalone(name, comm):
    def body(*refs):
        nr, na, nf = len(comm.reads), len(comm.aliased), len(comm.fresh)
        rd, al, fr, sems = refs[:nr], refs[nr + na:nr + 2 * na], refs[nr + 2 * na:nr + 2 * na + nf], refs[nr + 2 * na + nf:]
        comm.start(rd, al, fr, sems)
        comm.finish(rd, al, fr, sems)

    nr, na, nf = len(comm.reads), len(comm.aliased), len(comm.fresh)
    struct = [jax.ShapeDtypeStruct(a.shape, a.dtype) for a in comm.aliased]
    res = _call(body, name=name, in_specs=_hbm_specs(nr + na), out_specs=_hbm_specs(na + nf),
                out_shape=struct + comm.fresh, scratch_shapes=comm.sems,
                input_output_aliases={nr + i: i for i in range(na)},
                compiler_params=_comm_params())(*comm.reads, *comm.aliased)
    _deliver(comm, res[:na], res[na:])


def _matmul(name, grid, a_ops, b_ops, terms, dims, out_shape, out_specs, epilogue, extra_ops=(), nk=1,
            acc_shapes=(), alias=None, comm=None, chunk=None):
    na, nb, ne, no = len(a_ops), len(b_ops), len(extra_ops), len(out_shape)

    def body(*refs):
        a = refs[:na]
        b = refs[na:na + nb]
        e = refs[na + nb:na + nb + ne]
        first_out = na + nb + ne + (1 if alias is not None else 0)
        o = refs[first_out:first_out + no]
        accs = refs[first_out + no:]

        def partial(t, cols=None):
            tot = None
            for ai, bi in t:
                rhs = b[bi][...] if cols is None else (b[bi][cols, :] if dims == NT else b[bi][:, cols])
                d = lax.dot_general(a[ai][...], rhs, dims, preferred_element_type=F32)
                tot = d if tot is None else tot + d
            return tot

        if nk == 1 and chunk is not None:
            width = b[0].shape[0] if dims == NT else b[0].shape[1]
            for c0 in range(0, width, chunk):
                cols = slice(c0, min(c0 + chunk, width))
                epilogue([partial(t, cols) for t in terms], e, o, cols)
        elif nk == 1:
            epilogue([partial(t) for t in terms], e, o)
        else:
            k = pl.program_id(len(grid) - 1)

            @pl.when(k == 0)
            def _():
                for acc in accs:
                    acc[...] = jnp.zeros(acc.shape, F32)

            for acc, t in zip(accs, terms):
                acc[...] += partial(t)

            @pl.when(k == nk - 1)
            def _():
                epilogue([acc[...] for acc in accs], e, o)

    ops = list(a_ops) + list(b_ops) + list(extra_ops)
    arrays = [x for x, _ in ops]
    in_specs = [s for _, s in ops]
    aliases = {}
    if alias is not None:
        arrays.append(alias[0])
        in_specs.append(pl.BlockSpec(memory_space=pl.ANY))
        aliases[len(arrays) - 1] = alias[1]
    scratch = [pltpu.VMEM(s, F32) for s in acc_shapes] if nk > 1 else []
    return _hosted_call(body, comm, name, grid, arrays, in_specs, out_shape, out_specs, scratch, aliases)


def _mm_ffn_in(h, w_g, layer, comm=None):
    t_len, d = h.shape
    fs = w_g.shape[3]
    f = 2 * fs
    tm = _pick(t_len, (2048, 1024, 512))
    tn = _pick(fs, (256, 128))
    nj = fs // tn

    def epilogue(accs, e, o):
        g, u = accs
        o[0][0] = g.astype(BF16)
        o[0][1] = u.astype(BF16)
        o[1][...] = (g * _sigmoid(g) * u).astype(BF16)

    return _matmul(
        "ffn_in", (t_len // tm, 2, nj),
        [(h, pl.BlockSpec((tm, d), lambda i, s, j: (i, 0)))],
        [(w_g, pl.BlockSpec((None, None, d, tn), lambda i, s, j: (layer, s, 0, j))),
         (w_g, pl.BlockSpec((None, None, d, tn), lambda i, s, j: (layer, s + 2, 0, j)))],
        [[(0, 0)], [(0, 1)]], NN,
        [jax.ShapeDtypeStruct((2, t_len, f), BF16), jax.ShapeDtypeStruct((t_len, f), BF16)],
        [pl.BlockSpec((2, tm, tn), lambda i, s, j: (0, i, s * nj + j)),
         pl.BlockSpec((tm, tn), lambda i, s, j: (i, s * nj + j))],
        epilogue, comm=comm)


def _mm_out_res(name, a, w_g, layer, x, scale, comm=None):
    t_len = a.shape[0]
    ks, n = w_g.shape[2], w_g.shape[3]
    tm = _pick(t_len, (512,))
    tn = _pick(n, (1024,))

    def epilogue(accs, e, o):
        o[0][...] = e[0][...] + scale * accs[0]

    return _matmul(
        name, (t_len // tm, n // tn),
        [(a, pl.BlockSpec((tm, ks), lambda i, j, s=s: (i, s))) for s in range(N_CHIPS)],
        [(w_g, pl.BlockSpec((None, None, ks, tn), lambda i, j, s=s: (layer, s, 0, j))) for s in range(N_CHIPS)],
        [[(s, s) for s in range(N_CHIPS)]], NN,
        [jax.ShapeDtypeStruct((t_len, n), F32)],
        [pl.BlockSpec((tm, tn), lambda i, j: (i, j))],
        epilogue, extra_ops=[(x, pl.BlockSpec((tm, tn), lambda i, j: (i, j)))], comm=comm)


def _mm_proj(h, w_g, layer, comm=None):
    t_len, d = h.shape
    cs = w_g.shape[3]
    tm = _pick(t_len, (1024, 512))

    def epilogue(accs, e, o):
        o[0][...] = accs[0]

    return _matmul(
        "mix_in", (t_len // tm, N_CHIPS),
        [(h, pl.BlockSpec((tm, d), lambda i, s: (i, 0)))],
        [(w_g, pl.BlockSpec((None, None, d, cs), lambda i, s: (layer, s, 0, 0)))],
        [[(0, 0)]], NN,
        [jax.ShapeDtypeStruct((t_len, N_CHIPS * cs), F32)],
        [pl.BlockSpec((tm, cs), lambda i, s: (i, s))],
        epilogue, comm=comm)


def _mm_dact_swiglu(dxb, w_g, layer, gu, scale, comm=None):
    t_len, d = dxb.shape
    rs = w_g.shape[2]
    tm = _pick(t_len, (512,))
    tn = _pick(rs, (1408, 256, 128))
    nj = rs // tn

    def epilogue(accs, e, o, cols):
        dact = scale * accs[0]
        g = e[0][0, :, cols].astype(F32)
        u = e[0][1, :, cols].astype(F32)
        sig = _sigmoid(g)
        o[0][0, :, cols] = (dact * u * (sig * (1.0 + g * (1.0 - sig)))).astype(BF16)
        o[0][1, :, cols] = (dact * (g * sig)).astype(BF16)

    gu_spec = pl.BlockSpec((2, tm, tn), lambda i, s, j: (0, i, s * nj + j))
    return _matmul(
        "ffn_dact", (t_len // tm, N_CHIPS, nj),
        [(dxb, pl.BlockSpec((tm, d), lambda i, s, j: (i, 0)))],
        [(w_g, pl.BlockSpec((None, None, tn, d), lambda i, s, j: (layer, s, j, 0)))],
        [[(0, 0)]], NT,
        [jax.ShapeDtypeStruct(gu.shape, BF16)], [gu_spec],
        epilogue, extra_ops=[(gu, gu_spec)], comm=comm, chunk=2 * LANES)[0]


def _mm_dh_ffn(dgu, w_g, layer, comm=None):
    t_len = dgu.shape[1]
    d, fs = w_g.shape[2], w_g.shape[3]
    tm = _pick(t_len, (1024, 512))
    tk = _pick(fs, (256, 128))
    nks = fs // tk
    nk = 2 * nks

    def epilogue(accs, e, o):
        o[0][...] = accs[0]

    return _matmul(
        "ffn_dh", (t_len // tm, nk),
        [(dgu, pl.BlockSpec((None, tm, tk), lambda i, k: (0, i, k))),
         (dgu, pl.BlockSpec((None, tm, tk), lambda i, k: (1, i, k)))],
        [(w_g, pl.BlockSpec((None, None, d, tk), lambda i, k: (layer, k // nks, 0, k % nks))),
         (w_g, pl.BlockSpec((None, None, d, tk), lambda i, k: (layer, k // nks + 2, 0, k % nks)))],
        [[(0, 0), (1, 1)]], NT,
        [jax.ShapeDtypeStruct((t_len, d), F32)],
        [pl.BlockSpec((tm, d), lambda i, k: (i, 0))],
        epilogue, nk=nk, acc_shapes=[(tm, d)], comm=comm)[0]


def _mm_dw(name, a, a_spec_of, b, b_spec_of, layer, stack, rows, cols, tn, scale, comm=None):
    t_len = a.shape[0]
    tt = _pick(t_len, (1024, 512))
    nj = cols // tn

    def epilogue(accs, e, o):
        o[0][...] = (scale * accs[0]).astype(BF16)

    shape = jax.ShapeDtypeStruct((2, N_CHIPS, rows, cols), BF16)
    return _matmul(
        name, (N_CHIPS, nj, t_len // tt),
        [(a, a_spec_of(tt))], [(b, b_spec_of(tt, tn, nj))],
        [[(0, 0)]], TN, [shape],
        [pl.BlockSpec((None, None, rows, tn), lambda s, j, t: (layer, s, 0, j))],
        epilogue, nk=t_len // tt, acc_shapes=[(rows, tn)],
        alias=None if stack is None else (stack, 0), comm=comm)[0]


def _dw_ffn_in(h, dgu, layer, stack, comm=None):
    d = h.shape[1]
    fs = dgu.shape[2] // 2
    tn = _pick(fs, (1408, 256))
    return _mm_dw(
        "ffn_dw_in", h, lambda tt: pl.BlockSpec((tt, d), lambda s, j, t: (t, 0)),
        dgu, lambda tt, tn_, nj: pl.BlockSpec((None, tt, tn_), lambda s, j, t: (s // 2, t, (s % 2) * nj + j)),
        layer, stack, d, fs, tn, 1.0, comm)


def _dw_rows(name, a, dxb, layer, stack, scale, comm=None):
    rs = a.shape[1] // N_CHIPS
    d = dxb.shape[1]
    tn = _pick(d, (1024,))
    return _mm_dw(
        name, a, lambda tt: pl.BlockSpec((tt, rs), lambda s, j, t: (t, s)),
        dxb, lambda tt, tn_, nj: pl.BlockSpec((tt, tn_), lambda s, j, t: (t, j)),
        layer, stack, rs, d, tn, scale, comm)


def _dw_mix_in(h, dp, layer, stack, comm=None):
    d = h.shape[1]
    cs = dp.shape[1] // N_CHIPS
    return _mm_dw(
        "mix_dw_in", h, lambda tt: pl.BlockSpec((tt, d), lambda s, j, t: (t, 0)),
        dp, lambda tt, tn_, nj: pl.BlockSpec((tt, tn_), lambda s, j, t: (t, s)),
        layer, stack, d, cs, cs, 1.0, comm)


def _mm_dmix(dxb, w_g, layer, comm=None):
    t_len, d = dxb.shape
    rs = w_g.shape[2]
    tm = _pick(t_len, (1024, 512))

    def epilogue(accs, e, o):
        o[0][...] = accs[0]

    return _matmul(
        "mix_dout", (t_len // tm, N_CHIPS),
        [(dxb, pl.BlockSpec((tm, d), lambda i, s: (i, 0)))],
        [(w_g, pl.BlockSpec((None, None, rs, d), lambda i, s: (layer, s, 0, 0)))],
        [[(0, 0)]], NT,
        [jax.ShapeDtypeStruct((t_len, N_CHIPS * rs), F32)],
        [pl.BlockSpec((tm, rs), lambda i, s: (i, s))],
        epilogue, comm=comm)[0]


def _mm_dh_mix(dp, w_g, layer, comm=None):
    t_len = dp.shape[0]
    d, cs = w_g.shape[2], w_g.shape[3]
    tm = _pick(t_len, (1024, 512))

    def epilogue(accs, e, o):
        o[0][...] = accs[0]

    return _matmul(
        "mix_dh", (t_len // tm, N_CHIPS),
        [(dp, pl.BlockSpec((tm, cs), lambda i, k: (i, k)))],
        [(w_g, pl.BlockSpec((None, None, d, cs), lambda i, k: (layer, k, 0, 0)))],
        [[(0, 0)]], NT,
        [jax.ShapeDtypeStruct((t_len, d), F32)],
        [pl.BlockSpec((tm, d), lambda i, k: (i, 0))],
        epilogue, nk=N_CHIPS, acc_shapes=[(tm, d)], comm=comm)[0]


def _rms_stats(x):
    r = lax.rsqrt(jnp.mean(x * x, axis=-1, keepdims=True) + NORM_EPS)
    return r, x * r


def _accumulate(ref, part, first):
    @pl.when(first)
    def _():
        ref[...] = part

    @pl.when(jnp.logical_not(first))
    def _():
        ref[...] += part


def _rmsnorm_fwd(x, g):
    t_len, d = x.shape
    tm = _pick(t_len, (512,))

    def body(x_ref, g_ref, h_ref):
        _, xhat = _rms_stats(x_ref[...])
        h_ref[...] = (xhat * g_ref[...]).astype(BF16)

    row = pl.BlockSpec((tm, d), lambda i: (i, 0))
    vec = pl.BlockSpec((1, d), lambda i: (0, 0))
    return _call(body, name="rmsnorm_fwd", grid=(t_len // tm,), in_specs=[row, vec], out_specs=row,
                 out_shape=jax.ShapeDtypeStruct((t_len, d), BF16), compiler_params=_params(1))(x, g)


def _rmsnorm_bwd(dh, x, g, dres):
    t_len, d = x.shape
    tm = _pick(t_len, (256,))

    def body(dh_ref, x_ref, g_ref, dres_ref, dx_ref, dxb_ref, dg_ref):
        r, xhat = _rms_stats(x_ref[...])
        dh_v = dh_ref[...]
        gd = dh_v * g_ref[...]
        dx = dres_ref[...] + r * (gd - xhat * jnp.mean(gd * xhat, axis=-1, keepdims=True))
        dx_ref[...] = dx
        dxb_ref[...] = dx.astype(BF16)
        _accumulate(dg_ref, jnp.sum(dh_v * xhat, axis=0, keepdims=True), pl.program_id(0) == 0)

    row = pl.BlockSpec((tm, d), lambda i: (i, 0))
    vec = pl.BlockSpec((1, d), lambda i: (0, 0))
    return _call(body, name="rmsnorm_bwd", grid=(t_len // tm,), in_specs=[row, row, vec, row],
                 out_specs=[row, row, vec],
                 out_shape=[jax.ShapeDtypeStruct((t_len, d), F32), jax.ShapeDtypeStruct((t_len, d), BF16),
                            jax.ShapeDtypeStruct((1, d), F32)],
                 compiler_params=_params(1))(dh, x, g, dres)


def _loss_head(x, g, target):
    t_len, d = x.shape
    tm = _pick(t_len, (256,))

    def body(x_ref, g_ref, t_ref, dx_ref, dxb_ref, dg_ref, loss_ref):
        first = pl.program_id(0) == 0
        r, xhat = _rms_stats(x_ref[...])
        g_v = g_ref[...]
        err = xhat * g_v - t_ref[...]
        per_token = jnp.mean(err * err, axis=-1, keepdims=True)
        part = 0.5 * jnp.sum(per_token, axis=0, keepdims=True)
        _accumulate(loss_ref, jnp.broadcast_to(part, (1, LANES)), first)
        dy = err * (1.0 / d)
        _accumulate(dg_ref, jnp.sum(dy * xhat, axis=0, keepdims=True), first)
        gd = dy * g_v
        dx = r * (gd - xhat * jnp.mean(gd * xhat, axis=-1, keepdims=True))
        dx_ref[...] = dx
        dxb_ref[...] = dx.astype(BF16)

    row = pl.BlockSpec((tm, d), lambda i: (i, 0))
    vec = pl.BlockSpec((1, d), lambda i: (0, 0))
    return _call(body, name="loss_head", grid=(t_len // tm,), in_specs=[row, vec, row],
                 out_specs=[row, row, vec, pl.BlockSpec((1, LANES), lambda i: (0, 0))],
                 out_shape=[jax.ShapeDtypeStruct((t_len, d), F32), jax.ShapeDtypeStruct((t_len, d), BF16),
                            jax.ShapeDtypeStruct((1, d), F32), jax.ShapeDtypeStruct((1, LANES), F32)],
                 compiler_params=_params(1))(x, g, target)


def _ln_stats(x):
    mu = jnp.mean(x, axis=-1, keepdims=True)
    xc = x - mu
    r = lax.rsqrt(jnp.mean(xc * xc, axis=-1, keepdims=True) + NORM_EPS)
    return r, xc * r


def _ln_bwd(dy, r, xhat, g):
    dxh = dy * g
    return r * (dxh - jnp.mean(dxh, axis=-1, keepdims=True) - xhat * jnp.mean(dxh * xhat, axis=-1, keepdims=True))


def _rope_tables(positions):
    t_len = positions.shape[-1]
    inv_freq = 1.0 / (ROPE_THETA ** (jnp.arange(0, 2 * ROT_HALF, 2, dtype=F32) / (2 * ROT_HALF)))
    ang = positions.astype(F32).reshape(t_len, 1) * inv_freq
    cos = jnp.tile(jnp.cos(ang), (1, LANES // ROT_HALF))
    sin = jnp.tile(jnp.sin(ang), (1, LANES // ROT_HALF))
    lane = jnp.arange(LANES) % HEAD_DIM
    c = jnp.where(lane < 2 * ROT_HALF, cos, 1.0)
    s1 = jnp.where(lane < ROT_HALF, -sin, 0.0)
    s2 = jnp.where((lane >= ROT_HALF) & (lane < 2 * ROT_HALF), sin, 0.0)
    return c.astype(F32), s1.astype(F32), s2.astype(F32)


def _rope_fwd(p, tables):
    t_len = p.shape[0]
    tm = _pick(t_len, (256,))
    n_rot = K_END // LANES

    def body(p_ref, c_ref, s1_ref, s2_ref, o_ref):
        c, s1, s2 = c_ref[...], s1_ref[...], s2_ref[...]
        for j in range(V_END // LANES):
            sl = slice(j * LANES, (j + 1) * LANES)
            t = p_ref[:, sl]
            if j < n_rot:
                t = t * c + pltpu.roll(t, LANES - ROT_HALF, 1) * s1 + pltpu.roll(t, ROT_HALF, 1) * s2
            if j < Q_END // LANES:
                t = t * ATTN_SCALE
            o_ref[:, sl] = t.astype(BF16)

    tab = pl.BlockSpec((tm, LANES), lambda i: (i, 0))
    blk = pl.BlockSpec((tm, V_END), lambda i: (i, 0))
    return _call(body, name="rope_fwd", grid=(t_len // tm,), in_specs=[blk, tab, tab, tab], out_specs=blk,
                 out_shape=jax.ShapeDtypeStruct((t_len, V_END), BF16), compiler_params=_params(1))(p, *tables)


def _assemble_dp(dq, dkc, dkp, dvc, dvp, tables, dalin, dagate, du, dvin):
    t_len = dq.shape[0]
    steps = t_len // ATTN_STEP

    def body(dq_ref, dkc_ref, dkp_ref, dvc_ref, dvp_ref, c_ref, s1_ref, s2_ref, dalin_ref, dagate_ref,
             du_ref, dvin_ref, o_ref):
        keep = (pl.program_id(0) < steps - 1).astype(F32)
        for part in range(ATTN_STEP // BLK):
            rows = slice(part * BLK, (part + 1) * BLK)
            c, s1, s2 = c_ref[rows, :], s1_ref[rows, :], s2_ref[rows, :]

            def unrotate(dr):
                return dr * c + pltpu.roll(dr * s1, ROT_HALF, 1) + pltpu.roll(dr * s2, LANES - ROT_HALF, 1)

            for j in range(Q_END // LANES):
                sl = slice(j * LANES, (j + 1) * LANES)
                o_ref[rows, sl] = unrotate(dq_ref[rows, sl]).astype(BF16)
            for j in range((K_END - Q_END) // LANES):
                sl = slice(j * LANES, (j + 1) * LANES)
                dk, dv = dkc_ref[rows, sl], dvc_ref[rows, sl]
                if part == ATTN_STEP // BLK - 1:
                    dk = dk + keep * dkp_ref[:, sl]
                    dv = dv + keep * dvp_ref[:, sl]
                o_ref[rows, Q_END + j * LANES:Q_END + (j + 1) * LANES] = unrotate(dk).astype(BF16)
                o_ref[rows, K_END + j * LANES:K_END + (j + 1) * LANES] = dv.astype(BF16)
        o_ref[:, V_END:V_END + CONV_CH] = dalin_ref[...]
        o_ref[:, V_END + CONV_CH:CONV_END] = dagate_ref[...]
        o_ref[:, CONV_END:CONV_END + SGU_CH] = du_ref[...]
        o_ref[:, CONV_END + SGU_CH:IN_COLS] = dvin_ref[...]

    def cur(w):
        return pl.BlockSpec((ATTN_STEP, w), lambda i: (i, 0))

    def nxt(w):
        return pl.BlockSpec((BLK, w), lambda i: (jnp.minimum(i + 1, steps - 1), 0))

    kvw = K_END - Q_END
    return _call(body, name="assemble_dp", grid=(steps,),
                 in_specs=[cur(Q_END), cur(kvw), nxt(kvw), cur(kvw), nxt(kvw), cur(LANES), cur(LANES), cur(LANES),
                           cur(CONV_CH), cur(CONV_CH), cur(SGU_CH), cur(SGU_CH)],
                 out_specs=cur(IN_COLS), out_shape=jax.ShapeDtypeStruct((t_len, IN_COLS), BF16),
                 compiler_params=_params(1))(dq, dkc, dkp, dvc, dvp, *tables, dalin, dagate, du, dvin)


ATTN_STEP = 2 * BLK
ATTN_SCALE = HEAD_DIM ** -0.5
MASKED = -1e30


def _attn_bias():
    qi = jnp.arange(GQ * BLK)[None, :] % BLK
    kj = jnp.arange(2 * BLK)[:, None]
    dist = qi + BLK - kj
    band = (dist >= 0) & (dist < BLK)
    return jnp.stack([jnp.where(band & (kj >= BLK), 0.0, MASKED), jnp.where(band, 0.0, MASKED)]).astype(F32)


def _side_by_side(ref, cols):
    return jnp.concatenate([ref[h, :, cols] for h in range(GQ)], axis=1)


def _attn_chains(qt_ref, kp_ref, kc_ref):
    kc = kc_ref[...]
    pos_a, pos_b = slice(0, BLK), slice(BLK, ATTN_STEP)
    return [(pos_a, _side_by_side(qt_ref, pos_a), jnp.concatenate([kp_ref[...], kc[:BLK]], axis=0)),
            (pos_b, _side_by_side(qt_ref, pos_b), kc)]


def _attn_weights(qt, kk, bias, snk):
    s = lax.dot_general(kk, qt, NN, preferred_element_type=F32) + bias
    m = jnp.maximum(jnp.max(s, axis=0, keepdims=True), snk)
    e = jnp.exp(s - m)
    es = jnp.exp(snk - m)
    return e, es, 1.0 / (jnp.sum(e, axis=0, keepdims=True) + es)


def _attn_specs():
    before = lambda n: jnp.maximum(2 * n - 1, 0)
    return dict(
        qt=pl.BlockSpec((GQ, HEAD_DIM, ATTN_STEP), lambda g, n: (g, 0, n)),
        q=pl.BlockSpec((GQ, ATTN_STEP, HEAD_DIM), lambda g, n: (g, n, 0)),
        cur=pl.BlockSpec((None, ATTN_STEP, HEAD_DIM), lambda g, n: (g, n, 0)),
        prev=pl.BlockSpec((None, BLK, HEAD_DIM), lambda g, n: (g, before(n), 0)),
        cur_t=pl.BlockSpec((None, HEAD_DIM, ATTN_STEP), lambda g, n: (g, 0, n)),
        prev_t=pl.BlockSpec((None, HEAD_DIM, BLK), lambda g, n: (g, 0, before(n))),
        snk=pl.BlockSpec((None, 1, GQ * BLK), lambda g, n: (g, 0, 0)),
        bias_a=pl.BlockSpec((None, 2 * BLK, GQ * BLK), lambda g, n: (jnp.minimum(n, 1), 0, 0)),
        bias_b=pl.BlockSpec((None, 2 * BLK, GQ * BLK), lambda g, n: (1, 0, 0)))


def _attn_fwd(qt, k, vt, snk, comm=None):
    steps = qt.shape[2] // ATTN_STEP

    def body(qt_ref, kp_ref, kc_ref, vtp_ref, vtc_ref, snk_ref, ba_ref, bb_ref, o_ref):
        snk = snk_ref[...]
        vtc = vtc_ref[...]
        values = (jnp.concatenate([vtp_ref[...], vtc[:, :BLK]], axis=1), vtc)
        for (pos, qt_v, kk), vt_v, b_ref in zip(_attn_chains(qt_ref, kp_ref, kc_ref), values, (ba_ref, bb_ref)):
            e, _, inv = _attn_weights(qt_v, kk, b_ref[...], snk)
            o = lax.dot_general(vt_v, e.astype(BF16), NN, preferred_element_type=F32) * inv
            for h in range(GQ):
                o_ref[h, :, pos] = o[:, h * BLK:(h + 1) * BLK].astype(BF16)

    sp = _attn_specs()
    bias = _attn_bias()
    return _hosted_call(body, comm, "attn_fwd", (N_KV_HEADS, steps), [qt, k, k, vt, vt, snk, bias, bias],
                        [sp["qt"], sp["prev"], sp["cur"], sp["prev_t"], sp["cur_t"], sp["snk"], sp["bias_a"],
                         sp["bias_b"]], [jax.ShapeDtypeStruct(qt.shape, BF16)], [sp["qt"]])[0]


def _attn_bwd(qt, q, k, kt, v, snk, dot_, do, comm=None):
    steps = qt.shape[2] // ATTN_STEP

    def body(qt_ref, q_ref, kp_ref, kc_ref, ktp_ref, ktc_ref, vp_ref, vc_ref, snk_ref, ba_ref, bb_ref, dot_ref,
             do_ref, dq_ref, dkp_ref, dkc_ref, dvp_ref, dvc_ref, dsnk_ref):
        snk = snk_ref[...]
        vc, ktc = vc_ref[...], ktc_ref[...]
        values = (jnp.concatenate([vp_ref[...], vc[:BLK]], axis=0), vc)
        keys_t = (jnp.concatenate([ktp_ref[...], ktc[:, :BLK]], axis=1), ktc)
        row = lax.broadcasted_iota(jnp.int32, (SUBLANES, LANES), 0)
        tile = jnp.zeros((SUBLANES, LANES), F32)
        grads = []
        for (pos, qt_v, kk), vv, kt_v, b_ref in zip(_attn_chains(qt_ref, kp_ref, kc_ref), values, keys_t,
                                                    (ba_ref, bb_ref)):
            e, es, inv = _attn_weights(qt_v, kk, b_ref[...], snk)
            p = e * inv
            dp = lax.dot_general(vv, _side_by_side(dot_ref, pos), NN, preferred_element_type=F32)
            delta = jnp.sum(p * dp, axis=0, keepdims=True)
            ds = (p * (dp - delta)).astype(BF16)
            dq = lax.dot_general(kt_v, ds, NN, preferred_element_type=F32) * ATTN_SCALE
            for h in range(GQ):
                dq_ref[h, :, pos] = dq[:, h * BLK:(h + 1) * BLK]
            q_v = q_ref[:, pos, :].reshape(GQ * BLK, HEAD_DIM)
            do_v = do_ref[:, pos, :].reshape(GQ * BLK, HEAD_DIM)
            grads.append((lax.dot_general(ds, q_v, NN, preferred_element_type=F32),
                          lax.dot_general(p.astype(BF16), do_v, NN, preferred_element_type=F32)))
            per_query = -(es * inv) * delta
            for hh in range(GQ):
                tot = jnp.sum(per_query[:, hh * BLK:(hh + 1) * BLK], axis=1, keepdims=True)
                tile = tile + jnp.where(row == hh, tot, 0.0)
        (dk_a, dv_a), (dk_b, dv_b) = grads
        dkp_ref[...] = dk_a[:BLK]
        dvp_ref[...] = dv_a[:BLK]
        dkc_ref[0:BLK, :] = dk_a[BLK:] + dk_b[:BLK]
        dvc_ref[0:BLK, :] = dv_a[BLK:] + dv_b[:BLK]
        dkc_ref[BLK:ATTN_STEP, :] = dk_b[BLK:]
        dvc_ref[BLK:ATTN_STEP, :] = dv_b[BLK:]
        _accumulate(dsnk_ref, tile, pl.program_id(1) == 0)

    sp = _attn_specs()
    bias = _attn_bias()
    step_blk = pl.BlockSpec((None, BLK, HEAD_DIM), lambda g, n: (g, n, 0))
    kv_shape = jax.ShapeDtypeStruct(k.shape, F32)
    prev_shape = jax.ShapeDtypeStruct((N_KV_HEADS, steps * BLK, HEAD_DIM), F32)
    return _hosted_call(
        body, comm, "attn_bwd", (N_KV_HEADS, steps), [qt, q, k, k, kt, kt, v, v, snk, bias, bias, dot_, do],
        [sp["qt"], sp["q"], sp["prev"], sp["cur"], sp["prev_t"], sp["cur_t"], sp["prev"], sp["cur"], sp["snk"],
         sp["bias_a"], sp["bias_b"], sp["qt"], sp["q"]],
        [jax.ShapeDtypeStruct(qt.shape, F32), prev_shape, kv_shape, prev_shape, kv_shape,
         jax.ShapeDtypeStruct((N_KV_HEADS, SUBLANES, LANES), F32)],
        [sp["qt"], step_blk, sp["cur"], step_blk, sp["cur"],
         pl.BlockSpec((None, SUBLANES, LANES), lambda g, n: (g, 0, 0))])


CONV_CHUNK = 256


def _shift_up(win, s):
    n = win.shape[0]
    return win if s == 0 else pltpu.roll(win, n - s, 0)


def _conv_col_specs(t_len):
    lin = pl.BlockSpec((t_len, LANES), lambda j: (0, V_END // LANES + j))
    gate = pl.BlockSpec((t_len, LANES), lambda j: (0, (V_END + CONV_CH) // LANES + j))
    col = pl.BlockSpec((t_len, LANES), lambda j: (0, j))
    wsp = pl.BlockSpec((CONV_PAD, LANES), lambda j: (0, j))
    return lin, gate, col, wsp


def _conv_fwd(p, w, b):
    t_len = p.shape[0]
    ch = CONV_CHUNK

    def body(lin_ref, gate_ref, w_ref, b_ref, y_ref, hp_ref):
        hp_ref[0:CONV_PAD, :] = jnp.zeros((CONV_PAD, LANES), F32)

        def fill(c, carry):
            r0 = pl.multiple_of(c * ch, ch)
            hp_ref[pl.ds(r0 + CONV_PAD, ch), :] = lin_ref[pl.ds(r0, ch), :] * _sigmoid(gate_ref[pl.ds(r0, ch), :])
            return carry

        lax.fori_loop(0, t_len // ch, fill, 0)

        def conv(c, carry):
            r0 = pl.multiple_of(c * ch, ch)
            win = hp_ref[pl.ds(r0, ch + CONV_PAD), :]
            acc = jnp.zeros((ch, LANES), F32)
            for k in range(CONV_WIDTH):
                acc = acc + _shift_up(win, CONV_PAD - (CONV_WIDTH - 1) + k)[:ch] * w_ref[k:k + 1, :]
            y_ref[pl.ds(r0, ch), :] = acc + b_ref[...]
            return carry

        lax.fori_loop(0, t_len // ch, conv, 0)

    lin, gate, col, wsp = _conv_col_specs(t_len)
    return _call(body, name="conv_fwd", grid=(CONV_CH // LANES,),
                 in_specs=[lin, gate, wsp, pl.BlockSpec((1, LANES), lambda j: (0, j))], out_specs=col,
                 out_shape=jax.ShapeDtypeStruct((t_len, CONV_CH), F32),
                 scratch_shapes=[pltpu.VMEM((t_len + CONV_PAD, LANES), F32)],
                 compiler_params=_params(1))(p, p, w, b)


def _conv_post_fwd(y, g, b):
    t_len = y.shape[0]
    tm = _pick(t_len, (512,))

    def body(y_ref, g_ref, b_ref, o_ref):
        _, xhat = _ln_stats(y_ref[...])
        z = xhat * g_ref[...] + b_ref[...]
        o_ref[...] = (z * _sigmoid(z)).astype(BF16)

    row = pl.BlockSpec((tm, CONV_CH), lambda i: (i, 0))
    vec = pl.BlockSpec((1, CONV_CH), lambda i: (0, 0))
    return _call(body, name="conv_post_fwd", grid=(t_len // tm,), in_specs=[row, vec, vec], out_specs=row,
                 out_shape=jax.ShapeDtypeStruct((t_len, CONV_CH), BF16), compiler_params=_params(1))(y, g, b)


def _conv_post_bwd(dmix, y, g, b):
    t_len = y.shape[0]
    tm = _pick(t_len, (512,))

    def body(do_ref, y_ref, g_ref, b_ref, dy_ref, dg_ref, db_ref, dcb_ref):
        first = pl.program_id(0) == 0
        r, xhat = _ln_stats(y_ref[...])
        g_v = g_ref[...]
        z = xhat * g_v + b_ref[...]
        sig = _sigmoid(z)
        dz = do_ref[...] * (sig * (1.0 + z * (1.0 - sig)))
        _accumulate(db_ref, jnp.sum(dz, axis=0, keepdims=True), first)
        _accumulate(dg_ref, jnp.sum(dz * xhat, axis=0, keepdims=True), first)
        dy = _ln_bwd(dz, r, xhat, g_v)
        dy_ref[...] = dy
        _accumulate(dcb_ref, jnp.sum(dy, axis=0, keepdims=True), first)

    row = pl.BlockSpec((tm, CONV_CH), lambda i: (i, 0))
    do_spec = pl.BlockSpec((tm, CONV_CH), lambda i: (i, Q_END // CONV_CH))
    vec = pl.BlockSpec((1, CONV_CH), lambda i: (0, 0))
    vshape = jax.ShapeDtypeStruct((1, CONV_CH), F32)
    return _call(body, name="conv_post_bwd", grid=(t_len // tm,), in_specs=[do_spec, row, vec, vec],
                 out_specs=[row, vec, vec, vec],
                 out_shape=[jax.ShapeDtypeStruct((t_len, CONV_CH), F32), vshape, vshape, vshape],
                 compiler_params=_params(1))(dmix, y, g, b)


def _conv_bwd(p, dy, w):
    t_len = p.shape[0]
    ch = CONV_CHUNK

    def body(lin_ref, gate_ref, dy_ref, w_ref, dlin_ref, dgate_ref, dw_ref, hp_ref, dyp_ref):
        hp_ref[0:CONV_PAD, :] = jnp.zeros((CONV_PAD, LANES), F32)
        dyp_ref[t_len:t_len + CONV_PAD, :] = jnp.zeros((CONV_PAD, LANES), F32)
        dw_ref[...] = jnp.zeros((CONV_PAD, LANES), F32)

        def fill(c, carry):
            r0 = pl.multiple_of(c * ch, ch)
            hp_ref[pl.ds(r0 + CONV_PAD, ch), :] = lin_ref[pl.ds(r0, ch), :] * _sigmoid(gate_ref[pl.ds(r0, ch), :])
            dyp_ref[pl.ds(r0, ch), :] = dy_ref[pl.ds(r0, ch), :]
            return carry

        lax.fori_loop(0, t_len // ch, fill, 0)

        def step(c, carry):
            r0 = pl.multiple_of(c * ch, ch)
            win_h = hp_ref[pl.ds(r0, ch + CONV_PAD), :]
            win_dy = dyp_ref[pl.ds(r0, ch + CONV_PAD), :]
            dyc = win_dy[:ch]
            dh = jnp.zeros((ch, LANES), F32)
            for k in range(CONV_WIDTH):
                tap = _shift_up(win_h, CONV_PAD - (CONV_WIDTH - 1) + k)[:ch]
                dw_ref[k:k + 1, :] += jnp.sum(dyc * tap, axis=0, keepdims=True)
                dh = dh + _shift_up(win_dy, CONV_WIDTH - 1 - k)[:ch] * w_ref[k:k + 1, :]
            lin = lin_ref[pl.ds(r0, ch), :]
            sig = _sigmoid(gate_ref[pl.ds(r0, ch), :])
            dlin_ref[pl.ds(r0, ch), :] = (dh * sig).astype(BF16)
            dgate_ref[pl.ds(r0, ch), :] = (dh * lin * (sig * (1.0 - sig))).astype(BF16)
            return carry

        lax.fori_loop(0, t_len // ch, step, 0)

    lin, gate, col, wsp = _conv_col_specs(t_len)
    half = jax.ShapeDtypeStruct((t_len, CONV_CH), BF16)
    return _call(body, name="conv_bwd", grid=(CONV_CH // LANES,), in_specs=[lin, gate, col, wsp],
                 out_specs=[col, col, wsp],
                 out_shape=[half, half, jax.ShapeDtypeStruct((CONV_PAD, CONV_CH), F32)],
                 scratch_shapes=[pltpu.VMEM((t_len + CONV_PAD, LANES), F32), pltpu.VMEM((t_len + CONV_PAD, LANES), F32)],
                 compiler_params=_params(1))(p, p, dy, w)


def _sgu_mixed(v, w_ref, bt_ref, j):
    lane = lax.broadcasted_iota(jnp.int32, (BLK, LANES), 1)
    lo = lane < HEAD_DIM
    tri = lax.broadcasted_iota(jnp.int32, (BLK, BLK), 0) >= lax.broadcasted_iota(jnp.int32, (BLK, BLK), 1)
    vs = v[:, j * LANES:(j + 1) * LANES]
    v_lo = jnp.where(lo, vs, 0.0).astype(BF16)
    v_hi = jnp.where(lo, 0.0, vs).astype(BF16)
    w_lo = jnp.where(tri, w_ref[2 * j], 0.0).astype(BF16)
    w_hi = jnp.where(tri, w_ref[2 * j + 1], 0.0).astype(BF16)
    m = (lax.dot_general(w_lo, v_lo, NN, preferred_element_type=F32)
         + lax.dot_general(w_hi, v_hi, NN, preferred_element_type=F32))
    bias = jnp.where(lo, bt_ref[:, 2 * j:2 * j + 1], bt_ref[:, 2 * j + 1:2 * j + 2])
    return m + bias, (v_lo, v_hi, w_lo, w_hi, lo, tri)


def _sgu_specs():
    u_spec = pl.BlockSpec((BLK, SGU_CH), lambda i: (i, CONV_END // SGU_CH))
    v_spec = pl.BlockSpec((BLK, SGU_CH), lambda i: (i, CONV_END // SGU_CH + 1))
    vec = pl.BlockSpec((1, SGU_CH), lambda i: (0, 0))
    w_spec = pl.BlockSpec((SGU_HEADS, BLK, BLK), lambda i: (0, 0, 0))
    bt_spec = pl.BlockSpec((BLK, SGU_HEADS), lambda i: (0, 0))
    row = pl.BlockSpec((BLK, SGU_CH), lambda i: (i, 0))
    return u_spec, v_spec, vec, w_spec, bt_spec, row


def _sgu_fwd(p, g, b, w, bt):
    t_len = p.shape[0]

    def body(u_ref, vin_ref, g_ref, b_ref, w_ref, bt_ref, o_ref):
        _, xhat = _ln_stats(vin_ref[...])
        v = xhat * g_ref[...] + b_ref[...]
        for j in range(SGU_CH // LANES):
            m, _ = _sgu_mixed(v, w_ref, bt_ref, j)
            sl = slice(j * LANES, (j + 1) * LANES)
            o_ref[:, sl] = (u_ref[:, sl] * m).astype(BF16)

    u_spec, v_spec, vec, w_spec, bt_spec, row = _sgu_specs()
    return _call(body, name="sgu_fwd", grid=(t_len // BLK,), in_specs=[u_spec, v_spec, vec, vec, w_spec, bt_spec],
                 out_specs=row, out_shape=jax.ShapeDtypeStruct((t_len, SGU_CH), BF16),
                 compiler_params=_params(1))(p, p, g, b, w, bt)


def _sgu_bwd(p, dmix, g, b, w, bt):
    t_len = p.shape[0]

    def body(u_ref, vin_ref, do_ref, g_ref, b_ref, w_ref, bt_ref, du_ref, dvin_ref, dw_ref, dbt_ref, dg_ref,
             db_ref, dv_ref):
        first = pl.program_id(0) == 0
        r, xhat = _ln_stats(vin_ref[...])
        g_v = g_ref[...]
        v = xhat * g_v + b_ref[...]
        lane = lax.broadcasted_iota(jnp.int32, (BLK, LANES), 1)
        dbt = jnp.zeros((BLK, LANES), F32)

        @pl.when(first)
        def _():
            dw_ref[...] = jnp.zeros((SGU_HEADS, BLK, BLK), F32)

        for j in range(SGU_CH // LANES):
            m, (v_lo, v_hi, w_lo, w_hi, lo, tri) = _sgu_mixed(v, w_ref, bt_ref, j)
            sl = slice(j * LANES, (j + 1) * LANES)
            do_v = do_ref[:, sl]
            du_ref[:, sl] = (do_v * m).astype(BF16)
            dm = do_v * u_ref[:, sl]
            dm_lo = jnp.where(lo, dm, 0.0)
            dm_hi = jnp.where(lo, 0.0, dm)
            dbt = dbt + jnp.where(lane == 2 * j, jnp.sum(dm_lo, axis=-1, keepdims=True), 0.0)
            dbt = dbt + jnp.where(lane == 2 * j + 1, jnp.sum(dm_hi, axis=-1, keepdims=True), 0.0)
            dm_lo, dm_hi = dm_lo.astype(BF16), dm_hi.astype(BF16)
            dw_ref[2 * j] += jnp.where(tri, lax.dot_general(dm_lo, v_lo, NT, preferred_element_type=F32), 0.0)
            dw_ref[2 * j + 1] += jnp.where(tri, lax.dot_general(dm_hi, v_hi, NT, preferred_element_type=F32), 0.0)
            dv_ref[:, sl] = (lax.dot_general(w_lo, dm_lo, TN, preferred_element_type=F32)
                             + lax.dot_general(w_hi, dm_hi, TN, preferred_element_type=F32))
        _accumulate(dbt_ref, dbt, first)
        dv = dv_ref[...]
        _accumulate(db_ref, jnp.sum(dv, axis=0, keepdims=True), first)
        _accumulate(dg_ref, jnp.sum(dv * xhat, axis=0, keepdims=True), first)
        dvin_ref[...] = _ln_bwd(dv, r, xhat, g_v).astype(BF16)

    u_spec, v_spec, vec, w_spec, bt_spec, row = _sgu_specs()
    do_spec = pl.BlockSpec((BLK, SGU_CH), lambda i: (i, (Q_END + CONV_CH) // SGU_CH))
    half = jax.ShapeDtypeStruct((t_len, SGU_CH), BF16)
    vshape = jax.ShapeDtypeStruct((1, SGU_CH), F32)
    return _call(body, name="sgu_bwd", grid=(t_len // BLK,),
                 in_specs=[u_spec, v_spec, do_spec, vec, vec, w_spec, bt_spec],
                 out_specs=[row, row, w_spec, pl.BlockSpec((BLK, LANES), lambda i: (0, 0)), vec, vec],
                 out_shape=[half, half, jax.ShapeDtypeStruct((SGU_HEADS, BLK, BLK), F32),
                            jax.ShapeDtypeStruct((BLK, LANES), F32), vshape, vshape],
                 scratch_shapes=[pltpu.VMEM((BLK, SGU_CH), F32)],
                 compiler_params=_params(1))(p, p, dmix, g, b, w, bt)


def _place():
    x, y, c = lax.axis_index("x"), lax.axis_index("y"), lax.axis_index("c")
    chips = [(1 - x, y), (x, 1 - y), (1 - x, 1 - y)]
    return x, y, c, chips


def _hbm_specs(n):
    return [pl.BlockSpec(memory_space=pltpu.HBM)] * n


def _comm_params():
    return pltpu.CompilerParams(has_side_effects=True)


def _remote(src, dst, send_sem, recv_sem, to):
    return pltpu.make_async_remote_copy(src_ref=src, dst_ref=dst, send_sem=send_sem, recv_sem=recv_sem,
                                        device_id=to, device_id_type=MESH_ID)


def _cast_place(w_local, chip):
    n, rows, cols = w_local.shape

    def body(chip_ref, w_ref, o_ref):
        o_ref[...] = w_ref[...].astype(BF16)

    grid_spec = pltpu.PrefetchScalarGridSpec(
        num_scalar_prefetch=1, grid=(n, rows // ROW_TILE),
        in_specs=[pl.BlockSpec((None, ROW_TILE, cols), lambda l, i, ch: (l, i, 0))],
        out_specs=pl.BlockSpec((None, None, ROW_TILE, cols), lambda l, i, ch: (l, ch[0], i, 0)))
    return _call(body, name="cast_place", grid_spec=grid_spec,
                 out_shape=jax.ShapeDtypeStruct((n, N_CHIPS, rows, cols), BF16), compiler_params=_params(2))(chip, w_local)


def _all_gather_weights(placed, shards):
    n_placed, nt = len(placed), len(placed) + len(shards)

    def body(*refs):
        ins, outs = refs[:nt], refs[nt:2 * nt]
        ici_send, ici_recv, d2d_send, d2d_recv, local_sem = refs[2 * nt:]
        x, y, c, chips = _place()
        me = 2 * x + y
        sibling = (x, y, 1 - c)
        local = [pltpu.make_async_copy(ins[t].at[l], outs[t].at[l, me], local_sem.at[2 * (t - n_placed) + l])
                 for t in range(n_placed, nt) for l in range(2)]
        for cp in local:
            cp.start()
        sends = []
        for t in range(nt):
            src = outs[t].at[c, me] if t < n_placed else ins[t].at[c]
            for j, (px, py) in enumerate(chips):
                sends.append(_remote(src, outs[t].at[c, me], ici_send.at[3 * t + j], ici_recv.at[3 * t + j],
                                     (px, py, c)))
        for cp in sends:
            cp.start()
        for t in range(nt):
            for j, (px, py) in enumerate(chips):
                slab = outs[t].at[c, 2 * px + py]
                _remote(slab, slab, ici_send.at[3 * t + j], ici_recv.at[3 * t + j], (px, py, c)).wait_recv()
                fwd = _remote(slab, slab, d2d_send.at[3 * t + j], d2d_recv.at[3 * t + j], sibling)
                fwd.start()
                sends.append(fwd)
        for t in range(nt):
            for j, (px, py) in enumerate(chips):
                slab = outs[t].at[1 - c, 2 * px + py]
                _remote(slab, slab, d2d_send.at[3 * t + j], d2d_recv.at[3 * t + j], sibling).wait_recv()
        for cp in sends:
            cp.wait_send()
        for cp in local:
            cp.wait()

    out_shape = [jax.ShapeDtypeStruct(p.shape, p.dtype) for p in placed]
    out_shape += [jax.ShapeDtypeStruct((2, N_CHIPS) + s.shape[1:], s.dtype) for s in shards]
    sems = [pltpu.SemaphoreType.DMA((3 * nt,))] * 4 + [pltpu.SemaphoreType.DMA((2 * len(shards),))]
    return _call(body, name="all_gather_weights", in_specs=_hbm_specs(nt), out_specs=_hbm_specs(nt),
                 out_shape=out_shape, scratch_shapes=sems, input_output_aliases={t: t for t in range(n_placed)},
                 compiler_params=_comm_params())(*placed, *shards)


def _gather_comm(bufs, pieces):
    n = len(pieces)
    sems = [pltpu.SemaphoreType.DMA((3 * n,))] * 4

    def half(ref, layer, chip, r0, nr, which):
        return ref.at[layer, chip, pl.ds(pl.multiple_of(r0 + which * (nr // 2), SUBLANES), nr // 2)]

    def start(rd, al, fr, sm):
        ici_send, ici_recv, _, _ = sm
        x, y, c, chips = _place()
        for i, (t, layer, r0, nr) in enumerate(pieces):
            own = half(al[t], layer, 2 * x + y, r0, nr, c)
            for j, (px, py) in enumerate(chips):
                _remote(own, own, ici_send.at[3 * i + j], ici_recv.at[3 * i + j], (px, py, c)).start()

    def finish(rd, al, fr, sm):
        ici_send, ici_recv, d2d_send, d2d_recv = sm
        x, y, c, chips = _place()
        sibling = (x, y, 1 - c)
        passed = []
        for i, (t, layer, r0, nr) in enumerate(pieces):
            for j, (px, py) in enumerate(chips):
                got = half(al[t], layer, 2 * px + py, r0, nr, c)
                _remote(got, got, ici_send.at[3 * i + j], ici_recv.at[3 * i + j], (px, py, c)).wait_recv()
                fwd = _remote(got, got, d2d_send.at[3 * i + j], d2d_recv.at[3 * i + j], sibling)
                fwd.start()
                passed.append(fwd)
        for i, (t, layer, r0, nr) in enumerate(pieces):
            own = half(al[t], layer, 2 * x + y, r0, nr, c)
            for j, (px, py) in enumerate(chips):
                _remote(own, own, ici_send.at[3 * i + j], ici_recv.at[3 * i + j], (px, py, c)).wait_send()
                theirs = half(al[t], layer, 2 * px + py, r0, nr, 1 - c)
                _remote(theirs, theirs, d2d_send.at[3 * i + j], d2d_recv.at[3 * i + j], sibling).wait_recv()
        for fwd in passed:
            fwd.wait_send()

    return _Comm([], bufs, [], sems, start, finish)


def _own_rows(ref, c, which=0):
    hr = ref.shape[-2] // 2
    start = pl.multiple_of((c if which == 0 else 1 - c) * hr, SUBLANES)
    return ref.at[(slice(None),) * (len(ref.shape) - 2) + (pl.ds(start, hr),)]


def _to_owner_comm(stacks, layer):
    nt = len(stacks)
    sems = [pltpu.SemaphoreType.DMA((nt,))] * 2
    fresh = [jax.ShapeDtypeStruct((N_CHIPS, s.shape[2] // 2, s.shape[3]), s.dtype) for s in stacks]

    def copies(rd, fr, sm):
        x, y, c, _ = _place()
        return [_remote(_own_rows(rd[t].at[layer], c, 1), fr[t], sm[0].at[t], sm[1].at[t], (x, y, 1 - c))
                for t in range(nt)]

    def start(rd, al, fr, sm):
        for cp in copies(rd, fr, sm):
            cp.start()

    def finish(rd, al, fr, sm):
        for cp in copies(rd, fr, sm):
            cp.wait()

    return _Comm(stacks, [], fresh, sems, start, finish)


def _chip_comm(partials):
    nt = len(partials)
    sems = [pltpu.SemaphoreType.DMA((3 * nt,))] * 2
    fresh = [jax.ShapeDtypeStruct((3,) + p.shape[1:], p.dtype) for p in partials]

    def each(rd, fr, sm, act):
        x, y, c, chips = _place()
        for t in range(nt):
            for j, (px, py) in enumerate(chips):
                act(_remote(rd[t].at[2 * px + py], fr[t].at[j], sm[0].at[3 * t + j], sm[1].at[3 * t + j], (px, py, c)))

    def start(rd, al, fr, sm):
        each(rd, fr, sm, lambda cp: cp.start())

    def finish(rd, al, fr, sm):
        each(rd, fr, sm, lambda cp: cp.wait())

    return _Comm(partials, [], fresh, sems, start, finish)


def _from_owner_comm(finals, layer):
    nt = len(finals)
    sems = [pltpu.SemaphoreType.DMA((nt,))] * 2

    def start(rd, al, fr, sm):
        x, y, c, _ = _place()
        for t in range(nt):
            mine = _own_rows(al[t].at[layer], c)
            _remote(mine, mine, sm[0].at[t], sm[1].at[t], (x, y, 1 - c)).start()

    def finish(rd, al, fr, sm):
        x, y, c, _ = _place()
        for t in range(nt):
            mine, theirs = _own_rows(al[t].at[layer], c), _own_rows(al[t].at[layer], c, 1)
            _remote(mine, mine, sm[0].at[t], sm[1].at[t], (x, y, 1 - c)).wait_send()
            _remote(theirs, theirs, sm[0].at[t], sm[1].at[t], (x, y, 1 - c)).wait_recv()

    return _Comm([], finals, [], sems, start, finish)


def _all_reduce_small(buf):
    rows = buf.shape[0]

    def body(x_ref, out_ref, all_ref, send_sems, recv_sems, local_sem):
        x, y, c, chips = _place()
        me, sibling = (x, y, c), (x, y, 1 - c)

        def block(px, py, pc):
            return all_ref.at[pl.ds((4 * px + 2 * py + pc) * rows, rows), :]

        def copy(k, blk, to, src=None):
            return _remote(block(*blk) if src is None else src, block(*blk), send_sems.at[k], recv_sems.at[k], to)

        mine = pltpu.make_async_copy(x_ref, block(*me), local_sem)
        mine.start()
        first = [copy(0, me, sibling, src=x_ref)]
        first += [copy(1 + j, me, (*chip, c), src=x_ref) for j, chip in enumerate(chips)]
        for cp in first:
            cp.start()
        passed = [copy(4 + j, (*chip, c), sibling) for j, chip in enumerate(chips)]
        for j, chip in enumerate(chips):
            copy(1 + j, (*chip, c), me).wait_recv()
            passed[j].start()
        copy(0, sibling, me).wait_recv()
        for j, chip in enumerate(chips):
            copy(4 + j, (*chip, 1 - c), me).wait_recv()
        for cp in first + passed:
            cp.wait_send()
        mine.wait()
        tot = all_ref[0:rows, :]
        for k in range(1, N_DEV):
            tot = tot + all_ref[k * rows:(k + 1) * rows, :]
        out_ref[...] = tot

    vm = pl.BlockSpec(memory_space=pltpu.VMEM)
    return _call(body, name="all_reduce_small", in_specs=[vm], out_specs=vm,
                 out_shape=jax.ShapeDtypeStruct(buf.shape, F32),
                 scratch_shapes=[pltpu.VMEM((N_DEV * rows, LANES), F32), pltpu.SemaphoreType.DMA((7,)),
                                 pltpu.SemaphoreType.DMA((7,)), pltpu.SemaphoreType.DMA],
                 compiler_params=pltpu.CompilerParams(has_side_effects=True,
                                                      vmem_limit_bytes=V7X_VMEM_LIMIT_BYTES))(buf)


ROW_TILE = 128


def _chip_partial(stack, received, layer, place):
    _, half_rows, cols = received.shape
    tr = _pick(half_rows, (ROW_TILE, ROW_TILE // 2))
    nh = half_rows // tr

    def body(place_ref, a_ref, b_ref, o_ref):
        o_ref[...] = (a_ref[...].astype(F32) + b_ref[...].astype(F32)).astype(BF16)

    blk = pl.BlockSpec((None, tr, cols), lambda s, i, pr: (s, i, 0))
    grid_spec = pltpu.PrefetchScalarGridSpec(
        num_scalar_prefetch=1, grid=(N_CHIPS, nh),
        in_specs=[pl.BlockSpec((None, None, tr, cols), lambda s, i, pr: (layer, s, pr[0] * nh + i, 0)), blk],
        out_specs=blk)
    return _call(body, name="chip_partial", grid_spec=grid_spec, out_shape=jax.ShapeDtypeStruct(received.shape, BF16),
                 compiler_params=_params(2))(place, stack, received)


def _final_sum(partial, from_chips, layer, place, finals):
    _, half_rows, cols = partial.shape
    tr = _pick(half_rows, (ROW_TILE, ROW_TILE // 2))
    nh = half_rows // tr

    def body(place_ref, a_ref, r_ref, *rest):
        o_ref = rest[-1]
        tot = a_ref[...].astype(F32)
        for j in range(3):
            tot = tot + r_ref[j].astype(F32)
        o_ref[...] = tot

    in_specs = [pl.BlockSpec((None, tr, cols), lambda i, pr: (pr[1], i, 0)),
                pl.BlockSpec((3, tr, cols), lambda i, pr: (0, i, 0))]
    args = [place, partial, from_chips]
    kw = {}
    if finals is not None:
        in_specs.append(pl.BlockSpec(memory_space=pl.ANY))
        args.append(finals)
        kw["input_output_aliases"] = {3: 0}
    grid_spec = pltpu.PrefetchScalarGridSpec(
        num_scalar_prefetch=1, grid=(nh,), in_specs=in_specs,
        out_specs=pl.BlockSpec((None, tr, cols), lambda i, pr: (layer, pr[0] * nh + i, 0)))
    return _call(body, name="final_sum", grid_spec=grid_spec,
                 out_shape=jax.ShapeDtypeStruct((2, 2 * half_rows, cols), F32), compiler_params=_params(1), **kw)(*args)


def _adamw(w, g, m, v, comm=None):
    n, rows, cols = w.shape
    tr = _pick(rows, (ROW_TILE, SUBLANES))
    c1 = 1.0 - ADAM_B1 ** ADAM_STEP
    c2 = 1.0 - ADAM_B2 ** ADAM_STEP

    def body(w_ref, g_ref, m_ref, v_ref, d_ref, nm_ref, nv_ref, go_ref):
        g_v = g_ref[...]
        go_ref[...] = g_v
        nm = ADAM_B1 * m_ref[...] + (1.0 - ADAM_B1) * g_v
        nv = ADAM_B2 * v_ref[...] + (1.0 - ADAM_B2) * (g_v * g_v)
        nm_ref[...] = nm
        nv_ref[...] = nv
        d_ref[...] = -ADAM_LR * ((nm / c1) / (jnp.sqrt(nv / c2) + ADAM_EPS) + ADAM_WD * w_ref[...])

    blk = pl.BlockSpec((None, tr, cols), lambda l, i: (l, i, 0))
    shape = jax.ShapeDtypeStruct(w.shape, F32)
    return _hosted_call(body, comm, "adamw", (n, rows // tr), [w, g, m, v], [blk] * 4, [shape] * 4, [blk] * 4)


def _to_heads(a, n_heads, transposed=False):
    t_len = a.shape[0]
    return a.reshape(t_len, n_heads, HEAD_DIM).transpose((1, 2, 0) if transposed else (1, 0, 2))


def _from_heads(a, transposed=False):
    a = a.transpose((2, 0, 1) if transposed else (1, 0, 2))
    return a.reshape(a.shape[0], a.shape[1] * HEAD_DIM)


class _Schedule:
    def __init__(self):
        self.sites = {}
        self.open = []

    def add(self, site, make, done=None):
        self.sites.setdefault(site, []).append((make, done))

    def begin(self, site):
        self.open = [(make(), done) for make, done in self.sites.pop(site, [])]
        return _merge_comms([cm for cm, _ in self.open])

    def end(self):
        for cm, done in self.open:
            if done is not None:
                done(cm)
        self.open = []


def _ffn_fwd(x, gain, wg, which, layer, sched):
    w_in_name, w_out_name = f"ffn{which}_w_in", f"ffn{which}_w_out"
    h = _rmsnorm_fwd(x, gain)
    comm = sched.begin(("ffn_in", layer, which))
    gu, act = _mm_ffn_in(h, wg[w_in_name], layer, comm)
    sched.end()
    comm = sched.begin(("ffn_out", layer, which))
    x_new = _mm_out_res("ffn_out", act, wg[w_out_name], layer, x, FFN_RESIDUAL_WEIGHT, comm)[0]
    sched.end()
    return x_new, (x, h, gu, act)


def _ffn_bwd(dx, dxb, saved, gain, wg, which, layer, stacks, sched, weights_first=False):
    w_in_name, w_out_name = f"ffn{which}_w_in", f"ffn{which}_w_out"
    x, h, gu, act = saved
    out = {}

    def dact():
        comm = sched.begin(("ffn_dact", layer, which))
        out["dgu"] = _mm_dact_swiglu(dxb, wg[w_out_name], layer, gu, FFN_RESIDUAL_WEIGHT, comm)
        sched.end()

    def dw_out():
        comm = sched.begin(("ffn_dw_out", layer, which))
        stacks[w_out_name] = _dw_rows("ffn_dw_out", act, dxb, layer, stacks[w_out_name], FFN_RESIDUAL_WEIGHT, comm)
        sched.end()

    def dh():
        comm = sched.begin(("ffn_dh", layer, which))
        out["dh"] = _mm_dh_ffn(out["dgu"], wg[w_in_name], layer, comm)
        sched.end()

    def dw_in():
        comm = sched.begin(("ffn_dw_in", layer, which))
        stacks[w_in_name] = _dw_ffn_in(h, out["dgu"], layer, stacks[w_in_name], comm)
        sched.end()

    for step in ((dact, dw_in, dw_out, dh) if weights_first else (dact, dw_out, dh, dw_in)):
        step()
    return _rmsnorm_bwd(out["dh"], x, gain, dx)


def _mix_fwd(x, gain, wg, layer, small, tables, sched):
    h = _rmsnorm_fwd(x, gain)
    comm = sched.begin(("mix_in", layer))
    p = _mm_proj(h, wg["w_in"], layer, comm)[0]
    sched.end()
    qkv = _rope_fwd(p, tables)
    q_rows, k_rows, v_rows = qkv[:, :Q_END], qkv[:, Q_END:K_END], qkv[:, K_END:V_END]
    qt, q = _to_heads(q_rows, N_Q_HEADS, True), _to_heads(q_rows, N_Q_HEADS)
    kt, k = _to_heads(k_rows, N_KV_HEADS, True), _to_heads(k_rows, N_KV_HEADS)
    vt, v = _to_heads(v_rows, N_KV_HEADS, True), _to_heads(v_rows, N_KV_HEADS)
    comm = sched.begin(("attn", layer))
    attn = _from_heads(_attn_fwd(qt, k, vt, small["snk"], comm), True)
    sched.end()
    y = _conv_fwd(p, small["conv_w"], small["conv_b"])
    conv = _conv_post_fwd(y, small["conv_ln_g"], small["conv_ln_b"])
    sgu = _sgu_fwd(p, small["sgu_ln_g"], small["sgu_ln_b"], small["sgu_w"], small["sgu_bt"])
    mix = jnp.concatenate([attn, conv, sgu], axis=1)
    x_new = _mm_out_res("mix_out", mix, wg["w_out"], layer, x, 1.0)[0]
    return x_new, (x, h, p, (qt, q, k, kt, v), y, mix)


def _mix_bwd(dx, dxb, saved, gain, wg, small, tables, layer, stacks, sched):
    x, h, p, (qt, q, k, kt, v), y, mix = saved
    comm = sched.begin(("mix_dout", layer))
    dmix = _mm_dmix(dxb, wg["w_out"], layer, comm)
    sched.end()
    stacks["w_out"] = _dw_rows("mix_dw_out", mix, dxb, layer, stacks["w_out"], 1.0)
    do_rows = dmix[:, :Q_END].astype(BF16)
    comm = sched.begin(("attn_bwd", layer))
    dqt, dkp, dkc, dvp, dvc, dsnk = _attn_bwd(qt, q, k, kt, v, small["snk"], _to_heads(do_rows, N_Q_HEADS, True),
                                              _to_heads(do_rows, N_Q_HEADS), comm)
    sched.end()
    dy, d_ln_g, d_ln_b, d_conv_b = _conv_post_bwd(dmix, y, small["conv_ln_g"], small["conv_ln_b"])
    dalin, dagate, d_conv_w = _conv_bwd(p, dy, small["conv_w"])
    du, dvin, d_sgu_w, d_sgu_bt, d_sgu_g, d_sgu_b = _sgu_bwd(p, dmix, small["sgu_ln_g"], small["sgu_ln_b"],
                                                           small["sgu_w"], small["sgu_bt"])
    dp = _assemble_dp(_from_heads(dqt, True), _from_heads(dkc), _from_heads(dkp), _from_heads(dvc), _from_heads(dvp),
                      tables, dalin, dagate, du, dvin)
    comm = sched.begin(("mix_dw_in", layer))
    stacks["w_in"] = _dw_mix_in(h, dp, layer, stacks["w_in"], comm)
    sched.end()
    comm = sched.begin(("mix_dh", layer))
    dh = _mm_dh_mix(dp, wg["w_in"], layer, comm)
    sched.end()
    dx_in, dxb_in, dgain = _rmsnorm_bwd(dh, x, gain, dx)
    grads = {
        "norm_mix": dgain[0], "conv_dw_w": d_conv_w[:CONV_WIDTH], "conv_dw_b": d_conv_b[0],
        "conv_ln_g": d_ln_g[0], "conv_ln_b": d_ln_b[0], "sgu_ln_g": d_sgu_g[0], "sgu_ln_b": d_sgu_b[0],
        "sgu_w": d_sgu_w, "sgu_b": d_sgu_bt[:, :SGU_HEADS].T, "attn_sinks": dsnk[:, :GQ, 0].reshape(N_Q_HEADS),
    }
    return dx_in, dxb_in, grads


BIG = ("ffn1_w_in", "ffn1_w_out", "w_in", "w_out", "ffn2_w_in", "ffn2_w_out")
SMALL = ("norm_ffn1", "norm_mix", "conv_dw_w", "conv_dw_b", "conv_ln_g", "conv_ln_b", "sgu_ln_g", "sgu_ln_b",
         "sgu_w", "sgu_b", "attn_sinks", "norm_ffn2", "final_norm")
WEIGHTS = ("norm_ffn1", "ffn1_w_in", "ffn1_w_out", "norm_mix", "w_in", "conv_dw_w", "conv_dw_b", "conv_ln_g",
           "conv_ln_b", "sgu_ln_g", "sgu_ln_b", "sgu_w", "sgu_b", "attn_sinks", "w_out", "norm_ffn2", "ffn2_w_in",
           "ffn2_w_out", "final_norm")
PACK_ROWS = SUBLANES * LANES

FIRST_GATHER = [("ffn1_w_in", 0, None)]
FORWARD_PLAN = {
    ("ffn_in", 0, 1): [("ffn1_w_out", 0, None), ("w_in", 0, None)],
    ("ffn_out", 0, 1): [("ffn2_w_in", 0, 0)],
    ("mix_in", 0): [("w_out", 0, None)],
    ("attn", 0): [("ffn2_w_in", 0, 1)],
    ("ffn_in", 0, 2): [("ffn2_w_out", 0, None), ("ffn1_w_in", 1, 0)],
    ("ffn_out", 0, 2): [("ffn1_w_in", 1, 1)],
    ("ffn_in", 1, 1): [("ffn1_w_out", 1, None), ("w_in", 1, None)],
    ("ffn_out", 1, 1): [("ffn2_w_in", 1, 0)],
    ("mix_in", 1): [("w_out", 1, None)],
    ("attn", 1): [("ffn2_w_in", 1, 1)],
    ("ffn_in", 1, 2): [("ffn2_w_out", 1, None)],
}
SUBLAYER_WEIGHTS = {"ffn1": ["ffn1_w_out", "ffn1_w_in"], "ffn2": ["ffn2_w_out", "ffn2_w_in"], "mix": ["w_out", "w_in"]}


def _pack(arrays):
    flat = jnp.concatenate([a.reshape(-1).astype(F32) for a in arrays])
    pad = (-flat.shape[0]) % PACK_ROWS
    return jnp.pad(flat, (0, pad)).reshape(-1, LANES)


def _unpack(buf, shapes):
    flat = buf.reshape(-1)
    out, off = [], 0
    for s in shapes:
        n = 1
        for d in s:
            n *= d
        out.append(flat[off:off + n].reshape(s))
        off += n
    return out


def kernel(x, positions, norm_ffn1, ffn1_w_in, ffn1_w_out, norm_mix, w_in, conv_dw_w, conv_dw_b, conv_ln_g, conv_ln_b, sgu_ln_g, sgu_ln_b, sgu_w, sgu_b, attn_sinks, w_out, norm_ffn2, ffn2_w_in, ffn2_w_out, final_norm, loss_target, m_norm_ffn1, m_ffn1_w_in, m_ffn1_w_out, m_norm_mix, m_w_in, m_conv_dw_w, m_conv_dw_b, m_conv_ln_g, m_conv_ln_b, m_sgu_ln_g, m_sgu_ln_b, m_sgu_w, m_sgu_b, m_attn_sinks, m_w_out, m_norm_ffn2, m_ffn2_w_in, m_ffn2_w_out, m_final_norm, v_norm_ffn1, v_ffn1_w_in, v_ffn1_w_out, v_norm_mix, v_w_in, v_conv_dw_w, v_conv_dw_b, v_conv_ln_g, v_conv_ln_b, v_sgu_ln_g, v_sgu_ln_b, v_sgu_w, v_sgu_b, v_attn_sinks, v_w_out, v_norm_ffn2, v_ffn2_w_in, v_ffn2_w_out, v_final_norm):
    w = dict(norm_ffn1=norm_ffn1, ffn1_w_in=ffn1_w_in, ffn1_w_out=ffn1_w_out, norm_mix=norm_mix, w_in=w_in,
             conv_dw_w=conv_dw_w, conv_dw_b=conv_dw_b, conv_ln_g=conv_ln_g, conv_ln_b=conv_ln_b, sgu_ln_g=sgu_ln_g,
             sgu_ln_b=sgu_ln_b, sgu_w=sgu_w, sgu_b=sgu_b, attn_sinks=attn_sinks, w_out=w_out, norm_ffn2=norm_ffn2,
             ffn2_w_in=ffn2_w_in, ffn2_w_out=ffn2_w_out, final_norm=final_norm)
    m = dict(norm_ffn1=m_norm_ffn1, ffn1_w_in=m_ffn1_w_in, ffn1_w_out=m_ffn1_w_out, norm_mix=m_norm_mix, w_in=m_w_in,
             conv_dw_w=m_conv_dw_w, conv_dw_b=m_conv_dw_b, conv_ln_g=m_conv_ln_g, conv_ln_b=m_conv_ln_b,
             sgu_ln_g=m_sgu_ln_g, sgu_ln_b=m_sgu_ln_b, sgu_w=m_sgu_w, sgu_b=m_sgu_b, attn_sinks=m_attn_sinks,
             w_out=m_w_out, norm_ffn2=m_norm_ffn2, ffn2_w_in=m_ffn2_w_in, ffn2_w_out=m_ffn2_w_out,
             final_norm=m_final_norm)
    v = dict(norm_ffn1=v_norm_ffn1, ffn1_w_in=v_ffn1_w_in, ffn1_w_out=v_ffn1_w_out, norm_mix=v_norm_mix, w_in=v_w_in,
             conv_dw_w=v_conv_dw_w, conv_dw_b=v_conv_dw_b, conv_ln_g=v_conv_ln_g, conv_ln_b=v_conv_ln_b,
             sgu_ln_g=v_sgu_ln_g, sgu_ln_b=v_sgu_ln_b, sgu_w=v_sgu_w, sgu_b=v_sgu_b, attn_sinks=v_attn_sinks,
             w_out=v_w_out, norm_ffn2=v_norm_ffn2, ffn2_w_in=v_ffn2_w_in, ffn2_w_out=v_ffn2_w_out,
             final_norm=v_final_norm)
    depth = norm_ffn1.shape[0]
    assert depth == 2 and x.shape[0] == 1
    xc = lax.axis_index("x")
    yc = lax.axis_index("y")
    cc = lax.axis_index("c")
    chip = 2 * xc + yc

    chip_arr = chip.reshape(1).astype(jnp.int32)
    place = jnp.stack([cc, chip]).astype(jnp.int32)
    wg = {n: _cast_place(w[n], chip_arr) for n in BIG}
    sched = _Schedule()

    def gather(pieces):
        names = sorted({n for n, _, _ in pieces})
        half = w["ffn1_w_in"].shape[1] // 2

        def make():
            rows = lambda n, part: (0, wg[n].shape[2]) if part is None else (part * half, half)
            return _gather_comm([wg[n] for n in names], [(names.index(n), l, *rows(n, part)) for n, l, part in pieces])

        return make, lambda cm: wg.update(zip(names, cm.aliased_out))

    make, done = gather(FIRST_GATHER)
    first = make()
    _standalone("gather_first", first)
    done(first)
    for site, pieces in FORWARD_PLAN.items():
        sched.add(site, *gather(pieces))
    conv_w_full = _all_gather_weights([], [conv_dw_w])[0].transpose(0, 2, 1, 3).reshape(depth, CONV_WIDTH, CONV_CH)
    conv_w_full = jnp.pad(conv_w_full, ((0, 0), (0, CONV_PAD - CONV_WIDTH), (0, 0)))

    tables = _rope_tables(positions)
    small = []
    for l in range(depth):
        small.append(dict(
            snk=jnp.broadcast_to(attn_sinks[l].reshape(N_KV_HEADS, 1, GQ, 1), (N_KV_HEADS, 1, GQ, BLK)).reshape(
                N_KV_HEADS, 1, GQ * BLK),
            conv_w=conv_w_full[l], conv_b=conv_dw_b[l][None], conv_ln_g=conv_ln_g[l][None],
            conv_ln_b=conv_ln_b[l][None], sgu_ln_g=sgu_ln_g[l][None], sgu_ln_b=sgu_ln_b[l][None], sgu_w=sgu_w[l],
            sgu_bt=sgu_b[l].T))

    xs = x[0]
    saved = []
    for l in range(depth):
        xs, s1 = _ffn_fwd(xs, norm_ffn1[l][None], wg, 1, l, sched)
        xs, s2 = _mix_fwd(xs, norm_mix[l][None], wg, l, small[l], tables, sched)
        xs, s3 = _ffn_fwd(xs, norm_ffn2[l][None], wg, 2, l, sched)
        saved.append((s1, s2, s3))
    dx, dxb, d_final, loss_part = _loss_head(xs, final_norm[None], loss_target[0])

    stacks = {n: None for n in BIG}
    partials, from_chips = {}, {}
    finals = {n: None for n in BIG}

    def to_owner(layer, names):
        def done(cm):
            for n, received in zip(names, cm.fresh_out):
                partials[n, layer] = _chip_partial(stacks[n], received, layer, place)

        return lambda: _to_owner_comm([stacks[n] for n in names], layer), done

    def between_chips(layer, names, then_sum=()):
        def done(cm):
            from_chips.update({(n, layer): r for n, r in zip(names, cm.fresh_out)})
            for n in then_sum:
                finals[n] = _final_sum(partials[n, layer], from_chips[n, layer], layer, place, finals[n])

        return lambda: _chip_comm([partials[n, layer] for n in names]), done

    def from_owner(layer, names):
        return (lambda: _from_owner_comm([finals[n] for n in names], layer),
                lambda cm: finals.update(zip(names, cm.aliased_out)))

    order = [(l, kind) for l in reversed(range(depth)) for kind in ("ffn2", "mix", "ffn1")]
    for (layer, kind), (nxt_layer, nxt_kind) in zip(order[:-1], order[1:]):
        names = SUBLAYER_WEIGHTS[kind]
        if (nxt_layer, nxt_kind) == order[-1]:
            w_out_name, w_in_name = SUBLAYER_WEIGHTS[nxt_kind]
            which = int(nxt_kind[-1])
            sched.add(("ffn_dact", nxt_layer, which), *to_owner(layer, names))
            sched.add(("ffn_dw_in", nxt_layer, which), *between_chips(layer, names, names))
            sched.add(("ffn_dw_out", nxt_layer, which), *from_owner(layer, names))
            sched.add(("ffn_dw_out", nxt_layer, which), *to_owner(nxt_layer, [w_in_name]))
            sched.add(("ffn_dh", nxt_layer, which), *between_chips(nxt_layer, [w_in_name], [w_in_name]))
            sched.add(("ffn_dh", nxt_layer, which), *to_owner(nxt_layer, [w_out_name]))
        elif nxt_kind == "mix":
            after = order[order.index((nxt_layer, nxt_kind)) + 1]
            sched.add(("mix_dout", nxt_layer), *to_owner(layer, names))
            if after == order[-1]:
                which = int(after[1][-1])
                sched.add(("attn_bwd", nxt_layer), *between_chips(layer, names[1:], names[1:]))
                sched.add(("mix_dh", nxt_layer), *from_owner(layer, names[1:]))
                sched.add(("ffn_dw_in", after[0], which), *between_chips(layer, names[:1], names[:1]))
                sched.add(("ffn_dw_out", after[0], which), *from_owner(layer, names[:1]))
            else:
                sched.add(("attn_bwd", nxt_layer), *between_chips(layer, names[:1], names[:1]))
                sched.add(("mix_dh", nxt_layer), *from_owner(layer, names[:1]))
                which = int(after[1][-1])
                sched.add(("ffn_dh", after[0], which), *between_chips(layer, names[1:], names[1:]))
                sched.add(("ffn_dw_in", after[0], which), *from_owner(layer, names[1:]))
        else:
            which = int(nxt_kind[-1])
            sched.add(("ffn_dact", nxt_layer, which), *to_owner(layer, names))
            if kind == "mix":
                sched.add(("ffn_dw_out", nxt_layer, which), *between_chips(layer, names, names))
            else:
                sched.add(("ffn_dw_out", nxt_layer, which), *between_chips(layer, names[:1]))
                sched.add(("ffn_dh", nxt_layer, which), *between_chips(layer, names[1:], names))
            sched.add(("ffn_dw_in", nxt_layer, which), *from_owner(layer, names))

    small_grads = [None] * depth
    for l in reversed(range(depth)):
        s1, s2, s3 = saved[l]
        dx, dxb, dg2 = _ffn_bwd(dx, dxb, s3, norm_ffn2[l][None], wg, 2, l, stacks, sched)
        dx, dxb, gm = _mix_bwd(dx, dxb, s2, norm_mix[l][None], wg, small[l], tables, l, stacks, sched)
        dx, dxb, dg1 = _ffn_bwd(dx, dxb, s1, norm_ffn1[l][None], wg, 1, l, stacks, sched, weights_first=l == 0)
        gm["norm_ffn1"] = dg1[0]
        gm["norm_ffn2"] = dg2[0]
        small_grads[l] = gm
    grad_x = dx[None]
    assert not sched.sites, sched.sites

    per_layer = [n for n in SMALL if n != "final_norm"]
    small_local = [jnp.stack([small_grads[l][n] for l in range(depth)]) for n in per_layer]
    small_local += [d_final[0], loss_part[0, :1]]
    small_shapes = [a.shape for a in small_local]
    summed = _unpack(_all_reduce_small(_pack(small_local)), small_shapes)
    loss = summed[-1][0]
    sg = dict(zip(per_layer + ["final_norm"], summed[:-1]))
    sg["conv_dw_w"] = lax.dynamic_slice_in_dim(sg["conv_dw_w"], chip * LANES, LANES, axis=2)

    delta, new_m, new_v = {}, {}, {}
    shapes = [w[n].shape for n in SMALL]
    packed = [_pack([d[n] for n in SMALL])[None] for d in (w, sg, m, v)]
    last_layer, last_kind = order[-1]
    names = SUBLAYER_WEIGHTS[last_kind]
    make, done = between_chips(last_layer, names[:1], names[:1])
    cm = make()
    outs = _adamw(*packed, comm=cm)
    done(cm)
    for d, buf in zip((delta, new_m, new_v), outs[:3]):
        d.update(zip(SMALL, _unpack(buf[0], shapes)))
    make, done = from_owner(last_layer, names)
    cm = make()
    _standalone("from_owner", cm)
    done(cm)
    big_grads = dict(finals)
    for n in BIG:
        delta[n], new_m[n], new_v[n], big_grads[n] = _adamw(w[n], big_grads[n], m[n], v[n])
    grads = {**big_grads, **sg}
    return (loss, grad_x, *[grads[n] for n in WEIGHTS], *[delta[n] for n in WEIGHTS],
            *[new_m[n] for n in WEIGHTS], *[new_v[n] for n in WEIGHTS])
```

```python
import functools

import jax
import jax.numpy as jnp
from jax import lax
from jax.experimental import pallas as pl
from jax.experimental.pallas import tpu as pltpu

F32 = jnp.float32
BF16 = jnp.bfloat16
MESH_ID = pl.DeviceIdType.MESH

V7X_VMEM_LIMIT_BYTES = 56 * 2**20
LANES = 128
SUBLANES = 8

HEAD_DIM = 64
N_Q_HEADS = 16
N_KV_HEADS = 4
GQ = N_Q_HEADS // N_KV_HEADS
BLK = 128
ROT_HALF = 8
ROPE_THETA = 500000.0
CONV_WIDTH = 31
CONV_PAD = 32
CONV_CH = 512
SGU_CH = 512
SGU_HEADS = 8
Q_END = N_Q_HEADS * HEAD_DIM
K_END = Q_END + N_KV_HEADS * HEAD_DIM
V_END = K_END + N_KV_HEADS * HEAD_DIM
CONV_END = V_END + 2 * CONV_CH
IN_COLS = CONV_END + 2 * SGU_CH
NORM_EPS = 1e-5
FFN_RESIDUAL_WEIGHT = 0.5
N_CHIPS = 4
N_DEV = 8

ADAM_LR = 0.001
ADAM_B1 = 0.9
ADAM_B2 = 0.999
ADAM_EPS = 1e-08
ADAM_WD = 0.01
ADAM_STEP = 10

NN = (((1,), (0,)), ((), ()))
NT = (((1,), (1,)), ((), ()))
TN = (((0,), (0,)), ((), ()))


def _pick(n, cands):
    for c in cands:
        if n % c == 0:
            return c
    raise ValueError(f"no tile of {cands} divides {n}")


def _params(n_axes):
    return pltpu.CompilerParams(dimension_semantics=("arbitrary",) * n_axes, vmem_limit_bytes=V7X_VMEM_LIMIT_BYTES)


def _call(body, **kw):
    return pl.pallas_call(body, **kw)


def _sigmoid(x):
    return 1.0 / (1.0 + jnp.exp(-x))


class _Comm:
    def __init__(self, reads, aliased, fresh, sems, start, finish):
        self.reads, self.aliased, self.fresh, self.sems = list(reads), list(aliased), list(fresh), list(sems)
        self.start, self.finish = start, finish
        self.aliased_out, self.fresh_out = None, None


def _merge_comms(comms):
    comms = [cm for cm in comms if cm is not None]
    if not comms:
        return None
    if len(comms) == 1:
        return comms[0]

    def split(refs, counts):
        out, off = [], 0
        for n in counts:
            out.append(refs[off:off + n])
            off += n
        return out

    def run(which):
        def f(rd, al, fr, sm):
            parts = zip(split(rd, [len(cm.reads) for cm in comms]), split(al, [len(cm.aliased) for cm in comms]),
                        split(fr, [len(cm.fresh) for cm in comms]), split(sm, [len(cm.sems) for cm in comms]))
            for cm, (r, a, f_, s) in zip(comms, parts):
                getattr(cm, which)(r, a, f_, s)
        return f

    merged = _Comm(sum((cm.reads for cm in comms), []), sum((cm.aliased for cm in comms), []),
                   sum((cm.fresh for cm in comms), []), sum((cm.sems for cm in comms), []), run("start"), run("finish"))
    merged.parts = comms
    return merged


def _hosted_call(body, comm, name, grid, inputs, in_specs, out_shape, out_specs, scratch_shapes=(), aliases=None):
    n_in, n_out, n_scr = len(inputs), len(out_shape), len(scratch_shapes)
    aliases = dict(aliases or {})
    if comm is None:
        return _call(body, name=name, grid=grid, in_specs=list(in_specs), out_specs=list(out_specs),
                     out_shape=list(out_shape), scratch_shapes=list(scratch_shapes), input_output_aliases=aliases,
                     compiler_params=_params(len(grid)))(*inputs)
    nr, na, nf = len(comm.reads), len(comm.aliased), len(comm.fresh)

    def full(*refs):
        ins = refs[:n_in]
        rd = refs[n_in:n_in + nr]
        pos = n_in + nr + na
        outs = refs[pos:pos + n_out]
        al = refs[pos + n_out:pos + n_out + na]
        fr = refs[pos + n_out + na:pos + n_out + na + nf]
        pos = pos + n_out + na + nf
        scr = refs[pos:pos + n_scr]
        sems = refs[pos + n_scr:]
        first, last = None, None
        for axis, size in enumerate(grid):
            f, l = pl.program_id(axis) == 0, pl.program_id(axis) == size - 1
            first = f if first is None else jnp.logical_and(first, f)
            last = l if last is None else jnp.logical_and(last, l)

        @pl.when(first)
        def _():
            comm.start(rd, al, fr, sems)

        body(*ins, *outs, *scr)

        @pl.when(last)
        def _():
            comm.finish(rd, al, fr, sems)

    hbm = pl.BlockSpec(memory_space=pltpu.HBM)
    for i in range(na):
        aliases[n_in + nr + i] = n_out + i
    struct = [jax.ShapeDtypeStruct(a.shape, a.dtype) for a in comm.aliased]
    res = _call(full, name=name, grid=grid, in_specs=list(in_specs) + [hbm] * (nr + na),
                out_specs=list(out_specs) + [hbm] * (na + nf), out_shape=list(out_shape) + struct + comm.fresh,
                scratch_shapes=list(scratch_shapes) + comm.sems, input_output_aliases=aliases,
                compiler_params=pltpu.CompilerParams(dimension_semantics=("arbitrary",) * len(grid),
                                                     vmem_limit_bytes=V7X_VMEM_LIMIT_BYTES, has_side_effects=True),
                )(*inputs, *comm.reads, *comm.aliased)
    _deliver(comm, res[n_out:n_out + na], res[n_out + na:])
    return res[:n_out]


def _deliver(comm, aliased_out, fresh_out):
    comm.aliased_out, comm.fresh_out = list(aliased_out), list(fresh_out)
    off_a = off_f = 0
    for part in getattr(comm, "parts", []):
        _deliver(part, aliased_out[off_a:off_a + len(part.aliased)], fresh_out[off_f:off_f + len(part.fresh)])
        off_a += len(part.aliased)
        off_f += len(part.fresh)


def _standalone(name, comm):
    def body(*refs):
        nr, na, nf = len(comm.reads), len(comm.aliased), len(comm.fresh)
        rd, al, fr, sems = refs[:nr], refs[nr + na:nr + 2 * na], refs[nr + 2 * na:nr + 2 * na + nf], refs[nr + 2 * na + nf:]
        comm.start(rd, al, fr, sems)
        comm.finish(rd, al, fr, sems)

    nr, na, nf = len(comm.reads), len(comm.aliased), len(comm.fresh)
    struct = [jax.ShapeDtypeStruct(a.shape, a.dtype) for a in comm.aliased]
    res = _call(body, name=name, in_specs=_hbm_specs(nr + na), out_specs=_hbm_specs(na + nf),
                out_shape=struct + comm.fresh, scratch_shapes=comm.sems,
                input_output_aliases={nr + i: i for i in range(na)},
                compiler_params=_comm_params())(*comm.reads, *comm.aliased)
    _deliver(comm, res[:na], res[na:])


def _matmul(name, grid, a_ops, b_ops, terms, dims, out_shape, out_specs, epilogue, extra_ops=(), nk=1,
            acc_shapes=(), alias=None, comm=None, chunk=None):
    na, nb, ne, no = len(a_ops), len(b_ops), len(extra_ops), len(out_shape)

    def body(*refs):
        a = refs[:na]
        b = refs[na:na + nb]
        e = refs[na + nb:na + nb + ne]
        first_out = na + nb + ne + (1 if alias is not None else 0)
        o = refs[first_out:first_out + no]
        accs = refs[first_out + no:]

        def partial(t, cols=None):
            tot = None
            for ai, bi in t:
                rhs = b[bi][...] if cols is None else (b[bi][cols, :] if dims == NT else b[bi][:, cols])
                d = lax.dot_general(a[ai][...], rhs, dims, preferred_element_type=F32)
                tot = d if tot is None else tot + d
            return tot

        if nk == 1 and chunk is not None:
            width = b[0].shape[0] if dims == NT else b[0].shape[1]
            for c0 in range(0, width, chunk):
                cols = slice(c0, min(c0 + chunk, width))
                epilogue([partial(t, cols) for t in terms], e, o, cols)
        elif nk == 1:
            epilogue([partial(t) for t in terms], e, o)
        else:
            k = pl.program_id(len(grid) - 1)

            @pl.when(k == 0)
            def _():
                for acc in accs:
                    acc[...] = jnp.zeros(acc.shape, F32)

            for acc, t in zip(accs, terms):
                acc[...] += partial(t)

            @pl.when(k == nk - 1)
            def _():
                epilogue([acc[...] for acc in accs], e, o)

    ops = list(a_ops) + list(b_ops) + list(extra_ops)
    arrays = [x for x, _ in ops]
    in_specs = [s for _, s in ops]
    aliases = {}
    if alias is not None:
        arrays.append(alias[0])
        in_specs.append(pl.BlockSpec(memory_space=pl.ANY))
        aliases[len(arrays) - 1] = alias[1]
    scratch = [pltpu.VMEM(s, F32) for s in acc_shapes] if nk > 1 else []
    return _hosted_call(body, comm, name, grid, arrays, in_specs, out_shape, out_specs, scratch, aliases)


def _mm_ffn_in(h, w_g, layer, comm=None):
    t_len, d = h.shape
    fs = w_g.shape[3]
    f = 2 * fs
    tm = _pick(t_len, (2048, 1024, 512))
    tn = _pick(fs, (256, 128))
    nj = fs // tn

    def epilogue(accs, e, o):
        g, u = accs
        o[0][0] = g.astype(BF16)
        o[0][1] = u.astype(BF16)
        o[1][...] = (g * _sigmoid(g) * u).astype(BF16)

    return _matmul(
        "ffn_in", (t_len // tm, 2, nj),
        [(h, pl.BlockSpec((tm, d), lambda i, s, j: (i, 0)))],
        [(w_g, pl.BlockSpec((None, None, d, tn), lambda i, s, j: (layer, s, 0, j))),
         (w_g, pl.BlockSpec((None, None, d, tn), lambda i, s, j: (layer, s + 2, 0, j)))],
        [[(0, 0)], [(0, 1)]], NN,
        [jax.ShapeDtypeStruct((2, t_len, f), BF16), jax.ShapeDtypeStruct((t_len, f), BF16)],
        [pl.BlockSpec((2, tm, tn), lambda i, s, j: (0, i, s * nj + j)),
         pl.BlockSpec((tm, tn), lambda i, s, j: (i, s * nj + j))],
        epilogue, comm=comm)


def _mm_out_res(name, a, w_g, layer, x, scale, comm=None):
    t_len = a.shape[0]
    ks, n = w_g.shape[2], w_g.shape[3]
    tm = _pick(t_len, (512,))
    tn = _pick(n, (1024,))

    def epilogue(accs, e, o):
        o[0][...] = e[0][...] + scale * accs[0]

    return _matmul(
        name, (t_len // tm, n // tn),
        [(a, pl.BlockSpec((tm, ks), lambda i, j, s=s: (i, s))) for s in range(N_CHIPS)],
        [(w_g, pl.BlockSpec((None, None, ks, tn), lambda i, j, s=s: (layer, s, 0, j))) for s in range(N_CHIPS)],
        [[(s, s) for s in range(N_CHIPS)]], NN,
        [jax.ShapeDtypeStruct((t_len, n), F32)],
        [pl.BlockSpec((tm, tn), lambda i, j: (i, j))],
        epilogue, extra_ops=[(x, pl.BlockSpec((tm, tn), lambda i, j: (i, j)))], comm=comm)


def _mm_proj(h, w_g, layer, comm=None):
    t_len, d = h.shape
    cs = w_g.shape[3]
    tm = _pick(t_len, (1024, 512))

    def epilogue(accs, e, o):
        o[0][...] = accs[0]

    return _matmul(
        "mix_in", (t_len // tm, N_CHIPS),
        [(h, pl.BlockSpec((tm, d), lambda i, s: (i, 0)))],
        [(w_g, pl.BlockSpec((None, None, d, cs), lambda i, s: (layer, s, 0, 0)))],
        [[(0, 0)]], NN,
        [jax.ShapeDtypeStruct((t_len, N_CHIPS * cs), F32)],
        [pl.BlockSpec((tm, cs), lambda i, s: (i, s))],
        epilogue, comm=comm)


def _mm_dact_swiglu(dxb, w_g, layer, gu, scale, comm=None):
    t_len, d = dxb.shape
    rs = w_g.shape[2]
    tm = _pick(t_len, (512,))
    tn = _pick(rs, (1408, 256, 128))
    nj = rs // tn

    def epilogue(accs, e, o, cols):
        dact = scale * accs[0]
        g = e[0][0, :, cols].astype(F32)
        u = e[0][1, :, cols].astype(F32)
        sig = _sigmoid(g)
        o[0][0, :, cols] = (dact * u * (sig * (1.0 + g * (1.0 - sig)))).astype(BF16)
        o[0][1, :, cols] = (dact * (g * sig)).astype(BF16)

    gu_spec = pl.BlockSpec((2, tm, tn), lambda i, s, j: (0, i, s * nj + j))
    return _matmul(
        "ffn_dact", (t_len // tm, N_CHIPS, nj),
        [(dxb, pl.BlockSpec((tm, d), lambda i, s, j: (i, 0)))],
        [(w_g, pl.BlockSpec((None, None, tn, d), lambda i, s, j: (layer, s, j, 0)))],
        [[(0, 0)]], NT,
        [jax.ShapeDtypeStruct(gu.shape, BF16)], [gu_spec],
        epilogue, extra_ops=[(gu, gu_spec)], comm=comm, chunk=2 * LANES)[0]


def _mm_dh_ffn(dgu, w_g, layer, comm=None):
    t_len = dgu.shape[1]
    d, fs = w_g.shape[2], w_g.shape[3]
    tm = _pick(t_len, (1024, 512))
    tk = _pick(fs, (256, 128))
    nks = fs // tk
    nk = 2 * nks

    def epilogue(accs, e, o):
        o[0][...] = accs[0]

    return _matmul(
        "ffn_dh", (t_len // tm, nk),
        [(dgu, pl.BlockSpec((None, tm, tk), lambda i, k: (0, i, k))),
         (dgu, pl.BlockSpec((None, tm, tk), lambda i, k: (1, i, k)))],
        [(w_g, pl.BlockSpec((None, None, d, tk), lambda i, k: (layer, k // nks, 0, k % nks))),
         (w_g, pl.BlockSpec((None, None, d, tk), lambda i, k: (layer, k // nks + 2, 0, k % nks)))],
        [[(0, 0), (1, 1)]], NT,
        [jax.ShapeDtypeStruct((t_len, d), F32)],
        [pl.BlockSpec((tm, d), lambda i, k: (i, 0))],
        epilogue, nk=nk, acc_shapes=[(tm, d)], comm=comm)[0]


def _mm_dw(name, a, a_spec_of, b, b_spec_of, layer, stack, rows, cols, tn, scale, comm=None):
    t_len = a.shape[0]
    tt = _pick(t_len, (1024, 512))
    nj = cols // tn

    def epilogue(accs, e, o):
        o[0][...] = (scale * accs[0]).astype(BF16)

    shape = jax.ShapeDtypeStruct((2, N_CHIPS, rows, cols), BF16)
    return _matmul(
        name, (N_CHIPS, nj, t_len // tt),
        [(a, a_spec_of(tt))], [(b, b_spec_of(tt, tn, nj))],
        [[(0, 0)]], TN, [shape],
        [pl.BlockSpec((None, None, rows, tn), lambda s, j, t: (layer, s, 0, j))],
        epilogue, nk=t_len // tt, acc_shapes=[(rows, tn)],
        alias=None if stack is None else (stack, 0), comm=comm)[0]


def _dw_ffn_in(h, dgu, layer, stack, comm=None):
    d = h.shape[1]
    fs = dgu.shape[2] // 2
    tn = _pick(fs, (1408, 256))
    return _mm_dw(
        "ffn_dw_in", h, lambda tt: pl.BlockSpec((tt, d), lambda s, j, t: (t, 0)),
        dgu, lambda tt, tn_, nj: pl.BlockSpec((None, tt, tn_), lambda s, j, t: (s // 2, t, (s % 2) * nj + j)),
        layer, stack, d, fs, tn, 1.0, comm)


def _dw_rows(name, a, dxb, layer, stack, scale, comm=None):
    rs = a.shape[1] // N_CHIPS
    d = dxb.shape[1]
    tn = _pick(d, (1024,))
    return _mm_dw(
        name, a, lambda tt: pl.BlockSpec((tt, rs), lambda s, j, t: (t, s)),
        dxb, lambda tt, tn_, nj: pl.BlockSpec((tt, tn_), lambda s, j, t: (t, j)),
        layer, stack, rs, d, tn, scale, comm)


def _dw_mix_in(h, dp, layer, stack, comm=None):
    d = h.shape[1]
    cs = dp.shape[1] // N_CHIPS
    return _mm_dw(
        "mix_dw_in", h, lambda tt: pl.BlockSpec((tt, d), lambda s, j, t: (t, 0)),
        dp, lambda tt, tn_, nj: pl.BlockSpec((tt, tn_), lambda s, j, t: (t, s)),
        layer, stack, d, cs, cs, 1.0, comm)


def _mm_dmix(dxb, w_g, layer, comm=None):
    t_len, d = dxb.shape
    rs = w_g.shape[2]
    tm = _pick(t_len, (1024, 512))

    def epilogue(accs, e, o):
        o[0][...] = accs[0]

    return _matmul(
        "mix_dout", (t_len // tm, N_CHIPS),
        [(dxb, pl.BlockSpec((tm, d), lambda i, s: (i, 0)))],
        [(w_g, pl.BlockSpec((None, None, rs, d), lambda i, s: (layer, s, 0, 0)))],
        [[(0, 0)]], NT,
        [jax.ShapeDtypeStruct((t_len, N_CHIPS * rs), F32)],
        [pl.BlockSpec((tm, rs), lambda i, s: (i, s))],
        epilogue, comm=comm)[0]


def _mm_dh_mix(dp, w_g, layer, comm=None):
    t_len = dp.shape[0]
    d, cs = w_g.shape[2], w_g.shape[3]
    tm = _pick(t_len, (1024, 512))

    def epilogue(accs, e, o):
        o[0][...] = accs[0]

    return _matmul(
        "mix_dh", (t_len // tm, N_CHIPS),
        [(dp, pl.BlockSpec((tm, cs), lambda i, k: (i, k)))],
        [(w_g, pl.BlockSpec((None, None, d, cs), lambda i, k: (layer, k, 0, 0)))],
        [[(0, 0)]], NT,
        [jax.ShapeDtypeStruct((t_len, d), F32)],
        [pl.BlockSpec((tm, d), lambda i, k: (i, 0))],
        epilogue, nk=N_CHIPS, acc_shapes=[(tm, d)], comm=comm)[0]


def _rms_stats(x):
    r = lax.rsqrt(jnp.mean(x * x, axis=-1, keepdims=True) + NORM_EPS)
    return r, x * r


def _accumulate(ref, part, first):
    @pl.when(first)
    def _():
        ref[...] = part

    @pl.when(jnp.logical_not(first))
    def _():
        ref[...] += part


def _rmsnorm_fwd(x, g):
    t_len, d = x.shape
    tm = _pick(t_len, (512,))

    def body(x_ref, g_ref, h_ref):
        _, xhat = _rms_stats(x_ref[...])
        h_ref[...] = (xhat * g_ref[...]).astype(BF16)

    row = pl.BlockSpec((tm, d), lambda i: (i, 0))
    vec = pl.BlockSpec((1, d), lambda i: (0, 0))
    return _call(body, name="rmsnorm_fwd", grid=(t_len // tm,), in_specs=[row, vec], out_specs=row,
                 out_shape=jax.ShapeDtypeStruct((t_len, d), BF16), compiler_params=_params(1))(x, g)


def _rmsnorm_bwd(dh, x, g, dres):
    t_len, d = x.shape
    tm = _pick(t_len, (256,))

    def body(dh_ref, x_ref, g_ref, dres_ref, dx_ref, dxb_ref, dg_ref):
        r, xhat = _rms_stats(x_ref[...])
        dh_v = dh_ref[...]
        gd = dh_v * g_ref[...]
        dx = dres_ref[...] + r * (gd - xhat * jnp.mean(gd * xhat, axis=-1, keepdims=True))
        dx_ref[...] = dx
        dxb_ref[...] = dx.astype(BF16)
        _accumulate(dg_ref, jnp.sum(dh_v * xhat, axis=0, keepdims=True), pl.program_id(0) == 0)

    row = pl.BlockSpec((tm, d), lambda i: (i, 0))
    vec = pl.BlockSpec((1, d), lambda i: (0, 0))
    return _call(body, name="rmsnorm_bwd", grid=(t_len // tm,), in_specs=[row, row, vec, row],
                 out_specs=[row, row, vec],
                 out_shape=[jax.ShapeDtypeStruct((t_len, d), F32), jax.ShapeDtypeStruct((t_len, d), BF16),
                            jax.ShapeDtypeStruct((1, d), F32)],
                 compiler_params=_params(1))(dh, x, g, dres)


def _loss_head(x, g, target):
    t_len, d = x.shape
    tm = _pick(t_len, (256,))

    def body(x_ref, g_ref, t_ref, dx_ref, dxb_ref, dg_ref, loss_ref):
        first = pl.program_id(0) == 0
        r, xhat = _rms_stats(x_ref[...])
        g_v = g_ref[...]
        err = xhat * g_v - t_ref[...]
        per_token = jnp.mean(err * err, axis=-1, keepdims=True)
        part = 0.5 * jnp.sum(per_token, axis=0, keepdims=True)
        _accumulate(loss_ref, jnp.broadcast_to(part, (1, LANES)), first)
        dy = err * (1.0 / d)
        _accumulate(dg_ref, jnp.sum(dy * xhat, axis=0, keepdims=True), first)
        gd = dy * g_v
        dx = r * (gd - xhat * jnp.mean(gd * xhat, axis=-1, keepdims=True))
        dx_ref[...] = dx
        dxb_ref[...] = dx.astype(BF16)

    row = pl.BlockSpec((tm, d), lambda i: (i, 0))
    vec = pl.BlockSpec((1, d), lambda i: (0, 0))
    return _call(body, name="loss_head", grid=(t_len // tm,), in_specs=[row, vec, row],
                 out_specs=[row, row, vec, pl.BlockSpec((1, LANES), lambda i: (0, 0))],
                 out_shape=[jax.ShapeDtypeStruct((t_len, d), F32), jax.ShapeDtypeStruct((t_len, d), BF16),
                            jax.ShapeDtypeStruct((1, d), F32), jax.ShapeDtypeStruct((1, LANES), F32)],
                 compiler_params=_params(1))(x, g, target)


def _ln_stats(x):
    mu = jnp.mean(x, axis=-1, keepdims=True)
    xc = x - mu
    r = lax.rsqrt(jnp.mean(xc * xc, axis=-1, keepdims=True) + NORM_EPS)
    return r, xc * r


def _ln_bwd(dy, r, xhat, g):
    dxh = dy * g
    return r * (dxh - jnp.mean(dxh, axis=-1, keepdims=True) - xhat * jnp.mean(dxh * xhat, axis=-1, keepdims=True))


def _rope_tables(positions):
    t_len = positions.shape[-1]
    inv_freq = 1.0 / (ROPE_THETA ** (jnp.arange(0, 2 * ROT_HALF, 2, dtype=F32) / (2 * ROT_HALF)))
    ang = positions.astype(F32).reshape(t_len, 1) * inv_freq
    cos = jnp.tile(jnp.cos(ang), (1, LANES // ROT_HALF))
    sin = jnp.tile(jnp.sin(ang), (1, LANES // ROT_HALF))
    lane = jnp.arange(LANES) % HEAD_DIM
    c = jnp.where(lane < 2 * ROT_HALF, cos, 1.0)
    s1 = jnp.where(lane < ROT_HALF, -sin, 0.0)
    s2 = jnp.where((lane >= ROT_HALF) & (lane < 2 * ROT_HALF), sin, 0.0)
    return c.astype(F32), s1.astype(F32), s2.astype(F32)


def _rope_fwd(p, tables):
    t_len = p.shape[0]
    tm = _pick(t_len, (256,))
    n_rot = K_END // LANES

    def body(p_ref, c_ref, s1_ref, s2_ref, o_ref):
        c, s1, s2 = c_ref[...], s1_ref[...], s2_ref[...]
        for j in range(V_END // LANES):
            sl = slice(j * LANES, (j + 1) * LANES)
            t = p_ref[:, sl]
            if j < n_rot:
                t = t * c + pltpu.roll(t, LANES - ROT_HALF, 1) * s1 + pltpu.roll(t, ROT_HALF, 1) * s2
            if j < Q_END // LANES:
                t = t * ATTN_SCALE
            o_ref[:, sl] = t.astype(BF16)

    tab = pl.BlockSpec((tm, LANES), lambda i: (i, 0))
    blk = pl.BlockSpec((tm, V_END), lambda i: (i, 0))
    return _call(body, name="rope_fwd", grid=(t_len // tm,), in_specs=[blk, tab, tab, tab], out_specs=blk,
                 out_shape=jax.ShapeDtypeStruct((t_len, V_END), BF16), compiler_params=_params(1))(p, *tables)


def _assemble_dp(dq, dkc, dkp, dvc, dvp, tables, dalin, dagate, du, dvin):
    t_len = dq.shape[0]
    steps = t_len // ATTN_STEP

    def body(dq_ref, dkc_ref, dkp_ref, dvc_ref, dvp_ref, c_ref, s1_ref, s2_ref, dalin_ref, dagate_ref,
             du_ref, dvin_ref, o_ref):
        keep = (pl.program_id(0) < steps - 1).astype(F32)
        for part in range(ATTN_STEP // BLK):
            rows = slice(part * BLK, (part + 1) * BLK)
            c, s1, s2 = c_ref[rows, :], s1_ref[rows, :], s2_ref[rows, :]

            def unrotate(dr):
                return dr * c + pltpu.roll(dr * s1, ROT_HALF, 1) + pltpu.roll(dr * s2, LANES - ROT_HALF, 1)

            for j in range(Q_END // LANES):
                sl = slice(j * LANES, (j + 1) * LANES)
                o_ref[rows, sl] = unrotate(dq_ref[rows, sl]).astype(BF16)
            for j in range((K_END - Q_END) // LANES):
                sl = slice(j * LANES, (j + 1) * LANES)
                dk, dv = dkc_ref[rows, sl], dvc_ref[rows, sl]
                if part == ATTN_STEP // BLK - 1:
                    dk = dk + keep * dkp_ref[:, sl]
                    dv = dv + keep * dvp_ref[:, sl]
                o_ref[rows, Q_END + j * LANES:Q_END + (j + 1) * LANES] = unrotate(dk).astype(BF16)
                o_ref[rows, K_END + j * LANES:K_END + (j + 1) * LANES] = dv.astype(BF16)
        o_ref[:, V_END:V_END + CONV_CH] = dalin_ref[...]
        o_ref[:, V_END + CONV_CH:CONV_END] = dagate_ref[...]
        o_ref[:, CONV_END:CONV_END + SGU_CH] = du_ref[...]
        o_ref[:, CONV_END + SGU_CH:IN_COLS] = dvin_ref[...]

    def cur(w):
        return pl.BlockSpec((ATTN_STEP, w), lambda i: (i, 0))

    def nxt(w):
        return pl.BlockSpec((BLK, w), lambda i: (jnp.minimum(i + 1, steps - 1), 0))

    kvw = K_END - Q_END
    return _call(body, name="assemble_dp", grid=(steps,),
                 in_specs=[cur(Q_END), cur(kvw), nxt(kvw), cur(kvw), nxt(kvw), cur(LANES), cur(LANES), cur(LANES),
                           cur(CONV_CH), cur(CONV_CH), cur(SGU_CH), cur(SGU_CH)],
                 out_specs=cur(IN_COLS), out_shape=jax.ShapeDtypeStruct((t_len, IN_COLS), BF16),
                 compiler_params=_params(1))(dq, dkc, dkp, dvc, dvp, *tables, dalin, dagate, du, dvin)


ATTN_STEP = 2 * BLK
ATTN_SCALE = HEAD_DIM ** -0.5
MASKED = -1e30


def _attn_bias():
    qi = jnp.arange(GQ * BLK)[None, :] % BLK
    kj = jnp.arange(2 * BLK)[:, None]
    dist = qi + BLK - kj
    band = (dist >= 0) & (dist < BLK)
    return jnp.stack([jnp.where(band & (kj >= BLK), 0.0, MASKED), jnp.where(band, 0.0, MASKED)]).astype(F32)


def _side_by_side(ref, cols):
    return jnp.concatenate([ref[h, :, cols] for h in range(GQ)], axis=1)


def _attn_chains(qt_ref, kp_ref, kc_ref):
    kc = kc_ref[...]
    pos_a, pos_b = slice(0, BLK), slice(BLK, ATTN_STEP)
    return [(pos_a, _side_by_side(qt_ref, pos_a), jnp.concatenate([kp_ref[...], kc[:BLK]], axis=0)),
            (pos_b, _side_by_side(qt_ref, pos_b), kc)]


def _attn_weights(qt, kk, bias, snk):
    s = lax.dot_general(kk, qt, NN, preferred_element_type=F32) + bias
    m = jnp.maximum(jnp.max(s, axis=0, keepdims=True), snk)
    e = jnp.exp(s - m)
    es = jnp.exp(snk - m)
    return e, es, 1.0 / (jnp.sum(e, axis=0, keepdims=True) + es)


def _attn_specs():
    before = lambda n: jnp.maximum(2 * n - 1, 0)
    return dict(
        qt=pl.BlockSpec((GQ, HEAD_DIM, ATTN_STEP), lambda g, n: (g, 0, n)),
        q=pl.BlockSpec((GQ, ATTN_STEP, HEAD_DIM), lambda g, n: (g, n, 0)),
        cur=pl.BlockSpec((None, ATTN_STEP, HEAD_DIM), lambda g, n: (g, n, 0)),
        prev=pl.BlockSpec((None, BLK, HEAD_DIM), lambda g, n: (g, before(n), 0)),
        cur_t=pl.BlockSpec((None, HEAD_DIM, ATTN_STEP), lambda g, n: (g, 0, n)),
        prev_t=pl.BlockSpec((None, HEAD_DIM, BLK), lambda g, n: (g, 0, before(n))),
        snk=pl.BlockSpec((None, 1, GQ * BLK), lambda g, n: (g, 0, 0)),
        bias_a=pl.BlockSpec((None, 2 * BLK, GQ * BLK), lambda g, n: (jnp.minimum(n, 1), 0, 0)),
        bias_b=pl.BlockSpec((None, 2 * BLK, GQ * BLK), lambda g, n: (1, 0, 0)))


def _attn_fwd(qt, k, vt, snk, comm=None):
    steps = qt.shape[2] // ATTN_STEP

    def body(qt_ref, kp_ref, kc_ref, vtp_ref, vtc_ref, snk_ref, ba_ref, bb_ref, o_ref):
        snk = snk_ref[...]
        vtc = vtc_ref[...]
        values = (jnp.concatenate([vtp_ref[...], vtc[:, :BLK]], axis=1), vtc)
        for (pos, qt_v, kk), vt_v, b_ref in zip(_attn_chains(qt_ref, kp_ref, kc_ref), values, (ba_ref, bb_ref)):
            e, _, inv = _attn_weights(qt_v, kk, b_ref[...], snk)
            o = lax.dot_general(vt_v, e.astype(BF16), NN, preferred_element_type=F32) * inv
            for h in range(GQ):
                o_ref[h, :, pos] = o[:, h * BLK:(h + 1) * BLK].astype(BF16)

    sp = _attn_specs()
    bias = _attn_bias()
    return _hosted_call(body, comm, "attn_fwd", (N_KV_HEADS, steps), [qt, k, k, vt, vt, snk, bias, bias],
                        [sp["qt"], sp["prev"], sp["cur"], sp["prev_t"], sp["cur_t"], sp["snk"], sp["bias_a"],
                         sp["bias_b"]], [jax.ShapeDtypeStruct(qt.shape, BF16)], [sp["qt"]])[0]


def _attn_bwd(qt, q, k, kt, v, snk, dot_, do, comm=None):
    steps = qt.shape[2] // ATTN_STEP

    def body(qt_ref, q_ref, kp_ref, kc_ref, ktp_ref, ktc_ref, vp_ref, vc_ref, snk_ref, ba_ref, bb_ref, dot_ref,
             do_ref, dq_ref, dkp_ref, dkc_ref, dvp_ref, dvc_ref, dsnk_ref):
        snk = snk_ref[...]
        vc, ktc = vc_ref[...], ktc_ref[...]
        values = (jnp.concatenate([vp_ref[...], vc[:BLK]], axis=0), vc)
        keys_t = (jnp.concatenate([ktp_ref[...], ktc[:, :BLK]], axis=1), ktc)
        row = lax.broadcasted_iota(jnp.int32, (SUBLANES, LANES), 0)
        tile = jnp.zeros((SUBLANES, LANES), F32)
        grads = []
        for (pos, qt_v, kk), vv, kt_v, b_ref in zip(_attn_chains(qt_ref, kp_ref, kc_ref), values, keys_t,
                                                    (ba_ref, bb_ref)):
            e, es, inv = _attn_weights(qt_v, kk, b_ref[...], snk)
            p = e * inv
            dp = lax.dot_general(vv, _side_by_side(dot_ref, pos), NN, preferred_element_type=F32)
            delta = jnp.sum(p * dp, axis=0, keepdims=True)
            ds = (p * (dp - delta)).astype(BF16)
            dq = lax.dot_general(kt_v, ds, NN, preferred_element_type=F32) * ATTN_SCALE
            for h in range(GQ):
                dq_ref[h, :, pos] = dq[:, h * BLK:(h + 1) * BLK]
            q_v = q_ref[:, pos, :].reshape(GQ * BLK, HEAD_DIM)
            do_v = do_ref[:, pos, :].reshape(GQ * BLK, HEAD_DIM)
            grads.append((lax.dot_general(ds, q_v, NN, preferred_element_type=F32),
                          lax.dot_general(p.astype(BF16), do_v, NN, preferred_element_type=F32)))
            per_query = -(es * inv) * delta
            for hh in range(GQ):
                tot = jnp.sum(per_query[:, hh * BLK:(hh + 1) * BLK], axis=1, keepdims=True)
                tile = tile + jnp.where(row == hh, tot, 0.0)
        (dk_a, dv_a), (dk_b, dv_b) = grads
        dkp_ref[...] = dk_a[:BLK]
        dvp_ref[...] = dv_a[:BLK]
        dkc_ref[0:BLK, :] = dk_a[BLK:] + dk_b[:BLK]
        dvc_ref[0:BLK, :] = dv_a[BLK:] + dv_b[:BLK]
        dkc_ref[BLK:ATTN_STEP, :] = dk_b[BLK:]
        dvc_ref[BLK:ATTN_STEP, :] = dv_b[BLK:]
        _accumulate(dsnk_ref, tile, pl.program_id(1) == 0)

    sp = _attn_specs()
    bias = _attn_bias()
    step_blk = pl.BlockSpec((None, BLK, HEAD_DIM), lambda g, n: (g, n, 0))
    kv_shape = jax.ShapeDtypeStruct(k.shape, F32)
    prev_shape = jax.ShapeDtypeStruct((N_KV_HEADS, steps * BLK, HEAD_DIM), F32)
    return _hosted_call(
        body, comm, "attn_bwd", (N_KV_HEADS, steps), [qt, q, k, k, kt, kt, v, v, snk, bias, bias, dot_, do],
        [sp["qt"], sp["q"], sp["prev"], sp["cur"], sp["prev_t"], sp["cur_t"], sp["prev"], sp["cur"], sp["snk"],
         sp["bias_a"], sp["bias_b"], sp["qt"], sp["q"]],
        [jax.ShapeDtypeStruct(qt.shape, F32), prev_shape, kv_shape, prev_shape, kv_shape,
         jax.ShapeDtypeStruct((N_KV_HEADS, SUBLANES, LANES), F32)],
        [sp["qt"], step_blk, sp["cur"], step_blk, sp["cur"],
         pl.BlockSpec((None, SUBLANES, LANES), lambda g, n: (g, 0, 0))])


CONV_CHUNK = 256


def _shift_up(win, s):
    n = win.shape[0]
    return win if s == 0 else pltpu.roll(win, n - s, 0)


def _conv_col_specs(t_len):
    lin = pl.BlockSpec((t_len, LANES), lambda j: (0, V_END // LANES + j))
    gate = pl.BlockSpec((t_len, LANES), lambda j: (0, (V_END + CONV_CH) // LANES + j))
    col = pl.BlockSpec((t_len, LANES), lambda j: (0, j))
    wsp = pl.BlockSpec((CONV_PAD, LANES), lambda j: (0, j))
    return lin, gate, col, wsp


def _conv_fwd(p, w, b):
    t_len = p.shape[0]
    ch = CONV_CHUNK

    def body(lin_ref, gate_ref, w_ref, b_ref, y_ref, hp_ref):
        hp_ref[0:CONV_PAD, :] = jnp.zeros((CONV_PAD, LANES), F32)

        def fill(c, carry):
            r0 = pl.multiple_of(c * ch, ch)
            hp_ref[pl.ds(r0 + CONV_PAD, ch), :] = lin_ref[pl.ds(r0, ch), :] * _sigmoid(gate_ref[pl.ds(r0, ch), :])
            return carry

        lax.fori_loop(0, t_len // ch, fill, 0)

        def conv(c, carry):
            r0 = pl.multiple_of(c * ch, ch)
            win = hp_ref[pl.ds(r0, ch + CONV_PAD), :]
            acc = jnp.zeros((ch, LANES), F32)
            for k in range(CONV_WIDTH):
                acc = acc + _shift_up(win, CONV_PAD - (CONV_WIDTH - 1) + k)[:ch] * w_ref[k:k + 1, :]
            y_ref[pl.ds(r0, ch), :] = acc + b_ref[...]
            return carry

        lax.fori_loop(0, t_len // ch, conv, 0)

    lin, gate, col, wsp = _conv_col_specs(t_len)
    return _call(body, name="conv_fwd", grid=(CONV_CH // LANES,),
                 in_specs=[lin, gate, wsp, pl.BlockSpec((1, LANES), lambda j: (0, j))], out_specs=col,
                 out_shape=jax.ShapeDtypeStruct((t_len, CONV_CH), F32),
                 scratch_shapes=[pltpu.VMEM((t_len + CONV_PAD, LANES), F32)],
                 compiler_params=_params(1))(p, p, w, b)


def _conv_post_fwd(y, g, b):
    t_len = y.shape[0]
    tm = _pick(t_len, (512,))

    def body(y_ref, g_ref, b_ref, o_ref):
        _, xhat = _ln_stats(y_ref[...])
        z = xhat * g_ref[...] + b_ref[...]
        o_ref[...] = (z * _sigmoid(z)).astype(BF16)

    row = pl.BlockSpec((tm, CONV_CH), lambda i: (i, 0))
    vec = pl.BlockSpec((1, CONV_CH), lambda i: (0, 0))
    return _call(body, name="conv_post_fwd", grid=(t_len // tm,), in_specs=[row, vec, vec], out_specs=row,
                 out_shape=jax.ShapeDtypeStruct((t_len, CONV_CH), BF16), compiler_params=_params(1))(y, g, b)


def _conv_post_bwd(dmix, y, g, b):
    t_len = y.shape[0]
    tm = _pick(t_len, (512,))

    def body(do_ref, y_ref, g_ref, b_ref, dy_ref, dg_ref, db_ref, dcb_ref):
        first = pl.program_id(0) == 0
        r, xhat = _ln_stats(y_ref[...])
        g_v = g_ref[...]
        z = xhat * g_v + b_ref[...]
        sig = _sigmoid(z)
        dz = do_ref[...] * (sig * (1.0 + z * (1.0 - sig)))
        _accumulate(db_ref, jnp.sum(dz, axis=0, keepdims=True), first)
        _accumulate(dg_ref, jnp.sum(dz * xhat, axis=0, keepdims=True), first)
        dy = _ln_bwd(dz, r, xhat, g_v)
        dy_ref[...] = dy
        _accumulate(dcb_ref, jnp.sum(dy, axis=0, keepdims=True), first)

    row = pl.BlockSpec((tm, CONV_CH), lambda i: (i, 0))
    do_spec = pl.BlockSpec((tm, CONV_CH), lambda i: (i, Q_END // CONV_CH))
    vec = pl.BlockSpec((1, CONV_CH), lambda i: (0, 0))
    vshape = jax.ShapeDtypeStruct((1, CONV_CH), F32)
    return _call(body, name="conv_post_bwd", grid=(t_len // tm,), in_specs=[do_spec, row, vec, vec],
                 out_specs=[row, vec, vec, vec],
                 out_shape=[jax.ShapeDtypeStruct((t_len, CONV_CH), F32), vshape, vshape, vshape],
                 compiler_params=_params(1))(dmix, y, g, b)


def _conv_bwd(p, dy, w):
    t_len = p.shape[0]
    ch = CONV_CHUNK

    def body(lin_ref, gate_ref, dy_ref, w_ref, dlin_ref, dgate_ref, dw_ref, hp_ref, dyp_ref):
        hp_ref[0:CONV_PAD, :] = jnp.zeros((CONV_PAD, LANES), F32)
        dyp_ref[t_len:t_len + CONV_PAD, :] = jnp.zeros((CONV_PAD, LANES), F32)
        dw_ref[...] = jnp.zeros((CONV_PAD, LANES), F32)

        def fill(c, carry):
            r0 = pl.multiple_of(c * ch, ch)
            hp_ref[pl.ds(r0 + CONV_PAD, ch), :] = lin_ref[pl.ds(r0, ch), :] * _sigmoid(gate_ref[pl.ds(r0, ch), :])
            dyp_ref[pl.ds(r0, ch), :] = dy_ref[pl.ds(r0, ch), :]
            return carry

        lax.fori_loop(0, t_len // ch, fill, 0)

        def step(c, carry):
            r0 = pl.multiple_of(c * ch, ch)
            win_h = hp_ref[pl.ds(r0, ch + CONV_PAD), :]
            win_dy = dyp_ref[pl.ds(r0, ch + CONV_PAD), :]
            dyc = win_dy[:ch]
            dh = jnp.zeros((ch, LANES), F32)
            for k in range(CONV_WIDTH):
                tap = _shift_up(win_h, CONV_PAD - (CONV_WIDTH - 1) + k)[:ch]
                dw_ref[k:k + 1, :] += jnp.sum(dyc * tap, axis=0, keepdims=True)
                dh = dh + _shift_up(win_dy, CONV_WIDTH - 1 - k)[:ch] * w_ref[k:k + 1, :]
            lin = lin_ref[pl.ds(r0, ch), :]
            sig = _sigmoid(gate_ref[pl.ds(r0, ch), :])
            dlin_ref[pl.ds(r0, ch), :] = (dh * sig).astype(BF16)
            dgate_ref[pl.ds(r0, ch), :] = (dh * lin * (sig * (1.0 - sig))).astype(BF16)
            return carry

        lax.fori_loop(0, t_len // ch, step, 0)

    lin, gate, col, wsp = _conv_col_specs(t_len)
    half = jax.ShapeDtypeStruct((t_len, CONV_CH), BF16)
    return _call(body, name="conv_bwd", grid=(CONV_CH // LANES,), in_specs=[lin, gate, col, wsp],
                 out_specs=[col, col, wsp],
                 out_shape=[half, half, jax.ShapeDtypeStruct((CONV_PAD, CONV_CH), F32)],
                 scratch_shapes=[pltpu.VMEM((t_len + CONV_PAD, LANES), F32), pltpu.VMEM((t_len + CONV_PAD, LANES), F32)],
                 compiler_params=_params(1))(p, p, dy, w)


def _sgu_mixed(v, w_ref, bt_ref, j):
    lane = lax.broadcasted_iota(jnp.int32, (BLK, LANES), 1)
    lo = lane < HEAD_DIM
    tri = lax.broadcasted_iota(jnp.int32, (BLK, BLK), 0) >= lax.broadcasted_iota(jnp.int32, (BLK, BLK), 1)
    vs = v[:, j * LANES:(j + 1) * LANES]
    v_lo = jnp.where(lo, vs, 0.0).astype(BF16)
    v_hi = jnp.where(lo, 0.0, vs).astype(BF16)
    w_lo = jnp.where(tri, w_ref[2 * j], 0.0).astype(BF16)
    w_hi = jnp.where(tri, w_ref[2 * j + 1], 0.0).astype(BF16)
    m = (lax.dot_general(w_lo, v_lo, NN, preferred_element_type=F32)
         + lax.dot_general(w_hi, v_hi, NN, preferred_element_type=F32))
    bias = jnp.where(lo, bt_ref[:, 2 * j:2 * j + 1], bt_ref[:, 2 * j + 1:2 * j + 2])
    return m + bias, (v_lo, v_hi, w_lo, w_hi, lo, tri)


def _sgu_specs():
    u_spec = pl.BlockSpec((BLK, SGU_CH), lambda i: (i, CONV_END // SGU_CH))
    v_spec = pl.BlockSpec((BLK, SGU_CH), lambda i: (i, CONV_END // SGU_CH + 1))
    vec = pl.BlockSpec((1, SGU_CH), lambda i: (0, 0))
    w_spec = pl.BlockSpec((SGU_HEADS, BLK, BLK), lambda i: (0, 0, 0))
    bt_spec = pl.BlockSpec((BLK, SGU_HEADS), lambda i: (0, 0))
    row = pl.BlockSpec((BLK, SGU_CH), lambda i: (i, 0))
    return u_spec, v_spec, vec, w_spec, bt_spec, row


def _sgu_fwd(p, g, b, w, bt):
    t_len = p.shape[0]

    def body(u_ref, vin_ref, g_ref, b_ref, w_ref, bt_ref, o_ref):
        _, xhat = _ln_stats(vin_ref[...])
        v = xhat * g_ref[...] + b_ref[...]
        for j in range(SGU_CH // LANES):
            m, _ = _sgu_mixed(v, w_ref, bt_ref, j)
            sl = slice(j * LANES, (j + 1) * LANES)
            o_ref[:, sl] = (u_ref[:, sl] * m).astype(BF16)

    u_spec, v_spec, vec, w_spec, bt_spec, row = _sgu_specs()
    return _call(body, name="sgu_fwd", grid=(t_len // BLK,), in_specs=[u_spec, v_spec, vec, vec, w_spec, bt_spec],
                 out_specs=row, out_shape=jax.ShapeDtypeStruct((t_len, SGU_CH), BF16),
                 compiler_params=_params(1))(p, p, g, b, w, bt)


def _sgu_bwd(p, dmix, g, b, w, bt):
    t_len = p.shape[0]

    def body(u_ref, vin_ref, do_ref, g_ref, b_ref, w_ref, bt_ref, du_ref, dvin_ref, dw_ref, dbt_ref, dg_ref,
             db_ref, dv_ref):
        first = pl.program_id(0) == 0
        r, xhat = _ln_stats(vin_ref[...])
        g_v = g_ref[...]
        v = xhat * g_v + b_ref[...]
        lane = lax.broadcasted_iota(jnp.int32, (BLK, LANES), 1)
        dbt = jnp.zeros((BLK, LANES), F32)

        @pl.when(first)
        def _():
            dw_ref[...] = jnp.zeros((SGU_HEADS, BLK, BLK), F32)

        for j in range(SGU_CH // LANES):
            m, (v_lo, v_hi, w_lo, w_hi, lo, tri) = _sgu_mixed(v, w_ref, bt_ref, j)
            sl = slice(j * LANES, (j + 1) * LANES)
            do_v = do_ref[:, sl]
            du_ref[:, sl] = (do_v * m).astype(BF16)
            dm = do_v * u_ref[:, sl]
            dm_lo = jnp.where(lo, dm, 0.0)
            dm_hi = jnp.where(lo, 0.0, dm)
            dbt = dbt + jnp.where(lane == 2 * j, jnp.sum(dm_lo, axis=-1, keepdims=True), 0.0)
            dbt = dbt + jnp.where(lane == 2 * j + 1, jnp.sum(dm_hi, axis=-1, keepdims=True), 0.0)
            dm_lo, dm_hi = dm_lo.astype(BF16), dm_hi.astype(BF16)
            dw_ref[2 * j] += jnp.where(tri, lax.dot_general(dm_lo, v_lo, NT, preferred_element_type=F32), 0.0)
            dw_ref[2 * j + 1] += jnp.where(tri, lax.dot_general(dm_hi, v_hi, NT, preferred_element_type=F32), 0.0)
            dv_ref[:, sl] = (lax.dot_general(w_lo, dm_lo, TN, preferred_element_type=F32)
                             + lax.dot_general(w_hi, dm_hi, TN, preferred_element_type=F32))
        _accumulate(dbt_ref, dbt, first)
        dv = dv_ref[...]
        _accumulate(db_ref, jnp.sum(dv, axis=0, keepdims=True), first)
        _accumulate(dg_ref, jnp.sum(dv * xhat, axis=0, keepdims=True), first)
        dvin_ref[...] = _ln_bwd(dv, r, xhat, g_v).astype(BF16)

    u_spec, v_spec, vec, w_spec, bt_spec, row = _sgu_specs()
    do_spec = pl.BlockSpec((BLK, SGU_CH), lambda i: (i, (Q_END + CONV_CH) // SGU_CH))
    half = jax.ShapeDtypeStruct((t_len, SGU_CH), BF16)
    vshape = jax.ShapeDtypeStruct((1, SGU_CH), F32)
    return _call(body, name="sgu_bwd", grid=(t_len // BLK,),
                 in_specs=[u_spec, v_spec, do_spec, vec, vec, w_spec, bt_spec],
                 out_specs=[row, row, w_spec, pl.BlockSpec((BLK, LANES), lambda i: (0, 0)), vec, vec],
                 out_shape=[half, half, jax.ShapeDtypeStruct((SGU_HEADS, BLK, BLK), F32),
                            jax.ShapeDtypeStruct((BLK, LANES), F32), vshape, vshape],
                 scratch_shapes=[pltpu.VMEM((BLK, SGU_CH), F32)],
                 compiler_params=_params(1))(p, p, dmix, g, b, w, bt)


def _place():
    x, y, c = lax.axis_index("x"), lax.axis_index("y"), lax.axis_index("c")
    chips = [(1 - x, y), (x, 1 - y), (1 - x, 1 - y)]
    return x, y, c, chips


def _hbm_specs(n):
    return [pl.BlockSpec(memory_space=pltpu.HBM)] * n


def _comm_params():
    return pltpu.CompilerParams(has_side_effects=True)


def _remote(src, dst, send_sem, recv_sem, to):
    return pltpu.make_async_remote_copy(src_ref=src, dst_ref=dst, send_sem=send_sem, recv_sem=recv_sem,
                                        device_id=to, device_id_type=MESH_ID)


def _cast_place(w_local, chip):
    n, rows, cols = w_local.shape

    def body(chip_ref, w_ref, o_ref):
        o_ref[...] = w_ref[...].astype(BF16)

    grid_spec = pltpu.PrefetchScalarGridSpec(
        num_scalar_prefetch=1, grid=(n, rows // ROW_TILE),
        in_specs=[pl.BlockSpec((None, ROW_TILE, cols), lambda l, i, ch: (l, i, 0))],
        out_specs=pl.BlockSpec((None, None, ROW_TILE, cols), lambda l, i, ch: (l, ch[0], i, 0)))
    return _call(body, name="cast_place", grid_spec=grid_spec,
                 out_shape=jax.ShapeDtypeStruct((n, N_CHIPS, rows, cols), BF16), compiler_params=_params(2))(chip, w_local)


def _all_gather_weights(placed, shards):
    n_placed, nt = len(placed), len(placed) + len(shards)

    def body(*refs):
        ins, outs = refs[:nt], refs[nt:2 * nt]
        ici_send, ici_recv, d2d_send, d2d_recv, local_sem = refs[2 * nt:]
        x, y, c, chips = _place()
        me = 2 * x + y
        sibling = (x, y, 1 - c)
        local = [pltpu.make_async_copy(ins[t].at[l], outs[t].at[l, me], local_sem.at[2 * (t - n_placed) + l])
                 for t in range(n_placed, nt) for l in range(2)]
        for cp in local:
            cp.start()
        sends = []
        for t in range(nt):
            src = outs[t].at[c, me] if t < n_placed else ins[t].at[c]
            for j, (px, py) in enumerate(chips):
                sends.append(_remote(src, outs[t].at[c, me], ici_send.at[3 * t + j], ici_recv.at[3 * t + j],
                                     (px, py, c)))
        for cp in sends:
            cp.start()
        for t in range(nt):
            for j, (px, py) in enumerate(chips):
                slab = outs[t].at[c, 2 * px + py]
                _remote(slab, slab, ici_send.at[3 * t + j], ici_recv.at[3 * t + j], (px, py, c)).wait_recv()
                fwd = _remote(slab, slab, d2d_send.at[3 * t + j], d2d_recv.at[3 * t + j], sibling)
                fwd.start()
                sends.append(fwd)
        for t in range(nt):
            for j, (px, py) in enumerate(chips):
                slab = outs[t].at[1 - c, 2 * px + py]
                _remote(slab, slab, d2d_send.at[3 * t + j], d2d_recv.at[3 * t + j], sibling).wait_recv()
        for cp in sends:
            cp.wait_send()
        for cp in local:
            cp.wait()

    out_shape = [jax.ShapeDtypeStruct(p.shape, p.dtype) for p in placed]
    out_shape += [jax.ShapeDtypeStruct((2, N_CHIPS) + s.shape[1:], s.dtype) for s in shards]
    sems = [pltpu.SemaphoreType.DMA((3 * nt,))] * 4 + [pltpu.SemaphoreType.DMA((2 * len(shards),))]
    return _call(body, name="all_gather_weights", in_specs=_hbm_specs(nt), out_specs=_hbm_specs(nt),
                 out_shape=out_shape, scratch_shapes=sems, input_output_aliases={t: t for t in range(n_placed)},
                 compiler_params=_comm_params())(*placed, *shards)


def _gather_comm(bufs, pieces):
    n = len(pieces)
    sems = [pltpu.SemaphoreType.DMA((3 * n,))] * 4

    def half(ref, layer, chip, r0, nr, which):
        return ref.at[layer, chip, pl.ds(pl.multiple_of(r0 + which * (nr // 2), SUBLANES), nr // 2)]

    def start(rd, al, fr, sm):
        ici_send, ici_recv, _, _ = sm
        x, y, c, chips = _place()
        for i, (t, layer, r0, nr) in enumerate(pieces):
            own = half(al[t], layer, 2 * x + y, r0, nr, c)
            for j, (px, py) in enumerate(chips):
                _remote(own, own, ici_send.at[3 * i + j], ici_recv.at[3 * i + j], (px, py, c)).start()

    def finish(rd, al, fr, sm):
        ici_send, ici_recv, d2d_send, d2d_recv = sm
        x, y, c, chips = _place()
        sibling = (x, y, 1 - c)
        passed = []
        for i, (t, layer, r0, nr) in enumerate(pieces):
            for j, (px, py) in enumerate(chips):
                got = half(al[t], layer, 2 * px + py, r0, nr, c)
                _remote(got, got, ici_send.at[3 * i + j], ici_recv.at[3 * i + j], (px, py, c)).wait_recv()
                fwd = _remote(got, got, d2d_send.at[3 * i + j], d2d_recv.at[3 * i + j], sibling)
                fwd.start()
                passed.append(fwd)
        for i, (t, layer, r0, nr) in enumerate(pieces):
            own = half(al[t], layer, 2 * x + y, r0, nr, c)
            for j, (px, py) in enumerate(chips):
                _remote(own, own, ici_send.at[3 * i + j], ici_recv.at[3 * i + j], (px, py, c)).wait_send()
                theirs = half(al[t], layer, 2 * px + py, r0, nr, 1 - c)
                _remote(theirs, theirs, d2d_send.at[3 * i + j], d2d_recv.at[3 * i + j], sibling).wait_recv()
        for fwd in passed:
            fwd.wait_send()

    return _Comm([], bufs, [], sems, start, finish)


def _own_rows(ref, c, which=0):
    hr = ref.shape[-2] // 2
    start = pl.multiple_of((c if which == 0 else 1 - c) * hr, SUBLANES)
    return ref.at[(slice(None),) * (len(ref.shape) - 2) + (pl.ds(start, hr),)]


def _to_owner_comm(stacks, layer):
    nt = len(stacks)
    sems = [pltpu.SemaphoreType.DMA((nt,))] * 2
    fresh = [jax.ShapeDtypeStruct((N_CHIPS, s.shape[2] // 2, s.shape[3]), s.dtype) for s in stacks]

    def copies(rd, fr, sm):
        x, y, c, _ = _place()
        return [_remote(_own_rows(rd[t].at[layer], c, 1), fr[t], sm[0].at[t], sm[1].at[t], (x, y, 1 - c))
                for t in range(nt)]

    def start(rd, al, fr, sm):
        for cp in copies(rd, fr, sm):
            cp.start()

    def finish(rd, al, fr, sm):
        for cp in copies(rd, fr, sm):
            cp.wait()

    return _Comm(stacks, [], fresh, sems, start, finish)


def _chip_comm(partials):
    nt = len(partials)
    sems = [pltpu.SemaphoreType.DMA((3 * nt,))] * 2
    fresh = [jax.ShapeDtypeStruct((3,) + p.shape[1:], p.dtype) for p in partials]

    def each(rd, fr, sm, act):
        x, y, c, chips = _place()
        for t in range(nt):
            for j, (px, py) in enumerate(chips):
                act(_remote(rd[t].at[2 * px + py], fr[t].at[j], sm[0].at[3 * t + j], sm[1].at[3 * t + j], (px, py, c)))

    def start(rd, al, fr, sm):
        each(rd, fr, sm, lambda cp: cp.start())

    def finish(rd, al, fr, sm):
        each(rd, fr, sm, lambda cp: cp.wait())

    return _Comm(partials, [], fresh, sems, start, finish)


def _from_owner_comm(finals, layer):
    nt = len(finals)
    sems = [pltpu.SemaphoreType.DMA((nt,))] * 2

    def start(rd, al, fr, sm):
        x, y, c, _ = _place()
        for t in range(nt):
            mine = _own_rows(al[t].at[layer], c)
            _remote(mine, mine, sm[0].at[t], sm[1].at[t], (x, y, 1 - c)).start()

    def finish(rd, al, fr, sm):
        x, y, c, _ = _place()
        for t in range(nt):
            mine, theirs = _own_rows(al[t].at[layer], c), _own_rows(al[t].at[layer], c, 1)
            _remote(mine, mine, sm[0].at[t], sm[1].at[t], (x, y, 1 - c)).wait_send()
            _remote(theirs, theirs, sm[0].at[t], sm[1].at[t], (x, y, 1 - c)).wait_recv()

    return _Comm([], finals, [], sems, start, finish)


def _all_reduce_small(buf):
    rows = buf.shape[0]

    def body(x_ref, out_ref, all_ref, send_sems, recv_sems, local_sem):
        x, y, c, chips = _place()
        me, sibling = (x, y, c), (x, y, 1 - c)

        def block(px, py, pc):
            return all_ref.at[pl.ds((4 * px + 2 * py + pc) * rows, rows), :]

        def copy(k, blk, to, src=None):
            return _remote(block(*blk) if src is None else src, block(*blk), send_sems.at[k], recv_sems.at[k], to)

        mine = pltpu.make_async_copy(x_ref, block(*me), local_sem)
        mine.start()
        first = [copy(0, me, sibling, src=x_ref)]
        first += [copy(1 + j, me, (*chip, c), src=x_ref) for j, chip in enumerate(chips)]
        for cp in first:
            cp.start()
        passed = [copy(4 + j, (*chip, c), sibling) for j, chip in enumerate(chips)]
        for j, chip in enumerate(chips):
            copy(1 + j, (*chip, c), me).wait_recv()
            passed[j].start()
        copy(0, sibling, me).wait_recv()
        for j, chip in enumerate(chips):
            copy(4 + j, (*chip, 1 - c), me).wait_recv()
        for cp in first + passed:
            cp.wait_send()
        mine.wait()
        tot = all_ref[0:rows, :]
        for k in range(1, N_DEV):
            tot = tot + all_ref[k * rows:(k + 1) * rows, :]
        out_ref[...] = tot

    vm = pl.BlockSpec(memory_space=pltpu.VMEM)
    return _call(body, name="all_reduce_small", in_specs=[vm], out_specs=vm,
                 out_shape=jax.ShapeDtypeStruct(buf.shape, F32),
                 scratch_shapes=[pltpu.VMEM((N_DEV * rows, LANES), F32), pltpu.SemaphoreType.DMA((7,)),
                                 pltpu.SemaphoreType.DMA((7,)), pltpu.SemaphoreType.DMA],
                 compiler_params=pltpu.CompilerParams(has_side_effects=True,
                                                      vmem_limit_bytes=V7X_VMEM_LIMIT_BYTES))(buf)


ROW_TILE = 128
ROW_STEP_BYTES = 12 * 2**20


def _row_tile(rows, bytes_per_row, align=2 * SUBLANES):
    cap = max(align, min(rows, ROW_STEP_BYTES // bytes_per_row) // align * align)
    for tr in range(cap, 0, -align):
        if rows % tr == 0:
            return tr
    raise ValueError(f"no row tile for {rows} rows")


def _chip_partial(stack, received, layer, place):
    _, half_rows, cols = received.shape
    tr = _row_tile(half_rows, 3 * 2 * cols)
    nh = half_rows // tr

    def body(place_ref, a_ref, b_ref, o_ref):
        o_ref[...] = (a_ref[...].astype(F32) + b_ref[...].astype(F32)).astype(BF16)

    blk = pl.BlockSpec((None, tr, cols), lambda s, i, pr: (s, i, 0))
    grid_spec = pltpu.PrefetchScalarGridSpec(
        num_scalar_prefetch=1, grid=(N_CHIPS, nh),
        in_specs=[pl.BlockSpec((None, None, tr, cols), lambda s, i, pr: (layer, s, pr[0] * nh + i, 0)), blk],
        out_specs=blk)
    return _call(body, name="chip_partial", grid_spec=grid_spec, out_shape=jax.ShapeDtypeStruct(received.shape, BF16),
                 compiler_params=_params(2))(place, stack, received)


def _final_sum(partial, from_chips, layer, place, finals):
    _, half_rows, cols = partial.shape
    tr = _row_tile(half_rows, (4 * 2 + 4) * cols)
    nh = half_rows // tr

    def body(place_ref, a_ref, r_ref, *rest):
        o_ref = rest[-1]
        tot = a_ref[...].astype(F32)
        for j in range(3):
            tot = tot + r_ref[j].astype(F32)
        o_ref[...] = tot

    in_specs = [pl.BlockSpec((None, tr, cols), lambda i, pr: (pr[1], i, 0)),
                pl.BlockSpec((3, tr, cols), lambda i, pr: (0, i, 0))]
    args = [place, partial, from_chips]
    kw = {}
    if finals is not None:
        in_specs.append(pl.BlockSpec(memory_space=pl.ANY))
        args.append(finals)
        kw["input_output_aliases"] = {3: 0}
    grid_spec = pltpu.PrefetchScalarGridSpec(
        num_scalar_prefetch=1, grid=(nh,), in_specs=in_specs,
        out_specs=pl.BlockSpec((None, tr, cols), lambda i, pr: (layer, pr[0] * nh + i, 0)))
    return _call(body, name="final_sum", grid_spec=grid_spec,
                 out_shape=jax.ShapeDtypeStruct((2, 2 * half_rows, cols), F32), compiler_params=_params(1), **kw)(*args)


def _adamw(w, g, m, v, comm=None):
    n, rows, cols = w.shape
    tr = _row_tile(rows, 8 * 4 * cols, SUBLANES)
    c1 = 1.0 - ADAM_B1 ** ADAM_STEP
    c2 = 1.0 - ADAM_B2 ** ADAM_STEP

    def body(w_ref, g_ref, m_ref, v_ref, d_ref, nm_ref, nv_ref, go_ref):
        g_v = g_ref[...]
        go_ref[...] = g_v
        nm = ADAM_B1 * m_ref[...] + (1.0 - ADAM_B1) * g_v
        nv = ADAM_B2 * v_ref[...] + (1.0 - ADAM_B2) * (g_v * g_v)
        nm_ref[...] = nm
        nv_ref[...] = nv
        d_ref[...] = -ADAM_LR * ((nm / c1) / (jnp.sqrt(nv / c2) + ADAM_EPS) + ADAM_WD * w_ref[...])

    blk = pl.BlockSpec((None, tr, cols), lambda l, i: (l, i, 0))
    shape = jax.ShapeDtypeStruct(w.shape, F32)
    return _hosted_call(body, comm, "adamw", (n, rows // tr), [w, g, m, v], [blk] * 4, [shape] * 4, [blk] * 4)


def _to_heads(a, n_heads, transposed=False):
    t_len = a.shape[0]
    return a.reshape(t_len, n_heads, HEAD_DIM).transpose((1, 2, 0) if transposed else (1, 0, 2))


def _from_heads(a, transposed=False):
    a = a.transpose((2, 0, 1) if transposed else (1, 0, 2))
    return a.reshape(a.shape[0], a.shape[1] * HEAD_DIM)


class _Schedule:
    def __init__(self):
        self.sites = {}
        self.open = []

    def add(self, site, make, done=None):
        self.sites.setdefault(site, []).append((make, done))

    def begin(self, site):
        self.open = [(make(), done) for make, done in self.sites.pop(site, [])]
        return _merge_comms([cm for cm, _ in self.open])

    def end(self):
        for cm, done in self.open:
            if done is not None:
                done(cm)
        self.open = []


def _ffn_fwd(x, gain, wg, which, layer, sched):
    w_in_name, w_out_name = f"ffn{which}_w_in", f"ffn{which}_w_out"
    h = _rmsnorm_fwd(x, gain)
    comm = sched.begin(("ffn_in", layer, which))
    gu, act = _mm_ffn_in(h, wg[w_in_name], layer, comm)
    sched.end()
    comm = sched.begin(("ffn_out", layer, which))
    x_new = _mm_out_res("ffn_out", act, wg[w_out_name], layer, x, FFN_RESIDUAL_WEIGHT, comm)[0]
    sched.end()
    return x_new, (x, h, gu, act)


def _ffn_bwd(dx, dxb, saved, gain, wg, which, layer, stacks, sched, weights_first=False):
    w_in_name, w_out_name = f"ffn{which}_w_in", f"ffn{which}_w_out"
    x, h, gu, act = saved
    out = {}

    def dact():
        comm = sched.begin(("ffn_dact", layer, which))
        out["dgu"] = _mm_dact_swiglu(dxb, wg[w_out_name], layer, gu, FFN_RESIDUAL_WEIGHT, comm)
        sched.end()

    def dw_out():
        comm = sched.begin(("ffn_dw_out", layer, which))
        stacks[w_out_name] = _dw_rows("ffn_dw_out", act, dxb, layer, stacks[w_out_name], FFN_RESIDUAL_WEIGHT, comm)
        sched.end()

    def dh():
        comm = sched.begin(("ffn_dh", layer, which))
        out["dh"] = _mm_dh_ffn(out["dgu"], wg[w_in_name], layer, comm)
        sched.end()

    def dw_in():
        comm = sched.begin(("ffn_dw_in", layer, which))
        stacks[w_in_name] = _dw_ffn_in(h, out["dgu"], layer, stacks[w_in_name], comm)
        sched.end()

    for step in ((dact, dw_in, dw_out, dh) if weights_first else (dact, dw_out, dh, dw_in)):
        step()
    return _rmsnorm_bwd(out["dh"], x, gain, dx)


def _mix_fwd(x, gain, wg, layer, small, tables, sched):
    h = _rmsnorm_fwd(x, gain)
    comm = sched.begin(("mix_in", layer))
    p = _mm_proj(h, wg["w_in"], layer, comm)[0]
    sched.end()
    qkv = _rope_fwd(p, tables)
    q_rows, k_rows, v_rows = qkv[:, :Q_END], qkv[:, Q_END:K_END], qkv[:, K_END:V_END]
    qt, q = _to_heads(q_rows, N_Q_HEADS, True), _to_heads(q_rows, N_Q_HEADS)
    kt, k = _to_heads(k_rows, N_KV_HEADS, True), _to_heads(k_rows, N_KV_HEADS)
    vt, v = _to_heads(v_rows, N_KV_HEADS, True), _to_heads(v_rows, N_KV_HEADS)
    comm = sched.begin(("attn", layer))
    attn = _from_heads(_attn_fwd(qt, k, vt, small["snk"], comm), True)
    sched.end()
    y = _conv_fwd(p, small["conv_w"], small["conv_b"])
    conv = _conv_post_fwd(y, small["conv_ln_g"], small["conv_ln_b"])
    sgu = _sgu_fwd(p, small["sgu_ln_g"], small["sgu_ln_b"], small["sgu_w"], small["sgu_bt"])
    mix = jnp.concatenate([attn, conv, sgu], axis=1)
    x_new = _mm_out_res("mix_out", mix, wg["w_out"], layer, x, 1.0)[0]
    return x_new, (x, h, p, (qt, q, k, kt, v), y, mix)


def _mix_bwd(dx, dxb, saved, gain, wg, small, tables, layer, stacks, sched):
    x, h, p, (qt, q, k, kt, v), y, mix = saved
    comm = sched.begin(("mix_dout", layer))
    dmix = _mm_dmix(dxb, wg["w_out"], layer, comm)
    sched.end()
    stacks["w_out"] = _dw_rows("mix_dw_out", mix, dxb, layer, stacks["w_out"], 1.0)
    do_rows = dmix[:, :Q_END].astype(BF16)
    comm = sched.begin(("attn_bwd", layer))
    dqt, dkp, dkc, dvp, dvc, dsnk = _attn_bwd(qt, q, k, kt, v, small["snk"], _to_heads(do_rows, N_Q_HEADS, True),
                                              _to_heads(do_rows, N_Q_HEADS), comm)
    sched.end()
    dy, d_ln_g, d_ln_b, d_conv_b = _conv_post_bwd(dmix, y, small["conv_ln_g"], small["conv_ln_b"])
    dalin, dagate, d_conv_w = _conv_bwd(p, dy, small["conv_w"])
    du, dvin, d_sgu_w, d_sgu_bt, d_sgu_g, d_sgu_b = _sgu_bwd(p, dmix, small["sgu_ln_g"], small["sgu_ln_b"],
                                                           small["sgu_w"], small["sgu_bt"])
    dp = _assemble_dp(_from_heads(dqt, True), _from_heads(dkc), _from_heads(dkp), _from_heads(dvc), _from_heads(dvp),
                      tables, dalin, dagate, du, dvin)
    comm = sched.begin(("mix_dw_in", layer))
    stacks["w_in"] = _dw_mix_in(h, dp, layer, stacks["w_in"], comm)
    sched.end()
    comm = sched.begin(("mix_dh", layer))
    dh = _mm_dh_mix(dp, wg["w_in"], layer, comm)
    sched.end()
    dx_in, dxb_in, dgain = _rmsnorm_bwd(dh, x, gain, dx)
    grads = {
        "norm_mix": dgain[0], "conv_dw_w": d_conv_w[:CONV_WIDTH], "conv_dw_b": d_conv_b[0],
        "conv_ln_g": d_ln_g[0], "conv_ln_b": d_ln_b[0], "sgu_ln_g": d_sgu_g[0], "sgu_ln_b": d_sgu_b[0],
        "sgu_w": d_sgu_w, "sgu_b": d_sgu_bt[:, :SGU_HEADS].T, "attn_sinks": dsnk[:, :GQ, 0].reshape(N_Q_HEADS),
    }
    return dx_in, dxb_in, grads


BIG = ("ffn1_w_in", "ffn1_w_out", "w_in", "w_out", "ffn2_w_in", "ffn2_w_out")
SMALL = ("norm_ffn1", "norm_mix", "conv_dw_w", "conv_dw_b", "conv_ln_g", "conv_ln_b", "sgu_ln_g", "sgu_ln_b",
         "sgu_w", "sgu_b", "attn_sinks", "norm_ffn2", "final_norm")
WEIGHTS = ("norm_ffn1", "ffn1_w_in", "ffn1_w_out", "norm_mix", "w_in", "conv_dw_w", "conv_dw_b", "conv_ln_g",
           "conv_ln_b", "sgu_ln_g", "sgu_ln_b", "sgu_w", "sgu_b", "attn_sinks", "w_out", "norm_ffn2", "ffn2_w_in",
           "ffn2_w_out", "final_norm")
PACK_ROWS = SUBLANES * LANES

FIRST_GATHER = [("ffn1_w_in", 0, None)]
FORWARD_PLAN = {
    ("ffn_in", 0, 1): [("ffn1_w_out", 0, None), ("w_in", 0, None)],
    ("ffn_out", 0, 1): [("ffn2_w_in", 0, 0)],
    ("mix_in", 0): [("w_out", 0, None)],
    ("attn", 0): [("ffn2_w_in", 0, 1)],
    ("ffn_in", 0, 2): [("ffn2_w_out", 0, None), ("ffn1_w_in", 1, 0)],
    ("ffn_out", 0, 2): [("ffn1_w_in", 1, 1)],
    ("ffn_in", 1, 1): [("ffn1_w_out", 1, None), ("w_in", 1, None)],
    ("ffn_out", 1, 1): [("ffn2_w_in", 1, 0)],
    ("mix_in", 1): [("w_out", 1, None)],
    ("attn", 1): [("ffn2_w_in", 1, 1)],
    ("ffn_in", 1, 2): [("ffn2_w_out", 1, None)],
}
SUBLAYER_WEIGHTS = {"ffn1": ["ffn1_w_out", "ffn1_w_in"], "ffn2": ["ffn2_w_out", "ffn2_w_in"], "mix": ["w_out", "w_in"]}


def _pack(arrays):
    flat = jnp.concatenate([a.reshape(-1).astype(F32) for a in arrays])
    pad = (-flat.shape[0]) % PACK_ROWS
    return jnp.pad(flat, (0, pad)).reshape(-1, LANES)


def _unpack(buf, shapes):
    flat = buf.reshape(-1)
    out, off = [], 0
    for s in shapes:
        n = 1
        for d in s:
            n *= d
        out.append(flat[off:off + n].reshape(s))
        off += n
    return out


def kernel(x, positions, norm_ffn1, ffn1_w_in, ffn1_w_out, norm_mix, w_in, conv_dw_w, conv_dw_b, conv_ln_g, conv_ln_b, sgu_ln_g, sgu_ln_b, sgu_w, sgu_b, attn_sinks, w_out, norm_ffn2, ffn2_w_in, ffn2_w_out, final_norm, loss_target, m_norm_ffn1, m_ffn1_w_in, m_ffn1_w_out, m_norm_mix, m_w_in, m_conv_dw_w, m_conv_dw_b, m_conv_ln_g, m_conv_ln_b, m_sgu_ln_g, m_sgu_ln_b, m_sgu_w, m_sgu_b, m_attn_sinks, m_w_out, m_norm_ffn2, m_ffn2_w_in, m_ffn2_w_out, m_final_norm, v_norm_ffn1, v_ffn1_w_in, v_ffn1_w_out, v_norm_mix, v_w_in, v_conv_dw_w, v_conv_dw_b, v_conv_ln_g, v_conv_ln_b, v_sgu_ln_g, v_sgu_ln_b, v_sgu_w, v_sgu_b, v_attn_sinks, v_w_out, v_norm_ffn2, v_ffn2_w_in, v_ffn2_w_out, v_final_norm):
    w = dict(norm_ffn1=norm_ffn1, ffn1_w_in=ffn1_w_in, ffn1_w_out=ffn1_w_out, norm_mix=norm_mix, w_in=w_in,
             conv_dw_w=conv_dw_w, conv_dw_b=conv_dw_b, conv_ln_g=conv_ln_g, conv_ln_b=conv_ln_b, sgu_ln_g=sgu_ln_g,
             sgu_ln_b=sgu_ln_b, sgu_w=sgu_w, sgu_b=sgu_b, attn_sinks=attn_sinks, w_out=w_out, norm_ffn2=norm_ffn2,
             ffn2_w_in=ffn2_w_in, ffn2_w_out=ffn2_w_out, final_norm=final_norm)
    m = dict(norm_ffn1=m_norm_ffn1, ffn1_w_in=m_ffn1_w_in, ffn1_w_out=m_ffn1_w_out, norm_mix=m_norm_mix, w_in=m_w_in,
             conv_dw_w=m_conv_dw_w, conv_dw_b=m_conv_dw_b, conv_ln_g=m_conv_ln_g, conv_ln_b=m_conv_ln_b,
             sgu_ln_g=m_sgu_ln_g, sgu_ln_b=m_sgu_ln_b, sgu_w=m_sgu_w, sgu_b=m_sgu_b, attn_sinks=m_attn_sinks,
             w_out=m_w_out, norm_ffn2=m_norm_ffn2, ffn2_w_in=m_ffn2_w_in, ffn2_w_out=m_ffn2_w_out,
             final_norm=m_final_norm)
    v = dict(norm_ffn1=v_norm_ffn1, ffn1_w_in=v_ffn1_w_in, ffn1_w_out=v_ffn1_w_out, norm_mix=v_norm_mix, w_in=v_w_in,
             conv_dw_w=v_conv_dw_w, conv_dw_b=v_conv_dw_b, conv_ln_g=v_conv_ln_g, conv_ln_b=v_conv_ln_b,
             sgu_ln_g=v_sgu_ln_g, sgu_ln_b=v_sgu_ln_b, sgu_w=v_sgu_w, sgu_b=v_sgu_b, attn_sinks=v_attn_sinks,
             w_out=v_w_out, norm_ffn2=v_norm_ffn2, ffn2_w_in=v_ffn2_w_in, ffn2_w_out=v_ffn2_w_out,
             final_norm=v_final_norm)
    depth = norm_ffn1.shape[0]
    assert depth == 2 and x.shape[0] == 1
    xc = lax.axis_index("x")
    yc = lax.axis_index("y")
    cc = lax.axis_index("c")
    chip = 2 * xc + yc

    chip_arr = chip.reshape(1).astype(jnp.int32)
    place = jnp.stack([cc, chip]).astype(jnp.int32)
    wg = {n: _cast_place(w[n], chip_arr) for n in BIG}
    sched = _Schedule()

    def gather(pieces):
        names = sorted({n for n, _, _ in pieces})
        half = w["ffn1_w_in"].shape[1] // 2

        def make():
            rows = lambda n, part: (0, wg[n].shape[2]) if part is None else (part * half, half)
            return _gather_comm([wg[n] for n in names], [(names.index(n), l, *rows(n, part)) for n, l, part in pieces])

        return make, lambda cm: wg.update(zip(names, cm.aliased_out))

    make, done = gather(FIRST_GATHER)
    first = make()
    _standalone("gather_first", first)
    done(first)
    for site, pieces in FORWARD_PLAN.items():
        sched.add(site, *gather(pieces))
    conv_w_full = _all_gather_weights([], [conv_dw_w])[0].transpose(0, 2, 1, 3).reshape(depth, CONV_WIDTH, CONV_CH)
    conv_w_full = jnp.pad(conv_w_full, ((0, 0), (0, CONV_PAD - CONV_WIDTH), (0, 0)))

    tables = _rope_tables(positions)
    small = []
    for l in range(depth):
        small.append(dict(
            snk=jnp.broadcast_to(attn_sinks[l].reshape(N_KV_HEADS, 1, GQ, 1), (N_KV_HEADS, 1, GQ, BLK)).reshape(
                N_KV_HEADS, 1, GQ * BLK),
            conv_w=conv_w_full[l], conv_b=conv_dw_b[l][None], conv_ln_g=conv_ln_g[l][None],
            conv_ln_b=conv_ln_b[l][None], sgu_ln_g=sgu_ln_g[l][None], sgu_ln_b=sgu_ln_b[l][None], sgu_w=sgu_w[l],
            sgu_bt=sgu_b[l].T))

    xs = x[0]
    saved = []
    for l in range(depth):
        xs, s1 = _ffn_fwd(xs, norm_ffn1[l][None], wg, 1, l, sched)
        xs, s2 = _mix_fwd(xs, norm_mix[l][None], wg, l, small[l], tables, sched)
        xs, s3 = _ffn_fwd(xs, norm_ffn2[l][None], wg, 2, l, sched)
        saved.append((s1, s2, s3))
    dx, dxb, d_final, loss_part = _loss_head(xs, final_norm[None], loss_target[0])

    stacks = {n: None for n in BIG}
    partials, from_chips = {}, {}
    finals = {n: None for n in BIG}

    def to_owner(layer, names):
        def done(cm):
            for n, received in zip(names, cm.fresh_out):
                partials[n, layer] = _chip_partial(stacks[n], received, layer, place)

        return lambda: _to_owner_comm([stacks[n] for n in names], layer), done

    def between_chips(layer, names, then_sum=()):
        def done(cm):
            from_chips.update({(n, layer): r for n, r in zip(names, cm.fresh_out)})
            for n in then_sum:
                finals[n] = _final_sum(partials[n, layer], from_chips[n, layer], layer, place, finals[n])

        return lambda: _chip_comm([partials[n, layer] for n in names]), done

    def from_owner(layer, names):
        return (lambda: _from_owner_comm([finals[n] for n in names], layer),
                lambda cm: finals.update(zip(names, cm.aliased_out)))

    order = [(l, kind) for l in reversed(range(depth)) for kind in ("ffn2", "mix", "ffn1")]
    for (layer, kind), (nxt_layer, nxt_kind) in zip(order[:-1], order[1:]):
        names = SUBLAYER_WEIGHTS[kind]
        if (nxt_layer, nxt_kind) == order[-1]:
            w_out_name, w_in_name = SUBLAYER_WEIGHTS[nxt_kind]
            which = int(nxt_kind[-1])
            sched.add(("ffn_dact", nxt_layer, which), *to_owner(layer, names))
            sched.add(("ffn_dw_in", nxt_layer, which), *between_chips(layer, names, names))
            sched.add(("ffn_dw_out", nxt_layer, which), *from_owner(layer, names))
            sched.add(("ffn_dw_out", nxt_layer, which), *to_owner(nxt_layer, [w_in_name]))
            sched.add(("ffn_dh", nxt_layer, which), *between_chips(nxt_layer, [w_in_name], [w_in_name]))
            sched.add(("ffn_dh", nxt_layer, which), *to_owner(nxt_layer, [w_out_name]))
        elif nxt_kind == "mix":
            after = order[order.index((nxt_layer, nxt_kind)) + 1]
            sched.add(("mix_dout", nxt_layer), *to_owner(layer, names))
            if after == order[-1]:
                which = int(after[1][-1])
                sched.add(("attn_bwd", nxt_layer), *between_chips(layer, names[1:], names[1:]))
                sched.add(("mix_dh", nxt_layer), *from_owner(layer, names[1:]))
                sched.add(("ffn_dw_in", after[0], which), *between_chips(layer, names[:1], names[:1]))
                sched.add(("ffn_dw_out", after[0], which), *from_owner(layer, names[:1]))
            else:
                sched.add(("attn_bwd", nxt_layer), *between_chips(layer, names[:1], names[:1]))
                sched.add(("mix_dh", nxt_layer), *from_owner(layer, names[:1]))
                which = int(after[1][-1])
                sched.add(("ffn_dh", after[0], which), *between_chips(layer, names[1:], names[1:]))
                sched.add(("ffn_dw_in", after[0], which), *from_owner(layer, names[1:]))
        else:
            which = int(nxt_kind[-1])
            sched.add(("ffn_dact", nxt_layer, which), *to_owner(layer, names))
            if kind == "mix":
                sched.add(("ffn_dw_out", nxt_layer, which), *between_chips(layer, names, names))
            else:
                sched.add(("ffn_dw_out", nxt_layer, which), *between_chips(layer, names[:1]))
                sched.add(("ffn_dh", nxt_layer, which), *between_chips(layer, names[1:], names))
            sched.add(("ffn_dw_in", nxt_layer, which), *from_owner(layer, names))

    small_grads = [None] * depth
    for l in reversed(range(depth)):
        s1, s2, s3 = saved[l]
        dx, dxb, dg2 = _ffn_bwd(dx, dxb, s3, norm_ffn2[l][None], wg, 2, l, stacks, sched)
        dx, dxb, gm = _mix_bwd(dx, dxb, s2, norm_mix[l][None], wg, small[l], tables, l, stacks, sched)
        dx, dxb, dg1 = _ffn_bwd(dx, dxb, s1, norm_ffn1[l][None], wg, 1, l, stacks, sched, weights_first=l == 0)
        gm["norm_ffn1"] = dg1[0]
        gm["norm_ffn2"] = dg2[0]
        small_grads[l] = gm
    grad_x = dx[None]
    assert not sched.sites, sched.sites

    per_layer = [n for n in SMALL if n != "final_norm"]
    small_local = [jnp.stack([small_grads[l][n] for l in range(depth)]) for n in per_layer]
    small_local += [d_final[0], loss_part[0, :1]]
    small_shapes = [a.shape for a in small_local]
    summed = _unpack(_all_reduce_small(_pack(small_local)), small_shapes)
    loss = summed[-1][0]
    sg = dict(zip(per_layer + ["final_norm"], summed[:-1]))
    sg["conv_dw_w"] = lax.dynamic_slice_in_dim(sg["conv_dw_w"], chip * LANES, LANES, axis=2)

    delta, new_m, new_v = {}, {}, {}
    shapes = [w[n].shape for n in SMALL]
    packed = [_pack([d[n] for n in SMALL])[None] for d in (w, sg, m, v)]
    last_layer, last_kind = order[-1]
    names = SUBLAYER_WEIGHTS[last_kind]
    make, done = between_chips(last_layer, names[:1], names[:1])
    cm = make()
    outs = _adamw(*packed, comm=cm)
    done(cm)
    for d, buf in zip((delta, new_m, new_v), outs[:3]):
        d.update(zip(SMALL, _unpack(buf[0], shapes)))
    make, done = from_owner(last_layer, names)
    cm = make()
    _standalone("from_owner", cm)
    done(cm)
    big_grads = dict(finals)
    for n in BIG:
        delta[n], new_m[n], new_v[n], big_grads[n] = _adamw(w[n], big_grads[n], m[n], v[n])
    grads = {**big_grads, **sg}
    return (loss, grad_x, *[grads[n] for n in WEIGHTS], *[delta[n] for n in WEIGHTS],
            *[new_m[n] for n in WEIGHTS], *[new_v[n] for n in WEIGHTS])
```

```python
import functools

import jax
import jax.numpy as jnp
from jax import lax
from jax.experimental import pallas as pl
from jax.experimental.pallas import tpu as pltpu

F32 = jnp.float32
BF16 = jnp.bfloat16
MESH_ID = pl.DeviceIdType.MESH

V7X_VMEM_LIMIT_BYTES = 56 * 2**20
LANES = 128
SUBLANES = 8

HEAD_DIM = 64
N_Q_HEADS = 16
N_KV_HEADS = 4
GQ = N_Q_HEADS // N_KV_HEADS
BLK = 128
ROT_HALF = 8
ROPE_THETA = 500000.0
CONV_WIDTH = 31
CONV_PAD = 32
CONV_CH = 512
SGU_CH = 512
SGU_HEADS = 8
Q_END = N_Q_HEADS * HEAD_DIM
K_END = Q_END + N_KV_HEADS * HEAD_DIM
V_END = K_END + N_KV_HEADS * HEAD_DIM
CONV_END = V_END + 2 * CONV_CH
IN_COLS = CONV_END + 2 * SGU_CH
NORM_EPS = 1e-5
FFN_RESIDUAL_WEIGHT = 0.5
N_CHIPS = 4
N_DEV = 8

ADAM_LR = 0.001
ADAM_B1 = 0.9
ADAM_B2 = 0.999
ADAM_EPS = 1e-08
ADAM_WD = 0.01
ADAM_STEP = 10

NN = (((1,), (0,)), ((), ()))
NT = (((1,), (1,)), ((), ()))
TN = (((0,), (0,)), ((), ()))


def _pick(n, cands):
    for c in cands:
        if n % c == 0:
            return c
    raise ValueError(f"no tile of {cands} divides {n}")


def _params(n_axes):
    return pltpu.CompilerParams(dimension_semantics=("arbitrary",) * n_axes, vmem_limit_bytes=V7X_VMEM_LIMIT_BYTES)


def _call(body, **kw):
    return pl.pallas_call(body, **kw)


def _sigmoid(x):
    return 1.0 / (1.0 + jnp.exp(-x))


class _Comm:
    def __init__(self, reads, aliased, fresh, sems, start, finish):
        self.reads, self.aliased, self.fresh, self.sems = list(reads), list(aliased), list(fresh), list(sems)
        self.start, self.finish = start, finish
        self.aliased_out, self.fresh_out = None, None


def _merge_comms(comms):
    comms = [cm for cm in comms if cm is not None]
    if not comms:
        return None
    if len(comms) == 1:
        return comms[0]

    def split(refs, counts):
        out, off = [], 0
        for n in counts:
            out.append(refs[off:off + n])
            off += n
        return out

    def run(which):
        def f(rd, al, fr, sm):
            parts = zip(split(rd, [len(cm.reads) for cm in comms]), split(al, [len(cm.aliased) for cm in comms]),
                        split(fr, [len(cm.fresh) for cm in comms]), split(sm, [len(cm.sems) for cm in comms]))
            for cm, (r, a, f_, s) in zip(comms, parts):
                getattr(cm, which)(r, a, f_, s)
        return f

    merged = _Comm(sum((cm.reads for cm in comms), []), sum((cm.aliased for cm in comms), []),
                   sum((cm.fresh for cm in comms), []), sum((cm.sems for cm in comms), []), run("start"), run("finish"))
    merged.parts = comms
    return merged


def _hosted_call(body, comm, name, grid, inputs, in_specs, out_shape, out_specs, scratch_shapes=(), aliases=None):
    n_in, n_out, n_scr = len(inputs), len(out_shape), len(scratch_shapes)
    aliases = dict(aliases or {})
    if comm is None:
        return _call(body, name=name, grid=grid, in_specs=list(in_specs), out_specs=list(out_specs),
                     out_shape=list(out_shape), scratch_shapes=list(scratch_shapes), input_output_aliases=aliases,
                     compiler_params=_params(len(grid)))(*inputs)
    nr, na, nf = len(comm.reads), len(comm.aliased), len(comm.fresh)

    def full(*refs):
        ins = refs[:n_in]
        rd = refs[n_in:n_in + nr]
        pos = n_in + nr + na
        outs = refs[pos:pos + n_out]
        al = refs[pos + n_out:pos + n_out + na]
        fr = refs[pos + n_out + na:pos + n_out + na + nf]
        pos = pos + n_out + na + nf
        scr = refs[pos:pos + n_scr]
        sems = refs[pos + n_scr:]
        first, last = None, None
        for axis, size in enumerate(grid):
            f, l = pl.program_id(axis) == 0, pl.program_id(axis) == size - 1
            first = f if first is None else jnp.logical_and(first, f)
            last = l if last is None else jnp.logical_and(last, l)

        @pl.when(first)
        def _():
            comm.start(rd, al, fr, sems)

        body(*ins, *outs, *scr)

        @pl.when(last)
        def _():
            comm.finish(rd, al, fr, sems)

    hbm = pl.BlockSpec(memory_space=pltpu.HBM)
    for i in range(na):
        aliases[n_in + nr + i] = n_out + i
    struct = [jax.ShapeDtypeStruct(a.shape, a.dtype) for a in comm.aliased]
    res = _call(full, name=name, grid=grid, in_specs=list(in_specs) + [hbm] * (nr + na),
                out_specs=list(out_specs) + [hbm] * (na + nf), out_shape=list(out_shape) + struct + comm.fresh,
                scratch_shapes=list(scratch_shapes) + comm.sems, input_output_aliases=aliases,
                compiler_params=pltpu.CompilerParams(dimension_semantics=("arbitrary",) * len(grid),
                                                     vmem_limit_bytes=V7X_VMEM_LIMIT_BYTES, has_side_effects=True),
                )(*inputs, *comm.reads, *comm.aliased)
    _deliver(comm, res[n_out:n_out + na], res[n_out + na:])
    return res[:n_out]


def _deliver(comm, aliased_out, fresh_out):
    comm.aliased_out, comm.fresh_out = list(aliased_out), list(fresh_out)
    off_a = off_f = 0
    for part in getattr(comm, "parts", []):
        _deliver(part, aliased_out[off_a:off_a + len(part.aliased)], fresh_out[off_f:off_f + len(part.fresh)])
        off_a += len(part.aliased)
        off_f += len(part.fresh)


def _standalone(name, comm):
    def body(*refs):
        nr, na, nf = len(comm.reads), len(comm.aliased), len(comm.fresh)
        rd, al, fr, sems = refs[:nr], refs[nr + na:nr + 2 * na], refs[nr + 2 * na:nr + 2 * na + nf], refs[nr + 2 * na + nf:]
        comm.start(rd, al, fr, sems)
        comm.finish(rd, al, fr, sems)

    nr, na, nf = len(comm.reads), len(comm.aliased), len(comm.fresh)
    struct = [jax.ShapeDtypeStruct(a.shape, a.dtype) for a in comm.aliased]
    res = _call(body, name=name, in_specs=_hbm_specs(nr + na), out_specs=_hbm_specs(na + nf),
                out_shape=struct + comm.fresh, scratch_shapes=comm.sems,
                input_output_aliases={nr + i: i for i in range(na)},
                compiler_params=_comm_params())(*comm.reads, *comm.aliased)
    _deliver(comm, res[:na], res[na:])


def _matmul(name, grid, a_ops, b_ops, terms, dims, out_shape, out_specs, epilogue, extra_ops=(), nk=1,
            acc_shapes=(), alias=None, comm=None, chunk=None):
    na, nb, ne, no = len(a_ops), len(b_ops), len(extra_ops), len(out_shape)

    def body(*refs):
        a = refs[:na]
        b = refs[na:na + nb]
        e = refs[na + nb:na + nb + ne]
        first_out = na + nb + ne + (1 if alias is not None else 0)
        o = refs[first_out:first_out + no]
        accs = refs[first_out + no:]

        def partial(t, cols=None):
            tot = None
            for ai, bi in t:
                rhs = b[bi][...] if cols is None else (b[bi][cols, :] if dims == NT else b[bi][:, cols])
                d = lax.dot_general(a[ai][...], rhs, dims, preferred_element_type=F32)
                tot = d if tot is None else tot + d
            return tot

        if nk == 1 and chunk is not None:
            width = b[0].shape[0] if dims == NT else b[0].shape[1]
            for c0 in range(0, width, chunk):
                cols = slice(c0, min(c0 + chunk, width))
                epilogue([partial(t, cols) for t in terms], e, o, cols)
        elif nk == 1:
            epilogue([partial(t) for t in terms], e, o)
        else:
            k = pl.program_id(len(grid) - 1)

            @pl.when(k == 0)
            def _():
                for acc in accs:
                    acc[...] = jnp.zeros(acc.shape, F32)

            for acc, t in zip(accs, terms):
                acc[...] += partial(t)

            @pl.when(k == nk - 1)
            def _():
                epilogue([acc[...] for acc in accs], e, o)

    ops = list(a_ops) + list(b_ops) + list(extra_ops)
    arrays = [x for x, _ in ops]
    in_specs = [s for _, s in ops]
    aliases = {}
    if alias is not None:
        arrays.append(alias[0])
        in_specs.append(pl.BlockSpec(memory_space=pl.ANY))
        aliases[len(arrays) - 1] = alias[1]
    scratch = [pltpu.VMEM(s, F32) for s in acc_shapes] if nk > 1 else []
    return _hosted_call(body, comm, name, grid, arrays, in_specs, out_shape, out_specs, scratch, aliases)


def _mm_ffn_in(h, w_g, layer, comm=None):
    t_len, d = h.shape
    fs = w_g.shape[3]
    f = 2 * fs
    tm = _pick(t_len, (2048, 1024, 512))
    tn = _pick(fs, (256, 128))
    nj = fs // tn

    def epilogue(accs, e, o):
        g, u = accs
        o[0][0] = g.astype(BF16)
        o[0][1] = u.astype(BF16)
        o[1][...] = (g * _sigmoid(g) * u).astype(BF16)

    return _matmul(
        "ffn_in", (t_len // tm, 2, nj),
        [(h, pl.BlockSpec((tm, d), lambda i, s, j: (i, 0)))],
        [(w_g, pl.BlockSpec((None, None, d, tn), lambda i, s, j: (layer, s, 0, j))),
         (w_g, pl.BlockSpec((None, None, d, tn), lambda i, s, j: (layer, s + 2, 0, j)))],
        [[(0, 0)], [(0, 1)]], NN,
        [jax.ShapeDtypeStruct((2, t_len, f), BF16), jax.ShapeDtypeStruct((t_len, f), BF16)],
        [pl.BlockSpec((2, tm, tn), lambda i, s, j: (0, i, s * nj + j)),
         pl.BlockSpec((tm, tn), lambda i, s, j: (i, s * nj + j))],
        epilogue, comm=comm)


def _mm_out_res(name, a, w_g, layer, x, scale, comm=None):
    t_len = a.shape[0]
    ks, n = w_g.shape[2], w_g.shape[3]
    tm = _pick(t_len, (512,))
    tn = _pick(n, (1024,))

    def epilogue(accs, e, o):
        o[0][...] = e[0][...] + scale * accs[0]

    return _matmul(
        name, (t_len // tm, n // tn),
        [(a, pl.BlockSpec((tm, ks), lambda i, j, s=s: (i, s))) for s in range(N_CHIPS)],
        [(w_g, pl.BlockSpec((None, None, ks, tn), lambda i, j, s=s: (layer, s, 0, j))) for s in range(N_CHIPS)],
        [[(s, s) for s in range(N_CHIPS)]], NN,
        [jax.ShapeDtypeStruct((t_len, n), F32)],
        [pl.BlockSpec((tm, tn), lambda i, j: (i, j))],
        epilogue, extra_ops=[(x, pl.BlockSpec((tm, tn), lambda i, j: (i, j)))], comm=comm)


def _mm_proj(h, w_g, layer, comm=None):
    t_len, d = h.shape
    cs = w_g.shape[3]
    tm = _pick(t_len, (1024, 512))

    def epilogue(accs, e, o):
        o[0][...] = accs[0]

    return _matmul(
        "mix_in", (t_len // tm, N_CHIPS),
        [(h, pl.BlockSpec((tm, d), lambda i, s: (i, 0)))],
        [(w_g, pl.BlockSpec((None, None, d, cs), lambda i, s: (layer, s, 0, 0)))],
        [[(0, 0)]], NN,
        [jax.ShapeDtypeStruct((t_len, N_CHIPS * cs), F32)],
        [pl.BlockSpec((tm, cs), lambda i, s: (i, s))],
        epilogue, comm=comm)


def _mm_dact_swiglu(dxb, w_g, layer, gu, scale, comm=None):
    t_len, d = dxb.shape
    rs = w_g.shape[2]
    tm = _pick(t_len, (512,))
    tn = _pick(rs, (1408, 256, 128))
    nj = rs // tn

    def epilogue(accs, e, o, cols):
        dact = scale * accs[0]
        g = e[0][0, :, cols].astype(F32)
        u = e[0][1, :, cols].astype(F32)
        sig = _sigmoid(g)
        o[0][0, :, cols] = (dact * u * (sig * (1.0 + g * (1.0 - sig)))).astype(BF16)
        o[0][1, :, cols] = (dact * (g * sig)).astype(BF16)

    gu_spec = pl.BlockSpec((2, tm, tn), lambda i, s, j: (0, i, s * nj + j))
    return _matmul(
        "ffn_dact", (t_len // tm, N_CHIPS, nj),
        [(dxb, pl.BlockSpec((tm, d), lambda i, s, j: (i, 0)))],
        [(w_g, pl.BlockSpec((None, None, tn, d), lambda i, s, j: (layer, s, j, 0)))],
        [[(0, 0)]], NT,
        [jax.ShapeDtypeStruct(gu.shape, BF16)], [gu_spec],
        epilogue, extra_ops=[(gu, gu_spec)], comm=comm, chunk=2 * LANES)[0]


def _mm_dh_ffn(dgu, w_g, layer, comm=None):
    t_len = dgu.shape[1]
    d, fs = w_g.shape[2], w_g.shape[3]
    tm = _pick(t_len, (1024, 512))
    tk = _pick(fs, (256, 128))
    nks = fs // tk
    nk = 2 * nks

    def epilogue(accs, e, o):
        o[0][...] = accs[0]

    return _matmul(
        "ffn_dh", (t_len // tm, nk),
        [(dgu, pl.BlockSpec((None, tm, tk), lambda i, k: (0, i, k))),
         (dgu, pl.BlockSpec((None, tm, tk), lambda i, k: (1, i, k)))],
        [(w_g, pl.BlockSpec((None, None, d, tk), lambda i, k: (layer, k // nks, 0, k % nks))),
         (w_g, pl.BlockSpec((None, None, d, tk), lambda i, k: (layer, k // nks + 2, 0, k % nks)))],
        [[(0, 0), (1, 1)]], NT,
        [jax.ShapeDtypeStruct((t_len, d), F32)],
        [pl.BlockSpec((tm, d), lambda i, k: (i, 0))],
        epilogue, nk=nk, acc_shapes=[(tm, d)], comm=comm)[0]


def _mm_dw(name, a, a_spec_of, b, b_spec_of, layer, stack, rows, cols, tn, scale, comm=None):
    t_len = a.shape[0]
    tt = _pick(t_len, (1024, 512))
    nj = cols // tn

    def epilogue(accs, e, o):
        o[0][...] = (scale * accs[0]).astype(BF16)

    shape = jax.ShapeDtypeStruct((2, N_CHIPS, rows, cols), BF16)
    return _matmul(
        name, (N_CHIPS, nj, t_len // tt),
        [(a, a_spec_of(tt))], [(b, b_spec_of(tt, tn, nj))],
        [[(0, 0)]], TN, [shape],
        [pl.BlockSpec((None, None, rows, tn), lambda s, j, t: (layer, s, 0, j))],
        epilogue, nk=t_len // tt, acc_shapes=[(rows, tn)],
        alias=None if stack is None else (stack, 0), comm=comm)[0]


def _dw_ffn_in(h, dgu, layer, stack, comm=None):
    d = h.shape[1]
    fs = dgu.shape[2] // 2
    tn = _pick(fs, (1408, 256))
    return _mm_dw(
        "ffn_dw_in", h, lambda tt: pl.BlockSpec((tt, d), lambda s, j, t: (t, 0)),
        dgu, lambda tt, tn_, nj: pl.BlockSpec((None, tt, tn_), lambda s, j, t: (s // 2, t, (s % 2) * nj + j)),
        layer, stack, d, fs, tn, 1.0, comm)


def _dw_rows(name, a, dxb, layer, stack, scale, comm=None):
    rs = a.shape[1] // N_CHIPS
    d = dxb.shape[1]
    tn = _pick(d, (1024,))
    return _mm_dw(
        name, a, lambda tt: pl.BlockSpec((tt, rs), lambda s, j, t: (t, s)),
        dxb, lambda tt, tn_, nj: pl.BlockSpec((tt, tn_), lambda s, j, t: (t, j)),
        layer, stack, rs, d, tn, scale, comm)


def _dw_mix_in(h, dp, layer, stack, comm=None):
    d = h.shape[1]
    cs = dp.shape[1] // N_CHIPS
    return _mm_dw(
        "mix_dw_in", h, lambda tt: pl.BlockSpec((tt, d), lambda s, j, t: (t, 0)),
        dp, lambda tt, tn_, nj: pl.BlockSpec((tt, tn_), lambda s, j, t: (t, s)),
        layer, stack, d, cs, cs, 1.0, comm)


def _mm_dmix(dxb, w_g, layer, comm=None):
    t_len, d = dxb.shape
    rs = w_g.shape[2]
    tm = _pick(t_len, (1024, 512))

    def epilogue(accs, e, o):
        o[0][...] = accs[0]

    return _matmul(
        "mix_dout", (t_len // tm, N_CHIPS),
        [(dxb, pl.BlockSpec((tm, d), lambda i, s: (i, 0)))],
        [(w_g, pl.BlockSpec((None, None, rs, d), lambda i, s: (layer, s, 0, 0)))],
        [[(0, 0)]], NT,
        [jax.ShapeDtypeStruct((t_len, N_CHIPS * rs), F32)],
        [pl.BlockSpec((tm, rs), lambda i, s: (i, s))],
        epilogue, comm=comm)[0]


def _mm_dh_mix(dp, w_g, layer, comm=None):
    t_len = dp.shape[0]
    d, cs = w_g.shape[2], w_g.shape[3]
    tm = _pick(t_len, (1024, 512))

    def epilogue(accs, e, o):
        o[0][...] = accs[0]

    return _matmul(
        "mix_dh", (t_len // tm, N_CHIPS),
        [(dp, pl.BlockSpec((tm, cs), lambda i, k: (i, k)))],
        [(w_g, pl.BlockSpec((None, None, d, cs), lambda i, k: (layer, k, 0, 0)))],
        [[(0, 0)]], NT,
        [jax.ShapeDtypeStruct((t_len, d), F32)],
        [pl.BlockSpec((tm, d), lambda i, k: (i, 0))],
        epilogue, nk=N_CHIPS, acc_shapes=[(tm, d)], comm=comm)[0]


def _rms_stats(x):
    r = lax.rsqrt(jnp.mean(x * x, axis=-1, keepdims=True) + NORM_EPS)
    return r, x * r


def _accumulate(ref, part, first):
    @pl.when(first)
    def _():
        ref[...] = part

    @pl.when(jnp.logical_not(first))
    def _():
        ref[...] += part


def _rmsnorm_fwd(x, g):
    t_len, d = x.shape
    tm = _pick(t_len, (512,))

    def body(x_ref, g_ref, h_ref):
        _, xhat = _rms_stats(x_ref[...])
        h_ref[...] = (xhat * g_ref[...]).astype(BF16)

    row = pl.BlockSpec((tm, d), lambda i: (i, 0))
    vec = pl.BlockSpec((1, d), lambda i: (0, 0))
    return _call(body, name="rmsnorm_fwd", grid=(t_len // tm,), in_specs=[row, vec], out_specs=row,
                 out_shape=jax.ShapeDtypeStruct((t_len, d), BF16), compiler_params=_params(1))(x, g)


def _rmsnorm_bwd(dh, x, g, dres):
    t_len, d = x.shape
    tm = _pick(t_len, (512,))

    def body(dh_ref, x_ref, g_ref, dres_ref, dx_ref, dxb_ref, dg_ref):
        r, xhat = _rms_stats(x_ref[...])
        dh_v = dh_ref[...]
        gd = dh_v * g_ref[...]
        dx = dres_ref[...] + r * (gd - xhat * jnp.mean(gd * xhat, axis=-1, keepdims=True))
        dx_ref[...] = dx
        dxb_ref[...] = dx.astype(BF16)
        _accumulate(dg_ref, jnp.sum(dh_v * xhat, axis=0, keepdims=True), pl.program_id(0) == 0)

    row = pl.BlockSpec((tm, d), lambda i: (i, 0))
    vec = pl.BlockSpec((1, d), lambda i: (0, 0))
    return _call(body, name="rmsnorm_bwd", grid=(t_len // tm,), in_specs=[row, row, vec, row],
                 out_specs=[row, row, vec],
                 out_shape=[jax.ShapeDtypeStruct((t_len, d), F32), jax.ShapeDtypeStruct((t_len, d), BF16),
                            jax.ShapeDtypeStruct((1, d), F32)],
                 compiler_params=_params(1))(dh, x, g, dres)


def _loss_head(x, g, target):
    t_len, d = x.shape
    tm = _pick(t_len, (512,))

    def body(x_ref, g_ref, t_ref, dx_ref, dxb_ref, dg_ref, loss_ref):
        first = pl.program_id(0) == 0
        r, xhat = _rms_stats(x_ref[...])
        g_v = g_ref[...]
        err = xhat * g_v - t_ref[...]
        per_token = jnp.mean(err * err, axis=-1, keepdims=True)
        part = 0.5 * jnp.sum(per_token, axis=0, keepdims=True)
        _accumulate(loss_ref, jnp.broadcast_to(part, (1, LANES)), first)
        dy = err * (1.0 / d)
        _accumulate(dg_ref, jnp.sum(dy * xhat, axis=0, keepdims=True), first)
        gd = dy * g_v
        dx = r * (gd - xhat * jnp.mean(gd * xhat, axis=-1, keepdims=True))
        dx_ref[...] = dx
        dxb_ref[...] = dx.astype(BF16)

    row = pl.BlockSpec((tm, d), lambda i: (i, 0))
    vec = pl.BlockSpec((1, d), lambda i: (0, 0))
    return _call(body, name="loss_head", grid=(t_len // tm,), in_specs=[row, vec, row],
                 out_specs=[row, row, vec, pl.BlockSpec((1, LANES), lambda i: (0, 0))],
                 out_shape=[jax.ShapeDtypeStruct((t_len, d), F32), jax.ShapeDtypeStruct((t_len, d), BF16),
                            jax.ShapeDtypeStruct((1, d), F32), jax.ShapeDtypeStruct((1, LANES), F32)],
                 compiler_params=_params(1))(x, g, target)


def _ln_stats(x):
    mu = jnp.mean(x, axis=-1, keepdims=True)
    xc = x - mu
    r = lax.rsqrt(jnp.mean(xc * xc, axis=-1, keepdims=True) + NORM_EPS)
    return r, xc * r


def _ln_bwd(dy, r, xhat, g):
    dxh = dy * g
    return r * (dxh - jnp.mean(dxh, axis=-1, keepdims=True) - xhat * jnp.mean(dxh * xhat, axis=-1, keepdims=True))


def _rope_tables(positions):
    t_len = positions.shape[-1]
    inv_freq = 1.0 / (ROPE_THETA ** (jnp.arange(0, 2 * ROT_HALF, 2, dtype=F32) / (2 * ROT_HALF)))
    ang = positions.astype(F32).reshape(t_len, 1) * inv_freq
    cos = jnp.tile(jnp.cos(ang), (1, LANES // ROT_HALF))
    sin = jnp.tile(jnp.sin(ang), (1, LANES // ROT_HALF))
    lane = jnp.arange(LANES) % HEAD_DIM
    c = jnp.where(lane < 2 * ROT_HALF, cos, 1.0)
    s1 = jnp.where(lane < ROT_HALF, -sin, 0.0)
    s2 = jnp.where((lane >= ROT_HALF) & (lane < 2 * ROT_HALF), sin, 0.0)
    return c.astype(F32), s1.astype(F32), s2.astype(F32)


def _rope_fwd(p, tables):
    t_len = p.shape[0]
    tm = _pick(t_len, (256,))
    n_rot = K_END // LANES

    def body(p_ref, c_ref, s1_ref, s2_ref, o_ref):
        c, s1, s2 = c_ref[...], s1_ref[...], s2_ref[...]
        for j in range(V_END // LANES):
            sl = slice(j * LANES, (j + 1) * LANES)
            t = p_ref[:, sl]
            if j < n_rot:
                t = t * c + pltpu.roll(t, LANES - ROT_HALF, 1) * s1 + pltpu.roll(t, ROT_HALF, 1) * s2
            if j < Q_END // LANES:
                t = t * ATTN_SCALE
            o_ref[:, sl] = t.astype(BF16)

    tab = pl.BlockSpec((tm, LANES), lambda i: (i, 0))
    blk = pl.BlockSpec((tm, V_END), lambda i: (i, 0))
    return _call(body, name="rope_fwd", grid=(t_len // tm,), in_specs=[blk, tab, tab, tab], out_specs=blk,
                 out_shape=jax.ShapeDtypeStruct((t_len, V_END), BF16), compiler_params=_params(1))(p, *tables)


def _assemble_dp(dq, dkc, dkp, dvc, dvp, tables, dalin, dagate, du, dvin):
    t_len = dq.shape[0]
    steps = t_len // ATTN_STEP

    def body(dq_ref, dkc_ref, dkp_ref, dvc_ref, dvp_ref, c_ref, s1_ref, s2_ref, dalin_ref, dagate_ref,
             du_ref, dvin_ref, o_ref):
        keep = (pl.program_id(0) < steps - 1).astype(F32)
        for part in range(ATTN_STEP // BLK):
            rows = slice(part * BLK, (part + 1) * BLK)
            c, s1, s2 = c_ref[rows, :], s1_ref[rows, :], s2_ref[rows, :]

            def unrotate(dr):
                return dr * c + pltpu.roll(dr * s1, ROT_HALF, 1) + pltpu.roll(dr * s2, LANES - ROT_HALF, 1)

            for j in range(Q_END // LANES):
                sl = slice(j * LANES, (j + 1) * LANES)
                o_ref[rows, sl] = unrotate(dq_ref[rows, sl]).astype(BF16)
            for j in range((K_END - Q_END) // LANES):
                sl = slice(j * LANES, (j + 1) * LANES)
                dk, dv = dkc_ref[rows, sl], dvc_ref[rows, sl]
                if part == ATTN_STEP // BLK - 1:
                    dk = dk + keep * dkp_ref[:, sl]
                    dv = dv + keep * dvp_ref[:, sl]
                o_ref[rows, Q_END + j * LANES:Q_END + (j + 1) * LANES] = unrotate(dk).astype(BF16)
                o_ref[rows, K_END + j * LANES:K_END + (j + 1) * LANES] = dv.astype(BF16)
        o_ref[:, V_END:V_END + CONV_CH] = dalin_ref[...]
        o_ref[:, V_END + CONV_CH:CONV_END] = dagate_ref[...]
        o_ref[:, CONV_END:CONV_END + SGU_CH] = du_ref[...]
        o_ref[:, CONV_END + SGU_CH:IN_COLS] = dvin_ref[...]

    def cur(w):
        return pl.BlockSpec((ATTN_STEP, w), lambda i: (i, 0))

    def nxt(w):
        return pl.BlockSpec((BLK, w), lambda i: (jnp.minimum(i + 1, steps - 1), 0))

    kvw = K_END - Q_END
    return _call(body, name="assemble_dp", grid=(steps,),
                 in_specs=[cur(Q_END), cur(kvw), nxt(kvw), cur(kvw), nxt(kvw), cur(LANES), cur(LANES), cur(LANES),
                           cur(CONV_CH), cur(CONV_CH), cur(SGU_CH), cur(SGU_CH)],
                 out_specs=cur(IN_COLS), out_shape=jax.ShapeDtypeStruct((t_len, IN_COLS), BF16),
                 compiler_params=_params(1))(dq, dkc, dkp, dvc, dvp, *tables, dalin, dagate, du, dvin)


ATTN_STEP = 2 * BLK
ATTN_SCALE = HEAD_DIM ** -0.5
MASKED = -1e30


def _attn_bias():
    qi = jnp.arange(GQ * BLK)[None, :] % BLK
    kj = jnp.arange(2 * BLK)[:, None]
    dist = qi + BLK - kj
    band = (dist >= 0) & (dist < BLK)
    return jnp.stack([jnp.where(band & (kj >= BLK), 0.0, MASKED), jnp.where(band, 0.0, MASKED)]).astype(F32)


def _side_by_side(ref, cols):
    return jnp.concatenate([ref[h, :, cols] for h in range(GQ)], axis=1)


def _attn_chains(qt_ref, kp_ref, kc_ref):
    kc = kc_ref[...]
    pos_a, pos_b = slice(0, BLK), slice(BLK, ATTN_STEP)
    return [(pos_a, _side_by_side(qt_ref, pos_a), jnp.concatenate([kp_ref[...], kc[:BLK]], axis=0)),
            (pos_b, _side_by_side(qt_ref, pos_b), kc)]


def _attn_weights(qt, kk, bias, snk):
    s = lax.dot_general(kk, qt, NN, preferred_element_type=F32) + bias
    m = jnp.maximum(jnp.max(s, axis=0, keepdims=True), snk)
    e = jnp.exp(s - m)
    es = jnp.exp(snk - m)
    return e, es, 1.0 / (jnp.sum(e, axis=0, keepdims=True) + es)


def _attn_specs():
    before = lambda n: jnp.maximum(2 * n - 1, 0)
    return dict(
        qt=pl.BlockSpec((GQ, HEAD_DIM, ATTN_STEP), lambda g, n: (g, 0, n)),
        q=pl.BlockSpec((GQ, ATTN_STEP, HEAD_DIM), lambda g, n: (g, n, 0)),
        cur=pl.BlockSpec((None, ATTN_STEP, HEAD_DIM), lambda g, n: (g, n, 0)),
        prev=pl.BlockSpec((None, BLK, HEAD_DIM), lambda g, n: (g, before(n), 0)),
        cur_t=pl.BlockSpec((None, HEAD_DIM, ATTN_STEP), lambda g, n: (g, 0, n)),
        prev_t=pl.BlockSpec((None, HEAD_DIM, BLK), lambda g, n: (g, 0, before(n))),
        snk=pl.BlockSpec((None, 1, GQ * BLK), lambda g, n: (g, 0, 0)),
        bias_a=pl.BlockSpec((None, 2 * BLK, GQ * BLK), lambda g, n: (jnp.minimum(n, 1), 0, 0)),
        bias_b=pl.BlockSpec((None, 2 * BLK, GQ * BLK), lambda g, n: (1, 0, 0)))


def _attn_fwd(qt, k, vt, snk, comm=None):
    steps = qt.shape[2] // ATTN_STEP

    def body(qt_ref, kp_ref, kc_ref, vtp_ref, vtc_ref, snk_ref, ba_ref, bb_ref, o_ref):
        snk = snk_ref[...]
        vtc = vtc_ref[...]
        values = (jnp.concatenate([vtp_ref[...], vtc[:, :BLK]], axis=1), vtc)
        for (pos, qt_v, kk), vt_v, b_ref in zip(_attn_chains(qt_ref, kp_ref, kc_ref), values, (ba_ref, bb_ref)):
            e, _, inv = _attn_weights(qt_v, kk, b_ref[...], snk)
            o = lax.dot_general(vt_v, e.astype(BF16), NN, preferred_element_type=F32) * inv
            for h in range(GQ):
                o_ref[h, :, pos] = o[:, h * BLK:(h + 1) * BLK].astype(BF16)

    sp = _attn_specs()
    bias = _attn_bias()
    return _hosted_call(body, comm, "attn_fwd", (N_KV_HEADS, steps), [qt, k, k, vt, vt, snk, bias, bias],
                        [sp["qt"], sp["prev"], sp["cur"], sp["prev_t"], sp["cur_t"], sp["snk"], sp["bias_a"],
                         sp["bias_b"]], [jax.ShapeDtypeStruct(qt.shape, BF16)], [sp["qt"]])[0]


def _attn_bwd(qt, q, k, kt, v, snk, dot_, do, comm=None):
    steps = qt.shape[2] // ATTN_STEP

    def body(qt_ref, q_ref, kp_ref, kc_ref, ktp_ref, ktc_ref, vp_ref, vc_ref, snk_ref, ba_ref, bb_ref, dot_ref,
             do_ref, dq_ref, dkp_ref, dkc_ref, dvp_ref, dvc_ref, dsnk_ref):
        snk = snk_ref[...]
        vc, ktc = vc_ref[...], ktc_ref[...]
        values = (jnp.concatenate([vp_ref[...], vc[:BLK]], axis=0), vc)
        keys_t = (jnp.concatenate([ktp_ref[...], ktc[:, :BLK]], axis=1), ktc)
        row = lax.broadcasted_iota(jnp.int32, (SUBLANES, LANES), 0)
        tile = jnp.zeros((SUBLANES, LANES), F32)
        grads = []
        for (pos, qt_v, kk), vv, kt_v, b_ref in zip(_attn_chains(qt_ref, kp_ref, kc_ref), values, keys_t,
                                                    (ba_ref, bb_ref)):
            e, es, inv = _attn_weights(qt_v, kk, b_ref[...], snk)
            p = e * inv
            dp = lax.dot_general(vv, _side_by_side(dot_ref, pos), NN, preferred_element_type=F32)
            delta = jnp.sum(p * dp, axis=0, keepdims=True)
            ds = (p * (dp - delta)).astype(BF16)
            dq = lax.dot_general(kt_v, ds, NN, preferred_element_type=F32) * ATTN_SCALE
            for h in range(GQ):
                dq_ref[h, :, pos] = dq[:, h * BLK:(h + 1) * BLK]
            q_v = q_ref[:, pos, :].reshape(GQ * BLK, HEAD_DIM)
            do_v = do_ref[:, pos, :].reshape(GQ * BLK, HEAD_DIM)
            grads.append((lax.dot_general(ds, q_v, NN, preferred_element_type=F32),
                          lax.dot_general(p.astype(BF16), do_v, NN, preferred_element_type=F32)))
            per_query = -(es * inv) * delta
            for hh in range(GQ):
                tot = jnp.sum(per_query[:, hh * BLK:(hh + 1) * BLK], axis=1, keepdims=True)
                tile = tile + jnp.where(row == hh, tot, 0.0)
        (dk_a, dv_a), (dk_b, dv_b) = grads
        dkp_ref[...] = dk_a[:BLK]
        dvp_ref[...] = dv_a[:BLK]
        dkc_ref[0:BLK, :] = dk_a[BLK:] + dk_b[:BLK]
        dvc_ref[0:BLK, :] = dv_a[BLK:] + dv_b[:BLK]
        dkc_ref[BLK:ATTN_STEP, :] = dk_b[BLK:]
        dvc_ref[BLK:ATTN_STEP, :] = dv_b[BLK:]
        _accumulate(dsnk_ref, tile, pl.program_id(1) == 0)

    sp = _attn_specs()
    bias = _attn_bias()
    step_blk = pl.BlockSpec((None, BLK, HEAD_DIM), lambda g, n: (g, n, 0))
    kv_shape = jax.ShapeDtypeStruct(k.shape, F32)
    prev_shape = jax.ShapeDtypeStruct((N_KV_HEADS, steps * BLK, HEAD_DIM), F32)
    return _hosted_call(
        body, comm, "attn_bwd", (N_KV_HEADS, steps), [qt, q, k, k, kt, kt, v, v, snk, bias, bias, dot_, do],
        [sp["qt"], sp["q"], sp["prev"], sp["cur"], sp["prev_t"], sp["cur_t"], sp["prev"], sp["cur"], sp["snk"],
         sp["bias_a"], sp["bias_b"], sp["qt"], sp["q"]],
        [jax.ShapeDtypeStruct(qt.shape, F32), prev_shape, kv_shape, prev_shape, kv_shape,
         jax.ShapeDtypeStruct((N_KV_HEADS, SUBLANES, LANES), F32)],
        [sp["qt"], step_blk, sp["cur"], step_blk, sp["cur"],
         pl.BlockSpec((None, SUBLANES, LANES), lambda g, n: (g, 0, 0))])


CONV_CHUNK = 256


def _shift_up(win, s):
    n = win.shape[0]
    return win if s == 0 else pltpu.roll(win, n - s, 0)


def _conv_col_specs(t_len):
    lin = pl.BlockSpec((t_len, LANES), lambda j: (0, V_END // LANES + j))
    gate = pl.BlockSpec((t_len, LANES), lambda j: (0, (V_END + CONV_CH) // LANES + j))
    col = pl.BlockSpec((t_len, LANES), lambda j: (0, j))
    wsp = pl.BlockSpec((CONV_PAD, LANES), lambda j: (0, j))
    return lin, gate, col, wsp


def _conv_fwd(p, w, b):
    t_len = p.shape[0]
    ch = CONV_CHUNK

    def body(lin_ref, gate_ref, w_ref, b_ref, y_ref, hp_ref):
        hp_ref[0:CONV_PAD, :] = jnp.zeros((CONV_PAD, LANES), F32)

        def fill(c, carry):
            r0 = pl.multiple_of(c * ch, ch)
            hp_ref[pl.ds(r0 + CONV_PAD, ch), :] = lin_ref[pl.ds(r0, ch), :] * _sigmoid(gate_ref[pl.ds(r0, ch), :])
            return carry

        lax.fori_loop(0, t_len // ch, fill, 0)

        def conv(c, carry):
            r0 = pl.multiple_of(c * ch, ch)
            win = hp_ref[pl.ds(r0, ch + CONV_PAD), :]
            acc = jnp.zeros((ch, LANES), F32)
            for k in range(CONV_WIDTH):
                acc = acc + _shift_up(win, CONV_PAD - (CONV_WIDTH - 1) + k)[:ch] * w_ref[k:k + 1, :]
            y_ref[pl.ds(r0, ch), :] = acc + b_ref[...]
            return carry

        lax.fori_loop(0, t_len // ch, conv, 0)

    lin, gate, col, wsp = _conv_col_specs(t_len)
    return _call(body, name="conv_fwd", grid=(CONV_CH // LANES,),
                 in_specs=[lin, gate, wsp, pl.BlockSpec((1, LANES), lambda j: (0, j))], out_specs=col,
                 out_shape=jax.ShapeDtypeStruct((t_len, CONV_CH), F32),
                 scratch_shapes=[pltpu.VMEM((t_len + CONV_PAD, LANES), F32)],
                 compiler_params=_params(1))(p, p, w, b)


def _conv_post_fwd(y, g, b):
    t_len = y.shape[0]
    tm = _pick(t_len, (512,))

    def body(y_ref, g_ref, b_ref, o_ref):
        _, xhat = _ln_stats(y_ref[...])
        z = xhat * g_ref[...] + b_ref[...]
        o_ref[...] = (z * _sigmoid(z)).astype(BF16)

    row = pl.BlockSpec((tm, CONV_CH), lambda i: (i, 0))
    vec = pl.BlockSpec((1, CONV_CH), lambda i: (0, 0))
    return _call(body, name="conv_post_fwd", grid=(t_len // tm,), in_specs=[row, vec, vec], out_specs=row,
                 out_shape=jax.ShapeDtypeStruct((t_len, CONV_CH), BF16), compiler_params=_params(1))(y, g, b)


def _conv_post_bwd(dmix, y, g, b):
    t_len = y.shape[0]
    tm = _pick(t_len, (512,))

    def body(do_ref, y_ref, g_ref, b_ref, dy_ref, dg_ref, db_ref, dcb_ref):
        first = pl.program_id(0) == 0
        r, xhat = _ln_stats(y_ref[...])
        g_v = g_ref[...]
        z = xhat * g_v + b_ref[...]
        sig = _sigmoid(z)
        dz = do_ref[...] * (sig * (1.0 + z * (1.0 - sig)))
        _accumulate(db_ref, jnp.sum(dz, axis=0, keepdims=True), first)
        _accumulate(dg_ref, jnp.sum(dz * xhat, axis=0, keepdims=True), first)
        dy = _ln_bwd(dz, r, xhat, g_v)
        dy_ref[...] = dy
        _accumulate(dcb_ref, jnp.sum(dy, axis=0, keepdims=True), first)

    row = pl.BlockSpec((tm, CONV_CH), lambda i: (i, 0))
    do_spec = pl.BlockSpec((tm, CONV_CH), lambda i: (i, Q_END // CONV_CH))
    vec = pl.BlockSpec((1, CONV_CH), lambda i: (0, 0))
    vshape = jax.ShapeDtypeStruct((1, CONV_CH), F32)
    return _call(body, name="conv_post_bwd", grid=(t_len // tm,), in_specs=[do_spec, row, vec, vec],
                 out_specs=[row, vec, vec, vec],
                 out_shape=[jax.ShapeDtypeStruct((t_len, CONV_CH), F32), vshape, vshape, vshape],
                 compiler_params=_params(1))(dmix, y, g, b)


def _conv_bwd(p, dy, w):
    t_len = p.shape[0]
    ch = CONV_CHUNK

    def body(lin_ref, gate_ref, dy_ref, w_ref, dlin_ref, dgate_ref, dw_ref, hp_ref, dyp_ref):
        hp_ref[0:CONV_PAD, :] = jnp.zeros((CONV_PAD, LANES), F32)
        dyp_ref[t_len:t_len + CONV_PAD, :] = jnp.zeros((CONV_PAD, LANES), F32)
        dw_ref[...] = jnp.zeros((CONV_PAD, LANES), F32)

        def fill(c, carry):
            r0 = pl.multiple_of(c * ch, ch)
            hp_ref[pl.ds(r0 + CONV_PAD, ch), :] = lin_ref[pl.ds(r0, ch), :] * _sigmoid(gate_ref[pl.ds(r0, ch), :])
            dyp_ref[pl.ds(r0, ch), :] = dy_ref[pl.ds(r0, ch), :]
            return carry

        lax.fori_loop(0, t_len // ch, fill, 0)

        def step(c, carry):
            r0 = pl.multiple_of(c * ch, ch)
            win_h = hp_ref[pl.ds(r0, ch + CONV_PAD), :]
            win_dy = dyp_ref[pl.ds(r0, ch + CONV_PAD), :]
            dyc = win_dy[:ch]
            dh = jnp.zeros((ch, LANES), F32)
            for k in range(CONV_WIDTH):
                tap = _shift_up(win_h, CONV_PAD - (CONV_WIDTH - 1) + k)[:ch]
                dw_ref[k:k + 1, :] += jnp.sum(dyc * tap, axis=0, keepdims=True)
                dh = dh + _shift_up(win_dy, CONV_WIDTH - 1 - k)[:ch] * w_ref[k:k + 1, :]
            lin = lin_ref[pl.ds(r0, ch), :]
            sig = _sigmoid(gate_ref[pl.ds(r0, ch), :])
            dlin_ref[pl.ds(r0, ch), :] = (dh * sig).astype(BF16)
            dgate_ref[pl.ds(r0, ch), :] = (dh * lin * (sig * (1.0 - sig))).astype(BF16)
            return carry

        lax.fori_loop(0, t_len // ch, step, 0)

    lin, gate, col, wsp = _conv_col_specs(t_len)
    half = jax.ShapeDtypeStruct((t_len, CONV_CH), BF16)
    return _call(body, name="conv_bwd", grid=(CONV_CH // LANES,), in_specs=[lin, gate, col, wsp],
                 out_specs=[col, col, wsp],
                 out_shape=[half, half, jax.ShapeDtypeStruct((CONV_PAD, CONV_CH), F32)],
                 scratch_shapes=[pltpu.VMEM((t_len + CONV_PAD, LANES), F32), pltpu.VMEM((t_len + CONV_PAD, LANES), F32)],
                 compiler_params=_params(1))(p, p, dy, w)


def _sgu_mixed(v, w_ref, bt_ref, j):
    lane = lax.broadcasted_iota(jnp.int32, (BLK, LANES), 1)
    lo = lane < HEAD_DIM
    tri = lax.broadcasted_iota(jnp.int32, (BLK, BLK), 0) >= lax.broadcasted_iota(jnp.int32, (BLK, BLK), 1)
    vs = v[:, j * LANES:(j + 1) * LANES]
    v_lo = jnp.where(lo, vs, 0.0).astype(BF16)
    v_hi = jnp.where(lo, 0.0, vs).astype(BF16)
    w_lo = jnp.where(tri, w_ref[2 * j], 0.0).astype(BF16)
    w_hi = jnp.where(tri, w_ref[2 * j + 1], 0.0).astype(BF16)
    m = (lax.dot_general(w_lo, v_lo, NN, preferred_element_type=F32)
         + lax.dot_general(w_hi, v_hi, NN, preferred_element_type=F32))
    bias = jnp.where(lo, bt_ref[:, 2 * j:2 * j + 1], bt_ref[:, 2 * j + 1:2 * j + 2])
    return m + bias, (v_lo, v_hi, w_lo, w_hi, lo, tri)


def _sgu_specs():
    u_spec = pl.BlockSpec((BLK, SGU_CH), lambda i: (i, CONV_END // SGU_CH))
    v_spec = pl.BlockSpec((BLK, SGU_CH), lambda i: (i, CONV_END // SGU_CH + 1))
    vec = pl.BlockSpec((1, SGU_CH), lambda i: (0, 0))
    w_spec = pl.BlockSpec((SGU_HEADS, BLK, BLK), lambda i: (0, 0, 0))
    bt_spec = pl.BlockSpec((BLK, SGU_HEADS), lambda i: (0, 0))
    row = pl.BlockSpec((BLK, SGU_CH), lambda i: (i, 0))
    return u_spec, v_spec, vec, w_spec, bt_spec, row


def _sgu_fwd(p, g, b, w, bt):
    t_len = p.shape[0]

    def body(u_ref, vin_ref, g_ref, b_ref, w_ref, bt_ref, o_ref):
        _, xhat = _ln_stats(vin_ref[...])
        v = xhat * g_ref[...] + b_ref[...]
        for j in range(SGU_CH // LANES):
            m, _ = _sgu_mixed(v, w_ref, bt_ref, j)
            sl = slice(j * LANES, (j + 1) * LANES)
            o_ref[:, sl] = (u_ref[:, sl] * m).astype(BF16)

    u_spec, v_spec, vec, w_spec, bt_spec, row = _sgu_specs()
    return _call(body, name="sgu_fwd", grid=(t_len // BLK,), in_specs=[u_spec, v_spec, vec, vec, w_spec, bt_spec],
                 out_specs=row, out_shape=jax.ShapeDtypeStruct((t_len, SGU_CH), BF16),
                 compiler_params=_params(1))(p, p, g, b, w, bt)


def _sgu_bwd(p, dmix, g, b, w, bt):
    t_len = p.shape[0]

    def body(u_ref, vin_ref, do_ref, g_ref, b_ref, w_ref, bt_ref, du_ref, dvin_ref, dw_ref, dbt_ref, dg_ref,
             db_ref, dv_ref):
        first = pl.program_id(0) == 0
        r, xhat = _ln_stats(vin_ref[...])
        g_v = g_ref[...]
        v = xhat * g_v + b_ref[...]
        lane = lax.broadcasted_iota(jnp.int32, (BLK, LANES), 1)
        dbt = jnp.zeros((BLK, LANES), F32)

        @pl.when(first)
        def _():
            dw_ref[...] = jnp.zeros((SGU_HEADS, BLK, BLK), F32)

        for j in range(SGU_CH // LANES):
            m, (v_lo, v_hi, w_lo, w_hi, lo, tri) = _sgu_mixed(v, w_ref, bt_ref, j)
            sl = slice(j * LANES, (j + 1) * LANES)
            do_v = do_ref[:, sl]
            du_ref[:, sl] = (do_v * m).astype(BF16)
            dm = do_v * u_ref[:, sl]
            dm_lo = jnp.where(lo, dm, 0.0)
            dm_hi = jnp.where(lo, 0.0, dm)
            dbt = dbt + jnp.where(lane == 2 * j, jnp.sum(dm_lo, axis=-1, keepdims=True), 0.0)
            dbt = dbt + jnp.where(lane == 2 * j + 1, jnp.sum(dm_hi, axis=-1, keepdims=True), 0.0)
            dm_lo, dm_hi = dm_lo.astype(BF16), dm_hi.astype(BF16)
            dw_ref[2 * j] += jnp.where(tri, lax.dot_general(dm_lo, v_lo, NT, preferred_element_type=F32), 0.0)
            dw_ref[2 * j + 1] += jnp.where(tri, lax.dot_general(dm_hi, v_hi, NT, preferred_element_type=F32), 0.0)
            dv_ref[:, sl] = (lax.dot_general(w_lo, dm_lo, TN, preferred_element_type=F32)
                             + lax.dot_general(w_hi, dm_hi, TN, preferred_element_type=F32))
        _accumulate(dbt_ref, dbt, first)
        dv = dv_ref[...]
        _accumulate(db_ref, jnp.sum(dv, axis=0, keepdims=True), first)
        _accumulate(dg_ref, jnp.sum(dv * xhat, axis=0, keepdims=True), first)
        dvin_ref[...] = _ln_bwd(dv, r, xhat, g_v).astype(BF16)

    u_spec, v_spec, vec, w_spec, bt_spec, row = _sgu_specs()
    do_spec = pl.BlockSpec((BLK, SGU_CH), lambda i: (i, (Q_END + CONV_CH) // SGU_CH))
    half = jax.ShapeDtypeStruct((t_len, SGU_CH), BF16)
    vshape = jax.ShapeDtypeStruct((1, SGU_CH), F32)
    return _call(body, name="sgu_bwd", grid=(t_len // BLK,),
                 in_specs=[u_spec, v_spec, do_spec, vec, vec, w_spec, bt_spec],
                 out_specs=[row, row, w_spec, pl.BlockSpec((BLK, LANES), lambda i: (0, 0)), vec, vec],
                 out_shape=[half, half, jax.ShapeDtypeStruct((SGU_HEADS, BLK, BLK), F32),
                            jax.ShapeDtypeStruct((BLK, LANES), F32), vshape, vshape],
                 scratch_shapes=[pltpu.VMEM((BLK, SGU_CH), F32)],
                 compiler_params=_params(1))(p, p, dmix, g, b, w, bt)


def _place():
    x, y, c = lax.axis_index("x"), lax.axis_index("y"), lax.axis_index("c")
    chips = [(1 - x, y), (x, 1 - y), (1 - x, 1 - y)]
    return x, y, c, chips


def _hbm_specs(n):
    return [pl.BlockSpec(memory_space=pltpu.HBM)] * n


def _comm_params():
    return pltpu.CompilerParams(has_side_effects=True)


def _remote(src, dst, send_sem, recv_sem, to):
    return pltpu.make_async_remote_copy(src_ref=src, dst_ref=dst, send_sem=send_sem, recv_sem=recv_sem,
                                        device_id=to, device_id_type=MESH_ID)


def _cast_place(w_local, chip):
    n, rows, cols = w_local.shape
    tr = _row_tile(rows, (4 + 2) * cols)

    def body(chip_ref, w_ref, o_ref):
        o_ref[...] = w_ref[...].astype(BF16)

    grid_spec = pltpu.PrefetchScalarGridSpec(
        num_scalar_prefetch=1, grid=(n, rows // tr),
        in_specs=[pl.BlockSpec((None, tr, cols), lambda l, i, ch: (l, i, 0))],
        out_specs=pl.BlockSpec((None, None, tr, cols), lambda l, i, ch: (l, ch[0], i, 0)))
    return _call(body, name="cast_place", grid_spec=grid_spec,
                 out_shape=jax.ShapeDtypeStruct((n, N_CHIPS, rows, cols), BF16), compiler_params=_params(2))(chip, w_local)


def _all_gather_weights(placed, shards):
    n_placed, nt = len(placed), len(placed) + len(shards)

    def body(*refs):
        ins, outs = refs[:nt], refs[nt:2 * nt]
        ici_send, ici_recv, d2d_send, d2d_recv, local_sem = refs[2 * nt:]
        x, y, c, chips = _place()
        me = 2 * x + y
        sibling = (x, y, 1 - c)
        local = [pltpu.make_async_copy(ins[t].at[l], outs[t].at[l, me], local_sem.at[2 * (t - n_placed) + l])
                 for t in range(n_placed, nt) for l in range(2)]
        for cp in local:
            cp.start()
        sends = []
        for t in range(nt):
            src = outs[t].at[c, me] if t < n_placed else ins[t].at[c]
            for j, (px, py) in enumerate(chips):
                sends.append(_remote(src, outs[t].at[c, me], ici_send.at[3 * t + j], ici_recv.at[3 * t + j],
                                     (px, py, c)))
        for cp in sends:
            cp.start()
        for t in range(nt):
            for j, (px, py) in enumerate(chips):
                slab = outs[t].at[c, 2 * px + py]
                _remote(slab, slab, ici_send.at[3 * t + j], ici_recv.at[3 * t + j], (px, py, c)).wait_recv()
                fwd = _remote(slab, slab, d2d_send.at[3 * t + j], d2d_recv.at[3 * t + j], sibling)
                fwd.start()
                sends.append(fwd)
        for t in range(nt):
            for j, (px, py) in enumerate(chips):
                slab = outs[t].at[1 - c, 2 * px + py]
                _remote(slab, slab, d2d_send.at[3 * t + j], d2d_recv.at[3 * t + j], sibling).wait_recv()
        for cp in sends:
            cp.wait_send()
        for cp in local:
            cp.wait()

    out_shape = [jax.ShapeDtypeStruct(p.shape, p.dtype) for p in placed]
    out_shape += [jax.ShapeDtypeStruct((2, N_CHIPS) + s.shape[1:], s.dtype) for s in shards]
    sems = [pltpu.SemaphoreType.DMA((3 * nt,))] * 4 + [pltpu.SemaphoreType.DMA((2 * len(shards),))]
    return _call(body, name="all_gather_weights", in_specs=_hbm_specs(nt), out_specs=_hbm_specs(nt),
                 out_shape=out_shape, scratch_shapes=sems, input_output_aliases={t: t for t in range(n_placed)},
                 compiler_params=_comm_params())(*placed, *shards)


def _gather_comm(bufs, pieces):
    n = len(pieces)
    sems = [pltpu.SemaphoreType.DMA((3 * n,))] * 4

    def half(ref, layer, chip, r0, nr, which):
        return ref.at[layer, chip, pl.ds(pl.multiple_of(r0 + which * (nr // 2), SUBLANES), nr // 2)]

    def start(rd, al, fr, sm):
        ici_send, ici_recv, _, _ = sm
        x, y, c, chips = _place()
        for i, (t, layer, r0, nr) in enumerate(pieces):
            own = half(al[t], layer, 2 * x + y, r0, nr, c)
            for j, (px, py) in enumerate(chips):
                _remote(own, own, ici_send.at[3 * i + j], ici_recv.at[3 * i + j], (px, py, c)).start()

    def finish(rd, al, fr, sm):
        ici_send, ici_recv, d2d_send, d2d_recv = sm
        x, y, c, chips = _place()
        sibling = (x, y, 1 - c)
        passed = []
        for i, (t, layer, r0, nr) in enumerate(pieces):
            for j, (px, py) in enumerate(chips):
                got = half(al[t], layer, 2 * px + py, r0, nr, c)
                _remote(got, got, ici_send.at[3 * i + j], ici_recv.at[3 * i + j], (px, py, c)).wait_recv()
                fwd = _remote(got, got, d2d_send.at[3 * i + j], d2d_recv.at[3 * i + j], sibling)
                fwd.start()
                passed.append(fwd)
        for i, (t, layer, r0, nr) in enumerate(pieces):
            own = half(al[t], layer, 2 * x + y, r0, nr, c)
            for j, (px, py) in enumerate(chips):
                _remote(own, own, ici_send.at[3 * i + j], ici_recv.at[3 * i + j], (px, py, c)).wait_send()
                theirs = half(al[t], layer, 2 * px + py, r0, nr, 1 - c)
                _remote(theirs, theirs, d2d_send.at[3 * i + j], d2d_recv.at[3 * i + j], sibling).wait_recv()
        for fwd in passed:
            fwd.wait_send()

    return _Comm([], bufs, [], sems, start, finish)


def _own_rows(ref, c, which=0):
    hr = ref.shape[-2] // 2
    start = pl.multiple_of((c if which == 0 else 1 - c) * hr, SUBLANES)
    return ref.at[(slice(None),) * (len(ref.shape) - 2) + (pl.ds(start, hr),)]


def _to_owner_comm(stacks, layer):
    nt = len(stacks)
    sems = [pltpu.SemaphoreType.DMA((nt,))] * 2
    fresh = [jax.ShapeDtypeStruct((N_CHIPS, s.shape[2] // 2, s.shape[3]), s.dtype) for s in stacks]

    def copies(rd, fr, sm):
        x, y, c, _ = _place()
        return [_remote(_own_rows(rd[t].at[layer], c, 1), fr[t], sm[0].at[t], sm[1].at[t], (x, y, 1 - c))
                for t in range(nt)]

    def start(rd, al, fr, sm):
        for cp in copies(rd, fr, sm):
            cp.start()

    def finish(rd, al, fr, sm):
        for cp in copies(rd, fr, sm):
            cp.wait()

    return _Comm(stacks, [], fresh, sems, start, finish)


def _chip_comm(partials):
    nt = len(partials)
    sems = [pltpu.SemaphoreType.DMA((3 * nt,))] * 2
    fresh = [jax.ShapeDtypeStruct((3,) + p.shape[1:], p.dtype) for p in partials]

    def each(rd, fr, sm, act):
        x, y, c, chips = _place()
        for t in range(nt):
            for j, (px, py) in enumerate(chips):
                act(_remote(rd[t].at[2 * px + py], fr[t].at[j], sm[0].at[3 * t + j], sm[1].at[3 * t + j], (px, py, c)))

    def start(rd, al, fr, sm):
        each(rd, fr, sm, lambda cp: cp.start())

    def finish(rd, al, fr, sm):
        each(rd, fr, sm, lambda cp: cp.wait())

    return _Comm(partials, [], fresh, sems, start, finish)


def _from_owner_comm(finals, layer):
    nt = len(finals)
    sems = [pltpu.SemaphoreType.DMA((nt,))] * 2

    def start(rd, al, fr, sm):
        x, y, c, _ = _place()
        for t in range(nt):
            mine = _own_rows(al[t].at[layer], c)
            _remote(mine, mine, sm[0].at[t], sm[1].at[t], (x, y, 1 - c)).start()

    def finish(rd, al, fr, sm):
        x, y, c, _ = _place()
        for t in range(nt):
            mine, theirs = _own_rows(al[t].at[layer], c), _own_rows(al[t].at[layer], c, 1)
            _remote(mine, mine, sm[0].at[t], sm[1].at[t], (x, y, 1 - c)).wait_send()
            _remote(theirs, theirs, sm[0].at[t], sm[1].at[t], (x, y, 1 - c)).wait_recv()

    return _Comm([], finals, [], sems, start, finish)


def _all_reduce_small(buf):
    rows = buf.shape[0]

    def body(x_ref, out_ref, all_ref, send_sems, recv_sems, local_sem):
        x, y, c, chips = _place()
        me, sibling = (x, y, c), (x, y, 1 - c)

        def block(px, py, pc):
            return all_ref.at[pl.ds((4 * px + 2 * py + pc) * rows, rows), :]

        def copy(k, blk, to, src=None):
            return _remote(block(*blk) if src is None else src, block(*blk), send_sems.at[k], recv_sems.at[k], to)

        mine = pltpu.make_async_copy(x_ref, block(*me), local_sem)
        mine.start()
        first = [copy(0, me, sibling, src=x_ref)]
        first += [copy(1 + j, me, (*chip, c), src=x_ref) for j, chip in enumerate(chips)]
        for cp in first:
            cp.start()
        passed = [copy(4 + j, (*chip, c), sibling) for j, chip in enumerate(chips)]
        for j, chip in enumerate(chips):
            copy(1 + j, (*chip, c), me).wait_recv()
            passed[j].start()
        copy(0, sibling, me).wait_recv()
        for j, chip in enumerate(chips):
            copy(4 + j, (*chip, 1 - c), me).wait_recv()
        for cp in first + passed:
            cp.wait_send()
        mine.wait()
        tot = all_ref[0:rows, :]
        for k in range(1, N_DEV):
            tot = tot + all_ref[k * rows:(k + 1) * rows, :]
        out_ref[...] = tot

    vm = pl.BlockSpec(memory_space=pltpu.VMEM)
    return _call(body, name="all_reduce_small", in_specs=[vm], out_specs=vm,
                 out_shape=jax.ShapeDtypeStruct(buf.shape, F32),
                 scratch_shapes=[pltpu.VMEM((N_DEV * rows, LANES), F32), pltpu.SemaphoreType.DMA((7,)),
                                 pltpu.SemaphoreType.DMA((7,)), pltpu.SemaphoreType.DMA],
                 compiler_params=pltpu.CompilerParams(has_side_effects=True,
                                                      vmem_limit_bytes=V7X_VMEM_LIMIT_BYTES))(buf)


ROW_TILE = 128
ROW_STEP_BYTES = 12 * 2**20


def _row_tile(rows, bytes_per_row, align=2 * SUBLANES):
    cap = max(align, min(rows, ROW_STEP_BYTES // bytes_per_row) // align * align)
    for tr in range(cap, 0, -align):
        if rows % tr == 0:
            return tr
    raise ValueError(f"no row tile for {rows} rows")


def _chip_partial(stack, received, layer, place):
    _, half_rows, cols = received.shape
    tr = _row_tile(half_rows, 3 * 2 * cols)
    nh = half_rows // tr

    def body(place_ref, a_ref, b_ref, o_ref):
        o_ref[...] = (a_ref[...].astype(F32) + b_ref[...].astype(F32)).astype(BF16)

    blk = pl.BlockSpec((None, tr, cols), lambda s, i, pr: (s, i, 0))
    grid_spec = pltpu.PrefetchScalarGridSpec(
        num_scalar_prefetch=1, grid=(N_CHIPS, nh),
        in_specs=[pl.BlockSpec((None, None, tr, cols), lambda s, i, pr: (layer, s, pr[0] * nh + i, 0)), blk],
        out_specs=blk)
    return _call(body, name="chip_partial", grid_spec=grid_spec, out_shape=jax.ShapeDtypeStruct(received.shape, BF16),
                 compiler_params=_params(2))(place, stack, received)


def _final_sum(partial, from_chips, layer, place, finals):
    _, half_rows, cols = partial.shape
    tr = _row_tile(half_rows, (4 * 2 + 4) * cols)
    nh = half_rows // tr

    def body(place_ref, a_ref, r_ref, *rest):
        o_ref = rest[-1]
        tot = a_ref[...].astype(F32)
        for j in range(3):
            tot = tot + r_ref[j].astype(F32)
        o_ref[...] = tot

    in_specs = [pl.BlockSpec((None, tr, cols), lambda i, pr: (pr[1], i, 0)),
                pl.BlockSpec((3, tr, cols), lambda i, pr: (0, i, 0))]
    args = [place, partial, from_chips]
    kw = {}
    if finals is not None:
        in_specs.append(pl.BlockSpec(memory_space=pl.ANY))
        args.append(finals)
        kw["input_output_aliases"] = {3: 0}
    grid_spec = pltpu.PrefetchScalarGridSpec(
        num_scalar_prefetch=1, grid=(nh,), in_specs=in_specs,
        out_specs=pl.BlockSpec((None, tr, cols), lambda i, pr: (layer, pr[0] * nh + i, 0)))
    return _call(body, name="final_sum", grid_spec=grid_spec,
                 out_shape=jax.ShapeDtypeStruct((2, 2 * half_rows, cols), F32), compiler_params=_params(1), **kw)(*args)


def _adamw(w, g, m, v, comm=None):
    n, rows, cols = w.shape
    tr = _row_tile(rows, 8 * 4 * cols, SUBLANES)
    c1 = 1.0 - ADAM_B1 ** ADAM_STEP
    c2 = 1.0 - ADAM_B2 ** ADAM_STEP

    def body(w_ref, g_ref, m_ref, v_ref, d_ref, nm_ref, nv_ref, go_ref):
        g_v = g_ref[...]
        go_ref[...] = g_v
        nm = ADAM_B1 * m_ref[...] + (1.0 - ADAM_B1) * g_v
        nv = ADAM_B2 * v_ref[...] + (1.0 - ADAM_B2) * (g_v * g_v)
        nm_ref[...] = nm
        nv_ref[...] = nv
        d_ref[...] = -ADAM_LR * ((nm / c1) / (jnp.sqrt(nv / c2) + ADAM_EPS) + ADAM_WD * w_ref[...])

    blk = pl.BlockSpec((None, tr, cols), lambda l, i: (l, i, 0))
    shape = jax.ShapeDtypeStruct(w.shape, F32)
    return _hosted_call(body, comm, "adamw", (n, rows // tr), [w, g, m, v], [blk] * 4, [shape] * 4, [blk] * 4)


def _to_heads(a, n_heads, transposed=False):
    t_len = a.shape[0]
    return a.reshape(t_len, n_heads, HEAD_DIM).transpose((1, 2, 0) if transposed else (1, 0, 2))


def _from_heads(a, transposed=False):
    a = a.transpose((2, 0, 1) if transposed else (1, 0, 2))
    return a.reshape(a.shape[0], a.shape[1] * HEAD_DIM)


class _Schedule:
    def __init__(self):
        self.sites = {}
        self.open = []

    def add(self, site, make, done=None):
        self.sites.setdefault(site, []).append((make, done))

    def begin(self, site):
        self.open = [(make(), done) for make, done in self.sites.pop(site, [])]
        return _merge_comms([cm for cm, _ in self.open])

    def end(self):
        for cm, done in self.open:
            if done is not None:
                done(cm)
        self.open = []


def _ffn_fwd(x, gain, wg, which, layer, sched):
    w_in_name, w_out_name = f"ffn{which}_w_in", f"ffn{which}_w_out"
    h = _rmsnorm_fwd(x, gain)
    comm = sched.begin(("ffn_in", layer, which))
    gu, act = _mm_ffn_in(h, wg[w_in_name], layer, comm)
    sched.end()
    comm = sched.begin(("ffn_out", layer, which))
    x_new = _mm_out_res("ffn_out", act, wg[w_out_name], layer, x, FFN_RESIDUAL_WEIGHT, comm)[0]
    sched.end()
    return x_new, (x, h, gu, act)


def _ffn_bwd(dx, dxb, saved, gain, wg, which, layer, stacks, sched, weights_first=False):
    w_in_name, w_out_name = f"ffn{which}_w_in", f"ffn{which}_w_out"
    x, h, gu, act = saved
    out = {}

    def dact():
        comm = sched.begin(("ffn_dact", layer, which))
        out["dgu"] = _mm_dact_swiglu(dxb, wg[w_out_name], layer, gu, FFN_RESIDUAL_WEIGHT, comm)
        sched.end()

    def dw_out():
        comm = sched.begin(("ffn_dw_out", layer, which))
        stacks[w_out_name] = _dw_rows("ffn_dw_out", act, dxb, layer, stacks[w_out_name], FFN_RESIDUAL_WEIGHT, comm)
        sched.end()

    def dh():
        comm = sched.begin(("ffn_dh", layer, which))
        out["dh"] = _mm_dh_ffn(out["dgu"], wg[w_in_name], layer, comm)
        sched.end()

    def dw_in():
        comm = sched.begin(("ffn_dw_in", layer, which))
        stacks[w_in_name] = _dw_ffn_in(h, out["dgu"], layer, stacks[w_in_name], comm)
        sched.end()

    for step in ((dact, dw_in, dw_out, dh) if weights_first else (dact, dw_out, dh, dw_in)):
        step()
    return _rmsnorm_bwd(out["dh"], x, gain, dx)


def _mix_fwd(x, gain, wg, layer, small, tables, sched):
    h = _rmsnorm_fwd(x, gain)
    comm = sched.begin(("mix_in", layer))
    p = _mm_proj(h, wg["w_in"], layer, comm)[0]
    sched.end()
    qkv = _rope_fwd(p, tables)
    q_rows, k_rows, v_rows = qkv[:, :Q_END], qkv[:, Q_END:K_END], qkv[:, K_END:V_END]
    qt, q = _to_heads(q_rows, N_Q_HEADS, True), _to_heads(q_rows, N_Q_HEADS)
    kt, k = _to_heads(k_rows, N_KV_HEADS, True), _to_heads(k_rows, N_KV_HEADS)
    vt, v = _to_heads(v_rows, N_KV_HEADS, True), _to_heads(v_rows, N_KV_HEADS)
    comm = sched.begin(("attn", layer))
    attn = _from_heads(_attn_fwd(qt, k, vt, small["snk"], comm), True)
    sched.end()
    y = _conv_fwd(p, small["conv_w"], small["conv_b"])
    conv = _conv_post_fwd(y, small["conv_ln_g"], small["conv_ln_b"])
    sgu = _sgu_fwd(p, small["sgu_ln_g"], small["sgu_ln_b"], small["sgu_w"], small["sgu_bt"])
    mix = jnp.concatenate([attn, conv, sgu], axis=1)
    x_new = _mm_out_res("mix_out", mix, wg["w_out"], layer, x, 1.0)[0]
    return x_new, (x, h, p, (qt, q, k, kt, v), y, mix)


def _mix_bwd(dx, dxb, saved, gain, wg, small, tables, layer, stacks, sched):
    x, h, p, (qt, q, k, kt, v), y, mix = saved
    comm = sched.begin(("mix_dout", layer))
    dmix = _mm_dmix(dxb, wg["w_out"], layer, comm)
    sched.end()
    stacks["w_out"] = _dw_rows("mix_dw_out", mix, dxb, layer, stacks["w_out"], 1.0)
    do_rows = dmix[:, :Q_END].astype(BF16)
    comm = sched.begin(("attn_bwd", layer))
    dqt, dkp, dkc, dvp, dvc, dsnk = _attn_bwd(qt, q, k, kt, v, small["snk"], _to_heads(do_rows, N_Q_HEADS, True),
                                              _to_heads(do_rows, N_Q_HEADS), comm)
    sched.end()
    dy, d_ln_g, d_ln_b, d_conv_b = _conv_post_bwd(dmix, y, small["conv_ln_g"], small["conv_ln_b"])
    dalin, dagate, d_conv_w = _conv_bwd(p, dy, small["conv_w"])
    du, dvin, d_sgu_w, d_sgu_bt, d_sgu_g, d_sgu_b = _sgu_bwd(p, dmix, small["sgu_ln_g"], small["sgu_ln_b"],
                                                           small["sgu_w"], small["sgu_bt"])
    dp = _assemble_dp(_from_heads(dqt, True), _from_heads(dkc), _from_heads(dkp), _from_heads(dvc), _from_heads(dvp),
                      tables, dalin, dagate, du, dvin)
    comm = sched.begin(("mix_dw_in", layer))
    stacks["w_in"] = _dw_mix_in(h, dp, layer, stacks["w_in"], comm)
    sched.end()
    comm = sched.begin(("mix_dh", layer))
    dh = _mm_dh_mix(dp, wg["w_in"], layer, comm)
    sched.end()
    dx_in, dxb_in, dgain = _rmsnorm_bwd(dh, x, gain, dx)
    grads = {
        "norm_mix": dgain[0], "conv_dw_w": d_conv_w[:CONV_WIDTH], "conv_dw_b": d_conv_b[0],
        "conv_ln_g": d_ln_g[0], "conv_ln_b": d_ln_b[0], "sgu_ln_g": d_sgu_g[0], "sgu_ln_b": d_sgu_b[0],
        "sgu_w": d_sgu_w, "sgu_b": d_sgu_bt[:, :SGU_HEADS].T, "attn_sinks": dsnk[:, :GQ, 0].reshape(N_Q_HEADS),
    }
    return dx_in, dxb_in, grads


BIG = ("ffn1_w_in", "ffn1_w_out", "w_in", "w_out", "ffn2_w_in", "ffn2_w_out")
SMALL = ("norm_ffn1", "norm_mix", "conv_dw_w", "conv_dw_b", "conv_ln_g", "conv_ln_b", "sgu_ln_g", "sgu_ln_b",
         "sgu_w", "sgu_b", "attn_sinks", "norm_ffn2", "final_norm")
WEIGHTS = ("norm_ffn1", "ffn1_w_in", "ffn1_w_out", "norm_mix", "w_in", "conv_dw_w", "conv_dw_b", "conv_ln_g",
           "conv_ln_b", "sgu_ln_g", "sgu_ln_b", "sgu_w", "sgu_b", "attn_sinks", "w_out", "norm_ffn2", "ffn2_w_in",
           "ffn2_w_out", "final_norm")
PACK_ROWS = SUBLANES * LANES

FIRST_GATHER = [("ffn1_w_in", 0, None)]
FORWARD_PLAN = {
    ("ffn_in", 0, 1): [("ffn1_w_out", 0, None), ("w_in", 0, None)],
    ("ffn_out", 0, 1): [("ffn2_w_in", 0, 0)],
    ("mix_in", 0): [("w_out", 0, None)],
    ("attn", 0): [("ffn2_w_in", 0, 1)],
    ("ffn_in", 0, 2): [("ffn2_w_out", 0, None), ("ffn1_w_in", 1, 0)],
    ("ffn_out", 0, 2): [("ffn1_w_in", 1, 1)],
    ("ffn_in", 1, 1): [("ffn1_w_out", 1, None), ("w_in", 1, None)],
    ("ffn_out", 1, 1): [("ffn2_w_in", 1, 0)],
    ("mix_in", 1): [("w_out", 1, None)],
    ("attn", 1): [("ffn2_w_in", 1, 1)],
    ("ffn_in", 1, 2): [("ffn2_w_out", 1, None)],
}
SUBLAYER_WEIGHTS = {"ffn1": ["ffn1_w_out", "ffn1_w_in"], "ffn2": ["ffn2_w_out", "ffn2_w_in"], "mix": ["w_out", "w_in"]}


def _pack(arrays):
    flat = jnp.concatenate([a.reshape(-1).astype(F32) for a in arrays])
    pad = (-flat.shape[0]) % PACK_ROWS
    return jnp.pad(flat, (0, pad)).reshape(-1, LANES)


def _unpack(buf, shapes):
    flat = buf.reshape(-1)
    out, off = [], 0
    for s in shapes:
        n = 1
        for d in s:
            n *= d
        out.append(flat[off:off + n].reshape(s))
        off += n
    return out


def kernel(x, positions, norm_ffn1, ffn1_w_in, ffn1_w_out, norm_mix, w_in, conv_dw_w, conv_dw_b, conv_ln_g, conv_ln_b, sgu_ln_g, sgu_ln_b, sgu_w, sgu_b, attn_sinks, w_out, norm_ffn2, ffn2_w_in, ffn2_w_out, final_norm, loss_target, m_norm_ffn1, m_ffn1_w_in, m_ffn1_w_out, m_norm_mix, m_w_in, m_conv_dw_w, m_conv_dw_b, m_conv_ln_g, m_conv_ln_b, m_sgu_ln_g, m_sgu_ln_b, m_sgu_w, m_sgu_b, m_attn_sinks, m_w_out, m_norm_ffn2, m_ffn2_w_in, m_ffn2_w_out, m_final_norm, v_norm_ffn1, v_ffn1_w_in, v_ffn1_w_out, v_norm_mix, v_w_in, v_conv_dw_w, v_conv_dw_b, v_conv_ln_g, v_conv_ln_b, v_sgu_ln_g, v_sgu_ln_b, v_sgu_w, v_sgu_b, v_attn_sinks, v_w_out, v_norm_ffn2, v_ffn2_w_in, v_ffn2_w_out, v_final_norm):
    w = dict(norm_ffn1=norm_ffn1, ffn1_w_in=ffn1_w_in, ffn1_w_out=ffn1_w_out, norm_mix=norm_mix, w_in=w_in,
             conv_dw_w=conv_dw_w, conv_dw_b=conv_dw_b, conv_ln_g=conv_ln_g, conv_ln_b=conv_ln_b, sgu_ln_g=sgu_ln_g,
             sgu_ln_b=sgu_ln_b, sgu_w=sgu_w, sgu_b=sgu_b, attn_sinks=attn_sinks, w_out=w_out, norm_ffn2=norm_ffn2,
             ffn2_w_in=ffn2_w_in, ffn2_w_out=ffn2_w_out, final_norm=final_norm)
    m = dict(norm_ffn1=m_norm_ffn1, ffn1_w_in=m_ffn1_w_in, ffn1_w_out=m_ffn1_w_out, norm_mix=m_norm_mix, w_in=m_w_in,
             conv_dw_w=m_conv_dw_w, conv_dw_b=m_conv_dw_b, conv_ln_g=m_conv_ln_g, conv_ln_b=m_conv_ln_b,
             sgu_ln_g=m_sgu_ln_g, sgu_ln_b=m_sgu_ln_b, sgu_w=m_sgu_w, sgu_b=m_sgu_b, attn_sinks=m_attn_sinks,
             w_out=m_w_out, norm_ffn2=m_norm_ffn2, ffn2_w_in=m_ffn2_w_in, ffn2_w_out=m_ffn2_w_out,
             final_norm=m_final_norm)
    v = dict(norm_ffn1=v_norm_ffn1, ffn1_w_in=v_ffn1_w_in, ffn1_w_out=v_ffn1_w_out, norm_mix=v_norm_mix, w_in=v_w_in,
             conv_dw_w=v_conv_dw_w, conv_dw_b=v_conv_dw_b, conv_ln_g=v_conv_ln_g, conv_ln_b=v_conv_ln_b,
             sgu_ln_g=v_sgu_ln_g, sgu_ln_b=v_sgu_ln_b, sgu_w=v_sgu_w, sgu_b=v_sgu_b, attn_sinks=v_attn_sinks,
             w_out=v_w_out, norm_ffn2=v_norm_ffn2, ffn2_w_in=v_ffn2_w_in, ffn2_w_out=v_ffn2_w_out,
             final_norm=v_final_norm)
    depth = norm_ffn1.shape[0]
    assert depth == 2 and x.shape[0] == 1
    xc = lax.axis_index("x")
    yc = lax.axis_index("y")
    cc = lax.axis_index("c")
    chip = 2 * xc + yc

    chip_arr = chip.reshape(1).astype(jnp.int32)
    place = jnp.stack([cc, chip]).astype(jnp.int32)
    wg = {n: _cast_place(w[n], chip_arr) for n in BIG}
    sched = _Schedule()

    def gather(pieces):
        names = sorted({n for n, _, _ in pieces})
        half = w["ffn1_w_in"].shape[1] // 2

        def make():
            rows = lambda n, part: (0, wg[n].shape[2]) if part is None else (part * half, half)
            return _gather_comm([wg[n] for n in names], [(names.index(n), l, *rows(n, part)) for n, l, part in pieces])

        return make, lambda cm: wg.update(zip(names, cm.aliased_out))

    make, done = gather(FIRST_GATHER)
    first = make()
    _standalone("gather_first", first)
    done(first)
    for site, pieces in FORWARD_PLAN.items():
        sched.add(site, *gather(pieces))
    conv_w_full = _all_gather_weights([], [conv_dw_w])[0].transpose(0, 2, 1, 3).reshape(depth, CONV_WIDTH, CONV_CH)
    conv_w_full = jnp.pad(conv_w_full, ((0, 0), (0, CONV_PAD - CONV_WIDTH), (0, 0)))

    tables = _rope_tables(positions)
    small = []
    for l in range(depth):
        small.append(dict(
            snk=jnp.broadcast_to(attn_sinks[l].reshape(N_KV_HEADS, 1, GQ, 1), (N_KV_HEADS, 1, GQ, BLK)).reshape(
                N_KV_HEADS, 1, GQ * BLK),
            conv_w=conv_w_full[l], conv_b=conv_dw_b[l][None], conv_ln_g=conv_ln_g[l][None],
            conv_ln_b=conv_ln_b[l][None], sgu_ln_g=sgu_ln_g[l][None], sgu_ln_b=sgu_ln_b[l][None], sgu_w=sgu_w[l],
            sgu_bt=sgu_b[l].T))

    xs = x[0]
    saved = []
    for l in range(depth):
        xs, s1 = _ffn_fwd(xs, norm_ffn1[l][None], wg, 1, l, sched)
        xs, s2 = _mix_fwd(xs, norm_mix[l][None], wg, l, small[l], tables, sched)
        xs, s3 = _ffn_fwd(xs, norm_ffn2[l][None], wg, 2, l, sched)
        saved.append((s1, s2, s3))
    dx, dxb, d_final, loss_part = _loss_head(xs, final_norm[None], loss_target[0])

    stacks = {n: None for n in BIG}
    partials, from_chips = {}, {}
    finals = {n: None for n in BIG}

    def to_owner(layer, names):
        def done(cm):
            for n, received in zip(names, cm.fresh_out):
                partials[n, layer] = _chip_partial(stacks[n], received, layer, place)

        return lambda: _to_owner_comm([stacks[n] for n in names], layer), done

    def between_chips(layer, names, then_sum=()):
        def done(cm):
            from_chips.update({(n, layer): r for n, r in zip(names, cm.fresh_out)})
            for n in then_sum:
                finals[n] = _final_sum(partials[n, layer], from_chips[n, layer], layer, place, finals[n])

        return lambda: _chip_comm([partials[n, layer] for n in names]), done

    def from_owner(layer, names):
        return (lambda: _from_owner_comm([finals[n] for n in names], layer),
                lambda cm: finals.update(zip(names, cm.aliased_out)))

    order = [(l, kind) for l in reversed(range(depth)) for kind in ("ffn2", "mix", "ffn1")]
    for (layer, kind), (nxt_layer, nxt_kind) in zip(order[:-1], order[1:]):
        names = SUBLAYER_WEIGHTS[kind]
        if (nxt_layer, nxt_kind) == order[-1]:
            w_out_name, w_in_name = SUBLAYER_WEIGHTS[nxt_kind]
            which = int(nxt_kind[-1])
            sched.add(("ffn_dact", nxt_layer, which), *to_owner(layer, names))
            sched.add(("ffn_dw_in", nxt_layer, which), *between_chips(layer, names, names))
            sched.add(("ffn_dw_out", nxt_layer, which), *from_owner(layer, names))
            sched.add(("ffn_dw_out", nxt_layer, which), *to_owner(nxt_layer, [w_in_name]))
            sched.add(("ffn_dh", nxt_layer, which), *between_chips(nxt_layer, [w_in_name], [w_in_name]))
            sched.add(("ffn_dh", nxt_layer, which), *to_owner(nxt_layer, [w_out_name]))
        elif nxt_kind == "mix":
            after = order[order.index((nxt_layer, nxt_kind)) + 1]
            sched.add(("mix_dout", nxt_layer), *to_owner(layer, names))
            if after == order[-1]:
                which = int(after[1][-1])
                sched.add(("attn_bwd", nxt_layer), *between_chips(layer, names[1:], names[1:]))
                sched.add(("mix_dh", nxt_layer), *from_owner(layer, names[1:]))
                sched.add(("ffn_dw_in", after[0], which), *between_chips(layer, names[:1], names[:1]))
                sched.add(("ffn_dw_out", after[0], which), *from_owner(layer, names[:1]))
            else:
                sched.add(("attn_bwd", nxt_layer), *between_chips(layer, names[:1], names[:1]))
                sched.add(("mix_dh", nxt_layer), *from_owner(layer, names[:1]))
                which = int(after[1][-1])
                sched.add(("ffn_dh", after[0], which), *between_chips(layer, names[1:], names[1:]))
                sched.add(("ffn_dw_in", after[0], which), *from_owner(layer, names[1:]))
        else:
            which = int(nxt_kind[-1])
            sched.add(("ffn_dact", nxt_layer, which), *to_owner(layer, names))
            if kind == "mix":
                sched.add(("ffn_dw_out", nxt_layer, which), *between_chips(layer, names, names))
            else:
                sched.add(("ffn_dw_out", nxt_layer, which), *between_chips(layer, names[:1]))
                sched.add(("ffn_dh", nxt_layer, which), *between_chips(layer, names[1:], names))
            sched.add(("ffn_dw_in", nxt_layer, which), *from_owner(layer, names))

    small_grads = [None] * depth
    for l in reversed(range(depth)):
        s1, s2, s3 = saved[l]
        dx, dxb, dg2 = _ffn_bwd(dx, dxb, s3, norm_ffn2[l][None], wg, 2, l, stacks, sched)
        dx, dxb, gm = _mix_bwd(dx, dxb, s2, norm_mix[l][None], wg, small[l], tables, l, stacks, sched)
        dx, dxb, dg1 = _ffn_bwd(dx, dxb, s1, norm_ffn1[l][None], wg, 1, l, stacks, sched, weights_first=l == 0)
        gm["norm_ffn1"] = dg1[0]
        gm["norm_ffn2"] = dg2[0]
        small_grads[l] = gm
    grad_x = dx[None]
    assert not sched.sites, sched.sites

    per_layer = [n for n in SMALL if n != "final_norm"]
    small_local = [jnp.stack([small_grads[l][n] for l in range(depth)]) for n in per_layer]
    small_local += [d_final[0], loss_part[0, :1]]
    small_shapes = [a.shape for a in small_local]
    summed = _unpack(_all_reduce_small(_pack(small_local)), small_shapes)
    loss = summed[-1][0]
    sg = dict(zip(per_layer + ["final_norm"], summed[:-1]))
    sg["conv_dw_w"] = lax.dynamic_slice_in_dim(sg["conv_dw_w"], chip * LANES, LANES, axis=2)

    delta, new_m, new_v = {}, {}, {}
    shapes = [w[n].shape for n in SMALL]
    packed = [_pack([d[n] for n in SMALL])[None] for d in (w, sg, m, v)]
    last_layer, last_kind = order[-1]
    names = SUBLAYER_WEIGHTS[last_kind]
    make, done = between_chips(last_layer, names[:1], names[:1])
    cm = make()
    outs = _adamw(*packed, comm=cm)
    done(cm)
    for d, buf in zip((delta, new_m, new_v), outs[:3]):
        d.update(zip(SMALL, _unpack(buf[0], shapes)))
    make, done = from_owner(last_layer, names)
    cm = make()
    _standalone("from_owner", cm)
    done(cm)
    big_grads = dict(finals)
    for n in BIG:
        delta[n], new_m[n], new_v[n], big_grads[n] = _adamw(w[n], big_grads[n], m[n], v[n])
    grads = {**big_grads, **sg}
    return (loss, grad_x, *[grads[n] for n in WEIGHTS], *[delta[n] for n in WEIGHTS],
            *[new_m[n] for n in WEIGHTS], *[new_v[n] for n in WEIGHTS])
```
